```python
import jax
import jax.numpy as jnp
from jax import lax
import numpy as np

D_MODEL = 1024
BATCH = 4
SEQ = 4096
DEPTH = 2

GRID_W = 64
CTX_LEN = 256
ROPE_BASE = 10000.0
NORM_EPS = 1e-6
NEG_INF = -1e30
BLOCK = 128
N_MOD = 6

MLA_HEADS = 8
MLA_NOPE = 64
MLA_ROPE = 32
MLA_V = 64
MLA_Q_LORA = 256
MLA_KV_LORA = 256
GQA_HEADS = 8
GQA_KV_HEADS = 2
GQA_GROUP = GQA_HEADS // GQA_KV_HEADS
GQA_DIM = 64
WINDOW = 128
RET_HEADS = 4
RET_QK = 64
RET_V = 128
RET_CHUNK = 128

MLA_OUT = MLA_HEADS * MLA_V
GQA_OUT = GQA_HEADS * GQA_DIM
RET_OUT = RET_HEADS * RET_V
N_BRANCH = 3
IN_SPLITS = (MLA_Q_LORA, MLA_KV_LORA, MLA_ROPE,
             GQA_HEADS * GQA_DIM, GQA_KV_HEADS * GQA_DIM, GQA_KV_HEADS * GQA_DIM,
             RET_HEADS * RET_QK, RET_HEADS * RET_QK, RET_OUT, RET_OUT,
             N_BRANCH * D_MODEL)
IN_COLS = sum(IN_SPLITS)

N_EXPERTS = 64
N_EXPERT_GROUPS = 8
EXPERTS_PER_GROUP = N_EXPERTS // N_EXPERT_GROUPS
TOPK_GROUPS = 4
TOP_K = 8
EXPERT_HIDDEN = 256
SHARED_HIDDEN = 256
ROUTED_SCALE = 2.5

kernel_name = 'hybrid_mla_swa_retention_moe_dit'


def rms_norm(x, g):
    xf = x.astype(jnp.float32)
    y = xf * lax.rsqrt(jnp.mean(xf * xf, axis=-1, keepdims=True) + NORM_EPS)
    return (y * g.astype(jnp.float32)).astype(x.dtype)


def adaln_params(cond, ada_w, ada_b):
    return jnp.split(jax.nn.silu(cond) @ ada_w + ada_b, N_MOD, axis=-1)


def axial_rope(n_tok, rot_dim, dtype):
    rows = n_tok // GRID_W
    row_id = jnp.repeat(jnp.arange(rows, dtype=jnp.float32), GRID_W)
    col_id = jnp.tile(jnp.arange(GRID_W, dtype=jnp.float32), rows)
    axis_dim = rot_dim // 2
    inv_freq = ROPE_BASE ** (-jnp.arange(0, axis_dim, 2, dtype=jnp.float32) / axis_dim)
    ang_r = row_id[:, None] * inv_freq[None, :]
    ang_c = col_id[:, None] * inv_freq[None, :]
    return tuple(t.astype(dtype) for t in (jnp.cos(ang_r), jnp.sin(ang_r), jnp.cos(ang_c), jnp.sin(ang_c)))


def _rotate_half(x, cos, sin):
    x1, x2 = jnp.split(x, 2, axis=-1)
    return jnp.concatenate([x1 * cos - x2 * sin, x1 * sin + x2 * cos], axis=-1)


def apply_axial_rope(x, rope):
    cos_r, sin_r, cos_c, sin_c = rope
    half = x.shape[-1] // 2
    return jnp.concatenate([_rotate_half(x[..., :half], cos_r, sin_r),
                            _rotate_half(x[..., half:], cos_c, sin_c)], axis=-1)


def split_heads(t, n_heads):
    b, n, _ = t.shape
    return t.reshape(b, n, n_heads, -1).transpose(0, 2, 1, 3)


def merge_heads(t):
    b, h, n, d = t.shape
    return t.transpose(0, 2, 1, 3).reshape(b, n, h * d)


def softmax_with_sink(s, sink):
    sink_col = jnp.broadcast_to(sink.astype(jnp.float32), s.shape[:-1] + (1,))
    return jax.nn.softmax(jnp.concatenate([s, sink_col], axis=-1), axis=-1)[..., :-1]


def branch_inputs(u, lp, rope_qk, rope_pe):
    b, n, _ = u.shape
    offs = np.cumsum(np.array(IN_SPLITS))[:-1].tolist()
    (cq, ckv, kpe, gq, gk, gv, rq, rk, rv, rg, gate_logits) = jnp.split(u @ lp['w_in'], offs, axis=-1)
    q = split_heads(rms_norm(cq, lp['mla_q_norm']) @ lp['mla_w_uq'], MLA_HEADS)
    kv = split_heads(rms_norm(ckv, lp['mla_kv_norm']) @ lp['mla_w_ukv'], MLA_HEADS)
    q_nope, q_pe = q[..., :MLA_NOPE], q[..., MLA_NOPE:]
    k_nope, mla_v = kv[..., :MLA_NOPE], kv[..., MLA_NOPE:]
    k_pe = kpe[:, None]
    gq, gk, gv = split_heads(gq, GQA_HEADS), split_heads(gk, GQA_KV_HEADS), split_heads(gv, GQA_KV_HEADS)
    rq, rk, rv = split_heads(rq, RET_HEADS), split_heads(rk, RET_HEADS), split_heads(rv, RET_HEADS)
    if rope_qk is not None:
        q_pe, k_pe = apply_axial_rope(q_pe, rope_pe), apply_axial_rope(k_pe, rope_pe)
        gq, gk = apply_axial_rope(gq, rope_qk), apply_axial_rope(gk, rope_qk)
        rq, rk = apply_axial_rope(rq, rope_qk), apply_axial_rope(rk, rope_qk)
    mla_q = jnp.concatenate([q_nope, q_pe], axis=-1) * (MLA_NOPE + MLA_ROPE) ** -0.5
    mla_k = jnp.concatenate([k_nope, jnp.broadcast_to(k_pe, (b, MLA_HEADS, n, MLA_ROPE))], axis=-1)
    return {'mla_q': mla_q, 'mla_k': mla_k, 'mla_v': mla_v,
            'gqa_q': (gq * GQA_DIM ** -0.5).reshape(b, GQA_KV_HEADS, GQA_GROUP, n, GQA_DIM),
            'gqa_k': gk, 'gqa_v': gv,
            'ret_q': rq, 'ret_k': rk * RET_QK ** -0.5, 'ret_v': rv, 'ret_g': rg,
            'gates': gate_logits}


def full_attention(q, k, v):
    p = jax.nn.softmax(jnp.einsum('bhqd,bhkd->bhqk', q, k).astype(jnp.float32), axis=-1).astype(v.dtype)
    return jnp.einsum('bhqk,bhkd->bhqd', p, v)


def mla_latent_attention(q, k, v, k_ctx, v_ctx):
    b, h, n_lat, dq = q.shape
    nb = n_lat // BLOCK
    q_blocks = q.reshape(b, h, nb, BLOCK, dq).transpose(2, 0, 1, 3, 4)

    def one_block(qb):
        s_lat = jnp.einsum('bhqd,bhkd->bhqk', qb, k)
        s_ctx = jnp.einsum('bhqd,bhkd->bhqk', qb, k_ctx)
        p = jax.nn.softmax(jnp.concatenate([s_lat, s_ctx], axis=-1).astype(jnp.float32), axis=-1).astype(v.dtype)
        return (jnp.einsum('bhqk,bhkd->bhqd', p[..., :n_lat], v)
                + jnp.einsum('bhqk,bhkd->bhqd', p[..., n_lat:], v_ctx))

    out = lax.map(one_block, q_blocks)
    return out.transpose(1, 2, 0, 3, 4).reshape(b, h, n_lat, v.shape[-1])


def window_attention(q, k, v, k_ctx, v_ctx, sink):
    b, hk, g, n, d = q.shape
    nb = n // BLOCK
    pad = ((0, 0), (0, 0), (BLOCK, BLOCK), (0, 0))
    kp = jnp.pad(k, pad).reshape(b, hk, nb + 2, BLOCK, d)
    vp = jnp.pad(v, pad).reshape(b, hk, nb + 2, BLOCK, d)
    kw = jnp.concatenate([kp[:, :, :-2], kp[:, :, 1:-1], kp[:, :, 2:]], axis=3)
    vw = jnp.concatenate([vp[:, :, :-2], vp[:, :, 1:-1], vp[:, :, 2:]], axis=3)
    qb = q.reshape(b, hk, g, nb, BLOCK, d)
    s_loc = jnp.einsum('bkgnqd,bknjd->bkgnqj', qb, kw).astype(jnp.float32)
    q_pos = jnp.arange(nb)[:, None] * BLOCK + jnp.arange(BLOCK)[None, :]
    k_pos = (jnp.arange(nb)[:, None] - 1) * BLOCK + jnp.arange(3 * BLOCK)[None, :]
    valid = ((jnp.abs(q_pos[:, :, None] - k_pos[:, None, :]) <= WINDOW)
             & (k_pos[:, None, :] >= 0) & (k_pos[:, None, :] < n))
    s_loc = jnp.where(valid, s_loc, NEG_INF)
    s_ctx = jnp.einsum('bkgnqd,bkcd->bkgnqc', qb, k_ctx).astype(jnp.float32)
    p = softmax_with_sink(jnp.concatenate([s_loc, s_ctx], axis=-1),
                          sink.reshape(1, hk, g, 1, 1, 1)).astype(v.dtype)
    nl = 3 * BLOCK
    out = (jnp.einsum('bkgnqj,bknjd->bkgnqd', p[..., :nl], vw)
           + jnp.einsum('bkgnqc,bkcd->bkgnqd', p[..., nl:], v_ctx))
    return out.reshape(b, hk * g, n, d)


def gqa_context_attention(q, k, v, sink):
    b, hk, g, c, d = q.shape
    s = jnp.einsum('bkgqd,bkcd->bkgqc', q, k).astype(jnp.float32)
    p = softmax_with_sink(s, sink.reshape(1, hk, g, 1, 1)).astype(v.dtype)
    return jnp.einsum('bkgqc,bkcd->bkgqd', p, v).reshape(b, hk * g, c, d)


def retention_chunkwise(q, k, v, log_gamma, state0):
    b, h, n, dk = q.shape
    dv = v.shape[-1]
    nc = n // RET_CHUNK
    idx = jnp.arange(RET_CHUNK, dtype=jnp.float32)
    diff = idx[:, None] - idx[None, :]
    inner = jnp.where(diff >= 0, jnp.exp(log_gamma[:, None, None] * jnp.maximum(diff, 0.0)), 0.0)
    q_decay = jnp.exp(log_gamma[:, None] * (idx + 1.0))[None, :, :, None]
    k_decay = jnp.exp(log_gamma[:, None] * (RET_CHUNK - 1.0 - idx))[None, :, :, None]
    chunk_decay = jnp.exp(log_gamma * RET_CHUNK)[None, :, None, None]

    def to_chunks(t):
        return t.reshape(b, h, nc, RET_CHUNK, t.shape[-1]).transpose(2, 0, 1, 3, 4)

    def step(state, qkv):
        qi, ki, vi = qkv
        attn = jnp.einsum('bhid,bhjd->bhij', qi, ki) * inner
        o = (jnp.einsum('bhij,bhjv->bhiv', attn, vi)
             + jnp.einsum('bhid,bhdv->bhiv', qi * q_decay, state))
        state = state * chunk_decay + jnp.einsum('bhjd,bhjv->bhdv', ki * k_decay, vi)
        return state, o

    state, o = lax.scan(step, state0, (to_chunks(q), to_chunks(k), to_chunks(v)))
    return o.transpose(1, 2, 0, 3, 4).reshape(b, h, n, dv), state


def retention_final_state(k, v, log_gamma):
    n = k.shape[2]
    w = jnp.exp(log_gamma[:, None] * (n - 1.0 - jnp.arange(n, dtype=jnp.float32)))
    return jnp.einsum('bhnd,bhnv->bhdv', k * w[None, :, :, None], v)


def retention_readout(o, g):
    mu = jnp.mean(o, axis=-1, keepdims=True)
    var = jnp.mean(jnp.square(o - mu), axis=-1, keepdims=True)
    o = (o - mu) * lax.rsqrt(var + NORM_EPS)
    return jax.nn.silu(g) * merge_heads(o).astype(g.dtype)


def merge_branches(a, w, r, gate_logits, lp):
    g_a, g_w, g_r = jnp.split(jax.nn.sigmoid(gate_logits), N_BRANCH, axis=-1)
    y = g_a * (a @ lp['w_br_mla']) + g_w * (w @ lp['w_br_gqa']) + g_r * (r @ lp['w_br_ret'])
    return y @ lp['w_out']


def hybrid_mixer(u_c, u_x, lp, rope_qk, rope_pe, need_ctx_out):
    bc = branch_inputs(u_c, lp, None, None)
    bx = branch_inputs(u_x, lp, rope_qk, rope_pe)
    a_x = merge_heads(mla_latent_attention(bx['mla_q'], bx['mla_k'], bx['mla_v'], bc['mla_k'], bc['mla_v']))
    w_x = merge_heads(window_attention(bx['gqa_q'], bx['gqa_k'], bx['gqa_v'],
                                       bc['gqa_k'], bc['gqa_v'], lp['gqa_sink']))
    lg_f = jax.nn.log_sigmoid(lp['ret_decay_fwd'].astype(jnp.float32))
    lg_b = jax.nn.log_sigmoid(lp['ret_decay_bwd'].astype(jnp.float32))
    qc, kc, vc = (bc[nm].astype(jnp.float32) for nm in ('ret_q', 'ret_k', 'ret_v'))
    qx, kx, vx = (bx[nm].astype(jnp.float32) for nm in ('ret_q', 'ret_k', 'ret_v'))

    def flip(t):
        return jnp.flip(t, axis=2)

    if need_ctx_out:
        zero = jnp.zeros(kc.shape[:2] + (RET_QK, RET_V), jnp.float32)
        oc_f, st_f = retention_chunkwise(qc, kc, vc, lg_f, zero)
        oc_b, st_b = retention_chunkwise(flip(qc), flip(kc), flip(vc), lg_b, zero)
        r_c = retention_readout(oc_f + flip(oc_b), bc['ret_g'])
    else:
        st_f = retention_final_state(kc, vc, lg_f)
        st_b = retention_final_state(flip(kc), flip(vc), lg_b)
    ox_f, _ = retention_chunkwise(qx, kx, vx, lg_f, st_f)
    ox_b, _ = retention_chunkwise(flip(qx), flip(kx), flip(vx), lg_b, st_b)
    r_x = retention_readout(ox_f + flip(ox_b), bx['ret_g'])
    y_x = merge_branches(a_x, w_x, r_x, bx['gates'], lp)
    if not need_ctx_out:
        return None, y_x
    a_c = merge_heads(full_attention(bc['mla_q'], bc['mla_k'], bc['mla_v']))
    w_c = merge_heads(gqa_context_attention(bc['gqa_q'], bc['gqa_k'], bc['gqa_v'], lp['gqa_sink']))
    y_c = merge_branches(a_c, w_c, r_c, bc['gates'], lp)
    return y_c, y_x


def moe_ffn(h, lp):
    t = h.shape[0]
    scores = jax.nn.sigmoid((h @ lp['router_w']).astype(jnp.float32))
    sel = scores + lp['router_bias'].astype(jnp.float32)
    grp_score = lax.top_k(sel.reshape(t, N_EXPERT_GROUPS, EXPERTS_PER_GROUP), 2)[0].sum(-1)
    _, top_groups = lax.top_k(grp_score, TOPK_GROUPS)
    group_ok = jnp.any(top_groups[:, :, None] == jnp.arange(N_EXPERT_GROUPS)[None, None, :], axis=1)
    sel = jnp.where(jnp.repeat(group_ok, EXPERTS_PER_GROUP, axis=1), sel, NEG_INF)
    _, top_e = lax.top_k(sel, TOP_K)
    w = jnp.take_along_axis(scores, top_e, axis=-1)
    w = ROUTED_SCALE * w / jnp.sum(w, axis=-1, keepdims=True)
    gate = jnp.sum(jax.nn.one_hot(top_e, N_EXPERTS, dtype=jnp.float32) * w[..., None], axis=1).astype(h.dtype)
    out = (jax.nn.silu(h @ lp['shared_w_gate']) * (h @ lp['shared_w_up'])) @ lp['shared_w_down']
    for gi in range(N_EXPERT_GROUPS):
        sl = slice(gi * EXPERTS_PER_GROUP, (gi + 1) * EXPERTS_PER_GROUP)
        a = jnp.einsum('td,edh->teh', h, lp['exp_w_gate'][sl])
        u = jnp.einsum('td,edh->teh', h, lp['exp_w_up'][sl])
        out = out + jnp.einsum('teh,ehd->td', jax.nn.silu(a) * u * gate[:, sl, None], lp['exp_w_down'][sl])
    return out


def setup_inputs(seed: int = 0) -> dict:
    key = jax.random.key(seed)
    ks = iter(jax.random.split(key, 32))
    f32 = jnp.float32
    L, D = DEPTH, D_MODEL

    def nrm(shape, scale):
        return jax.random.normal(next(ks), shape, f32) * scale

    def gain(shape):
        return 1.0 + 0.05 * jax.random.normal(next(ks), shape, f32)

    base_decay = jnp.log(2.0 ** (5.0 + jnp.arange(RET_HEADS, dtype=f32)) - 1.0)
    return {
        'x': nrm((BATCH, SEQ, D), 1.0),
        'c': nrm((BATCH, D), 1.0),
        'ctx': nrm((BATCH, CTX_LEN, D), 1.0),
        'c_ctx': nrm((D,), 1.0),
        'ada_w': nrm((L, D, N_MOD * D), 0.5 * D ** -0.5),
        'ada_b': nrm((L, N_MOD * D), 0.01),
        'norm_mix_pre': gain((L, D)),
        'norm_mix_post': gain((L, D)),
        'norm_ffn_pre': gain((L, D)),
        'norm_ffn_post': gain((L, D)),
        'w_in': nrm((L, D, IN_COLS), D ** -0.5),
        'mla_q_norm': gain((L, MLA_Q_LORA)),
        'mla_w_uq': nrm((L, MLA_Q_LORA, MLA_HEADS * (MLA_NOPE + MLA_ROPE)), MLA_Q_LORA ** -0.5),
        'mla_kv_norm': gain((L, MLA_KV_LORA)),
        'mla_w_ukv': nrm((L, MLA_KV_LORA, MLA_HEADS * (MLA_NOPE + MLA_V)), MLA_KV_LORA ** -0.5),
        'gqa_sink': nrm((L, GQA_HEADS), 0.5),
        'ret_decay_fwd': base_decay[None, :] + nrm((L, RET_HEADS), 0.1),
        'ret_decay_bwd': base_decay[None, :] + nrm((L, RET_HEADS), 0.1),
        'w_br_mla': nrm((L, MLA_OUT, D), MLA_OUT ** -0.5),
        'w_br_gqa': nrm((L, GQA_OUT, D), GQA_OUT ** -0.5),
        'w_br_ret': nrm((L, RET_OUT, D), RET_OUT ** -0.5),
        'w_out': nrm((L, D, D), D ** -0.5),
        'router_w': nrm((L, D, N_EXPERTS), D ** -0.5),
        'router_bias': nrm((L, N_EXPERTS), 0.01),
        'exp_w_gate': nrm((L, N_EXPERTS, D, EXPERT_HIDDEN), D ** -0.5),
        'exp_w_up': nrm((L, N_EXPERTS, D, EXPERT_HIDDEN), D ** -0.5),
        'exp_w_down': nrm((L, N_EXPERTS, EXPERT_HIDDEN, D), EXPERT_HIDDEN ** -0.5),
        'shared_w_gate': nrm((L, D, SHARED_HIDDEN), D ** -0.5),
        'shared_w_up': nrm((L, D, SHARED_HIDDEN), D ** -0.5),
        'shared_w_down': nrm((L, SHARED_HIDDEN, D), SHARED_HIDDEN ** -0.5),
    }


def reference(x, c, ctx, c_ctx, ada_w, ada_b, norm_mix_pre, norm_mix_post, norm_ffn_pre, norm_ffn_post,
              w_in, mla_q_norm, mla_w_uq, mla_kv_norm, mla_w_ukv, gqa_sink, ret_decay_fwd, ret_decay_bwd,
              w_br_mla, w_br_gqa, w_br_ret, w_out, router_w, router_bias, exp_w_gate, exp_w_up, exp_w_down,
              shared_w_gate, shared_w_up, shared_w_down):
    n_lat = x.shape[1]
    rope_qk = axial_rope(n_lat, GQA_DIM, x.dtype)
    rope_pe = axial_rope(n_lat, MLA_ROPE, x.dtype)
    h_x, h_c = x, ctx
    for l in range(DEPTH):
        lp = {'w_in': w_in[l], 'mla_q_norm': mla_q_norm[l], 'mla_w_uq': mla_w_uq[l],
              'mla_kv_norm': mla_kv_norm[l], 'mla_w_ukv': mla_w_ukv[l], 'gqa_sink': gqa_sink[l],
              'ret_decay_fwd': ret_decay_fwd[l], 'ret_decay_bwd': ret_decay_bwd[l],
              'w_br_mla': w_br_mla[l], 'w_br_gqa': w_br_gqa[l], 'w_br_ret': w_br_ret[l], 'w_out': w_out[l],
              'router_w': router_w[l], 'router_bias': router_bias[l], 'exp_w_gate': exp_w_gate[l],
              'exp_w_up': exp_w_up[l], 'exp_w_down': exp_w_down[l], 'shared_w_gate': shared_w_gate[l],
              'shared_w_up': shared_w_up[l], 'shared_w_down': shared_w_down[l]}
        last = l == DEPTH - 1
        sh1, sc1, g1, sh2, sc2, g2 = [m[:, None, :] for m in adaln_params(c, ada_w[l], ada_b[l])]
        csh1, csc1, cg1, csh2, csc2, cg2 = adaln_params(c_ctx, ada_w[l], ada_b[l])
        u_x = rms_norm(h_x, norm_mix_pre[l]) * (1 + sc1) + sh1
        u_c = rms_norm(h_c, norm_mix_pre[l]) * (1 + csc1) + csh1
        y_c, y_x = hybrid_mixer(u_c, u_x, lp, rope_qk, rope_pe, not last)
        h_x = h_x + g1 * rms_norm(y_x, norm_mix_post[l])
        v_x = rms_norm(h_x, norm_ffn_pre[l]) * (1 + sc2) + sh2
        b, n, d = v_x.shape
        if last:
            f_x = moe_ffn(v_x.reshape(b * n, d), lp).reshape(b, n, d)
        else:
            h_c = h_c + cg1 * rms_norm(y_c, norm_mix_post[l])
            v_c = rms_norm(h_c, norm_ffn_pre[l]) * (1 + csc2) + csh2
            f = moe_ffn(jnp.concatenate([v_x.reshape(b * n, d), v_c.reshape(-1, d)], axis=0), lp)
            f_x = f[:b * n].reshape(b, n, d)
            h_c = h_c + cg2 * rms_norm(f[b * n:].reshape(v_c.shape), norm_ffn_post[l])
        h_x = h_x + g2 * rms_norm(f_x, norm_ffn_post[l])
    return h_x
```

```python
import functools

import jax
import jax.numpy as jnp
from jax import lax
from jax.experimental import pallas as pl
from jax.experimental.pallas import tpu as pltpu

F32 = jnp.float32
BF16 = jnp.bfloat16

GRID_W = 64
ROPE_BASE = 10000.0
NORM_EPS = 1e-6
NEG_INF = -1e30
N_MOD = 6
MLA_HEADS, MLA_NOPE, MLA_ROPE, MLA_V = 8, 64, 32, 64
MLA_Q_LORA, MLA_KV_LORA = 256, 256
GQA_HEADS, GQA_KV_HEADS, GQA_DIM, WINDOW = 8, 2, 64, 128
RET_HEADS, RET_QK, RET_V, RET_CHUNK = 4, 64, 128, 128
N_EXPERTS, N_EXPERT_GROUPS, TOPK_GROUPS, TOP_K = 64, 8, 4, 8
EXPERTS_PER_GROUP = N_EXPERTS // N_EXPERT_GROUPS
ROUTED_SCALE = 2.5

LANES = 128
TM = 256
MOD_ROWS = 8
V7X_VMEM_LIMIT = 56 * 1024 * 1024

C_CQ, C_CKV, C_KPE = 0, 256, 512
C_G = 640
C_R = 1664
C_GATE = 3200
W_COLS = 6272


def _cparams(sem):
    return pltpu.CompilerParams(dimension_semantics=sem, vmem_limit_bytes=V7X_VMEM_LIMIT)


def _rms(x, g):
    return x * lax.rsqrt(jnp.mean(x * x, axis=-1, keepdims=True) + NORM_EPS) * g


def _dot(a, b):
    return jnp.dot(a, b, preferred_element_type=F32)


def _dot_nt(a, b):
    return lax.dot_general(a, b, (((1,), (1,)), ((), ())), preferred_element_type=F32)


def _dot_tn(a, b):
    return lax.dot_general(a, b, (((0,), (0,)), ((), ())), preferred_element_type=F32)


def _rope(x, cos, sin, half):
    n = x.shape[-1]
    reps = n // LANES
    if reps > 1:
        cos = jnp.concatenate([cos] * reps, axis=1)
        sin = jnp.concatenate([sin] * reps, axis=1)
    lane = lax.broadcasted_iota(jnp.int32, x.shape, 1)
    up = pltpu.roll(x, half, 1)
    dn = pltpu.roll(x, n - half, 1)
    partner = jnp.where((lane & (2 * half - 1)) < half, dn, up)
    return x * cos + partner * sin


def _lane_lo(shape):
    return (lax.broadcasted_iota(jnp.int32, shape, 1) & (LANES - 1)) < (LANES // 2)


def _ada_kernel(c_ref, w_ref, b_ref, o_ref):
    c = c_ref[...]
    s = c * jax.nn.sigmoid(c)
    o_ref[...] = _dot(s.astype(BF16), w_ref[...].astype(BF16)) + b_ref[...]


def _adaln(cond, ada_w, ada_b):
    n_layers, d, n = ada_w.shape
    tn = 1024
    return pl.pallas_call(
        _ada_kernel,
        grid=(n_layers, n // tn),
        in_specs=[pl.BlockSpec((MOD_ROWS, d), lambda l, j: (0, 0)),
                  pl.BlockSpec((None, d, tn), lambda l, j: (l, 0, j)),
                  pl.BlockSpec((None, 1, tn), lambda l, j: (l, 0, j))],
        out_specs=pl.BlockSpec((None, MOD_ROWS, tn), lambda l, j: (l, 0, j)),
        out_shape=jax.ShapeDtypeStruct((n_layers, MOD_ROWS, n), F32),
        compiler_params=_cparams(("arbitrary", "arbitrary")),
        name="adaln",
    )(cond, ada_w, ada_b.reshape(n_layers, 1, n))


def _inproj_kernel(h_ref, mod_ref, gpre_ref, w_ref, rope_ref, qn_ref, kvn_ref, wuq_ref, wuk_ref, wuv_ref,
                   mq_ref, mk_ref, mv_ref, gqa_ref, ret_ref, gate_ref, *, tiles_per_batch, n_batch, d):
    i = pl.program_id(0)
    bi = jnp.minimum(i // tiles_per_batch, n_batch)
    sh = mod_ref[pl.ds(bi, 1), 0:d]
    sc = mod_ref[pl.ds(bi, 1), d:2 * d]
    u = (_rms(h_ref[...], gpre_ref[...]) * (1.0 + sc) + sh).astype(BF16)

    cos64 = rope_ref[:, 0:LANES]
    sin64 = rope_ref[:, LANES:2 * LANES]
    cospe = rope_ref[:, 2 * LANES:3 * LANES]
    sinpe = rope_ref[:, 3 * LANES:4 * LANES]

    c = _dot(u, w_ref[:, C_CQ:C_G])
    kpe = _rope(c[:, C_KPE:C_G], cospe, sinpe, MLA_ROPE // 4)
    qn = _rms(c[:, C_CQ:C_CKV], qn_ref[...]).astype(BF16)
    q = _rope(_dot(qn, wuq_ref[...]), cospe, sinpe, MLA_ROPE // 4)
    mq_ref[...] = (q * (MLA_NOPE + MLA_ROPE) ** -0.5).astype(mq_ref.dtype)
    kvn = _rms(c[:, C_CKV:C_KPE], kvn_ref[...]).astype(BF16)
    k = _dot(kvn, wuk_ref[...]) + jnp.concatenate([kpe] * MLA_HEADS, axis=1)
    mk_ref[...] = k.astype(mk_ref.dtype)
    mv_ref[...] = _dot(kvn, wuv_ref[...]).astype(mv_ref.dtype)

    g = _dot(u, w_ref[:, C_G:C_R])
    n_qk = GQA_HEADS * GQA_DIM + 2 * GQA_KV_HEADS * GQA_DIM
    gqa_ref[:, 0:n_qk] = _rope(g[:, 0:n_qk], cos64, sin64, GQA_DIM // 4).astype(gqa_ref.dtype)
    gqa_ref[:, n_qk:] = g[:, n_qk:].astype(gqa_ref.dtype)

    r = _dot(u, w_ref[:, C_R:C_GATE])
    n_qk = 2 * RET_HEADS * RET_QK
    ret_ref[:, 0:n_qk] = _rope(r[:, 0:n_qk], cos64, sin64, RET_QK // 4)
    ret_ref[:, n_qk:] = r[:, n_qk:]

    gate_ref[...] = _dot(u, w_ref[:, C_GATE:W_COLS]).astype(gate_ref.dtype)


def _inproj(h, mods, gpre, w_all, rope, qn, kvn, wuq, wuk, wuv, *, n_batch, seq):
    t, d = h.shape
    tiles_per_batch = seq // TM
    n_lat_tiles = n_batch * tiles_per_batch
    const = lambda i: (0, 0)
    rows = lambda i: (i, 0)
    rope_idx = lambda i: (jnp.where(i < n_lat_tiles, i % tiles_per_batch, tiles_per_batch), 0)
    hq = MLA_HEADS * LANES
    outs = [jax.ShapeDtypeStruct((t, hq), BF16), jax.ShapeDtypeStruct((t, hq), BF16),
            jax.ShapeDtypeStruct((t, MLA_HEADS * MLA_V), BF16),
            jax.ShapeDtypeStruct((t, C_R - C_G), BF16),
            jax.ShapeDtypeStruct((t, C_GATE - C_R), F32),
            jax.ShapeDtypeStruct((t, W_COLS - C_GATE), BF16)]
    return pl.pallas_call(
        functools.partial(_inproj_kernel, tiles_per_batch=tiles_per_batch, n_batch=n_batch, d=d),
        grid=(t // TM,),
        in_specs=[pl.BlockSpec((TM, d), rows),
                  pl.BlockSpec(mods.shape, const),
                  pl.BlockSpec((1, d), const),
                  pl.BlockSpec(w_all.shape, const),
                  pl.BlockSpec((TM, 4 * LANES), rope_idx),
                  pl.BlockSpec(qn.shape, const), pl.BlockSpec(kvn.shape, const),
                  pl.BlockSpec(wuq.shape, const), pl.BlockSpec(wuk.shape, const), pl.BlockSpec(wuv.shape, const)],
        out_specs=[pl.BlockSpec((TM, o.shape[1]), rows) for o in outs],
        out_shape=outs,
        compiler_params=_cparams(("arbitrary",)),
        name="inproj",
    )(h, mods, gpre, w_all, rope, qn, kvn, wuq, wuk, wuv)


def _mla_kernel(q_ref, kl_ref, kc_ref, vl_ref, vc_ref, o_ref, *, n_lat_tiles, has_ctx_tile):
    def body(with_lat):
        outs = []
        for h in range(2):
            sl = slice(h * LANES, (h + 1) * LANES)
            qh = q_ref[:, sl]
            s_c = _dot_nt(qh, kc_ref[:, sl])
            m = jnp.max(s_c, axis=-1, keepdims=True)
            if with_lat:
                s_l = _dot_nt(qh, kl_ref[:, sl])
                m = jnp.maximum(m, jnp.max(s_l, axis=-1, keepdims=True))
            p_c = jnp.exp(s_c - m)
            den = jnp.sum(p_c, axis=-1, keepdims=True)
            o = _dot(p_c.astype(BF16), vc_ref[...])
            if with_lat:
                p_l = jnp.exp(s_l - m)
                den = den + jnp.sum(p_l, axis=-1, keepdims=True)
                o = o + _dot(p_l.astype(BF16), vl_ref[...])
            outs.append(o / den)
        o_ref[...] = jnp.where(_lane_lo(outs[0].shape), outs[0], outs[1]).astype(o_ref.dtype)

    if has_ctx_tile:
        i = pl.program_id(2)
        pl.when(i < n_lat_tiles)(lambda: body(True))
        pl.when(i == n_lat_tiles)(lambda: body(False))
    else:
        body(True)


def _mla_attention(mq, mk, mv, *, n_batch, seq, ctx, with_ctx_queries):
    t = mq.shape[0]
    nq = seq // TM
    n_lat_blocks = n_batch * nq
    ctx_blk0 = n_batch * seq // ctx
    n_pairs = MLA_HEADS // 2
    assert ctx == TM
    q_idx = lambda b, p, i: (jnp.where(i < nq, b * nq + i, n_lat_blocks + b), p)
    lat_idx = lambda b, p, i: (b, p)
    ctx_idx = lambda b, p, i: (ctx_blk0 + b, p)
    n_rows = t if with_ctx_queries else n_batch * seq
    return pl.pallas_call(
        functools.partial(_mla_kernel, n_lat_tiles=nq, has_ctx_tile=with_ctx_queries),
        grid=(n_batch, n_pairs, nq + (1 if with_ctx_queries else 0)),
        in_specs=[pl.BlockSpec((TM, 2 * LANES), q_idx),
                  pl.BlockSpec((seq, 2 * LANES), lat_idx),
                  pl.BlockSpec((ctx, 2 * LANES), ctx_idx),
                  pl.BlockSpec((seq, LANES), lat_idx),
                  pl.BlockSpec((ctx, LANES), ctx_idx)],
        out_specs=pl.BlockSpec((TM, LANES), q_idx),
        out_shape=jax.ShapeDtypeStruct((n_rows, MLA_HEADS * MLA_V), BF16),
        compiler_params=_cparams(("arbitrary", "arbitrary", "arbitrary")),
        name="mla_attn",
    )(mq, mk, mk, mv, mv)


def _win_kernel(q_ref, kp_ref, kcur_ref, kn_ref, vp_ref, vcur_ref, vn_ref, kc_ref, vc_ref, sink_ref, o_ref,
                *, n_lat_tiles, seq, has_ctx_tile):
    i = pl.program_id(1)
    tq = q_ref.shape[0]
    group = GQA_HEADS // GQA_KV_HEADS

    def body(with_lat):
        if with_lat:
            n_win = tq + 2 * WINDOW
            q_pos = i * tq + lax.broadcasted_iota(jnp.int32, (tq, n_win), 0)
            k_pos = i * tq - WINDOW + lax.broadcasted_iota(jnp.int32, (tq, n_win), 1)
            valid = (jnp.abs(q_pos - k_pos) <= WINDOW) & (k_pos >= 0) & (k_pos < seq)
        for kv in range(GQA_KV_HEADS):
            sl = slice(kv * LANES, (kv + 1) * LANES)
            k_ctx, v_ctx = kc_ref[:, sl], vc_ref[:, sl]
            if with_lat:
                k_win = jnp.concatenate([kp_ref[:, sl], kcur_ref[:, sl], kn_ref[:, sl]], axis=0)
                v_win = jnp.concatenate([vp_ref[:, sl], vcur_ref[:, sl], vn_ref[:, sl]], axis=0)
            for pr in range(group // 2):
                col = slice((kv * (group // 2) + pr) * LANES, (kv * (group // 2) + pr + 1) * LANES)
                qp = q_ref[:, col]
                lo = _lane_lo(qp.shape)
                halves = []
                for half in range(2):
                    hd = kv * group + pr * 2 + half
                    qm = jnp.where(lo if half == 0 else jnp.logical_not(lo), qp, jnp.zeros_like(qp))
                    sink = sink_ref[hd:hd + 1, 0:1]
                    s_c = _dot_nt(qm, k_ctx)
                    m = jnp.maximum(jnp.max(s_c, axis=-1, keepdims=True), sink)
                    if with_lat:
                        s_l = jnp.where(valid, _dot_nt(qm, k_win), NEG_INF)
                        m = jnp.maximum(m, jnp.max(s_l, axis=-1, keepdims=True))
                    p_c = jnp.exp(s_c - m)
                    den = jnp.sum(p_c, axis=-1, keepdims=True) + jnp.exp(sink - m)
                    o = _dot(p_c.astype(BF16), v_ctx)
                    if with_lat:
                        p_l = jnp.exp(s_l - m)
                        den = den + jnp.sum(p_l, axis=-1, keepdims=True)
                        o = o + _dot(p_l.astype(BF16), v_win)
                    halves.append(o / den)
                o_ref[:, col] = jnp.where(lo, halves[0], halves[1]).astype(o_ref.dtype)

    if has_ctx_tile:
        pl.when(i < n_lat_tiles)(lambda: body(True))
        pl.when(i == n_lat_tiles)(lambda: body(False))
    else:
        body(True)


def _window_attention(gqa, sink_tab, *, n_batch, seq, ctx, with_ctx_queries):
    t = gqa.shape[0]
    nq = seq // TM
    n_lat_blocks = n_batch * nq
    per_tile = TM // WINDOW
    n_win_blocks = seq // WINDOW
    assert ctx == TM
    ctx_blk0 = n_batch * seq // ctx
    nqk = GQA_HEADS * GQA_DIM
    kw = 2 * GQA_KV_HEADS * GQA_DIM
    k_col, v_col = nqk // kw, nqk // kw + 1
    q_idx = lambda b, i: (jnp.where(i < nq, b * nq + i, n_lat_blocks + b), 0)
    cur = lambda col: (lambda b, i: (b * nq + jnp.minimum(i, nq - 1), col))
    prev = lambda col: (lambda b, i: (b * n_win_blocks + jnp.clip(per_tile * i - 1, 0, n_win_blocks - 1), col))
    nxt = lambda col: (lambda b, i: (b * n_win_blocks + jnp.clip(per_tile * (i + 1), 0, n_win_blocks - 1), col))
    cidx = lambda col: (lambda b, i: (ctx_blk0 + b, col))
    n_rows = t if with_ctx_queries else n_batch * seq
    return pl.pallas_call(
        functools.partial(_win_kernel, n_lat_tiles=nq, seq=seq, has_ctx_tile=with_ctx_queries),
        grid=(n_batch, nq + (1 if with_ctx_queries else 0)),
        in_specs=[pl.BlockSpec((TM, nqk), q_idx),
                  pl.BlockSpec((WINDOW, kw), prev(k_col)), pl.BlockSpec((TM, kw), cur(k_col)),
                  pl.BlockSpec((WINDOW, kw), nxt(k_col)),
                  pl.BlockSpec((WINDOW, kw), prev(v_col)), pl.BlockSpec((TM, kw), cur(v_col)),
                  pl.BlockSpec((WINDOW, kw), nxt(v_col)),
                  pl.BlockSpec((ctx, kw), cidx(k_col)), pl.BlockSpec((ctx, kw), cidx(v_col)),
                  pl.BlockSpec(sink_tab.shape, lambda b, i: (0, 0))],
        out_specs=pl.BlockSpec((TM, nqk), q_idx),
        out_shape=jax.ShapeDtypeStruct((n_rows, nqk), BF16),
        compiler_params=_cparams(("arbitrary", "arbitrary")),
        name="win_attn",
    )(gqa, gqa, gqa, gqa, gqa, gqa, gqa, gqa, gqa, sink_tab)


def _ret_kernel(f_ref, b_ref, lg_ref, of_ref, ob_ref, sf_ref, sb_ref):
    @pl.when(pl.program_id(1) == 0)
    def _():
        sf_ref[...] = jnp.zeros_like(sf_ref)
        sb_ref[...] = jnp.zeros_like(sb_ref)

    L = f_ref.shape[0]
    ii = lax.broadcasted_iota(jnp.int32, (L, L), 0)
    jj = lax.broadcasted_iota(jnp.int32, (L, L), 1)
    row = lax.broadcasted_iota(jnp.int32, (L, LANES), 0).astype(F32)
    lo = _lane_lo((L, LANES))
    srow_lo = lax.broadcasted_iota(jnp.int32, (LANES, LANES), 0) < RET_QK
    nq = RET_HEADS * RET_QK

    def direction(x_ref, o_ref, s_ref, lg_row0, forward):
        for pr in range(RET_HEADS // 2):
            lg = [lg_ref[lg_row0 + 2 * pr + e:lg_row0 + 2 * pr + e + 1, :] for e in range(2)]
            lg_lane = jnp.where(lo, lg[0], lg[1])
            q = x_ref[:, pr * LANES:(pr + 1) * LANES]
            k = x_ref[:, nq + pr * LANES:nq + (pr + 1) * LANES]
            if forward:
                qd = q * jnp.exp(lg_lane * (row + 1.0))
                kd = k * jnp.exp(lg_lane * (L - 1.0 - row))
                dist = ii - jj
            else:
                qd = q * jnp.exp(lg_lane * (L - row))
                kd = k * jnp.exp(lg_lane * row)
                dist = jj - ii
            distf = jnp.maximum(dist, 0).astype(F32)
            kb, kdb = k.astype(BF16), kd.astype(BF16)
            state = s_ref[pr]
            state_b = state.astype(BF16)
            upd = []
            for e in range(2):
                hd = 2 * pr + e
                keep = lo if e == 0 else jnp.logical_not(lo)
                v = x_ref[:, 2 * nq + hd * RET_V:2 * nq + (hd + 1) * RET_V].astype(BF16)
                inner = jnp.where(dist >= 0, jnp.exp(lg[e][:, 0:1] * distf), 0.0)
                attn = _dot_nt(jnp.where(keep, q, 0.0).astype(BF16), kb) * inner
                o = _dot(attn.astype(BF16), v) + _dot(jnp.where(keep, qd, 0.0).astype(BF16), state_b)
                o_ref[:, hd * RET_V:(hd + 1) * RET_V] = o
                upd.append(_dot_tn(kdb, v))
            chunk_decay = jnp.where(srow_lo, jnp.exp(lg[0] * float(L)), jnp.exp(lg[1] * float(L)))
            s_ref[pr] = state * chunk_decay + jnp.where(srow_lo, upd[0], upd[1])

    direction(f_ref, of_ref, sf_ref, 0, True)
    direction(b_ref, ob_ref, sb_ref, RET_HEADS, False)


def _retention(ret, lg_tab, *, n_batch, seq, ctx):
    t = ret.shape[0]
    L = RET_CHUNK
    n_lat, n_ctx = seq // L, ctx // L
    ctx0 = n_batch * n_lat
    width = 2 * RET_HEADS * RET_QK + RET_HEADS * RET_V
    fwd = lambda b, s: (jnp.where(s < n_ctx, ctx0 + b * n_ctx + s, b * n_lat + s - n_ctx), 0)
    bwd = lambda b, s: (jnp.where(s < n_ctx, ctx0 + b * n_ctx + n_ctx - 1 - s, b * n_lat + n_lat - 1 - (s - n_ctx)), 0)
    out = jax.ShapeDtypeStruct((t, RET_HEADS * RET_V), F32)
    return pl.pallas_call(
        _ret_kernel,
        grid=(n_batch, n_lat + n_ctx),
        in_specs=[pl.BlockSpec((L, width), fwd), pl.BlockSpec((L, width), bwd),
                  pl.BlockSpec(lg_tab.shape, lambda b, s: (0, 0))],
        out_specs=[pl.BlockSpec((L, RET_HEADS * RET_V), fwd), pl.BlockSpec((L, RET_HEADS * RET_V), bwd)],
        out_shape=[out, out],
        scratch_shapes=[pltpu.VMEM((RET_HEADS // 2, LANES, RET_V), F32),
                        pltpu.VMEM((RET_HEADS // 2, LANES, RET_V), F32)],
        compiler_params=_cparams(("arbitrary", "arbitrary")),
        name="retention",
    )(ret, ret, lg_tab)


def _merge_kernel(a_ref, w_ref, of_ref, ob_ref, rg_ref, gt_ref, h_ref, mod_ref, gpost_ref, gffn_ref,
                  wa_ref, ww_ref, wr_ref, wo_ref, h1_ref, v_ref, *, tiles_per_batch, n_batch, d):
    i = pl.program_id(0)
    bi = jnp.minimum(i // tiles_per_batch, n_batch)
    o = of_ref[...] + ob_ref[...]
    normed = []
    for hd in range(RET_HEADS):
        oh = o[:, hd * RET_V:(hd + 1) * RET_V]
        dev = oh - jnp.mean(oh, axis=-1, keepdims=True)
        normed.append(dev * lax.rsqrt(jnp.mean(dev * dev, axis=-1, keepdims=True) + NORM_EPS))
    g = rg_ref[...]
    r = (g * jax.nn.sigmoid(g)) * jnp.concatenate(normed, axis=1)
    y = (jax.nn.sigmoid(gt_ref[:, 0:d].astype(F32)) * _dot(a_ref[...], wa_ref[...])
         + jax.nn.sigmoid(gt_ref[:, d:2 * d].astype(F32)) * _dot(w_ref[...], ww_ref[...])
         + jax.nn.sigmoid(gt_ref[:, 2 * d:3 * d].astype(F32)) * _dot(r.astype(BF16), wr_ref[...]))
    z = _dot(y.astype(BF16), wo_ref[...])
    g1 = mod_ref[pl.ds(bi, 1), 2 * d:3 * d]
    sh2 = mod_ref[pl.ds(bi, 1), 3 * d:4 * d]
    sc2 = mod_ref[pl.ds(bi, 1), 4 * d:5 * d]
    h1 = h_ref[...] + g1 * _rms(z, gpost_ref[...])
    h1_ref[...] = h1
    v_ref[...] = (_rms(h1, gffn_ref[...]) * (1.0 + sc2) + sh2).astype(v_ref.dtype)


def _merge(a, w, o_f, o_b, ret, gates, h, mods, gpost, gffn, wa, ww, wr, wo, *, n_rows, n_batch, seq):
    d = h.shape[1]
    rows = lambda i: (i, 0)
    const = lambda i: (0, 0)
    rv = RET_HEADS * RET_V
    rg_col = (2 * RET_HEADS * RET_QK + rv) // rv
    outs = [jax.ShapeDtypeStruct((n_rows, d), F32), jax.ShapeDtypeStruct((n_rows, d), BF16)]
    return pl.pallas_call(
        functools.partial(_merge_kernel, tiles_per_batch=seq // TM, n_batch=n_batch, d=d),
        grid=(n_rows // TM,),
        in_specs=[pl.BlockSpec((TM, a.shape[1]), rows), pl.BlockSpec((TM, w.shape[1]), rows),
                  pl.BlockSpec((TM, rv), rows), pl.BlockSpec((TM, rv), rows),
                  pl.BlockSpec((TM, rv), lambda i: (i, rg_col)),
                  pl.BlockSpec((TM, 3 * d), rows), pl.BlockSpec((TM, d), rows),
                  pl.BlockSpec(mods.shape, const), pl.BlockSpec((1, d), const), pl.BlockSpec((1, d), const),
                  pl.BlockSpec(wa.shape, const), pl.BlockSpec(ww.shape, const),
                  pl.BlockSpec(wr.shape, const), pl.BlockSpec(wo.shape, const)],
        out_specs=[pl.BlockSpec((TM, d), rows), pl.BlockSpec((TM, d), rows)],
        out_shape=outs,
        compiler_params=_cparams(("arbitrary",)),
        name="merge",
    )(a, w, o_f, o_b, ret, gates, h, mods, gpost, gffn, wa, ww, wr, wo)


def _router_kernel(v_ref, rw_ref, rb_ref, gate_ref):
    scores = jax.nn.sigmoid(_dot(v_ref[...], rw_ref[...]))
    sel = scores + rb_ref[...]
    shape = sel.shape
    lane = lax.broadcasted_iota(jnp.int32, shape, 1)
    grp = lane >> (EXPERTS_PER_GROUP.bit_length() - 1)
    neg = -jnp.inf

    def first_argmax(x):
        m = jnp.max(x, axis=-1, keepdims=True)
        idx = jnp.min(jnp.where(x == m, lane, N_EXPERTS), axis=-1, keepdims=True)
        return m, idx

    gscore = []
    for gi in range(N_EXPERT_GROUPS):
        x = jnp.where(grp == gi, sel, neg)
        m1, i1 = first_argmax(x)
        m2 = jnp.max(jnp.where(lane == i1, neg, x), axis=-1, keepdims=True)
        gscore.append(m1 + m2)
    ok = jnp.zeros(shape, jnp.bool_)
    for gi in range(N_EXPERT_GROUPS):
        rank = jnp.zeros_like(gscore[gi], dtype=jnp.int32)
        for gj in range(N_EXPERT_GROUPS):
            if gj == gi:
                continue
            ahead = gscore[gj] >= gscore[gi] if gj < gi else gscore[gj] > gscore[gi]
            rank = rank + ahead.astype(jnp.int32)
        ok = ok | ((grp == gi) & (rank < TOPK_GROUPS))
    sel = jnp.where(ok, sel, NEG_INF)
    chosen = jnp.zeros(shape, jnp.bool_)
    for _ in range(TOP_K):
        _, idx = first_argmax(sel)
        hit = lane == idx
        chosen = chosen | hit
        sel = jnp.where(hit, neg, sel)
    w = jnp.where(chosen, scores, 0.0)
    gate_ref[...] = ROUTED_SCALE * w / jnp.sum(w, axis=-1, keepdims=True)


def _router(v, rw, rb):
    n_rows, d = v.shape
    return pl.pallas_call(
        _router_kernel,
        grid=(n_rows // TM,),
        in_specs=[pl.BlockSpec((TM, d), lambda i: (i, 0)), pl.BlockSpec(rw.shape, lambda i: (0, 0)),
                  pl.BlockSpec(rb.shape, lambda i: (0, 0))],
        out_specs=pl.BlockSpec((TM, N_EXPERTS), lambda i: (i, 0)),
        out_shape=jax.ShapeDtypeStruct((n_rows, N_EXPERTS), F32),
        compiler_params=_cparams(("arbitrary",)),
        name="router",
    )(v, rw, rb)


def _moe_kernel(v_ref, gate_ref, wg_ref, wu_ref, wd_ref, sg_ref, su_ref, sd_ref, h1_ref, mod_ref, gpost_ref,
                o_ref, acc_ref, *, tiles_per_batch, n_batch, d, tm):
    i = pl.program_id(0)
    e = pl.program_id(1)
    x = v_ref[...]

    def ffn(wg, wu, wd, scale):
        a = _dot(x, wg)
        hid = (a * jax.nn.sigmoid(a)) * _dot(x, wu)
        if scale is not None:
            hid = hid * scale
        return _dot(hid.astype(BF16), wd)

    @pl.when(e == 0)
    def _():
        acc_ref[...] = ffn(sg_ref[...], su_ref[...], sd_ref[...], None)

    lane = lax.broadcasted_iota(jnp.int32, gate_ref.shape, 1)
    col = jnp.sum(jnp.where(lane == e, gate_ref[...], 0.0), axis=-1, keepdims=True)
    acc_ref[...] += ffn(wg_ref[...].astype(BF16), wu_ref[...].astype(BF16), wd_ref[...].astype(BF16), col)

    @pl.when(e == pl.num_programs(1) - 1)
    def _():
        bi = jnp.minimum((i * tm) // (tiles_per_batch * TM), n_batch)
        g2 = mod_ref[pl.ds(bi, 1), 5 * d:6 * d]
        o_ref[...] = h1_ref[...] + g2 * _rms(acc_ref[...], gpost_ref[...])


def _moe(v, gate, layer, exp_wg, exp_wu, exp_wd, sg, su, sd, h1, mods, gpost, *, n_batch, seq):
    n_rows, d = v.shape
    tm = next(c for c in (1024, 512, 256) if n_rows % c == 0 and seq % c == 0)
    hid = exp_wg.shape[-1]
    rows = lambda i, e: (i, 0)
    const = lambda i, e: (0, 0)
    return pl.pallas_call(
        functools.partial(_moe_kernel, tiles_per_batch=seq // TM, n_batch=n_batch, d=d, tm=tm),
        grid=(n_rows // tm, N_EXPERTS),
        in_specs=[pl.BlockSpec((tm, d), rows), pl.BlockSpec((tm, N_EXPERTS), rows),
                  pl.BlockSpec((None, None, d, hid), lambda i, e: (layer, e, 0, 0)),
                  pl.BlockSpec((None, None, d, hid), lambda i, e: (layer, e, 0, 0)),
                  pl.BlockSpec((None, None, hid, d), lambda i, e: (layer, e, 0, 0)),
                  pl.BlockSpec(sg.shape, const), pl.BlockSpec(su.shape, const), pl.BlockSpec(sd.shape, const),
                  pl.BlockSpec((tm, d), rows), pl.BlockSpec(mods.shape, const), pl.BlockSpec((1, d), const)],
        out_specs=pl.BlockSpec((tm, d), rows),
        out_shape=jax.ShapeDtypeStruct((n_rows, d), F32),
        scratch_shapes=[pltpu.VMEM((tm, d), F32)],
        compiler_params=_cparams(("arbitrary", "arbitrary")),
        name="moe",
    )(v, gate, exp_wg, exp_wu, exp_wd, sg, su, sd, h1, mods, gpost)


def _rope_tables(seq):
    rows = seq // GRID_W
    row_id = jnp.repeat(jnp.arange(rows, dtype=F32), GRID_W)
    col_id = jnp.tile(jnp.arange(GRID_W, dtype=F32), rows)

    def tables(rot_dim):
        axis_dim = rot_dim // 2
        inv_freq = ROPE_BASE ** (-jnp.arange(0, axis_dim, 2, dtype=F32) / axis_dim)
        ang_r = row_id[:, None] * inv_freq[None, :]
        ang_c = col_id[:, None] * inv_freq[None, :]
        cos = jnp.concatenate([jnp.cos(ang_r), jnp.cos(ang_r), jnp.cos(ang_c), jnp.cos(ang_c)], axis=1)
        sin = jnp.concatenate([-jnp.sin(ang_r), jnp.sin(ang_r), -jnp.sin(ang_c), jnp.sin(ang_c)], axis=1)
        return cos, sin

    cos64, sin64 = tables(GQA_DIM)
    cos32, sin32 = tables(MLA_ROPE)
    ones = jnp.ones((seq, MLA_NOPE), F32)
    pad = LANES - MLA_NOPE - MLA_ROPE
    cospe = jnp.concatenate([ones, cos32, jnp.ones((seq, pad), F32)], axis=1)
    sinpe = jnp.concatenate([0 * ones, sin32, jnp.zeros((seq, pad), F32)], axis=1)
    tab = jnp.concatenate([cos64, cos64, sin64, sin64, cospe, sinpe], axis=1)
    ident = jnp.concatenate([jnp.ones((TM, LANES), F32), jnp.zeros((TM, LANES), F32),
                             jnp.ones((TM, LANES), F32), jnp.zeros((TM, LANES), F32)], axis=1)
    return jnp.concatenate([tab, ident], axis=0)


def _pack_w_in(w):
    d = w.shape[0]
    sizes = (MLA_Q_LORA, MLA_KV_LORA, MLA_ROPE, GQA_HEADS * GQA_DIM, GQA_KV_HEADS * GQA_DIM,
             GQA_KV_HEADS * GQA_DIM, RET_HEADS * RET_QK, RET_HEADS * RET_QK, RET_HEADS * RET_V,
             RET_HEADS * RET_V, 3 * d)
    offs, parts = 0, []
    for s in sizes:
        parts.append(w[:, offs:offs + s])
        offs += s
    cq, ckv, kpe, gq, gk, gv, rq, rk, rv, rg, gates = parts

    def twice(m):
        heads = [m[:, i * GQA_DIM:(i + 1) * GQA_DIM] for i in range(GQA_KV_HEADS)]
        return jnp.concatenate([hh for hd in heads for hh in (hd, hd)], axis=1)

    kpe_slab = jnp.concatenate([jnp.zeros((d, MLA_NOPE), F32), kpe,
                                jnp.zeros((d, LANES - MLA_NOPE - MLA_ROPE), F32)], axis=1)
    packed = jnp.concatenate([cq, ckv, kpe_slab, gq * GQA_DIM ** -0.5, twice(gk), twice(gv),
                              rq, rk * RET_QK ** -0.5, rv, rg, gates], axis=1)
    assert packed.shape[1] == W_COLS
    return packed.astype(BF16)


def _pack_mla_up(w_uq, w_ukv):
    r = w_uq.shape[0]
    dq = MLA_NOPE + MLA_ROPE
    wq = jnp.pad(w_uq.reshape(r, MLA_HEADS, dq), ((0, 0), (0, 0), (0, LANES - dq))).reshape(r, MLA_HEADS * LANES)
    kv = w_ukv.reshape(r, MLA_HEADS, MLA_NOPE + MLA_V)
    wk = jnp.pad(kv[:, :, :MLA_NOPE], ((0, 0), (0, 0), (0, LANES - MLA_NOPE))).reshape(r, MLA_HEADS * LANES)
    wv = kv[:, :, MLA_NOPE:].reshape(r, MLA_HEADS * MLA_V)
    return wq.astype(BF16), wk.astype(BF16), wv.astype(BF16)


def kernel(x, c, ctx, c_ctx, ada_w, ada_b, norm_mix_pre, norm_mix_post, norm_ffn_pre, norm_ffn_post, w_in, mla_q_norm, mla_w_uq, mla_kv_norm, mla_w_ukv, gqa_sink, ret_decay_fwd, ret_decay_bwd, w_br_mla, w_br_gqa, w_br_ret, w_out, router_w, router_bias, exp_w_gate, exp_w_up, exp_w_down, shared_w_gate, shared_w_up, shared_w_down):
    n_batch, seq, d = x.shape
    n_ctx = ctx.shape[1]
    depth = ada_w.shape[0]
    n_lat_rows = n_batch * seq
    assert seq % TM == 0 and n_ctx == TM and n_batch < MOD_ROWS and seq % GRID_W == 0

    cond = jnp.zeros((MOD_ROWS, d), F32).at[:n_batch].set(c).at[n_batch].set(c_ctx)
    mods_all = _adaln(cond, ada_w, ada_b)
    rope = _rope_tables(seq)
    h = jnp.concatenate([x.reshape(n_lat_rows, d), ctx.reshape(n_batch * n_ctx, d)], axis=0)
    row = lambda p: p.reshape(1, -1)
    dims = dict(n_batch=n_batch, seq=seq)

    for l in range(depth):
        last = l == depth - 1
        mods = mods_all[l]
        wq, wk, wv = _pack_mla_up(mla_w_uq[l], mla_w_ukv[l])
        mq, mk, mv, gqa, ret, gates = _inproj(h, mods, row(norm_mix_pre[l]), _pack_w_in(w_in[l]), rope,
                                              row(mla_q_norm[l]), row(mla_kv_norm[l]), wq, wk, wv, **dims)
        a = _mla_attention(mq, mk, mv, ctx=n_ctx, with_ctx_queries=not last, **dims)
        sink_tab = jnp.broadcast_to(gqa_sink[l].astype(F32)[:, None], (GQA_HEADS, LANES))
        w = _window_attention(gqa, sink_tab, ctx=n_ctx, with_ctx_queries=not last, **dims)
        lg = jnp.concatenate([jax.nn.log_sigmoid(ret_decay_fwd[l].astype(F32)),
                              jax.nn.log_sigmoid(ret_decay_bwd[l].astype(F32))])
        o_f, o_b = _retention(ret, jnp.broadcast_to(lg[:, None], (2 * RET_HEADS, LANES)), ctx=n_ctx, **dims)
        n_rows = n_lat_rows if last else h.shape[0]
        h1, v = _merge(a, w, o_f, o_b, ret, gates, h, mods, row(norm_mix_post[l]), row(norm_ffn_pre[l]),
                       w_br_mla[l].astype(BF16), w_br_gqa[l].astype(BF16), w_br_ret[l].astype(BF16),
                       w_out[l].astype(BF16), n_rows=n_rows, **dims)
        gate = _router(v, router_w[l].astype(BF16), row(router_bias[l].astype(F32)))
        h = _moe(v, gate, l, exp_w_gate, exp_w_up, exp_w_down, shared_w_gate[l].astype(BF16),
                 shared_w_up[l].astype(BF16), shared_w_down[l].astype(BF16), h1, mods,
                 row(norm_ffn_post[l]), **dims)
    return h[:n_lat_rows].reshape(n_batch, seq, d)
```

```python
import functools

import jax
import jax.numpy as jnp
from jax import lax
from jax.experimental import pallas as pl
from jax.experimental.pallas import tpu as pltpu
from jax.experimental.pallas import tpu_sc as plsc

F32 = jnp.float32
BF16 = jnp.bfloat16

GRID_W = 64
ROPE_BASE = 10000.0
NORM_EPS = 1e-6
NEG_INF = -1e30
N_MOD = 6
MLA_HEADS, MLA_NOPE, MLA_ROPE, MLA_V = 8, 64, 32, 64
MLA_Q_LORA, MLA_KV_LORA = 256, 256
GQA_HEADS, GQA_KV_HEADS, GQA_DIM, WINDOW = 8, 2, 64, 128
RET_HEADS, RET_QK, RET_V, RET_CHUNK = 4, 64, 128, 128
N_EXPERTS, N_EXPERT_GROUPS, TOPK_GROUPS, TOP_K = 64, 8, 4, 8
EXPERTS_PER_GROUP = N_EXPERTS // N_EXPERT_GROUPS
ROUTED_SCALE = 2.5

LANES = 128
TM = 256
MOD_ROWS = 8
V7X_VMEM_LIMIT = 56 * 1024 * 1024

C_CQ, C_CKV, C_KPE = 0, 256, 512
C_G = 640
C_R = 1664
C_GATE = 3200
W_COLS = 6272


def _cparams(sem):
    return pltpu.CompilerParams(dimension_semantics=sem, vmem_limit_bytes=V7X_VMEM_LIMIT)


def _rms(x, g):
    return x * lax.rsqrt(jnp.mean(x * x, axis=-1, keepdims=True) + NORM_EPS) * g


def _dot(a, b):
    return jnp.dot(a, b, preferred_element_type=F32)


def _dot_nt(a, b):
    return lax.dot_general(a, b, (((1,), (1,)), ((), ())), preferred_element_type=F32)


def _dot_tn(a, b):
    return lax.dot_general(a, b, (((0,), (0,)), ((), ())), preferred_element_type=F32)


def _rope(x, cos, sin, half):
    n = x.shape[-1]
    reps = n // LANES
    if reps > 1:
        cos = jnp.concatenate([cos] * reps, axis=1)
        sin = jnp.concatenate([sin] * reps, axis=1)
    lane = lax.broadcasted_iota(jnp.int32, x.shape, 1)
    up = pltpu.roll(x, half, 1)
    dn = pltpu.roll(x, n - half, 1)
    partner = jnp.where((lane & (2 * half - 1)) < half, dn, up)
    return x * cos + partner * sin


def _lane_lo(shape):
    return (lax.broadcasted_iota(jnp.int32, shape, 1) & (LANES - 1)) < (LANES // 2)


def _pack_halves(x):
    n = x.shape[1] // 2
    bits = lambda t: lax.bitcast_convert_type(t.astype(BF16).astype(F32), jnp.uint32)
    return (bits(x[:, :n]) >> 16) | bits(x[:, n:])


def _unpack_halves(p):
    lo = lax.bitcast_convert_type(p << 16, F32)
    hi = lax.bitcast_convert_type(p & jnp.uint32(0xFFFF0000), F32)
    return lo, hi


def _ada_kernel(c_ref, w_ref, b_ref, o_ref):
    c = c_ref[...]
    s = c * jax.nn.sigmoid(c)
    o_ref[...] = _dot(s.astype(BF16), w_ref[...].astype(BF16)) + b_ref[...]


def _adaln(cond, ada_w, ada_b):
    n_layers, d, n = ada_w.shape
    tn = 1024
    return pl.pallas_call(
        _ada_kernel,
        grid=(n_layers, n // tn),
        in_specs=[pl.BlockSpec((MOD_ROWS, d), lambda l, j: (0, 0)),
                  pl.BlockSpec((None, d, tn), lambda l, j: (l, 0, j)),
                  pl.BlockSpec((None, 1, tn), lambda l, j: (l, 0, j))],
        out_specs=pl.BlockSpec((None, MOD_ROWS, tn), lambda l, j: (l, 0, j)),
        out_shape=jax.ShapeDtypeStruct((n_layers, MOD_ROWS, n), F32),
        compiler_params=_cparams(("arbitrary", "arbitrary")),
        name="adaln",
    )(cond, ada_w, ada_b.reshape(n_layers, 1, n))


def _inproj_kernel(h_ref, mod_ref, gpre_ref, w_ref, rope_ref, qn_ref, kvn_ref, wuq_ref, wuk_ref, wuv_ref,
                   mq_ref, mk_ref, mv_ref, gqa_ref, ret_ref, gate_ref, *, tiles_per_batch, n_batch, d):
    i = pl.program_id(0)
    bi = jnp.minimum(i // tiles_per_batch, n_batch)
    sh = mod_ref[pl.ds(bi, 1), 0:d]
    sc = mod_ref[pl.ds(bi, 1), d:2 * d]
    u = (_rms(h_ref[...], gpre_ref[...]) * (1.0 + sc) + sh).astype(BF16)

    cos64 = rope_ref[:, 0:LANES]
    sin64 = rope_ref[:, LANES:2 * LANES]
    cospe = rope_ref[:, 2 * LANES:3 * LANES]
    sinpe = rope_ref[:, 3 * LANES:4 * LANES]

    c = _dot(u, w_ref[:, C_CQ:C_G])
    kpe = _rope(c[:, C_KPE:C_G], cospe, sinpe, MLA_ROPE // 4)
    qn = _rms(c[:, C_CQ:C_CKV], qn_ref[...]).astype(BF16)
    q = _rope(_dot(qn, wuq_ref[...]), cospe, sinpe, MLA_ROPE // 4)
    mq_ref[...] = (q * (MLA_NOPE + MLA_ROPE) ** -0.5).astype(mq_ref.dtype)
    kvn = _rms(c[:, C_CKV:C_KPE], kvn_ref[...]).astype(BF16)
    k = _dot(kvn, wuk_ref[...]) + jnp.concatenate([kpe] * MLA_HEADS, axis=1)
    mk_ref[...] = k.astype(mk_ref.dtype)
    mv_ref[...] = _dot(kvn, wuv_ref[...]).astype(mv_ref.dtype)

    g = _dot(u, w_ref[:, C_G:C_R])
    n_qk = GQA_HEADS * GQA_DIM + 2 * GQA_KV_HEADS * GQA_DIM
    gqa_ref[:, 0:n_qk] = _rope(g[:, 0:n_qk], cos64, sin64, GQA_DIM // 4).astype(gqa_ref.dtype)
    gqa_ref[:, n_qk:] = g[:, n_qk:].astype(gqa_ref.dtype)

    r = _dot(u, w_ref[:, C_R:C_GATE])
    n_qk = 2 * RET_HEADS * RET_QK
    ret_ref[:, 0:n_qk] = _rope(r[:, 0:n_qk], cos64, sin64, RET_QK // 4)
    ret_ref[:, n_qk:] = r[:, n_qk:]

    gate_ref[...] = _dot(u, w_ref[:, C_GATE:W_COLS]).astype(gate_ref.dtype)


def _inproj(h, mods, gpre, w_all, rope, qn, kvn, wuq, wuk, wuv, *, n_batch, seq):
    t, d = h.shape
    tiles_per_batch = seq // TM
    n_lat_tiles = n_batch * tiles_per_batch
    const = lambda i: (0, 0)
    rows = lambda i: (i, 0)
    rope_idx = lambda i: (jnp.where(i < n_lat_tiles, i % tiles_per_batch, tiles_per_batch), 0)
    hq = MLA_HEADS * LANES
    outs = [jax.ShapeDtypeStruct((t, hq), BF16), jax.ShapeDtypeStruct((t, hq), BF16),
            jax.ShapeDtypeStruct((t, MLA_HEADS * MLA_V), BF16),
            jax.ShapeDtypeStruct((t, C_R - C_G), BF16),
            jax.ShapeDtypeStruct((t, C_GATE - C_R), F32),
            jax.ShapeDtypeStruct((t, W_COLS - C_GATE), BF16)]
    return pl.pallas_call(
        functools.partial(_inproj_kernel, tiles_per_batch=tiles_per_batch, n_batch=n_batch, d=d),
        grid=(t // TM,),
        in_specs=[pl.BlockSpec((TM, d), rows),
                  pl.BlockSpec(mods.shape, const),
                  pl.BlockSpec((1, d), const),
                  pl.BlockSpec(w_all.shape, const),
                  pl.BlockSpec((TM, 4 * LANES), rope_idx),
                  pl.BlockSpec(qn.shape, const), pl.BlockSpec(kvn.shape, const),
                  pl.BlockSpec(wuq.shape, const), pl.BlockSpec(wuk.shape, const), pl.BlockSpec(wuv.shape, const)],
        out_specs=[pl.BlockSpec((TM, o.shape[1]), rows) for o in outs],
        out_shape=outs,
        compiler_params=_cparams(("arbitrary",)),
        name="inproj",
    )(h, mods, gpre, w_all, rope, qn, kvn, wuq, wuk, wuv)


def _mla_kernel(q_ref, kl_ref, kc_ref, vl_ref, vc_ref, o_ref, *, n_lat_tiles, has_ctx_tile):
    def body(with_lat):
        outs = []
        for h in range(2):
            sl = slice(h * LANES, (h + 1) * LANES)
            qh = q_ref[:, sl]
            s_c = _dot_nt(qh, kc_ref[:, sl])
            m = jnp.max(s_c, axis=-1, keepdims=True)
            if with_lat:
                s_l = _dot_nt(qh, kl_ref[:, sl])
                m = jnp.maximum(m, jnp.max(s_l, axis=-1, keepdims=True))
            p_c = jnp.exp(s_c - m)
            den = jnp.sum(p_c, axis=-1, keepdims=True)
            o = _dot(p_c.astype(BF16), vc_ref[...])
            if with_lat:
                p_l = jnp.exp(s_l - m)
                den = den + jnp.sum(p_l, axis=-1, keepdims=True)
                o = o + _dot(p_l.astype(BF16), vl_ref[...])
            outs.append(o / den)
        o_ref[...] = jnp.where(_lane_lo(outs[0].shape), outs[0], outs[1]).astype(o_ref.dtype)

    if has_ctx_tile:
        i = pl.program_id(2)
        pl.when(i < n_lat_tiles)(lambda: body(True))
        pl.when(i == n_lat_tiles)(lambda: body(False))
    else:
        body(True)


def _mla_attention(mq, mk, mv, *, n_batch, seq, ctx, with_ctx_queries):
    t = mq.shape[0]
    nq = seq // TM
    n_lat_blocks = n_batch * nq
    ctx_blk0 = n_batch * seq // ctx
    n_pairs = MLA_HEADS // 2
    assert ctx == TM
    q_idx = lambda b, p, i: (jnp.where(i < nq, b * nq + i, n_lat_blocks + b), p)
    lat_idx = lambda b, p, i: (b, p)
    ctx_idx = lambda b, p, i: (ctx_blk0 + b, p)
    n_rows = t if with_ctx_queries else n_batch * seq
    return pl.pallas_call(
        functools.partial(_mla_kernel, n_lat_tiles=nq, has_ctx_tile=with_ctx_queries),
        grid=(n_batch, n_pairs, nq + (1 if with_ctx_queries else 0)),
        in_specs=[pl.BlockSpec((TM, 2 * LANES), q_idx),
                  pl.BlockSpec((seq, 2 * LANES), lat_idx),
                  pl.BlockSpec((ctx, 2 * LANES), ctx_idx),
                  pl.BlockSpec((seq, LANES), lat_idx),
                  pl.BlockSpec((ctx, LANES), ctx_idx)],
        out_specs=pl.BlockSpec((TM, LANES), q_idx),
        out_shape=jax.ShapeDtypeStruct((n_rows, MLA_HEADS * MLA_V), BF16),
        compiler_params=_cparams(("arbitrary", "arbitrary", "arbitrary")),
        name="mla_attn",
    )(mq, mk, mk, mv, mv)


def _win_kernel(q_ref, kp_ref, kcur_ref, kn_ref, vp_ref, vcur_ref, vn_ref, kc_ref, vc_ref, sink_ref, o_ref,
                *, n_lat_tiles, seq, has_ctx_tile):
    i = pl.program_id(1)
    tq = q_ref.shape[0]
    group = GQA_HEADS // GQA_KV_HEADS

    def body(with_lat):
        if with_lat:
            n_win = tq + 2 * WINDOW
            q_pos = i * tq + lax.broadcasted_iota(jnp.int32, (tq, n_win), 0)
            k_pos = i * tq - WINDOW + lax.broadcasted_iota(jnp.int32, (tq, n_win), 1)
            valid = (jnp.abs(q_pos - k_pos) <= WINDOW) & (k_pos >= 0) & (k_pos < seq)
        for kv in range(GQA_KV_HEADS):
            sl = slice(kv * LANES, (kv + 1) * LANES)
            k_ctx, v_ctx = kc_ref[:, sl], vc_ref[:, sl]
            if with_lat:
                k_win = jnp.concatenate([kp_ref[:, sl], kcur_ref[:, sl], kn_ref[:, sl]], axis=0)
                v_win = jnp.concatenate([vp_ref[:, sl], vcur_ref[:, sl], vn_ref[:, sl]], axis=0)
            for pr in range(group // 2):
                col = slice((kv * (group // 2) + pr) * LANES, (kv * (group // 2) + pr + 1) * LANES)
                qp = q_ref[:, col]
                lo = _lane_lo(qp.shape)
                halves = []
                for half in range(2):
                    hd = kv * group + pr * 2 + half
                    qm = jnp.where(lo if half == 0 else jnp.logical_not(lo), qp, jnp.zeros_like(qp))
                    sink = sink_ref[hd:hd + 1, 0:1]
                    s_c = _dot_nt(qm, k_ctx)
                    m = jnp.maximum(jnp.max(s_c, axis=-1, keepdims=True), sink)
                    if with_lat:
                        s_l = jnp.where(valid, _dot_nt(qm, k_win), NEG_INF)
                        m = jnp.maximum(m, jnp.max(s_l, axis=-1, keepdims=True))
                    p_c = jnp.exp(s_c - m)
                    den = jnp.sum(p_c, axis=-1, keepdims=True) + jnp.exp(sink - m)
                    o = _dot(p_c.astype(BF16), v_ctx)
                    if with_lat:
                        p_l = jnp.exp(s_l - m)
                        den = den + jnp.sum(p_l, axis=-1, keepdims=True)
                        o = o + _dot(p_l.astype(BF16), v_win)
                    halves.append(o / den)
                o_ref[:, col] = jnp.where(lo, halves[0], halves[1]).astype(o_ref.dtype)

    if has_ctx_tile:
        pl.when(i < n_lat_tiles)(lambda: body(True))
        pl.when(i == n_lat_tiles)(lambda: body(False))
    else:
        body(True)


def _window_attention(gqa, sink_tab, *, n_batch, seq, ctx, with_ctx_queries):
    t = gqa.shape[0]
    nq = seq // TM
    n_lat_blocks = n_batch * nq
    per_tile = TM // WINDOW
    n_win_blocks = seq // WINDOW
    assert ctx == TM
    ctx_blk0 = n_batch * seq // ctx
    nqk = GQA_HEADS * GQA_DIM
    kw = 2 * GQA_KV_HEADS * GQA_DIM
    k_col, v_col = nqk // kw, nqk // kw + 1
    q_idx = lambda b, i: (jnp.where(i < nq, b * nq + i, n_lat_blocks + b), 0)
    cur = lambda col: (lambda b, i: (b * nq + jnp.minimum(i, nq - 1), col))
    prev = lambda col: (lambda b, i: (b * n_win_blocks + jnp.clip(per_tile * i - 1, 0, n_win_blocks - 1), col))
    nxt = lambda col: (lambda b, i: (b * n_win_blocks + jnp.clip(per_tile * (i + 1), 0, n_win_blocks - 1), col))
    cidx = lambda col: (lambda b, i: (ctx_blk0 + b, col))
    n_rows = t if with_ctx_queries else n_batch * seq
    return pl.pallas_call(
        functools.partial(_win_kernel, n_lat_tiles=nq, seq=seq, has_ctx_tile=with_ctx_queries),
        grid=(n_batch, nq + (1 if with_ctx_queries else 0)),
        in_specs=[pl.BlockSpec((TM, nqk), q_idx),
                  pl.BlockSpec((WINDOW, kw), prev(k_col)), pl.BlockSpec((TM, kw), cur(k_col)),
                  pl.BlockSpec((WINDOW, kw), nxt(k_col)),
                  pl.BlockSpec((WINDOW, kw), prev(v_col)), pl.BlockSpec((TM, kw), cur(v_col)),
                  pl.BlockSpec((WINDOW, kw), nxt(v_col)),
                  pl.BlockSpec((ctx, kw), cidx(k_col)), pl.BlockSpec((ctx, kw), cidx(v_col)),
                  pl.BlockSpec(sink_tab.shape, lambda b, i: (0, 0))],
        out_specs=pl.BlockSpec((TM, nqk), q_idx),
        out_shape=jax.ShapeDtypeStruct((n_rows, nqk), BF16),
        compiler_params=_cparams(("arbitrary", "arbitrary")),
        name="win_attn",
    )(gqa, gqa, gqa, gqa, gqa, gqa, gqa, gqa, gqa, sink_tab)


def _ret_kernel(f_ref, b_ref, lg_ref, of_ref, ob_ref, sf_ref, sb_ref):
    @pl.when(pl.program_id(1) == 0)
    def _():
        sf_ref[...] = jnp.zeros_like(sf_ref)
        sb_ref[...] = jnp.zeros_like(sb_ref)

    L = f_ref.shape[0]
    ii = lax.broadcasted_iota(jnp.int32, (L, L), 0)
    jj = lax.broadcasted_iota(jnp.int32, (L, L), 1)
    row = lax.broadcasted_iota(jnp.int32, (L, LANES), 0).astype(F32)
    lo = _lane_lo((L, LANES))
    srow_lo = lax.broadcasted_iota(jnp.int32, (LANES, LANES), 0) < RET_QK
    nq = RET_HEADS * RET_QK

    def direction(x_ref, o_ref, s_ref, lg_row0, forward):
        for pr in range(RET_HEADS // 2):
            lg = [lg_ref[lg_row0 + 2 * pr + e:lg_row0 + 2 * pr + e + 1, :] for e in range(2)]
            lg_lane = jnp.where(lo, lg[0], lg[1])
            q = x_ref[:, pr * LANES:(pr + 1) * LANES]
            k = x_ref[:, nq + pr * LANES:nq + (pr + 1) * LANES]
            if forward:
                qd = q * jnp.exp(lg_lane * (row + 1.0))
                kd = k * jnp.exp(lg_lane * (L - 1.0 - row))
                dist = ii - jj
            else:
                qd = q * jnp.exp(lg_lane * (L - row))
                kd = k * jnp.exp(lg_lane * row)
                dist = jj - ii
            distf = jnp.maximum(dist, 0).astype(F32)
            kb, kdb = k.astype(BF16), kd.astype(BF16)
            state = s_ref[pr]
            state_b = state.astype(BF16)
            upd = []
            for e in range(2):
                hd = 2 * pr + e
                keep = lo if e == 0 else jnp.logical_not(lo)
                v = x_ref[:, 2 * nq + hd * RET_V:2 * nq + (hd + 1) * RET_V].astype(BF16)
                inner = jnp.where(dist >= 0, jnp.exp(lg[e][:, 0:1] * distf), 0.0)
                attn = _dot_nt(jnp.where(keep, q, 0.0).astype(BF16), kb) * inner
                o = _dot(attn.astype(BF16), v) + _dot(jnp.where(keep, qd, 0.0).astype(BF16), state_b)
                o_ref[:, hd * RET_V:(hd + 1) * RET_V] = o
                upd.append(_dot_tn(kdb, v))
            chunk_decay = jnp.where(srow_lo, jnp.exp(lg[0] * float(L)), jnp.exp(lg[1] * float(L)))
            s_ref[pr] = state * chunk_decay + jnp.where(srow_lo, upd[0], upd[1])

    direction(f_ref, of_ref, sf_ref, 0, True)
    direction(b_ref, ob_ref, sb_ref, RET_HEADS, False)


def _retention(ret, lg_tab, *, n_batch, seq, ctx):
    t = ret.shape[0]
    L = RET_CHUNK
    n_lat, n_ctx = seq // L, ctx // L
    ctx0 = n_batch * n_lat
    width = 2 * RET_HEADS * RET_QK + RET_HEADS * RET_V
    fwd = lambda b, s: (jnp.where(s < n_ctx, ctx0 + b * n_ctx + s, b * n_lat + s - n_ctx), 0)
    bwd = lambda b, s: (jnp.where(s < n_ctx, ctx0 + b * n_ctx + n_ctx - 1 - s, b * n_lat + n_lat - 1 - (s - n_ctx)), 0)
    out = jax.ShapeDtypeStruct((t, RET_HEADS * RET_V), F32)
    return pl.pallas_call(
        _ret_kernel,
        grid=(n_batch, n_lat + n_ctx),
        in_specs=[pl.BlockSpec((L, width), fwd), pl.BlockSpec((L, width), bwd),
                  pl.BlockSpec(lg_tab.shape, lambda b, s: (0, 0))],
        out_specs=[pl.BlockSpec((L, RET_HEADS * RET_V), fwd), pl.BlockSpec((L, RET_HEADS * RET_V), bwd)],
        out_shape=[out, out],
        scratch_shapes=[pltpu.VMEM((RET_HEADS // 2, LANES, RET_V), F32),
                        pltpu.VMEM((RET_HEADS // 2, LANES, RET_V), F32)],
        compiler_params=_cparams(("arbitrary", "arbitrary")),
        name="retention",
    )(ret, ret, lg_tab)


def _merge_kernel(a_ref, w_ref, of_ref, ob_ref, rg_ref, gt_ref, h_ref, mod_ref, gpost_ref, gffn_ref,
                  wa_ref, ww_ref, wr_ref, wo_ref, h1_ref, v_ref, vp_ref, *, tiles_per_batch, n_batch, d):
    i = pl.program_id(0)
    bi = jnp.minimum(i // tiles_per_batch, n_batch)
    o = of_ref[...] + ob_ref[...]
    normed = []
    for hd in range(RET_HEADS):
        oh = o[:, hd * RET_V:(hd + 1) * RET_V]
        dev = oh - jnp.mean(oh, axis=-1, keepdims=True)
        normed.append(dev * lax.rsqrt(jnp.mean(dev * dev, axis=-1, keepdims=True) + NORM_EPS))
    g = rg_ref[...]
    r = (g * jax.nn.sigmoid(g)) * jnp.concatenate(normed, axis=1)
    y = (jax.nn.sigmoid(gt_ref[:, 0:d].astype(F32)) * _dot(a_ref[...], wa_ref[...])
         + jax.nn.sigmoid(gt_ref[:, d:2 * d].astype(F32)) * _dot(w_ref[...], ww_ref[...])
         + jax.nn.sigmoid(gt_ref[:, 2 * d:3 * d].astype(F32)) * _dot(r.astype(BF16), wr_ref[...]))
    z = _dot(y.astype(BF16), wo_ref[...])
    g1 = mod_ref[pl.ds(bi, 1), 2 * d:3 * d]
    sh2 = mod_ref[pl.ds(bi, 1), 3 * d:4 * d]
    sc2 = mod_ref[pl.ds(bi, 1), 4 * d:5 * d]
    h1 = h_ref[...] + g1 * _rms(z, gpost_ref[...])
    h1_ref[...] = h1
    v = _rms(h1, gffn_ref[...]) * (1.0 + sc2) + sh2
    v_ref[...] = v.astype(v_ref.dtype)
    vp_ref[...] = _pack_halves(v)


def _merge(a, w, o_f, o_b, ret, gates, h, mods, gpost, gffn, wa, ww, wr, wo, *, n_rows, n_batch, seq):
    d = h.shape[1]
    rows = lambda i: (i, 0)
    const = lambda i: (0, 0)
    rv = RET_HEADS * RET_V
    rg_col = (2 * RET_HEADS * RET_QK + rv) // rv
    outs = [jax.ShapeDtypeStruct((n_rows, d), F32), jax.ShapeDtypeStruct((n_rows, d), BF16),
            jax.ShapeDtypeStruct((n_rows, d // 2), jnp.uint32)]
    return pl.pallas_call(
        functools.partial(_merge_kernel, tiles_per_batch=seq // TM, n_batch=n_batch, d=d),
        grid=(n_rows // TM,),
        in_specs=[pl.BlockSpec((TM, a.shape[1]), rows), pl.BlockSpec((TM, w.shape[1]), rows),
                  pl.BlockSpec((TM, rv), rows), pl.BlockSpec((TM, rv), rows),
                  pl.BlockSpec((TM, rv), lambda i: (i, rg_col)),
                  pl.BlockSpec((TM, 3 * d), rows), pl.BlockSpec((TM, d), rows),
                  pl.BlockSpec(mods.shape, const), pl.BlockSpec((1, d), const), pl.BlockSpec((1, d), const),
                  pl.BlockSpec(wa.shape, const), pl.BlockSpec(ww.shape, const),
                  pl.BlockSpec(wr.shape, const), pl.BlockSpec(wo.shape, const)],
        out_specs=[pl.BlockSpec((TM, o.shape[1]), rows) for o in outs],
        out_shape=outs,
        compiler_params=_cparams(("arbitrary",)),
        name="merge",
    )(a, w, o_f, o_b, ret, gates, h, mods, gpost, gffn, wa, ww, wr, wo)


def _router_kernel(v_ref, rw_ref, rb_ref, eidx_ref, rank_ref, w_ref, cnt_ref, carry_ref):
    @pl.when(pl.program_id(0) == 0)
    def _():
        carry_ref[...] = jnp.zeros_like(carry_ref)

    scores = jax.nn.sigmoid(_dot(v_ref[...], rw_ref[...]))
    sel = scores + rb_ref[...]
    shape = sel.shape
    lane = lax.broadcasted_iota(jnp.int32, shape, 1)
    grp = lane >> (EXPERTS_PER_GROUP.bit_length() - 1)
    neg = -jnp.inf

    def first_argmax(x):
        m = jnp.max(x, axis=-1, keepdims=True)
        idx = jnp.min(jnp.where(x == m, lane, N_EXPERTS), axis=-1, keepdims=True)
        return m, idx

    gscore = []
    for gi in range(N_EXPERT_GROUPS):
        x = jnp.where(grp == gi, sel, neg)
        m1, i1 = first_argmax(x)
        m2 = jnp.max(jnp.where(lane == i1, neg, x), axis=-1, keepdims=True)
        gscore.append(m1 + m2)
    ok = jnp.zeros(shape, jnp.bool_)
    for gi in range(N_EXPERT_GROUPS):
        rank = jnp.zeros_like(gscore[gi], dtype=jnp.int32)
        for gj in range(N_EXPERT_GROUPS):
            if gj == gi:
                continue
            ahead = gscore[gj] >= gscore[gi] if gj < gi else gscore[gj] > gscore[gi]
            rank = rank + ahead.astype(jnp.int32)
        ok = ok | ((grp == gi) & (rank < TOPK_GROUPS))
    sel = jnp.where(ok, sel, NEG_INF)
    chosen = jnp.zeros(shape, jnp.bool_)
    picks = []
    for _ in range(TOP_K):
        _, idx = first_argmax(sel)
        hit = lane == idx
        chosen = chosen | hit
        sel = jnp.where(hit, neg, sel)
        picks.append(idx)
    w = jnp.where(chosen, scores, 0.0)
    gate = ROUTED_SCALE * w / jnp.sum(w, axis=-1, keepdims=True)

    tm = shape[0]
    member = jnp.where(chosen, 1.0, 0.0)
    earlier = lax.broadcasted_iota(jnp.int32, (tm, tm), 0) > lax.broadcasted_iota(jnp.int32, (tm, tm), 1)
    pos = _dot(jnp.where(earlier, 1.0, 0.0).astype(BF16), member.astype(BF16)) + carry_ref[...]
    slot_lane = lax.broadcasted_iota(jnp.int32, (tm, TOP_K), 1)
    eidx = jnp.zeros((tm, TOP_K), jnp.int32)
    rank = jnp.zeros((tm, TOP_K), F32)
    wsel = jnp.zeros((tm, TOP_K), F32)
    for k, idx in enumerate(picks):
        hit = lane == idx
        eidx = jnp.where(slot_lane == k, idx, eidx)
        rank = jnp.where(slot_lane == k, jnp.sum(jnp.where(hit, pos, 0.0), axis=-1, keepdims=True), rank)
        wsel = jnp.where(slot_lane == k, jnp.sum(jnp.where(hit, gate, 0.0), axis=-1, keepdims=True), wsel)
    eidx_ref[...] = eidx
    rank_ref[...] = rank
    w_ref[...] = wsel
    carry_ref[...] += jnp.sum(member, axis=0, keepdims=True)
    cnt_ref[...] = carry_ref[...]


def _router(v, rw, rb):
    n_rows, d = v.shape
    rows = lambda i: (i, 0)
    const = lambda i: (0, 0)
    outs = [jax.ShapeDtypeStruct((n_rows, TOP_K), jnp.int32), jax.ShapeDtypeStruct((n_rows, TOP_K), F32),
            jax.ShapeDtypeStruct((n_rows, TOP_K), F32), jax.ShapeDtypeStruct((1, N_EXPERTS), F32)]
    return pl.pallas_call(
        _router_kernel,
        grid=(n_rows // TM,),
        in_specs=[pl.BlockSpec((TM, d), rows), pl.BlockSpec(rw.shape, const), pl.BlockSpec(rb.shape, const)],
        out_specs=[pl.BlockSpec((TM, TOP_K), rows), pl.BlockSpec((TM, TOP_K), rows),
                   pl.BlockSpec((TM, TOP_K), rows), pl.BlockSpec((1, N_EXPERTS), const)],
        out_shape=outs,
        scratch_shapes=[pltpu.VMEM((1, N_EXPERTS), F32)],
        compiler_params=_cparams(("arbitrary",)),
        name="router",
    )(v, rw, rb)


def _slots_kernel(eidx_ref, rank_ref, cnt_ref, slot_ref):
    eidx = eidx_ref[...]
    lane = lax.broadcasted_iota(jnp.int32, (eidx.shape[0], N_EXPERTS), 1)
    slot_lane = lax.broadcasted_iota(jnp.int32, eidx.shape, 1)
    base = jnp.zeros(eidx.shape, F32)
    for k in range(TOP_K):
        before = jnp.sum(jnp.where(lane < eidx[:, k:k + 1], cnt_ref[...], 0.0), axis=-1, keepdims=True)
        base = jnp.where(slot_lane == k, before, base)
    slot_ref[...] = (base + rank_ref[...]).astype(jnp.int32)


def _slots(eidx, rank, cnt):
    n_rows = eidx.shape[0]
    rows = lambda i: (i, 0)
    return pl.pallas_call(
        _slots_kernel,
        grid=(n_rows // TM,),
        in_specs=[pl.BlockSpec((TM, TOP_K), rows), pl.BlockSpec((TM, TOP_K), rows),
                  pl.BlockSpec(cnt.shape, lambda i: (0, 0))],
        out_specs=pl.BlockSpec((TM, TOP_K), rows),
        out_shape=jax.ShapeDtypeStruct((n_rows, TOP_K), jnp.int32),
        compiler_params=_cparams(("arbitrary",)),
        name="slots",
    )(eidx, rank, cnt)


SC_WINDOW = 128


def _sc_mesh():
    return plsc.VectorSubcoreMesh(core_axis_name="core", subcore_axis_name="subcore")


def _sc_dispatch(rows, slot_t, n_out):
    n_rows, width = rows.shape
    n_chunks = n_rows // SC_WINDOW
    info = plsc.get_sparse_core_info()
    n_workers = info.num_cores * info.num_subcores

    @functools.partial(
        pl.kernel, mesh=_sc_mesh(),
        out_type=jax.ShapeDtypeStruct((n_out, width), rows.dtype),
        scratch_types=[pltpu.VMEM((TOP_K, SC_WINDOW), jnp.int32), pltpu.VMEM((SC_WINDOW, width), rows.dtype)],
        name="moe_dispatch")
    def run(rows_hbm, idx_hbm, out_hbm, idx_v, rows_v):
        wid = lax.axis_index("subcore") * info.num_cores + lax.axis_index("core")

        @pl.loop(wid, n_chunks, step=n_workers)
        def _(c):
            r0 = pl.multiple_of(c * SC_WINDOW, SC_WINDOW)
            pltpu.sync_copy(idx_hbm.at[:, pl.ds(r0, SC_WINDOW)], idx_v)
            pltpu.sync_copy(rows_hbm.at[pl.ds(r0, SC_WINDOW)], rows_v)
            for k in range(TOP_K):
                pltpu.sync_copy(rows_v, out_hbm.at[idx_v.at[k]])

    return run(rows, slot_t)


def _sc_collect(rows, slot_t):
    n_picks, n_rows = slot_t.shape
    width = rows.shape[1]
    n_chunks = n_rows // SC_WINDOW
    info = plsc.get_sparse_core_info()
    n_workers = info.num_cores * info.num_subcores

    @functools.partial(
        pl.kernel, mesh=_sc_mesh(),
        out_type=jax.ShapeDtypeStruct((n_picks, n_rows, width), rows.dtype),
        scratch_types=[pltpu.VMEM((TOP_K, SC_WINDOW), jnp.int32), pltpu.VMEM((SC_WINDOW, width), rows.dtype)],
        name="moe_collect")
    def run(rows_hbm, idx_hbm, out_hbm, idx_v, rows_v):
        wid = lax.axis_index("subcore") * info.num_cores + lax.axis_index("core")

        @pl.loop(wid, n_chunks, step=n_workers)
        def _(c):
            r0 = pl.multiple_of(c * SC_WINDOW, SC_WINDOW)
            pltpu.sync_copy(idx_hbm.at[:, pl.ds(r0, SC_WINDOW)], idx_v)
            for k in range(TOP_K):
                pltpu.sync_copy(rows_hbm.at[idx_v.at[k]], rows_v)
                pltpu.sync_copy(rows_v, out_hbm.at[k, pl.ds(r0, SC_WINDOW)])

    return run(rows, slot_t)


EXPERT_TILE = 256


def _work_items(cnt, n_slots):
    counts = cnt[0].astype(jnp.int32)
    ends = jnp.cumsum(counts)
    n_tiles = n_slots // EXPERT_TILE
    bounds = jnp.sort(jnp.concatenate([jnp.arange(n_tiles, dtype=jnp.int32) * EXPERT_TILE, ends - counts]))
    nxt = jnp.concatenate([bounds[1:], jnp.array([n_slots], jnp.int32)])
    tile = jnp.minimum(bounds // EXPERT_TILE, n_tiles - 1)
    expert = jnp.minimum(jnp.searchsorted(ends, bounds, side="right"), N_EXPERTS - 1).astype(jnp.int32)
    return tile, expert, bounds - tile * EXPERT_TILE, nxt - tile * EXPERT_TILE


def _experts_kernel(tile_ref, exp_ref, lo_ref, hi_ref, xs_ref, wg_ref, wu_ref, wd_ref, ys_ref,
                    acc_ref, wgb_ref, wub_ref, wdb_ref):
    i = pl.program_id(0)
    lo, hi = lo_ref[i], hi_ref[i]

    @pl.when((i == 0) | (exp_ref[i] != exp_ref[jnp.maximum(i - 1, 0)]))
    def _():
        wgb_ref[...] = wg_ref[...].astype(BF16)
        wub_ref[...] = wu_ref[...].astype(BF16)
        wdb_ref[...] = wd_ref[...].astype(BF16)

    @pl.when(lo == 0)
    def _():
        acc_ref[...] = jnp.zeros_like(acc_ref)

    @pl.when(hi > lo)
    def _():
        x_lo, x_hi = _unpack_halves(xs_ref[...])
        x_lo, x_hi = x_lo.astype(BF16), x_hi.astype(BF16)
        n = x_lo.shape[1]
        a = _dot(x_lo, wgb_ref[0:n, :]) + _dot(x_hi, wgb_ref[n:, :])
        u = _dot(x_lo, wub_ref[0:n, :]) + _dot(x_hi, wub_ref[n:, :])
        y = _dot(((a * jax.nn.sigmoid(a)) * u).astype(BF16), wdb_ref[...])
        row = lax.broadcasted_iota(jnp.int32, y.shape, 0)
        acc_ref[...] += jnp.where((row >= lo) & (row < hi), y, 0.0)

    ys_ref[...] = _pack_halves(acc_ref[...])


def _experts(xs, items, layer, exp_wg, exp_wu, exp_wd):
    n_slots, half = xs.shape
    d, hid = exp_wg.shape[-2:]
    tile, expert, lo, hi = items
    grid_spec = pltpu.PrefetchScalarGridSpec(
        num_scalar_prefetch=4,
        grid=(tile.shape[0],),
        in_specs=[pl.BlockSpec((EXPERT_TILE, half), lambda i, t, e, lo, hi: (t[i], 0)),
                  pl.BlockSpec((None, None, d, hid), lambda i, t, e, lo, hi: (layer, e[i], 0, 0)),
                  pl.BlockSpec((None, None, d, hid), lambda i, t, e, lo, hi: (layer, e[i], 0, 0)),
                  pl.BlockSpec((None, None, hid, d), lambda i, t, e, lo, hi: (layer, e[i], 0, 0))],
        out_specs=pl.BlockSpec((EXPERT_TILE, half), lambda i, t, e, lo, hi: (t[i], 0)),
        scratch_shapes=[pltpu.VMEM((EXPERT_TILE, d), F32), pltpu.VMEM((d, hid), BF16),
                        pltpu.VMEM((d, hid), BF16), pltpu.VMEM((hid, d), BF16)])
    return pl.pallas_call(
        _experts_kernel,
        grid_spec=grid_spec,
        out_shape=jax.ShapeDtypeStruct((n_slots, half), jnp.uint32),
        compiler_params=_cparams(("arbitrary",)),
        name="experts",
    )(tile, expert, lo, hi, xs, exp_wg, exp_wu, exp_wd)


def _moe_out_kernel(yg_ref, w_ref, v_ref, sg_ref, su_ref, sd_ref, h1_ref, mod_ref, gpost_ref, o_ref,
                    *, tiles_per_batch, n_batch, d):
    i = pl.program_id(0)
    x = v_ref[...]
    a = _dot(x, sg_ref[...])
    f = _dot(((a * jax.nn.sigmoid(a)) * _dot(x, su_ref[...])).astype(BF16), sd_ref[...])
    n = d // 2
    f_lo, f_hi = f[:, :n], f[:, n:]
    w = w_ref[...]
    for k in range(TOP_K):
        y_lo, y_hi = _unpack_halves(yg_ref[k])
        wk = w[:, k:k + 1]
        f_lo = f_lo + wk * y_lo
        f_hi = f_hi + wk * y_hi
    f = jnp.concatenate([f_lo, f_hi], axis=1)
    bi = jnp.minimum(i // tiles_per_batch, n_batch)
    g2 = mod_ref[pl.ds(bi, 1), 5 * d:6 * d]
    o_ref[...] = h1_ref[...] + g2 * _rms(f, gpost_ref[...])


def _moe_out(yg, w, v, sg, su, sd, h1, mods, gpost, *, n_batch, seq):
    n_rows, d = v.shape
    rows = lambda i: (i, 0)
    const = lambda i: (0, 0)
    return pl.pallas_call(
        functools.partial(_moe_out_kernel, tiles_per_batch=seq // TM, n_batch=n_batch, d=d),
        grid=(n_rows // TM,),
        in_specs=[pl.BlockSpec((TOP_K, TM, d // 2), lambda i: (0, i, 0)), pl.BlockSpec((TM, TOP_K), rows),
                  pl.BlockSpec((TM, d), rows),
                  pl.BlockSpec(sg.shape, const), pl.BlockSpec(su.shape, const), pl.BlockSpec(sd.shape, const),
                  pl.BlockSpec((TM, d), rows), pl.BlockSpec(mods.shape, const), pl.BlockSpec((1, d), const)],
        out_specs=pl.BlockSpec((TM, d), rows),
        out_shape=jax.ShapeDtypeStruct((n_rows, d), F32),
        compiler_params=_cparams(("arbitrary",)),
        name="moe_out",
    )(yg, w, v, sg, su, sd, h1, mods, gpost)


def _moe(v, vp, layer, rw, rb, exp_wg, exp_wu, exp_wd, sg, su, sd, h1, mods, gpost, *, n_batch, seq):
    n_rows = v.shape[0]
    eidx, rank, w, cnt = _router(v, rw, rb)
    slot_t = _slots(eidx, rank, cnt).T
    n_slots = n_rows * TOP_K
    xs = _sc_dispatch(vp, slot_t, n_slots)
    ys = _experts(xs, _work_items(cnt, n_slots), layer, exp_wg, exp_wu, exp_wd)
    yg = _sc_collect(ys, slot_t)
    return _moe_out(yg, w, v, sg, su, sd, h1, mods, gpost, n_batch=n_batch, seq=seq)


def _rope_tables(seq):
    rows = seq // GRID_W
    row_id = jnp.repeat(jnp.arange(rows, dtype=F32), GRID_W)
    col_id = jnp.tile(jnp.arange(GRID_W, dtype=F32), rows)

    def tables(rot_dim):
        axis_dim = rot_dim // 2
        inv_freq = ROPE_BASE ** (-jnp.arange(0, axis_dim, 2, dtype=F32) / axis_dim)
        ang_r = row_id[:, None] * inv_freq[None, :]
        ang_c = col_id[:, None] * inv_freq[None, :]
        cos = jnp.concatenate([jnp.cos(ang_r), jnp.cos(ang_r), jnp.cos(ang_c), jnp.cos(ang_c)], axis=1)
        sin = jnp.concatenate([-jnp.sin(ang_r), jnp.sin(ang_r), -jnp.sin(ang_c), jnp.sin(ang_c)], axis=1)
        return cos, sin

    cos64, sin64 = tables(GQA_DIM)
    cos32, sin32 = tables(MLA_ROPE)
    ones = jnp.ones((seq, MLA_NOPE), F32)
    pad = LANES - MLA_NOPE - MLA_ROPE
    cospe = jnp.concatenate([ones, cos32, jnp.ones((seq, pad), F32)], axis=1)
    sinpe = jnp.concatenate([0 * ones, sin32, jnp.zeros((seq, pad), F32)], axis=1)
    tab = jnp.concatenate([cos64, cos64, sin64, sin64, cospe, sinpe], axis=1)
    ident = jnp.concatenate([jnp.ones((TM, LANES), F32), jnp.zeros((TM, LANES), F32),
                             jnp.ones((TM, LANES), F32), jnp.zeros((TM, LANES), F32)], axis=1)
    return jnp.concatenate([tab, ident], axis=0)


def _pack_w_in(w):
    d = w.shape[0]
    sizes = (MLA_Q_LORA, MLA_KV_LORA, MLA_ROPE, GQA_HEADS * GQA_DIM, GQA_KV_HEADS * GQA_DIM,
             GQA_KV_HEADS * GQA_DIM, RET_HEADS * RET_QK, RET_HEADS * RET_QK, RET_HEADS * RET_V,
             RET_HEADS * RET_V, 3 * d)
    offs, parts = 0, []
    for s in sizes:
        parts.append(w[:, offs:offs + s])
        offs += s
    cq, ckv, kpe, gq, gk, gv, rq, rk, rv, rg, gates = parts

    def twice(m):
        heads = [m[:, i * GQA_DIM:(i + 1) * GQA_DIM] for i in range(GQA_KV_HEADS)]
        return jnp.concatenate([hh for hd in heads for hh in (hd, hd)], axis=1)

    kpe_slab = jnp.concatenate([jnp.zeros((d, MLA_NOPE), F32), kpe,
                                jnp.zeros((d, LANES - MLA_NOPE - MLA_ROPE), F32)], axis=1)
    packed = jnp.concatenate([cq, ckv, kpe_slab, gq * GQA_DIM ** -0.5, twice(gk), twice(gv),
                              rq, rk * RET_QK ** -0.5, rv, rg, gates], axis=1)
    assert packed.shape[1] == W_COLS
    return packed.astype(BF16)


def _pack_mla_up(w_uq, w_ukv):
    r = w_uq.shape[0]
    dq = MLA_NOPE + MLA_ROPE
    wq = jnp.pad(w_uq.reshape(r, MLA_HEADS, dq), ((0, 0), (0, 0), (0, LANES - dq))).reshape(r, MLA_HEADS * LANES)
    kv = w_ukv.reshape(r, MLA_HEADS, MLA_NOPE + MLA_V)
    wk = jnp.pad(kv[:, :, :MLA_NOPE], ((0, 0), (0, 0), (0, LANES - MLA_NOPE))).reshape(r, MLA_HEADS * LANES)
    wv = kv[:, :, MLA_NOPE:].reshape(r, MLA_HEADS * MLA_V)
    return wq.astype(BF16), wk.astype(BF16), wv.astype(BF16)


def kernel(x, c, ctx, c_ctx, ada_w, ada_b, norm_mix_pre, norm_mix_post, norm_ffn_pre, norm_ffn_post, w_in, mla_q_norm, mla_w_uq, mla_kv_norm, mla_w_ukv, gqa_sink, ret_decay_fwd, ret_decay_bwd, w_br_mla, w_br_gqa, w_br_ret, w_out, router_w, router_bias, exp_w_gate, exp_w_up, exp_w_down, shared_w_gate, shared_w_up, shared_w_down):
    n_batch, seq, d = x.shape
    n_ctx = ctx.shape[1]
    depth = ada_w.shape[0]
    n_lat_rows = n_batch * seq
    assert seq % TM == 0 and n_ctx == TM and n_batch < MOD_ROWS and seq % GRID_W == 0

    cond = jnp.zeros((MOD_ROWS, d), F32).at[:n_batch].set(c).at[n_batch].set(c_ctx)
    mods_all = _adaln(cond, ada_w, ada_b)
    rope = _rope_tables(seq)
    h = jnp.concatenate([x.reshape(n_lat_rows, d), ctx.reshape(n_batch * n_ctx, d)], axis=0)
    row = lambda p: p.reshape(1, -1)
    dims = dict(n_batch=n_batch, seq=seq)

    for l in range(depth):
        last = l == depth - 1
        mods = mods_all[l]
        wq, wk, wv = _pack_mla_up(mla_w_uq[l], mla_w_ukv[l])
        mq, mk, mv, gqa, ret, gates = _inproj(h, mods, row(norm_mix_pre[l]), _pack_w_in(w_in[l]), rope,
                                              row(mla_q_norm[l]), row(mla_kv_norm[l]), wq, wk, wv, **dims)
        a = _mla_attention(mq, mk, mv, ctx=n_ctx, with_ctx_queries=not last, **dims)
        sink_tab = jnp.broadcast_to(gqa_sink[l].astype(F32)[:, None], (GQA_HEADS, LANES))
        w = _window_attention(gqa, sink_tab, ctx=n_ctx, with_ctx_queries=not last, **dims)
        lg = jnp.concatenate([jax.nn.log_sigmoid(ret_decay_fwd[l].astype(F32)),
                              jax.nn.log_sigmoid(ret_decay_bwd[l].astype(F32))])
        o_f, o_b = _retention(ret, jnp.broadcast_to(lg[:, None], (2 * RET_HEADS, LANES)), ctx=n_ctx, **dims)
        n_rows = n_lat_rows if last else h.shape[0]
        h1, v, vp = _merge(a, w, o_f, o_b, ret, gates, h, mods, row(norm_mix_post[l]), row(norm_ffn_pre[l]),
                           w_br_mla[l].astype(BF16), w_br_gqa[l].astype(BF16), w_br_ret[l].astype(BF16),
                           w_out[l].astype(BF16), n_rows=n_rows, **dims)
        h = _moe(v, vp, l, router_w[l].astype(BF16), row(router_bias[l].astype(F32)),
                 exp_w_gate, exp_w_up, exp_w_down, shared_w_gate[l].astype(BF16),
                 shared_w_up[l].astype(BF16), shared_w_down[l].astype(BF16), h1, mods,
                 row(norm_ffn_post[l]), **dims)
    return h[:n_lat_rows].reshape(n_batch, seq, d)
```

```python
import functools

import jax
import jax.numpy as jnp
from jax import lax
from jax.experimental import pallas as pl
from jax.experimental.pallas import tpu as pltpu
from jax.experimental.pallas import tpu_sc as plsc

F32 = jnp.float32
BF16 = jnp.bfloat16

GRID_W = 64
ROPE_BASE = 10000.0
NORM_EPS = 1e-6
NEG_INF = -1e30
N_MOD = 6
MLA_HEADS, MLA_NOPE, MLA_ROPE, MLA_V = 8, 64, 32, 64
MLA_Q_LORA, MLA_KV_LORA = 256, 256
GQA_HEADS, GQA_KV_HEADS, GQA_DIM, WINDOW = 8, 2, 64, 128
RET_HEADS, RET_QK, RET_V, RET_CHUNK = 4, 64, 128, 128
N_EXPERTS, N_EXPERT_GROUPS, TOPK_GROUPS, TOP_K = 64, 8, 4, 8
EXPERTS_PER_GROUP = N_EXPERTS // N_EXPERT_GROUPS
ROUTED_SCALE = 2.5

LANES = 128
TM = 256
MOD_ROWS = 8
V7X_VMEM_LIMIT = 56 * 1024 * 1024

C_CQ, C_CKV, C_KPE = 0, 256, 512
C_G = 640
C_R = 1664
C_GATE = 3200
W_COLS = 6272


def _cparams(sem):
    return pltpu.CompilerParams(dimension_semantics=sem, vmem_limit_bytes=V7X_VMEM_LIMIT)


def _rms(x, g):
    return x * lax.rsqrt(jnp.mean(x * x, axis=-1, keepdims=True) + NORM_EPS) * g


def _dot(a, b):
    return jnp.dot(a, b, preferred_element_type=F32)


def _dot_nt(a, b):
    return lax.dot_general(a, b, (((1,), (1,)), ((), ())), preferred_element_type=F32)


def _dot_tn(a, b):
    return lax.dot_general(a, b, (((0,), (0,)), ((), ())), preferred_element_type=F32)


def _rope(x, cos, sin, half):
    n = x.shape[-1]
    reps = n // LANES
    if reps > 1:
        cos = jnp.concatenate([cos] * reps, axis=1)
        sin = jnp.concatenate([sin] * reps, axis=1)
    lane = lax.broadcasted_iota(jnp.int32, x.shape, 1)
    up = pltpu.roll(x, half, 1)
    dn = pltpu.roll(x, n - half, 1)
    partner = jnp.where((lane & (2 * half - 1)) < half, dn, up)
    return x * cos + partner * sin


def _lane_lo(shape):
    return (lax.broadcasted_iota(jnp.int32, shape, 1) & (LANES - 1)) < (LANES // 2)


def _pack_halves(x):
    n = x.shape[1] // 2
    bits = lambda t: lax.bitcast_convert_type(t.astype(BF16).astype(F32), jnp.uint32)
    return (bits(x[:, :n]) >> 16) | bits(x[:, n:])


def _unpack_halves(p):
    lo = lax.bitcast_convert_type(p << 16, F32)
    hi = lax.bitcast_convert_type(p & jnp.uint32(0xFFFF0000), F32)
    return lo, hi


def _ada_kernel(c_ref, w_ref, b_ref, o_ref):
    c = c_ref[...]
    s = c * jax.nn.sigmoid(c)
    o_ref[...] = _dot(s.astype(BF16), w_ref[...].astype(BF16)) + b_ref[...]


def _adaln(cond, ada_w, ada_b):
    n_layers, d, n = ada_w.shape
    tn = 1024
    return pl.pallas_call(
        _ada_kernel,
        grid=(n_layers, n // tn),
        in_specs=[pl.BlockSpec((MOD_ROWS, d), lambda l, j: (0, 0)),
                  pl.BlockSpec((None, d, tn), lambda l, j: (l, 0, j)),
                  pl.BlockSpec((None, 1, tn), lambda l, j: (l, 0, j))],
        out_specs=pl.BlockSpec((None, MOD_ROWS, tn), lambda l, j: (l, 0, j)),
        out_shape=jax.ShapeDtypeStruct((n_layers, MOD_ROWS, n), F32),
        compiler_params=_cparams(("arbitrary", "arbitrary")),
        name="adaln",
    )(cond, ada_w, ada_b.reshape(n_layers, 1, n))


def _inproj_kernel(h_ref, mod_ref, gpre_ref, w_ref, rope_ref, qn_ref, kvn_ref, wuq_ref, wuk_ref, wuv_ref,
                   mq_ref, mk_ref, mv_ref, gqa_ref, ret_ref, gate_ref, *, tiles_per_batch, n_batch, d):
    i = pl.program_id(0)
    bi = jnp.minimum(i // tiles_per_batch, n_batch)
    sh = mod_ref[pl.ds(bi, 1), 0:d]
    sc = mod_ref[pl.ds(bi, 1), d:2 * d]
    u = (_rms(h_ref[...], gpre_ref[...]) * (1.0 + sc) + sh).astype(BF16)

    cos64 = rope_ref[:, 0:LANES]
    sin64 = rope_ref[:, LANES:2 * LANES]
    cospe = rope_ref[:, 2 * LANES:3 * LANES]
    sinpe = rope_ref[:, 3 * LANES:4 * LANES]

    c = _dot(u, w_ref[:, C_CQ:C_G])
    kpe = _rope(c[:, C_KPE:C_G], cospe, sinpe, MLA_ROPE // 4)
    qn = _rms(c[:, C_CQ:C_CKV], qn_ref[...]).astype(BF16)
    q = _rope(_dot(qn, wuq_ref[...]), cospe, sinpe, MLA_ROPE // 4)
    mq_ref[...] = (q * (MLA_NOPE + MLA_ROPE) ** -0.5).astype(mq_ref.dtype)
    kvn = _rms(c[:, C_CKV:C_KPE], kvn_ref[...]).astype(BF16)
    k = _dot(kvn, wuk_ref[...]) + jnp.concatenate([kpe] * MLA_HEADS, axis=1)
    mk_ref[...] = k.astype(mk_ref.dtype)
    mv_ref[...] = _dot(kvn, wuv_ref[...]).astype(mv_ref.dtype)

    g = _dot(u, w_ref[:, C_G:C_R])
    n_qk = GQA_HEADS * GQA_DIM + 2 * GQA_KV_HEADS * GQA_DIM
    gqa_ref[:, 0:n_qk] = _rope(g[:, 0:n_qk], cos64, sin64, GQA_DIM // 4).astype(gqa_ref.dtype)
    gqa_ref[:, n_qk:] = g[:, n_qk:].astype(gqa_ref.dtype)

    r = _dot(u, w_ref[:, C_R:C_GATE])
    n_qk = 2 * RET_HEADS * RET_QK
    ret_ref[:, 0:n_qk] = _rope(r[:, 0:n_qk], cos64, sin64, RET_QK // 4)
    ret_ref[:, n_qk:] = r[:, n_qk:]

    gate_ref[...] = _dot(u, w_ref[:, C_GATE:W_COLS]).astype(gate_ref.dtype)


def _inproj(h, mods, gpre, w_all, rope, qn, kvn, wuq, wuk, wuv, *, n_batch, seq):
    t, d = h.shape
    tiles_per_batch = seq // TM
    n_lat_tiles = n_batch * tiles_per_batch
    const = lambda i: (0, 0)
    rows = lambda i: (i, 0)
    rope_idx = lambda i: (jnp.where(i < n_lat_tiles, i % tiles_per_batch, tiles_per_batch), 0)
    hq = MLA_HEADS * LANES
    outs = [jax.ShapeDtypeStruct((t, hq), BF16), jax.ShapeDtypeStruct((t, hq), BF16),
            jax.ShapeDtypeStruct((t, MLA_HEADS * MLA_V), BF16),
            jax.ShapeDtypeStruct((t, C_R - C_G), BF16),
            jax.ShapeDtypeStruct((t, C_GATE - C_R), F32),
            jax.ShapeDtypeStruct((t, W_COLS - C_GATE), BF16)]
    return pl.pallas_call(
        functools.partial(_inproj_kernel, tiles_per_batch=tiles_per_batch, n_batch=n_batch, d=d),
        grid=(t // TM,),
        in_specs=[pl.BlockSpec((TM, d), rows),
                  pl.BlockSpec(mods.shape, const),
                  pl.BlockSpec((1, d), const),
                  pl.BlockSpec(w_all.shape, const),
                  pl.BlockSpec((TM, 4 * LANES), rope_idx),
                  pl.BlockSpec(qn.shape, const), pl.BlockSpec(kvn.shape, const),
                  pl.BlockSpec(wuq.shape, const), pl.BlockSpec(wuk.shape, const), pl.BlockSpec(wuv.shape, const)],
        out_specs=[pl.BlockSpec((TM, o.shape[1]), rows) for o in outs],
        out_shape=outs,
        compiler_params=_cparams(("arbitrary",)),
        name="inproj",
    )(h, mods, gpre, w_all, rope, qn, kvn, wuq, wuk, wuv)


def _mla_kernel(q_ref, kl_ref, kc_ref, vl_ref, vc_ref, o_ref, *, n_lat_tiles, has_ctx_tile):
    def body(with_lat):
        outs = []
        for h in range(2):
            sl = slice(h * LANES, (h + 1) * LANES)
            qh = q_ref[:, sl]
            s_c = _dot_nt(qh, kc_ref[:, sl])
            m = jnp.max(s_c, axis=-1, keepdims=True)
            if with_lat:
                s_l = _dot_nt(qh, kl_ref[:, sl])
                m = jnp.maximum(m, jnp.max(s_l, axis=-1, keepdims=True))
            p_c = jnp.exp(s_c - m)
            den = jnp.sum(p_c, axis=-1, keepdims=True)
            o = _dot(p_c.astype(BF16), vc_ref[...])
            if with_lat:
                p_l = jnp.exp(s_l - m)
                den = den + jnp.sum(p_l, axis=-1, keepdims=True)
                o = o + _dot(p_l.astype(BF16), vl_ref[...])
            outs.append(o / den)
        o_ref[...] = jnp.where(_lane_lo(outs[0].shape), outs[0], outs[1]).astype(o_ref.dtype)

    if has_ctx_tile:
        i = pl.program_id(2)
        pl.when(i < n_lat_tiles)(lambda: body(True))
        pl.when(i == n_lat_tiles)(lambda: body(False))
    else:
        body(True)


def _mla_attention(mq, mk, mv, *, n_batch, seq, ctx, with_ctx_queries):
    t = mq.shape[0]
    nq = seq // TM
    n_lat_blocks = n_batch * nq
    ctx_blk0 = n_batch * seq // ctx
    n_pairs = MLA_HEADS // 2
    assert ctx == TM
    q_idx = lambda b, p, i: (jnp.where(i < nq, b * nq + i, n_lat_blocks + b), p)
    lat_idx = lambda b, p, i: (b, p)
    ctx_idx = lambda b, p, i: (ctx_blk0 + b, p)
    n_rows = t if with_ctx_queries else n_batch * seq
    return pl.pallas_call(
        functools.partial(_mla_kernel, n_lat_tiles=nq, has_ctx_tile=with_ctx_queries),
        grid=(n_batch, n_pairs, nq + (1 if with_ctx_queries else 0)),
        in_specs=[pl.BlockSpec((TM, 2 * LANES), q_idx),
                  pl.BlockSpec((seq, 2 * LANES), lat_idx),
                  pl.BlockSpec((ctx, 2 * LANES), ctx_idx),
                  pl.BlockSpec((seq, LANES), lat_idx),
                  pl.BlockSpec((ctx, LANES), ctx_idx)],
        out_specs=pl.BlockSpec((TM, LANES), q_idx),
        out_shape=jax.ShapeDtypeStruct((n_rows, MLA_HEADS * MLA_V), BF16),
        compiler_params=_cparams(("arbitrary", "arbitrary", "arbitrary")),
        name="mla_attn",
    )(mq, mk, mk, mv, mv)


def _win_kernel(q_ref, kp_ref, kcur_ref, kn_ref, vp_ref, vcur_ref, vn_ref, kc_ref, vc_ref, sink_ref, o_ref,
                *, n_lat_tiles, seq, has_ctx_tile):
    i = pl.program_id(1)
    tq = q_ref.shape[0]
    group = GQA_HEADS // GQA_KV_HEADS

    def body(with_lat):
        if with_lat:
            n_win = tq + 2 * WINDOW
            q_pos = i * tq + lax.broadcasted_iota(jnp.int32, (tq, n_win), 0)
            k_pos = i * tq - WINDOW + lax.broadcasted_iota(jnp.int32, (tq, n_win), 1)
            valid = (jnp.abs(q_pos - k_pos) <= WINDOW) & (k_pos >= 0) & (k_pos < seq)
        for kv in range(GQA_KV_HEADS):
            sl = slice(kv * LANES, (kv + 1) * LANES)
            k_ctx, v_ctx = kc_ref[:, sl], vc_ref[:, sl]
            if with_lat:
                k_win = jnp.concatenate([kp_ref[:, sl], kcur_ref[:, sl], kn_ref[:, sl]], axis=0)
                v_win = jnp.concatenate([vp_ref[:, sl], vcur_ref[:, sl], vn_ref[:, sl]], axis=0)
            for pr in range(group // 2):
                col = slice((kv * (group // 2) + pr) * LANES, (kv * (group // 2) + pr + 1) * LANES)
                qp = q_ref[:, col]
                lo = _lane_lo(qp.shape)
                halves = []
                for half in range(2):
                    hd = kv * group + pr * 2 + half
                    qm = jnp.where(lo if half == 0 else jnp.logical_not(lo), qp, jnp.zeros_like(qp))
                    sink = sink_ref[hd:hd + 1, 0:1]
                    s_c = _dot_nt(qm, k_ctx)
                    m = jnp.maximum(jnp.max(s_c, axis=-1, keepdims=True), sink)
                    if with_lat:
                        s_l = jnp.where(valid, _dot_nt(qm, k_win), NEG_INF)
                        m = jnp.maximum(m, jnp.max(s_l, axis=-1, keepdims=True))
                    p_c = jnp.exp(s_c - m)
                    den = jnp.sum(p_c, axis=-1, keepdims=True) + jnp.exp(sink - m)
                    o = _dot(p_c.astype(BF16), v_ctx)
                    if with_lat:
                        p_l = jnp.exp(s_l - m)
                        den = den + jnp.sum(p_l, axis=-1, keepdims=True)
                        o = o + _dot(p_l.astype(BF16), v_win)
                    halves.append(o / den)
                o_ref[:, col] = jnp.where(lo, halves[0], halves[1]).astype(o_ref.dtype)

    if has_ctx_tile:
        pl.when(i < n_lat_tiles)(lambda: body(True))
        pl.when(i == n_lat_tiles)(lambda: body(False))
    else:
        body(True)


def _window_attention(gqa, sink_tab, *, n_batch, seq, ctx, with_ctx_queries):
    t = gqa.shape[0]
    nq = seq // TM
    n_lat_blocks = n_batch * nq
    per_tile = TM // WINDOW
    n_win_blocks = seq // WINDOW
    assert ctx == TM
    ctx_blk0 = n_batch * seq // ctx
    nqk = GQA_HEADS * GQA_DIM
    kw = 2 * GQA_KV_HEADS * GQA_DIM
    k_col, v_col = nqk // kw, nqk // kw + 1
    q_idx = lambda b, i: (jnp.where(i < nq, b * nq + i, n_lat_blocks + b), 0)
    cur = lambda col: (lambda b, i: (b * nq + jnp.minimum(i, nq - 1), col))
    prev = lambda col: (lambda b, i: (b * n_win_blocks + jnp.clip(per_tile * i - 1, 0, n_win_blocks - 1), col))
    nxt = lambda col: (lambda b, i: (b * n_win_blocks + jnp.clip(per_tile * (i + 1), 0, n_win_blocks - 1), col))
    cidx = lambda col: (lambda b, i: (ctx_blk0 + b, col))
    n_rows = t if with_ctx_queries else n_batch * seq
    return pl.pallas_call(
        functools.partial(_win_kernel, n_lat_tiles=nq, seq=seq, has_ctx_tile=with_ctx_queries),
        grid=(n_batch, nq + (1 if with_ctx_queries else 0)),
        in_specs=[pl.BlockSpec((TM, nqk), q_idx),
                  pl.BlockSpec((WINDOW, kw), prev(k_col)), pl.BlockSpec((TM, kw), cur(k_col)),
                  pl.BlockSpec((WINDOW, kw), nxt(k_col)),
                  pl.BlockSpec((WINDOW, kw), prev(v_col)), pl.BlockSpec((TM, kw), cur(v_col)),
                  pl.BlockSpec((WINDOW, kw), nxt(v_col)),
                  pl.BlockSpec((ctx, kw), cidx(k_col)), pl.BlockSpec((ctx, kw), cidx(v_col)),
                  pl.BlockSpec(sink_tab.shape, lambda b, i: (0, 0))],
        out_specs=pl.BlockSpec((TM, nqk), q_idx),
        out_shape=jax.ShapeDtypeStruct((n_rows, nqk), BF16),
        compiler_params=_cparams(("arbitrary", "arbitrary")),
        name="win_attn",
    )(gqa, gqa, gqa, gqa, gqa, gqa, gqa, gqa, gqa, sink_tab)


def _ret_kernel(f_ref, b_ref, lg_ref, of_ref, ob_ref, sf_ref, sb_ref):
    @pl.when(pl.program_id(1) == 0)
    def _():
        sf_ref[...] = jnp.zeros_like(sf_ref)
        sb_ref[...] = jnp.zeros_like(sb_ref)

    L = f_ref.shape[0]
    ii = lax.broadcasted_iota(jnp.int32, (L, L), 0)
    jj = lax.broadcasted_iota(jnp.int32, (L, L), 1)
    row = lax.broadcasted_iota(jnp.int32, (L, LANES), 0).astype(F32)
    lo = _lane_lo((L, LANES))
    srow_lo = lax.broadcasted_iota(jnp.int32, (LANES, LANES), 0) < RET_QK
    nq = RET_HEADS * RET_QK

    def direction(x_ref, o_ref, s_ref, lg_row0, forward):
        for pr in range(RET_HEADS // 2):
            lg = [lg_ref[lg_row0 + 2 * pr + e:lg_row0 + 2 * pr + e + 1, :] for e in range(2)]
            lg_lane = jnp.where(lo, lg[0], lg[1])
            q = x_ref[:, pr * LANES:(pr + 1) * LANES]
            k = x_ref[:, nq + pr * LANES:nq + (pr + 1) * LANES]
            if forward:
                qd = q * jnp.exp(lg_lane * (row + 1.0))
                kd = k * jnp.exp(lg_lane * (L - 1.0 - row))
                dist = ii - jj
            else:
                qd = q * jnp.exp(lg_lane * (L - row))
                kd = k * jnp.exp(lg_lane * row)
                dist = jj - ii
            distf = jnp.maximum(dist, 0).astype(F32)
            kb, kdb = k.astype(BF16), kd.astype(BF16)
            state = s_ref[pr]
            state_b = state.astype(BF16)
            upd = []
            for e in range(2):
                hd = 2 * pr + e
                keep = lo if e == 0 else jnp.logical_not(lo)
                v = x_ref[:, 2 * nq + hd * RET_V:2 * nq + (hd + 1) * RET_V].astype(BF16)
                inner = jnp.where(dist >= 0, jnp.exp(lg[e][:, 0:1] * distf), 0.0)
                attn = _dot_nt(jnp.where(keep, q, 0.0).astype(BF16), kb) * inner
                o = _dot(attn.astype(BF16), v) + _dot(jnp.where(keep, qd, 0.0).astype(BF16), state_b)
                o_ref[:, hd * RET_V:(hd + 1) * RET_V] = o
                upd.append(_dot_tn(kdb, v))
            chunk_decay = jnp.where(srow_lo, jnp.exp(lg[0] * float(L)), jnp.exp(lg[1] * float(L)))
            s_ref[pr] = state * chunk_decay + jnp.where(srow_lo, upd[0], upd[1])

    direction(f_ref, of_ref, sf_ref, 0, True)
    direction(b_ref, ob_ref, sb_ref, RET_HEADS, False)


def _retention(ret, lg_tab, *, n_batch, seq, ctx):
    t = ret.shape[0]
    L = RET_CHUNK
    n_lat, n_ctx = seq // L, ctx // L
    ctx0 = n_batch * n_lat
    width = 2 * RET_HEADS * RET_QK + RET_HEADS * RET_V
    fwd = lambda b, s: (jnp.where(s < n_ctx, ctx0 + b * n_ctx + s, b * n_lat + s - n_ctx), 0)
    bwd = lambda b, s: (jnp.where(s < n_ctx, ctx0 + b * n_ctx + n_ctx - 1 - s, b * n_lat + n_lat - 1 - (s - n_ctx)), 0)
    out = jax.ShapeDtypeStruct((t, RET_HEADS * RET_V), F32)
    return pl.pallas_call(
        _ret_kernel,
        grid=(n_batch, n_lat + n_ctx),
        in_specs=[pl.BlockSpec((L, width), fwd), pl.BlockSpec((L, width), bwd),
                  pl.BlockSpec(lg_tab.shape, lambda b, s: (0, 0))],
        out_specs=[pl.BlockSpec((L, RET_HEADS * RET_V), fwd), pl.BlockSpec((L, RET_HEADS * RET_V), bwd)],
        out_shape=[out, out],
        scratch_shapes=[pltpu.VMEM((RET_HEADS // 2, LANES, RET_V), F32),
                        pltpu.VMEM((RET_HEADS // 2, LANES, RET_V), F32)],
        compiler_params=_cparams(("arbitrary", "arbitrary")),
        name="retention",
    )(ret, ret, lg_tab)


def _merge_kernel(a_ref, w_ref, of_ref, ob_ref, rg_ref, gt_ref, h_ref, mod_ref, gpost_ref, gffn_ref,
                  wa_ref, ww_ref, wr_ref, wo_ref, h1_ref, v_ref, vp_ref, *, tiles_per_batch, n_batch, d):
    i = pl.program_id(0)
    bi = jnp.minimum(i // tiles_per_batch, n_batch)
    o = of_ref[...] + ob_ref[...]
    normed = []
    for hd in range(RET_HEADS):
        oh = o[:, hd * RET_V:(hd + 1) * RET_V]
        dev = oh - jnp.mean(oh, axis=-1, keepdims=True)
        normed.append(dev * lax.rsqrt(jnp.mean(dev * dev, axis=-1, keepdims=True) + NORM_EPS))
    g = rg_ref[...]
    r = (g * jax.nn.sigmoid(g)) * jnp.concatenate(normed, axis=1)
    y = (jax.nn.sigmoid(gt_ref[:, 0:d].astype(F32)) * _dot(a_ref[...], wa_ref[...])
         + jax.nn.sigmoid(gt_ref[:, d:2 * d].astype(F32)) * _dot(w_ref[...], ww_ref[...])
         + jax.nn.sigmoid(gt_ref[:, 2 * d:3 * d].astype(F32)) * _dot(r.astype(BF16), wr_ref[...]))
    z = _dot(y.astype(BF16), wo_ref[...])
    g1 = mod_ref[pl.ds(bi, 1), 2 * d:3 * d]
    sh2 = mod_ref[pl.ds(bi, 1), 3 * d:4 * d]
    sc2 = mod_ref[pl.ds(bi, 1), 4 * d:5 * d]
    h1 = h_ref[...] + g1 * _rms(z, gpost_ref[...])
    h1_ref[...] = h1
    v = _rms(h1, gffn_ref[...]) * (1.0 + sc2) + sh2
    v_ref[...] = v.astype(v_ref.dtype)
    vp_ref[...] = _pack_halves(v)


def _merge(a, w, o_f, o_b, ret, gates, h, mods, gpost, gffn, wa, ww, wr, wo, *, n_rows, n_batch, seq):
    d = h.shape[1]
    rows = lambda i: (i, 0)
    const = lambda i: (0, 0)
    rv = RET_HEADS * RET_V
    rg_col = (2 * RET_HEADS * RET_QK + rv) // rv
    outs = [jax.ShapeDtypeStruct((n_rows, d), F32), jax.ShapeDtypeStruct((n_rows, d), BF16),
            jax.ShapeDtypeStruct((n_rows, d // 2), jnp.uint32)]
    return pl.pallas_call(
        functools.partial(_merge_kernel, tiles_per_batch=seq // TM, n_batch=n_batch, d=d),
        grid=(n_rows // TM,),
        in_specs=[pl.BlockSpec((TM, a.shape[1]), rows), pl.BlockSpec((TM, w.shape[1]), rows),
                  pl.BlockSpec((TM, rv), rows), pl.BlockSpec((TM, rv), rows),
                  pl.BlockSpec((TM, rv), lambda i: (i, rg_col)),
                  pl.BlockSpec((TM, 3 * d), rows), pl.BlockSpec((TM, d), rows),
                  pl.BlockSpec(mods.shape, const), pl.BlockSpec((1, d), const), pl.BlockSpec((1, d), const),
                  pl.BlockSpec(wa.shape, const), pl.BlockSpec(ww.shape, const),
                  pl.BlockSpec(wr.shape, const), pl.BlockSpec(wo.shape, const)],
        out_specs=[pl.BlockSpec((TM, o.shape[1]), rows) for o in outs],
        out_shape=outs,
        compiler_params=_cparams(("arbitrary",)),
        name="merge",
    )(a, w, o_f, o_b, ret, gates, h, mods, gpost, gffn, wa, ww, wr, wo)


def _router_kernel(v_ref, rw_ref, rb_ref, eidx_ref, rank_ref, w_ref, cnt_ref, carry_ref):
    @pl.when(pl.program_id(0) == 0)
    def _():
        carry_ref[...] = jnp.zeros_like(carry_ref)

    tm = v_ref.shape[0]
    scores = jax.nn.sigmoid(_dot_nt(rw_ref[...], v_ref[...]))
    sel = scores + rb_ref[...]
    neg = -jnp.inf
    n_grp, per = N_EXPERT_GROUPS, EXPERTS_PER_GROUP

    sel3 = sel.reshape(n_grp, per, tm)
    member_id = lax.broadcasted_iota(jnp.int32, sel3.shape, 1)
    m1 = jnp.max(sel3, axis=1, keepdims=True)
    i1 = jnp.min(jnp.where(sel3 == m1, member_id, per), axis=1, keepdims=True)
    m2 = jnp.max(jnp.where(member_id == i1, neg, sel3), axis=1, keepdims=True)
    gscore = (m1 + m2).reshape(n_grp, tm)
    gid = lax.broadcasted_iota(jnp.int32, gscore.shape, 0)
    ahead = jnp.zeros(gscore.shape, jnp.int32)
    for gj in range(n_grp):
        other = gscore[gj:gj + 1, :]
        ahead = ahead + jnp.where((other > gscore) | ((other == gscore) & (gid > gj)), 1, 0)
    group_ok = (ahead < TOPK_GROUPS).reshape(n_grp, 1, tm)
    sel = jnp.where(group_ok, sel3, NEG_INF).reshape(N_EXPERTS, tm)

    eid = lax.broadcasted_iota(jnp.int32, sel.shape, 0)
    chosen = jnp.zeros(sel.shape, jnp.bool_)
    picks = []
    for _ in range(TOP_K):
        m = jnp.max(sel, axis=0, keepdims=True)
        idx = jnp.min(jnp.where(sel == m, eid, N_EXPERTS), axis=0, keepdims=True)
        hit = eid == idx
        chosen = chosen | hit
        sel = jnp.where(hit, neg, sel)
        picks.append(idx)
    w = jnp.where(chosen, scores, 0.0)
    gate = ROUTED_SCALE * w / jnp.sum(w, axis=0, keepdims=True)

    member = jnp.where(chosen, 1.0, 0.0)
    earlier = lax.broadcasted_iota(jnp.int32, (tm, tm), 0) < lax.broadcasted_iota(jnp.int32, (tm, tm), 1)
    pos = _dot(member.astype(BF16), jnp.where(earlier, 1.0, 0.0).astype(BF16)) + carry_ref[...]
    for k, idx in enumerate(picks):
        hit = eid == idx
        eidx_ref[k:k + 1, :] = idx
        rank_ref[k:k + 1, :] = jnp.sum(jnp.where(hit, pos, 0.0), axis=0, keepdims=True)
        w_ref[k:k + 1, :] = jnp.sum(jnp.where(hit, gate, 0.0), axis=0, keepdims=True)
    carry_ref[...] += jnp.sum(member, axis=1, keepdims=True)
    cnt_ref[...] = carry_ref[...]


def _router(v, rw_t, rb):
    n_rows, d = v.shape
    cols = lambda i: (0, i)
    const = lambda i: (0, 0)
    outs = [jax.ShapeDtypeStruct((TOP_K, n_rows), jnp.int32), jax.ShapeDtypeStruct((TOP_K, n_rows), F32),
            jax.ShapeDtypeStruct((TOP_K, n_rows), F32), jax.ShapeDtypeStruct((N_EXPERTS, 1), F32)]
    return pl.pallas_call(
        _router_kernel,
        grid=(n_rows // TM,),
        in_specs=[pl.BlockSpec((TM, d), lambda i: (i, 0)), pl.BlockSpec(rw_t.shape, const),
                  pl.BlockSpec(rb.shape, const)],
        out_specs=[pl.BlockSpec((TOP_K, TM), cols), pl.BlockSpec((TOP_K, TM), cols),
                   pl.BlockSpec((TOP_K, TM), cols), pl.BlockSpec((N_EXPERTS, 1), const)],
        out_shape=outs,
        scratch_shapes=[pltpu.VMEM((N_EXPERTS, 1), F32)],
        compiler_params=_cparams(("arbitrary",)),
        name="router",
    )(v, rw_t, rb)


def _slots_kernel(eidx_ref, rank_ref, cnt_ref, slot_ref):
    tm = eidx_ref.shape[1]
    eid = lax.broadcasted_iota(jnp.int32, (N_EXPERTS, tm), 0)
    for k in range(TOP_K):
        before = jnp.sum(jnp.where(eid < eidx_ref[k:k + 1, :], cnt_ref[...], 0.0), axis=0, keepdims=True)
        slot_ref[k:k + 1, :] = (before + rank_ref[k:k + 1, :]).astype(jnp.int32)


def _slots(eidx, rank, cnt):
    n_rows = eidx.shape[1]
    tm = next(c for c in (2048, 1024, 512, 256) if n_rows % c == 0)
    cols = lambda i: (0, i)
    return pl.pallas_call(
        _slots_kernel,
        grid=(n_rows // tm,),
        in_specs=[pl.BlockSpec((TOP_K, tm), cols), pl.BlockSpec((TOP_K, tm), cols),
                  pl.BlockSpec(cnt.shape, lambda i: (0, 0))],
        out_specs=pl.BlockSpec((TOP_K, tm), cols),
        out_shape=jax.ShapeDtypeStruct((TOP_K, n_rows), jnp.int32),
        compiler_params=_cparams(("arbitrary",)),
        name="slots",
    )(eidx, rank, cnt)


SC_WINDOW = 128


def _sc_mesh():
    return plsc.VectorSubcoreMesh(core_axis_name="core", subcore_axis_name="subcore")


def _sc_dispatch(rows, slot_t, n_out):
    n_rows, width = rows.shape
    n_chunks = n_rows // SC_WINDOW
    info = plsc.get_sparse_core_info()
    n_workers = info.num_cores * info.num_subcores

    @functools.partial(
        pl.kernel, mesh=_sc_mesh(),
        out_type=jax.ShapeDtypeStruct((n_out, width), rows.dtype),
        scratch_types=[pltpu.VMEM((TOP_K, SC_WINDOW), jnp.int32), pltpu.VMEM((SC_WINDOW, width), rows.dtype)],
        name="moe_dispatch")
    def run(rows_hbm, idx_hbm, out_hbm, idx_v, rows_v):
        wid = lax.axis_index("subcore") * info.num_cores + lax.axis_index("core")

        @pl.loop(wid, n_chunks, step=n_workers)
        def _(c):
            r0 = pl.multiple_of(c * SC_WINDOW, SC_WINDOW)
            pltpu.sync_copy(idx_hbm.at[:, pl.ds(r0, SC_WINDOW)], idx_v)
            pltpu.sync_copy(rows_hbm.at[pl.ds(r0, SC_WINDOW)], rows_v)
            for k in range(TOP_K):
                pltpu.sync_copy(rows_v, out_hbm.at[idx_v.at[k]])

    return run(rows, slot_t)


def _sc_collect(rows, slot_t):
    n_picks, n_rows = slot_t.shape
    width = rows.shape[1]
    n_chunks = n_rows // SC_WINDOW
    info = plsc.get_sparse_core_info()
    n_workers = info.num_cores * info.num_subcores

    @functools.partial(
        pl.kernel, mesh=_sc_mesh(),
        out_type=jax.ShapeDtypeStruct((n_picks, n_rows, width), rows.dtype),
        scratch_types=[pltpu.VMEM((TOP_K, SC_WINDOW), jnp.int32), pltpu.VMEM((SC_WINDOW, width), rows.dtype)],
        name="moe_collect")
    def run(rows_hbm, idx_hbm, out_hbm, idx_v, rows_v):
        wid = lax.axis_index("subcore") * info.num_cores + lax.axis_index("core")

        @pl.loop(wid, n_chunks, step=n_workers)
        def _(c):
            r0 = pl.multiple_of(c * SC_WINDOW, SC_WINDOW)
            pltpu.sync_copy(idx_hbm.at[:, pl.ds(r0, SC_WINDOW)], idx_v)
            for k in range(TOP_K):
                pltpu.sync_copy(rows_hbm.at[idx_v.at[k]], rows_v)
                pltpu.sync_copy(rows_v, out_hbm.at[k, pl.ds(r0, SC_WINDOW)])

    return run(rows, slot_t)


EXPERT_TILE = 512


def _work_items(cnt, n_slots):
    counts = cnt[:, 0].astype(jnp.int32)
    ends = jnp.cumsum(counts)
    n_tiles = n_slots // EXPERT_TILE
    bounds = jnp.sort(jnp.concatenate([jnp.arange(n_tiles, dtype=jnp.int32) * EXPERT_TILE, ends - counts]))
    nxt = jnp.concatenate([bounds[1:], jnp.array([n_slots], jnp.int32)])
    tile = jnp.minimum(bounds // EXPERT_TILE, n_tiles - 1)
    expert = jnp.sum((ends[None, :] <= bounds[:, None]).astype(jnp.int32), axis=1)
    expert = jnp.minimum(expert, N_EXPERTS - 1)
    return tile, expert, bounds - tile * EXPERT_TILE, nxt - tile * EXPERT_TILE


def _experts_kernel(tile_ref, exp_ref, lo_ref, hi_ref, xs_ref, wg_ref, wu_ref, wd_ref, ys_ref,
                    acc_ref, wgb_ref, wub_ref, wdb_ref):
    i = pl.program_id(0)
    lo, hi = lo_ref[i], hi_ref[i]

    @pl.when((i == 0) | (exp_ref[i] != exp_ref[jnp.maximum(i - 1, 0)]))
    def _():
        wgb_ref[...] = wg_ref[...].astype(BF16)
        wub_ref[...] = wu_ref[...].astype(BF16)
        wdb_ref[...] = wd_ref[...].astype(BF16)

    def ffn():
        x_lo, x_hi = _unpack_halves(xs_ref[...])
        x_lo, x_hi = x_lo.astype(BF16), x_hi.astype(BF16)
        n = x_lo.shape[1]
        a = _dot(x_lo, wgb_ref[0:n, :]) + _dot(x_hi, wgb_ref[n:, :])
        u = _dot(x_lo, wub_ref[0:n, :]) + _dot(x_hi, wub_ref[n:, :])
        return _dot(((a * jax.nn.sigmoid(a)) * u).astype(BF16), wdb_ref[...])

    whole = (lo == 0) & (hi == EXPERT_TILE)

    @pl.when(whole)
    def _():
        ys_ref[...] = _pack_halves(ffn())

    @pl.when(jnp.logical_not(whole) & (hi > lo))
    def _():
        y = ffn()
        row = lax.broadcasted_iota(jnp.int32, y.shape, 0)
        y = jnp.where((row >= lo) & (row < hi), y, 0.0)

        @pl.when(lo == 0)
        def _():
            acc_ref[...] = y

        @pl.when((lo > 0) & (hi < EXPERT_TILE))
        def _():
            acc_ref[...] += y

        @pl.when((lo > 0) & (hi == EXPERT_TILE))
        def _():
            ys_ref[...] = _pack_halves(acc_ref[...] + y)


def _experts(xs, items, layer, exp_wg, exp_wu, exp_wd):
    n_slots, half = xs.shape
    d, hid = exp_wg.shape[-2:]
    tile, expert, lo, hi = items
    grid_spec = pltpu.PrefetchScalarGridSpec(
        num_scalar_prefetch=4,
        grid=(tile.shape[0],),
        in_specs=[pl.BlockSpec((EXPERT_TILE, half), lambda i, t, e, lo, hi: (t[i], 0)),
                  pl.BlockSpec((None, None, d, hid), lambda i, t, e, lo, hi: (layer, e[i], 0, 0)),
                  pl.BlockSpec((None, None, d, hid), lambda i, t, e, lo, hi: (layer, e[i], 0, 0)),
                  pl.BlockSpec((None, None, hid, d), lambda i, t, e, lo, hi: (layer, e[i], 0, 0))],
        out_specs=pl.BlockSpec((EXPERT_TILE, half), lambda i, t, e, lo, hi: (t[i], 0)),
        scratch_shapes=[pltpu.VMEM((EXPERT_TILE, d), F32), pltpu.VMEM((d, hid), BF16),
                        pltpu.VMEM((d, hid), BF16), pltpu.VMEM((hid, d), BF16)])
    return pl.pallas_call(
        _experts_kernel,
        grid_spec=grid_spec,
        out_shape=jax.ShapeDtypeStruct((n_slots, half), jnp.uint32),
        compiler_params=_cparams(("arbitrary",)),
        name="experts",
    )(tile, expert, lo, hi, xs, exp_wg, exp_wu, exp_wd)


def _moe_out_kernel(yg_ref, w_ref, v_ref, sg_ref, su_ref, sd_ref, h1_ref, mod_ref, gpost_ref, o_ref,
                    *, tiles_per_batch, n_batch, d):
    i = pl.program_id(0)
    x = v_ref[...]
    a = _dot(x, sg_ref[...])
    f = _dot(((a * jax.nn.sigmoid(a)) * _dot(x, su_ref[...])).astype(BF16), sd_ref[...])
    n = d // 2
    f_lo, f_hi = f[:, :n], f[:, n:]
    w = w_ref[...]
    for k in range(TOP_K):
        y_lo, y_hi = _unpack_halves(yg_ref[k])
        wk = w[:, k:k + 1]
        f_lo = f_lo + wk * y_lo
        f_hi = f_hi + wk * y_hi
    f = jnp.concatenate([f_lo, f_hi], axis=1)
    bi = jnp.minimum(i // tiles_per_batch, n_batch)
    g2 = mod_ref[pl.ds(bi, 1), 5 * d:6 * d]
    o_ref[...] = h1_ref[...] + g2 * _rms(f, gpost_ref[...])


def _moe_out(yg, w, v, sg, su, sd, h1, mods, gpost, *, n_batch, seq):
    n_rows, d = v.shape
    rows = lambda i: (i, 0)
    const = lambda i: (0, 0)
    return pl.pallas_call(
        functools.partial(_moe_out_kernel, tiles_per_batch=seq // TM, n_batch=n_batch, d=d),
        grid=(n_rows // TM,),
        in_specs=[pl.BlockSpec((TOP_K, TM, d // 2), lambda i: (0, i, 0)), pl.BlockSpec((TM, TOP_K), rows),
                  pl.BlockSpec((TM, d), rows),
                  pl.BlockSpec(sg.shape, const), pl.BlockSpec(su.shape, const), pl.BlockSpec(sd.shape, const),
                  pl.BlockSpec((TM, d), rows), pl.BlockSpec(mods.shape, const), pl.BlockSpec((1, d), const)],
        out_specs=pl.BlockSpec((TM, d), rows),
        out_shape=jax.ShapeDtypeStruct((n_rows, d), F32),
        compiler_params=_cparams(("arbitrary",)),
        name="moe_out",
    )(yg, w, v, sg, su, sd, h1, mods, gpost)


def _moe(v, vp, layer, rw, rb, exp_wg, exp_wu, exp_wd, sg, su, sd, h1, mods, gpost, *, n_batch, seq):
    n_rows = v.shape[0]
    eidx, rank, w_t, cnt = _router(v, rw, rb)
    slot_t = _slots(eidx, rank, cnt)
    n_slots = n_rows * TOP_K
    xs = _sc_dispatch(vp, slot_t, n_slots)
    ys = _experts(xs, _work_items(cnt, n_slots), layer, exp_wg, exp_wu, exp_wd)
    yg = _sc_collect(ys, slot_t)
    return _moe_out(yg, w_t.T, v, sg, su, sd, h1, mods, gpost, n_batch=n_batch, seq=seq)


def _rope_tables(seq):
    rows = seq // GRID_W
    row_id = jnp.repeat(jnp.arange(rows, dtype=F32), GRID_W)
    col_id = jnp.tile(jnp.arange(GRID_W, dtype=F32), rows)

    def tables(rot_dim):
        axis_dim = rot_dim // 2
        inv_freq = ROPE_BASE ** (-jnp.arange(0, axis_dim, 2, dtype=F32) / axis_dim)
        ang_r = row_id[:, None] * inv_freq[None, :]
        ang_c = col_id[:, None] * inv_freq[None, :]
        cos = jnp.concatenate([jnp.cos(ang_r), jnp.cos(ang_r), jnp.cos(ang_c), jnp.cos(ang_c)], axis=1)
        sin = jnp.concatenate([-jnp.sin(ang_r), jnp.sin(ang_r), -jnp.sin(ang_c), jnp.sin(ang_c)], axis=1)
        return cos, sin

    cos64, sin64 = tables(GQA_DIM)
    cos32, sin32 = tables(MLA_ROPE)
    ones = jnp.ones((seq, MLA_NOPE), F32)
    pad = LANES - MLA_NOPE - MLA_ROPE
    cospe = jnp.concatenate([ones, cos32, jnp.ones((seq, pad), F32)], axis=1)
    sinpe = jnp.concatenate([0 * ones, sin32, jnp.zeros((seq, pad), F32)], axis=1)
    tab = jnp.concatenate([cos64, cos64, sin64, sin64, cospe, sinpe], axis=1)
    ident = jnp.concatenate([jnp.ones((TM, LANES), F32), jnp.zeros((TM, LANES), F32),
                             jnp.ones((TM, LANES), F32), jnp.zeros((TM, LANES), F32)], axis=1)
    return jnp.concatenate([tab, ident], axis=0)


def _pack_w_in(w):
    d = w.shape[0]
    sizes = (MLA_Q_LORA, MLA_KV_LORA, MLA_ROPE, GQA_HEADS * GQA_DIM, GQA_KV_HEADS * GQA_DIM,
             GQA_KV_HEADS * GQA_DIM, RET_HEADS * RET_QK, RET_HEADS * RET_QK, RET_HEADS * RET_V,
             RET_HEADS * RET_V, 3 * d)
    offs, parts = 0, []
    for s in sizes:
        parts.append(w[:, offs:offs + s])
        offs += s
    cq, ckv, kpe, gq, gk, gv, rq, rk, rv, rg, gates = parts

    def twice(m):
        heads = [m[:, i * GQA_DIM:(i + 1) * GQA_DIM] for i in range(GQA_KV_HEADS)]
        return jnp.concatenate([hh for hd in heads for hh in (hd, hd)], axis=1)

    kpe_slab = jnp.concatenate([jnp.zeros((d, MLA_NOPE), F32), kpe,
                                jnp.zeros((d, LANES - MLA_NOPE - MLA_ROPE), F32)], axis=1)
    packed = jnp.concatenate([cq, ckv, kpe_slab, gq * GQA_DIM ** -0.5, twice(gk), twice(gv),
                              rq, rk * RET_QK ** -0.5, rv, rg, gates], axis=1)
    assert packed.shape[1] == W_COLS
    return packed.astype(BF16)


def _pack_mla_up(w_uq, w_ukv):
    r = w_uq.shape[0]
    dq = MLA_NOPE + MLA_ROPE
    wq = jnp.pad(w_uq.reshape(r, MLA_HEADS, dq), ((0, 0), (0, 0), (0, LANES - dq))).reshape(r, MLA_HEADS * LANES)
    kv = w_ukv.reshape(r, MLA_HEADS, MLA_NOPE + MLA_V)
    wk = jnp.pad(kv[:, :, :MLA_NOPE], ((0, 0), (0, 0), (0, LANES - MLA_NOPE))).reshape(r, MLA_HEADS * LANES)
    wv = kv[:, :, MLA_NOPE:].reshape(r, MLA_HEADS * MLA_V)
    return wq.astype(BF16), wk.astype(BF16), wv.astype(BF16)


def kernel(x, c, ctx, c_ctx, ada_w, ada_b, norm_mix_pre, norm_mix_post, norm_ffn_pre, norm_ffn_post, w_in, mla_q_norm, mla_w_uq, mla_kv_norm, mla_w_ukv, gqa_sink, ret_decay_fwd, ret_decay_bwd, w_br_mla, w_br_gqa, w_br_ret, w_out, router_w, router_bias, exp_w_gate, exp_w_up, exp_w_down, shared_w_gate, shared_w_up, shared_w_down):
    n_batch, seq, d = x.shape
    n_ctx = ctx.shape[1]
    depth = ada_w.shape[0]
    n_lat_rows = n_batch * seq
    assert seq % TM == 0 and n_ctx == TM and n_batch < MOD_ROWS and seq % GRID_W == 0

    cond = jnp.zeros((MOD_ROWS, d), F32).at[:n_batch].set(c).at[n_batch].set(c_ctx)
    mods_all = _adaln(cond, ada_w, ada_b)
    rope = _rope_tables(seq)
    h = jnp.concatenate([x.reshape(n_lat_rows, d), ctx.reshape(n_batch * n_ctx, d)], axis=0)
    row = lambda p: p.reshape(1, -1)
    dims = dict(n_batch=n_batch, seq=seq)

    for l in range(depth):
        last = l == depth - 1
        mods = mods_all[l]
        wq, wk, wv = _pack_mla_up(mla_w_uq[l], mla_w_ukv[l])
        mq, mk, mv, gqa, ret, gates = _inproj(h, mods, row(norm_mix_pre[l]), _pack_w_in(w_in[l]), rope,
                                              row(mla_q_norm[l]), row(mla_kv_norm[l]), wq, wk, wv, **dims)
        a = _mla_attention(mq, mk, mv, ctx=n_ctx, with_ctx_queries=not last, **dims)
        sink_tab = jnp.broadcast_to(gqa_sink[l].astype(F32)[:, None], (GQA_HEADS, LANES))
        w = _window_attention(gqa, sink_tab, ctx=n_ctx, with_ctx_queries=not last, **dims)
        lg = jnp.concatenate([jax.nn.log_sigmoid(ret_decay_fwd[l].astype(F32)),
                              jax.nn.log_sigmoid(ret_decay_bwd[l].astype(F32))])
        o_f, o_b = _retention(ret, jnp.broadcast_to(lg[:, None], (2 * RET_HEADS, LANES)), ctx=n_ctx, **dims)
        n_rows = n_lat_rows if last else h.shape[0]
        h1, v, vp = _merge(a, w, o_f, o_b, ret, gates, h, mods, row(norm_mix_post[l]), row(norm_ffn_pre[l]),
                           w_br_mla[l].astype(BF16), w_br_gqa[l].astype(BF16), w_br_ret[l].astype(BF16),
                           w_out[l].astype(BF16), n_rows=n_rows, **dims)
        h = _moe(v, vp, l, router_w[l].T.astype(BF16), router_bias[l].astype(F32).reshape(-1, 1),
                 exp_w_gate, exp_w_up, exp_w_down, shared_w_gate[l].astype(BF16),
                 shared_w_up[l].astype(BF16), shared_w_down[l].astype(BF16), h1, mods,
                 row(norm_ffn_post[l]), **dims)
    return h[:n_lat_rows].reshape(n_batch, seq, d)
```

```python
import functools

import jax
import jax.numpy as jnp
from jax import lax
from jax.experimental import pallas as pl
from jax.experimental.pallas import tpu as pltpu
from jax.experimental.pallas import tpu_sc as plsc

F32 = jnp.float32
BF16 = jnp.bfloat16

GRID_W = 64
ROPE_BASE = 10000.0
NORM_EPS = 1e-6
NEG_INF = -1e30
LOG2_E = 1.4426950408889634
N_MOD = 6
MLA_HEADS, MLA_NOPE, MLA_ROPE, MLA_V = 8, 64, 32, 64
MLA_Q_LORA, MLA_KV_LORA = 256, 256
GQA_HEADS, GQA_KV_HEADS, GQA_DIM, WINDOW = 8, 2, 64, 128
RET_HEADS, RET_QK, RET_V, RET_CHUNK = 4, 64, 128, 128
N_EXPERTS, N_EXPERT_GROUPS, TOPK_GROUPS, TOP_K = 64, 8, 4, 8
EXPERTS_PER_GROUP = N_EXPERTS // N_EXPERT_GROUPS
ROUTED_SCALE = 2.5

LANES = 128
TM = 256
MOD_ROWS = 8
V7X_VMEM_LIMIT = 56 * 1024 * 1024

C_CQ, C_CKV, C_KPE = 0, 256, 512
C_G = 640
C_R = 1664
C_GATE = 3200
W_COLS = 6272


def _cparams(sem):
    return pltpu.CompilerParams(dimension_semantics=sem, vmem_limit_bytes=V7X_VMEM_LIMIT)


def _rms(x, g):
    return x * lax.rsqrt(jnp.mean(x * x, axis=-1, keepdims=True) + NORM_EPS) * g


def _dot(a, b):
    return jnp.dot(a, b, preferred_element_type=F32)


def _dot_nt(a, b):
    return lax.dot_general(a, b, (((1,), (1,)), ((), ())), preferred_element_type=F32)


def _dot_tn(a, b):
    return lax.dot_general(a, b, (((0,), (0,)), ((), ())), preferred_element_type=F32)


def _rope(x, cos, sin, half):
    n = x.shape[-1]
    reps = n // LANES
    if reps > 1:
        cos = jnp.concatenate([cos] * reps, axis=1)
        sin = jnp.concatenate([sin] * reps, axis=1)
    lane = lax.broadcasted_iota(jnp.int32, x.shape, 1)
    up = pltpu.roll(x, half, 1)
    dn = pltpu.roll(x, n - half, 1)
    partner = jnp.where((lane & (2 * half - 1)) < half, dn, up)
    return x * cos + partner * sin


def _lane_lo(shape):
    return (lax.broadcasted_iota(jnp.int32, shape, 1) & (LANES - 1)) < (LANES // 2)


def _pack_halves(x):
    n = x.shape[1] // 2
    bits = lambda t: lax.bitcast_convert_type(t.astype(BF16).astype(F32), jnp.uint32)
    return (bits(x[:, :n]) >> 16) | bits(x[:, n:])


def _unpack_halves(p):
    lo = lax.bitcast_convert_type(p << 16, F32)
    hi = lax.bitcast_convert_type(p & jnp.uint32(0xFFFF0000), F32)
    return lo, hi


def _ada_kernel(c_ref, w_ref, b_ref, o_ref):
    c = c_ref[...]
    s = c * jax.nn.sigmoid(c)
    o_ref[...] = _dot(s.astype(BF16), w_ref[...].astype(BF16)) + b_ref[...]


def _adaln(cond, ada_w, ada_b):
    n_layers, d, n = ada_w.shape
    tn = 1024
    return pl.pallas_call(
        _ada_kernel,
        grid=(n_layers, n // tn),
        in_specs=[pl.BlockSpec((MOD_ROWS, d), lambda l, j: (0, 0)),
                  pl.BlockSpec((None, d, tn), lambda l, j: (l, 0, j)),
                  pl.BlockSpec((None, 1, tn), lambda l, j: (l, 0, j))],
        out_specs=pl.BlockSpec((None, MOD_ROWS, tn), lambda l, j: (l, 0, j)),
        out_shape=jax.ShapeDtypeStruct((n_layers, MOD_ROWS, n), F32),
        compiler_params=_cparams(("arbitrary", "arbitrary")),
        name="adaln",
    )(cond, ada_w, ada_b.reshape(n_layers, 1, n))


def _inproj_kernel(h_ref, mod_ref, gpre_ref, w_ref, rope_ref, qn_ref, kvn_ref, wuq_ref, wuk_ref, wuv_ref,
                   mq_ref, mk_ref, mv_ref, gqa_ref, ret_ref, gate_ref, *, tiles_per_batch, n_batch, d):
    i = pl.program_id(0)
    bi = jnp.minimum(i // tiles_per_batch, n_batch)
    sh = mod_ref[pl.ds(bi, 1), 0:d]
    sc = mod_ref[pl.ds(bi, 1), d:2 * d]
    u = (_rms(h_ref[...], gpre_ref[...]) * (1.0 + sc) + sh).astype(BF16)

    cos64 = rope_ref[:, 0:LANES]
    sin64 = rope_ref[:, LANES:2 * LANES]
    cospe = rope_ref[:, 2 * LANES:3 * LANES]
    sinpe = rope_ref[:, 3 * LANES:4 * LANES]

    c = _dot(u, w_ref[:, C_CQ:C_G])
    kpe = _rope(c[:, C_KPE:C_G], cospe, sinpe, MLA_ROPE // 4)
    qn = _rms(c[:, C_CQ:C_CKV], qn_ref[...]).astype(BF16)
    q = _rope(_dot(qn, wuq_ref[...]), cospe, sinpe, MLA_ROPE // 4)
    mq_ref[...] = (q * ((MLA_NOPE + MLA_ROPE) ** -0.5 * LOG2_E)).astype(mq_ref.dtype)
    kvn = _rms(c[:, C_CKV:C_KPE], kvn_ref[...]).astype(BF16)
    k = _dot(kvn, wuk_ref[...]) + jnp.concatenate([kpe] * MLA_HEADS, axis=1)
    mk_ref[...] = k.astype(mk_ref.dtype)
    v = _dot(kvn, wuv_ref[...])
    lane = lax.broadcasted_iota(jnp.int32, v.shape, 1)
    value_lane = ((lane & (LANES - 1)) < MLA_V) == (((lane >> (LANES.bit_length() - 1)) & 1) == 0)
    mv_ref[...] = jnp.where(value_lane, v, 1.0).astype(mv_ref.dtype)

    g = _dot(u, w_ref[:, C_G:C_R])
    n_qk = GQA_HEADS * GQA_DIM + 2 * GQA_KV_HEADS * GQA_DIM
    gqa_ref[:, 0:n_qk] = _rope(g[:, 0:n_qk], cos64, sin64, GQA_DIM // 4).astype(gqa_ref.dtype)
    gqa_ref[:, n_qk:] = g[:, n_qk:].astype(gqa_ref.dtype)

    r = _dot(u, w_ref[:, C_R:C_GATE])
    n_qk = 2 * RET_HEADS * RET_QK
    ret_ref[:, 0:n_qk] = _rope(r[:, 0:n_qk], cos64, sin64, RET_QK // 4)
    ret_ref[:, n_qk:] = r[:, n_qk:]

    gate_ref[...] = _dot(u, w_ref[:, C_GATE:W_COLS]).astype(gate_ref.dtype)


def _inproj(h, mods, gpre, w_all, rope, qn, kvn, wuq, wuk, wuv, *, n_batch, seq):
    t, d = h.shape
    tiles_per_batch = seq // TM
    n_lat_tiles = n_batch * tiles_per_batch
    const = lambda i: (0, 0)
    rows = lambda i: (i, 0)
    rope_idx = lambda i: (jnp.where(i < n_lat_tiles, i % tiles_per_batch, tiles_per_batch), 0)
    hq = MLA_HEADS * LANES
    outs = [jax.ShapeDtypeStruct((t, hq), BF16), jax.ShapeDtypeStruct((t, hq), BF16),
            jax.ShapeDtypeStruct((t, hq), BF16),
            jax.ShapeDtypeStruct((t, C_R - C_G), BF16),
            jax.ShapeDtypeStruct((t, C_GATE - C_R), F32),
            jax.ShapeDtypeStruct((t, W_COLS - C_GATE), BF16)]
    return pl.pallas_call(
        functools.partial(_inproj_kernel, tiles_per_batch=tiles_per_batch, n_batch=n_batch, d=d),
        grid=(t // TM,),
        in_specs=[pl.BlockSpec((TM, d), rows),
                  pl.BlockSpec(mods.shape, const),
                  pl.BlockSpec((1, d), const),
                  pl.BlockSpec(w_all.shape, const),
                  pl.BlockSpec((TM, 4 * LANES), rope_idx),
                  pl.BlockSpec(qn.shape, const), pl.BlockSpec(kvn.shape, const),
                  pl.BlockSpec(wuq.shape, const), pl.BlockSpec(wuk.shape, const), pl.BlockSpec(wuv.shape, const)],
        out_specs=[pl.BlockSpec((TM, o.shape[1]), rows) for o in outs],
        out_shape=outs,
        compiler_params=_cparams(("arbitrary",)),
        name="inproj",
    )(h, mods, gpre, w_all, rope, qn, kvn, wuq, wuk, wuv)


MLA_HEADS_PER_STEP = 4


def _mla_kernel(q_ref, kl_ref, kc_ref, vl_ref, vc_ref, o_ref, s_ref, p_ref, *, n_lat_tiles, has_ctx_tile):
    n_ctx = kc_ref.shape[0]

    def body(with_lat):
        n_keys = n_ctx + (kl_ref.shape[0] if with_lat else 0)

        def scores(h):
            sl = slice(h * LANES, (h + 1) * LANES)
            s_ref[h % 2, :, 0:n_ctx] = _dot_nt(q_ref[:, sl], kc_ref[:, sl])
            if with_lat:
                s_ref[h % 2, :, n_ctx:n_keys] = _dot_nt(q_ref[:, sl], kl_ref[:, sl])

        def probs(h):
            s = s_ref[h % 2, :, 0:n_keys]
            p_ref[h % 2, :, 0:n_keys] = jnp.exp2(s - jnp.max(s, axis=-1, keepdims=True)).astype(BF16)

        def weighted(h):
            sl = slice(h * LANES, (h + 1) * LANES)
            o = _dot(p_ref[h % 2, :, 0:n_ctx], vc_ref[:, sl])
            if with_lat:
                o = o + _dot(p_ref[h % 2, :, n_ctx:n_keys], vl_ref[:, sl])
            return o / pltpu.roll(o, LANES // 2, 1)

        outs = [None] * MLA_HEADS_PER_STEP
        scores(0)
        for h in range(MLA_HEADS_PER_STEP):
            if h + 1 < MLA_HEADS_PER_STEP:
                scores(h + 1)
            probs(h)
            outs[h] = weighted(h)
        for pr in range(MLA_HEADS_PER_STEP // 2):
            even, odd = outs[2 * pr], outs[2 * pr + 1]
            o_ref[:, pr * LANES:(pr + 1) * LANES] = jnp.where(_lane_lo(even.shape), even, odd).astype(o_ref.dtype)

    if has_ctx_tile:
        i = pl.program_id(2)
        pl.when(i < n_lat_tiles)(lambda: body(True))
        pl.when(i == n_lat_tiles)(lambda: body(False))
    else:
        body(True)


def _mla_attention(mq, mk, mv, *, n_batch, seq, ctx, with_ctx_queries):
    t = mq.shape[0]
    nq = seq // TM
    n_lat_blocks = n_batch * nq
    ctx_blk0 = n_batch * seq // ctx
    hps = MLA_HEADS_PER_STEP
    assert ctx == TM
    q_idx = lambda b, g, i: (jnp.where(i < nq, b * nq + i, n_lat_blocks + b), g)
    lat_idx = lambda b, g, i: (b, g)
    ctx_idx = lambda b, g, i: (ctx_blk0 + b, g)
    n_rows = t if with_ctx_queries else n_batch * seq
    return pl.pallas_call(
        functools.partial(_mla_kernel, n_lat_tiles=nq, has_ctx_tile=with_ctx_queries),
        grid=(n_batch, MLA_HEADS // hps, nq + (1 if with_ctx_queries else 0)),
        in_specs=[pl.BlockSpec((TM, hps * LANES), q_idx),
                  pl.BlockSpec((seq, hps * LANES), lat_idx),
                  pl.BlockSpec((ctx, hps * LANES), ctx_idx),
                  pl.BlockSpec((seq, hps * LANES), lat_idx),
                  pl.BlockSpec((ctx, hps * LANES), ctx_idx)],
        out_specs=pl.BlockSpec((TM, hps * MLA_V), q_idx),
        out_shape=jax.ShapeDtypeStruct((n_rows, MLA_HEADS * MLA_V), BF16),
        scratch_shapes=[pltpu.VMEM((2, TM, ctx + seq), F32), pltpu.VMEM((2, TM, ctx + seq), BF16)],
        compiler_params=_cparams(("arbitrary", "arbitrary", "arbitrary")),
        name="mla_attn",
    )(mq, mk, mk, mv, mv)


def _win_kernel(q_ref, kp_ref, kcur_ref, kn_ref, vp_ref, vcur_ref, vn_ref, kc_ref, vc_ref, sink_ref, o_ref,
                s_ref, p_ref, *, n_lat_tiles, seq, has_ctx_tile):
    i = pl.program_id(1)
    tq = q_ref.shape[0]
    group = GQA_HEADS // GQA_KV_HEADS

    def body(with_lat):
        n_ctx = kc_ref.shape[0]
        n_keys = n_ctx + (tq + 2 * WINDOW if with_lat else 0)
        if with_lat:
            q_pos = i * tq + lax.broadcasted_iota(jnp.int32, (tq, n_keys), 0)
            k_pos = i * tq - WINDOW - n_ctx + lax.broadcasted_iota(jnp.int32, (tq, n_keys), 1)
            in_band = (jnp.abs(q_pos - k_pos) <= WINDOW) & (k_pos >= 0) & (k_pos < seq)
            valid = in_band | (lax.broadcasted_iota(jnp.int32, (tq, n_keys), 1) < n_ctx)
        lo = _lane_lo((tq, LANES))
        lo_k = _lane_lo((n_keys, LANES))
        keys, values = [], []
        for kv in range(GQA_KV_HEADS):
            sl = slice(kv * LANES, (kv + 1) * LANES)
            if with_lat:
                k_all = jnp.concatenate([kc_ref[:, sl], kp_ref[:, sl], kcur_ref[:, sl], kn_ref[:, sl]], axis=0)
                v_all = jnp.concatenate([vc_ref[:, sl], vp_ref[:, sl], vcur_ref[:, sl], vn_ref[:, sl]], axis=0)
            else:
                k_all, v_all = kc_ref[:, sl], vc_ref[:, sl]
            keys.append(k_all)
            one = jnp.ones_like(v_all)
            values.append((jnp.where(lo_k, v_all, one), jnp.where(lo_k, one, v_all)))

        def scores(hd):
            kv, pair = hd // group, hd // 2
            qp = q_ref[:, pair * LANES:(pair + 1) * LANES]
            qm = jnp.where(lo if hd % 2 == 0 else jnp.logical_not(lo), qp, jnp.zeros_like(qp))
            s = _dot_nt(qm, keys[kv])
            s_ref[hd % 2, :, 0:n_keys] = jnp.where(valid, s, NEG_INF) if with_lat else s

        def probs(hd):
            s = s_ref[hd % 2, :, 0:n_keys]
            m = jnp.maximum(jnp.max(s, axis=-1, keepdims=True), sink_ref[hd:hd + 1, 0:1])
            p_ref[hd % 2, :, 0:n_keys] = jnp.exp2(s - m).astype(BF16)
            return jnp.exp2(sink_ref[hd:hd + 1, 0:1] - m)

        def weighted(hd, sink_term):
            o = _dot(p_ref[hd % 2, :, 0:n_keys], values[hd // group][hd % 2])
            return o / (pltpu.roll(o, LANES // 2, 1) + sink_term)

        outs = [None] * GQA_HEADS
        scores(0)
        for hd in range(GQA_HEADS):
            if hd + 1 < GQA_HEADS:
                scores(hd + 1)
            outs[hd] = weighted(hd, probs(hd))
        for pair in range(GQA_HEADS // 2):
            o_ref[:, pair * LANES:(pair + 1) * LANES] = jnp.where(
                lo, outs[2 * pair], outs[2 * pair + 1]).astype(o_ref.dtype)

    if has_ctx_tile:
        pl.when(i < n_lat_tiles)(lambda: body(True))
        pl.when(i == n_lat_tiles)(lambda: body(False))
    else:
        body(True)


def _window_attention(gqa, sink_tab, *, n_batch, seq, ctx, with_ctx_queries):
    t = gqa.shape[0]
    nq = seq // TM
    n_lat_blocks = n_batch * nq
    per_tile = TM // WINDOW
    n_win_blocks = seq // WINDOW
    assert ctx == TM
    ctx_blk0 = n_batch * seq // ctx
    nqk = GQA_HEADS * GQA_DIM
    kw = 2 * GQA_KV_HEADS * GQA_DIM
    k_col, v_col = nqk // kw, nqk // kw + 1
    q_idx = lambda b, i: (jnp.where(i < nq, b * nq + i, n_lat_blocks + b), 0)
    cur = lambda col: (lambda b, i: (b * nq + jnp.minimum(i, nq - 1), col))
    prev = lambda col: (lambda b, i: (b * n_win_blocks + jnp.clip(per_tile * i - 1, 0, n_win_blocks - 1), col))
    nxt = lambda col: (lambda b, i: (b * n_win_blocks + jnp.clip(per_tile * (i + 1), 0, n_win_blocks - 1), col))
    cidx = lambda col: (lambda b, i: (ctx_blk0 + b, col))
    n_rows = t if with_ctx_queries else n_batch * seq
    return pl.pallas_call(
        functools.partial(_win_kernel, n_lat_tiles=nq, seq=seq, has_ctx_tile=with_ctx_queries),
        grid=(n_batch, nq + (1 if with_ctx_queries else 0)),
        in_specs=[pl.BlockSpec((TM, nqk), q_idx),
                  pl.BlockSpec((WINDOW, kw), prev(k_col)), pl.BlockSpec((TM, kw), cur(k_col)),
                  pl.BlockSpec((WINDOW, kw), nxt(k_col)),
                  pl.BlockSpec((WINDOW, kw), prev(v_col)), pl.BlockSpec((TM, kw), cur(v_col)),
                  pl.BlockSpec((WINDOW, kw), nxt(v_col)),
                  pl.BlockSpec((ctx, kw), cidx(k_col)), pl.BlockSpec((ctx, kw), cidx(v_col)),
                  pl.BlockSpec(sink_tab.shape, lambda b, i: (0, 0))],
        out_specs=pl.BlockSpec((TM, nqk), q_idx),
        out_shape=jax.ShapeDtypeStruct((n_rows, nqk), BF16),
        scratch_shapes=[pltpu.VMEM((2, TM, ctx + TM + 2 * WINDOW), F32),
                        pltpu.VMEM((2, TM, ctx + TM + 2 * WINDOW), BF16)],
        compiler_params=_cparams(("arbitrary", "arbitrary")),
        name="win_attn",
    )(gqa, gqa, gqa, gqa, gqa, gqa, gqa, gqa, gqa, sink_tab)


def _ret_kernel(f_ref, b_ref, lg_ref, of_ref, ob_ref, sf_ref, sb_ref):
    @pl.when(pl.program_id(1) == 0)
    def _():
        sf_ref[...] = jnp.zeros_like(sf_ref)
        sb_ref[...] = jnp.zeros_like(sb_ref)

    L = f_ref.shape[0]
    ii = lax.broadcasted_iota(jnp.int32, (L, L), 0)
    jj = lax.broadcasted_iota(jnp.int32, (L, L), 1)
    row = lax.broadcasted_iota(jnp.int32, (L, LANES), 0).astype(F32)
    lo = _lane_lo((L, LANES))
    srow_lo = lax.broadcasted_iota(jnp.int32, (LANES, LANES), 0) < RET_QK
    nq = RET_HEADS * RET_QK

    def direction(x_ref, o_ref, s_ref, lg_row0, forward):
        for pr in range(RET_HEADS // 2):
            lg = [lg_ref[lg_row0 + 2 * pr + e:lg_row0 + 2 * pr + e + 1, :] for e in range(2)]
            lg_lane = jnp.where(lo, lg[0], lg[1])
            q = x_ref[:, pr * LANES:(pr + 1) * LANES]
            k = x_ref[:, nq + pr * LANES:nq + (pr + 1) * LANES]
            if forward:
                qd = q * jnp.exp(lg_lane * (row + 1.0))
                kd = k * jnp.exp(lg_lane * (L - 1.0 - row))
                dist = ii - jj
            else:
                qd = q * jnp.exp(lg_lane * (L - row))
                kd = k * jnp.exp(lg_lane * row)
                dist = jj - ii
            distf = jnp.maximum(dist, 0).astype(F32)
            kb, kdb = k.astype(BF16), kd.astype(BF16)
            state = s_ref[pr]
            state_b = state.astype(BF16)
            upd = []
            for e in range(2):
                hd = 2 * pr + e
                keep = lo if e == 0 else jnp.logical_not(lo)
                v = x_ref[:, 2 * nq + hd * RET_V:2 * nq + (hd + 1) * RET_V].astype(BF16)
                inner = jnp.where(dist >= 0, jnp.exp(lg[e][:, 0:1] * distf), 0.0)
                attn = _dot_nt(jnp.where(keep, q, 0.0).astype(BF16), kb) * inner
                o = _dot(attn.astype(BF16), v) + _dot(jnp.where(keep, qd, 0.0).astype(BF16), state_b)
                o_ref[:, hd * RET_V:(hd + 1) * RET_V] = o
                upd.append(_dot_tn(kdb, v))
            chunk_decay = jnp.where(srow_lo, jnp.exp(lg[0] * float(L)), jnp.exp(lg[1] * float(L)))
            s_ref[pr] = state * chunk_decay + jnp.where(srow_lo, upd[0], upd[1])

    direction(f_ref, of_ref, sf_ref, 0, True)
    direction(b_ref, ob_ref, sb_ref, RET_HEADS, False)


def _retention(ret, lg_tab, *, n_batch, seq, ctx):
    t = ret.shape[0]
    L = RET_CHUNK
    n_lat, n_ctx = seq // L, ctx // L
    ctx0 = n_batch * n_lat
    width = 2 * RET_HEADS * RET_QK + RET_HEADS * RET_V
    fwd = lambda b, s: (jnp.where(s < n_ctx, ctx0 + b * n_ctx + s, b * n_lat + s - n_ctx), 0)
    bwd = lambda b, s: (jnp.where(s < n_ctx, ctx0 + b * n_ctx + n_ctx - 1 - s, b * n_lat + n_lat - 1 - (s - n_ctx)), 0)
    out = jax.ShapeDtypeStruct((t, RET_HEADS * RET_V), F32)
    return pl.pallas_call(
        _ret_kernel,
        grid=(n_batch, n_lat + n_ctx),
        in_specs=[pl.BlockSpec((L, width), fwd), pl.BlockSpec((L, width), bwd),
                  pl.BlockSpec(lg_tab.shape, lambda b, s: (0, 0))],
        out_specs=[pl.BlockSpec((L, RET_HEADS * RET_V), fwd), pl.BlockSpec((L, RET_HEADS * RET_V), bwd)],
        out_shape=[out, out],
        scratch_shapes=[pltpu.VMEM((RET_HEADS // 2, LANES, RET_V), F32),
                        pltpu.VMEM((RET_HEADS // 2, LANES, RET_V), F32)],
        compiler_params=_cparams(("arbitrary", "arbitrary")),
        name="retention",
    )(ret, ret, lg_tab)


def _merge_kernel(a_ref, w_ref, of_ref, ob_ref, rg_ref, gt_ref, h_ref, mod_ref, gpost_ref, gffn_ref,
                  wa_ref, ww_ref, wr_ref, wo_ref, h1_ref, v_ref, vp_ref, *, tiles_per_batch, n_batch, d):
    i = pl.program_id(0)
    bi = jnp.minimum(i // tiles_per_batch, n_batch)
    o = of_ref[...] + ob_ref[...]
    normed = []
    for hd in range(RET_HEADS):
        oh = o[:, hd * RET_V:(hd + 1) * RET_V]
        dev = oh - jnp.mean(oh, axis=-1, keepdims=True)
        normed.append(dev * lax.rsqrt(jnp.mean(dev * dev, axis=-1, keepdims=True) + NORM_EPS))
    g = rg_ref[...]
    r = (g * jax.nn.sigmoid(g)) * jnp.concatenate(normed, axis=1)
    y = (jax.nn.sigmoid(gt_ref[:, 0:d].astype(F32)) * _dot(a_ref[...], wa_ref[...])
         + jax.nn.sigmoid(gt_ref[:, d:2 * d].astype(F32)) * _dot(w_ref[...], ww_ref[...])
         + jax.nn.sigmoid(gt_ref[:, 2 * d:3 * d].astype(F32)) * _dot(r.astype(BF16), wr_ref[...]))
    z = _dot(y.astype(BF16), wo_ref[...])
    g1 = mod_ref[pl.ds(bi, 1), 2 * d:3 * d]
    sh2 = mod_ref[pl.ds(bi, 1), 3 * d:4 * d]
    sc2 = mod_ref[pl.ds(bi, 1), 4 * d:5 * d]
    h1 = h_ref[...] + g1 * _rms(z, gpost_ref[...])
    h1_ref[...] = h1
    v = _rms(h1, gffn_ref[...]) * (1.0 + sc2) + sh2
    v_ref[...] = v.astype(v_ref.dtype)
    vp_ref[...] = _pack_halves(v)


def _merge(a, w, o_f, o_b, ret, gates, h, mods, gpost, gffn, wa, ww, wr, wo, *, n_rows, n_batch, seq):
    d = h.shape[1]
    rows = lambda i: (i, 0)
    const = lambda i: (0, 0)
    rv = RET_HEADS * RET_V
    rg_col = (2 * RET_HEADS * RET_QK + rv) // rv
    outs = [jax.ShapeDtypeStruct((n_rows, d), F32), jax.ShapeDtypeStruct((n_rows, d), BF16),
            jax.ShapeDtypeStruct((n_rows, d // 2), jnp.uint32)]
    return pl.pallas_call(
        functools.partial(_merge_kernel, tiles_per_batch=seq // TM, n_batch=n_batch, d=d),
        grid=(n_rows // TM,),
        in_specs=[pl.BlockSpec((TM, a.shape[1]), rows), pl.BlockSpec((TM, w.shape[1]), rows),
                  pl.BlockSpec((TM, rv), rows), pl.BlockSpec((TM, rv), rows),
                  pl.BlockSpec((TM, rv), lambda i: (i, rg_col)),
                  pl.BlockSpec((TM, 3 * d), rows), pl.BlockSpec((TM, d), rows),
                  pl.BlockSpec(mods.shape, const), pl.BlockSpec((1, d), const), pl.BlockSpec((1, d), const),
                  pl.BlockSpec(wa.shape, const), pl.BlockSpec(ww.shape, const),
                  pl.BlockSpec(wr.shape, const), pl.BlockSpec(wo.shape, const)],
        out_specs=[pl.BlockSpec((TM, o.shape[1]), rows) for o in outs],
        out_shape=outs,
        compiler_params=_cparams(("arbitrary",)),
        name="merge",
    )(a, w, o_f, o_b, ret, gates, h, mods, gpost, gffn, wa, ww, wr, wo)


def _router_kernel(v_ref, rw_ref, rb_ref, eidx_ref, rank_ref, w_ref, cnt_ref, carry_ref):
    @pl.when(pl.program_id(0) == 0)
    def _():
        carry_ref[...] = jnp.zeros_like(carry_ref)

    tm = v_ref.shape[0]
    scores = jax.nn.sigmoid(_dot_nt(rw_ref[...], v_ref[...]))
    sel = scores + rb_ref[...]
    neg = -jnp.inf
    n_grp, per = N_EXPERT_GROUPS, EXPERTS_PER_GROUP

    sel3 = sel.reshape(n_grp, per, tm)
    member_id = lax.broadcasted_iota(jnp.int32, sel3.shape, 1)
    m1 = jnp.max(sel3, axis=1, keepdims=True)
    i1 = jnp.min(jnp.where(sel3 == m1, member_id, per), axis=1, keepdims=True)
    m2 = jnp.max(jnp.where(member_id == i1, neg, sel3), axis=1, keepdims=True)
    gscore = (m1 + m2).reshape(n_grp, tm)
    gid = lax.broadcasted_iota(jnp.int32, gscore.shape, 0)
    ahead = jnp.zeros(gscore.shape, jnp.int32)
    for gj in range(n_grp):
        other = gscore[gj:gj + 1, :]
        ahead = ahead + jnp.where((other > gscore) | ((other == gscore) & (gid > gj)), 1, 0)
    group_ok = (ahead < TOPK_GROUPS).reshape(n_grp, 1, tm)
    sel = jnp.where(group_ok, sel3, NEG_INF).reshape(N_EXPERTS, tm)

    eid = lax.broadcasted_iota(jnp.int32, sel.shape, 0)
    chosen = jnp.zeros(sel.shape, jnp.bool_)
    picks = []
    for _ in range(TOP_K):
        m = jnp.max(sel, axis=0, keepdims=True)
        idx = jnp.min(jnp.where(sel == m, eid, N_EXPERTS), axis=0, keepdims=True)
        hit = eid == idx
        chosen = chosen | hit
        sel = jnp.where(hit, neg, sel)
        picks.append(idx)
    w = jnp.where(chosen, scores, 0.0)
    gate = ROUTED_SCALE * w / jnp.sum(w, axis=0, keepdims=True)

    member = jnp.where(chosen, 1.0, 0.0)
    earlier = lax.broadcasted_iota(jnp.int32, (tm, tm), 0) < lax.broadcasted_iota(jnp.int32, (tm, tm), 1)
    pos = _dot(member.astype(BF16), jnp.where(earlier, 1.0, 0.0).astype(BF16)) + carry_ref[...]
    for k, idx in enumerate(picks):
        hit = eid == idx
        eidx_ref[k:k + 1, :] = idx
        rank_ref[k:k + 1, :] = jnp.sum(jnp.where(hit, pos, 0.0), axis=0, keepdims=True)
        w_ref[k:k + 1, :] = jnp.sum(jnp.where(hit, gate, 0.0), axis=0, keepdims=True)
    carry_ref[...] += jnp.sum(member, axis=1, keepdims=True)
    cnt_ref[...] = carry_ref[...]


def _router(v, rw_t, rb):
    n_rows, d = v.shape
    cols = lambda i: (0, i)
    const = lambda i: (0, 0)
    outs = [jax.ShapeDtypeStruct((TOP_K, n_rows), jnp.int32), jax.ShapeDtypeStruct((TOP_K, n_rows), F32),
            jax.ShapeDtypeStruct((TOP_K, n_rows), F32), jax.ShapeDtypeStruct((N_EXPERTS, 1), F32)]
    return pl.pallas_call(
        _router_kernel,
        grid=(n_rows // TM,),
        in_specs=[pl.BlockSpec((TM, d), lambda i: (i, 0)), pl.BlockSpec(rw_t.shape, const),
                  pl.BlockSpec(rb.shape, const)],
        out_specs=[pl.BlockSpec((TOP_K, TM), cols), pl.BlockSpec((TOP_K, TM), cols),
                   pl.BlockSpec((TOP_K, TM), cols), pl.BlockSpec((N_EXPERTS, 1), const)],
        out_shape=outs,
        scratch_shapes=[pltpu.VMEM((N_EXPERTS, 1), F32)],
        compiler_params=_cparams(("arbitrary",)),
        name="router",
    )(v, rw_t, rb)


def _slots_kernel(eidx_ref, rank_ref, cnt_ref, slot_ref):
    tm = eidx_ref.shape[1]
    eid = lax.broadcasted_iota(jnp.int32, (N_EXPERTS, tm), 0)
    for k in range(TOP_K):
        before = jnp.sum(jnp.where(eid < eidx_ref[k:k + 1, :], cnt_ref[...], 0.0), axis=0, keepdims=True)
        slot_ref[k:k + 1, :] = (before + rank_ref[k:k + 1, :]).astype(jnp.int32)


def _slots(eidx, rank, cnt):
    n_rows = eidx.shape[1]
    tm = next(c for c in (2048, 1024, 512, 256) if n_rows % c == 0)
    cols = lambda i: (0, i)
    return pl.pallas_call(
        _slots_kernel,
        grid=(n_rows // tm,),
        in_specs=[pl.BlockSpec((TOP_K, tm), cols), pl.BlockSpec((TOP_K, tm), cols),
                  pl.BlockSpec(cnt.shape, lambda i: (0, 0))],
        out_specs=pl.BlockSpec((TOP_K, tm), cols),
        out_shape=jax.ShapeDtypeStruct((TOP_K, n_rows), jnp.int32),
        compiler_params=_cparams(("arbitrary",)),
        name="slots",
    )(eidx, rank, cnt)


SC_WINDOW = 128


def _sc_mesh():
    return plsc.VectorSubcoreMesh(core_axis_name="core", subcore_axis_name="subcore")


def _sc_dispatch(rows, slot_t, n_out):
    n_rows, width = rows.shape
    n_chunks = n_rows // SC_WINDOW
    info = plsc.get_sparse_core_info()
    n_workers = info.num_cores * info.num_subcores

    @functools.partial(
        pl.kernel, mesh=_sc_mesh(),
        out_type=jax.ShapeDtypeStruct((n_out, width), rows.dtype),
        scratch_types=[pltpu.VMEM((TOP_K, SC_WINDOW), jnp.int32), pltpu.VMEM((SC_WINDOW, width), rows.dtype)],
        name="moe_dispatch")
    def run(rows_hbm, idx_hbm, out_hbm, idx_v, rows_v):
        wid = lax.axis_index("subcore") * info.num_cores + lax.axis_index("core")

        @pl.loop(wid, n_chunks, step=n_workers)
        def _(c):
            r0 = pl.multiple_of(c * SC_WINDOW, SC_WINDOW)
            pltpu.sync_copy(idx_hbm.at[:, pl.ds(r0, SC_WINDOW)], idx_v)
            pltpu.sync_copy(rows_hbm.at[pl.ds(r0, SC_WINDOW)], rows_v)
            for k in range(TOP_K):
                pltpu.sync_copy(rows_v, out_hbm.at[idx_v.at[k]])

    return run(rows, slot_t)


def _sc_collect(rows, slot_t):
    n_picks, n_rows = slot_t.shape
    width = rows.shape[1]
    n_chunks = n_rows // SC_WINDOW
    info = plsc.get_sparse_core_info()
    n_workers = info.num_cores * info.num_subcores

    @functools.partial(
        pl.kernel, mesh=_sc_mesh(),
        out_type=jax.ShapeDtypeStruct((n_picks, n_rows, width), rows.dtype),
        scratch_types=[pltpu.VMEM((TOP_K, SC_WINDOW), jnp.int32), pltpu.VMEM((SC_WINDOW, width), rows.dtype)],
        name="moe_collect")
    def run(rows_hbm, idx_hbm, out_hbm, idx_v, rows_v):
        wid = lax.axis_index("subcore") * info.num_cores + lax.axis_index("core")

        @pl.loop(wid, n_chunks, step=n_workers)
        def _(c):
            r0 = pl.multiple_of(c * SC_WINDOW, SC_WINDOW)
            pltpu.sync_copy(idx_hbm.at[:, pl.ds(r0, SC_WINDOW)], idx_v)
            for k in range(TOP_K):
                pltpu.sync_copy(rows_hbm.at[idx_v.at[k]], rows_v)
                pltpu.sync_copy(rows_v, out_hbm.at[k, pl.ds(r0, SC_WINDOW)])

    return run(rows, slot_t)


EXPERT_TILE = 512


def _work_items(cnt, n_slots):
    counts = cnt[:, 0].astype(jnp.int32)
    ends = jnp.cumsum(counts)
    n_tiles = n_slots // EXPERT_TILE
    bounds = jnp.sort(jnp.concatenate([jnp.arange(n_tiles, dtype=jnp.int32) * EXPERT_TILE, ends - counts]))
    nxt = jnp.concatenate([bounds[1:], jnp.array([n_slots], jnp.int32)])
    tile = jnp.minimum(bounds // EXPERT_TILE, n_tiles - 1)
    expert = jnp.sum((ends[None, :] <= bounds[:, None]).astype(jnp.int32), axis=1)
    expert = jnp.minimum(expert, N_EXPERTS - 1)
    return tile, expert, bounds - tile * EXPERT_TILE, nxt - tile * EXPERT_TILE


def _experts_kernel(tile_ref, exp_ref, lo_ref, hi_ref, xs_ref, wg_ref, wu_ref, wd_ref, ys_ref,
                    acc_ref, wgb_ref, wub_ref, wdb_ref):
    i = pl.program_id(0)
    lo, hi = lo_ref[i], hi_ref[i]

    @pl.when((i == 0) | (exp_ref[i] != exp_ref[jnp.maximum(i - 1, 0)]))
    def _():
        wgb_ref[...] = wg_ref[...].astype(BF16)
        wub_ref[...] = wu_ref[...].astype(BF16)
        wdb_ref[...] = wd_ref[...].astype(BF16)

    def ffn():
        x_lo, x_hi = _unpack_halves(xs_ref[...])
        x_lo, x_hi = x_lo.astype(BF16), x_hi.astype(BF16)
        n = x_lo.shape[1]
        a = _dot(x_lo, wgb_ref[0:n, :]) + _dot(x_hi, wgb_ref[n:, :])
        u = _dot(x_lo, wub_ref[0:n, :]) + _dot(x_hi, wub_ref[n:, :])
        return _dot(((a * jax.nn.sigmoid(a)) * u).astype(BF16), wdb_ref[...])

    whole = (lo == 0) & (hi == EXPERT_TILE)

    @pl.when(whole)
    def _():
        ys_ref[...] = _pack_halves(ffn())

    @pl.when(jnp.logical_not(whole) & (hi > lo))
    def _():
        y = ffn()
        row = lax.broadcasted_iota(jnp.int32, y.shape, 0)
        y = jnp.where((row >= lo) & (row < hi), y, 0.0)

        @pl.when(lo == 0)
        def _():
            acc_ref[...] = y

        @pl.when((lo > 0) & (hi < EXPERT_TILE))
        def _():
            acc_ref[...] += y

        @pl.when((lo > 0) & (hi == EXPERT_TILE))
        def _():
            ys_ref[...] = _pack_halves(acc_ref[...] + y)


def _experts(xs, items, layer, exp_wg, exp_wu, exp_wd):
    n_slots, half = xs.shape
    d, hid = exp_wg.shape[-2:]
    tile, expert, lo, hi = items
    grid_spec = pltpu.PrefetchScalarGridSpec(
        num_scalar_prefetch=4,
        grid=(tile.shape[0],),
        in_specs=[pl.BlockSpec((EXPERT_TILE, half), lambda i, t, e, lo, hi: (t[i], 0)),
                  pl.BlockSpec((None, None, d, hid), lambda i, t, e, lo, hi: (layer, e[i], 0, 0)),
                  pl.BlockSpec((None, None, d, hid), lambda i, t, e, lo, hi: (layer, e[i], 0, 0)),
                  pl.BlockSpec((None, None, hid, d), lambda i, t, e, lo, hi: (layer, e[i], 0, 0))],
        out_specs=pl.BlockSpec((EXPERT_TILE, half), lambda i, t, e, lo, hi: (t[i], 0)),
        scratch_shapes=[pltpu.VMEM((EXPERT_TILE, d), F32), pltpu.VMEM((d, hid), BF16),
                        pltpu.VMEM((d, hid), BF16), pltpu.VMEM((hid, d), BF16)])
    return pl.pallas_call(
        _experts_kernel,
        grid_spec=grid_spec,
        out_shape=jax.ShapeDtypeStruct((n_slots, half), jnp.uint32),
        compiler_params=_cparams(("arbitrary",)),
        name="experts",
    )(tile, expert, lo, hi, xs, exp_wg, exp_wu, exp_wd)


def _moe_out_kernel(yg_ref, w_ref, v_ref, sg_ref, su_ref, sd_ref, h1_ref, mod_ref, gpost_ref, o_ref,
                    *, tiles_per_batch, n_batch, d):
    i = pl.program_id(0)
    x = v_ref[...]
    a = _dot(x, sg_ref[...])
    f = _dot(((a * jax.nn.sigmoid(a)) * _dot(x, su_ref[...])).astype(BF16), sd_ref[...])
    n = d // 2
    f_lo, f_hi = f[:, :n], f[:, n:]
    w = w_ref[...]
    for k in range(TOP_K):
        y_lo, y_hi = _unpack_halves(yg_ref[k])
        wk = w[:, k:k + 1]
        f_lo = f_lo + wk * y_lo
        f_hi = f_hi + wk * y_hi
    f = jnp.concatenate([f_lo, f_hi], axis=1)
    bi = jnp.minimum(i // tiles_per_batch, n_batch)
    g2 = mod_ref[pl.ds(bi, 1), 5 * d:6 * d]
    o_ref[...] = h1_ref[...] + g2 * _rms(f, gpost_ref[...])


def _moe_out(yg, w, v, sg, su, sd, h1, mods, gpost, *, n_batch, seq):
    n_rows, d = v.shape
    rows = lambda i: (i, 0)
    const = lambda i: (0, 0)
    return pl.pallas_call(
        functools.partial(_moe_out_kernel, tiles_per_batch=seq // TM, n_batch=n_batch, d=d),
        grid=(n_rows // TM,),
        in_specs=[pl.BlockSpec((TOP_K, TM, d // 2), lambda i: (0, i, 0)), pl.BlockSpec((TM, TOP_K), rows),
                  pl.BlockSpec((TM, d), rows),
                  pl.BlockSpec(sg.shape, const), pl.BlockSpec(su.shape, const), pl.BlockSpec(sd.shape, const),
                  pl.BlockSpec((TM, d), rows), pl.BlockSpec(mods.shape, const), pl.BlockSpec((1, d), const)],
        out_specs=pl.BlockSpec((TM, d), rows),
        out_shape=jax.ShapeDtypeStruct((n_rows, d), F32),
        compiler_params=_cparams(("arbitrary",)),
        name="moe_out",
    )(yg, w, v, sg, su, sd, h1, mods, gpost)


def _moe(v, vp, layer, rw, rb, exp_wg, exp_wu, exp_wd, sg, su, sd, h1, mods, gpost, *, n_batch, seq):
    n_rows = v.shape[0]
    eidx, rank, w_t, cnt = _router(v, rw, rb)
    slot_t = _slots(eidx, rank, cnt)
    n_slots = n_rows * TOP_K
    xs = _sc_dispatch(vp, slot_t, n_slots)
    ys = _experts(xs, _work_items(cnt, n_slots), layer, exp_wg, exp_wu, exp_wd)
    yg = _sc_collect(ys, slot_t)
    return _moe_out(yg, w_t.T, v, sg, su, sd, h1, mods, gpost, n_batch=n_batch, seq=seq)


def _rope_tables(seq):
    rows = seq // GRID_W
    row_id = jnp.repeat(jnp.arange(rows, dtype=F32), GRID_W)
    col_id = jnp.tile(jnp.arange(GRID_W, dtype=F32), rows)

    def tables(rot_dim):
        axis_dim = rot_dim // 2
        inv_freq = ROPE_BASE ** (-jnp.arange(0, axis_dim, 2, dtype=F32) / axis_dim)
        ang_r = row_id[:, None] * inv_freq[None, :]
        ang_c = col_id[:, None] * inv_freq[None, :]
        cos = jnp.concatenate([jnp.cos(ang_r), jnp.cos(ang_r), jnp.cos(ang_c), jnp.cos(ang_c)], axis=1)
        sin = jnp.concatenate([-jnp.sin(ang_r), jnp.sin(ang_r), -jnp.sin(ang_c), jnp.sin(ang_c)], axis=1)
        return cos, sin

    cos64, sin64 = tables(GQA_DIM)
    cos32, sin32 = tables(MLA_ROPE)
    ones = jnp.ones((seq, MLA_NOPE), F32)
    pad = LANES - MLA_NOPE - MLA_ROPE
    cospe = jnp.concatenate([ones, cos32, jnp.ones((seq, pad), F32)], axis=1)
    sinpe = jnp.concatenate([0 * ones, sin32, jnp.zeros((seq, pad), F32)], axis=1)
    tab = jnp.concatenate([cos64, cos64, sin64, sin64, cospe, sinpe], axis=1)
    ident = jnp.concatenate([jnp.ones((TM, LANES), F32), jnp.zeros((TM, LANES), F32),
                             jnp.ones((TM, LANES), F32), jnp.zeros((TM, LANES), F32)], axis=1)
    return jnp.concatenate([tab, ident], axis=0)


def _pack_w_in(w):
    d = w.shape[0]
    sizes = (MLA_Q_LORA, MLA_KV_LORA, MLA_ROPE, GQA_HEADS * GQA_DIM, GQA_KV_HEADS * GQA_DIM,
             GQA_KV_HEADS * GQA_DIM, RET_HEADS * RET_QK, RET_HEADS * RET_QK, RET_HEADS * RET_V,
             RET_HEADS * RET_V, 3 * d)
    offs, parts = 0, []
    for s in sizes:
        parts.append(w[:, offs:offs + s])
        offs += s
    cq, ckv, kpe, gq, gk, gv, rq, rk, rv, rg, gates = parts

    def twice(m):
        heads = [m[:, i * GQA_DIM:(i + 1) * GQA_DIM] for i in range(GQA_KV_HEADS)]
        return jnp.concatenate([hh for hd in heads for hh in (hd, hd)], axis=1)

    kpe_slab = jnp.concatenate([jnp.zeros((d, MLA_NOPE), F32), kpe,
                                jnp.zeros((d, LANES - MLA_NOPE - MLA_ROPE), F32)], axis=1)
    packed = jnp.concatenate([cq, ckv, kpe_slab, gq * (GQA_DIM ** -0.5 * LOG2_E), twice(gk), twice(gv),
                              rq, rk * RET_QK ** -0.5, rv, rg, gates], axis=1)
    assert packed.shape[1] == W_COLS
    return packed.astype(BF16)


def _pack_mla_up(w_uq, w_ukv):
    r = w_uq.shape[0]
    dq = MLA_NOPE + MLA_ROPE
    wq = jnp.pad(w_uq.reshape(r, MLA_HEADS, dq), ((0, 0), (0, 0), (0, LANES - dq))).reshape(r, MLA_HEADS * LANES)
    kv = w_ukv.reshape(r, MLA_HEADS, MLA_NOPE + MLA_V)
    wk = jnp.pad(kv[:, :, :MLA_NOPE], ((0, 0), (0, 0), (0, LANES - MLA_NOPE))).reshape(r, MLA_HEADS * LANES)
    wv = kv[:, :, MLA_NOPE:]
    zeros = jnp.zeros_like(wv)
    even = jnp.concatenate([wv, zeros], axis=2)
    odd = jnp.concatenate([zeros, wv], axis=2)
    wv = jnp.where((jnp.arange(MLA_HEADS) % 2 == 0)[None, :, None], even, odd).reshape(r, MLA_HEADS * LANES)
    return wq.astype(BF16), wk.astype(BF16), wv.astype(BF16)


def kernel(x, c, ctx, c_ctx, ada_w, ada_b, norm_mix_pre, norm_mix_post, norm_ffn_pre, norm_ffn_post, w_in, mla_q_norm, mla_w_uq, mla_kv_norm, mla_w_ukv, gqa_sink, ret_decay_fwd, ret_decay_bwd, w_br_mla, w_br_gqa, w_br_ret, w_out, router_w, router_bias, exp_w_gate, exp_w_up, exp_w_down, shared_w_gate, shared_w_up, shared_w_down):
    n_batch, seq, d = x.shape
    n_ctx = ctx.shape[1]
    depth = ada_w.shape[0]
    n_lat_rows = n_batch * seq
    assert seq % TM == 0 and n_ctx == TM and n_batch < MOD_ROWS and seq % GRID_W == 0

    cond = jnp.zeros((MOD_ROWS, d), F32).at[:n_batch].set(c).at[n_batch].set(c_ctx)
    mods_all = _adaln(cond, ada_w, ada_b)
    rope = _rope_tables(seq)
    h = jnp.concatenate([x.reshape(n_lat_rows, d), ctx.reshape(n_batch * n_ctx, d)], axis=0)
    row = lambda p: p.reshape(1, -1)
    dims = dict(n_batch=n_batch, seq=seq)

    for l in range(depth):
        last = l == depth - 1
        mods = mods_all[l]
        wq, wk, wv = _pack_mla_up(mla_w_uq[l], mla_w_ukv[l])
        mq, mk, mv, gqa, ret, gates = _inproj(h, mods, row(norm_mix_pre[l]), _pack_w_in(w_in[l]), rope,
                                              row(mla_q_norm[l]), row(mla_kv_norm[l]), wq, wk, wv, **dims)
        a = _mla_attention(mq, mk, mv, ctx=n_ctx, with_ctx_queries=not last, **dims)
        sink_tab = jnp.broadcast_to(gqa_sink[l].astype(F32)[:, None] * LOG2_E, (GQA_HEADS, LANES))
        w = _window_attention(gqa, sink_tab, ctx=n_ctx, with_ctx_queries=not last, **dims)
        lg = jnp.concatenate([jax.nn.log_sigmoid(ret_decay_fwd[l].astype(F32)),
                              jax.nn.log_sigmoid(ret_decay_bwd[l].astype(F32))])
        o_f, o_b = _retention(ret, jnp.broadcast_to(lg[:, None], (2 * RET_HEADS, LANES)), ctx=n_ctx, **dims)
        n_rows = n_lat_rows if last else h.shape[0]
        h1, v, vp = _merge(a, w, o_f, o_b, ret, gates, h, mods, row(norm_mix_post[l]), row(norm_ffn_pre[l]),
                           w_br_mla[l].astype(BF16), w_br_gqa[l].astype(BF16), w_br_ret[l].astype(BF16),
                           w_out[l].astype(BF16), n_rows=n_rows, **dims)
        h = _moe(v, vp, l, router_w[l].T.astype(BF16), router_bias[l].astype(F32).reshape(-1, 1),
                 exp_w_gate, exp_w_up, exp_w_down, shared_w_gate[l].astype(BF16),
                 shared_w_up[l].astype(BF16), shared_w_down[l].astype(BF16), h1, mods,
                 row(norm_ffn_post[l]), **dims)
    return h[:n_lat_rows].reshape(n_batch, seq, d)
```

```python
import functools

import jax
import jax.numpy as jnp
from jax import lax
from jax.experimental import pallas as pl
from jax.experimental.pallas import tpu as pltpu
from jax.experimental.pallas import tpu_sc as plsc

F32 = jnp.float32
BF16 = jnp.bfloat16

GRID_W = 64
ROPE_BASE = 10000.0
NORM_EPS = 1e-6
NEG_INF = -1e30
LOG2_E = 1.4426950408889634
N_MOD = 6
MLA_HEADS, MLA_NOPE, MLA_ROPE, MLA_V = 8, 64, 32, 64
MLA_Q_LORA, MLA_KV_LORA = 256, 256
GQA_HEADS, GQA_KV_HEADS, GQA_DIM, WINDOW = 8, 2, 64, 128
RET_HEADS, RET_QK, RET_V, RET_CHUNK = 4, 64, 128, 128
N_EXPERTS, N_EXPERT_GROUPS, TOPK_GROUPS, TOP_K = 64, 8, 4, 8
EXPERTS_PER_GROUP = N_EXPERTS // N_EXPERT_GROUPS
ROUTED_SCALE = 2.5

LANES = 128
TM = 256
MOD_ROWS = 8
V7X_VMEM_LIMIT = 56 * 1024 * 1024

C_CQ, C_CKV, C_KPE = 0, 256, 512
C_G = 640
C_R = 1664
C_GATE = 3200
W_COLS = 6272


def _cparams(sem):
    return pltpu.CompilerParams(dimension_semantics=sem, vmem_limit_bytes=V7X_VMEM_LIMIT)


def _rms(x, g):
    return x * lax.rsqrt(jnp.mean(x * x, axis=-1, keepdims=True) + NORM_EPS) * g


def _sigmoid(x):
    return 0.5 * jnp.tanh(0.5 * x) + 0.5


def _dot(a, b):
    return jnp.dot(a, b, preferred_element_type=F32)


def _dot_nt(a, b):
    return lax.dot_general(a, b, (((1,), (1,)), ((), ())), preferred_element_type=F32)


def _dot_tn(a, b):
    return lax.dot_general(a, b, (((0,), (0,)), ((), ())), preferred_element_type=F32)


def _rope(x, cos, sin, half):
    n = x.shape[-1]
    reps = n // LANES
    if reps > 1:
        cos = jnp.concatenate([cos] * reps, axis=1)
        sin = jnp.concatenate([sin] * reps, axis=1)
    lane = lax.broadcasted_iota(jnp.int32, x.shape, 1)
    up = pltpu.roll(x, half, 1)
    dn = pltpu.roll(x, n - half, 1)
    partner = jnp.where((lane & (2 * half - 1)) < half, dn, up)
    return x * cos + partner * sin


def _lane_lo(shape):
    return (lax.broadcasted_iota(jnp.int32, shape, 1) & (LANES - 1)) < (LANES // 2)


def _pack_halves(x):
    n = x.shape[1] // 2
    bits = lambda t: lax.bitcast_convert_type(t.astype(BF16).astype(F32), jnp.uint32)
    return (bits(x[:, :n]) >> 16) | bits(x[:, n:])


def _unpack_halves(p):
    lo = lax.bitcast_convert_type(p << 16, F32)
    hi = lax.bitcast_convert_type(p & jnp.uint32(0xFFFF0000), F32)
    return lo, hi


def _ada_kernel(c_ref, w_ref, b_ref, o_ref):
    c = c_ref[...]
    s = c * _sigmoid(c)
    o_ref[...] = _dot(s.astype(BF16), w_ref[...].astype(BF16)) + b_ref[...]


def _adaln(cond, ada_w, ada_b):
    n_layers, d, n = ada_w.shape
    tn = 1024
    return pl.pallas_call(
        _ada_kernel,
        grid=(n_layers, n // tn),
        in_specs=[pl.BlockSpec((MOD_ROWS, d), lambda l, j: (0, 0)),
                  pl.BlockSpec((None, d, tn), lambda l, j: (l, 0, j)),
                  pl.BlockSpec((None, 1, tn), lambda l, j: (l, 0, j))],
        out_specs=pl.BlockSpec((None, MOD_ROWS, tn), lambda l, j: (l, 0, j)),
        out_shape=jax.ShapeDtypeStruct((n_layers, MOD_ROWS, n), F32),
        compiler_params=_cparams(("arbitrary", "arbitrary")),
        name="adaln",
    )(cond, ada_w, ada_b.reshape(n_layers, 1, n))


def _inproj_kernel(h_ref, mod_ref, gpre_ref, w_ref, rope_ref, qn_ref, kvn_ref, wuq_ref, wuk_ref, wuv_ref,
                   mq_ref, mk_ref, mv_ref, gqa_ref, ret_ref, gate_ref, *, tiles_per_batch, n_batch, d):
    i = pl.program_id(0)
    bi = jnp.minimum(i // tiles_per_batch, n_batch)
    sh = mod_ref[pl.ds(bi, 1), 0:d]
    sc = mod_ref[pl.ds(bi, 1), d:2 * d]
    u = (_rms(h_ref[...], gpre_ref[...]) * (1.0 + sc) + sh).astype(BF16)

    cos64 = rope_ref[:, 0:LANES]
    sin64 = rope_ref[:, LANES:2 * LANES]
    cospe = rope_ref[:, 2 * LANES:3 * LANES]
    sinpe = rope_ref[:, 3 * LANES:4 * LANES]

    c = _dot(u, w_ref[:, C_CQ:C_G])
    kpe = _rope(c[:, C_KPE:C_G], cospe, sinpe, MLA_ROPE // 4)
    qn = _rms(c[:, C_CQ:C_CKV], qn_ref[...]).astype(BF16)
    q = _rope(_dot(qn, wuq_ref[...]), cospe, sinpe, MLA_ROPE // 4)
    mq_ref[...] = (q * ((MLA_NOPE + MLA_ROPE) ** -0.5 * LOG2_E)).astype(mq_ref.dtype)
    kvn = _rms(c[:, C_CKV:C_KPE], kvn_ref[...]).astype(BF16)
    k = _dot(kvn, wuk_ref[...]) + jnp.concatenate([kpe] * MLA_HEADS, axis=1)
    mk_ref[...] = k.astype(mk_ref.dtype)
    v = _dot(kvn, wuv_ref[...])
    lane = lax.broadcasted_iota(jnp.int32, v.shape, 1)
    value_lane = ((lane & (LANES - 1)) < MLA_V) == (((lane >> (LANES.bit_length() - 1)) & 1) == 0)
    mv_ref[...] = jnp.where(value_lane, v, 1.0).astype(mv_ref.dtype)

    g = _dot(u, w_ref[:, C_G:C_R])
    n_qk = GQA_HEADS * GQA_DIM + 2 * GQA_KV_HEADS * GQA_DIM
    gqa_ref[:, 0:n_qk] = _rope(g[:, 0:n_qk], cos64, sin64, GQA_DIM // 4).astype(gqa_ref.dtype)
    gqa_ref[:, n_qk:] = g[:, n_qk:].astype(gqa_ref.dtype)

    r = _dot(u, w_ref[:, C_R:C_GATE])
    n_qk = 2 * RET_HEADS * RET_QK
    ret_ref[:, 0:n_qk] = _rope(r[:, 0:n_qk], cos64, sin64, RET_QK // 4)
    ret_ref[:, n_qk:] = r[:, n_qk:]

    gate_ref[...] = _dot(u, w_ref[:, C_GATE:W_COLS]).astype(gate_ref.dtype)


def _inproj(h, mods, gpre, w_all, rope, qn, kvn, wuq, wuk, wuv, *, n_batch, seq):
    t, d = h.shape
    tiles_per_batch = seq // TM
    n_lat_tiles = n_batch * tiles_per_batch
    const = lambda i: (0, 0)
    rows = lambda i: (i, 0)
    rope_idx = lambda i: (jnp.where(i < n_lat_tiles, i % tiles_per_batch, tiles_per_batch), 0)
    hq = MLA_HEADS * LANES
    outs = [jax.ShapeDtypeStruct((t, hq), BF16), jax.ShapeDtypeStruct((t, hq), BF16),
            jax.ShapeDtypeStruct((t, hq), BF16),
            jax.ShapeDtypeStruct((t, C_R - C_G), BF16),
            jax.ShapeDtypeStruct((t, C_GATE - C_R), F32),
            jax.ShapeDtypeStruct((t, W_COLS - C_GATE), BF16)]
    return pl.pallas_call(
        functools.partial(_inproj_kernel, tiles_per_batch=tiles_per_batch, n_batch=n_batch, d=d),
        grid=(t // TM,),
        in_specs=[pl.BlockSpec((TM, d), rows),
                  pl.BlockSpec(mods.shape, const),
                  pl.BlockSpec((1, d), const),
                  pl.BlockSpec(w_all.shape, const),
                  pl.BlockSpec((TM, 4 * LANES), rope_idx),
                  pl.BlockSpec(qn.shape, const), pl.BlockSpec(kvn.shape, const),
                  pl.BlockSpec(wuq.shape, const), pl.BlockSpec(wuk.shape, const), pl.BlockSpec(wuv.shape, const)],
        out_specs=[pl.BlockSpec((TM, o.shape[1]), rows) for o in outs],
        out_shape=outs,
        compiler_params=_cparams(("arbitrary",)),
        name="inproj",
    )(h, mods, gpre, w_all, rope, qn, kvn, wuq, wuk, wuv)


MLA_HEADS_PER_STEP = 4


def _mla_kernel(q_ref, kl_ref, kc_ref, vl_ref, vc_ref, o_ref, s_ref, p_ref, *, n_lat_tiles, has_ctx_tile):
    n_ctx = kc_ref.shape[0]

    def body(with_lat):
        n_keys = n_ctx + (kl_ref.shape[0] if with_lat else 0)

        def scores(h):
            sl = slice(h * LANES, (h + 1) * LANES)
            s_ref[h % 2, :, 0:n_ctx] = _dot_nt(q_ref[:, sl], kc_ref[:, sl])
            if with_lat:
                s_ref[h % 2, :, n_ctx:n_keys] = _dot_nt(q_ref[:, sl], kl_ref[:, sl])

        def probs(h):
            s = s_ref[h % 2, :, 0:n_keys]
            p_ref[h % 2, :, 0:n_keys] = jnp.exp2(s - jnp.max(s, axis=-1, keepdims=True)).astype(BF16)

        def weighted(h):
            sl = slice(h * LANES, (h + 1) * LANES)
            o = _dot(p_ref[h % 2, :, 0:n_ctx], vc_ref[:, sl])
            if with_lat:
                o = o + _dot(p_ref[h % 2, :, n_ctx:n_keys], vl_ref[:, sl])
            return o / pltpu.roll(o, LANES // 2, 1)

        outs = [None] * MLA_HEADS_PER_STEP
        scores(0)
        for h in range(MLA_HEADS_PER_STEP):
            if h + 1 < MLA_HEADS_PER_STEP:
                scores(h + 1)
            probs(h)
            outs[h] = weighted(h)
        for pr in range(MLA_HEADS_PER_STEP // 2):
            even, odd = outs[2 * pr], outs[2 * pr + 1]
            o_ref[:, pr * LANES:(pr + 1) * LANES] = jnp.where(_lane_lo(even.shape), even, odd).astype(o_ref.dtype)

    if has_ctx_tile:
        i = pl.program_id(2)
        pl.when(i < n_lat_tiles)(lambda: body(True))
        pl.when(i == n_lat_tiles)(lambda: body(False))
    else:
        body(True)


def _mla_attention(mq, mk, mv, *, n_batch, seq, ctx, with_ctx_queries):
    t = mq.shape[0]
    nq = seq // TM
    n_lat_blocks = n_batch * nq
    ctx_blk0 = n_batch * seq // ctx
    hps = MLA_HEADS_PER_STEP
    assert ctx == TM
    q_idx = lambda b, g, i: (jnp.where(i < nq, b * nq + i, n_lat_blocks + b), g)
    lat_idx = lambda b, g, i: (b, g)
    ctx_idx = lambda b, g, i: (ctx_blk0 + b, g)
    n_rows = t if with_ctx_queries else n_batch * seq
    return pl.pallas_call(
        functools.partial(_mla_kernel, n_lat_tiles=nq, has_ctx_tile=with_ctx_queries),
        grid=(n_batch, MLA_HEADS // hps, nq + (1 if with_ctx_queries else 0)),
        in_specs=[pl.BlockSpec((TM, hps * LANES), q_idx),
                  pl.BlockSpec((seq, hps * LANES), lat_idx),
                  pl.BlockSpec((ctx, hps * LANES), ctx_idx),
                  pl.BlockSpec((seq, hps * LANES), lat_idx),
                  pl.BlockSpec((ctx, hps * LANES), ctx_idx)],
        out_specs=pl.BlockSpec((TM, hps * MLA_V), q_idx),
        out_shape=jax.ShapeDtypeStruct((n_rows, MLA_HEADS * MLA_V), BF16),
        scratch_shapes=[pltpu.VMEM((2, TM, ctx + seq), F32), pltpu.VMEM((2, TM, ctx + seq), BF16)],
        compiler_params=_cparams(("arbitrary", "arbitrary", "arbitrary")),
        name="mla_attn",
    )(mq, mk, mk, mv, mv)


def _win_kernel(q_ref, kp_ref, kcur_ref, kn_ref, vp_ref, vcur_ref, vn_ref, kc_ref, vc_ref, sink_ref, o_ref,
                s_ref, p_ref, *, n_lat_tiles, seq, has_ctx_tile):
    i = pl.program_id(1)
    tq = q_ref.shape[0]
    group = GQA_HEADS // GQA_KV_HEADS

    def body(with_lat):
        n_ctx = kc_ref.shape[0]
        n_keys = n_ctx + (tq + 2 * WINDOW if with_lat else 0)
        if with_lat:
            q_pos = i * tq + lax.broadcasted_iota(jnp.int32, (tq, n_keys), 0)
            k_pos = i * tq - WINDOW - n_ctx + lax.broadcasted_iota(jnp.int32, (tq, n_keys), 1)
            in_band = (jnp.abs(q_pos - k_pos) <= WINDOW) & (k_pos >= 0) & (k_pos < seq)
            valid = in_band | (lax.broadcasted_iota(jnp.int32, (tq, n_keys), 1) < n_ctx)
        lo = _lane_lo((tq, LANES))
        lo_k = _lane_lo((n_keys, LANES))
        keys, values = [], []
        for kv in range(GQA_KV_HEADS):
            sl = slice(kv * LANES, (kv + 1) * LANES)
            if with_lat:
                k_all = jnp.concatenate([kc_ref[:, sl], kp_ref[:, sl], kcur_ref[:, sl], kn_ref[:, sl]], axis=0)
                v_all = jnp.concatenate([vc_ref[:, sl], vp_ref[:, sl], vcur_ref[:, sl], vn_ref[:, sl]], axis=0)
            else:
                k_all, v_all = kc_ref[:, sl], vc_ref[:, sl]
            keys.append(k_all)
            one = jnp.ones_like(v_all)
            values.append((jnp.where(lo_k, v_all, one), jnp.where(lo_k, one, v_all)))

        def scores(hd):
            kv, pair = hd // group, hd // 2
            qp = q_ref[:, pair * LANES:(pair + 1) * LANES]
            qm = jnp.where(lo if hd % 2 == 0 else jnp.logical_not(lo), qp, jnp.zeros_like(qp))
            s = _dot_nt(qm, keys[kv])
            s_ref[hd % 2, :, 0:n_keys] = jnp.where(valid, s, NEG_INF) if with_lat else s

        def probs(hd):
            s = s_ref[hd % 2, :, 0:n_keys]
            m = jnp.maximum(jnp.max(s, axis=-1, keepdims=True), sink_ref[hd:hd + 1, 0:1])
            p_ref[hd % 2, :, 0:n_keys] = jnp.exp2(s - m).astype(BF16)
            return jnp.exp2(sink_ref[hd:hd + 1, 0:1] - m)

        def weighted(hd, sink_term):
            o = _dot(p_ref[hd % 2, :, 0:n_keys], values[hd // group][hd % 2])
            return o / (pltpu.roll(o, LANES // 2, 1) + sink_term)

        outs = [None] * GQA_HEADS
        scores(0)
        for hd in range(GQA_HEADS):
            if hd + 1 < GQA_HEADS:
                scores(hd + 1)
            outs[hd] = weighted(hd, probs(hd))
        for pair in range(GQA_HEADS // 2):
            o_ref[:, pair * LANES:(pair + 1) * LANES] = jnp.where(
                lo, outs[2 * pair], outs[2 * pair + 1]).astype(o_ref.dtype)

    if has_ctx_tile:
        pl.when(i < n_lat_tiles)(lambda: body(True))
        pl.when(i == n_lat_tiles)(lambda: body(False))
    else:
        body(True)


def _window_attention(gqa, sink_tab, *, n_batch, seq, ctx, with_ctx_queries):
    t = gqa.shape[0]
    nq = seq // TM
    n_lat_blocks = n_batch * nq
    per_tile = TM // WINDOW
    n_win_blocks = seq // WINDOW
    assert ctx == TM
    ctx_blk0 = n_batch * seq // ctx
    nqk = GQA_HEADS * GQA_DIM
    kw = 2 * GQA_KV_HEADS * GQA_DIM
    k_col, v_col = nqk // kw, nqk // kw + 1
    q_idx = lambda b, i: (jnp.where(i < nq, b * nq + i, n_lat_blocks + b), 0)
    cur = lambda col: (lambda b, i: (b * nq + jnp.minimum(i, nq - 1), col))
    prev = lambda col: (lambda b, i: (b * n_win_blocks + jnp.clip(per_tile * i - 1, 0, n_win_blocks - 1), col))
    nxt = lambda col: (lambda b, i: (b * n_win_blocks + jnp.clip(per_tile * (i + 1), 0, n_win_blocks - 1), col))
    cidx = lambda col: (lambda b, i: (ctx_blk0 + b, col))
    n_rows = t if with_ctx_queries else n_batch * seq
    return pl.pallas_call(
        functools.partial(_win_kernel, n_lat_tiles=nq, seq=seq, has_ctx_tile=with_ctx_queries),
        grid=(n_batch, nq + (1 if with_ctx_queries else 0)),
        in_specs=[pl.BlockSpec((TM, nqk), q_idx),
                  pl.BlockSpec((WINDOW, kw), prev(k_col)), pl.BlockSpec((TM, kw), cur(k_col)),
                  pl.BlockSpec((WINDOW, kw), nxt(k_col)),
                  pl.BlockSpec((WINDOW, kw), prev(v_col)), pl.BlockSpec((TM, kw), cur(v_col)),
                  pl.BlockSpec((WINDOW, kw), nxt(v_col)),
                  pl.BlockSpec((ctx, kw), cidx(k_col)), pl.BlockSpec((ctx, kw), cidx(v_col)),
                  pl.BlockSpec(sink_tab.shape, lambda b, i: (0, 0))],
        out_specs=pl.BlockSpec((TM, nqk), q_idx),
        out_shape=jax.ShapeDtypeStruct((n_rows, nqk), BF16),
        scratch_shapes=[pltpu.VMEM((2, TM, ctx + TM + 2 * WINDOW), F32),
                        pltpu.VMEM((2, TM, ctx + TM + 2 * WINDOW), BF16)],
        compiler_params=_cparams(("arbitrary", "arbitrary")),
        name="win_attn",
    )(gqa, gqa, gqa, gqa, gqa, gqa, gqa, gqa, gqa, sink_tab)


def _ret_kernel(f_ref, b_ref, lg_ref, of_ref, ob_ref, sf_ref, sb_ref, qdec_ref, kdec_ref, cdec_ref, inner_ref):
    @pl.when(pl.program_id(1) == 0)
    def _():
        sf_ref[...] = jnp.zeros_like(sf_ref)
        sb_ref[...] = jnp.zeros_like(sb_ref)

    L = f_ref.shape[0]
    lo = _lane_lo((L, LANES))
    srow_lo = lax.broadcasted_iota(jnp.int32, (LANES, LANES), 0) < RET_QK
    nq = RET_HEADS * RET_QK
    n_pairs = RET_HEADS // 2

    @pl.when(pl.program_id(1) == 0)
    def _():
        ii = lax.broadcasted_iota(jnp.int32, (L, L), 0)
        jj = lax.broadcasted_iota(jnp.int32, (L, L), 1)
        row = lax.broadcasted_iota(jnp.int32, (L, LANES), 0).astype(F32)
        for direction, forward in enumerate((True, False)):
            dist = ii - jj if forward else jj - ii
            distf = jnp.maximum(dist, 0).astype(F32)
            for pr in range(n_pairs):
                r0 = direction * RET_HEADS + 2 * pr
                lg = [lg_ref[r0 + e:r0 + e + 1, :] for e in range(2)]
                lg_lane = jnp.where(lo, lg[0], lg[1])
                qdec_ref[direction * n_pairs + pr] = jnp.exp(lg_lane * ((row + 1.0) if forward else (L - row)))
                kdec_ref[direction * n_pairs + pr] = jnp.exp(lg_lane * ((L - 1.0 - row) if forward else row))
                cdec_ref[direction * n_pairs + pr] = jnp.where(srow_lo, jnp.exp(lg[0] * float(L)),
                                                               jnp.exp(lg[1] * float(L)))
                for e in range(2):
                    inner_ref[r0 + e] = jnp.where(dist >= 0, jnp.exp(lg[e][:, 0:1] * distf), 0.0)

    def scan_chunk(x_ref, o_ref, s_ref, direction):
        for pr in range(n_pairs):
            q = x_ref[:, pr * LANES:(pr + 1) * LANES]
            k = x_ref[:, nq + pr * LANES:nq + (pr + 1) * LANES]
            qd = q * qdec_ref[direction * n_pairs + pr]
            kdb = (k * kdec_ref[direction * n_pairs + pr]).astype(BF16)
            kb = k.astype(BF16)
            state = s_ref[pr]
            state_b = state.astype(BF16)
            upd = []
            for e in range(2):
                hd = 2 * pr + e
                keep = lo if e == 0 else jnp.logical_not(lo)
                v = x_ref[:, 2 * nq + hd * RET_V:2 * nq + (hd + 1) * RET_V].astype(BF16)
                attn = _dot_nt(jnp.where(keep, q, 0.0).astype(BF16), kb) * inner_ref[direction * RET_HEADS + hd]
                o = _dot(attn.astype(BF16), v) + _dot(jnp.where(keep, qd, 0.0).astype(BF16), state_b)
                o_ref[:, hd * RET_V:(hd + 1) * RET_V] = o
                upd.append(_dot_tn(kdb, v))
            s_ref[pr] = state * cdec_ref[direction * n_pairs + pr] + jnp.where(srow_lo, upd[0], upd[1])

    scan_chunk(f_ref, of_ref, sf_ref, 0)
    scan_chunk(b_ref, ob_ref, sb_ref, 1)


def _retention(ret, lg_tab, *, n_batch, seq, ctx):
    t = ret.shape[0]
    L = RET_CHUNK
    n_lat, n_ctx = seq // L, ctx // L
    ctx0 = n_batch * n_lat
    width = 2 * RET_HEADS * RET_QK + RET_HEADS * RET_V
    fwd = lambda b, s: (jnp.where(s < n_ctx, ctx0 + b * n_ctx + s, b * n_lat + s - n_ctx), 0)
    bwd = lambda b, s: (jnp.where(s < n_ctx, ctx0 + b * n_ctx + n_ctx - 1 - s, b * n_lat + n_lat - 1 - (s - n_ctx)), 0)
    out = jax.ShapeDtypeStruct((t, RET_HEADS * RET_V), F32)
    return pl.pallas_call(
        _ret_kernel,
        grid=(n_batch, n_lat + n_ctx),
        in_specs=[pl.BlockSpec((L, width), fwd), pl.BlockSpec((L, width), bwd),
                  pl.BlockSpec(lg_tab.shape, lambda b, s: (0, 0))],
        out_specs=[pl.BlockSpec((L, RET_HEADS * RET_V), fwd), pl.BlockSpec((L, RET_HEADS * RET_V), bwd)],
        out_shape=[out, out],
        scratch_shapes=[pltpu.VMEM((RET_HEADS // 2, LANES, RET_V), F32),
                        pltpu.VMEM((RET_HEADS // 2, LANES, RET_V), F32),
                        pltpu.VMEM((RET_HEADS, L, LANES), F32), pltpu.VMEM((RET_HEADS, L, LANES), F32),
                        pltpu.VMEM((RET_HEADS, LANES, RET_V), F32), pltpu.VMEM((2 * RET_HEADS, L, L), F32)],
        compiler_params=_cparams(("arbitrary", "arbitrary")),
        name="retention",
    )(ret, ret, lg_tab)


def _merge_kernel(a_ref, w_ref, of_ref, ob_ref, rg_ref, gt_ref, h_ref, mod_ref, gpost_ref, gffn_ref,
                  wa_ref, ww_ref, wr_ref, wo_ref, h1_ref, v_ref, vp_ref, *, tiles_per_batch, n_batch, d):
    i = pl.program_id(0)
    bi = jnp.minimum(i // tiles_per_batch, n_batch)
    o = of_ref[...] + ob_ref[...]
    normed = []
    for hd in range(RET_HEADS):
        oh = o[:, hd * RET_V:(hd + 1) * RET_V]
        dev = oh - jnp.mean(oh, axis=-1, keepdims=True)
        normed.append(dev * lax.rsqrt(jnp.mean(dev * dev, axis=-1, keepdims=True) + NORM_EPS))
    g = rg_ref[...]
    r = (g * _sigmoid(g)) * jnp.concatenate(normed, axis=1)
    y = (_sigmoid(gt_ref[:, 0:d].astype(F32)) * _dot(a_ref[...], wa_ref[...])
         + _sigmoid(gt_ref[:, d:2 * d].astype(F32)) * _dot(w_ref[...], ww_ref[...])
         + _sigmoid(gt_ref[:, 2 * d:3 * d].astype(F32)) * _dot(r.astype(BF16), wr_ref[...]))
    z = _dot(y.astype(BF16), wo_ref[...])
    g1 = mod_ref[pl.ds(bi, 1), 2 * d:3 * d]
    sh2 = mod_ref[pl.ds(bi, 1), 3 * d:4 * d]
    sc2 = mod_ref[pl.ds(bi, 1), 4 * d:5 * d]
    h1 = h_ref[...] + g1 * _rms(z, gpost_ref[...])
    h1_ref[...] = h1
    v = _rms(h1, gffn_ref[...]) * (1.0 + sc2) + sh2
    v_ref[...] = v.astype(v_ref.dtype)
    vp_ref[...] = _pack_halves(v)


def _merge(a, w, o_f, o_b, ret, gates, h, mods, gpost, gffn, wa, ww, wr, wo, *, n_rows, n_batch, seq):
    d = h.shape[1]
    rows = lambda i: (i, 0)
    const = lambda i: (0, 0)
    rv = RET_HEADS * RET_V
    rg_col = (2 * RET_HEADS * RET_QK + rv) // rv
    outs = [jax.ShapeDtypeStruct((n_rows, d), F32), jax.ShapeDtypeStruct((n_rows, d), BF16),
            jax.ShapeDtypeStruct((n_rows, d // 2), jnp.uint32)]
    return pl.pallas_call(
        functools.partial(_merge_kernel, tiles_per_batch=seq // TM, n_batch=n_batch, d=d),
        grid=(n_rows // TM,),
        in_specs=[pl.BlockSpec((TM, a.shape[1]), rows), pl.BlockSpec((TM, w.shape[1]), rows),
                  pl.BlockSpec((TM, rv), rows), pl.BlockSpec((TM, rv), rows),
                  pl.BlockSpec((TM, rv), lambda i: (i, rg_col)),
                  pl.BlockSpec((TM, 3 * d), rows), pl.BlockSpec((TM, d), rows),
                  pl.BlockSpec(mods.shape, const), pl.BlockSpec((1, d), const), pl.BlockSpec((1, d), const),
                  pl.BlockSpec(wa.shape, const), pl.BlockSpec(ww.shape, const),
                  pl.BlockSpec(wr.shape, const), pl.BlockSpec(wo.shape, const)],
        out_specs=[pl.BlockSpec((TM, o.shape[1]), rows) for o in outs],
        out_shape=outs,
        compiler_params=_cparams(("arbitrary",)),
        name="merge",
    )(a, w, o_f, o_b, ret, gates, h, mods, gpost, gffn, wa, ww, wr, wo)


def _router_kernel(v_ref, rw_ref, rb_ref, eidx_ref, rank_ref, w_ref, cnt_ref, carry_ref):
    @pl.when(pl.program_id(0) == 0)
    def _():
        carry_ref[...] = jnp.zeros_like(carry_ref)

    tm = v_ref.shape[0]
    scores = _sigmoid(_dot_nt(rw_ref[...], v_ref[...]))
    sel = scores + rb_ref[...]
    neg = -jnp.inf
    n_grp, per = N_EXPERT_GROUPS, EXPERTS_PER_GROUP

    sel3 = sel.reshape(n_grp, per, tm)
    member_id = lax.broadcasted_iota(jnp.int32, sel3.shape, 1)
    m1 = jnp.max(sel3, axis=1, keepdims=True)
    i1 = jnp.min(jnp.where(sel3 == m1, member_id, per), axis=1, keepdims=True)
    m2 = jnp.max(jnp.where(member_id == i1, neg, sel3), axis=1, keepdims=True)
    gscore = (m1 + m2).reshape(n_grp, tm)
    gid = lax.broadcasted_iota(jnp.int32, gscore.shape, 0)
    ahead = jnp.zeros(gscore.shape, jnp.int32)
    for gj in range(n_grp):
        other = gscore[gj:gj + 1, :]
        ahead = ahead + jnp.where((other > gscore) | ((other == gscore) & (gid > gj)), 1, 0)
    group_ok = (ahead < TOPK_GROUPS).reshape(n_grp, 1, tm)
    sel = jnp.where(group_ok, sel3, NEG_INF).reshape(N_EXPERTS, tm)

    eid = lax.broadcasted_iota(jnp.int32, sel.shape, 0)
    chosen = jnp.zeros(sel.shape, jnp.bool_)
    picks = []
    for _ in range(TOP_K):
        m = jnp.max(sel, axis=0, keepdims=True)
        idx = jnp.min(jnp.where(sel == m, eid, N_EXPERTS), axis=0, keepdims=True)
        hit = eid == idx
        chosen = chosen | hit
        sel = jnp.where(hit, neg, sel)
        picks.append(idx)
    w = jnp.where(chosen, scores, 0.0)
    gate = ROUTED_SCALE * w / jnp.sum(w, axis=0, keepdims=True)

    member = jnp.where(chosen, 1.0, 0.0)
    earlier = lax.broadcasted_iota(jnp.int32, (tm, tm), 0) < lax.broadcasted_iota(jnp.int32, (tm, tm), 1)
    pos = _dot(member.astype(BF16), jnp.where(earlier, 1.0, 0.0).astype(BF16)) + carry_ref[...]
    for k, idx in enumerate(picks):
        hit = eid == idx
        eidx_ref[k:k + 1, :] = idx
        rank_ref[k:k + 1, :] = jnp.sum(jnp.where(hit, pos, 0.0), axis=0, keepdims=True)
        w_ref[k:k + 1, :] = jnp.sum(jnp.where(hit, gate, 0.0), axis=0, keepdims=True)
    carry_ref[...] += jnp.sum(member, axis=1, keepdims=True)
    cnt_ref[...] = carry_ref[...]


def _router(v, rw_t, rb):
    n_rows, d = v.shape
    cols = lambda i: (0, i)
    const = lambda i: (0, 0)
    outs = [jax.ShapeDtypeStruct((TOP_K, n_rows), jnp.int32), jax.ShapeDtypeStruct((TOP_K, n_rows), F32),
            jax.ShapeDtypeStruct((TOP_K, n_rows), F32), jax.ShapeDtypeStruct((N_EXPERTS, 1), F32)]
    return pl.pallas_call(
        _router_kernel,
        grid=(n_rows // TM,),
        in_specs=[pl.BlockSpec((TM, d), lambda i: (i, 0)), pl.BlockSpec(rw_t.shape, const),
                  pl.BlockSpec(rb.shape, const)],
        out_specs=[pl.BlockSpec((TOP_K, TM), cols), pl.BlockSpec((TOP_K, TM), cols),
                   pl.BlockSpec((TOP_K, TM), cols), pl.BlockSpec((N_EXPERTS, 1), const)],
        out_shape=outs,
        scratch_shapes=[pltpu.VMEM((N_EXPERTS, 1), F32)],
        compiler_params=_cparams(("arbitrary",)),
        name="router",
    )(v, rw_t, rb)


def _slots_kernel(eidx_ref, rank_ref, cnt_ref, slot_ref):
    tm = eidx_ref.shape[1]
    eid = lax.broadcasted_iota(jnp.int32, (N_EXPERTS, tm), 0)
    for k in range(TOP_K):
        before = jnp.sum(jnp.where(eid < eidx_ref[k:k + 1, :], cnt_ref[...], 0.0), axis=0, keepdims=True)
        slot_ref[k:k + 1, :] = (before + rank_ref[k:k + 1, :]).astype(jnp.int32)


def _slots(eidx, rank, cnt):
    n_rows = eidx.shape[1]
    tm = next(c for c in (2048, 1024, 512, 256) if n_rows % c == 0)
    cols = lambda i: (0, i)
    return pl.pallas_call(
        _slots_kernel,
        grid=(n_rows // tm,),
        in_specs=[pl.BlockSpec((TOP_K, tm), cols), pl.BlockSpec((TOP_K, tm), cols),
                  pl.BlockSpec(cnt.shape, lambda i: (0, 0))],
        out_specs=pl.BlockSpec((TOP_K, tm), cols),
        out_shape=jax.ShapeDtypeStruct((TOP_K, n_rows), jnp.int32),
        compiler_params=_cparams(("arbitrary",)),
        name="slots",
    )(eidx, rank, cnt)


SC_WINDOW = 128


def _sc_mesh():
    return plsc.VectorSubcoreMesh(core_axis_name="core", subcore_axis_name="subcore")


def _sc_dispatch(rows, slot_t, n_out):
    n_rows, width = rows.shape
    n_chunks = n_rows // SC_WINDOW
    info = plsc.get_sparse_core_info()
    n_workers = info.num_cores * info.num_subcores

    @functools.partial(
        pl.kernel, mesh=_sc_mesh(),
        out_type=jax.ShapeDtypeStruct((n_out, width), rows.dtype),
        scratch_types=[pltpu.VMEM((TOP_K, SC_WINDOW), jnp.int32), pltpu.VMEM((SC_WINDOW, width), rows.dtype)],
        name="moe_dispatch")
    def run(rows_hbm, idx_hbm, out_hbm, idx_v, rows_v):
        wid = lax.axis_index("subcore") * info.num_cores + lax.axis_index("core")

        @pl.loop(wid, n_chunks, step=n_workers)
        def _(c):
            r0 = pl.multiple_of(c * SC_WINDOW, SC_WINDOW)
            pltpu.sync_copy(idx_hbm.at[:, pl.ds(r0, SC_WINDOW)], idx_v)
            pltpu.sync_copy(rows_hbm.at[pl.ds(r0, SC_WINDOW)], rows_v)
            for k in range(TOP_K):
                pltpu.sync_copy(rows_v, out_hbm.at[idx_v.at[k]])

    return run(rows, slot_t)


def _sc_collect(rows, slot_t):
    n_picks, n_rows = slot_t.shape
    width = rows.shape[1]
    n_chunks = n_rows // SC_WINDOW
    info = plsc.get_sparse_core_info()
    n_workers = info.num_cores * info.num_subcores

    @functools.partial(
        pl.kernel, mesh=_sc_mesh(),
        out_type=jax.ShapeDtypeStruct((n_picks, n_rows, width), rows.dtype),
        scratch_types=[pltpu.VMEM((TOP_K, SC_WINDOW), jnp.int32), pltpu.VMEM((SC_WINDOW, width), rows.dtype)],
        name="moe_collect")
    def run(rows_hbm, idx_hbm, out_hbm, idx_v, rows_v):
        wid = lax.axis_index("subcore") * info.num_cores + lax.axis_index("core")

        @pl.loop(wid, n_chunks, step=n_workers)
        def _(c):
            r0 = pl.multiple_of(c * SC_WINDOW, SC_WINDOW)
            pltpu.sync_copy(idx_hbm.at[:, pl.ds(r0, SC_WINDOW)], idx_v)
            for k in range(TOP_K):
                pltpu.sync_copy(rows_hbm.at[idx_v.at[k]], rows_v)
                pltpu.sync_copy(rows_v, out_hbm.at[k, pl.ds(r0, SC_WINDOW)])

    return run(rows, slot_t)


EXPERT_TILE = 512


def _work_items(cnt, n_slots):
    counts = cnt[:, 0].astype(jnp.int32)
    ends = jnp.cumsum(counts)
    n_tiles = n_slots // EXPERT_TILE
    bounds = jnp.sort(jnp.concatenate([jnp.arange(n_tiles, dtype=jnp.int32) * EXPERT_TILE, ends - counts]))
    nxt = jnp.concatenate([bounds[1:], jnp.array([n_slots], jnp.int32)])
    tile = jnp.minimum(bounds // EXPERT_TILE, n_tiles - 1)
    expert = jnp.sum((ends[None, :] <= bounds[:, None]).astype(jnp.int32), axis=1)
    expert = jnp.minimum(expert, N_EXPERTS - 1)
    return tile, expert, bounds - tile * EXPERT_TILE, nxt - tile * EXPERT_TILE


def _experts_kernel(tile_ref, exp_ref, lo_ref, hi_ref, xs_ref, wg_ref, wu_ref, wd_ref, ys_ref,
                    acc_ref, wgb_ref, wub_ref, wdb_ref):
    i = pl.program_id(0)
    lo, hi = lo_ref[i], hi_ref[i]

    @pl.when((i == 0) | (exp_ref[i] != exp_ref[jnp.maximum(i - 1, 0)]))
    def _():
        wgb_ref[...] = wg_ref[...].astype(BF16)
        wub_ref[...] = wu_ref[...].astype(BF16)
        wdb_ref[...] = wd_ref[...].astype(BF16)

    def ffn(rows):
        x_lo, x_hi = _unpack_halves(xs_ref[rows, :])
        x_lo, x_hi = x_lo.astype(BF16), x_hi.astype(BF16)
        n = x_lo.shape[1]
        a = _dot(x_lo, wgb_ref[0:n, :]) + _dot(x_hi, wgb_ref[n:, :])
        u = _dot(x_lo, wub_ref[0:n, :]) + _dot(x_hi, wub_ref[n:, :])
        return _dot(((a * _sigmoid(a)) * u).astype(BF16), wdb_ref[...])

    blocks = [slice(r, r + EXPERT_TILE // 2) for r in range(0, EXPERT_TILE, EXPERT_TILE // 2)]
    whole = (lo == 0) & (hi == EXPERT_TILE)

    @pl.when(whole)
    def _():
        for rows in blocks:
            ys_ref[rows, :] = _pack_halves(ffn(rows))

    @pl.when(jnp.logical_not(whole) & (hi > lo))
    def _():
        for rows in blocks:
            touched = (hi > rows.start) & (lo < rows.stop)

            @pl.when(touched)
            def _():
                y = ffn(rows)
                row = rows.start + lax.broadcasted_iota(jnp.int32, y.shape, 0)
                y = jnp.where((row >= lo) & (row < hi), y, 0.0)

                @pl.when(lo == 0)
                def _():
                    acc_ref[rows, :] = y

                @pl.when((lo > 0) & (hi < EXPERT_TILE))
                def _():
                    acc_ref[rows, :] += y

                @pl.when((lo > 0) & (hi == EXPERT_TILE))
                def _():
                    ys_ref[rows, :] = _pack_halves(acc_ref[rows, :] + y)

            @pl.when(jnp.logical_not(touched) & (lo == 0))
            def _():
                acc_ref[rows, :] = jnp.zeros((rows.stop - rows.start, acc_ref.shape[1]), F32)

            @pl.when(jnp.logical_not(touched) & (lo > 0) & (hi == EXPERT_TILE))
            def _():
                ys_ref[rows, :] = _pack_halves(acc_ref[rows, :])


def _experts(xs, items, layer, exp_wg, exp_wu, exp_wd):
    n_slots, half = xs.shape
    d, hid = exp_wg.shape[-2:]
    tile, expert, lo, hi = items
    grid_spec = pltpu.PrefetchScalarGridSpec(
        num_scalar_prefetch=4,
        grid=(tile.shape[0],),
        in_specs=[pl.BlockSpec((EXPERT_TILE, half), lambda i, t, e, lo, hi: (t[i], 0)),
                  pl.BlockSpec((None, None, d, hid), lambda i, t, e, lo, hi: (layer, e[i], 0, 0)),
                  pl.BlockSpec((None, None, d, hid), lambda i, t, e, lo, hi: (layer, e[i], 0, 0)),
                  pl.BlockSpec((None, None, hid, d), lambda i, t, e, lo, hi: (layer, e[i], 0, 0))],
        out_specs=pl.BlockSpec((EXPERT_TILE, half), lambda i, t, e, lo, hi: (t[i], 0)),
        scratch_shapes=[pltpu.VMEM((EXPERT_TILE, d), F32), pltpu.VMEM((d, hid), BF16),
                        pltpu.VMEM((d, hid), BF16), pltpu.VMEM((hid, d), BF16)])
    return pl.pallas_call(
        _experts_kernel,
        grid_spec=grid_spec,
        out_shape=jax.ShapeDtypeStruct((n_slots, half), jnp.uint32),
        compiler_params=_cparams(("arbitrary",)),
        name="experts",
    )(tile, expert, lo, hi, xs, exp_wg, exp_wu, exp_wd)


def _moe_out_kernel(yg_ref, w_ref, v_ref, sg_ref, su_ref, sd_ref, h1_ref, mod_ref, gpost_ref, o_ref,
                    *, tiles_per_batch, n_batch, d):
    i = pl.program_id(0)
    x = v_ref[...]
    a = _dot(x, sg_ref[...])
    f = _dot(((a * _sigmoid(a)) * _dot(x, su_ref[...])).astype(BF16), sd_ref[...])
    n = d // 2
    f_lo, f_hi = f[:, :n], f[:, n:]
    w = w_ref[...]
    for k in range(TOP_K):
        y_lo, y_hi = _unpack_halves(yg_ref[k])
        wk = w[:, k:k + 1]
        f_lo = f_lo + wk * y_lo
        f_hi = f_hi + wk * y_hi
    f = jnp.concatenate([f_lo, f_hi], axis=1)
    bi = jnp.minimum(i // tiles_per_batch, n_batch)
    g2 = mod_ref[pl.ds(bi, 1), 5 * d:6 * d]
    o_ref[...] = h1_ref[...] + g2 * _rms(f, gpost_ref[...])


def _moe_out(yg, w, v, sg, su, sd, h1, mods, gpost, *, n_batch, seq):
    n_rows, d = v.shape
    rows = lambda i: (i, 0)
    const = lambda i: (0, 0)
    return pl.pallas_call(
        functools.partial(_moe_out_kernel, tiles_per_batch=seq // TM, n_batch=n_batch, d=d),
        grid=(n_rows // TM,),
        in_specs=[pl.BlockSpec((TOP_K, TM, d // 2), lambda i: (0, i, 0)), pl.BlockSpec((TM, TOP_K), rows),
                  pl.BlockSpec((TM, d), rows),
                  pl.BlockSpec(sg.shape, const), pl.BlockSpec(su.shape, const), pl.BlockSpec(sd.shape, const),
                  pl.BlockSpec((TM, d), rows), pl.BlockSpec(mods.shape, const), pl.BlockSpec((1, d), const)],
        out_specs=pl.BlockSpec((TM, d), rows),
        out_shape=jax.ShapeDtypeStruct((n_rows, d), F32),
        compiler_params=_cparams(("arbitrary",)),
        name="moe_out",
    )(yg, w, v, sg, su, sd, h1, mods, gpost)


def _moe(v, vp, layer, rw, rb, exp_wg, exp_wu, exp_wd, sg, su, sd, h1, mods, gpost, *, n_batch, seq):
    n_rows = v.shape[0]
    eidx, rank, w_t, cnt = _router(v, rw, rb)
    slot_t = _slots(eidx, rank, cnt)
    n_slots = n_rows * TOP_K
    xs = _sc_dispatch(vp, slot_t, n_slots)
    ys = _experts(xs, _work_items(cnt, n_slots), layer, exp_wg, exp_wu, exp_wd)
    yg = _sc_collect(ys, slot_t)
    return _moe_out(yg, w_t.T, v, sg, su, sd, h1, mods, gpost, n_batch=n_batch, seq=seq)


def _rope_tables(seq):
    rows = seq // GRID_W
    row_id = jnp.repeat(jnp.arange(rows, dtype=F32), GRID_W)
    col_id = jnp.tile(jnp.arange(GRID_W, dtype=F32), rows)

    def tables(rot_dim):
        axis_dim = rot_dim // 2
        inv_freq = ROPE_BASE ** (-jnp.arange(0, axis_dim, 2, dtype=F32) / axis_dim)
        ang_r = row_id[:, None] * inv_freq[None, :]
        ang_c = col_id[:, None] * inv_freq[None, :]
        cos = jnp.concatenate([jnp.cos(ang_r), jnp.cos(ang_r), jnp.cos(ang_c), jnp.cos(ang_c)], axis=1)
        sin = jnp.concatenate([-jnp.sin(ang_r), jnp.sin(ang_r), -jnp.sin(ang_c), jnp.sin(ang_c)], axis=1)
        return cos, sin

    cos64, sin64 = tables(GQA_DIM)
    cos32, sin32 = tables(MLA_ROPE)
    ones = jnp.ones((seq, MLA_NOPE), F32)
    pad = LANES - MLA_NOPE - MLA_ROPE
    cospe = jnp.concatenate([ones, cos32, jnp.ones((seq, pad), F32)], axis=1)
    sinpe = jnp.concatenate([0 * ones, sin32, jnp.zeros((seq, pad), F32)], axis=1)
    tab = jnp.concatenate([cos64, cos64, sin64, sin64, cospe, sinpe], axis=1)
    ident = jnp.concatenate([jnp.ones((TM, LANES), F32), jnp.zeros((TM, LANES), F32),
                             jnp.ones((TM, LANES), F32), jnp.zeros((TM, LANES), F32)], axis=1)
    return jnp.concatenate([tab, ident], axis=0)


def _pack_w_in(w):
    d = w.shape[0]
    sizes = (MLA_Q_LORA, MLA_KV_LORA, MLA_ROPE, GQA_HEADS * GQA_DIM, GQA_KV_HEADS * GQA_DIM,
             GQA_KV_HEADS * GQA_DIM, RET_HEADS * RET_QK, RET_HEADS * RET_QK, RET_HEADS * RET_V,
             RET_HEADS * RET_V, 3 * d)
    offs, parts = 0, []
    for s in sizes:
        parts.append(w[:, offs:offs + s])
        offs += s
    cq, ckv, kpe, gq, gk, gv, rq, rk, rv, rg, gates = parts

    def twice(m):
        heads = [m[:, i * GQA_DIM:(i + 1) * GQA_DIM] for i in range(GQA_KV_HEADS)]
        return jnp.concatenate([hh for hd in heads for hh in (hd, hd)], axis=1)

    kpe_slab = jnp.concatenate([jnp.zeros((d, MLA_NOPE), F32), kpe,
                                jnp.zeros((d, LANES - MLA_NOPE - MLA_ROPE), F32)], axis=1)
    packed = jnp.concatenate([cq, ckv, kpe_slab, gq * (GQA_DIM ** -0.5 * LOG2_E), twice(gk), twice(gv),
                              rq, rk * RET_QK ** -0.5, rv, rg, gates], axis=1)
    assert packed.shape[1] == W_COLS
    return packed.astype(BF16)


def _pack_mla_up(w_uq, w_ukv):
    r = w_uq.shape[0]
    dq = MLA_NOPE + MLA_ROPE
    wq = jnp.pad(w_uq.reshape(r, MLA_HEADS, dq), ((0, 0), (0, 0), (0, LANES - dq))).reshape(r, MLA_HEADS * LANES)
    kv = w_ukv.reshape(r, MLA_HEADS, MLA_NOPE + MLA_V)
    wk = jnp.pad(kv[:, :, :MLA_NOPE], ((0, 0), (0, 0), (0, LANES - MLA_NOPE))).reshape(r, MLA_HEADS * LANES)
    wv = kv[:, :, MLA_NOPE:]
    zeros = jnp.zeros_like(wv)
    even = jnp.concatenate([wv, zeros], axis=2)
    odd = jnp.concatenate([zeros, wv], axis=2)
    wv = jnp.where((jnp.arange(MLA_HEADS) % 2 == 0)[None, :, None], even, odd).reshape(r, MLA_HEADS * LANES)
    return wq.astype(BF16), wk.astype(BF16), wv.astype(BF16)


def kernel(x, c, ctx, c_ctx, ada_w, ada_b, norm_mix_pre, norm_mix_post, norm_ffn_pre, norm_ffn_post, w_in, mla_q_norm, mla_w_uq, mla_kv_norm, mla_w_ukv, gqa_sink, ret_decay_fwd, ret_decay_bwd, w_br_mla, w_br_gqa, w_br_ret, w_out, router_w, router_bias, exp_w_gate, exp_w_up, exp_w_down, shared_w_gate, shared_w_up, shared_w_down):
    n_batch, seq, d = x.shape
    n_ctx = ctx.shape[1]
    depth = ada_w.shape[0]
    n_lat_rows = n_batch * seq
    assert seq % TM == 0 and n_ctx == TM and n_batch < MOD_ROWS and seq % GRID_W == 0

    cond = jnp.zeros((MOD_ROWS, d), F32).at[:n_batch].set(c).at[n_batch].set(c_ctx)
    mods_all = _adaln(cond, ada_w, ada_b)
    rope = _rope_tables(seq)
    h = jnp.concatenate([x.reshape(n_lat_rows, d), ctx.reshape(n_batch * n_ctx, d)], axis=0)
    row = lambda p: p.reshape(1, -1)
    dims = dict(n_batch=n_batch, seq=seq)

    for l in range(depth):
        last = l == depth - 1
        mods = mods_all[l]
        wq, wk, wv = _pack_mla_up(mla_w_uq[l], mla_w_ukv[l])
        mq, mk, mv, gqa, ret, gates = _inproj(h, mods, row(norm_mix_pre[l]), _pack_w_in(w_in[l]), rope,
                                              row(mla_q_norm[l]), row(mla_kv_norm[l]), wq, wk, wv, **dims)
        a = _mla_attention(mq, mk, mv, ctx=n_ctx, with_ctx_queries=not last, **dims)
        sink_tab = jnp.broadcast_to(gqa_sink[l].astype(F32)[:, None] * LOG2_E, (GQA_HEADS, LANES))
        w = _window_attention(gqa, sink_tab, ctx=n_ctx, with_ctx_queries=not last, **dims)
        lg = jnp.concatenate([jax.nn.log_sigmoid(ret_decay_fwd[l].astype(F32)),
                              jax.nn.log_sigmoid(ret_decay_bwd[l].astype(F32))])
        o_f, o_b = _retention(ret, jnp.broadcast_to(lg[:, None], (2 * RET_HEADS, LANES)), ctx=n_ctx, **dims)
        n_rows = n_lat_rows if last else h.shape[0]
        h1, v, vp = _merge(a, w, o_f, o_b, ret, gates, h, mods, row(norm_mix_post[l]), row(norm_ffn_pre[l]),
                           w_br_mla[l].astype(BF16), w_br_gqa[l].astype(BF16), w_br_ret[l].astype(BF16),
                           w_out[l].astype(BF16), n_rows=n_rows, **dims)
        h = _moe(v, vp, l, router_w[l].T.astype(BF16), router_bias[l].astype(F32).reshape(-1, 1),
                 exp_w_gate, exp_w_up, exp_w_down, shared_w_gate[l].astype(BF16),
                 shared_w_up[l].astype(BF16), shared_w_down[l].astype(BF16), h1, mods,
                 row(norm_ffn_post[l]), **dims)
    return h[:n_lat_rows].reshape(n_batch, seq, d)
```

```python
import functools

import jax
import jax.numpy as jnp
from jax import lax
from jax.experimental import pallas as pl
from jax.experimental.pallas import tpu as pltpu
from jax.experimental.pallas import tpu_sc as plsc

F32 = jnp.float32
BF16 = jnp.bfloat16

GRID_W = 64
ROPE_BASE = 10000.0
NORM_EPS = 1e-6
NEG_INF = -1e30
LOG2_E = 1.4426950408889634
N_MOD = 6
MLA_HEADS, MLA_NOPE, MLA_ROPE, MLA_V = 8, 64, 32, 64
MLA_Q_LORA, MLA_KV_LORA = 256, 256
GQA_HEADS, GQA_KV_HEADS, GQA_DIM, WINDOW = 8, 2, 64, 128
RET_HEADS, RET_QK, RET_V, RET_CHUNK = 4, 64, 128, 128
N_EXPERTS, N_EXPERT_GROUPS, TOPK_GROUPS, TOP_K = 64, 8, 4, 8
EXPERTS_PER_GROUP = N_EXPERTS // N_EXPERT_GROUPS
ROUTED_SCALE = 2.5

LANES = 128
TM = 256
MOD_ROWS = 8
V7X_VMEM_LIMIT = 56 * 1024 * 1024

C_CQ, C_CKV, C_KPE = 0, 256, 512
C_G = 640
C_R = 1664
C_GATE = 3200
W_COLS = 6272


def _cparams(sem):
    return pltpu.CompilerParams(dimension_semantics=sem, vmem_limit_bytes=V7X_VMEM_LIMIT)


def _rms(x, g):
    return x * lax.rsqrt(jnp.mean(x * x, axis=-1, keepdims=True) + NORM_EPS) * g


def _sigmoid(x):
    return 0.5 * jnp.tanh(0.5 * x) + 0.5


def _dot(a, b):
    return jnp.dot(a, b, preferred_element_type=F32)


def _dot_nt(a, b):
    return lax.dot_general(a, b, (((1,), (1,)), ((), ())), preferred_element_type=F32)


def _dot_tn(a, b):
    return lax.dot_general(a, b, (((0,), (0,)), ((), ())), preferred_element_type=F32)


def _rope(x, cos, sin, half):
    n = x.shape[-1]
    reps = n // LANES
    if reps > 1:
        cos = jnp.concatenate([cos] * reps, axis=1)
        sin = jnp.concatenate([sin] * reps, axis=1)
    lane = lax.broadcasted_iota(jnp.int32, x.shape, 1)
    up = pltpu.roll(x, half, 1)
    dn = pltpu.roll(x, n - half, 1)
    partner = jnp.where((lane & (2 * half - 1)) < half, dn, up)
    return x * cos + partner * sin


def _lane_lo(shape):
    return (lax.broadcasted_iota(jnp.int32, shape, 1) & (LANES - 1)) < (LANES // 2)


def _pack_halves(x):
    n = x.shape[1] // 2
    bits = lambda t: lax.bitcast_convert_type(t.astype(BF16).astype(F32), jnp.uint32)
    return (bits(x[:, :n]) >> 16) | bits(x[:, n:])


def _unpack_halves(p):
    lo = lax.bitcast_convert_type(p << 16, F32)
    hi = lax.bitcast_convert_type(p & jnp.uint32(0xFFFF0000), F32)
    return lo, hi


def _ada_kernel(c_ref, w_ref, b_ref, o_ref):
    c = c_ref[...]
    s = c * _sigmoid(c)
    o_ref[...] = _dot(s.astype(BF16), w_ref[...].astype(BF16)) + b_ref[...]


def _adaln(cond, ada_w, ada_b):
    n_layers, d, n = ada_w.shape
    tn = 1024
    return pl.pallas_call(
        _ada_kernel,
        grid=(n_layers, n // tn),
        in_specs=[pl.BlockSpec((MOD_ROWS, d), lambda l, j: (0, 0)),
                  pl.BlockSpec((None, d, tn), lambda l, j: (l, 0, j)),
                  pl.BlockSpec((None, 1, tn), lambda l, j: (l, 0, j))],
        out_specs=pl.BlockSpec((None, MOD_ROWS, tn), lambda l, j: (l, 0, j)),
        out_shape=jax.ShapeDtypeStruct((n_layers, MOD_ROWS, n), F32),
        compiler_params=_cparams(("arbitrary", "arbitrary")),
        name="adaln",
    )(cond, ada_w, ada_b.reshape(n_layers, 1, n))


def _inproj_kernel(h_ref, mod_ref, gpre_ref, w_ref, rope_ref, qn_ref, kvn_ref, wuq_ref, wuk_ref, wuv_ref,
                   mq_ref, mk_ref, mv_ref, gqa_ref, ret_ref, gate_ref, *, tiles_per_batch, n_batch, d):
    i = pl.program_id(0)
    bi = jnp.minimum(i // tiles_per_batch, n_batch)
    sh = mod_ref[pl.ds(bi, 1), 0:d]
    sc = mod_ref[pl.ds(bi, 1), d:2 * d]
    u = (_rms(h_ref[...], gpre_ref[...]) * (1.0 + sc) + sh).astype(BF16)

    cos64 = rope_ref[:, 0:LANES]
    sin64 = rope_ref[:, LANES:2 * LANES]
    cospe = rope_ref[:, 2 * LANES:3 * LANES]
    sinpe = rope_ref[:, 3 * LANES:4 * LANES]

    c = _dot(u, w_ref[:, C_CQ:C_G])
    kpe = _rope(c[:, C_KPE:C_G], cospe, sinpe, MLA_ROPE // 4)
    qn = _rms(c[:, C_CQ:C_CKV], qn_ref[...]).astype(BF16)
    q = _rope(_dot(qn, wuq_ref[...]), cospe, sinpe, MLA_ROPE // 4)
    mq_ref[...] = (q * ((MLA_NOPE + MLA_ROPE) ** -0.5 * LOG2_E)).astype(mq_ref.dtype)
    kvn = _rms(c[:, C_CKV:C_KPE], kvn_ref[...]).astype(BF16)
    k = _dot(kvn, wuk_ref[...]) + jnp.concatenate([kpe] * MLA_HEADS, axis=1)
    mk_ref[...] = k.astype(mk_ref.dtype)
    v = _dot(kvn, wuv_ref[...])
    lane = lax.broadcasted_iota(jnp.int32, v.shape, 1)
    value_lane = ((lane & (LANES - 1)) < MLA_V) == (((lane >> (LANES.bit_length() - 1)) & 1) == 0)
    mv_ref[...] = jnp.where(value_lane, v, 1.0).astype(mv_ref.dtype)

    g = _dot(u, w_ref[:, C_G:C_R])
    n_qk = GQA_HEADS * GQA_DIM + 2 * GQA_KV_HEADS * GQA_DIM
    gqa_ref[:, 0:n_qk] = _rope(g[:, 0:n_qk], cos64, sin64, GQA_DIM // 4).astype(gqa_ref.dtype)
    gqa_ref[:, n_qk:] = g[:, n_qk:].astype(gqa_ref.dtype)

    r = _dot(u, w_ref[:, C_R:C_GATE])
    n_qk = 2 * RET_HEADS * RET_QK
    ret_ref[:, 0:n_qk] = _rope(r[:, 0:n_qk], cos64, sin64, RET_QK // 4)
    ret_ref[:, n_qk:] = r[:, n_qk:]

    gate_ref[...] = _dot(u, w_ref[:, C_GATE:W_COLS]).astype(gate_ref.dtype)


def _inproj(h, mods, gpre, w_all, rope, qn, kvn, wuq, wuk, wuv, *, n_batch, seq):
    t, d = h.shape
    tiles_per_batch = seq // TM
    n_lat_tiles = n_batch * tiles_per_batch
    const = lambda i: (0, 0)
    rows = lambda i: (i, 0)
    rope_idx = lambda i: (jnp.where(i < n_lat_tiles, i % tiles_per_batch, tiles_per_batch), 0)
    hq = MLA_HEADS * LANES
    outs = [jax.ShapeDtypeStruct((t, hq), BF16), jax.ShapeDtypeStruct((t, hq), BF16),
            jax.ShapeDtypeStruct((t, hq), BF16),
            jax.ShapeDtypeStruct((t, C_R - C_G), BF16),
            jax.ShapeDtypeStruct((t, C_GATE - C_R), F32),
            jax.ShapeDtypeStruct((t, W_COLS - C_GATE), BF16)]
    return pl.pallas_call(
        functools.partial(_inproj_kernel, tiles_per_batch=tiles_per_batch, n_batch=n_batch, d=d),
        grid=(t // TM,),
        in_specs=[pl.BlockSpec((TM, d), rows),
                  pl.BlockSpec(mods.shape, const),
                  pl.BlockSpec((1, d), const),
                  pl.BlockSpec(w_all.shape, const),
                  pl.BlockSpec((TM, 4 * LANES), rope_idx),
                  pl.BlockSpec(qn.shape, const), pl.BlockSpec(kvn.shape, const),
                  pl.BlockSpec(wuq.shape, const), pl.BlockSpec(wuk.shape, const), pl.BlockSpec(wuv.shape, const)],
        out_specs=[pl.BlockSpec((TM, o.shape[1]), rows) for o in outs],
        out_shape=outs,
        compiler_params=_cparams(("arbitrary",)),
        name="inproj",
    )(h, mods, gpre, w_all, rope, qn, kvn, wuq, wuk, wuv)


MLA_HEADS_PER_STEP = 4


def _mla_kernel(q_ref, kl_ref, kc_ref, vl_ref, vc_ref, o_ref, s_ref, p_ref, *, n_lat_tiles, has_ctx_tile):
    n_ctx = kc_ref.shape[0]

    def body(with_lat):
        n_keys = n_ctx + (kl_ref.shape[0] if with_lat else 0)

        def scores(h):
            sl = slice(h * LANES, (h + 1) * LANES)
            s_ref[h % 2, :, 0:n_ctx] = _dot_nt(q_ref[:, sl], kc_ref[:, sl])
            if with_lat:
                s_ref[h % 2, :, n_ctx:n_keys] = _dot_nt(q_ref[:, sl], kl_ref[:, sl])

        def probs(h):
            s = s_ref[h % 2, :, 0:n_keys]
            p_ref[h % 2, :, 0:n_keys] = jnp.exp2(s - jnp.max(s, axis=-1, keepdims=True)).astype(BF16)

        def weighted(h):
            sl = slice(h * LANES, (h + 1) * LANES)
            o = _dot(p_ref[h % 2, :, 0:n_ctx], vc_ref[:, sl])
            if with_lat:
                o = o + _dot(p_ref[h % 2, :, n_ctx:n_keys], vl_ref[:, sl])
            return o / pltpu.roll(o, LANES // 2, 1)

        outs = [None] * MLA_HEADS_PER_STEP
        scores(0)
        for h in range(MLA_HEADS_PER_STEP):
            if h + 1 < MLA_HEADS_PER_STEP:
                scores(h + 1)
            probs(h)
            outs[h] = weighted(h)
        for pr in range(MLA_HEADS_PER_STEP // 2):
            even, odd = outs[2 * pr], outs[2 * pr + 1]
            o_ref[:, pr * LANES:(pr + 1) * LANES] = jnp.where(_lane_lo(even.shape), even, odd).astype(o_ref.dtype)

    if has_ctx_tile:
        i = pl.program_id(2)
        pl.when(i < n_lat_tiles)(lambda: body(True))
        pl.when(i == n_lat_tiles)(lambda: body(False))
    else:
        body(True)


def _mla_attention(mq, mk, mv, *, n_batch, seq, ctx, with_ctx_queries):
    t = mq.shape[0]
    nq = seq // TM
    n_lat_blocks = n_batch * nq
    ctx_blk0 = n_batch * seq // ctx
    hps = MLA_HEADS_PER_STEP
    assert ctx == TM
    q_idx = lambda b, g, i: (jnp.where(i < nq, b * nq + i, n_lat_blocks + b), g)
    lat_idx = lambda b, g, i: (b, g)
    ctx_idx = lambda b, g, i: (ctx_blk0 + b, g)
    n_rows = t if with_ctx_queries else n_batch * seq
    return pl.pallas_call(
        functools.partial(_mla_kernel, n_lat_tiles=nq, has_ctx_tile=with_ctx_queries),
        grid=(n_batch, MLA_HEADS // hps, nq + (1 if with_ctx_queries else 0)),
        in_specs=[pl.BlockSpec((TM, hps * LANES), q_idx),
                  pl.BlockSpec((seq, hps * LANES), lat_idx),
                  pl.BlockSpec((ctx, hps * LANES), ctx_idx),
                  pl.BlockSpec((seq, hps * LANES), lat_idx),
                  pl.BlockSpec((ctx, hps * LANES), ctx_idx)],
        out_specs=pl.BlockSpec((TM, hps * MLA_V), q_idx),
        out_shape=jax.ShapeDtypeStruct((n_rows, MLA_HEADS * MLA_V), BF16),
        scratch_shapes=[pltpu.VMEM((2, TM, ctx + seq), F32), pltpu.VMEM((2, TM, ctx + seq), BF16)],
        compiler_params=_cparams(("arbitrary", "arbitrary", "arbitrary")),
        name="mla_attn",
    )(mq, mk, mk, mv, mv)


def _win_kernel(q_ref, kp_ref, kcur_ref, kn_ref, vp_ref, vcur_ref, vn_ref, kc_ref, vc_ref, sink_ref, o_ref,
                s_ref, p_ref, *, n_lat_tiles, seq, has_ctx_tile):
    i = pl.program_id(1)
    tq = q_ref.shape[0]
    group = GQA_HEADS // GQA_KV_HEADS

    def body(with_lat):
        n_ctx = kc_ref.shape[0]
        n_keys = n_ctx + (tq + 2 * WINDOW if with_lat else 0)
        if with_lat:
            q_pos = i * tq + lax.broadcasted_iota(jnp.int32, (tq, n_keys), 0)
            k_pos = i * tq - WINDOW - n_ctx + lax.broadcasted_iota(jnp.int32, (tq, n_keys), 1)
            in_band = (jnp.abs(q_pos - k_pos) <= WINDOW) & (k_pos >= 0) & (k_pos < seq)
            valid = in_band | (lax.broadcasted_iota(jnp.int32, (tq, n_keys), 1) < n_ctx)
        lo = _lane_lo((tq, LANES))
        lo_k = _lane_lo((n_keys, LANES))
        keys, values = [], []
        for kv in range(GQA_KV_HEADS):
            sl = slice(kv * LANES, (kv + 1) * LANES)
            if with_lat:
                k_all = jnp.concatenate([kc_ref[:, sl], kp_ref[:, sl], kcur_ref[:, sl], kn_ref[:, sl]], axis=0)
                v_all = jnp.concatenate([vc_ref[:, sl], vp_ref[:, sl], vcur_ref[:, sl], vn_ref[:, sl]], axis=0)
            else:
                k_all, v_all = kc_ref[:, sl], vc_ref[:, sl]
            keys.append(k_all)
            one = jnp.ones_like(v_all)
            values.append((jnp.where(lo_k, v_all, one), jnp.where(lo_k, one, v_all)))

        def scores(hd):
            kv, pair = hd // group, hd // 2
            qp = q_ref[:, pair * LANES:(pair + 1) * LANES]
            qm = jnp.where(lo if hd % 2 == 0 else jnp.logical_not(lo), qp, jnp.zeros_like(qp))
            s = _dot_nt(qm, keys[kv])
            s_ref[hd % 2, :, 0:n_keys] = jnp.where(valid, s, NEG_INF) if with_lat else s

        def probs(hd):
            s = s_ref[hd % 2, :, 0:n_keys]
            m = jnp.maximum(jnp.max(s, axis=-1, keepdims=True), sink_ref[hd:hd + 1, 0:1])
            p_ref[hd % 2, :, 0:n_keys] = jnp.exp2(s - m).astype(BF16)
            return jnp.exp2(sink_ref[hd:hd + 1, 0:1] - m)

        def weighted(hd, sink_term):
            o = _dot(p_ref[hd % 2, :, 0:n_keys], values[hd // group][hd % 2])
            return o / (pltpu.roll(o, LANES // 2, 1) + sink_term)

        outs = [None] * GQA_HEADS
        scores(0)
        for hd in range(GQA_HEADS):
            if hd + 1 < GQA_HEADS:
                scores(hd + 1)
            outs[hd] = weighted(hd, probs(hd))
        for pair in range(GQA_HEADS // 2):
            o_ref[:, pair * LANES:(pair + 1) * LANES] = jnp.where(
                lo, outs[2 * pair], outs[2 * pair + 1]).astype(o_ref.dtype)

    if has_ctx_tile:
        pl.when(i < n_lat_tiles)(lambda: body(True))
        pl.when(i == n_lat_tiles)(lambda: body(False))
    else:
        body(True)


def _window_attention(gqa, sink_tab, *, n_batch, seq, ctx, with_ctx_queries):
    t = gqa.shape[0]
    nq = seq // TM
    n_lat_blocks = n_batch * nq
    per_tile = TM // WINDOW
    n_win_blocks = seq // WINDOW
    assert ctx == TM
    ctx_blk0 = n_batch * seq // ctx
    nqk = GQA_HEADS * GQA_DIM
    kw = 2 * GQA_KV_HEADS * GQA_DIM
    k_col, v_col = nqk // kw, nqk // kw + 1
    q_idx = lambda b, i: (jnp.where(i < nq, b * nq + i, n_lat_blocks + b), 0)
    cur = lambda col: (lambda b, i: (b * nq + jnp.minimum(i, nq - 1), col))
    prev = lambda col: (lambda b, i: (b * n_win_blocks + jnp.clip(per_tile * i - 1, 0, n_win_blocks - 1), col))
    nxt = lambda col: (lambda b, i: (b * n_win_blocks + jnp.clip(per_tile * (i + 1), 0, n_win_blocks - 1), col))
    cidx = lambda col: (lambda b, i: (ctx_blk0 + b, col))
    n_rows = t if with_ctx_queries else n_batch * seq
    return pl.pallas_call(
        functools.partial(_win_kernel, n_lat_tiles=nq, seq=seq, has_ctx_tile=with_ctx_queries),
        grid=(n_batch, nq + (1 if with_ctx_queries else 0)),
        in_specs=[pl.BlockSpec((TM, nqk), q_idx),
                  pl.BlockSpec((WINDOW, kw), prev(k_col)), pl.BlockSpec((TM, kw), cur(k_col)),
                  pl.BlockSpec((WINDOW, kw), nxt(k_col)),
                  pl.BlockSpec((WINDOW, kw), prev(v_col)), pl.BlockSpec((TM, kw), cur(v_col)),
                  pl.BlockSpec((WINDOW, kw), nxt(v_col)),
                  pl.BlockSpec((ctx, kw), cidx(k_col)), pl.BlockSpec((ctx, kw), cidx(v_col)),
                  pl.BlockSpec(sink_tab.shape, lambda b, i: (0, 0))],
        out_specs=pl.BlockSpec((TM, nqk), q_idx),
        out_shape=jax.ShapeDtypeStruct((n_rows, nqk), BF16),
        scratch_shapes=[pltpu.VMEM((2, TM, ctx + TM + 2 * WINDOW), F32),
                        pltpu.VMEM((2, TM, ctx + TM + 2 * WINDOW), BF16)],
        compiler_params=_cparams(("arbitrary", "arbitrary")),
        name="win_attn",
    )(gqa, gqa, gqa, gqa, gqa, gqa, gqa, gqa, gqa, sink_tab)


def _ret_kernel(f_ref, b_ref, lg_ref, of_ref, ob_ref, sf_ref, sb_ref, qdec_ref, kdec_ref, cdec_ref, inner_ref):
    @pl.when(pl.program_id(1) == 0)
    def _():
        sf_ref[...] = jnp.zeros_like(sf_ref)
        sb_ref[...] = jnp.zeros_like(sb_ref)

    L = f_ref.shape[0]
    lo = _lane_lo((L, LANES))
    srow_lo = lax.broadcasted_iota(jnp.int32, (LANES, LANES), 0) < RET_QK
    nq = RET_HEADS * RET_QK
    n_pairs = RET_HEADS // 2

    @pl.when(pl.program_id(1) == 0)
    def _():
        ii = lax.broadcasted_iota(jnp.int32, (L, L), 0)
        jj = lax.broadcasted_iota(jnp.int32, (L, L), 1)
        row = lax.broadcasted_iota(jnp.int32, (L, LANES), 0).astype(F32)
        for direction, forward in enumerate((True, False)):
            dist = ii - jj if forward else jj - ii
            distf = jnp.maximum(dist, 0).astype(F32)
            for pr in range(n_pairs):
                r0 = direction * RET_HEADS + 2 * pr
                lg = [lg_ref[r0 + e:r0 + e + 1, :] for e in range(2)]
                lg_lane = jnp.where(lo, lg[0], lg[1])
                qdec_ref[direction * n_pairs + pr] = jnp.exp(lg_lane * ((row + 1.0) if forward else (L - row)))
                kdec_ref[direction * n_pairs + pr] = jnp.exp(lg_lane * ((L - 1.0 - row) if forward else row))
                cdec_ref[direction * n_pairs + pr] = jnp.where(srow_lo, jnp.exp(lg[0] * float(L)),
                                                               jnp.exp(lg[1] * float(L)))
                for e in range(2):
                    inner_ref[r0 + e] = jnp.where(dist >= 0, jnp.exp(lg[e][:, 0:1] * distf), 0.0)

    def scan_chunk(x_ref, o_ref, s_ref, direction):
        for pr in range(n_pairs):
            q = x_ref[:, pr * LANES:(pr + 1) * LANES]
            k = x_ref[:, nq + pr * LANES:nq + (pr + 1) * LANES]
            qd = q * qdec_ref[direction * n_pairs + pr]
            kdb = (k * kdec_ref[direction * n_pairs + pr]).astype(BF16)
            kb = k.astype(BF16)
            state = s_ref[pr]
            state_b = state.astype(BF16)
            upd = []
            for e in range(2):
                hd = 2 * pr + e
                keep = lo if e == 0 else jnp.logical_not(lo)
                v = x_ref[:, 2 * nq + hd * RET_V:2 * nq + (hd + 1) * RET_V].astype(BF16)
                attn = _dot_nt(jnp.where(keep, q, 0.0).astype(BF16), kb) * inner_ref[direction * RET_HEADS + hd]
                o = _dot(attn.astype(BF16), v) + _dot(jnp.where(keep, qd, 0.0).astype(BF16), state_b)
                o_ref[:, hd * RET_V:(hd + 1) * RET_V] = o
                upd.append(_dot_tn(kdb, v))
            s_ref[pr] = state * cdec_ref[direction * n_pairs + pr] + jnp.where(srow_lo, upd[0], upd[1])

    scan_chunk(f_ref, of_ref, sf_ref, 0)
    scan_chunk(b_ref, ob_ref, sb_ref, 1)


def _retention(ret, lg_tab, *, n_batch, seq, ctx):
    t = ret.shape[0]
    L = RET_CHUNK
    n_lat, n_ctx = seq // L, ctx // L
    ctx0 = n_batch * n_lat
    width = 2 * RET_HEADS * RET_QK + RET_HEADS * RET_V
    fwd = lambda b, s: (jnp.where(s < n_ctx, ctx0 + b * n_ctx + s, b * n_lat + s - n_ctx), 0)
    bwd = lambda b, s: (jnp.where(s < n_ctx, ctx0 + b * n_ctx + n_ctx - 1 - s, b * n_lat + n_lat - 1 - (s - n_ctx)), 0)
    out = jax.ShapeDtypeStruct((t, RET_HEADS * RET_V), F32)
    return pl.pallas_call(
        _ret_kernel,
        grid=(n_batch, n_lat + n_ctx),
        in_specs=[pl.BlockSpec((L, width), fwd), pl.BlockSpec((L, width), bwd),
                  pl.BlockSpec(lg_tab.shape, lambda b, s: (0, 0))],
        out_specs=[pl.BlockSpec((L, RET_HEADS * RET_V), fwd), pl.BlockSpec((L, RET_HEADS * RET_V), bwd)],
        out_shape=[out, out],
        scratch_shapes=[pltpu.VMEM((RET_HEADS // 2, LANES, RET_V), F32),
                        pltpu.VMEM((RET_HEADS // 2, LANES, RET_V), F32),
                        pltpu.VMEM((RET_HEADS, L, LANES), F32), pltpu.VMEM((RET_HEADS, L, LANES), F32),
                        pltpu.VMEM((RET_HEADS, LANES, RET_V), F32), pltpu.VMEM((2 * RET_HEADS, L, L), F32)],
        compiler_params=_cparams(("arbitrary", "arbitrary")),
        name="retention",
    )(ret, ret, lg_tab)


def _merge_kernel(a_ref, w_ref, of_ref, ob_ref, rg_ref, gt_ref, h_ref, mod_ref, gpost_ref, gffn_ref,
                  wa_ref, ww_ref, wr_ref, wo_ref, h1_ref, v_ref, vp_ref, *, tiles_per_batch, n_batch, d):
    i = pl.program_id(0)
    bi = jnp.minimum(i // tiles_per_batch, n_batch)
    o = of_ref[...] + ob_ref[...]
    normed = []
    for hd in range(RET_HEADS):
        oh = o[:, hd * RET_V:(hd + 1) * RET_V]
        dev = oh - jnp.mean(oh, axis=-1, keepdims=True)
        normed.append(dev * lax.rsqrt(jnp.mean(dev * dev, axis=-1, keepdims=True) + NORM_EPS))
    g = rg_ref[...]
    r = (g * _sigmoid(g)) * jnp.concatenate(normed, axis=1)
    y = (_sigmoid(gt_ref[:, 0:d].astype(F32)) * _dot(a_ref[...], wa_ref[...])
         + _sigmoid(gt_ref[:, d:2 * d].astype(F32)) * _dot(w_ref[...], ww_ref[...])
         + _sigmoid(gt_ref[:, 2 * d:3 * d].astype(F32)) * _dot(r.astype(BF16), wr_ref[...]))
    z = _dot(y.astype(BF16), wo_ref[...])
    g1 = mod_ref[pl.ds(bi, 1), 2 * d:3 * d]
    sh2 = mod_ref[pl.ds(bi, 1), 3 * d:4 * d]
    sc2 = mod_ref[pl.ds(bi, 1), 4 * d:5 * d]
    h1 = h_ref[...] + g1 * _rms(z, gpost_ref[...])
    h1_ref[...] = h1
    v = _rms(h1, gffn_ref[...]) * (1.0 + sc2) + sh2
    v_ref[...] = v.astype(v_ref.dtype)
    vp_ref[...] = _pack_halves(v)


def _merge(a, w, o_f, o_b, ret, gates, h, mods, gpost, gffn, wa, ww, wr, wo, *, n_rows, n_batch, seq):
    d = h.shape[1]
    rows = lambda i: (i, 0)
    const = lambda i: (0, 0)
    rv = RET_HEADS * RET_V
    rg_col = (2 * RET_HEADS * RET_QK + rv) // rv
    outs = [jax.ShapeDtypeStruct((n_rows, d), F32), jax.ShapeDtypeStruct((n_rows, d), BF16),
            jax.ShapeDtypeStruct((n_rows, d // 2), jnp.uint32)]
    return pl.pallas_call(
        functools.partial(_merge_kernel, tiles_per_batch=seq // TM, n_batch=n_batch, d=d),
        grid=(n_rows // TM,),
        in_specs=[pl.BlockSpec((TM, a.shape[1]), rows), pl.BlockSpec((TM, w.shape[1]), rows),
                  pl.BlockSpec((TM, rv), rows), pl.BlockSpec((TM, rv), rows),
                  pl.BlockSpec((TM, rv), lambda i: (i, rg_col)),
                  pl.BlockSpec((TM, 3 * d), rows), pl.BlockSpec((TM, d), rows),
                  pl.BlockSpec(mods.shape, const), pl.BlockSpec((1, d), const), pl.BlockSpec((1, d), const),
                  pl.BlockSpec(wa.shape, const), pl.BlockSpec(ww.shape, const),
                  pl.BlockSpec(wr.shape, const), pl.BlockSpec(wo.shape, const)],
        out_specs=[pl.BlockSpec((TM, o.shape[1]), rows) for o in outs],
        out_shape=outs,
        compiler_params=_cparams(("arbitrary",)),
        name="merge",
    )(a, w, o_f, o_b, ret, gates, h, mods, gpost, gffn, wa, ww, wr, wo)


def _router_kernel(v_ref, rw_ref, rb_ref, eidx_ref, rank_ref, w_ref, cnt_ref, carry_ref):
    @pl.when(pl.program_id(0) == 0)
    def _():
        carry_ref[...] = jnp.zeros_like(carry_ref)

    tm = v_ref.shape[0]
    scores = _sigmoid(_dot_nt(rw_ref[...], v_ref[...]))
    sel = scores + rb_ref[...]
    neg = -jnp.inf
    n_grp, per = N_EXPERT_GROUPS, EXPERTS_PER_GROUP

    sel3 = sel.reshape(n_grp, per, tm)
    member_id = lax.broadcasted_iota(jnp.int32, sel3.shape, 1)
    m1 = jnp.max(sel3, axis=1, keepdims=True)
    i1 = jnp.min(jnp.where(sel3 == m1, member_id, per), axis=1, keepdims=True)
    m2 = jnp.max(jnp.where(member_id == i1, neg, sel3), axis=1, keepdims=True)
    gscore = (m1 + m2).reshape(n_grp, tm)
    gid = lax.broadcasted_iota(jnp.int32, gscore.shape, 0)
    ahead = jnp.zeros(gscore.shape, jnp.int32)
    for gj in range(n_grp):
        other = gscore[gj:gj + 1, :]
        ahead = ahead + jnp.where((other > gscore) | ((other == gscore) & (gid > gj)), 1, 0)
    group_ok = (ahead < TOPK_GROUPS).reshape(n_grp, 1, tm)
    sel = jnp.where(group_ok, sel3, NEG_INF).reshape(N_EXPERTS, tm)

    eid = lax.broadcasted_iota(jnp.int32, sel.shape, 0)
    chosen = jnp.zeros(sel.shape, jnp.bool_)
    picks = []
    for _ in range(TOP_K):
        m = jnp.max(sel, axis=0, keepdims=True)
        idx = jnp.min(jnp.where(sel == m, eid, N_EXPERTS), axis=0, keepdims=True)
        hit = eid == idx
        chosen = chosen | hit
        sel = jnp.where(hit, neg, sel)
        picks.append(idx)
    w = jnp.where(chosen, scores, 0.0)
    gate = ROUTED_SCALE * w / jnp.sum(w, axis=0, keepdims=True)

    member = jnp.where(chosen, 1.0, 0.0)
    earlier = lax.broadcasted_iota(jnp.int32, (tm, tm), 0) < lax.broadcasted_iota(jnp.int32, (tm, tm), 1)
    pos = _dot(member.astype(BF16), jnp.where(earlier, 1.0, 0.0).astype(BF16)) + carry_ref[...]
    for k, idx in enumerate(picks):
        hit = eid == idx
        eidx_ref[k:k + 1, :] = idx
        rank_ref[k:k + 1, :] = jnp.sum(jnp.where(hit, pos, 0.0), axis=0, keepdims=True)
        w_ref[k:k + 1, :] = jnp.sum(jnp.where(hit, gate, 0.0), axis=0, keepdims=True)
    carry_ref[...] += jnp.sum(member, axis=1, keepdims=True)
    cnt_ref[...] = carry_ref[...]


def _router(v, rw_t, rb, tile0, n_tiles):
    d = v.shape[1]
    n_rows = n_tiles * TM
    cols = lambda i: (0, i)
    const = lambda i: (0, 0)
    outs = [jax.ShapeDtypeStruct((TOP_K, n_rows), jnp.int32), jax.ShapeDtypeStruct((TOP_K, n_rows), F32),
            jax.ShapeDtypeStruct((TOP_K, n_rows), F32), jax.ShapeDtypeStruct((N_EXPERTS, 1), F32)]
    return pl.pallas_call(
        _router_kernel,
        grid=(n_tiles,),
        in_specs=[pl.BlockSpec((TM, d), lambda i: (tile0 + i, 0)), pl.BlockSpec(rw_t.shape, const),
                  pl.BlockSpec(rb.shape, const)],
        out_specs=[pl.BlockSpec((TOP_K, TM), cols), pl.BlockSpec((TOP_K, TM), cols),
                   pl.BlockSpec((TOP_K, TM), cols), pl.BlockSpec((N_EXPERTS, 1), const)],
        out_shape=outs,
        scratch_shapes=[pltpu.VMEM((N_EXPERTS, 1), F32)],
        compiler_params=_cparams(("arbitrary",)),
        name="router",
    )(v, rw_t, rb)


def _slots_kernel(eidx_ref, rank_ref, cnt_ref, slot_ref):
    tm = eidx_ref.shape[1]
    eid = lax.broadcasted_iota(jnp.int32, (N_EXPERTS, tm), 0)
    for k in range(TOP_K):
        before = jnp.sum(jnp.where(eid < eidx_ref[k:k + 1, :], cnt_ref[...], 0.0), axis=0, keepdims=True)
        slot_ref[k:k + 1, :] = (before + rank_ref[k:k + 1, :]).astype(jnp.int32)


def _slots(eidx, rank, cnt):
    n_rows = eidx.shape[1]
    tm = next(c for c in (2048, 1024, 512, 256) if n_rows % c == 0)
    cols = lambda i: (0, i)
    return pl.pallas_call(
        _slots_kernel,
        grid=(n_rows // tm,),
        in_specs=[pl.BlockSpec((TOP_K, tm), cols), pl.BlockSpec((TOP_K, tm), cols),
                  pl.BlockSpec(cnt.shape, lambda i: (0, 0))],
        out_specs=pl.BlockSpec((TOP_K, tm), cols),
        out_shape=jax.ShapeDtypeStruct((TOP_K, n_rows), jnp.int32),
        compiler_params=_cparams(("arbitrary",)),
        name="slots",
    )(eidx, rank, cnt)


SC_WINDOW = 128


def _sc_mesh():
    return plsc.VectorSubcoreMesh(core_axis_name="core", subcore_axis_name="subcore")


def _sc_dispatch(rows, slot_t, n_out, row0):
    width = rows.shape[1]
    n_chunks = slot_t.shape[1] // SC_WINDOW
    info = plsc.get_sparse_core_info()
    n_workers = info.num_cores * info.num_subcores

    @functools.partial(
        pl.kernel, mesh=_sc_mesh(),
        out_type=jax.ShapeDtypeStruct((n_out, width), rows.dtype),
        scratch_types=[pltpu.VMEM((TOP_K, SC_WINDOW), jnp.int32), pltpu.VMEM((SC_WINDOW, width), rows.dtype)],
        name="moe_dispatch")
    def run(rows_hbm, idx_hbm, out_hbm, idx_v, rows_v):
        wid = lax.axis_index("subcore") * info.num_cores + lax.axis_index("core")

        @pl.loop(wid, n_chunks, step=n_workers)
        def _(c):
            r0 = pl.multiple_of(c * SC_WINDOW, SC_WINDOW)
            pltpu.sync_copy(idx_hbm.at[:, pl.ds(r0, SC_WINDOW)], idx_v)
            pltpu.sync_copy(rows_hbm.at[pl.ds(row0 + r0, SC_WINDOW)], rows_v)
            for k in range(TOP_K):
                pltpu.sync_copy(rows_v, out_hbm.at[idx_v.at[k]])

    return run(rows, slot_t)


def _sc_collect(rows, slot_t):
    n_picks, n_rows = slot_t.shape
    width = rows.shape[1]
    n_chunks = n_rows // SC_WINDOW
    info = plsc.get_sparse_core_info()
    n_workers = info.num_cores * info.num_subcores

    @functools.partial(
        pl.kernel, mesh=_sc_mesh(),
        out_type=jax.ShapeDtypeStruct((n_picks, n_rows, width), rows.dtype),
        scratch_types=[pltpu.VMEM((TOP_K, SC_WINDOW), jnp.int32), pltpu.VMEM((SC_WINDOW, width), rows.dtype)],
        name="moe_collect")
    def run(rows_hbm, idx_hbm, out_hbm, idx_v, rows_v):
        wid = lax.axis_index("subcore") * info.num_cores + lax.axis_index("core")

        @pl.loop(wid, n_chunks, step=n_workers)
        def _(c):
            r0 = pl.multiple_of(c * SC_WINDOW, SC_WINDOW)
            pltpu.sync_copy(idx_hbm.at[:, pl.ds(r0, SC_WINDOW)], idx_v)
            for k in range(TOP_K):
                pltpu.sync_copy(rows_hbm.at[idx_v.at[k]], rows_v)
                pltpu.sync_copy(rows_v, out_hbm.at[k, pl.ds(r0, SC_WINDOW)])

    return run(rows, slot_t)


EXPERT_TILE = 512


def _work_items(cnt, n_slots):
    counts = cnt[:, 0].astype(jnp.int32)
    ends = jnp.cumsum(counts)
    n_tiles = n_slots // EXPERT_TILE
    bounds = jnp.sort(jnp.concatenate([jnp.arange(n_tiles, dtype=jnp.int32) * EXPERT_TILE, ends - counts]))
    nxt = jnp.concatenate([bounds[1:], jnp.array([n_slots], jnp.int32)])
    tile = jnp.minimum(bounds // EXPERT_TILE, n_tiles - 1)
    expert = jnp.sum((ends[None, :] <= bounds[:, None]).astype(jnp.int32), axis=1)
    expert = jnp.minimum(expert, N_EXPERTS - 1)
    return tile, expert, bounds - tile * EXPERT_TILE, nxt - tile * EXPERT_TILE


def _experts_kernel(tile_ref, exp_ref, lo_ref, hi_ref, xs_ref, wg_ref, wu_ref, wd_ref, ys_ref,
                    acc_ref, wgb_ref, wub_ref, wdb_ref):
    i = pl.program_id(0)
    lo, hi = lo_ref[i], hi_ref[i]

    @pl.when((i == 0) | (exp_ref[i] != exp_ref[jnp.maximum(i - 1, 0)]))
    def _():
        wgb_ref[...] = wg_ref[...].astype(BF16)
        wub_ref[...] = wu_ref[...].astype(BF16)
        wdb_ref[...] = wd_ref[...].astype(BF16)

    def ffn():
        x_lo, x_hi = _unpack_halves(xs_ref[...])
        x_lo, x_hi = x_lo.astype(BF16), x_hi.astype(BF16)
        n = x_lo.shape[1]
        a = _dot(x_lo, wgb_ref[0:n, :]) + _dot(x_hi, wgb_ref[n:, :])
        u = _dot(x_lo, wub_ref[0:n, :]) + _dot(x_hi, wub_ref[n:, :])
        return _dot(((a * _sigmoid(a)) * u).astype(BF16), wdb_ref[...])

    whole = (lo == 0) & (hi == EXPERT_TILE)

    @pl.when(whole)
    def _():
        ys_ref[...] = _pack_halves(ffn())

    @pl.when(jnp.logical_not(whole) & (hi > lo))
    def _():
        y = ffn()
        row = lax.broadcasted_iota(jnp.int32, y.shape, 0)
        y = jnp.where((row >= lo) & (row < hi), y, 0.0)

        @pl.when(lo == 0)
        def _():
            acc_ref[...] = y

        @pl.when((lo > 0) & (hi < EXPERT_TILE))
        def _():
            acc_ref[...] += y

        @pl.when((lo > 0) & (hi == EXPERT_TILE))
        def _():
            ys_ref[...] = _pack_halves(acc_ref[...] + y)


def _experts(xs, items, layer, exp_wg, exp_wu, exp_wd):
    n_slots, half = xs.shape
    d, hid = exp_wg.shape[-2:]
    tile, expert, lo, hi = items
    grid_spec = pltpu.PrefetchScalarGridSpec(
        num_scalar_prefetch=4,
        grid=(tile.shape[0],),
        in_specs=[pl.BlockSpec((EXPERT_TILE, half), lambda i, t, e, lo, hi: (t[i], 0)),
                  pl.BlockSpec((None, None, d, hid), lambda i, t, e, lo, hi: (layer, e[i], 0, 0)),
                  pl.BlockSpec((None, None, d, hid), lambda i, t, e, lo, hi: (layer, e[i], 0, 0)),
                  pl.BlockSpec((None, None, hid, d), lambda i, t, e, lo, hi: (layer, e[i], 0, 0))],
        out_specs=pl.BlockSpec((EXPERT_TILE, half), lambda i, t, e, lo, hi: (t[i], 0)),
        scratch_shapes=[pltpu.VMEM((EXPERT_TILE, d), F32), pltpu.VMEM((d, hid), BF16),
                        pltpu.VMEM((d, hid), BF16), pltpu.VMEM((hid, d), BF16)])
    return pl.pallas_call(
        _experts_kernel,
        grid_spec=grid_spec,
        out_shape=jax.ShapeDtypeStruct((n_slots, half), jnp.uint32),
        compiler_params=_cparams(("arbitrary",)),
        name="experts",
    )(tile, expert, lo, hi, xs, exp_wg, exp_wu, exp_wd)


def _moe_out_kernel(yg_ref, w_ref, v_ref, sg_ref, su_ref, sd_ref, h1_ref, mod_ref, gpost_ref, *rest,
                    tile0, tiles_per_batch, n_batch, d):
    o_ref = rest[-1]
    i = tile0 + pl.program_id(0)
    x = v_ref[...]
    a = _dot(x, sg_ref[...])
    f = _dot(((a * _sigmoid(a)) * _dot(x, su_ref[...])).astype(BF16), sd_ref[...])
    n = d // 2
    f_lo, f_hi = f[:, :n], f[:, n:]
    w = w_ref[...]
    for k in range(TOP_K):
        y_lo, y_hi = _unpack_halves(yg_ref[k])
        wk = w[:, k:k + 1]
        f_lo = f_lo + wk * y_lo
        f_hi = f_hi + wk * y_hi
    f = jnp.concatenate([f_lo, f_hi], axis=1)
    bi = jnp.minimum(i // tiles_per_batch, n_batch)
    g2 = mod_ref[pl.ds(bi, 1), 5 * d:6 * d]
    o_ref[...] = h1_ref[...] + g2 * _rms(f, gpost_ref[...])


def _moe_out(yg, w, v, sg, su, sd, h1, mods, gpost, tile0, prev_out, *, n_batch, seq):
    n_rows, d = v.shape
    n_tiles = w.shape[0] // TM
    part = lambda i: (i, 0)
    rows = lambda i: (tile0 + i, 0)
    const = lambda i: (0, 0)
    in_specs = [pl.BlockSpec((TOP_K, TM, d // 2), lambda i: (0, i, 0)), pl.BlockSpec((TM, TOP_K), part),
                pl.BlockSpec((TM, d), rows),
                pl.BlockSpec(sg.shape, const), pl.BlockSpec(su.shape, const), pl.BlockSpec(sd.shape, const),
                pl.BlockSpec((TM, d), rows), pl.BlockSpec(mods.shape, const), pl.BlockSpec((1, d), const)]
    args = [yg, w, v, sg, su, sd, h1, mods, gpost]
    aliases = {}
    if prev_out is not None:
        in_specs.append(pl.BlockSpec(memory_space=pl.ANY))
        aliases = {len(args): 0}
        args.append(prev_out)
    return pl.pallas_call(
        functools.partial(_moe_out_kernel, tile0=tile0, tiles_per_batch=seq // TM, n_batch=n_batch, d=d),
        grid=(n_tiles,),
        in_specs=in_specs,
        out_specs=pl.BlockSpec((TM, d), rows),
        out_shape=jax.ShapeDtypeStruct((n_rows, d), F32),
        input_output_aliases=aliases,
        compiler_params=_cparams(("arbitrary",)),
        name="moe_out",
    )(*args)


MOE_PARTS = 2


def _moe(v, vp, layer, rw, rb, exp_wg, exp_wu, exp_wd, sg, su, sd, h1, mods, gpost, *, n_batch, seq):
    n_tiles = v.shape[0] // TM
    per_part = n_tiles // MOE_PARTS
    assert per_part * MOE_PARTS == n_tiles and (per_part * TM * TOP_K) % EXPERT_TILE == 0
    n_slots = per_part * TM * TOP_K
    routed = []
    for p in range(MOE_PARTS):
        eidx, rank, w_t, cnt = _router(v, rw, rb, p * per_part, per_part)
        routed.append((_slots(eidx, rank, cnt), w_t, cnt))
    xs = [_sc_dispatch(vp, slot_t, n_slots, p * per_part * TM) for p, (slot_t, _, _) in enumerate(routed)]
    ys = [_experts(x, _work_items(cnt, n_slots), layer, exp_wg, exp_wu, exp_wd)
          for x, (_, _, cnt) in zip(xs, routed)]
    out = None
    for p, (y, (slot_t, w_t, _)) in enumerate(zip(ys, routed)):
        yg = _sc_collect(y, slot_t)
        out = _moe_out(yg, w_t.T, v, sg, su, sd, h1, mods, gpost, p * per_part, out, n_batch=n_batch, seq=seq)
    return out


def _rope_tables(seq):
    rows = seq // GRID_W
    row_id = jnp.repeat(jnp.arange(rows, dtype=F32), GRID_W)
    col_id = jnp.tile(jnp.arange(GRID_W, dtype=F32), rows)

    def tables(rot_dim):
        axis_dim = rot_dim // 2
        inv_freq = ROPE_BASE ** (-jnp.arange(0, axis_dim, 2, dtype=F32) / axis_dim)
        ang_r = row_id[:, None] * inv_freq[None, :]
        ang_c = col_id[:, None] * inv_freq[None, :]
        cos = jnp.concatenate([jnp.cos(ang_r), jnp.cos(ang_r), jnp.cos(ang_c), jnp.cos(ang_c)], axis=1)
        sin = jnp.concatenate([-jnp.sin(ang_r), jnp.sin(ang_r), -jnp.sin(ang_c), jnp.sin(ang_c)], axis=1)
        return cos, sin

    cos64, sin64 = tables(GQA_DIM)
    cos32, sin32 = tables(MLA_ROPE)
    ones = jnp.ones((seq, MLA_NOPE), F32)
    pad = LANES - MLA_NOPE - MLA_ROPE
    cospe = jnp.concatenate([ones, cos32, jnp.ones((seq, pad), F32)], axis=1)
    sinpe = jnp.concatenate([0 * ones, sin32, jnp.zeros((seq, pad), F32)], axis=1)
    tab = jnp.concatenate([cos64, cos64, sin64, sin64, cospe, sinpe], axis=1)
    ident = jnp.concatenate([jnp.ones((TM, LANES), F32), jnp.zeros((TM, LANES), F32),
                             jnp.ones((TM, LANES), F32), jnp.zeros((TM, LANES), F32)], axis=1)
    return jnp.concatenate([tab, ident], axis=0)


def _pack_w_in(w):
    d = w.shape[0]
    sizes = (MLA_Q_LORA, MLA_KV_LORA, MLA_ROPE, GQA_HEADS * GQA_DIM, GQA_KV_HEADS * GQA_DIM,
             GQA_KV_HEADS * GQA_DIM, RET_HEADS * RET_QK, RET_HEADS * RET_QK, RET_HEADS * RET_V,
             RET_HEADS * RET_V, 3 * d)
    offs, parts = 0, []
    for s in sizes:
        parts.append(w[:, offs:offs + s])
        offs += s
    cq, ckv, kpe, gq, gk, gv, rq, rk, rv, rg, gates = parts

    def twice(m):
        heads = [m[:, i * GQA_DIM:(i + 1) * GQA_DIM] for i in range(GQA_KV_HEADS)]
        return jnp.concatenate([hh for hd in heads for hh in (hd, hd)], axis=1)

    kpe_slab = jnp.concatenate([jnp.zeros((d, MLA_NOPE), F32), kpe,
                                jnp.zeros((d, LANES - MLA_NOPE - MLA_ROPE), F32)], axis=1)
    packed = jnp.concatenate([cq, ckv, kpe_slab, gq * (GQA_DIM ** -0.5 * LOG2_E), twice(gk), twice(gv),
                              rq, rk * RET_QK ** -0.5, rv, rg, gates], axis=1)
    assert packed.shape[1] == W_COLS
    return packed.astype(BF16)


def _pack_mla_up(w_uq, w_ukv):
    r = w_uq.shape[0]
    dq = MLA_NOPE + MLA_ROPE
    wq = jnp.pad(w_uq.reshape(r, MLA_HEADS, dq), ((0, 0), (0, 0), (0, LANES - dq))).reshape(r, MLA_HEADS * LANES)
    kv = w_ukv.reshape(r, MLA_HEADS, MLA_NOPE + MLA_V)
    wk = jnp.pad(kv[:, :, :MLA_NOPE], ((0, 0), (0, 0), (0, LANES - MLA_NOPE))).reshape(r, MLA_HEADS * LANES)
    wv = kv[:, :, MLA_NOPE:]
    zeros = jnp.zeros_like(wv)
    even = jnp.concatenate([wv, zeros], axis=2)
    odd = jnp.concatenate([zeros, wv], axis=2)
    wv = jnp.where((jnp.arange(MLA_HEADS) % 2 == 0)[None, :, None], even, odd).reshape(r, MLA_HEADS * LANES)
    return wq.astype(BF16), wk.astype(BF16), wv.astype(BF16)


def kernel(x, c, ctx, c_ctx, ada_w, ada_b, norm_mix_pre, norm_mix_post, norm_ffn_pre, norm_ffn_post, w_in, mla_q_norm, mla_w_uq, mla_kv_norm, mla_w_ukv, gqa_sink, ret_decay_fwd, ret_decay_bwd, w_br_mla, w_br_gqa, w_br_ret, w_out, router_w, router_bias, exp_w_gate, exp_w_up, exp_w_down, shared_w_gate, shared_w_up, shared_w_down):
    n_batch, seq, d = x.shape
    n_ctx = ctx.shape[1]
    depth = ada_w.shape[0]
    n_lat_rows = n_batch * seq
    assert seq % TM == 0 and n_ctx == TM and n_batch < MOD_ROWS and seq % GRID_W == 0

    cond = jnp.zeros((MOD_ROWS, d), F32).at[:n_batch].set(c).at[n_batch].set(c_ctx)
    mods_all = _adaln(cond, ada_w, ada_b)
    rope = _rope_tables(seq)
    h = jnp.concatenate([x.reshape(n_lat_rows, d), ctx.reshape(n_batch * n_ctx, d)], axis=0)
    row = lambda p: p.reshape(1, -1)
    dims = dict(n_batch=n_batch, seq=seq)

    for l in range(depth):
        last = l == depth - 1
        mods = mods_all[l]
        wq, wk, wv = _pack_mla_up(mla_w_uq[l], mla_w_ukv[l])
        mq, mk, mv, gqa, ret, gates = _inproj(h, mods, row(norm_mix_pre[l]), _pack_w_in(w_in[l]), rope,
                                              row(mla_q_norm[l]), row(mla_kv_norm[l]), wq, wk, wv, **dims)
        a = _mla_attention(mq, mk, mv, ctx=n_ctx, with_ctx_queries=not last, **dims)
        sink_tab = jnp.broadcast_to(gqa_sink[l].astype(F32)[:, None] * LOG2_E, (GQA_HEADS, LANES))
        w = _window_attention(gqa, sink_tab, ctx=n_ctx, with_ctx_queries=not last, **dims)
        lg = jnp.concatenate([jax.nn.log_sigmoid(ret_decay_fwd[l].astype(F32)),
                              jax.nn.log_sigmoid(ret_decay_bwd[l].astype(F32))])
        o_f, o_b = _retention(ret, jnp.broadcast_to(lg[:, None], (2 * RET_HEADS, LANES)), ctx=n_ctx, **dims)
        n_rows = n_lat_rows if last else h.shape[0]
        h1, v, vp = _merge(a, w, o_f, o_b, ret, gates, h, mods, row(norm_mix_post[l]), row(norm_ffn_pre[l]),
                           w_br_mla[l].astype(BF16), w_br_gqa[l].astype(BF16), w_br_ret[l].astype(BF16),
                           w_out[l].astype(BF16), n_rows=n_rows, **dims)
        h = _moe(v, vp, l, router_w[l].T.astype(BF16), router_bias[l].astype(F32).reshape(-1, 1),
                 exp_w_gate, exp_w_up, exp_w_down, shared_w_gate[l].astype(BF16),
                 shared_w_up[l].astype(BF16), shared_w_down[l].astype(BF16), h1, mods,
                 row(norm_ffn_post[l]), **dims)
    return h[:n_lat_rows].reshape(n_batch, seq, d)
```

```python
import functools

import jax
import jax.numpy as jnp
from jax import lax
from jax.experimental import pallas as pl
from jax.experimental.pallas import tpu as pltpu
from jax.experimental.pallas import tpu_sc as plsc

F32 = jnp.float32
BF16 = jnp.bfloat16

GRID_W = 64
ROPE_BASE = 10000.0
NORM_EPS = 1e-6
NEG_INF = -1e30
LOG2_E = 1.4426950408889634
N_MOD = 6
MLA_HEADS, MLA_NOPE, MLA_ROPE, MLA_V = 8, 64, 32, 64
MLA_Q_LORA, MLA_KV_LORA = 256, 256
GQA_HEADS, GQA_KV_HEADS, GQA_DIM, WINDOW = 8, 2, 64, 128
RET_HEADS, RET_QK, RET_V, RET_CHUNK = 4, 64, 128, 128
N_EXPERTS, N_EXPERT_GROUPS, TOPK_GROUPS, TOP_K = 64, 8, 4, 8
EXPERTS_PER_GROUP = N_EXPERTS // N_EXPERT_GROUPS
ROUTED_SCALE = 2.5

LANES = 128
TM = 256
MOD_ROWS = 8
V7X_VMEM_LIMIT = 56 * 1024 * 1024

C_CQ, C_CKV, C_KPE = 0, 256, 512
C_G = 640
C_R = 1664
C_GATE = 3200
W_COLS = 6272


def _cparams(sem):
    return pltpu.CompilerParams(dimension_semantics=sem, vmem_limit_bytes=V7X_VMEM_LIMIT)


def _rms(x, g):
    return x * lax.rsqrt(jnp.mean(x * x, axis=-1, keepdims=True) + NORM_EPS) * g


def _sigmoid(x):
    return 0.5 * jnp.tanh(0.5 * x) + 0.5


def _dot(a, b):
    return jnp.dot(a, b, preferred_element_type=F32)


def _dot_nt(a, b):
    return lax.dot_general(a, b, (((1,), (1,)), ((), ())), preferred_element_type=F32)


def _dot_tn(a, b):
    return lax.dot_general(a, b, (((0,), (0,)), ((), ())), preferred_element_type=F32)


def _rope(x, cos, sin, half):
    n = x.shape[-1]
    reps = n // LANES
    if reps > 1:
        cos = jnp.concatenate([cos] * reps, axis=1)
        sin = jnp.concatenate([sin] * reps, axis=1)
    lane = lax.broadcasted_iota(jnp.int32, x.shape, 1)
    up = pltpu.roll(x, half, 1)
    dn = pltpu.roll(x, n - half, 1)
    partner = jnp.where((lane & (2 * half - 1)) < half, dn, up)
    return x * cos + partner * sin


def _lane_lo(shape):
    return (lax.broadcasted_iota(jnp.int32, shape, 1) & (LANES - 1)) < (LANES // 2)


def _pack_halves(x):
    n = x.shape[1] // 2
    bits = lambda t: lax.bitcast_convert_type(t.astype(BF16).astype(F32), jnp.uint32)
    return (bits(x[:, :n]) >> 16) | bits(x[:, n:])


def _unpack_halves(p):
    lo = lax.bitcast_convert_type(p << 16, F32)
    hi = lax.bitcast_convert_type(p & jnp.uint32(0xFFFF0000), F32)
    return lo, hi


def _ada_kernel(c_ref, w_ref, b_ref, o_ref):
    c = c_ref[...]
    s = c * _sigmoid(c)
    o_ref[...] = _dot(s.astype(BF16), w_ref[...].astype(BF16)) + b_ref[...]


def _adaln(cond, ada_w, ada_b):
    n_layers, d, n = ada_w.shape
    tn = 1024
    return pl.pallas_call(
        _ada_kernel,
        grid=(n_layers, n // tn),
        in_specs=[pl.BlockSpec((MOD_ROWS, d), lambda l, j: (0, 0)),
                  pl.BlockSpec((None, d, tn), lambda l, j: (l, 0, j)),
                  pl.BlockSpec((None, 1, tn), lambda l, j: (l, 0, j))],
        out_specs=pl.BlockSpec((None, MOD_ROWS, tn), lambda l, j: (l, 0, j)),
        out_shape=jax.ShapeDtypeStruct((n_layers, MOD_ROWS, n), F32),
        compiler_params=_cparams(("arbitrary", "arbitrary")),
        name="adaln",
    )(cond, ada_w, ada_b.reshape(n_layers, 1, n))


def _inproj_kernel(h_ref, mod_ref, gpre_ref, w_ref, rope_ref, qn_ref, kvn_ref, wuq_ref, wuk_ref, wuv_ref,
                   mq_ref, mk_ref, mv_ref, gqa_ref, ret_ref, gate_ref, *, tiles_per_batch, n_batch, d):
    i = pl.program_id(0)
    bi = jnp.minimum(i // tiles_per_batch, n_batch)
    sh = mod_ref[pl.ds(bi, 1), 0:d]
    sc = mod_ref[pl.ds(bi, 1), d:2 * d]
    u = (_rms(h_ref[...], gpre_ref[...]) * (1.0 + sc) + sh).astype(BF16)

    cos64 = rope_ref[:, 0:LANES]
    sin64 = rope_ref[:, LANES:2 * LANES]
    cospe = rope_ref[:, 2 * LANES:3 * LANES]
    sinpe = rope_ref[:, 3 * LANES:4 * LANES]

    c = _dot(u, w_ref[:, C_CQ:C_G])
    kpe = _rope(c[:, C_KPE:C_G], cospe, sinpe, MLA_ROPE // 4)
    qn = _rms(c[:, C_CQ:C_CKV], qn_ref[...]).astype(BF16)
    q = _rope(_dot(qn, wuq_ref[...]), cospe, sinpe, MLA_ROPE // 4)
    mq_ref[...] = (q * ((MLA_NOPE + MLA_ROPE) ** -0.5 * LOG2_E)).astype(mq_ref.dtype)
    kvn = _rms(c[:, C_CKV:C_KPE], kvn_ref[...]).astype(BF16)
    k = _dot(kvn, wuk_ref[...]) + jnp.concatenate([kpe] * MLA_HEADS, axis=1)
    mk_ref[...] = k.astype(mk_ref.dtype)
    v = _dot(kvn, wuv_ref[...])
    lane = lax.broadcasted_iota(jnp.int32, v.shape, 1)
    value_lane = ((lane & (LANES - 1)) < MLA_V) == (((lane >> (LANES.bit_length() - 1)) & 1) == 0)
    mv_ref[...] = jnp.where(value_lane, v, 1.0).astype(mv_ref.dtype)

    g = _dot(u, w_ref[:, C_G:C_R])
    n_qk = GQA_HEADS * GQA_DIM + 2 * GQA_KV_HEADS * GQA_DIM
    gqa_ref[:, 0:n_qk] = _rope(g[:, 0:n_qk], cos64, sin64, GQA_DIM // 4).astype(gqa_ref.dtype)
    gqa_ref[:, n_qk:] = g[:, n_qk:].astype(gqa_ref.dtype)

    r = _dot(u, w_ref[:, C_R:C_GATE])
    n_qk = 2 * RET_HEADS * RET_QK
    ret_ref[:, 0:n_qk] = _rope(r[:, 0:n_qk], cos64, sin64, RET_QK // 4)
    ret_ref[:, n_qk:] = r[:, n_qk:]

    gate_ref[...] = _dot(u, w_ref[:, C_GATE:W_COLS]).astype(gate_ref.dtype)


def _inproj(h, mods, gpre, w_all, rope, qn, kvn, wuq, wuk, wuv, *, n_batch, seq):
    t, d = h.shape
    tiles_per_batch = seq // TM
    n_lat_tiles = n_batch * tiles_per_batch
    const = lambda i: (0, 0)
    rows = lambda i: (i, 0)
    rope_idx = lambda i: (jnp.where(i < n_lat_tiles, i % tiles_per_batch, tiles_per_batch), 0)
    hq = MLA_HEADS * LANES
    outs = [jax.ShapeDtypeStruct((t, hq), BF16), jax.ShapeDtypeStruct((t, hq), BF16),
            jax.ShapeDtypeStruct((t, hq), BF16),
            jax.ShapeDtypeStruct((t, C_R - C_G), BF16),
            jax.ShapeDtypeStruct((t, C_GATE - C_R), F32),
            jax.ShapeDtypeStruct((t, W_COLS - C_GATE), BF16)]
    return pl.pallas_call(
        functools.partial(_inproj_kernel, tiles_per_batch=tiles_per_batch, n_batch=n_batch, d=d),
        grid=(t // TM,),
        in_specs=[pl.BlockSpec((TM, d), rows),
                  pl.BlockSpec(mods.shape, const),
                  pl.BlockSpec((1, d), const),
                  pl.BlockSpec(w_all.shape, const),
                  pl.BlockSpec((TM, 4 * LANES), rope_idx),
                  pl.BlockSpec(qn.shape, const), pl.BlockSpec(kvn.shape, const),
                  pl.BlockSpec(wuq.shape, const), pl.BlockSpec(wuk.shape, const), pl.BlockSpec(wuv.shape, const)],
        out_specs=[pl.BlockSpec((TM, o.shape[1]), rows) for o in outs],
        out_shape=outs,
        compiler_params=_cparams(("arbitrary",)),
        name="inproj",
    )(h, mods, gpre, w_all, rope, qn, kvn, wuq, wuk, wuv)


MLA_HEADS_PER_STEP = 4


def _mla_kernel(q_ref, kl_ref, kc_ref, vl_ref, vc_ref, o_ref, s_ref, p_ref, *, n_lat_tiles, has_ctx_tile):
    n_ctx = kc_ref.shape[0]

    def body(with_lat):
        n_keys = n_ctx + (kl_ref.shape[0] if with_lat else 0)

        def scores(h):
            sl = slice(h * LANES, (h + 1) * LANES)
            s_ref[h % 2, :, 0:n_ctx] = _dot_nt(q_ref[:, sl], kc_ref[:, sl])
            if with_lat:
                s_ref[h % 2, :, n_ctx:n_keys] = _dot_nt(q_ref[:, sl], kl_ref[:, sl])

        def probs(h):
            s = s_ref[h % 2, :, 0:n_keys]
            p_ref[h % 2, :, 0:n_keys] = jnp.exp2(s - jnp.max(s, axis=-1, keepdims=True)).astype(BF16)

        def weighted(h):
            sl = slice(h * LANES, (h + 1) * LANES)
            o = _dot(p_ref[h % 2, :, 0:n_ctx], vc_ref[:, sl])
            if with_lat:
                o = o + _dot(p_ref[h % 2, :, n_ctx:n_keys], vl_ref[:, sl])
            return o / pltpu.roll(o, LANES // 2, 1)

        outs = [None] * MLA_HEADS_PER_STEP
        scores(0)
        for h in range(MLA_HEADS_PER_STEP):
            if h + 1 < MLA_HEADS_PER_STEP:
                scores(h + 1)
            probs(h)
            outs[h] = weighted(h)
        for pr in range(MLA_HEADS_PER_STEP // 2):
            even, odd = outs[2 * pr], outs[2 * pr + 1]
            o_ref[:, pr * LANES:(pr + 1) * LANES] = jnp.where(_lane_lo(even.shape), even, odd).astype(o_ref.dtype)

    if has_ctx_tile:
        i = pl.program_id(2)
        pl.when(i < n_lat_tiles)(lambda: body(True))
        pl.when(i == n_lat_tiles)(lambda: body(False))
    else:
        body(True)


def _mla_attention(mq, mk, mv, *, n_batch, seq, ctx, with_ctx_queries):
    t = mq.shape[0]
    nq = seq // TM
    n_lat_blocks = n_batch * nq
    ctx_blk0 = n_batch * seq // ctx
    hps = MLA_HEADS_PER_STEP
    assert ctx == TM
    q_idx = lambda b, g, i: (jnp.where(i < nq, b * nq + i, n_lat_blocks + b), g)
    lat_idx = lambda b, g, i: (b, g)
    ctx_idx = lambda b, g, i: (ctx_blk0 + b, g)
    n_rows = t if with_ctx_queries else n_batch * seq
    return pl.pallas_call(
        functools.partial(_mla_kernel, n_lat_tiles=nq, has_ctx_tile=with_ctx_queries),
        grid=(n_batch, MLA_HEADS // hps, nq + (1 if with_ctx_queries else 0)),
        in_specs=[pl.BlockSpec((TM, hps * LANES), q_idx),
                  pl.BlockSpec((seq, hps * LANES), lat_idx),
                  pl.BlockSpec((ctx, hps * LANES), ctx_idx),
                  pl.BlockSpec((seq, hps * LANES), lat_idx),
                  pl.BlockSpec((ctx, hps * LANES), ctx_idx)],
        out_specs=pl.BlockSpec((TM, hps * MLA_V), q_idx),
        out_shape=jax.ShapeDtypeStruct((n_rows, MLA_HEADS * MLA_V), BF16),
        scratch_shapes=[pltpu.VMEM((2, TM, ctx + seq), F32), pltpu.VMEM((2, TM, ctx + seq), BF16)],
        compiler_params=_cparams(("arbitrary", "arbitrary", "arbitrary")),
        name="mla_attn",
    )(mq, mk, mk, mv, mv)


def _win_kernel(q_ref, kp_ref, kcur_ref, kn_ref, vp_ref, vcur_ref, vn_ref, kc_ref, vc_ref, sink_ref, o_ref,
                s_ref, p_ref, *, n_lat_tiles, seq, has_ctx_tile):
    i = pl.program_id(1)
    tq = q_ref.shape[0]
    group = GQA_HEADS // GQA_KV_HEADS

    def body(with_lat):
        n_ctx = kc_ref.shape[0]
        n_keys = n_ctx + (tq + 2 * WINDOW if with_lat else 0)
        if with_lat:
            q_pos = i * tq + lax.broadcasted_iota(jnp.int32, (tq, n_keys), 0)
            k_pos = i * tq - WINDOW - n_ctx + lax.broadcasted_iota(jnp.int32, (tq, n_keys), 1)
            in_band = (jnp.abs(q_pos - k_pos) <= WINDOW) & (k_pos >= 0) & (k_pos < seq)
            valid = in_band | (lax.broadcasted_iota(jnp.int32, (tq, n_keys), 1) < n_ctx)
        lo = _lane_lo((tq, LANES))
        lo_k = _lane_lo((n_keys, LANES))
        keys, values = [], []
        for kv in range(GQA_KV_HEADS):
            sl = slice(kv * LANES, (kv + 1) * LANES)
            if with_lat:
                k_all = jnp.concatenate([kc_ref[:, sl], kp_ref[:, sl], kcur_ref[:, sl], kn_ref[:, sl]], axis=0)
                v_all = jnp.concatenate([vc_ref[:, sl], vp_ref[:, sl], vcur_ref[:, sl], vn_ref[:, sl]], axis=0)
            else:
                k_all, v_all = kc_ref[:, sl], vc_ref[:, sl]
            keys.append(k_all)
            one = jnp.ones_like(v_all)
            values.append((jnp.where(lo_k, v_all, one), jnp.where(lo_k, one, v_all)))

        def scores(hd):
            kv, pair = hd // group, hd // 2
            qp = q_ref[:, pair * LANES:(pair + 1) * LANES]
            qm = jnp.where(lo if hd % 2 == 0 else jnp.logical_not(lo), qp, jnp.zeros_like(qp))
            s = _dot_nt(qm, keys[kv])
            s_ref[hd % 2, :, 0:n_keys] = jnp.where(valid, s, NEG_INF) if with_lat else s

        def probs(hd):
            s = s_ref[hd % 2, :, 0:n_keys]
            m = jnp.maximum(jnp.max(s, axis=-1, keepdims=True), sink_ref[hd:hd + 1, 0:1])
            p_ref[hd % 2, :, 0:n_keys] = jnp.exp2(s - m).astype(BF16)
            return jnp.exp2(sink_ref[hd:hd + 1, 0:1] - m)

        def weighted(hd, sink_term):
            o = _dot(p_ref[hd % 2, :, 0:n_keys], values[hd // group][hd % 2])
            return o / (pltpu.roll(o, LANES // 2, 1) + sink_term)

        outs = [None] * GQA_HEADS
        scores(0)
        for hd in range(GQA_HEADS):
            if hd + 1 < GQA_HEADS:
                scores(hd + 1)
            outs[hd] = weighted(hd, probs(hd))
        for pair in range(GQA_HEADS // 2):
            o_ref[:, pair * LANES:(pair + 1) * LANES] = jnp.where(
                lo, outs[2 * pair], outs[2 * pair + 1]).astype(o_ref.dtype)

    if has_ctx_tile:
        pl.when(i < n_lat_tiles)(lambda: body(True))
        pl.when(i == n_lat_tiles)(lambda: body(False))
    else:
        body(True)


def _window_attention(gqa, sink_tab, *, n_batch, seq, ctx, with_ctx_queries):
    t = gqa.shape[0]
    nq = seq // TM
    n_lat_blocks = n_batch * nq
    per_tile = TM // WINDOW
    n_win_blocks = seq // WINDOW
    assert ctx == TM
    ctx_blk0 = n_batch * seq // ctx
    nqk = GQA_HEADS * GQA_DIM
    kw = 2 * GQA_KV_HEADS * GQA_DIM
    k_col, v_col = nqk // kw, nqk // kw + 1
    q_idx = lambda b, i: (jnp.where(i < nq, b * nq + i, n_lat_blocks + b), 0)
    cur = lambda col: (lambda b, i: (b * nq + jnp.minimum(i, nq - 1), col))
    prev = lambda col: (lambda b, i: (b * n_win_blocks + jnp.clip(per_tile * i - 1, 0, n_win_blocks - 1), col))
    nxt = lambda col: (lambda b, i: (b * n_win_blocks + jnp.clip(per_tile * (i + 1), 0, n_win_blocks - 1), col))
    cidx = lambda col: (lambda b, i: (ctx_blk0 + b, col))
    n_rows = t if with_ctx_queries else n_batch * seq
    return pl.pallas_call(
        functools.partial(_win_kernel, n_lat_tiles=nq, seq=seq, has_ctx_tile=with_ctx_queries),
        grid=(n_batch, nq + (1 if with_ctx_queries else 0)),
        in_specs=[pl.BlockSpec((TM, nqk), q_idx),
                  pl.BlockSpec((WINDOW, kw), prev(k_col)), pl.BlockSpec((TM, kw), cur(k_col)),
                  pl.BlockSpec((WINDOW, kw), nxt(k_col)),
                  pl.BlockSpec((WINDOW, kw), prev(v_col)), pl.BlockSpec((TM, kw), cur(v_col)),
                  pl.BlockSpec((WINDOW, kw), nxt(v_col)),
                  pl.BlockSpec((ctx, kw), cidx(k_col)), pl.BlockSpec((ctx, kw), cidx(v_col)),
                  pl.BlockSpec(sink_tab.shape, lambda b, i: (0, 0))],
        out_specs=pl.BlockSpec((TM, nqk), q_idx),
        out_shape=jax.ShapeDtypeStruct((n_rows, nqk), BF16),
        scratch_shapes=[pltpu.VMEM((2, TM, ctx + TM + 2 * WINDOW), F32),
                        pltpu.VMEM((2, TM, ctx + TM + 2 * WINDOW), BF16)],
        compiler_params=_cparams(("arbitrary", "arbitrary")),
        name="win_attn",
    )(gqa, gqa, gqa, gqa, gqa, gqa, gqa, gqa, gqa, sink_tab)


def _ret_kernel(f_ref, b_ref, lg_ref, of_ref, ob_ref, sf_ref, sb_ref, qdec_ref, kdec_ref, cdec_ref, inner_ref):
    @pl.when(pl.program_id(1) == 0)
    def _():
        sf_ref[...] = jnp.zeros_like(sf_ref)
        sb_ref[...] = jnp.zeros_like(sb_ref)

    L = f_ref.shape[0]
    lo = _lane_lo((L, LANES))
    srow_lo = lax.broadcasted_iota(jnp.int32, (LANES, LANES), 0) < RET_QK
    nq = RET_HEADS * RET_QK
    n_pairs = RET_HEADS // 2

    @pl.when(pl.program_id(1) == 0)
    def _():
        ii = lax.broadcasted_iota(jnp.int32, (L, L), 0)
        jj = lax.broadcasted_iota(jnp.int32, (L, L), 1)
        row = lax.broadcasted_iota(jnp.int32, (L, LANES), 0).astype(F32)
        for direction, forward in enumerate((True, False)):
            dist = ii - jj if forward else jj - ii
            distf = jnp.maximum(dist, 0).astype(F32)
            for pr in range(n_pairs):
                r0 = direction * RET_HEADS + 2 * pr
                lg = [lg_ref[r0 + e:r0 + e + 1, :] for e in range(2)]
                lg_lane = jnp.where(lo, lg[0], lg[1])
                qdec_ref[direction * n_pairs + pr] = jnp.exp(lg_lane * ((row + 1.0) if forward else (L - row)))
                kdec_ref[direction * n_pairs + pr] = jnp.exp(lg_lane * ((L - 1.0 - row) if forward else row))
                cdec_ref[direction * n_pairs + pr] = jnp.where(srow_lo, jnp.exp(lg[0] * float(L)),
                                                               jnp.exp(lg[1] * float(L)))
                for e in range(2):
                    inner_ref[r0 + e] = jnp.where(dist >= 0, jnp.exp(lg[e][:, 0:1] * distf), 0.0)

    def scan_chunk(x_ref, o_ref, s_ref, direction):
        for pr in range(n_pairs):
            q = x_ref[:, pr * LANES:(pr + 1) * LANES]
            k = x_ref[:, nq + pr * LANES:nq + (pr + 1) * LANES]
            qd = q * qdec_ref[direction * n_pairs + pr]
            kdb = (k * kdec_ref[direction * n_pairs + pr]).astype(BF16)
            kb = k.astype(BF16)
            state = s_ref[pr]
            state_b = state.astype(BF16)
            upd = []
            for e in range(2):
                hd = 2 * pr + e
                keep = lo if e == 0 else jnp.logical_not(lo)
                v = x_ref[:, 2 * nq + hd * RET_V:2 * nq + (hd + 1) * RET_V].astype(BF16)
                attn = _dot_nt(jnp.where(keep, q, 0.0).astype(BF16), kb) * inner_ref[direction * RET_HEADS + hd]
                o = _dot(attn.astype(BF16), v) + _dot(jnp.where(keep, qd, 0.0).astype(BF16), state_b)
                o_ref[:, hd * RET_V:(hd + 1) * RET_V] = o
                upd.append(_dot_tn(kdb, v))
            s_ref[pr] = state * cdec_ref[direction * n_pairs + pr] + jnp.where(srow_lo, upd[0], upd[1])

    scan_chunk(f_ref, of_ref, sf_ref, 0)
    scan_chunk(b_ref, ob_ref, sb_ref, 1)


def _retention(ret, lg_tab, *, n_batch, seq, ctx):
    t = ret.shape[0]
    L = RET_CHUNK
    n_lat, n_ctx = seq // L, ctx // L
    ctx0 = n_batch * n_lat
    width = 2 * RET_HEADS * RET_QK + RET_HEADS * RET_V
    fwd = lambda b, s: (jnp.where(s < n_ctx, ctx0 + b * n_ctx + s, b * n_lat + s - n_ctx), 0)
    bwd = lambda b, s: (jnp.where(s < n_ctx, ctx0 + b * n_ctx + n_ctx - 1 - s, b * n_lat + n_lat - 1 - (s - n_ctx)), 0)
    out = jax.ShapeDtypeStruct((t, RET_HEADS * RET_V), F32)
    return pl.pallas_call(
        _ret_kernel,
        grid=(n_batch, n_lat + n_ctx),
        in_specs=[pl.BlockSpec((L, width), fwd), pl.BlockSpec((L, width), bwd),
                  pl.BlockSpec(lg_tab.shape, lambda b, s: (0, 0))],
        out_specs=[pl.BlockSpec((L, RET_HEADS * RET_V), fwd), pl.BlockSpec((L, RET_HEADS * RET_V), bwd)],
        out_shape=[out, out],
        scratch_shapes=[pltpu.VMEM((RET_HEADS // 2, LANES, RET_V), F32),
                        pltpu.VMEM((RET_HEADS // 2, LANES, RET_V), F32),
                        pltpu.VMEM((RET_HEADS, L, LANES), F32), pltpu.VMEM((RET_HEADS, L, LANES), F32),
                        pltpu.VMEM((RET_HEADS, LANES, RET_V), F32), pltpu.VMEM((2 * RET_HEADS, L, L), F32)],
        compiler_params=_cparams(("arbitrary", "arbitrary")),
        name="retention",
    )(ret, ret, lg_tab)


def _merge_kernel(a_ref, w_ref, of_ref, ob_ref, rg_ref, gt_ref, h_ref, mod_ref, gpost_ref, gffn_ref,
                  wa_ref, ww_ref, wr_ref, wo_ref, h1_ref, v_ref, vp_ref, *, tiles_per_batch, n_batch, d):
    i = pl.program_id(0)
    bi = jnp.minimum(i // tiles_per_batch, n_batch)
    o = of_ref[...] + ob_ref[...]
    normed = []
    for hd in range(RET_HEADS):
        oh = o[:, hd * RET_V:(hd + 1) * RET_V]
        dev = oh - jnp.mean(oh, axis=-1, keepdims=True)
        normed.append(dev * lax.rsqrt(jnp.mean(dev * dev, axis=-1, keepdims=True) + NORM_EPS))
    g = rg_ref[...]
    r = (g * _sigmoid(g)) * jnp.concatenate(normed, axis=1)
    y = (_sigmoid(gt_ref[:, 0:d].astype(F32)) * _dot(a_ref[...], wa_ref[...])
         + _sigmoid(gt_ref[:, d:2 * d].astype(F32)) * _dot(w_ref[...], ww_ref[...])
         + _sigmoid(gt_ref[:, 2 * d:3 * d].astype(F32)) * _dot(r.astype(BF16), wr_ref[...]))
    z = _dot(y.astype(BF16), wo_ref[...])
    g1 = mod_ref[pl.ds(bi, 1), 2 * d:3 * d]
    sh2 = mod_ref[pl.ds(bi, 1), 3 * d:4 * d]
    sc2 = mod_ref[pl.ds(bi, 1), 4 * d:5 * d]
    h1 = h_ref[...] + g1 * _rms(z, gpost_ref[...])
    h1_ref[...] = h1
    v = _rms(h1, gffn_ref[...]) * (1.0 + sc2) + sh2
    v_ref[...] = v.astype(v_ref.dtype)
    vp_ref[...] = _pack_halves(v)


def _merge(a, w, o_f, o_b, ret, gates, h, mods, gpost, gffn, wa, ww, wr, wo, *, n_rows, n_batch, seq):
    d = h.shape[1]
    rows = lambda i: (i, 0)
    const = lambda i: (0, 0)
    rv = RET_HEADS * RET_V
    rg_col = (2 * RET_HEADS * RET_QK + rv) // rv
    outs = [jax.ShapeDtypeStruct((n_rows, d), F32), jax.ShapeDtypeStruct((n_rows, d), BF16),
            jax.ShapeDtypeStruct((n_rows, d // 2), jnp.uint32)]
    return pl.pallas_call(
        functools.partial(_merge_kernel, tiles_per_batch=seq // TM, n_batch=n_batch, d=d),
        grid=(n_rows // TM,),
        in_specs=[pl.BlockSpec((TM, a.shape[1]), rows), pl.BlockSpec((TM, w.shape[1]), rows),
                  pl.BlockSpec((TM, rv), rows), pl.BlockSpec((TM, rv), rows),
                  pl.BlockSpec((TM, rv), lambda i: (i, rg_col)),
                  pl.BlockSpec((TM, 3 * d), rows), pl.BlockSpec((TM, d), rows),
                  pl.BlockSpec(mods.shape, const), pl.BlockSpec((1, d), const), pl.BlockSpec((1, d), const),
                  pl.BlockSpec(wa.shape, const), pl.BlockSpec(ww.shape, const),
                  pl.BlockSpec(wr.shape, const), pl.BlockSpec(wo.shape, const)],
        out_specs=[pl.BlockSpec((TM, o.shape[1]), rows) for o in outs],
        out_shape=outs,
        compiler_params=_cparams(("arbitrary",)),
        name="merge",
    )(a, w, o_f, o_b, ret, gates, h, mods, gpost, gffn, wa, ww, wr, wo)


def _router_kernel(v_ref, rw_ref, rb_ref, eidx_ref, rank_ref, w_ref, cnt_ref, carry_ref):
    @pl.when(pl.program_id(0) == 0)
    def _():
        carry_ref[...] = jnp.zeros_like(carry_ref)

    tm = v_ref.shape[0]
    scores = _sigmoid(_dot_nt(rw_ref[...], v_ref[...]))
    sel = scores + rb_ref[...]
    neg = -jnp.inf
    n_grp, per = N_EXPERT_GROUPS, EXPERTS_PER_GROUP

    sel3 = sel.reshape(n_grp, per, tm)
    member_id = lax.broadcasted_iota(jnp.int32, sel3.shape, 1)
    m1 = jnp.max(sel3, axis=1, keepdims=True)
    i1 = jnp.min(jnp.where(sel3 == m1, member_id, per), axis=1, keepdims=True)
    m2 = jnp.max(jnp.where(member_id == i1, neg, sel3), axis=1, keepdims=True)
    gscore = (m1 + m2).reshape(n_grp, tm)
    gid = lax.broadcasted_iota(jnp.int32, gscore.shape, 0)
    ahead = jnp.zeros(gscore.shape, jnp.int32)
    for gj in range(n_grp):
        other = gscore[gj:gj + 1, :]
        ahead = ahead + jnp.where((other > gscore) | ((other == gscore) & (gid > gj)), 1, 0)
    group_ok = (ahead < TOPK_GROUPS).reshape(n_grp, 1, tm)
    sel = jnp.where(group_ok, sel3, NEG_INF).reshape(N_EXPERTS, tm)

    eid = lax.broadcasted_iota(jnp.int32, sel.shape, 0)
    chosen = jnp.zeros(sel.shape, jnp.bool_)
    picks = []
    for _ in range(TOP_K):
        m = jnp.max(sel, axis=0, keepdims=True)
        idx = jnp.min(jnp.where(sel == m, eid, N_EXPERTS), axis=0, keepdims=True)
        hit = eid == idx
        chosen = chosen | hit
        sel = jnp.where(hit, neg, sel)
        picks.append(idx)
    w = jnp.where(chosen, scores, 0.0)
    gate = ROUTED_SCALE * w / jnp.sum(w, axis=0, keepdims=True)

    member = jnp.where(chosen, 1.0, 0.0)
    earlier = lax.broadcasted_iota(jnp.int32, (tm, tm), 0) < lax.broadcasted_iota(jnp.int32, (tm, tm), 1)
    pos = _dot(member.astype(BF16), jnp.where(earlier, 1.0, 0.0).astype(BF16)) + carry_ref[...]
    for k, idx in enumerate(picks):
        hit = eid == idx
        eidx_ref[k:k + 1, :] = idx
        rank_ref[k:k + 1, :] = jnp.sum(jnp.where(hit, pos, 0.0), axis=0, keepdims=True)
        w_ref[k:k + 1, :] = jnp.sum(jnp.where(hit, gate, 0.0), axis=0, keepdims=True)
    carry_ref[...] += jnp.sum(member, axis=1, keepdims=True)
    cnt_ref[...] = carry_ref[...]


def _router(v, rw_t, rb, tile0, n_tiles):
    d = v.shape[1]
    n_rows = n_tiles * TM
    cols = lambda i: (0, i)
    const = lambda i: (0, 0)
    outs = [jax.ShapeDtypeStruct((TOP_K, n_rows), jnp.int32), jax.ShapeDtypeStruct((TOP_K, n_rows), F32),
            jax.ShapeDtypeStruct((TOP_K, n_rows), F32), jax.ShapeDtypeStruct((N_EXPERTS, 1), F32)]
    return pl.pallas_call(
        _router_kernel,
        grid=(n_tiles,),
        in_specs=[pl.BlockSpec((TM, d), lambda i: (tile0 + i, 0)), pl.BlockSpec(rw_t.shape, const),
                  pl.BlockSpec(rb.shape, const)],
        out_specs=[pl.BlockSpec((TOP_K, TM), cols), pl.BlockSpec((TOP_K, TM), cols),
                   pl.BlockSpec((TOP_K, TM), cols), pl.BlockSpec((N_EXPERTS, 1), const)],
        out_shape=outs,
        scratch_shapes=[pltpu.VMEM((N_EXPERTS, 1), F32)],
        compiler_params=_cparams(("arbitrary",)),
        name="router",
    )(v, rw_t, rb)


def _slots_kernel(eidx_ref, rank_ref, cnt_ref, slot_ref):
    tm = eidx_ref.shape[1]
    eid = lax.broadcasted_iota(jnp.int32, (N_EXPERTS, tm), 0)
    for k in range(TOP_K):
        before = jnp.sum(jnp.where(eid < eidx_ref[k:k + 1, :], cnt_ref[...], 0.0), axis=0, keepdims=True)
        slot_ref[k:k + 1, :] = (before + rank_ref[k:k + 1, :]).astype(jnp.int32)


def _slots(eidx, rank, cnt):
    n_rows = eidx.shape[1]
    tm = next(c for c in (2048, 1024, 512, 256) if n_rows % c == 0)
    cols = lambda i: (0, i)
    return pl.pallas_call(
        _slots_kernel,
        grid=(n_rows // tm,),
        in_specs=[pl.BlockSpec((TOP_K, tm), cols), pl.BlockSpec((TOP_K, tm), cols),
                  pl.BlockSpec(cnt.shape, lambda i: (0, 0))],
        out_specs=pl.BlockSpec((TOP_K, tm), cols),
        out_shape=jax.ShapeDtypeStruct((TOP_K, n_rows), jnp.int32),
        compiler_params=_cparams(("arbitrary",)),
        name="slots",
    )(eidx, rank, cnt)


SC_WINDOW = 128


def _sc_mesh():
    return plsc.VectorSubcoreMesh(core_axis_name="core", subcore_axis_name="subcore")


def _sc_dispatch(rows, slot_t, n_out, row0):
    width = rows.shape[1]
    n_chunks = slot_t.shape[1] // SC_WINDOW
    info = plsc.get_sparse_core_info()
    n_workers = info.num_cores * info.num_subcores

    @functools.partial(
        pl.kernel, mesh=_sc_mesh(),
        out_type=jax.ShapeDtypeStruct((n_out, width), rows.dtype),
        scratch_types=[pltpu.VMEM((TOP_K, SC_WINDOW), jnp.int32), pltpu.VMEM((SC_WINDOW, width), rows.dtype)],
        name="moe_dispatch")
    def run(rows_hbm, idx_hbm, out_hbm, idx_v, rows_v):
        wid = lax.axis_index("subcore") * info.num_cores + lax.axis_index("core")

        @pl.loop(wid, n_chunks, step=n_workers)
        def _(c):
            r0 = pl.multiple_of(c * SC_WINDOW, SC_WINDOW)
            pltpu.sync_copy(idx_hbm.at[:, pl.ds(r0, SC_WINDOW)], idx_v)
            pltpu.sync_copy(rows_hbm.at[pl.ds(row0 + r0, SC_WINDOW)], rows_v)
            for k in range(TOP_K):
                pltpu.sync_copy(rows_v, out_hbm.at[idx_v.at[k]])

    return run(rows, slot_t)


def _sc_collect(rows, slot_t):
    n_picks, n_rows = slot_t.shape
    width = rows.shape[1]
    n_chunks = n_rows // SC_WINDOW
    info = plsc.get_sparse_core_info()
    n_workers = info.num_cores * info.num_subcores

    @functools.partial(
        pl.kernel, mesh=_sc_mesh(),
        out_type=jax.ShapeDtypeStruct((n_picks, n_rows, width), rows.dtype),
        scratch_types=[pltpu.VMEM((TOP_K, SC_WINDOW), jnp.int32), pltpu.VMEM((SC_WINDOW, width), rows.dtype)],
        name="moe_collect")
    def run(rows_hbm, idx_hbm, out_hbm, idx_v, rows_v):
        wid = lax.axis_index("subcore") * info.num_cores + lax.axis_index("core")

        @pl.loop(wid, n_chunks, step=n_workers)
        def _(c):
            r0 = pl.multiple_of(c * SC_WINDOW, SC_WINDOW)
            pltpu.sync_copy(idx_hbm.at[:, pl.ds(r0, SC_WINDOW)], idx_v)
            for k in range(TOP_K):
                pltpu.sync_copy(rows_hbm.at[idx_v.at[k]], rows_v)
                pltpu.sync_copy(rows_v, out_hbm.at[k, pl.ds(r0, SC_WINDOW)])

    return run(rows, slot_t)


EXPERT_TILE = 512


def _work_items(cnt, n_slots):
    counts = cnt[:, 0].astype(jnp.int32)
    ends = jnp.cumsum(counts)
    n_tiles = n_slots // EXPERT_TILE
    bounds = jnp.sort(jnp.concatenate([jnp.arange(n_tiles, dtype=jnp.int32) * EXPERT_TILE, ends - counts]))
    nxt = jnp.concatenate([bounds[1:], jnp.array([n_slots], jnp.int32)])
    tile = jnp.minimum(bounds // EXPERT_TILE, n_tiles - 1)
    expert = jnp.sum((ends[None, :] <= bounds[:, None]).astype(jnp.int32), axis=1)
    expert = jnp.minimum(expert, N_EXPERTS - 1)
    return tile, expert, bounds - tile * EXPERT_TILE, nxt - tile * EXPERT_TILE


XS_RING = 3


def _experts_kernel(tile_ref, exp_ref, lo_ref, hi_ref, xs_hbm, wg_ref, wu_ref, wd_ref, ys_ref,
                    acc_ref, wgb_ref, wub_ref, wdb_ref, xbuf_ref, xsem):
    i = pl.program_id(0)
    n_items = pl.num_programs(0)
    lo, hi = lo_ref[i], hi_ref[i]

    def tile_copy(item):
        slot = lax.rem(item, XS_RING)
        row0 = pl.multiple_of(tile_ref[item] * EXPERT_TILE, EXPERT_TILE)
        return pltpu.make_async_copy(xs_hbm.at[pl.ds(row0, EXPERT_TILE)], xbuf_ref.at[slot], xsem.at[slot])

    @pl.when(i == 0)
    def _():
        for ahead in range(XS_RING - 1):
            pl.when(ahead < n_items)(lambda: tile_copy(ahead).start())

    @pl.when(i + XS_RING - 1 < n_items)
    def _():
        tile_copy(i + XS_RING - 1).start()

    tile_copy(i).wait()
    xs_ref = xbuf_ref.at[lax.rem(i, XS_RING)]

    @pl.when((i == 0) | (exp_ref[i] != exp_ref[jnp.maximum(i - 1, 0)]))
    def _():
        wgb_ref[...] = wg_ref[...].astype(BF16)
        wub_ref[...] = wu_ref[...].astype(BF16)
        wdb_ref[...] = wd_ref[...].astype(BF16)

    def ffn():
        x_lo, x_hi = _unpack_halves(xs_ref[...])
        x_lo, x_hi = x_lo.astype(BF16), x_hi.astype(BF16)
        n = x_lo.shape[1]
        a = _dot(x_lo, wgb_ref[0:n, :]) + _dot(x_hi, wgb_ref[n:, :])
        u = _dot(x_lo, wub_ref[0:n, :]) + _dot(x_hi, wub_ref[n:, :])
        return _dot(((a * _sigmoid(a)) * u).astype(BF16), wdb_ref[...])

    whole = (lo == 0) & (hi == EXPERT_TILE)

    @pl.when(whole)
    def _():
        ys_ref[...] = _pack_halves(ffn())

    @pl.when(jnp.logical_not(whole) & (hi > lo))
    def _():
        y = ffn()
        row = lax.broadcasted_iota(jnp.int32, y.shape, 0)
        y = jnp.where((row >= lo) & (row < hi), y, 0.0)

        @pl.when(lo == 0)
        def _():
            acc_ref[...] = y

        @pl.when((lo > 0) & (hi < EXPERT_TILE))
        def _():
            acc_ref[...] += y

        @pl.when((lo > 0) & (hi == EXPERT_TILE))
        def _():
            ys_ref[...] = _pack_halves(acc_ref[...] + y)


def _experts(xs, items, layer, exp_wg, exp_wu, exp_wd):
    n_slots, half = xs.shape
    d, hid = exp_wg.shape[-2:]
    tile, expert, lo, hi = items
    grid_spec = pltpu.PrefetchScalarGridSpec(
        num_scalar_prefetch=4,
        grid=(tile.shape[0],),
        in_specs=[pl.BlockSpec(memory_space=pl.ANY),
                  pl.BlockSpec((None, None, d, hid), lambda i, t, e, lo, hi: (layer, e[i], 0, 0)),
                  pl.BlockSpec((None, None, d, hid), lambda i, t, e, lo, hi: (layer, e[i], 0, 0)),
                  pl.BlockSpec((None, None, hid, d), lambda i, t, e, lo, hi: (layer, e[i], 0, 0))],
        out_specs=pl.BlockSpec((EXPERT_TILE, half), lambda i, t, e, lo, hi: (t[i], 0)),
        scratch_shapes=[pltpu.VMEM((EXPERT_TILE, d), F32), pltpu.VMEM((d, hid), BF16),
                        pltpu.VMEM((d, hid), BF16), pltpu.VMEM((hid, d), BF16),
                        pltpu.VMEM((XS_RING, EXPERT_TILE, half), jnp.uint32),
                        pltpu.SemaphoreType.DMA((XS_RING,))])
    return pl.pallas_call(
        _experts_kernel,
        grid_spec=grid_spec,
        out_shape=jax.ShapeDtypeStruct((n_slots, half), jnp.uint32),
        compiler_params=_cparams(("arbitrary",)),
        name="experts",
    )(tile, expert, lo, hi, xs, exp_wg, exp_wu, exp_wd)


def _moe_out_kernel(yg_ref, w_ref, v_ref, sg_ref, su_ref, sd_ref, h1_ref, mod_ref, gpost_ref, *rest,
                    tile0, tiles_per_batch, n_batch, d):
    o_ref = rest[-1]
    i = tile0 + pl.program_id(0)
    x = v_ref[...]
    a = _dot(x, sg_ref[...])
    f = _dot(((a * _sigmoid(a)) * _dot(x, su_ref[...])).astype(BF16), sd_ref[...])
    n = d // 2
    f_lo, f_hi = f[:, :n], f[:, n:]
    w = w_ref[...]
    for k in range(TOP_K):
        y_lo, y_hi = _unpack_halves(yg_ref[k])
        wk = w[:, k:k + 1]
        f_lo = f_lo + wk * y_lo
        f_hi = f_hi + wk * y_hi
    f = jnp.concatenate([f_lo, f_hi], axis=1)
    bi = jnp.minimum(i // tiles_per_batch, n_batch)
    g2 = mod_ref[pl.ds(bi, 1), 5 * d:6 * d]
    o_ref[...] = h1_ref[...] + g2 * _rms(f, gpost_ref[...])


def _moe_out(yg, w, v, sg, su, sd, h1, mods, gpost, tile0, prev_out, *, n_batch, seq):
    n_rows, d = v.shape
    n_tiles = w.shape[0] // TM
    part = lambda i: (i, 0)
    rows = lambda i: (tile0 + i, 0)
    const = lambda i: (0, 0)
    in_specs = [pl.BlockSpec((TOP_K, TM, d // 2), lambda i: (0, i, 0)), pl.BlockSpec((TM, TOP_K), part),
                pl.BlockSpec((TM, d), rows),
                pl.BlockSpec(sg.shape, const), pl.BlockSpec(su.shape, const), pl.BlockSpec(sd.shape, const),
                pl.BlockSpec((TM, d), rows), pl.BlockSpec(mods.shape, const), pl.BlockSpec((1, d), const)]
    args = [yg, w, v, sg, su, sd, h1, mods, gpost]
    aliases = {}
    if prev_out is not None:
        in_specs.append(pl.BlockSpec(memory_space=pl.ANY))
        aliases = {len(args): 0}
        args.append(prev_out)
    return pl.pallas_call(
        functools.partial(_moe_out_kernel, tile0=tile0, tiles_per_batch=seq // TM, n_batch=n_batch, d=d),
        grid=(n_tiles,),
        in_specs=in_specs,
        out_specs=pl.BlockSpec((TM, d), rows),
        out_shape=jax.ShapeDtypeStruct((n_rows, d), F32),
        input_output_aliases=aliases,
        compiler_params=_cparams(("arbitrary",)),
        name="moe_out",
    )(*args)


MOE_PARTS = 1


def _moe(v, vp, layer, rw, rb, exp_wg, exp_wu, exp_wd, sg, su, sd, h1, mods, gpost, *, n_batch, seq):
    n_tiles = v.shape[0] // TM
    per_part = n_tiles // MOE_PARTS
    assert per_part * MOE_PARTS == n_tiles and (per_part * TM * TOP_K) % EXPERT_TILE == 0
    n_slots = per_part * TM * TOP_K
    routed = []
    for p in range(MOE_PARTS):
        eidx, rank, w_t, cnt = _router(v, rw, rb, p * per_part, per_part)
        routed.append((_slots(eidx, rank, cnt), w_t, cnt))
    xs = [_sc_dispatch(vp, slot_t, n_slots, p * per_part * TM) for p, (slot_t, _, _) in enumerate(routed)]
    ys = [_experts(x, _work_items(cnt, n_slots), layer, exp_wg, exp_wu, exp_wd)
          for x, (_, _, cnt) in zip(xs, routed)]
    out = None
    for p, (y, (slot_t, w_t, _)) in enumerate(zip(ys, routed)):
        yg = _sc_collect(y, slot_t)
        out = _moe_out(yg, w_t.T, v, sg, su, sd, h1, mods, gpost, p * per_part, out, n_batch=n_batch, seq=seq)
    return out


def _rope_tables(seq):
    rows = seq // GRID_W
    row_id = jnp.repeat(jnp.arange(rows, dtype=F32), GRID_W)
    col_id = jnp.tile(jnp.arange(GRID_W, dtype=F32), rows)

    def tables(rot_dim):
        axis_dim = rot_dim // 2
        inv_freq = ROPE_BASE ** (-jnp.arange(0, axis_dim, 2, dtype=F32) / axis_dim)
        ang_r = row_id[:, None] * inv_freq[None, :]
        ang_c = col_id[:, None] * inv_freq[None, :]
        cos = jnp.concatenate([jnp.cos(ang_r), jnp.cos(ang_r), jnp.cos(ang_c), jnp.cos(ang_c)], axis=1)
        sin = jnp.concatenate([-jnp.sin(ang_r), jnp.sin(ang_r), -jnp.sin(ang_c), jnp.sin(ang_c)], axis=1)
        return cos, sin

    cos64, sin64 = tables(GQA_DIM)
    cos32, sin32 = tables(MLA_ROPE)
    ones = jnp.ones((seq, MLA_NOPE), F32)
    pad = LANES - MLA_NOPE - MLA_ROPE
    cospe = jnp.concatenate([ones, cos32, jnp.ones((seq, pad), F32)], axis=1)
    sinpe = jnp.concatenate([0 * ones, sin32, jnp.zeros((seq, pad), F32)], axis=1)
    tab = jnp.concatenate([cos64, cos64, sin64, sin64, cospe, sinpe], axis=1)
    ident = jnp.concatenate([jnp.ones((TM, LANES), F32), jnp.zeros((TM, LANES), F32),
                             jnp.ones((TM, LANES), F32), jnp.zeros((TM, LANES), F32)], axis=1)
    return jnp.concatenate([tab, ident], axis=0)


def _pack_w_in(w):
    d = w.shape[0]
    sizes = (MLA_Q_LORA, MLA_KV_LORA, MLA_ROPE, GQA_HEADS * GQA_DIM, GQA_KV_HEADS * GQA_DIM,
             GQA_KV_HEADS * GQA_DIM, RET_HEADS * RET_QK, RET_HEADS * RET_QK, RET_HEADS * RET_V,
             RET_HEADS * RET_V, 3 * d)
    offs, parts = 0, []
    for s in sizes:
        parts.append(w[:, offs:offs + s])
        offs += s
    cq, ckv, kpe, gq, gk, gv, rq, rk, rv, rg, gates = parts

    def twice(m):
        heads = [m[:, i * GQA_DIM:(i + 1) * GQA_DIM] for i in range(GQA_KV_HEADS)]
        return jnp.concatenate([hh for hd in heads for hh in (hd, hd)], axis=1)

    kpe_slab = jnp.concatenate([jnp.zeros((d, MLA_NOPE), F32), kpe,
                                jnp.zeros((d, LANES - MLA_NOPE - MLA_ROPE), F32)], axis=1)
    packed = jnp.concatenate([cq, ckv, kpe_slab, gq * (GQA_DIM ** -0.5 * LOG2_E), twice(gk), twice(gv),
                              rq, rk * RET_QK ** -0.5, rv, rg, gates], axis=1)
    assert packed.shape[1] == W_COLS
    return packed.astype(BF16)


def _pack_mla_up(w_uq, w_ukv):
    r = w_uq.shape[0]
    dq = MLA_NOPE + MLA_ROPE
    wq = jnp.pad(w_uq.reshape(r, MLA_HEADS, dq), ((0, 0), (0, 0), (0, LANES - dq))).reshape(r, MLA_HEADS * LANES)
    kv = w_ukv.reshape(r, MLA_HEADS, MLA_NOPE + MLA_V)
    wk = jnp.pad(kv[:, :, :MLA_NOPE], ((0, 0), (0, 0), (0, LANES - MLA_NOPE))).reshape(r, MLA_HEADS * LANES)
    wv = kv[:, :, MLA_NOPE:]
    zeros = jnp.zeros_like(wv)
    even = jnp.concatenate([wv, zeros], axis=2)
    odd = jnp.concatenate([zeros, wv], axis=2)
    wv = jnp.where((jnp.arange(MLA_HEADS) % 2 == 0)[None, :, None], even, odd).reshape(r, MLA_HEADS * LANES)
    return wq.astype(BF16), wk.astype(BF16), wv.astype(BF16)


def kernel(x, c, ctx, c_ctx, ada_w, ada_b, norm_mix_pre, norm_mix_post, norm_ffn_pre, norm_ffn_post, w_in, mla_q_norm, mla_w_uq, mla_kv_norm, mla_w_ukv, gqa_sink, ret_decay_fwd, ret_decay_bwd, w_br_mla, w_br_gqa, w_br_ret, w_out, router_w, router_bias, exp_w_gate, exp_w_up, exp_w_down, shared_w_gate, shared_w_up, shared_w_down):
    n_batch, seq, d = x.shape
    n_ctx = ctx.shape[1]
    depth = ada_w.shape[0]
    n_lat_rows = n_batch * seq
    assert seq % TM == 0 and n_ctx == TM and n_batch < MOD_ROWS and seq % GRID_W == 0

    cond = jnp.zeros((MOD_ROWS, d), F32).at[:n_batch].set(c).at[n_batch].set(c_ctx)
    mods_all = _adaln(cond, ada_w, ada_b)
    rope = _rope_tables(seq)
    h = jnp.concatenate([x.reshape(n_lat_rows, d), ctx.reshape(n_batch * n_ctx, d)], axis=0)
    row = lambda p: p.reshape(1, -1)
    dims = dict(n_batch=n_batch, seq=seq)

    for l in range(depth):
        last = l == depth - 1
        mods = mods_all[l]
        wq, wk, wv = _pack_mla_up(mla_w_uq[l], mla_w_ukv[l])
        mq, mk, mv, gqa, ret, gates = _inproj(h, mods, row(norm_mix_pre[l]), _pack_w_in(w_in[l]), rope,
                                              row(mla_q_norm[l]), row(mla_kv_norm[l]), wq, wk, wv, **dims)
        a = _mla_attention(mq, mk, mv, ctx=n_ctx, with_ctx_queries=not last, **dims)
        sink_tab = jnp.broadcast_to(gqa_sink[l].astype(F32)[:, None] * LOG2_E, (GQA_HEADS, LANES))
        w = _window_attention(gqa, sink_tab, ctx=n_ctx, with_ctx_queries=not last, **dims)
        lg = jnp.concatenate([jax.nn.log_sigmoid(ret_decay_fwd[l].astype(F32)),
                              jax.nn.log_sigmoid(ret_decay_bwd[l].astype(F32))])
        o_f, o_b = _retention(ret, jnp.broadcast_to(lg[:, None], (2 * RET_HEADS, LANES)), ctx=n_ctx, **dims)
        n_rows = n_lat_rows if last else h.shape[0]
        h1, v, vp = _merge(a, w, o_f, o_b, ret, gates, h, mods, row(norm_mix_post[l]), row(norm_ffn_pre[l]),
                           w_br_mla[l].astype(BF16), w_br_gqa[l].astype(BF16), w_br_ret[l].astype(BF16),
                           w_out[l].astype(BF16), n_rows=n_rows, **dims)
        h = _moe(v, vp, l, router_w[l].T.astype(BF16), router_bias[l].astype(F32).reshape(-1, 1),
                 exp_w_gate, exp_w_up, exp_w_down, shared_w_gate[l].astype(BF16),
                 shared_w_up[l].astype(BF16), shared_w_down[l].astype(BF16), h1, mods,
                 row(norm_ffn_post[l]), **dims)
    return h[:n_lat_rows].reshape(n_batch, seq, d)
```

```python
import functools

import jax
import jax.numpy as jnp
from jax import lax
from jax.experimental import pallas as pl
from jax.experimental.pallas import tpu as pltpu
from jax.experimental.pallas import tpu_sc as plsc

F32 = jnp.float32
BF16 = jnp.bfloat16

GRID_W = 64
ROPE_BASE = 10000.0
NORM_EPS = 1e-6
NEG_INF = -1e30
LOG2_E = 1.4426950408889634
N_MOD = 6
MLA_HEADS, MLA_NOPE, MLA_ROPE, MLA_V = 8, 64, 32, 64
MLA_Q_LORA, MLA_KV_LORA = 256, 256
GQA_HEADS, GQA_KV_HEADS, GQA_DIM, WINDOW = 8, 2, 64, 128
RET_HEADS, RET_QK, RET_V, RET_CHUNK = 4, 64, 128, 128
N_EXPERTS, N_EXPERT_GROUPS, TOPK_GROUPS, TOP_K = 64, 8, 4, 8
EXPERTS_PER_GROUP = N_EXPERTS // N_EXPERT_GROUPS
ROUTED_SCALE = 2.5

LANES = 128
TM = 512
ATT_TQ = 256
MOD_ROWS = 8
V7X_VMEM_LIMIT = 56 * 1024 * 1024

C_CQ, C_CKV, C_KPE = 0, 256, 512
C_G = 640
C_R = 1664
C_GATE = 3200
W_COLS = 6272


def _cparams(sem):
    return pltpu.CompilerParams(dimension_semantics=sem, vmem_limit_bytes=V7X_VMEM_LIMIT)


def _rms(x, g):
    return x * lax.rsqrt(jnp.mean(x * x, axis=-1, keepdims=True) + NORM_EPS) * g


def _sigmoid(x):
    return 0.5 * jnp.tanh(0.5 * x) + 0.5


def _dot(a, b):
    return jnp.dot(a, b, preferred_element_type=F32)


def _dot_nt(a, b):
    return lax.dot_general(a, b, (((1,), (1,)), ((), ())), preferred_element_type=F32)


def _dot_tn(a, b):
    return lax.dot_general(a, b, (((0,), (0,)), ((), ())), preferred_element_type=F32)


def _rope(x, cos, sin, half):
    n = x.shape[-1]
    reps = n // LANES
    if reps > 1:
        cos = jnp.concatenate([cos] * reps, axis=1)
        sin = jnp.concatenate([sin] * reps, axis=1)
    lane = lax.broadcasted_iota(jnp.int32, x.shape, 1)
    up = pltpu.roll(x, half, 1)
    dn = pltpu.roll(x, n - half, 1)
    partner = jnp.where((lane & (2 * half - 1)) < half, dn, up)
    return x * cos + partner * sin


def _lane_lo(shape):
    return (lax.broadcasted_iota(jnp.int32, shape, 1) & (LANES - 1)) < (LANES // 2)


def _pack_halves(x):
    n = x.shape[1] // 2
    bits = lambda t: lax.bitcast_convert_type(t.astype(BF16).astype(F32), jnp.uint32)
    return (bits(x[:, :n]) >> 16) | bits(x[:, n:])


def _unpack_halves(p):
    lo = lax.bitcast_convert_type(p << 16, F32)
    hi = lax.bitcast_convert_type(p & jnp.uint32(0xFFFF0000), F32)
    return lo, hi


def _ada_kernel(c_ref, w_ref, b_ref, o_ref):
    c = c_ref[...]
    s = c * _sigmoid(c)
    o_ref[...] = _dot(s.astype(BF16), w_ref[...].astype(BF16)) + b_ref[...]


def _adaln(cond, ada_w, ada_b):
    n_layers, d, n = ada_w.shape
    tn = 1024
    return pl.pallas_call(
        _ada_kernel,
        grid=(n_layers, n // tn),
        in_specs=[pl.BlockSpec((MOD_ROWS, d), lambda l, j: (0, 0)),
                  pl.BlockSpec((None, d, tn), lambda l, j: (l, 0, j)),
                  pl.BlockSpec((None, 1, tn), lambda l, j: (l, 0, j))],
        out_specs=pl.BlockSpec((None, MOD_ROWS, tn), lambda l, j: (l, 0, j)),
        out_shape=jax.ShapeDtypeStruct((n_layers, MOD_ROWS, n), F32),
        compiler_params=_cparams(("arbitrary", "arbitrary")),
        name="adaln",
    )(cond, ada_w, ada_b.reshape(n_layers, 1, n))


def _inproj_kernel(h_ref, mod_ref, gpre_ref, w_ref, rope_ref, qn_ref, kvn_ref, wuq_ref, wuk_ref, wuv_ref,
                   mq_ref, mk_ref, mv_ref, gqa_ref, ret_ref, gate_ref, *, tiles_per_batch, n_batch, d):
    i = pl.program_id(0)
    bi = jnp.minimum(i // tiles_per_batch, n_batch)
    sh = mod_ref[pl.ds(bi, 1), 0:d]
    sc = mod_ref[pl.ds(bi, 1), d:2 * d]
    u = (_rms(h_ref[...], gpre_ref[...]) * (1.0 + sc) + sh).astype(BF16)

    cos64 = rope_ref[:, 0:LANES]
    sin64 = rope_ref[:, LANES:2 * LANES]
    cospe = rope_ref[:, 2 * LANES:3 * LANES]
    sinpe = rope_ref[:, 3 * LANES:4 * LANES]

    c = _dot(u, w_ref[:, C_CQ:C_G])
    kpe = _rope(c[:, C_KPE:C_G], cospe, sinpe, MLA_ROPE // 4)
    qn = _rms(c[:, C_CQ:C_CKV], qn_ref[...]).astype(BF16)
    q = _rope(_dot(qn, wuq_ref[...]), cospe, sinpe, MLA_ROPE // 4)
    mq_ref[...] = (q * ((MLA_NOPE + MLA_ROPE) ** -0.5 * LOG2_E)).astype(mq_ref.dtype)
    kvn = _rms(c[:, C_CKV:C_KPE], kvn_ref[...]).astype(BF16)
    k = _dot(kvn, wuk_ref[...]) + jnp.concatenate([kpe] * MLA_HEADS, axis=1)
    mk_ref[...] = k.astype(mk_ref.dtype)
    v = _dot(kvn, wuv_ref[...])
    lane = lax.broadcasted_iota(jnp.int32, v.shape, 1)
    value_lane = ((lane & (LANES - 1)) < MLA_V) == (((lane >> (LANES.bit_length() - 1)) & 1) == 0)
    mv_ref[...] = jnp.where(value_lane, v, 1.0).astype(mv_ref.dtype)

    g = _dot(u, w_ref[:, C_G:C_R])
    n_qk = GQA_HEADS * GQA_DIM + 2 * GQA_KV_HEADS * GQA_DIM
    gqa_ref[:, 0:n_qk] = _rope(g[:, 0:n_qk], cos64, sin64, GQA_DIM // 4).astype(gqa_ref.dtype)
    gqa_ref[:, n_qk:] = g[:, n_qk:].astype(gqa_ref.dtype)

    r = _dot(u, w_ref[:, C_R:C_GATE])
    n_qk = 2 * RET_HEADS * RET_QK
    ret_ref[:, 0:n_qk] = _rope(r[:, 0:n_qk], cos64, sin64, RET_QK // 4)
    ret_ref[:, n_qk:] = r[:, n_qk:]

    gate_ref[...] = _dot(u, w_ref[:, C_GATE:W_COLS]).astype(gate_ref.dtype)


def _inproj(h, mods, gpre, w_all, rope, qn, kvn, wuq, wuk, wuv, *, n_batch, seq):
    t, d = h.shape
    tiles_per_batch = seq // TM
    n_lat_tiles = n_batch * tiles_per_batch
    const = lambda i: (0, 0)
    rows = lambda i: (i, 0)
    rope_idx = lambda i: (jnp.where(i < n_lat_tiles, i % tiles_per_batch, tiles_per_batch), 0)
    hq = MLA_HEADS * LANES
    outs = [jax.ShapeDtypeStruct((t, hq), BF16), jax.ShapeDtypeStruct((t, hq), BF16),
            jax.ShapeDtypeStruct((t, hq), BF16),
            jax.ShapeDtypeStruct((t, C_R - C_G), BF16),
            jax.ShapeDtypeStruct((t, C_GATE - C_R), F32),
            jax.ShapeDtypeStruct((t, W_COLS - C_GATE), BF16)]
    return pl.pallas_call(
        functools.partial(_inproj_kernel, tiles_per_batch=tiles_per_batch, n_batch=n_batch, d=d),
        grid=(t // TM,),
        in_specs=[pl.BlockSpec((TM, d), rows),
                  pl.BlockSpec(mods.shape, const),
                  pl.BlockSpec((1, d), const),
                  pl.BlockSpec(w_all.shape, const, pipeline_mode=pl.Buffered(1)),
                  pl.BlockSpec((TM, 4 * LANES), rope_idx),
                  pl.BlockSpec(qn.shape, const), pl.BlockSpec(kvn.shape, const),
                  pl.BlockSpec(wuq.shape, const), pl.BlockSpec(wuk.shape, const), pl.BlockSpec(wuv.shape, const)],
        out_specs=[pl.BlockSpec((TM, o.shape[1]), rows) for o in outs],
        out_shape=outs,
        compiler_params=_cparams(("arbitrary",)),
        name="inproj",
    )(h, mods, gpre, w_all, rope, qn, kvn, wuq, wuk, wuv)


MLA_HEADS_PER_STEP = 4


def _mla_kernel(q_ref, kl_ref, kc_ref, vl_ref, vc_ref, o_ref, s_ref, p_ref, *, n_lat_tiles, has_ctx_tile):
    n_ctx = kc_ref.shape[0]

    def body(with_lat):
        n_keys = n_ctx + (kl_ref.shape[0] if with_lat else 0)

        def scores(h):
            sl = slice(h * LANES, (h + 1) * LANES)
            s_ref[h % 2, :, 0:n_ctx] = _dot_nt(q_ref[:, sl], kc_ref[:, sl])
            if with_lat:
                s_ref[h % 2, :, n_ctx:n_keys] = _dot_nt(q_ref[:, sl], kl_ref[:, sl])

        def probs(h):
            s = s_ref[h % 2, :, 0:n_keys]
            p_ref[h % 2, :, 0:n_keys] = jnp.exp2(s - jnp.max(s, axis=-1, keepdims=True)).astype(BF16)

        def weighted(h):
            sl = slice(h * LANES, (h + 1) * LANES)
            o = _dot(p_ref[h % 2, :, 0:n_ctx], vc_ref[:, sl])
            if with_lat:
                o = o + _dot(p_ref[h % 2, :, n_ctx:n_keys], vl_ref[:, sl])
            return o / pltpu.roll(o, LANES // 2, 1)

        outs = [None] * MLA_HEADS_PER_STEP
        scores(0)
        for h in range(MLA_HEADS_PER_STEP):
            if h + 1 < MLA_HEADS_PER_STEP:
                scores(h + 1)
            probs(h)
            outs[h] = weighted(h)
        for pr in range(MLA_HEADS_PER_STEP // 2):
            even, odd = outs[2 * pr], outs[2 * pr + 1]
            o_ref[:, pr * LANES:(pr + 1) * LANES] = jnp.where(_lane_lo(even.shape), even, odd).astype(o_ref.dtype)

    if has_ctx_tile:
        i = pl.program_id(2)
        pl.when(i < n_lat_tiles)(lambda: body(True))
        pl.when(i == n_lat_tiles)(lambda: body(False))
    else:
        body(True)


def _mla_attention(mq, mk, mv, *, n_batch, seq, ctx, with_ctx_queries):
    t = mq.shape[0]
    tq = ATT_TQ
    nq = seq // tq
    n_lat_blocks = n_batch * nq
    ctx_blk0 = n_batch * seq // ctx
    hps = MLA_HEADS_PER_STEP
    assert ctx == tq
    q_idx = lambda b, g, i: (jnp.where(i < nq, b * nq + i, n_lat_blocks + b), g)
    lat_idx = lambda b, g, i: (b, g)
    ctx_idx = lambda b, g, i: (ctx_blk0 + b, g)
    n_rows = t if with_ctx_queries else n_batch * seq
    return pl.pallas_call(
        functools.partial(_mla_kernel, n_lat_tiles=nq, has_ctx_tile=with_ctx_queries),
        grid=(n_batch, MLA_HEADS // hps, nq + (1 if with_ctx_queries else 0)),
        in_specs=[pl.BlockSpec((tq, hps * LANES), q_idx),
                  pl.BlockSpec((seq, hps * LANES), lat_idx),
                  pl.BlockSpec((ctx, hps * LANES), ctx_idx),
                  pl.BlockSpec((seq, hps * LANES), lat_idx),
                  pl.BlockSpec((ctx, hps * LANES), ctx_idx)],
        out_specs=pl.BlockSpec((tq, hps * MLA_V), q_idx),
        out_shape=jax.ShapeDtypeStruct((n_rows, MLA_HEADS * MLA_V), BF16),
        scratch_shapes=[pltpu.VMEM((2, tq, ctx + seq), F32), pltpu.VMEM((2, tq, ctx + seq), BF16)],
        compiler_params=_cparams(("arbitrary", "arbitrary", "arbitrary")),
        name="mla_attn",
    )(mq, mk, mk, mv, mv)


def _win_kernel(q_ref, kp_ref, kcur_ref, kn_ref, vp_ref, vcur_ref, vn_ref, kc_ref, vc_ref, sink_ref, o_ref,
                s_ref, p_ref, *, n_lat_tiles, seq, has_ctx_tile):
    i = pl.program_id(1)
    tq = q_ref.shape[0]
    group = GQA_HEADS // GQA_KV_HEADS

    def body(with_lat):
        n_ctx = kc_ref.shape[0]
        n_keys = n_ctx + (tq + 2 * WINDOW if with_lat else 0)
        if with_lat:
            q_pos = i * tq + lax.broadcasted_iota(jnp.int32, (tq, n_keys), 0)
            k_pos = i * tq - WINDOW - n_ctx + lax.broadcasted_iota(jnp.int32, (tq, n_keys), 1)
            in_band = (jnp.abs(q_pos - k_pos) <= WINDOW) & (k_pos >= 0) & (k_pos < seq)
            valid = in_band | (lax.broadcasted_iota(jnp.int32, (tq, n_keys), 1) < n_ctx)
        lo = _lane_lo((tq, LANES))
        lo_k = _lane_lo((n_keys, LANES))
        keys, values = [], []
        for kv in range(GQA_KV_HEADS):
            sl = slice(kv * LANES, (kv + 1) * LANES)
            if with_lat:
                k_all = jnp.concatenate([kc_ref[:, sl], kp_ref[:, sl], kcur_ref[:, sl], kn_ref[:, sl]], axis=0)
                v_all = jnp.concatenate([vc_ref[:, sl], vp_ref[:, sl], vcur_ref[:, sl], vn_ref[:, sl]], axis=0)
            else:
                k_all, v_all = kc_ref[:, sl], vc_ref[:, sl]
            keys.append(k_all)
            one = jnp.ones_like(v_all)
            values.append((jnp.where(lo_k, v_all, one), jnp.where(lo_k, one, v_all)))

        def scores(hd):
            kv, pair = hd // group, hd // 2
            qp = q_ref[:, pair * LANES:(pair + 1) * LANES]
            qm = jnp.where(lo if hd % 2 == 0 else jnp.logical_not(lo), qp, jnp.zeros_like(qp))
            s = _dot_nt(qm, keys[kv])
            s_ref[hd % 2, :, 0:n_keys] = jnp.where(valid, s, NEG_INF) if with_lat else s

        def probs(hd):
            s = s_ref[hd % 2, :, 0:n_keys]
            m = jnp.maximum(jnp.max(s, axis=-1, keepdims=True), sink_ref[hd:hd + 1, 0:1])
            p_ref[hd % 2, :, 0:n_keys] = jnp.exp2(s - m).astype(BF16)
            return jnp.exp2(sink_ref[hd:hd + 1, 0:1] - m)

        def weighted(hd, sink_term):
            o = _dot(p_ref[hd % 2, :, 0:n_keys], values[hd // group][hd % 2])
            return o / (pltpu.roll(o, LANES // 2, 1) + sink_term)

        outs = [None] * GQA_HEADS
        scores(0)
        for hd in range(GQA_HEADS):
            if hd + 1 < GQA_HEADS:
                scores(hd + 1)
            outs[hd] = weighted(hd, probs(hd))
        for pair in range(GQA_HEADS // 2):
            o_ref[:, pair * LANES:(pair + 1) * LANES] = jnp.where(
                lo, outs[2 * pair], outs[2 * pair + 1]).astype(o_ref.dtype)

    if has_ctx_tile:
        pl.when(i < n_lat_tiles)(lambda: body(True))
        pl.when(i == n_lat_tiles)(lambda: body(False))
    else:
        body(True)


def _window_attention(gqa, sink_tab, *, n_batch, seq, ctx, with_ctx_queries):
    t = gqa.shape[0]
    tq = ATT_TQ
    nq = seq // tq
    n_lat_blocks = n_batch * nq
    per_tile = tq // WINDOW
    n_win_blocks = seq // WINDOW
    assert ctx == tq
    ctx_blk0 = n_batch * seq // ctx
    nqk = GQA_HEADS * GQA_DIM
    kw = 2 * GQA_KV_HEADS * GQA_DIM
    k_col, v_col = nqk // kw, nqk // kw + 1
    q_idx = lambda b, i: (jnp.where(i < nq, b * nq + i, n_lat_blocks + b), 0)
    cur = lambda col: (lambda b, i: (b * nq + jnp.minimum(i, nq - 1), col))
    prev = lambda col: (lambda b, i: (b * n_win_blocks + jnp.clip(per_tile * i - 1, 0, n_win_blocks - 1), col))
    nxt = lambda col: (lambda b, i: (b * n_win_blocks + jnp.clip(per_tile * (i + 1), 0, n_win_blocks - 1), col))
    cidx = lambda col: (lambda b, i: (ctx_blk0 + b, col))
    n_rows = t if with_ctx_queries else n_batch * seq
    return pl.pallas_call(
        functools.partial(_win_kernel, n_lat_tiles=nq, seq=seq, has_ctx_tile=with_ctx_queries),
        grid=(n_batch, nq + (1 if with_ctx_queries else 0)),
        in_specs=[pl.BlockSpec((tq, nqk), q_idx),
                  pl.BlockSpec((WINDOW, kw), prev(k_col)), pl.BlockSpec((tq, kw), cur(k_col)),
                  pl.BlockSpec((WINDOW, kw), nxt(k_col)),
                  pl.BlockSpec((WINDOW, kw), prev(v_col)), pl.BlockSpec((tq, kw), cur(v_col)),
                  pl.BlockSpec((WINDOW, kw), nxt(v_col)),
                  pl.BlockSpec((ctx, kw), cidx(k_col)), pl.BlockSpec((ctx, kw), cidx(v_col)),
                  pl.BlockSpec(sink_tab.shape, lambda b, i: (0, 0))],
        out_specs=pl.BlockSpec((tq, nqk), q_idx),
        out_shape=jax.ShapeDtypeStruct((n_rows, nqk), BF16),
        scratch_shapes=[pltpu.VMEM((2, tq, ctx + tq + 2 * WINDOW), F32),
                        pltpu.VMEM((2, tq, ctx + tq + 2 * WINDOW), BF16)],
        compiler_params=_cparams(("arbitrary", "arbitrary")),
        name="win_attn",
    )(gqa, gqa, gqa, gqa, gqa, gqa, gqa, gqa, gqa, sink_tab)


def _ret_kernel(f_ref, b_ref, lg_ref, of_ref, ob_ref, sf_ref, sb_ref, qdec_ref, kdec_ref, cdec_ref, inner_ref):
    @pl.when(pl.program_id(1) == 0)
    def _():
        sf_ref[...] = jnp.zeros_like(sf_ref)
        sb_ref[...] = jnp.zeros_like(sb_ref)

    L = f_ref.shape[0]
    lo = _lane_lo((L, LANES))
    srow_lo = lax.broadcasted_iota(jnp.int32, (LANES, LANES), 0) < RET_QK
    nq = RET_HEADS * RET_QK
    n_pairs = RET_HEADS // 2

    @pl.when(pl.program_id(1) == 0)
    def _():
        ii = lax.broadcasted_iota(jnp.int32, (L, L), 0)
        jj = lax.broadcasted_iota(jnp.int32, (L, L), 1)
        row = lax.broadcasted_iota(jnp.int32, (L, LANES), 0).astype(F32)
        for direction, forward in enumerate((True, False)):
            dist = ii - jj if forward else jj - ii
            distf = jnp.maximum(dist, 0).astype(F32)
            for pr in range(n_pairs):
                r0 = direction * RET_HEADS + 2 * pr
                lg = [lg_ref[r0 + e:r0 + e + 1, :] for e in range(2)]
                lg_lane = jnp.where(lo, lg[0], lg[1])
                qdec_ref[direction * n_pairs + pr] = jnp.exp(lg_lane * ((row + 1.0) if forward else (L - row)))
                kdec_ref[direction * n_pairs + pr] = jnp.exp(lg_lane * ((L - 1.0 - row) if forward else row))
                cdec_ref[direction * n_pairs + pr] = jnp.where(srow_lo, jnp.exp(lg[0] * float(L)),
                                                               jnp.exp(lg[1] * float(L)))
                for e in range(2):
                    inner_ref[r0 + e] = jnp.where(dist >= 0, jnp.exp(lg[e][:, 0:1] * distf), 0.0)

    def scan_chunk(x_ref, o_ref, s_ref, direction):
        for pr in range(n_pairs):
            q = x_ref[:, pr * LANES:(pr + 1) * LANES]
            k = x_ref[:, nq + pr * LANES:nq + (pr + 1) * LANES]
            qd = q * qdec_ref[direction * n_pairs + pr]
            kdb = (k * kdec_ref[direction * n_pairs + pr]).astype(BF16)
            kb = k.astype(BF16)
            state = s_ref[pr]
            state_b = state.astype(BF16)
            upd = []
            for e in range(2):
                hd = 2 * pr + e
                keep = lo if e == 0 else jnp.logical_not(lo)
                v = x_ref[:, 2 * nq + hd * RET_V:2 * nq + (hd + 1) * RET_V].astype(BF16)
                attn = _dot_nt(jnp.where(keep, q, 0.0).astype(BF16), kb) * inner_ref[direction * RET_HEADS + hd]
                o = _dot(attn.astype(BF16), v) + _dot(jnp.where(keep, qd, 0.0).astype(BF16), state_b)
                o_ref[:, hd * RET_V:(hd + 1) * RET_V] = o
                upd.append(_dot_tn(kdb, v))
            s_ref[pr] = state * cdec_ref[direction * n_pairs + pr] + jnp.where(srow_lo, upd[0], upd[1])

    scan_chunk(f_ref, of_ref, sf_ref, 0)
    scan_chunk(b_ref, ob_ref, sb_ref, 1)


def _retention(ret, lg_tab, *, n_batch, seq, ctx):
    t = ret.shape[0]
    L = RET_CHUNK
    n_lat, n_ctx = seq // L, ctx // L
    ctx0 = n_batch * n_lat
    width = 2 * RET_HEADS * RET_QK + RET_HEADS * RET_V
    fwd = lambda b, s: (jnp.where(s < n_ctx, ctx0 + b * n_ctx + s, b * n_lat + s - n_ctx), 0)
    bwd = lambda b, s: (jnp.where(s < n_ctx, ctx0 + b * n_ctx + n_ctx - 1 - s, b * n_lat + n_lat - 1 - (s - n_ctx)), 0)
    out = jax.ShapeDtypeStruct((t, RET_HEADS * RET_V), F32)
    return pl.pallas_call(
        _ret_kernel,
        grid=(n_batch, n_lat + n_ctx),
        in_specs=[pl.BlockSpec((L, width), fwd), pl.BlockSpec((L, width), bwd),
                  pl.BlockSpec(lg_tab.shape, lambda b, s: (0, 0))],
        out_specs=[pl.BlockSpec((L, RET_HEADS * RET_V), fwd), pl.BlockSpec((L, RET_HEADS * RET_V), bwd)],
        out_shape=[out, out],
        scratch_shapes=[pltpu.VMEM((RET_HEADS // 2, LANES, RET_V), F32),
                        pltpu.VMEM((RET_HEADS // 2, LANES, RET_V), F32),
                        pltpu.VMEM((RET_HEADS, L, LANES), F32), pltpu.VMEM((RET_HEADS, L, LANES), F32),
                        pltpu.VMEM((RET_HEADS, LANES, RET_V), F32), pltpu.VMEM((2 * RET_HEADS, L, L), F32)],
        compiler_params=_cparams(("arbitrary", "arbitrary")),
        name="retention",
    )(ret, ret, lg_tab)


def _merge_kernel(a_ref, w_ref, of_ref, ob_ref, rg_ref, gt_ref, h_ref, mod_ref, gpost_ref, gffn_ref,
                  wa_ref, ww_ref, wr_ref, wo_ref, h1_ref, v_ref, vp_ref, *, tiles_per_batch, n_batch, d):
    i = pl.program_id(0)
    bi = jnp.minimum(i // tiles_per_batch, n_batch)
    o = of_ref[...] + ob_ref[...]
    normed = []
    for hd in range(RET_HEADS):
        oh = o[:, hd * RET_V:(hd + 1) * RET_V]
        dev = oh - jnp.mean(oh, axis=-1, keepdims=True)
        normed.append(dev * lax.rsqrt(jnp.mean(dev * dev, axis=-1, keepdims=True) + NORM_EPS))
    g = rg_ref[...]
    r = (g * _sigmoid(g)) * jnp.concatenate(normed, axis=1)
    y = (_sigmoid(gt_ref[:, 0:d].astype(F32)) * _dot(a_ref[...], wa_ref[...])
         + _sigmoid(gt_ref[:, d:2 * d].astype(F32)) * _dot(w_ref[...], ww_ref[...])
         + _sigmoid(gt_ref[:, 2 * d:3 * d].astype(F32)) * _dot(r.astype(BF16), wr_ref[...]))
    z = _dot(y.astype(BF16), wo_ref[...])
    g1 = mod_ref[pl.ds(bi, 1), 2 * d:3 * d]
    sh2 = mod_ref[pl.ds(bi, 1), 3 * d:4 * d]
    sc2 = mod_ref[pl.ds(bi, 1), 4 * d:5 * d]
    h1 = h_ref[...] + g1 * _rms(z, gpost_ref[...])
    h1_ref[...] = h1
    v = _rms(h1, gffn_ref[...]) * (1.0 + sc2) + sh2
    v_ref[...] = v.astype(v_ref.dtype)
    vp_ref[...] = _pack_halves(v)


def _merge(a, w, o_f, o_b, ret, gates, h, mods, gpost, gffn, wa, ww, wr, wo, *, n_rows, n_batch, seq):
    d = h.shape[1]
    rows = lambda i: (i, 0)
    const = lambda i: (0, 0)
    rv = RET_HEADS * RET_V
    rg_col = (2 * RET_HEADS * RET_QK + rv) // rv
    outs = [jax.ShapeDtypeStruct((n_rows, d), F32), jax.ShapeDtypeStruct((n_rows, d), BF16),
            jax.ShapeDtypeStruct((n_rows, d // 2), jnp.uint32)]
    return pl.pallas_call(
        functools.partial(_merge_kernel, tiles_per_batch=seq // TM, n_batch=n_batch, d=d),
        grid=(n_rows // TM,),
        in_specs=[pl.BlockSpec((TM, a.shape[1]), rows), pl.BlockSpec((TM, w.shape[1]), rows),
                  pl.BlockSpec((TM, rv), rows), pl.BlockSpec((TM, rv), rows),
                  pl.BlockSpec((TM, rv), lambda i: (i, rg_col)),
                  pl.BlockSpec((TM, 3 * d), rows), pl.BlockSpec((TM, d), rows),
                  pl.BlockSpec(mods.shape, const), pl.BlockSpec((1, d), const), pl.BlockSpec((1, d), const),
                  pl.BlockSpec(wa.shape, const), pl.BlockSpec(ww.shape, const),
                  pl.BlockSpec(wr.shape, const), pl.BlockSpec(wo.shape, const)],
        out_specs=[pl.BlockSpec((TM, o.shape[1]), rows) for o in outs],
        out_shape=outs,
        compiler_params=_cparams(("arbitrary",)),
        name="merge",
    )(a, w, o_f, o_b, ret, gates, h, mods, gpost, gffn, wa, ww, wr, wo)


def _router_kernel(v_ref, rw_ref, rb_ref, eidx_ref, rank_ref, w_ref, cnt_ref, carry_ref):
    @pl.when(pl.program_id(0) == 0)
    def _():
        carry_ref[...] = jnp.zeros_like(carry_ref)

    tm = v_ref.shape[0]
    scores = _sigmoid(_dot_nt(rw_ref[...], v_ref[...]))
    sel = scores + rb_ref[...]
    neg = -jnp.inf
    n_grp, per = N_EXPERT_GROUPS, EXPERTS_PER_GROUP

    sel3 = sel.reshape(n_grp, per, tm)
    member_id = lax.broadcasted_iota(jnp.int32, sel3.shape, 1)
    m1 = jnp.max(sel3, axis=1, keepdims=True)
    i1 = jnp.min(jnp.where(sel3 == m1, member_id, per), axis=1, keepdims=True)
    m2 = jnp.max(jnp.where(member_id == i1, neg, sel3), axis=1, keepdims=True)
    gscore = (m1 + m2).reshape(n_grp, tm)
    gid = lax.broadcasted_iota(jnp.int32, gscore.shape, 0)
    ahead = jnp.zeros(gscore.shape, jnp.int32)
    for gj in range(n_grp):
        other = gscore[gj:gj + 1, :]
        ahead = ahead + jnp.where((other > gscore) | ((other == gscore) & (gid > gj)), 1, 0)
    group_ok = (ahead < TOPK_GROUPS).reshape(n_grp, 1, tm)
    sel = jnp.where(group_ok, sel3, NEG_INF).reshape(N_EXPERTS, tm)

    eid = lax.broadcasted_iota(jnp.int32, sel.shape, 0)
    chosen = jnp.zeros(sel.shape, jnp.bool_)
    picks = []
    for _ in range(TOP_K):
        m = jnp.max(sel, axis=0, keepdims=True)
        idx = jnp.min(jnp.where(sel == m, eid, N_EXPERTS), axis=0, keepdims=True)
        hit = eid == idx
        chosen = chosen | hit
        sel = jnp.where(hit, neg, sel)
        picks.append(idx)
    w = jnp.where(chosen, scores, 0.0)
    gate = ROUTED_SCALE * w / jnp.sum(w, axis=0, keepdims=True)

    member = jnp.where(chosen, 1.0, 0.0)
    earlier = lax.broadcasted_iota(jnp.int32, (tm, tm), 0) < lax.broadcasted_iota(jnp.int32, (tm, tm), 1)
    pos = _dot(member.astype(BF16), jnp.where(earlier, 1.0, 0.0).astype(BF16)) + carry_ref[...]
    for k, idx in enumerate(picks):
        hit = eid == idx
        eidx_ref[k:k + 1, :] = idx
        rank_ref[k:k + 1, :] = jnp.sum(jnp.where(hit, pos, 0.0), axis=0, keepdims=True)
        w_ref[k:k + 1, :] = jnp.sum(jnp.where(hit, gate, 0.0), axis=0, keepdims=True)
    carry_ref[...] += jnp.sum(member, axis=1, keepdims=True)
    cnt_ref[...] = carry_ref[...]


def _router(v, rw_t, rb, tile0, n_tiles):
    d = v.shape[1]
    n_rows = n_tiles * TM
    cols = lambda i: (0, i)
    const = lambda i: (0, 0)
    outs = [jax.ShapeDtypeStruct((TOP_K, n_rows), jnp.int32), jax.ShapeDtypeStruct((TOP_K, n_rows), F32),
            jax.ShapeDtypeStruct((TOP_K, n_rows), F32), jax.ShapeDtypeStruct((N_EXPERTS, 1), F32)]
    return pl.pallas_call(
        _router_kernel,
        grid=(n_tiles,),
        in_specs=[pl.BlockSpec((TM, d), lambda i: (tile0 + i, 0)), pl.BlockSpec(rw_t.shape, const),
                  pl.BlockSpec(rb.shape, const)],
        out_specs=[pl.BlockSpec((TOP_K, TM), cols), pl.BlockSpec((TOP_K, TM), cols),
                   pl.BlockSpec((TOP_K, TM), cols), pl.BlockSpec((N_EXPERTS, 1), const)],
        out_shape=outs,
        scratch_shapes=[pltpu.VMEM((N_EXPERTS, 1), F32)],
        compiler_params=_cparams(("arbitrary",)),
        name="router",
    )(v, rw_t, rb)


def _slots_kernel(eidx_ref, rank_ref, cnt_ref, slot_ref):
    tm = eidx_ref.shape[1]
    eid = lax.broadcasted_iota(jnp.int32, (N_EXPERTS, tm), 0)
    for k in range(TOP_K):
        before = jnp.sum(jnp.where(eid < eidx_ref[k:k + 1, :], cnt_ref[...], 0.0), axis=0, keepdims=True)
        slot_ref[k:k + 1, :] = (before + rank_ref[k:k + 1, :]).astype(jnp.int32)


def _slots(eidx, rank, cnt):
    n_rows = eidx.shape[1]
    tm = next(c for c in (2048, 1024, 512, 256) if n_rows % c == 0)
    cols = lambda i: (0, i)
    return pl.pallas_call(
        _slots_kernel,
        grid=(n_rows // tm,),
        in_specs=[pl.BlockSpec((TOP_K, tm), cols), pl.BlockSpec((TOP_K, tm), cols),
                  pl.BlockSpec(cnt.shape, lambda i: (0, 0))],
        out_specs=pl.BlockSpec((TOP_K, tm), cols),
        out_shape=jax.ShapeDtypeStruct((TOP_K, n_rows), jnp.int32),
        compiler_params=_cparams(("arbitrary",)),
        name="slots",
    )(eidx, rank, cnt)


SC_WINDOW = 128


def _sc_mesh():
    return plsc.VectorSubcoreMesh(core_axis_name="core", subcore_axis_name="subcore")


def _sc_dispatch(rows, slot_t, n_out, row0):
    width = rows.shape[1]
    n_chunks = slot_t.shape[1] // SC_WINDOW
    info = plsc.get_sparse_core_info()
    n_workers = info.num_cores * info.num_subcores

    @functools.partial(
        pl.kernel, mesh=_sc_mesh(),
        out_type=jax.ShapeDtypeStruct((n_out, width), rows.dtype),
        scratch_types=[pltpu.VMEM((TOP_K, SC_WINDOW), jnp.int32), pltpu.VMEM((SC_WINDOW, width), rows.dtype)],
        name="moe_dispatch")
    def run(rows_hbm, idx_hbm, out_hbm, idx_v, rows_v):
        wid = lax.axis_index("subcore") * info.num_cores + lax.axis_index("core")

        @pl.loop(wid, n_chunks, step=n_workers)
        def _(c):
            r0 = pl.multiple_of(c * SC_WINDOW, SC_WINDOW)
            pltpu.sync_copy(idx_hbm.at[:, pl.ds(r0, SC_WINDOW)], idx_v)
            pltpu.sync_copy(rows_hbm.at[pl.ds(row0 + r0, SC_WINDOW)], rows_v)
            for k in range(TOP_K):
                pltpu.sync_copy(rows_v, out_hbm.at[idx_v.at[k]])

    return run(rows, slot_t)


def _sc_collect(rows, slot_t):
    n_picks, n_rows = slot_t.shape
    width = rows.shape[1]
    n_chunks = n_rows // SC_WINDOW
    info = plsc.get_sparse_core_info()
    n_workers = info.num_cores * info.num_subcores

    @functools.partial(
        pl.kernel, mesh=_sc_mesh(),
        out_type=jax.ShapeDtypeStruct((n_picks, n_rows, width), rows.dtype),
        scratch_types=[pltpu.VMEM((TOP_K, SC_WINDOW), jnp.int32), pltpu.VMEM((SC_WINDOW, width), rows.dtype)],
        name="moe_collect")
    def run(rows_hbm, idx_hbm, out_hbm, idx_v, rows_v):
        wid = lax.axis_index("subcore") * info.num_cores + lax.axis_index("core")

        @pl.loop(wid, n_chunks, step=n_workers)
        def _(c):
            r0 = pl.multiple_of(c * SC_WINDOW, SC_WINDOW)
            pltpu.sync_copy(idx_hbm.at[:, pl.ds(r0, SC_WINDOW)], idx_v)
            for k in range(TOP_K):
                pltpu.sync_copy(rows_hbm.at[idx_v.at[k]], rows_v)
                pltpu.sync_copy(rows_v, out_hbm.at[k, pl.ds(r0, SC_WINDOW)])

    return run(rows, slot_t)


EXPERT_TILE = 512


def _work_items(cnt, n_slots):
    counts = cnt[:, 0].astype(jnp.int32)
    ends = jnp.cumsum(counts)
    n_tiles = n_slots // EXPERT_TILE
    bounds = jnp.sort(jnp.concatenate([jnp.arange(n_tiles, dtype=jnp.int32) * EXPERT_TILE, ends - counts]))
    nxt = jnp.concatenate([bounds[1:], jnp.array([n_slots], jnp.int32)])
    tile = jnp.minimum(bounds // EXPERT_TILE, n_tiles - 1)
    expert = jnp.sum((ends[None, :] <= bounds[:, None]).astype(jnp.int32), axis=1)
    expert = jnp.minimum(expert, N_EXPERTS - 1)
    return tile, expert, bounds - tile * EXPERT_TILE, nxt - tile * EXPERT_TILE


XS_RING = 3


def _experts_kernel(tile_ref, exp_ref, lo_ref, hi_ref, xs_hbm, wg_ref, wu_ref, wd_ref, ys_ref,
                    acc_ref, wgb_ref, wub_ref, wdb_ref, xbuf_ref, xsem):
    i = pl.program_id(0)
    n_items = pl.num_programs(0)
    lo, hi = lo_ref[i], hi_ref[i]

    def tile_copy(item):
        slot = lax.rem(item, XS_RING)
        row0 = pl.multiple_of(tile_ref[item] * EXPERT_TILE, EXPERT_TILE)
        return pltpu.make_async_copy(xs_hbm.at[pl.ds(row0, EXPERT_TILE)], xbuf_ref.at[slot], xsem.at[slot])

    @pl.when(i == 0)
    def _():
        for ahead in range(XS_RING - 1):
            pl.when(ahead < n_items)(lambda: tile_copy(ahead).start())

    @pl.when(i + XS_RING - 1 < n_items)
    def _():
        tile_copy(i + XS_RING - 1).start()

    tile_copy(i).wait()
    xs_ref = xbuf_ref.at[lax.rem(i, XS_RING)]

    @pl.when((i == 0) | (exp_ref[i] != exp_ref[jnp.maximum(i - 1, 0)]))
    def _():
        wgb_ref[...] = wg_ref[...].astype(BF16)
        wub_ref[...] = wu_ref[...].astype(BF16)
        wdb_ref[...] = wd_ref[...].astype(BF16)

    def ffn():
        x_lo, x_hi = _unpack_halves(xs_ref[...])
        x_lo, x_hi = x_lo.astype(BF16), x_hi.astype(BF16)
        n = x_lo.shape[1]
        a = _dot(x_lo, wgb_ref[0:n, :]) + _dot(x_hi, wgb_ref[n:, :])
        u = _dot(x_lo, wub_ref[0:n, :]) + _dot(x_hi, wub_ref[n:, :])
        return _dot(((a * _sigmoid(a)) * u).astype(BF16), wdb_ref[...])

    whole = (lo == 0) & (hi == EXPERT_TILE)

    @pl.when(whole)
    def _():
        ys_ref[...] = _pack_halves(ffn())

    @pl.when(jnp.logical_not(whole) & (hi > lo))
    def _():
        y = ffn()
        row = lax.broadcasted_iota(jnp.int32, y.shape, 0)
        y = jnp.where((row >= lo) & (row < hi), y, 0.0)

        @pl.when(lo == 0)
        def _():
            acc_ref[...] = y

        @pl.when((lo > 0) & (hi < EXPERT_TILE))
        def _():
            acc_ref[...] += y

        @pl.when((lo > 0) & (hi == EXPERT_TILE))
        def _():
            ys_ref[...] = _pack_halves(acc_ref[...] + y)


def _experts(xs, items, layer, exp_wg, exp_wu, exp_wd):
    n_slots, half = xs.shape
    d, hid = exp_wg.shape[-2:]
    tile, expert, lo, hi = items
    grid_spec = pltpu.PrefetchScalarGridSpec(
        num_scalar_prefetch=4,
        grid=(tile.shape[0],),
        in_specs=[pl.BlockSpec(memory_space=pl.ANY),
                  pl.BlockSpec((None, None, d, hid), lambda i, t, e, lo, hi: (layer, e[i], 0, 0)),
                  pl.BlockSpec((None, None, d, hid), lambda i, t, e, lo, hi: (layer, e[i], 0, 0)),
                  pl.BlockSpec((None, None, hid, d), lambda i, t, e, lo, hi: (layer, e[i], 0, 0))],
        out_specs=pl.BlockSpec((EXPERT_TILE, half), lambda i, t, e, lo, hi: (t[i], 0)),
        scratch_shapes=[pltpu.VMEM((EXPERT_TILE, d), F32), pltpu.VMEM((d, hid), BF16),
                        pltpu.VMEM((d, hid), BF16), pltpu.VMEM((hid, d), BF16),
                        pltpu.VMEM((XS_RING, EXPERT_TILE, half), jnp.uint32),
                        pltpu.SemaphoreType.DMA((XS_RING,))])
    return pl.pallas_call(
        _experts_kernel,
        grid_spec=grid_spec,
        out_shape=jax.ShapeDtypeStruct((n_slots, half), jnp.uint32),
        compiler_params=_cparams(("arbitrary",)),
        name="experts",
    )(tile, expert, lo, hi, xs, exp_wg, exp_wu, exp_wd)


def _moe_out_kernel(yg_ref, w_ref, v_ref, sg_ref, su_ref, sd_ref, h1_ref, mod_ref, gpost_ref, *rest,
                    tile0, tiles_per_batch, n_batch, d):
    o_ref = rest[-1]
    i = tile0 + pl.program_id(0)
    x = v_ref[...]
    a = _dot(x, sg_ref[...])
    f = _dot(((a * _sigmoid(a)) * _dot(x, su_ref[...])).astype(BF16), sd_ref[...])
    n = d // 2
    f_lo, f_hi = f[:, :n], f[:, n:]
    w = w_ref[...]
    for k in range(TOP_K):
        y_lo, y_hi = _unpack_halves(yg_ref[k])
        wk = w[:, k:k + 1]
        f_lo = f_lo + wk * y_lo
        f_hi = f_hi + wk * y_hi
    f = jnp.concatenate([f_lo, f_hi], axis=1)
    bi = jnp.minimum(i // tiles_per_batch, n_batch)
    g2 = mod_ref[pl.ds(bi, 1), 5 * d:6 * d]
    o_ref[...] = h1_ref[...] + g2 * _rms(f, gpost_ref[...])


def _moe_out(yg, w, v, sg, su, sd, h1, mods, gpost, tile0, prev_out, *, n_batch, seq):
    n_rows, d = v.shape
    n_tiles = w.shape[0] // TM
    part = lambda i: (i, 0)
    rows = lambda i: (tile0 + i, 0)
    const = lambda i: (0, 0)
    in_specs = [pl.BlockSpec((TOP_K, TM, d // 2), lambda i: (0, i, 0)), pl.BlockSpec((TM, TOP_K), part),
                pl.BlockSpec((TM, d), rows),
                pl.BlockSpec(sg.shape, const), pl.BlockSpec(su.shape, const), pl.BlockSpec(sd.shape, const),
                pl.BlockSpec((TM, d), rows), pl.BlockSpec(mods.shape, const), pl.BlockSpec((1, d), const)]
    args = [yg, w, v, sg, su, sd, h1, mods, gpost]
    aliases = {}
    if prev_out is not None:
        in_specs.append(pl.BlockSpec(memory_space=pl.ANY))
        aliases = {len(args): 0}
        args.append(prev_out)
    return pl.pallas_call(
        functools.partial(_moe_out_kernel, tile0=tile0, tiles_per_batch=seq // TM, n_batch=n_batch, d=d),
        grid=(n_tiles,),
        in_specs=in_specs,
        out_specs=pl.BlockSpec((TM, d), rows),
        out_shape=jax.ShapeDtypeStruct((n_rows, d), F32),
        input_output_aliases=aliases,
        compiler_params=_cparams(("arbitrary",)),
        name="moe_out",
    )(*args)


MOE_PARTS = 1


def _moe(v, vp, layer, rw, rb, exp_wg, exp_wu, exp_wd, sg, su, sd, h1, mods, gpost, *, n_batch, seq):
    n_tiles = v.shape[0] // TM
    per_part = n_tiles // MOE_PARTS
    assert per_part * MOE_PARTS == n_tiles and (per_part * TM * TOP_K) % EXPERT_TILE == 0
    n_slots = per_part * TM * TOP_K
    routed = []
    for p in range(MOE_PARTS):
        eidx, rank, w_t, cnt = _router(v, rw, rb, p * per_part, per_part)
        routed.append((_slots(eidx, rank, cnt), w_t, cnt))
    xs = [_sc_dispatch(vp, slot_t, n_slots, p * per_part * TM) for p, (slot_t, _, _) in enumerate(routed)]
    ys = [_experts(x, _work_items(cnt, n_slots), layer, exp_wg, exp_wu, exp_wd)
          for x, (_, _, cnt) in zip(xs, routed)]
    out = None
    for p, (y, (slot_t, w_t, _)) in enumerate(zip(ys, routed)):
        yg = _sc_collect(y, slot_t)
        out = _moe_out(yg, w_t.T, v, sg, su, sd, h1, mods, gpost, p * per_part, out, n_batch=n_batch, seq=seq)
    return out


def _rope_tables(seq):
    rows = seq // GRID_W
    row_id = jnp.repeat(jnp.arange(rows, dtype=F32), GRID_W)
    col_id = jnp.tile(jnp.arange(GRID_W, dtype=F32), rows)

    def tables(rot_dim):
        axis_dim = rot_dim // 2
        inv_freq = ROPE_BASE ** (-jnp.arange(0, axis_dim, 2, dtype=F32) / axis_dim)
        ang_r = row_id[:, None] * inv_freq[None, :]
        ang_c = col_id[:, None] * inv_freq[None, :]
        cos = jnp.concatenate([jnp.cos(ang_r), jnp.cos(ang_r), jnp.cos(ang_c), jnp.cos(ang_c)], axis=1)
        sin = jnp.concatenate([-jnp.sin(ang_r), jnp.sin(ang_r), -jnp.sin(ang_c), jnp.sin(ang_c)], axis=1)
        return cos, sin

    cos64, sin64 = tables(GQA_DIM)
    cos32, sin32 = tables(MLA_ROPE)
    ones = jnp.ones((seq, MLA_NOPE), F32)
    pad = LANES - MLA_NOPE - MLA_ROPE
    cospe = jnp.concatenate([ones, cos32, jnp.ones((seq, pad), F32)], axis=1)
    sinpe = jnp.concatenate([0 * ones, sin32, jnp.zeros((seq, pad), F32)], axis=1)
    tab = jnp.concatenate([cos64, cos64, sin64, sin64, cospe, sinpe], axis=1)
    ident = jnp.concatenate([jnp.ones((TM, LANES), F32), jnp.zeros((TM, LANES), F32),
                             jnp.ones((TM, LANES), F32), jnp.zeros((TM, LANES), F32)], axis=1)
    return jnp.concatenate([tab, ident], axis=0)


def _pack_w_in(w):
    d = w.shape[0]
    sizes = (MLA_Q_LORA, MLA_KV_LORA, MLA_ROPE, GQA_HEADS * GQA_DIM, GQA_KV_HEADS * GQA_DIM,
             GQA_KV_HEADS * GQA_DIM, RET_HEADS * RET_QK, RET_HEADS * RET_QK, RET_HEADS * RET_V,
             RET_HEADS * RET_V, 3 * d)
    offs, parts = 0, []
    for s in sizes:
        parts.append(w[:, offs:offs + s])
        offs += s
    cq, ckv, kpe, gq, gk, gv, rq, rk, rv, rg, gates = parts

    def twice(m):
        heads = [m[:, i * GQA_DIM:(i + 1) * GQA_DIM] for i in range(GQA_KV_HEADS)]
        return jnp.concatenate([hh for hd in heads for hh in (hd, hd)], axis=1)

    kpe_slab = jnp.concatenate([jnp.zeros((d, MLA_NOPE), F32), kpe,
                                jnp.zeros((d, LANES - MLA_NOPE - MLA_ROPE), F32)], axis=1)
    packed = jnp.concatenate([cq, ckv, kpe_slab, gq * (GQA_DIM ** -0.5 * LOG2_E), twice(gk), twice(gv),
                              rq, rk * RET_QK ** -0.5, rv, rg, gates], axis=1)
    assert packed.shape[1] == W_COLS
    return packed.astype(BF16)


def _pack_mla_up(w_uq, w_ukv):
    r = w_uq.shape[0]
    dq = MLA_NOPE + MLA_ROPE
    wq = jnp.pad(w_uq.reshape(r, MLA_HEADS, dq), ((0, 0), (0, 0), (0, LANES - dq))).reshape(r, MLA_HEADS * LANES)
    kv = w_ukv.reshape(r, MLA_HEADS, MLA_NOPE + MLA_V)
    wk = jnp.pad(kv[:, :, :MLA_NOPE], ((0, 0), (0, 0), (0, LANES - MLA_NOPE))).reshape(r, MLA_HEADS * LANES)
    wv = kv[:, :, MLA_NOPE:]
    zeros = jnp.zeros_like(wv)
    even = jnp.concatenate([wv, zeros], axis=2)
    odd = jnp.concatenate([zeros, wv], axis=2)
    wv = jnp.where((jnp.arange(MLA_HEADS) % 2 == 0)[None, :, None], even, odd).reshape(r, MLA_HEADS * LANES)
    return wq.astype(BF16), wk.astype(BF16), wv.astype(BF16)


def kernel(x, c, ctx, c_ctx, ada_w, ada_b, norm_mix_pre, norm_mix_post, norm_ffn_pre, norm_ffn_post, w_in, mla_q_norm, mla_w_uq, mla_kv_norm, mla_w_ukv, gqa_sink, ret_decay_fwd, ret_decay_bwd, w_br_mla, w_br_gqa, w_br_ret, w_out, router_w, router_bias, exp_w_gate, exp_w_up, exp_w_down, shared_w_gate, shared_w_up, shared_w_down):
    n_batch, seq, d = x.shape
    n_ctx = ctx.shape[1]
    depth = ada_w.shape[0]
    n_lat_rows = n_batch * seq
    assert seq % TM == 0 and (n_batch * n_ctx) % TM == 0 and n_ctx == ATT_TQ
    assert n_batch < MOD_ROWS and seq % GRID_W == 0

    cond = jnp.zeros((MOD_ROWS, d), F32).at[:n_batch].set(c).at[n_batch].set(c_ctx)
    mods_all = _adaln(cond, ada_w, ada_b)
    rope = _rope_tables(seq)
    h = jnp.concatenate([x.reshape(n_lat_rows, d), ctx.reshape(n_batch * n_ctx, d)], axis=0)
    row = lambda p: p.reshape(1, -1)
    dims = dict(n_batch=n_batch, seq=seq)

    for l in range(depth):
        last = l == depth - 1
        mods = mods_all[l]
        wq, wk, wv = _pack_mla_up(mla_w_uq[l], mla_w_ukv[l])
        mq, mk, mv, gqa, ret, gates = _inproj(h, mods, row(norm_mix_pre[l]), _pack_w_in(w_in[l]), rope,
                                              row(mla_q_norm[l]), row(mla_kv_norm[l]), wq, wk, wv, **dims)
        a = _mla_attention(mq, mk, mv, ctx=n_ctx, with_ctx_queries=not last, **dims)
        sink_tab = jnp.broadcast_to(gqa_sink[l].astype(F32)[:, None] * LOG2_E, (GQA_HEADS, LANES))
        w = _window_attention(gqa, sink_tab, ctx=n_ctx, with_ctx_queries=not last, **dims)
        lg = jnp.concatenate([jax.nn.log_sigmoid(ret_decay_fwd[l].astype(F32)),
                              jax.nn.log_sigmoid(ret_decay_bwd[l].astype(F32))])
        o_f, o_b = _retention(ret, jnp.broadcast_to(lg[:, None], (2 * RET_HEADS, LANES)), ctx=n_ctx, **dims)
        n_rows = n_lat_rows if last else h.shape[0]
        h1, v, vp = _merge(a, w, o_f, o_b, ret, gates, h, mods, row(norm_mix_post[l]), row(norm_ffn_pre[l]),
                           w_br_mla[l].astype(BF16), w_br_gqa[l].astype(BF16), w_br_ret[l].astype(BF16),
                           w_out[l].astype(BF16), n_rows=n_rows, **dims)
        h = _moe(v, vp, l, router_w[l].T.astype(BF16), router_bias[l].astype(F32).reshape(-1, 1),
                 exp_w_gate, exp_w_up, exp_w_down, shared_w_gate[l].astype(BF16),
                 shared_w_up[l].astype(BF16), shared_w_down[l].astype(BF16), h1, mods,
                 row(norm_ffn_post[l]), **dims)
    return h[:n_lat_rows].reshape(n_batch, seq, d)
```

```python
import functools

import jax
import jax.numpy as jnp
from jax import lax
from jax.experimental import pallas as pl
from jax.experimental.pallas import tpu as pltpu
from jax.experimental.pallas import tpu_sc as plsc

F32 = jnp.float32
BF16 = jnp.bfloat16

GRID_W = 64
ROPE_BASE = 10000.0
NORM_EPS = 1e-6
NEG_INF = -1e30
LOG2_E = 1.4426950408889634
N_MOD = 6
MLA_HEADS, MLA_NOPE, MLA_ROPE, MLA_V = 8, 64, 32, 64
MLA_Q_LORA, MLA_KV_LORA = 256, 256
GQA_HEADS, GQA_KV_HEADS, GQA_DIM, WINDOW = 8, 2, 64, 128
RET_HEADS, RET_QK, RET_V, RET_CHUNK = 4, 64, 128, 128
N_EXPERTS, N_EXPERT_GROUPS, TOPK_GROUPS, TOP_K = 64, 8, 4, 8
EXPERTS_PER_GROUP = N_EXPERTS // N_EXPERT_GROUPS
ROUTED_SCALE = 2.5

LANES = 128
TM = 512
ATT_TQ = 512
WIN_TQ = 256
MOD_ROWS = 8
V7X_VMEM_LIMIT = 56 * 1024 * 1024

C_CQ, C_CKV, C_KPE = 0, 256, 512
C_G = 640
C_R = 1664
C_GATE = 3200
W_COLS = 6272


def _cparams(sem):
    return pltpu.CompilerParams(dimension_semantics=sem, vmem_limit_bytes=V7X_VMEM_LIMIT)


def _rms(x, g):
    return x * lax.rsqrt(jnp.mean(x * x, axis=-1, keepdims=True) + NORM_EPS) * g


def _sigmoid(x):
    return 0.5 * jnp.tanh(0.5 * x) + 0.5


def _dot(a, b):
    return jnp.dot(a, b, preferred_element_type=F32)


def _dot_nt(a, b):
    return lax.dot_general(a, b, (((1,), (1,)), ((), ())), preferred_element_type=F32)


def _dot_tn(a, b):
    return lax.dot_general(a, b, (((0,), (0,)), ((), ())), preferred_element_type=F32)


def _rope(x, cos, sin, half):
    n = x.shape[-1]
    reps = n // LANES
    if reps > 1:
        cos = jnp.concatenate([cos] * reps, axis=1)
        sin = jnp.concatenate([sin] * reps, axis=1)
    lane = lax.broadcasted_iota(jnp.int32, x.shape, 1)
    up = pltpu.roll(x, half, 1)
    dn = pltpu.roll(x, n - half, 1)
    partner = jnp.where((lane & (2 * half - 1)) < half, dn, up)
    return x * cos + partner * sin


def _lane_lo(shape):
    return (lax.broadcasted_iota(jnp.int32, shape, 1) & (LANES - 1)) < (LANES // 2)


def _pack_halves(x):
    n = x.shape[1] // 2
    bits = lambda t: lax.bitcast_convert_type(t.astype(BF16).astype(F32), jnp.uint32)
    return (bits(x[:, :n]) >> 16) | bits(x[:, n:])


def _unpack_halves(p):
    lo = lax.bitcast_convert_type(p << 16, F32)
    hi = lax.bitcast_convert_type(p & jnp.uint32(0xFFFF0000), F32)
    return lo, hi


def _ada_kernel(c_ref, w_ref, b_ref, o_ref):
    c = c_ref[...]
    s = c * _sigmoid(c)
    o_ref[...] = _dot(s.astype(BF16), w_ref[...].astype(BF16)) + b_ref[...]


def _adaln(cond, ada_w, ada_b):
    n_layers, d, n = ada_w.shape
    tn = 1024
    return pl.pallas_call(
        _ada_kernel,
        grid=(n_layers, n // tn),
        in_specs=[pl.BlockSpec((MOD_ROWS, d), lambda l, j: (0, 0)),
                  pl.BlockSpec((None, d, tn), lambda l, j: (l, 0, j)),
                  pl.BlockSpec((None, 1, tn), lambda l, j: (l, 0, j))],
        out_specs=pl.BlockSpec((None, MOD_ROWS, tn), lambda l, j: (l, 0, j)),
        out_shape=jax.ShapeDtypeStruct((n_layers, MOD_ROWS, n), F32),
        compiler_params=_cparams(("arbitrary", "arbitrary")),
        name="adaln",
    )(cond, ada_w, ada_b.reshape(n_layers, 1, n))


def _inproj_kernel(h_ref, mod_ref, gpre_ref, w_ref, rope_ref, qn_ref, kvn_ref, wuq_ref, wuk_ref, wuv_ref,
                   mq_ref, mk_ref, mv_ref, gqa_ref, ret_ref, gate_ref, *, tiles_per_batch, n_batch, d):
    i = pl.program_id(0)
    bi = jnp.minimum(i // tiles_per_batch, n_batch)
    sh = mod_ref[pl.ds(bi, 1), 0:d]
    sc = mod_ref[pl.ds(bi, 1), d:2 * d]
    u = (_rms(h_ref[...], gpre_ref[...]) * (1.0 + sc) + sh).astype(BF16)

    cos64 = rope_ref[:, 0:LANES]
    sin64 = rope_ref[:, LANES:2 * LANES]
    cospe = rope_ref[:, 2 * LANES:3 * LANES]
    sinpe = rope_ref[:, 3 * LANES:4 * LANES]

    c = _dot(u, w_ref[:, C_CQ:C_G])
    kpe = _rope(c[:, C_KPE:C_G], cospe, sinpe, MLA_ROPE // 4)
    qn = _rms(c[:, C_CQ:C_CKV], qn_ref[...]).astype(BF16)
    q = _rope(_dot(qn, wuq_ref[...]), cospe, sinpe, MLA_ROPE // 4)
    mq_ref[...] = (q * ((MLA_NOPE + MLA_ROPE) ** -0.5 * LOG2_E)).astype(mq_ref.dtype)
    kvn = _rms(c[:, C_CKV:C_KPE], kvn_ref[...]).astype(BF16)
    k = _dot(kvn, wuk_ref[...]) + jnp.concatenate([kpe] * MLA_HEADS, axis=1)
    mk_ref[...] = k.astype(mk_ref.dtype)
    v = _dot(kvn, wuv_ref[...])
    lane = lax.broadcasted_iota(jnp.int32, v.shape, 1)
    value_lane = ((lane & (LANES - 1)) < MLA_V) == (((lane >> (LANES.bit_length() - 1)) & 1) == 0)
    mv_ref[...] = jnp.where(value_lane, v, 1.0).astype(mv_ref.dtype)

    g = _dot(u, w_ref[:, C_G:C_R])
    n_qk = GQA_HEADS * GQA_DIM + 2 * GQA_KV_HEADS * GQA_DIM
    gqa_ref[:, 0:n_qk] = _rope(g[:, 0:n_qk], cos64, sin64, GQA_DIM // 4).astype(gqa_ref.dtype)
    gqa_ref[:, n_qk:] = g[:, n_qk:].astype(gqa_ref.dtype)

    r = _dot(u, w_ref[:, C_R:C_GATE])
    n_qk = 2 * RET_HEADS * RET_QK
    ret_ref[:, 0:n_qk] = _rope(r[:, 0:n_qk], cos64, sin64, RET_QK // 4)
    ret_ref[:, n_qk:] = r[:, n_qk:]

    gate_ref[...] = _dot(u, w_ref[:, C_GATE:W_COLS]).astype(gate_ref.dtype)


def _inproj(h, mods, gpre, w_all, rope, qn, kvn, wuq, wuk, wuv, *, n_batch, seq):
    t, d = h.shape
    tiles_per_batch = seq // TM
    n_lat_tiles = n_batch * tiles_per_batch
    const = lambda i: (0, 0)
    rows = lambda i: (i, 0)
    rope_idx = lambda i: (jnp.where(i < n_lat_tiles, i % tiles_per_batch, tiles_per_batch), 0)
    hq = MLA_HEADS * LANES
    outs = [jax.ShapeDtypeStruct((t, hq), BF16), jax.ShapeDtypeStruct((t, hq), BF16),
            jax.ShapeDtypeStruct((t, hq), BF16),
            jax.ShapeDtypeStruct((t, C_R - C_G), BF16),
            jax.ShapeDtypeStruct((t, C_GATE - C_R), F32),
            jax.ShapeDtypeStruct((t, W_COLS - C_GATE), BF16)]
    return pl.pallas_call(
        functools.partial(_inproj_kernel, tiles_per_batch=tiles_per_batch, n_batch=n_batch, d=d),
        grid=(t // TM,),
        in_specs=[pl.BlockSpec((TM, d), rows),
                  pl.BlockSpec(mods.shape, const),
                  pl.BlockSpec((1, d), const),
                  pl.BlockSpec(w_all.shape, const, pipeline_mode=pl.Buffered(1)),
                  pl.BlockSpec((TM, 4 * LANES), rope_idx),
                  pl.BlockSpec(qn.shape, const), pl.BlockSpec(kvn.shape, const),
                  pl.BlockSpec(wuq.shape, const), pl.BlockSpec(wuk.shape, const), pl.BlockSpec(wuv.shape, const)],
        out_specs=[pl.BlockSpec((TM, o.shape[1]), rows) for o in outs],
        out_shape=outs,
        compiler_params=_cparams(("arbitrary",)),
        name="inproj",
    )(h, mods, gpre, w_all, rope, qn, kvn, wuq, wuk, wuv)


MLA_HEADS_PER_STEP = 4


def _mla_kernel(q_ref, kl_ref, kc_ref, vl_ref, vc_ref, o_ref, s_ref, p_ref, *, with_lat):
    n_ctx = kc_ref.shape[0]

    def body(with_lat):
        n_keys = n_ctx + (kl_ref.shape[0] if with_lat else 0)

        def scores(h):
            sl = slice(h * LANES, (h + 1) * LANES)
            s_ref[h % 2, :, 0:n_ctx] = _dot_nt(q_ref[:, sl], kc_ref[:, sl])
            if with_lat:
                s_ref[h % 2, :, n_ctx:n_keys] = _dot_nt(q_ref[:, sl], kl_ref[:, sl])

        def probs(h):
            s = s_ref[h % 2, :, 0:n_keys]
            p_ref[h % 2, :, 0:n_keys] = jnp.exp2(s - jnp.max(s, axis=-1, keepdims=True)).astype(BF16)

        def weighted(h):
            sl = slice(h * LANES, (h + 1) * LANES)
            o = _dot(p_ref[h % 2, :, 0:n_ctx], vc_ref[:, sl])
            if with_lat:
                o = o + _dot(p_ref[h % 2, :, n_ctx:n_keys], vl_ref[:, sl])
            return o / pltpu.roll(o, LANES // 2, 1)

        outs = [None] * MLA_HEADS_PER_STEP
        scores(0)
        for h in range(MLA_HEADS_PER_STEP):
            if h + 1 < MLA_HEADS_PER_STEP:
                scores(h + 1)
            probs(h)
            outs[h] = weighted(h)
        for pr in range(MLA_HEADS_PER_STEP // 2):
            even, odd = outs[2 * pr], outs[2 * pr + 1]
            o_ref[:, pr * LANES:(pr + 1) * LANES] = jnp.where(_lane_lo(even.shape), even, odd).astype(o_ref.dtype)

    body(with_lat)


def _mla_call(mq, mk, mv, *, n_batch, seq, ctx, latent_queries):
    hps = MLA_HEADS_PER_STEP
    ctx_blk0 = n_batch * seq // ctx
    tq = ATT_TQ if latent_queries else ctx
    nq = seq // tq if latent_queries else 1
    q_blk0 = 0 if latent_queries else n_batch * seq // tq
    q_idx = lambda b, g, i: (q_blk0 + b * nq + i, g)
    ctx_idx = lambda b, g, i: (ctx_blk0 + b, g)
    lat_idx = (lambda b, g, i: (b, g)) if latent_queries else ctx_idx
    n_lat = seq if latent_queries else ctx
    n_keys = ctx + (seq if latent_queries else 0)
    in_specs = [pl.BlockSpec((tq, hps * LANES), q_idx),
                pl.BlockSpec((n_lat, hps * LANES), lat_idx), pl.BlockSpec((ctx, hps * LANES), ctx_idx),
                pl.BlockSpec((n_lat, hps * LANES), lat_idx), pl.BlockSpec((ctx, hps * LANES), ctx_idx)]
    return pl.pallas_call(
        functools.partial(_mla_kernel, with_lat=latent_queries),
        grid=(n_batch, MLA_HEADS // hps, nq),
        in_specs=in_specs,
        out_specs=pl.BlockSpec((tq, hps * MLA_V), lambda b, g, i: (b * nq + i, g)),
        out_shape=jax.ShapeDtypeStruct((n_batch * nq * tq, MLA_HEADS * MLA_V), BF16),
        scratch_shapes=[pltpu.VMEM((2, tq, n_keys), F32), pltpu.VMEM((2, tq, n_keys), BF16)],
        compiler_params=_cparams(("arbitrary", "arbitrary", "arbitrary")),
        name="mla_attn" if latent_queries else "mla_attn_ctx",
    )(mq, mk, mk, mv, mv)


def _mla_attention(mq, mk, mv, *, n_batch, seq, ctx, with_ctx_queries):
    dims = dict(n_batch=n_batch, seq=seq, ctx=ctx)
    lat = _mla_call(mq, mk, mv, latent_queries=True, **dims)
    return lat, (_mla_call(mq, mk, mv, latent_queries=False, **dims) if with_ctx_queries else lat)


def _win_kernel(q_ref, kp_ref, kcur_ref, kn_ref, vp_ref, vcur_ref, vn_ref, kc_ref, vc_ref, sink_ref, o_ref,
                s_ref, p_ref, *, seq, with_lat):
    i = pl.program_id(1)
    tq = q_ref.shape[0]
    group = GQA_HEADS // GQA_KV_HEADS

    def body(with_lat):
        n_ctx = kc_ref.shape[0]
        n_keys = n_ctx + (tq + 2 * WINDOW if with_lat else 0)
        if with_lat:
            q_pos = i * tq + lax.broadcasted_iota(jnp.int32, (tq, n_keys), 0)
            k_pos = i * tq - WINDOW - n_ctx + lax.broadcasted_iota(jnp.int32, (tq, n_keys), 1)
            in_band = (jnp.abs(q_pos - k_pos) <= WINDOW) & (k_pos >= 0) & (k_pos < seq)
            valid = in_band | (lax.broadcasted_iota(jnp.int32, (tq, n_keys), 1) < n_ctx)
        lo = _lane_lo((tq, LANES))
        lo_k = _lane_lo((n_keys, LANES))
        keys, values = [], []
        for kv in range(GQA_KV_HEADS):
            sl = slice(kv * LANES, (kv + 1) * LANES)
            if with_lat:
                k_all = jnp.concatenate([kc_ref[:, sl], kp_ref[:, sl], kcur_ref[:, sl], kn_ref[:, sl]], axis=0)
                v_all = jnp.concatenate([vc_ref[:, sl], vp_ref[:, sl], vcur_ref[:, sl], vn_ref[:, sl]], axis=0)
            else:
                k_all, v_all = kc_ref[:, sl], vc_ref[:, sl]
            keys.append(k_all)
            one = jnp.ones_like(v_all)
            values.append((jnp.where(lo_k, v_all, one), jnp.where(lo_k, one, v_all)))

        def scores(hd):
            kv, pair = hd // group, hd // 2
            qp = q_ref[:, pair * LANES:(pair + 1) * LANES]
            qm = jnp.where(lo if hd % 2 == 0 else jnp.logical_not(lo), qp, jnp.zeros_like(qp))
            s = _dot_nt(qm, keys[kv])
            s_ref[hd % 2, :, 0:n_keys] = jnp.where(valid, s, NEG_INF) if with_lat else s

        def probs(hd):
            s = s_ref[hd % 2, :, 0:n_keys]
            m = jnp.maximum(jnp.max(s, axis=-1, keepdims=True), sink_ref[hd:hd + 1, 0:1])
            p_ref[hd % 2, :, 0:n_keys] = jnp.exp2(s - m).astype(BF16)
            return jnp.exp2(sink_ref[hd:hd + 1, 0:1] - m)

        def weighted(hd, sink_term):
            o = _dot(p_ref[hd % 2, :, 0:n_keys], values[hd // group][hd % 2])
            return o / (pltpu.roll(o, LANES // 2, 1) + sink_term)

        outs = [None] * GQA_HEADS
        scores(0)
        for hd in range(GQA_HEADS):
            if hd + 1 < GQA_HEADS:
                scores(hd + 1)
            outs[hd] = weighted(hd, probs(hd))
        for pair in range(GQA_HEADS // 2):
            o_ref[:, pair * LANES:(pair + 1) * LANES] = jnp.where(
                lo, outs[2 * pair], outs[2 * pair + 1]).astype(o_ref.dtype)

    body(with_lat)


def _win_call(gqa, sink_tab, *, n_batch, seq, ctx, latent_queries):
    tq = WIN_TQ if latent_queries else ctx
    nq = seq // tq if latent_queries else 1
    q_blk0 = 0 if latent_queries else n_batch * seq // tq
    per_tile = tq // WINDOW
    n_win_blocks = seq // WINDOW
    ctx_blk0 = n_batch * seq // ctx
    nqk = GQA_HEADS * GQA_DIM
    kw = 2 * GQA_KV_HEADS * GQA_DIM
    k_col, v_col = nqk // kw, nqk // kw + 1
    q_idx = lambda b, i: (q_blk0 + b * nq + i, 0)
    cidx = lambda col: (lambda b, i: (ctx_blk0 + b, col))
    if latent_queries:
        cur = lambda col: (lambda b, i: (b * nq + i, col))
        prev = lambda col: (lambda b, i: (b * n_win_blocks + jnp.maximum(per_tile * i - 1, 0), col))
        nxt = lambda col: (lambda b, i: (b * n_win_blocks + jnp.minimum(per_tile * (i + 1), n_win_blocks - 1), col))
        band = [((WINDOW, kw), prev), ((tq, kw), cur), ((WINDOW, kw), nxt)]
    else:
        band = [((ctx, kw), cidx)] * 3
    n_keys = ctx + (tq + 2 * WINDOW if latent_queries else 0)
    in_specs = ([pl.BlockSpec((tq, nqk), q_idx)]
                + [pl.BlockSpec(shape, idx(k_col)) for shape, idx in band]
                + [pl.BlockSpec(shape, idx(v_col)) for shape, idx in band]
                + [pl.BlockSpec((ctx, kw), cidx(k_col)), pl.BlockSpec((ctx, kw), cidx(v_col)),
                   pl.BlockSpec(sink_tab.shape, lambda b, i: (0, 0))])
    return pl.pallas_call(
        functools.partial(_win_kernel, seq=seq, with_lat=latent_queries),
        grid=(n_batch, nq),
        in_specs=in_specs,
        out_specs=pl.BlockSpec((tq, nqk), lambda b, i: (b * nq + i, 0)),
        out_shape=jax.ShapeDtypeStruct((n_batch * nq * tq, nqk), BF16),
        scratch_shapes=[pltpu.VMEM((2, tq, n_keys), F32), pltpu.VMEM((2, tq, n_keys), BF16)],
        compiler_params=_cparams(("arbitrary", "arbitrary")),
        name="win_attn" if latent_queries else "win_attn_ctx",
    )(*([gqa] * 9 + [sink_tab]))


def _window_attention(gqa, sink_tab, *, n_batch, seq, ctx, with_ctx_queries):
    dims = dict(n_batch=n_batch, seq=seq, ctx=ctx)
    lat = _win_call(gqa, sink_tab, latent_queries=True, **dims)
    return lat, (_win_call(gqa, sink_tab, latent_queries=False, **dims) if with_ctx_queries else lat)


def _ret_kernel(f_ref, b_ref, lg_ref, of_ref, ob_ref, sf_ref, sb_ref, qdec_ref, kdec_ref, cdec_ref, inner_ref):
    @pl.when(pl.program_id(1) == 0)
    def _():
        sf_ref[...] = jnp.zeros_like(sf_ref)
        sb_ref[...] = jnp.zeros_like(sb_ref)

    L = f_ref.shape[0]
    lo = _lane_lo((L, LANES))
    srow_lo = lax.broadcasted_iota(jnp.int32, (LANES, LANES), 0) < RET_QK
    nq = RET_HEADS * RET_QK
    n_pairs = RET_HEADS // 2

    @pl.when(pl.program_id(1) == 0)
    def _():
        ii = lax.broadcasted_iota(jnp.int32, (L, L), 0)
        jj = lax.broadcasted_iota(jnp.int32, (L, L), 1)
        row = lax.broadcasted_iota(jnp.int32, (L, LANES), 0).astype(F32)
        for direction, forward in enumerate((True, False)):
            dist = ii - jj if forward else jj - ii
            distf = jnp.maximum(dist, 0).astype(F32)
            for pr in range(n_pairs):
                r0 = direction * RET_HEADS + 2 * pr
                lg = [lg_ref[r0 + e:r0 + e + 1, :] for e in range(2)]
                lg_lane = jnp.where(lo, lg[0], lg[1])
                qdec_ref[direction * n_pairs + pr] = jnp.exp(lg_lane * ((row + 1.0) if forward else (L - row)))
                kdec_ref[direction * n_pairs + pr] = jnp.exp(lg_lane * ((L - 1.0 - row) if forward else row))
                cdec_ref[direction * n_pairs + pr] = jnp.where(srow_lo, jnp.exp(lg[0] * float(L)),
                                                               jnp.exp(lg[1] * float(L)))
                for e in range(2):
                    inner_ref[r0 + e] = jnp.where(dist >= 0, jnp.exp(lg[e][:, 0:1] * distf), 0.0)

    def scan_chunk(x_ref, o_ref, s_ref, direction):
        for pr in range(n_pairs):
            q = x_ref[:, pr * LANES:(pr + 1) * LANES]
            k = x_ref[:, nq + pr * LANES:nq + (pr + 1) * LANES]
            qd = q * qdec_ref[direction * n_pairs + pr]
            kdb = (k * kdec_ref[direction * n_pairs + pr]).astype(BF16)
            kb = k.astype(BF16)
            state = s_ref[pr]
            state_b = state.astype(BF16)
            upd = []
            for e in range(2):
                hd = 2 * pr + e
                keep = lo if e == 0 else jnp.logical_not(lo)
                v = x_ref[:, 2 * nq + hd * RET_V:2 * nq + (hd + 1) * RET_V].astype(BF16)
                attn = _dot_nt(jnp.where(keep, q, 0.0).astype(BF16), kb) * inner_ref[direction * RET_HEADS + hd]
                o = _dot(attn.astype(BF16), v) + _dot(jnp.where(keep, qd, 0.0).astype(BF16), state_b)
                o_ref[:, hd * RET_V:(hd + 1) * RET_V] = o
                upd.append(_dot_tn(kdb, v))
            s_ref[pr] = state * cdec_ref[direction * n_pairs + pr] + jnp.where(srow_lo, upd[0], upd[1])

    scan_chunk(f_ref, of_ref, sf_ref, 0)
    scan_chunk(b_ref, ob_ref, sb_ref, 1)


def _retention(ret, lg_tab, *, n_batch, seq, ctx):
    t = ret.shape[0]
    L = RET_CHUNK
    n_lat, n_ctx = seq // L, ctx // L
    ctx0 = n_batch * n_lat
    width = 2 * RET_HEADS * RET_QK + RET_HEADS * RET_V
    fwd = lambda b, s: (jnp.where(s < n_ctx, ctx0 + b * n_ctx + s, b * n_lat + s - n_ctx), 0)
    bwd = lambda b, s: (jnp.where(s < n_ctx, ctx0 + b * n_ctx + n_ctx - 1 - s, b * n_lat + n_lat - 1 - (s - n_ctx)), 0)
    out = jax.ShapeDtypeStruct((t, RET_HEADS * RET_V), F32)
    return pl.pallas_call(
        _ret_kernel,
        grid=(n_batch, n_lat + n_ctx),
        in_specs=[pl.BlockSpec((L, width), fwd), pl.BlockSpec((L, width), bwd),
                  pl.BlockSpec(lg_tab.shape, lambda b, s: (0, 0))],
        out_specs=[pl.BlockSpec((L, RET_HEADS * RET_V), fwd), pl.BlockSpec((L, RET_HEADS * RET_V), bwd)],
        out_shape=[out, out],
        scratch_shapes=[pltpu.VMEM((RET_HEADS // 2, LANES, RET_V), F32),
                        pltpu.VMEM((RET_HEADS // 2, LANES, RET_V), F32),
                        pltpu.VMEM((RET_HEADS, L, LANES), F32), pltpu.VMEM((RET_HEADS, L, LANES), F32),
                        pltpu.VMEM((RET_HEADS, LANES, RET_V), F32), pltpu.VMEM((2 * RET_HEADS, L, L), F32)],
        compiler_params=_cparams(("arbitrary", "arbitrary")),
        name="retention",
    )(ret, ret, lg_tab)


def _merge_kernel(al_ref, ac_ref, wl_ref, wc_ref, of_ref, ob_ref, rg_ref, gt_ref, h_ref, mod_ref, gpost_ref,
                  gffn_ref, wa_ref, ww_ref, wr_ref, wo_ref, h1_ref, v_ref, vp_ref,
                  *, tiles_per_batch, n_batch, d):
    i = pl.program_id(0)
    bi = jnp.minimum(i // tiles_per_batch, n_batch)
    latent = i < tiles_per_batch * n_batch
    a_tile = jnp.where(latent, al_ref[...], ac_ref[...])
    w_tile = jnp.where(latent, wl_ref[...], wc_ref[...])
    o = of_ref[...] + ob_ref[...]
    normed = []
    for hd in range(RET_HEADS):
        oh = o[:, hd * RET_V:(hd + 1) * RET_V]
        dev = oh - jnp.mean(oh, axis=-1, keepdims=True)
        normed.append(dev * lax.rsqrt(jnp.mean(dev * dev, axis=-1, keepdims=True) + NORM_EPS))
    g = rg_ref[...]
    r = (g * _sigmoid(g)) * jnp.concatenate(normed, axis=1)
    y = (_sigmoid(gt_ref[:, 0:d].astype(F32)) * _dot(a_tile, wa_ref[...])
         + _sigmoid(gt_ref[:, d:2 * d].astype(F32)) * _dot(w_tile, ww_ref[...])
         + _sigmoid(gt_ref[:, 2 * d:3 * d].astype(F32)) * _dot(r.astype(BF16), wr_ref[...]))
    z = _dot(y.astype(BF16), wo_ref[...])
    g1 = mod_ref[pl.ds(bi, 1), 2 * d:3 * d]
    sh2 = mod_ref[pl.ds(bi, 1), 3 * d:4 * d]
    sc2 = mod_ref[pl.ds(bi, 1), 4 * d:5 * d]
    h1 = h_ref[...] + g1 * _rms(z, gpost_ref[...])
    h1_ref[...] = h1
    v = _rms(h1, gffn_ref[...]) * (1.0 + sc2) + sh2
    v_ref[...] = v.astype(v_ref.dtype)
    vp_ref[...] = _pack_halves(v)


def _merge(a, w, o_f, o_b, ret, gates, h, mods, gpost, gffn, wa, ww, wr, wo, *, n_rows, n_batch, seq):
    d = h.shape[1]
    n_lat_tiles = n_batch * seq // TM
    rows = lambda i: (i, 0)
    lat_rows = lambda i: (jnp.minimum(i, n_lat_tiles - 1), 0)
    ctx_rows = lambda i: (jnp.maximum(i - n_lat_tiles, 0), 0)
    const = lambda i: (0, 0)
    rv = RET_HEADS * RET_V
    rg_col = (2 * RET_HEADS * RET_QK + rv) // rv
    outs = [jax.ShapeDtypeStruct((n_rows, d), F32), jax.ShapeDtypeStruct((n_rows, d), BF16),
            jax.ShapeDtypeStruct((n_rows, d // 2), jnp.uint32)]
    return pl.pallas_call(
        functools.partial(_merge_kernel, tiles_per_batch=seq // TM, n_batch=n_batch, d=d),
        grid=(n_rows // TM,),
        in_specs=[pl.BlockSpec((TM, a[0].shape[1]), lat_rows), pl.BlockSpec((TM, a[1].shape[1]), ctx_rows),
                  pl.BlockSpec((TM, w[0].shape[1]), lat_rows), pl.BlockSpec((TM, w[1].shape[1]), ctx_rows),
                  pl.BlockSpec((TM, rv), rows), pl.BlockSpec((TM, rv), rows),
                  pl.BlockSpec((TM, rv), lambda i: (i, rg_col)),
                  pl.BlockSpec((TM, 3 * d), rows), pl.BlockSpec((TM, d), rows),
                  pl.BlockSpec(mods.shape, const), pl.BlockSpec((1, d), const), pl.BlockSpec((1, d), const),
                  pl.BlockSpec(wa.shape, const), pl.BlockSpec(ww.shape, const),
                  pl.BlockSpec(wr.shape, const), pl.BlockSpec(wo.shape, const)],
        out_specs=[pl.BlockSpec((TM, o.shape[1]), rows) for o in outs],
        out_shape=outs,
        compiler_params=_cparams(("arbitrary",)),
        name="merge",
    )(a[0], a[1], w[0], w[1], o_f, o_b, ret, gates, h, mods, gpost, gffn, wa, ww, wr, wo)


def _router_kernel(v_ref, rw_ref, rb_ref, eidx_ref, rank_ref, w_ref, cnt_ref, carry_ref):
    @pl.when(pl.program_id(0) == 0)
    def _():
        carry_ref[...] = jnp.zeros_like(carry_ref)

    tm = v_ref.shape[0]
    scores = _sigmoid(_dot_nt(rw_ref[...], v_ref[...]))
    sel = scores + rb_ref[...]
    neg = -jnp.inf
    n_grp, per = N_EXPERT_GROUPS, EXPERTS_PER_GROUP

    sel3 = sel.reshape(n_grp, per, tm)
    member_id = lax.broadcasted_iota(jnp.int32, sel3.shape, 1)
    m1 = jnp.max(sel3, axis=1, keepdims=True)
    i1 = jnp.min(jnp.where(sel3 == m1, member_id, per), axis=1, keepdims=True)
    m2 = jnp.max(jnp.where(member_id == i1, neg, sel3), axis=1, keepdims=True)
    gscore = (m1 + m2).reshape(n_grp, tm)
    gid = lax.broadcasted_iota(jnp.int32, gscore.shape, 0)
    ahead = jnp.zeros(gscore.shape, jnp.int32)
    for gj in range(n_grp):
        other = gscore[gj:gj + 1, :]
        ahead = ahead + jnp.where((other > gscore) | ((other == gscore) & (gid > gj)), 1, 0)
    group_ok = (ahead < TOPK_GROUPS).reshape(n_grp, 1, tm)
    sel = jnp.where(group_ok, sel3, NEG_INF).reshape(N_EXPERTS, tm)

    eid = lax.broadcasted_iota(jnp.int32, sel.shape, 0)
    chosen = jnp.zeros(sel.shape, jnp.bool_)
    picks = []
    for _ in range(TOP_K):
        m = jnp.max(sel, axis=0, keepdims=True)
        idx = jnp.min(jnp.where(sel == m, eid, N_EXPERTS), axis=0, keepdims=True)
        hit = eid == idx
        chosen = chosen | hit
        sel = jnp.where(hit, neg, sel)
        picks.append(idx)
    w = jnp.where(chosen, scores, 0.0)
    gate = ROUTED_SCALE * w / jnp.sum(w, axis=0, keepdims=True)

    member = jnp.where(chosen, 1.0, 0.0)
    earlier = lax.broadcasted_iota(jnp.int32, (tm, tm), 0) < lax.broadcasted_iota(jnp.int32, (tm, tm), 1)
    pos = _dot(member.astype(BF16), jnp.where(earlier, 1.0, 0.0).astype(BF16)) + carry_ref[...]
    for k, idx in enumerate(picks):
        hit = eid == idx
        eidx_ref[k:k + 1, :] = idx
        rank_ref[k:k + 1, :] = jnp.sum(jnp.where(hit, pos, 0.0), axis=0, keepdims=True)
        w_ref[k:k + 1, :] = jnp.sum(jnp.where(hit, gate, 0.0), axis=0, keepdims=True)
    carry_ref[...] += jnp.sum(member, axis=1, keepdims=True)
    cnt_ref[...] = carry_ref[...]


def _router(v, rw_t, rb, tile0, n_tiles):
    d = v.shape[1]
    n_rows = n_tiles * TM
    cols = lambda i: (0, i)
    const = lambda i: (0, 0)
    outs = [jax.ShapeDtypeStruct((TOP_K, n_rows), jnp.int32), jax.ShapeDtypeStruct((TOP_K, n_rows), F32),
            jax.ShapeDtypeStruct((TOP_K, n_rows), F32), jax.ShapeDtypeStruct((N_EXPERTS, 1), F32)]
    return pl.pallas_call(
        _router_kernel,
        grid=(n_tiles,),
        in_specs=[pl.BlockSpec((TM, d), lambda i: (tile0 + i, 0)), pl.BlockSpec(rw_t.shape, const),
                  pl.BlockSpec(rb.shape, const)],
        out_specs=[pl.BlockSpec((TOP_K, TM), cols), pl.BlockSpec((TOP_K, TM), cols),
                   pl.BlockSpec((TOP_K, TM), cols), pl.BlockSpec((N_EXPERTS, 1), const)],
        out_shape=outs,
        scratch_shapes=[pltpu.VMEM((N_EXPERTS, 1), F32)],
        compiler_params=_cparams(("arbitrary",)),
        name="router",
    )(v, rw_t, rb)


def _slots_kernel(eidx_ref, rank_ref, cnt_ref, slot_ref):
    tm = eidx_ref.shape[1]
    eid = lax.broadcasted_iota(jnp.int32, (N_EXPERTS, tm), 0)
    for k in range(TOP_K):
        before = jnp.sum(jnp.where(eid < eidx_ref[k:k + 1, :], cnt_ref[...], 0.0), axis=0, keepdims=True)
        slot_ref[k:k + 1, :] = (before + rank_ref[k:k + 1, :]).astype(jnp.int32)


def _slots(eidx, rank, cnt):
    n_rows = eidx.shape[1]
    tm = next(c for c in (2048, 1024, 512, 256) if n_rows % c == 0)
    cols = lambda i: (0, i)
    return pl.pallas_call(
        _slots_kernel,
        grid=(n_rows // tm,),
        in_specs=[pl.BlockSpec((TOP_K, tm), cols), pl.BlockSpec((TOP_K, tm), cols),
                  pl.BlockSpec(cnt.shape, lambda i: (0, 0))],
        out_specs=pl.BlockSpec((TOP_K, tm), cols),
        out_shape=jax.ShapeDtypeStruct((TOP_K, n_rows), jnp.int32),
        compiler_params=_cparams(("arbitrary",)),
        name="slots",
    )(eidx, rank, cnt)


SC_WINDOW = 128


def _sc_mesh():
    return plsc.VectorSubcoreMesh(core_axis_name="core", subcore_axis_name="subcore")


def _sc_dispatch(rows, slot_t, n_out, row0):
    width = rows.shape[1]
    n_chunks = slot_t.shape[1] // SC_WINDOW
    info = plsc.get_sparse_core_info()
    n_workers = info.num_cores * info.num_subcores

    @functools.partial(
        pl.kernel, mesh=_sc_mesh(),
        out_type=jax.ShapeDtypeStruct((n_out, width), rows.dtype),
        scratch_types=[pltpu.VMEM((TOP_K, SC_WINDOW), jnp.int32), pltpu.VMEM((SC_WINDOW, width), rows.dtype)],
        name="moe_dispatch")
    def run(rows_hbm, idx_hbm, out_hbm, idx_v, rows_v):
        wid = lax.axis_index("subcore") * info.num_cores + lax.axis_index("core")

        @pl.loop(wid, n_chunks, step=n_workers)
        def _(c):
            r0 = pl.multiple_of(c * SC_WINDOW, SC_WINDOW)
            pltpu.sync_copy(idx_hbm.at[:, pl.ds(r0, SC_WINDOW)], idx_v)
            pltpu.sync_copy(rows_hbm.at[pl.ds(row0 + r0, SC_WINDOW)], rows_v)
            for k in range(TOP_K):
                pltpu.sync_copy(rows_v, out_hbm.at[idx_v.at[k]])

    return run(rows, slot_t)


def _sc_collect(rows, slot_t):
    n_picks, n_rows = slot_t.shape
    width = rows.shape[1]
    n_chunks = n_rows // SC_WINDOW
    info = plsc.get_sparse_core_info()
    n_workers = info.num_cores * info.num_subcores

    @functools.partial(
        pl.kernel, mesh=_sc_mesh(),
        out_type=jax.ShapeDtypeStruct((n_picks, n_rows, width), rows.dtype),
        scratch_types=[pltpu.VMEM((TOP_K, SC_WINDOW), jnp.int32), pltpu.VMEM((SC_WINDOW, width), rows.dtype)],
        name="moe_collect")
    def run(rows_hbm, idx_hbm, out_hbm, idx_v, rows_v):
        wid = lax.axis_index("subcore") * info.num_cores + lax.axis_index("core")

        @pl.loop(wid, n_chunks, step=n_workers)
        def _(c):
            r0 = pl.multiple_of(c * SC_WINDOW, SC_WINDOW)
            pltpu.sync_copy(idx_hbm.at[:, pl.ds(r0, SC_WINDOW)], idx_v)
            for k in range(TOP_K):
                pltpu.sync_copy(rows_hbm.at[idx_v.at[k]], rows_v)
                pltpu.sync_copy(rows_v, out_hbm.at[k, pl.ds(r0, SC_WINDOW)])

    return run(rows, slot_t)


EXPERT_TILE = 512


def _work_items(cnt, n_slots):
    counts = cnt[:, 0].astype(jnp.int32)
    ends = jnp.cumsum(counts)
    n_tiles = n_slots // EXPERT_TILE
    bounds = jnp.sort(jnp.concatenate([jnp.arange(n_tiles, dtype=jnp.int32) * EXPERT_TILE, ends - counts]))
    nxt = jnp.concatenate([bounds[1:], jnp.array([n_slots], jnp.int32)])
    tile = jnp.minimum(bounds // EXPERT_TILE, n_tiles - 1)
    expert = jnp.sum((ends[None, :] <= bounds[:, None]).astype(jnp.int32), axis=1)
    expert = jnp.minimum(expert, N_EXPERTS - 1)
    return tile, expert, bounds - tile * EXPERT_TILE, nxt - tile * EXPERT_TILE


XS_RING = 3


def _experts_kernel(tile_ref, exp_ref, lo_ref, hi_ref, xs_hbm, wg_ref, wu_ref, wd_ref, ys_ref,
                    acc_ref, wgb_ref, wub_ref, wdb_ref, xbuf_ref, xsem):
    i = pl.program_id(0)
    n_items = pl.num_programs(0)
    lo, hi = lo_ref[i], hi_ref[i]

    def tile_copy(item):
        slot = lax.rem(item, XS_RING)
        row0 = pl.multiple_of(tile_ref[item] * EXPERT_TILE, EXPERT_TILE)
        return pltpu.make_async_copy(xs_hbm.at[pl.ds(row0, EXPERT_TILE)], xbuf_ref.at[slot], xsem.at[slot])

    @pl.when(i == 0)
    def _():
        for ahead in range(XS_RING - 1):
            pl.when(ahead < n_items)(lambda: tile_copy(ahead).start())

    @pl.when(i + XS_RING - 1 < n_items)
    def _():
        tile_copy(i + XS_RING - 1).start()

    tile_copy(i).wait()
    xs_ref = xbuf_ref.at[lax.rem(i, XS_RING)]

    @pl.when((i == 0) | (exp_ref[i] != exp_ref[jnp.maximum(i - 1, 0)]))
    def _():
        wgb_ref[...] = wg_ref[...].astype(BF16)
        wub_ref[...] = wu_ref[...].astype(BF16)
        wdb_ref[...] = wd_ref[...].astype(BF16)

    def ffn():
        x_lo, x_hi = _unpack_halves(xs_ref[...])
        x_lo, x_hi = x_lo.astype(BF16), x_hi.astype(BF16)
        n = x_lo.shape[1]
        a = _dot(x_lo, wgb_ref[0:n, :]) + _dot(x_hi, wgb_ref[n:, :])
        u = _dot(x_lo, wub_ref[0:n, :]) + _dot(x_hi, wub_ref[n:, :])
        return _dot(((a * _sigmoid(a)) * u).astype(BF16), wdb_ref[...])

    whole = (lo == 0) & (hi == EXPERT_TILE)

    @pl.when(whole)
    def _():
        ys_ref[...] = _pack_halves(ffn())

    @pl.when(jnp.logical_not(whole) & (hi > lo))
    def _():
        y = ffn()
        row = lax.broadcasted_iota(jnp.int32, y.shape, 0)
        y = jnp.where((row >= lo) & (row < hi), y, 0.0)

        @pl.when(lo == 0)
        def _():
            acc_ref[...] = y

        @pl.when((lo > 0) & (hi < EXPERT_TILE))
        def _():
            acc_ref[...] += y

        @pl.when((lo > 0) & (hi == EXPERT_TILE))
        def _():
            ys_ref[...] = _pack_halves(acc_ref[...] + y)


def _experts(xs, items, layer, exp_wg, exp_wu, exp_wd):
    n_slots, half = xs.shape
    d, hid = exp_wg.shape[-2:]
    tile, expert, lo, hi = items
    grid_spec = pltpu.PrefetchScalarGridSpec(
        num_scalar_prefetch=4,
        grid=(tile.shape[0],),
        in_specs=[pl.BlockSpec(memory_space=pl.ANY),
                  pl.BlockSpec((None, None, d, hid), lambda i, t, e, lo, hi: (layer, e[i], 0, 0)),
                  pl.BlockSpec((None, None, d, hid), lambda i, t, e, lo, hi: (layer, e[i], 0, 0)),
                  pl.BlockSpec((None, None, hid, d), lambda i, t, e, lo, hi: (layer, e[i], 0, 0))],
        out_specs=pl.BlockSpec((EXPERT_TILE, half), lambda i, t, e, lo, hi: (t[i], 0)),
        scratch_shapes=[pltpu.VMEM((EXPERT_TILE, d), F32), pltpu.VMEM((d, hid), BF16),
                        pltpu.VMEM((d, hid), BF16), pltpu.VMEM((hid, d), BF16),
                        pltpu.VMEM((XS_RING, EXPERT_TILE, half), jnp.uint32),
                        pltpu.SemaphoreType.DMA((XS_RING,))])
    return pl.pallas_call(
        _experts_kernel,
        grid_spec=grid_spec,
        out_shape=jax.ShapeDtypeStruct((n_slots, half), jnp.uint32),
        compiler_params=_cparams(("arbitrary",)),
        name="experts",
    )(tile, expert, lo, hi, xs, exp_wg, exp_wu, exp_wd)


def _moe_out_kernel(yg_ref, w_ref, v_ref, sg_ref, su_ref, sd_ref, h1_ref, mod_ref, gpost_ref, *rest,
                    tile0, tiles_per_batch, n_batch, d):
    o_ref = rest[-1]
    i = tile0 + pl.program_id(0)
    x = v_ref[...]
    a = _dot(x, sg_ref[...])
    f = _dot(((a * _sigmoid(a)) * _dot(x, su_ref[...])).astype(BF16), sd_ref[...])
    n = d // 2
    f_lo, f_hi = f[:, :n], f[:, n:]
    w = w_ref[...]
    for k in range(TOP_K):
        y_lo, y_hi = _unpack_halves(yg_ref[k])
        wk = w[:, k:k + 1]
        f_lo = f_lo + wk * y_lo
        f_hi = f_hi + wk * y_hi
    f = jnp.concatenate([f_lo, f_hi], axis=1)
    bi = jnp.minimum(i // tiles_per_batch, n_batch)
    g2 = mod_ref[pl.ds(bi, 1), 5 * d:6 * d]
    o_ref[...] = h1_ref[...] + g2 * _rms(f, gpost_ref[...])


def _moe_out(yg, w, v, sg, su, sd, h1, mods, gpost, tile0, prev_out, *, n_batch, seq):
    n_rows, d = v.shape
    n_tiles = w.shape[0] // TM
    part = lambda i: (i, 0)
    rows = lambda i: (tile0 + i, 0)
    const = lambda i: (0, 0)
    in_specs = [pl.BlockSpec((TOP_K, TM, d // 2), lambda i: (0, i, 0)), pl.BlockSpec((TM, TOP_K), part),
                pl.BlockSpec((TM, d), rows),
                pl.BlockSpec(sg.shape, const), pl.BlockSpec(su.shape, const), pl.BlockSpec(sd.shape, const),
                pl.BlockSpec((TM, d), rows), pl.BlockSpec(mods.shape, const), pl.BlockSpec((1, d), const)]
    args = [yg, w, v, sg, su, sd, h1, mods, gpost]
    aliases = {}
    if prev_out is not None:
        in_specs.append(pl.BlockSpec(memory_space=pl.ANY))
        aliases = {len(args): 0}
        args.append(prev_out)
    return pl.pallas_call(
        functools.partial(_moe_out_kernel, tile0=tile0, tiles_per_batch=seq // TM, n_batch=n_batch, d=d),
        grid=(n_tiles,),
        in_specs=in_specs,
        out_specs=pl.BlockSpec((TM, d), rows),
        out_shape=jax.ShapeDtypeStruct((n_rows, d), F32),
        input_output_aliases=aliases,
        compiler_params=_cparams(("arbitrary",)),
        name="moe_out",
    )(*args)


MOE_PARTS = 1


def _moe(v, vp, layer, rw, rb, exp_wg, exp_wu, exp_wd, sg, su, sd, h1, mods, gpost, *, n_batch, seq):
    n_tiles = v.shape[0] // TM
    per_part = n_tiles // MOE_PARTS
    assert per_part * MOE_PARTS == n_tiles and (per_part * TM * TOP_K) % EXPERT_TILE == 0
    n_slots = per_part * TM * TOP_K
    routed = []
    for p in range(MOE_PARTS):
        eidx, rank, w_t, cnt = _router(v, rw, rb, p * per_part, per_part)
        routed.append((_slots(eidx, rank, cnt), w_t, cnt))
    xs = [_sc_dispatch(vp, slot_t, n_slots, p * per_part * TM) for p, (slot_t, _, _) in enumerate(routed)]
    ys = [_experts(x, _work_items(cnt, n_slots), layer, exp_wg, exp_wu, exp_wd)
          for x, (_, _, cnt) in zip(xs, routed)]
    out = None
    for p, (y, (slot_t, w_t, _)) in enumerate(zip(ys, routed)):
        yg = _sc_collect(y, slot_t)
        out = _moe_out(yg, w_t.T, v, sg, su, sd, h1, mods, gpost, p * per_part, out, n_batch=n_batch, seq=seq)
    return out


def _rope_tables(seq):
    rows = seq // GRID_W
    row_id = jnp.repeat(jnp.arange(rows, dtype=F32), GRID_W)
    col_id = jnp.tile(jnp.arange(GRID_W, dtype=F32), rows)

    def tables(rot_dim):
        axis_dim = rot_dim // 2
        inv_freq = ROPE_BASE ** (-jnp.arange(0, axis_dim, 2, dtype=F32) / axis_dim)
        ang_r = row_id[:, None] * inv_freq[None, :]
        ang_c = col_id[:, None] * inv_freq[None, :]
        cos = jnp.concatenate([jnp.cos(ang_r), jnp.cos(ang_r), jnp.cos(ang_c), jnp.cos(ang_c)], axis=1)
        sin = jnp.concatenate([-jnp.sin(ang_r), jnp.sin(ang_r), -jnp.sin(ang_c), jnp.sin(ang_c)], axis=1)
        return cos, sin

    cos64, sin64 = tables(GQA_DIM)
    cos32, sin32 = tables(MLA_ROPE)
    ones = jnp.ones((seq, MLA_NOPE), F32)
    pad = LANES - MLA_NOPE - MLA_ROPE
    cospe = jnp.concatenate([ones, cos32, jnp.ones((seq, pad), F32)], axis=1)
    sinpe = jnp.concatenate([0 * ones, sin32, jnp.zeros((seq, pad), F32)], axis=1)
    tab = jnp.concatenate([cos64, cos64, sin64, sin64, cospe, sinpe], axis=1)
    ident = jnp.concatenate([jnp.ones((TM, LANES), F32), jnp.zeros((TM, LANES), F32),
                             jnp.ones((TM, LANES), F32), jnp.zeros((TM, LANES), F32)], axis=1)
    return jnp.concatenate([tab, ident], axis=0)


def _pack_w_in(w):
    d = w.shape[0]
    sizes = (MLA_Q_LORA, MLA_KV_LORA, MLA_ROPE, GQA_HEADS * GQA_DIM, GQA_KV_HEADS * GQA_DIM,
             GQA_KV_HEADS * GQA_DIM, RET_HEADS * RET_QK, RET_HEADS * RET_QK, RET_HEADS * RET_V,
             RET_HEADS * RET_V, 3 * d)
    offs, parts = 0, []
    for s in sizes:
        parts.append(w[:, offs:offs + s])
        offs += s
    cq, ckv, kpe, gq, gk, gv, rq, rk, rv, rg, gates = parts

    def twice(m):
        heads = [m[:, i * GQA_DIM:(i + 1) * GQA_DIM] for i in range(GQA_KV_HEADS)]
        return jnp.concatenate([hh for hd in heads for hh in (hd, hd)], axis=1)

    kpe_slab = jnp.concatenate([jnp.zeros((d, MLA_NOPE), F32), kpe,
                                jnp.zeros((d, LANES - MLA_NOPE - MLA_ROPE), F32)], axis=1)
    packed = jnp.concatenate([cq, ckv, kpe_slab, gq * (GQA_DIM ** -0.5 * LOG2_E), twice(gk), twice(gv),
                              rq, rk * RET_QK ** -0.5, rv, rg, gates], axis=1)
    assert packed.shape[1] == W_COLS
    return packed.astype(BF16)


def _pack_mla_up(w_uq, w_ukv):
    r = w_uq.shape[0]
    dq = MLA_NOPE + MLA_ROPE
    wq = jnp.pad(w_uq.reshape(r, MLA_HEADS, dq), ((0, 0), (0, 0), (0, LANES - dq))).reshape(r, MLA_HEADS * LANES)
    kv = w_ukv.reshape(r, MLA_HEADS, MLA_NOPE + MLA_V)
    wk = jnp.pad(kv[:, :, :MLA_NOPE], ((0, 0), (0, 0), (0, LANES - MLA_NOPE))).reshape(r, MLA_HEADS * LANES)
    wv = kv[:, :, MLA_NOPE:]
    zeros = jnp.zeros_like(wv)
    even = jnp.concatenate([wv, zeros], axis=2)
    odd = jnp.concatenate([zeros, wv], axis=2)
    wv = jnp.where((jnp.arange(MLA_HEADS) % 2 == 0)[None, :, None], even, odd).reshape(r, MLA_HEADS * LANES)
    return wq.astype(BF16), wk.astype(BF16), wv.astype(BF16)


def kernel(x, c, ctx, c_ctx, ada_w, ada_b, norm_mix_pre, norm_mix_post, norm_ffn_pre, norm_ffn_post, w_in, mla_q_norm, mla_w_uq, mla_kv_norm, mla_w_ukv, gqa_sink, ret_decay_fwd, ret_decay_bwd, w_br_mla, w_br_gqa, w_br_ret, w_out, router_w, router_bias, exp_w_gate, exp_w_up, exp_w_down, shared_w_gate, shared_w_up, shared_w_down):
    n_batch, seq, d = x.shape
    n_ctx = ctx.shape[1]
    depth = ada_w.shape[0]
    n_lat_rows = n_batch * seq
    assert seq % TM == 0 and (n_batch * n_ctx) % TM == 0 and seq % ATT_TQ == 0 and seq % n_ctx == 0
    assert n_batch < MOD_ROWS and seq % GRID_W == 0

    cond = jnp.zeros((MOD_ROWS, d), F32).at[:n_batch].set(c).at[n_batch].set(c_ctx)
    mods_all = _adaln(cond, ada_w, ada_b)
    rope = _rope_tables(seq)
    h = jnp.concatenate([x.reshape(n_lat_rows, d), ctx.reshape(n_batch * n_ctx, d)], axis=0)
    row = lambda p: p.reshape(1, -1)
    dims = dict(n_batch=n_batch, seq=seq)

    for l in range(depth):
        last = l == depth - 1
        mods = mods_all[l]
        wq, wk, wv = _pack_mla_up(mla_w_uq[l], mla_w_ukv[l])
        mq, mk, mv, gqa, ret, gates = _inproj(h, mods, row(norm_mix_pre[l]), _pack_w_in(w_in[l]), rope,
                                              row(mla_q_norm[l]), row(mla_kv_norm[l]), wq, wk, wv, **dims)
        a = _mla_attention(mq, mk, mv, ctx=n_ctx, with_ctx_queries=not last, **dims)
        sink_tab = jnp.broadcast_to(gqa_sink[l].astype(F32)[:, None] * LOG2_E, (GQA_HEADS, LANES))
        w = _window_attention(gqa, sink_tab, ctx=n_ctx, with_ctx_queries=not last, **dims)
        lg = jnp.concatenate([jax.nn.log_sigmoid(ret_decay_fwd[l].astype(F32)),
                              jax.nn.log_sigmoid(ret_decay_bwd[l].astype(F32))])
        o_f, o_b = _retention(ret, jnp.broadcast_to(lg[:, None], (2 * RET_HEADS, LANES)), ctx=n_ctx, **dims)
        n_rows = n_lat_rows if last else h.shape[0]
        h1, v, vp = _merge(a, w, o_f, o_b, ret, gates, h, mods, row(norm_mix_post[l]), row(norm_ffn_pre[l]),
                           w_br_mla[l].astype(BF16), w_br_gqa[l].astype(BF16), w_br_ret[l].astype(BF16),
                           w_out[l].astype(BF16), n_rows=n_rows, **dims)
        h = _moe(v, vp, l, router_w[l].T.astype(BF16), router_bias[l].astype(F32).reshape(-1, 1),
                 exp_w_gate, exp_w_up, exp_w_down, shared_w_gate[l].astype(BF16),
                 shared_w_up[l].astype(BF16), shared_w_down[l].astype(BF16), h1, mods,
                 row(norm_ffn_post[l]), **dims)
    return h[:n_lat_rows].reshape(n_batch, seq, d)
```

```python
import functools

import numpy as np
import jax
import jax.numpy as jnp
from jax import lax
from jax.experimental import pallas as pl
from jax.experimental.pallas import tpu as pltpu
from jax.experimental.pallas import tpu_sc as plsc

F32 = jnp.float32
BF16 = jnp.bfloat16

GRID_W = 64
ROPE_BASE = 10000.0
NORM_EPS = 1e-6
NEG_INF = -1e30
LOG2_E = 1.4426950408889634
N_MOD = 6
MLA_HEADS, MLA_NOPE, MLA_ROPE, MLA_V = 8, 64, 32, 64
MLA_Q_LORA, MLA_KV_LORA = 256, 256
GQA_HEADS, GQA_KV_HEADS, GQA_DIM, WINDOW = 8, 2, 64, 128
RET_HEADS, RET_QK, RET_V, RET_CHUNK = 4, 64, 128, 128
N_EXPERTS, N_EXPERT_GROUPS, TOPK_GROUPS, TOP_K = 64, 8, 4, 8
EXPERTS_PER_GROUP = N_EXPERTS // N_EXPERT_GROUPS
ROUTED_SCALE = 2.5

LANES = 128
TM = 512
ATT_TQ = 512
WIN_TQ = 256
RET_TILE = 256
MOD_ROWS = 8
V7X_VMEM_LIMIT = 56 * 1024 * 1024

C_CQ, C_CKV, C_KPE = 0, 256, 512
C_G = 640
C_R = 1664
C_GATE = 3200
W_COLS = 6272


def _cparams(sem):
    return pltpu.CompilerParams(dimension_semantics=sem, vmem_limit_bytes=V7X_VMEM_LIMIT)


def _rms(x, g):
    return x * lax.rsqrt(jnp.mean(x * x, axis=-1, keepdims=True) + NORM_EPS) * g


def _sigmoid(x):
    return 0.5 * jnp.tanh(0.5 * x) + 0.5


def _dot(a, b):
    return jnp.dot(a, b, preferred_element_type=F32)


def _dot_nt(a, b):
    return lax.dot_general(a, b, (((1,), (1,)), ((), ())), preferred_element_type=F32)


def _dot_tn(a, b):
    return lax.dot_general(a, b, (((0,), (0,)), ((), ())), preferred_element_type=F32)


def _rope(x, cos, sin, half):
    n = x.shape[-1]
    reps = n // LANES
    if reps > 1:
        cos = jnp.concatenate([cos] * reps, axis=1)
        sin = jnp.concatenate([sin] * reps, axis=1)
    lane = lax.broadcasted_iota(jnp.int32, x.shape, 1)
    up = pltpu.roll(x, half, 1)
    dn = pltpu.roll(x, n - half, 1)
    partner = jnp.where((lane & (2 * half - 1)) < half, dn, up)
    return x * cos + partner * sin


def _lane_lo(shape):
    return (lax.broadcasted_iota(jnp.int32, shape, 1) & (LANES - 1)) < (LANES // 2)


def _pack_halves(x):
    n = x.shape[1] // 2
    bits = lambda t: lax.bitcast_convert_type(t.astype(BF16).astype(F32), jnp.uint32)
    return (bits(x[:, :n]) >> 16) | bits(x[:, n:])


def _unpack_halves(p):
    lo = lax.bitcast_convert_type(p << 16, F32)
    hi = lax.bitcast_convert_type(p & jnp.uint32(0xFFFF0000), F32)
    return lo, hi


def _ada_kernel(c_ref, w_ref, b_ref, o_ref):
    c = c_ref[...]
    s = c * _sigmoid(c)
    o_ref[...] = _dot(s.astype(BF16), w_ref[...].astype(BF16)) + b_ref[...]


def _adaln(cond, ada_w, ada_b):
    n_layers, d, n = ada_w.shape
    tn = 1024
    return pl.pallas_call(
        _ada_kernel,
        grid=(n_layers, n // tn),
        in_specs=[pl.BlockSpec((MOD_ROWS, d), lambda l, j: (0, 0)),
                  pl.BlockSpec((None, d, tn), lambda l, j: (l, 0, j)),
                  pl.BlockSpec((None, 1, tn), lambda l, j: (l, 0, j))],
        out_specs=pl.BlockSpec((None, MOD_ROWS, tn), lambda l, j: (l, 0, j)),
        out_shape=jax.ShapeDtypeStruct((n_layers, MOD_ROWS, n), F32),
        compiler_params=_cparams(("arbitrary", "arbitrary")),
        name="adaln",
    )(cond, ada_w, ada_b.reshape(n_layers, 1, n))


def _inproj_kernel(h_ref, mod_ref, gpre_ref, w_ref, rope_ref, qn_ref, kvn_ref, wuq_ref, wuk_ref, wuv_ref,
                   mq_ref, mk_ref, mv_ref, gqa_ref, ret_ref, gate_ref, *, tiles_per_batch, n_batch, d):
    i = pl.program_id(0)
    bi = jnp.minimum(i // tiles_per_batch, n_batch)
    sh = mod_ref[pl.ds(bi, 1), 0:d]
    sc = mod_ref[pl.ds(bi, 1), d:2 * d]
    u = (_rms(h_ref[...], gpre_ref[...]) * (1.0 + sc) + sh).astype(BF16)

    cos64 = rope_ref[:, 0:LANES]
    sin64 = rope_ref[:, LANES:2 * LANES]
    cospe = rope_ref[:, 2 * LANES:3 * LANES]
    sinpe = rope_ref[:, 3 * LANES:4 * LANES]

    c = _dot(u, w_ref[:, C_CQ:C_G])
    kpe = _rope(c[:, C_KPE:C_G], cospe, sinpe, MLA_ROPE // 4)
    qn = _rms(c[:, C_CQ:C_CKV], qn_ref[...]).astype(BF16)
    q = _rope(_dot(qn, wuq_ref[...]), cospe, sinpe, MLA_ROPE // 4)
    mq_ref[...] = (q * ((MLA_NOPE + MLA_ROPE) ** -0.5 * LOG2_E)).astype(mq_ref.dtype)
    kvn = _rms(c[:, C_CKV:C_KPE], kvn_ref[...]).astype(BF16)
    k = _dot(kvn, wuk_ref[...]) + jnp.concatenate([kpe] * MLA_HEADS, axis=1)
    mk_ref[...] = k.astype(mk_ref.dtype)
    v = _dot(kvn, wuv_ref[...])
    lane = lax.broadcasted_iota(jnp.int32, v.shape, 1)
    value_lane = ((lane & (LANES - 1)) < MLA_V) == (((lane >> (LANES.bit_length() - 1)) & 1) == 0)
    mv_ref[...] = jnp.where(value_lane, v, 1.0).astype(mv_ref.dtype)

    g = _dot(u, w_ref[:, C_G:C_R])
    n_qk = GQA_HEADS * GQA_DIM + 2 * GQA_KV_HEADS * GQA_DIM
    gqa_ref[:, 0:n_qk] = _rope(g[:, 0:n_qk], cos64, sin64, GQA_DIM // 4).astype(gqa_ref.dtype)
    gqa_ref[:, n_qk:] = g[:, n_qk:].astype(gqa_ref.dtype)

    r = _dot(u, w_ref[:, C_R:C_GATE])
    n_qk = 2 * RET_HEADS * RET_QK
    ret_ref[:, 0:n_qk] = _rope(r[:, 0:n_qk], cos64, sin64, RET_QK // 4)
    ret_ref[:, n_qk:] = r[:, n_qk:]

    gate_ref[...] = _dot(u, w_ref[:, C_GATE:W_COLS]).astype(gate_ref.dtype)


def _inproj(h, mods, gpre, w_all, rope, qn, kvn, wuq, wuk, wuv, *, n_batch, seq):
    t, d = h.shape
    tiles_per_batch = seq // TM
    n_lat_tiles = n_batch * tiles_per_batch
    const = lambda i: (0, 0)
    rows = lambda i: (i, 0)
    rope_idx = lambda i: (jnp.where(i < n_lat_tiles, i % tiles_per_batch, tiles_per_batch), 0)
    hq = MLA_HEADS * LANES
    outs = [jax.ShapeDtypeStruct((t, hq), BF16), jax.ShapeDtypeStruct((t, hq), BF16),
            jax.ShapeDtypeStruct((t, hq), BF16),
            jax.ShapeDtypeStruct((t, C_R - C_G), BF16),
            jax.ShapeDtypeStruct((t, C_GATE - C_R), F32),
            jax.ShapeDtypeStruct((t, W_COLS - C_GATE), BF16)]
    return pl.pallas_call(
        functools.partial(_inproj_kernel, tiles_per_batch=tiles_per_batch, n_batch=n_batch, d=d),
        grid=(t // TM,),
        in_specs=[pl.BlockSpec((TM, d), rows),
                  pl.BlockSpec(mods.shape, const),
                  pl.BlockSpec((1, d), const),
                  pl.BlockSpec(w_all.shape, const, pipeline_mode=pl.Buffered(1)),
                  pl.BlockSpec((TM, 4 * LANES), rope_idx),
                  pl.BlockSpec(qn.shape, const), pl.BlockSpec(kvn.shape, const),
                  pl.BlockSpec(wuq.shape, const), pl.BlockSpec(wuk.shape, const), pl.BlockSpec(wuv.shape, const)],
        out_specs=[pl.BlockSpec((TM, o.shape[1]), rows) for o in outs],
        out_shape=outs,
        compiler_params=_cparams(("arbitrary",)),
        name="inproj",
    )(h, mods, gpre, w_all, rope, qn, kvn, wuq, wuk, wuv)


MLA_HEADS_PER_STEP = 4


def _mla_kernel(q_ref, kl_ref, kc_ref, vl_ref, vc_ref, o_ref, s_ref, p_ref, *, with_lat):
    n_ctx = kc_ref.shape[0]

    def body(with_lat):
        n_keys = n_ctx + (kl_ref.shape[0] if with_lat else 0)

        def scores(h):
            sl = slice(h * LANES, (h + 1) * LANES)
            s_ref[h % 2, :, 0:n_ctx] = _dot_nt(q_ref[:, sl], kc_ref[:, sl])
            if with_lat:
                s_ref[h % 2, :, n_ctx:n_keys] = _dot_nt(q_ref[:, sl], kl_ref[:, sl])

        def probs(h):
            s = s_ref[h % 2, :, 0:n_keys]
            p_ref[h % 2, :, 0:n_keys] = jnp.exp2(s - jnp.max(s, axis=-1, keepdims=True)).astype(BF16)

        def weighted(h):
            sl = slice(h * LANES, (h + 1) * LANES)
            o = _dot(p_ref[h % 2, :, 0:n_ctx], vc_ref[:, sl])
            if with_lat:
                o = o + _dot(p_ref[h % 2, :, n_ctx:n_keys], vl_ref[:, sl])
            return o / pltpu.roll(o, LANES // 2, 1)

        outs = [None] * MLA_HEADS_PER_STEP
        scores(0)
        for h in range(MLA_HEADS_PER_STEP):
            if h + 1 < MLA_HEADS_PER_STEP:
                scores(h + 1)
            probs(h)
            outs[h] = weighted(h)
        for pr in range(MLA_HEADS_PER_STEP // 2):
            even, odd = outs[2 * pr], outs[2 * pr + 1]
            o_ref[:, pr * LANES:(pr + 1) * LANES] = jnp.where(_lane_lo(even.shape), even, odd).astype(o_ref.dtype)

    body(with_lat)


def _mla_call(mq, mk, mv, *, n_batch, seq, ctx, latent_queries):
    hps = MLA_HEADS_PER_STEP
    ctx_blk0 = n_batch * seq // ctx
    tq = ATT_TQ if latent_queries else ctx
    nq = seq // tq if latent_queries else 1
    q_blk0 = 0 if latent_queries else n_batch * seq // tq
    q_idx = lambda b, g, i: (q_blk0 + b * nq + i, g)
    ctx_idx = lambda b, g, i: (ctx_blk0 + b, g)
    lat_idx = (lambda b, g, i: (b, g)) if latent_queries else ctx_idx
    n_lat = seq if latent_queries else ctx
    n_keys = ctx + (seq if latent_queries else 0)
    in_specs = [pl.BlockSpec((tq, hps * LANES), q_idx),
                pl.BlockSpec((n_lat, hps * LANES), lat_idx), pl.BlockSpec((ctx, hps * LANES), ctx_idx),
                pl.BlockSpec((n_lat, hps * LANES), lat_idx), pl.BlockSpec((ctx, hps * LANES), ctx_idx)]
    return pl.pallas_call(
        functools.partial(_mla_kernel, with_lat=latent_queries),
        grid=(n_batch, MLA_HEADS // hps, nq),
        in_specs=in_specs,
        out_specs=pl.BlockSpec((tq, hps * MLA_V), lambda b, g, i: (b * nq + i, g)),
        out_shape=jax.ShapeDtypeStruct((n_batch * nq * tq, MLA_HEADS * MLA_V), BF16),
        scratch_shapes=[pltpu.VMEM((2, tq, n_keys), F32), pltpu.VMEM((2, tq, n_keys), BF16)],
        compiler_params=_cparams(("arbitrary", "arbitrary", "arbitrary")),
        name="mla_attn" if latent_queries else "mla_attn_ctx",
    )(mq, mk, mk, mv, mv)


def _mla_attention(mq, mk, mv, *, n_batch, seq, ctx, with_ctx_queries):
    dims = dict(n_batch=n_batch, seq=seq, ctx=ctx)
    lat = _mla_call(mq, mk, mv, latent_queries=True, **dims)
    return lat, (_mla_call(mq, mk, mv, latent_queries=False, **dims) if with_ctx_queries else lat)


def _win_kernel(q_ref, kp_ref, kcur_ref, kn_ref, vp_ref, vcur_ref, vn_ref, kc_ref, vc_ref, sink_ref, o_ref,
                s_ref, p_ref, *, seq, with_lat):
    i = pl.program_id(1)
    tq = q_ref.shape[0]
    group = GQA_HEADS // GQA_KV_HEADS

    def body(with_lat):
        n_ctx = kc_ref.shape[0]
        n_keys = n_ctx + (tq + 2 * WINDOW if with_lat else 0)
        if with_lat:
            q_pos = i * tq + lax.broadcasted_iota(jnp.int32, (tq, n_keys), 0)
            k_pos = i * tq - WINDOW - n_ctx + lax.broadcasted_iota(jnp.int32, (tq, n_keys), 1)
            in_band = (jnp.abs(q_pos - k_pos) <= WINDOW) & (k_pos >= 0) & (k_pos < seq)
            valid = in_band | (lax.broadcasted_iota(jnp.int32, (tq, n_keys), 1) < n_ctx)
        lo = _lane_lo((tq, LANES))
        lo_k = _lane_lo((n_keys, LANES))
        keys, values = [], []
        for kv in range(GQA_KV_HEADS):
            sl = slice(kv * LANES, (kv + 1) * LANES)
            if with_lat:
                k_all = jnp.concatenate([kc_ref[:, sl], kp_ref[:, sl], kcur_ref[:, sl], kn_ref[:, sl]], axis=0)
                v_all = jnp.concatenate([vc_ref[:, sl], vp_ref[:, sl], vcur_ref[:, sl], vn_ref[:, sl]], axis=0)
            else:
                k_all, v_all = kc_ref[:, sl], vc_ref[:, sl]
            keys.append(k_all)
            one = jnp.ones_like(v_all)
            values.append((jnp.where(lo_k, v_all, one), jnp.where(lo_k, one, v_all)))

        def scores(hd):
            kv, pair = hd // group, hd // 2
            qp = q_ref[:, pair * LANES:(pair + 1) * LANES]
            qm = jnp.where(lo if hd % 2 == 0 else jnp.logical_not(lo), qp, jnp.zeros_like(qp))
            s = _dot_nt(qm, keys[kv])
            s_ref[hd % 2, :, 0:n_keys] = jnp.where(valid, s, NEG_INF) if with_lat else s

        def probs(hd):
            s = s_ref[hd % 2, :, 0:n_keys]
            m = jnp.maximum(jnp.max(s, axis=-1, keepdims=True), sink_ref[hd:hd + 1, 0:1])
            p_ref[hd % 2, :, 0:n_keys] = jnp.exp2(s - m).astype(BF16)
            return jnp.exp2(sink_ref[hd:hd + 1, 0:1] - m)

        def weighted(hd, sink_term):
            o = _dot(p_ref[hd % 2, :, 0:n_keys], values[hd // group][hd % 2])
            return o / (pltpu.roll(o, LANES // 2, 1) + sink_term)

        outs = [None] * GQA_HEADS
        scores(0)
        for hd in range(GQA_HEADS):
            if hd + 1 < GQA_HEADS:
                scores(hd + 1)
            outs[hd] = weighted(hd, probs(hd))
        for pair in range(GQA_HEADS // 2):
            o_ref[:, pair * LANES:(pair + 1) * LANES] = jnp.where(
                lo, outs[2 * pair], outs[2 * pair + 1]).astype(o_ref.dtype)

    body(with_lat)


def _win_call(gqa, sink_tab, *, n_batch, seq, ctx, latent_queries):
    tq = WIN_TQ if latent_queries else ctx
    nq = seq // tq if latent_queries else 1
    q_blk0 = 0 if latent_queries else n_batch * seq // tq
    per_tile = tq // WINDOW
    n_win_blocks = seq // WINDOW
    ctx_blk0 = n_batch * seq // ctx
    nqk = GQA_HEADS * GQA_DIM
    kw = 2 * GQA_KV_HEADS * GQA_DIM
    k_col, v_col = nqk // kw, nqk // kw + 1
    q_idx = lambda b, i: (q_blk0 + b * nq + i, 0)
    cidx = lambda col: (lambda b, i: (ctx_blk0 + b, col))
    if latent_queries:
        cur = lambda col: (lambda b, i: (b * nq + i, col))
        prev = lambda col: (lambda b, i: (b * n_win_blocks + jnp.maximum(per_tile * i - 1, 0), col))
        nxt = lambda col: (lambda b, i: (b * n_win_blocks + jnp.minimum(per_tile * (i + 1), n_win_blocks - 1), col))
        band = [((WINDOW, kw), prev), ((tq, kw), cur), ((WINDOW, kw), nxt)]
    else:
        band = [((ctx, kw), cidx)] * 3
    n_keys = ctx + (tq + 2 * WINDOW if latent_queries else 0)
    in_specs = ([pl.BlockSpec((tq, nqk), q_idx)]
                + [pl.BlockSpec(shape, idx(k_col)) for shape, idx in band]
                + [pl.BlockSpec(shape, idx(v_col)) for shape, idx in band]
                + [pl.BlockSpec((ctx, kw), cidx(k_col)), pl.BlockSpec((ctx, kw), cidx(v_col)),
                   pl.BlockSpec(sink_tab.shape, lambda b, i: (0, 0))])
    return pl.pallas_call(
        functools.partial(_win_kernel, seq=seq, with_lat=latent_queries),
        grid=(n_batch, nq),
        in_specs=in_specs,
        out_specs=pl.BlockSpec((tq, nqk), lambda b, i: (b * nq + i, 0)),
        out_shape=jax.ShapeDtypeStruct((n_batch * nq * tq, nqk), BF16),
        scratch_shapes=[pltpu.VMEM((2, tq, n_keys), F32), pltpu.VMEM((2, tq, n_keys), BF16)],
        compiler_params=_cparams(("arbitrary", "arbitrary")),
        name="win_attn" if latent_queries else "win_attn_ctx",
    )(*([gqa] * 9 + [sink_tab]))


def _window_attention(gqa, sink_tab, *, n_batch, seq, ctx, with_ctx_queries):
    dims = dict(n_batch=n_batch, seq=seq, ctx=ctx)
    lat = _win_call(gqa, sink_tab, latent_queries=True, **dims)
    return lat, (_win_call(gqa, sink_tab, latent_queries=False, **dims) if with_ctx_queries else lat)


def _ret_kernel(f_ref, b_ref, lg_ref, of_ref, ob_ref, sf_ref, sb_ref, qdec_ref, kdec_ref, cdec_ref, inner_ref):
    @pl.when(pl.program_id(1) == 0)
    def _():
        sf_ref[...] = jnp.zeros_like(sf_ref)
        sb_ref[...] = jnp.zeros_like(sb_ref)

    L = f_ref.shape[0]
    lo = _lane_lo((L, LANES))
    srow_lo = lax.broadcasted_iota(jnp.int32, (LANES, LANES), 0) < RET_QK
    nq = RET_HEADS * RET_QK
    n_pairs = RET_HEADS // 2

    @pl.when(pl.program_id(1) == 0)
    def _():
        ii = lax.broadcasted_iota(jnp.int32, (L, L), 0)
        jj = lax.broadcasted_iota(jnp.int32, (L, L), 1)
        row = lax.broadcasted_iota(jnp.int32, (L, LANES), 0).astype(F32)
        for direction, forward in enumerate((True, False)):
            dist = ii - jj if forward else jj - ii
            distf = jnp.maximum(dist, 0).astype(F32)
            for pr in range(n_pairs):
                r0 = direction * RET_HEADS + 2 * pr
                lg = [lg_ref[r0 + e:r0 + e + 1, :] for e in range(2)]
                lg_lane = jnp.where(lo, lg[0], lg[1])
                qdec_ref[direction * n_pairs + pr] = jnp.exp(lg_lane * ((row + 1.0) if forward else (L - row)))
                kdec_ref[direction * n_pairs + pr] = jnp.exp(lg_lane * ((L - 1.0 - row) if forward else row))
                cdec_ref[direction * n_pairs + pr] = jnp.where(srow_lo, jnp.exp(lg[0] * float(L)),
                                                               jnp.exp(lg[1] * float(L)))
                for e in range(2):
                    inner_ref[r0 + e] = jnp.where(dist >= 0, jnp.exp(lg[e][:, 0:1] * distf), 0.0)

    def scan_chunk(x_ref, o_ref, s_ref, direction):
        for pr in range(n_pairs):
            q = x_ref[:, pr * LANES:(pr + 1) * LANES]
            k = x_ref[:, nq + pr * LANES:nq + (pr + 1) * LANES]
            qd = q * qdec_ref[direction * n_pairs + pr]
            kdb = (k * kdec_ref[direction * n_pairs + pr]).astype(BF16)
            kb = k.astype(BF16)
            state = s_ref[pr]
            state_b = state.astype(BF16)
            upd = []
            for e in range(2):
                hd = 2 * pr + e
                keep = lo if e == 0 else jnp.logical_not(lo)
                v = x_ref[:, 2 * nq + hd * RET_V:2 * nq + (hd + 1) * RET_V].astype(BF16)
                attn = _dot_nt(jnp.where(keep, q, 0.0).astype(BF16), kb) * inner_ref[direction * RET_HEADS + hd]
                o = _dot(attn.astype(BF16), v) + _dot(jnp.where(keep, qd, 0.0).astype(BF16), state_b)
                o_ref[:, hd * RET_V:(hd + 1) * RET_V] = o
                upd.append(_dot_tn(kdb, v))
            s_ref[pr] = state * cdec_ref[direction * n_pairs + pr] + jnp.where(srow_lo, upd[0], upd[1])

    scan_chunk(f_ref, of_ref, sf_ref, 0)
    scan_chunk(b_ref, ob_ref, sb_ref, 1)


def _retention(ret, lg_tab, *, n_batch, seq, ctx):
    t = ret.shape[0]
    L = RET_TILE
    assert seq % L == 0 and ctx % L == 0
    n_lat, n_ctx = seq // L, ctx // L
    ctx0 = n_batch * n_lat
    width = 2 * RET_HEADS * RET_QK + RET_HEADS * RET_V
    fwd = lambda b, s: (jnp.where(s < n_ctx, ctx0 + b * n_ctx + s, b * n_lat + s - n_ctx), 0)
    bwd = lambda b, s: (jnp.where(s < n_ctx, ctx0 + b * n_ctx + n_ctx - 1 - s, b * n_lat + n_lat - 1 - (s - n_ctx)), 0)
    out = jax.ShapeDtypeStruct((t, RET_HEADS * RET_V), F32)
    return pl.pallas_call(
        _ret_kernel,
        grid=(n_batch, n_lat + n_ctx),
        in_specs=[pl.BlockSpec((L, width), fwd), pl.BlockSpec((L, width), bwd),
                  pl.BlockSpec(lg_tab.shape, lambda b, s: (0, 0))],
        out_specs=[pl.BlockSpec((L, RET_HEADS * RET_V), fwd), pl.BlockSpec((L, RET_HEADS * RET_V), bwd)],
        out_shape=[out, out],
        scratch_shapes=[pltpu.VMEM((RET_HEADS // 2, LANES, RET_V), F32),
                        pltpu.VMEM((RET_HEADS // 2, LANES, RET_V), F32),
                        pltpu.VMEM((RET_HEADS, L, LANES), F32), pltpu.VMEM((RET_HEADS, L, LANES), F32),
                        pltpu.VMEM((RET_HEADS, LANES, RET_V), F32), pltpu.VMEM((2 * RET_HEADS, L, L), F32)],
        compiler_params=_cparams(("arbitrary", "arbitrary")),
        name="retention",
    )(ret, ret, lg_tab)


def _merge_kernel(al_ref, ac_ref, wl_ref, wc_ref, of_ref, ob_ref, rg_ref, gt_ref, h_ref, mod_ref, gpost_ref,
                  gffn_ref, wa_ref, ww_ref, wr_ref, wo_ref, h1_ref, v_ref, vp_ref,
                  *, tiles_per_batch, n_batch, d):
    i = pl.program_id(0)
    bi = jnp.minimum(i // tiles_per_batch, n_batch)
    latent = i < tiles_per_batch * n_batch
    a_tile = jnp.where(latent, al_ref[...], ac_ref[...])
    w_tile = jnp.where(latent, wl_ref[...], wc_ref[...])
    o = of_ref[...] + ob_ref[...]
    normed = []
    for hd in range(RET_HEADS):
        oh = o[:, hd * RET_V:(hd + 1) * RET_V]
        dev = oh - jnp.mean(oh, axis=-1, keepdims=True)
        normed.append(dev * lax.rsqrt(jnp.mean(dev * dev, axis=-1, keepdims=True) + NORM_EPS))
    g = rg_ref[...]
    r = (g * _sigmoid(g)) * jnp.concatenate(normed, axis=1)
    y = (_sigmoid(gt_ref[:, 0:d].astype(F32)) * _dot(a_tile, wa_ref[...])
         + _sigmoid(gt_ref[:, d:2 * d].astype(F32)) * _dot(w_tile, ww_ref[...])
         + _sigmoid(gt_ref[:, 2 * d:3 * d].astype(F32)) * _dot(r.astype(BF16), wr_ref[...]))
    z = _dot(y.astype(BF16), wo_ref[...])
    g1 = mod_ref[pl.ds(bi, 1), 2 * d:3 * d]
    sh2 = mod_ref[pl.ds(bi, 1), 3 * d:4 * d]
    sc2 = mod_ref[pl.ds(bi, 1), 4 * d:5 * d]
    h1 = h_ref[...] + g1 * _rms(z, gpost_ref[...])
    h1_ref[...] = h1
    v = _rms(h1, gffn_ref[...]) * (1.0 + sc2) + sh2
    v_ref[...] = v.astype(v_ref.dtype)
    vp_ref[...] = _pack_halves(v)


def _merge(a, w, o_f, o_b, ret, gates, h, mods, gpost, gffn, wa, ww, wr, wo, *, n_rows, n_batch, seq):
    d = h.shape[1]
    n_lat_tiles = n_batch * seq // TM
    rows = lambda i: (i, 0)
    lat_rows = lambda i: (jnp.minimum(i, n_lat_tiles - 1), 0)
    ctx_rows = lambda i: (jnp.maximum(i - n_lat_tiles, 0), 0)
    const = lambda i: (0, 0)
    rv = RET_HEADS * RET_V
    rg_col = (2 * RET_HEADS * RET_QK + rv) // rv
    outs = [jax.ShapeDtypeStruct((n_rows, d), F32), jax.ShapeDtypeStruct((n_rows, d), BF16),
            jax.ShapeDtypeStruct((n_rows, d // 2), jnp.uint32)]
    return pl.pallas_call(
        functools.partial(_merge_kernel, tiles_per_batch=seq // TM, n_batch=n_batch, d=d),
        grid=(n_rows // TM,),
        in_specs=[pl.BlockSpec((TM, a[0].shape[1]), lat_rows), pl.BlockSpec((TM, a[1].shape[1]), ctx_rows),
                  pl.BlockSpec((TM, w[0].shape[1]), lat_rows), pl.BlockSpec((TM, w[1].shape[1]), ctx_rows),
                  pl.BlockSpec((TM, rv), rows), pl.BlockSpec((TM, rv), rows),
                  pl.BlockSpec((TM, rv), lambda i: (i, rg_col)),
                  pl.BlockSpec((TM, 3 * d), rows), pl.BlockSpec((TM, d), rows),
                  pl.BlockSpec(mods.shape, const), pl.BlockSpec((1, d), const), pl.BlockSpec((1, d), const),
                  pl.BlockSpec(wa.shape, const), pl.BlockSpec(ww.shape, const),
                  pl.BlockSpec(wr.shape, const), pl.BlockSpec(wo.shape, const)],
        out_specs=[pl.BlockSpec((TM, o.shape[1]), rows) for o in outs],
        out_shape=outs,
        compiler_params=_cparams(("arbitrary",)),
        name="merge",
    )(a[0], a[1], w[0], w[1], o_f, o_b, ret, gates, h, mods, gpost, gffn, wa, ww, wr, wo)


def _router_kernel(v_ref, rw_ref, rb_ref, eidx_ref, rank_ref, w_ref, cnt_ref, carry_ref):
    @pl.when(pl.program_id(0) == 0)
    def _():
        carry_ref[...] = jnp.zeros_like(carry_ref)

    tm = v_ref.shape[0]
    scores = _sigmoid(_dot_nt(rw_ref[...], v_ref[...]))
    sel = scores + rb_ref[...]
    neg = -jnp.inf
    n_grp, per = N_EXPERT_GROUPS, EXPERTS_PER_GROUP

    sel3 = sel.reshape(n_grp, per, tm)
    member_id = lax.broadcasted_iota(jnp.int32, sel3.shape, 1)
    m1 = jnp.max(sel3, axis=1, keepdims=True)
    i1 = jnp.min(jnp.where(sel3 == m1, member_id, per), axis=1, keepdims=True)
    m2 = jnp.max(jnp.where(member_id == i1, neg, sel3), axis=1, keepdims=True)
    gscore = (m1 + m2).reshape(n_grp, tm)
    gid = lax.broadcasted_iota(jnp.int32, gscore.shape, 0)
    ahead = jnp.zeros(gscore.shape, jnp.int32)
    for gj in range(n_grp):
        other = gscore[gj:gj + 1, :]
        ahead = ahead + jnp.where((other > gscore) | ((other == gscore) & (gid > gj)), 1, 0)
    group_ok = (ahead < TOPK_GROUPS).reshape(n_grp, 1, tm)
    sel = jnp.where(group_ok, sel3, NEG_INF).reshape(N_EXPERTS, tm)

    eid = lax.broadcasted_iota(jnp.int32, sel.shape, 0)
    chosen = jnp.zeros(sel.shape, jnp.bool_)
    picks = []
    for _ in range(TOP_K):
        m = jnp.max(sel, axis=0, keepdims=True)
        idx = jnp.min(jnp.where(sel == m, eid, N_EXPERTS), axis=0, keepdims=True)
        hit = eid == idx
        chosen = chosen | hit
        sel = jnp.where(hit, neg, sel)
        picks.append(idx)
    w = jnp.where(chosen, scores, 0.0)
    gate = ROUTED_SCALE * w / jnp.sum(w, axis=0, keepdims=True)

    member = jnp.where(chosen, 1.0, 0.0)
    earlier = lax.broadcasted_iota(jnp.int32, (tm, tm), 0) < lax.broadcasted_iota(jnp.int32, (tm, tm), 1)
    pos = _dot(member.astype(BF16), jnp.where(earlier, 1.0, 0.0).astype(BF16)) + carry_ref[...]
    for k, idx in enumerate(picks):
        hit = eid == idx
        eidx_ref[k:k + 1, :] = idx
        rank_ref[k:k + 1, :] = jnp.sum(jnp.where(hit, pos, 0.0), axis=0, keepdims=True)
        w_ref[k:k + 1, :] = jnp.sum(jnp.where(hit, gate, 0.0), axis=0, keepdims=True)
    carry_ref[...] += jnp.sum(member, axis=1, keepdims=True)
    cnt_ref[...] = carry_ref[...]


def _router(v, rw_t, rb, tile0, n_tiles):
    d = v.shape[1]
    n_rows = n_tiles * TM
    cols = lambda i: (0, i)
    const = lambda i: (0, 0)
    outs = [jax.ShapeDtypeStruct((TOP_K, n_rows), jnp.int32), jax.ShapeDtypeStruct((TOP_K, n_rows), F32),
            jax.ShapeDtypeStruct((TOP_K, n_rows), F32), jax.ShapeDtypeStruct((N_EXPERTS, 1), F32)]
    return pl.pallas_call(
        _router_kernel,
        grid=(n_tiles,),
        in_specs=[pl.BlockSpec((TM, d), lambda i: (tile0 + i, 0)), pl.BlockSpec(rw_t.shape, const),
                  pl.BlockSpec(rb.shape, const)],
        out_specs=[pl.BlockSpec((TOP_K, TM), cols), pl.BlockSpec((TOP_K, TM), cols),
                   pl.BlockSpec((TOP_K, TM), cols), pl.BlockSpec((N_EXPERTS, 1), const)],
        out_shape=outs,
        scratch_shapes=[pltpu.VMEM((N_EXPERTS, 1), F32)],
        compiler_params=_cparams(("arbitrary",)),
        name="router",
    )(v, rw_t, rb)


def _slots_kernel(eidx_ref, rank_ref, cnt_ref, slot_ref):
    tm = eidx_ref.shape[1]
    eid = lax.broadcasted_iota(jnp.int32, (N_EXPERTS, tm), 0)
    for k in range(TOP_K):
        before = jnp.sum(jnp.where(eid < eidx_ref[k:k + 1, :], cnt_ref[...], 0.0), axis=0, keepdims=True)
        slot_ref[k:k + 1, :] = (before + rank_ref[k:k + 1, :]).astype(jnp.int32)


def _slots(eidx, rank, cnt):
    n_rows = eidx.shape[1]
    tm = next(c for c in (2048, 1024, 512, 256) if n_rows % c == 0)
    cols = lambda i: (0, i)
    return pl.pallas_call(
        _slots_kernel,
        grid=(n_rows // tm,),
        in_specs=[pl.BlockSpec((TOP_K, tm), cols), pl.BlockSpec((TOP_K, tm), cols),
                  pl.BlockSpec(cnt.shape, lambda i: (0, 0))],
        out_specs=pl.BlockSpec((TOP_K, tm), cols),
        out_shape=jax.ShapeDtypeStruct((TOP_K, n_rows), jnp.int32),
        compiler_params=_cparams(("arbitrary",)),
        name="slots",
    )(eidx, rank, cnt)


SC_WINDOW = 128


def _sc_mesh():
    return plsc.VectorSubcoreMesh(core_axis_name="core", subcore_axis_name="subcore")


def _sc_dispatch(rows, slot_t, n_out, row0):
    width = rows.shape[1]
    n_chunks = slot_t.shape[1] // SC_WINDOW
    info = plsc.get_sparse_core_info()
    n_workers = info.num_cores * info.num_subcores

    @functools.partial(
        pl.kernel, mesh=_sc_mesh(),
        out_type=jax.ShapeDtypeStruct((n_out, width), rows.dtype),
        scratch_types=[pltpu.VMEM((TOP_K, SC_WINDOW), jnp.int32), pltpu.VMEM((SC_WINDOW, width), rows.dtype)],
        name="moe_dispatch")
    def run(rows_hbm, idx_hbm, out_hbm, idx_v, rows_v):
        wid = lax.axis_index("subcore") * info.num_cores + lax.axis_index("core")

        @pl.loop(wid, n_chunks, step=n_workers)
        def _(c):
            r0 = pl.multiple_of(c * SC_WINDOW, SC_WINDOW)
            pltpu.sync_copy(idx_hbm.at[:, pl.ds(r0, SC_WINDOW)], idx_v)
            pltpu.sync_copy(rows_hbm.at[pl.ds(row0 + r0, SC_WINDOW)], rows_v)
            for k in range(TOP_K):
                pltpu.sync_copy(rows_v, out_hbm.at[idx_v.at[k]])

    return run(rows, slot_t)


def _sc_collect(rows, slot_t):
    n_picks, n_rows = slot_t.shape
    width = rows.shape[1]
    n_chunks = n_rows // SC_WINDOW
    info = plsc.get_sparse_core_info()
    n_workers = info.num_cores * info.num_subcores

    @functools.partial(
        pl.kernel, mesh=_sc_mesh(),
        out_type=jax.ShapeDtypeStruct((n_picks, n_rows, width), rows.dtype),
        scratch_types=[pltpu.VMEM((TOP_K, SC_WINDOW), jnp.int32), pltpu.VMEM((SC_WINDOW, width), rows.dtype)],
        name="moe_collect")
    def run(rows_hbm, idx_hbm, out_hbm, idx_v, rows_v):
        wid = lax.axis_index("subcore") * info.num_cores + lax.axis_index("core")

        @pl.loop(wid, n_chunks, step=n_workers)
        def _(c):
            r0 = pl.multiple_of(c * SC_WINDOW, SC_WINDOW)
            pltpu.sync_copy(idx_hbm.at[:, pl.ds(r0, SC_WINDOW)], idx_v)
            for k in range(TOP_K):
                pltpu.sync_copy(rows_hbm.at[idx_v.at[k]], rows_v)
                pltpu.sync_copy(rows_v, out_hbm.at[k, pl.ds(r0, SC_WINDOW)])

    return run(rows, slot_t)


EXPERT_TILE = 512


def _work_items(cnt, n_slots):
    counts = cnt[:, 0].astype(jnp.int32)
    ends = jnp.cumsum(counts)
    n_tiles = n_slots // EXPERT_TILE
    bounds = jnp.sort(jnp.concatenate([jnp.arange(n_tiles, dtype=jnp.int32) * EXPERT_TILE, ends - counts]))
    nxt = jnp.concatenate([bounds[1:], jnp.array([n_slots], jnp.int32)])
    tile = jnp.minimum(bounds // EXPERT_TILE, n_tiles - 1)
    expert = jnp.sum((ends[None, :] <= bounds[:, None]).astype(jnp.int32), axis=1)
    expert = jnp.minimum(expert, N_EXPERTS - 1)
    return tile, expert, bounds - tile * EXPERT_TILE, nxt - tile * EXPERT_TILE


XS_RING = 3


def _experts_kernel(tile_ref, exp_ref, lo_ref, hi_ref, xs_hbm, wg_ref, wu_ref, wd_ref, ys_ref,
                    acc_ref, wgb_ref, wub_ref, wdb_ref, xbuf_ref, xsem):
    i = pl.program_id(0)
    n_items = pl.num_programs(0)
    lo, hi = lo_ref[i], hi_ref[i]

    def tile_copy(item):
        slot = lax.rem(item, XS_RING)
        row0 = pl.multiple_of(tile_ref[item] * EXPERT_TILE, EXPERT_TILE)
        return pltpu.make_async_copy(xs_hbm.at[pl.ds(row0, EXPERT_TILE)], xbuf_ref.at[slot], xsem.at[slot])

    @pl.when(i == 0)
    def _():
        for ahead in range(XS_RING - 1):
            pl.when(ahead < n_items)(lambda: tile_copy(ahead).start())

    @pl.when(i + XS_RING - 1 < n_items)
    def _():
        tile_copy(i + XS_RING - 1).start()

    tile_copy(i).wait()
    xs_ref = xbuf_ref.at[lax.rem(i, XS_RING)]

    @pl.when((i == 0) | (exp_ref[i] != exp_ref[jnp.maximum(i - 1, 0)]))
    def _():
        wgb_ref[...] = wg_ref[...].astype(BF16)
        wub_ref[...] = wu_ref[...].astype(BF16)
        wdb_ref[...] = wd_ref[...].astype(BF16)

    def ffn():
        x_lo, x_hi = _unpack_halves(xs_ref[...])
        x_lo, x_hi = x_lo.astype(BF16), x_hi.astype(BF16)
        n = x_lo.shape[1]
        a = _dot(x_lo, wgb_ref[0:n, :]) + _dot(x_hi, wgb_ref[n:, :])
        u = _dot(x_lo, wub_ref[0:n, :]) + _dot(x_hi, wub_ref[n:, :])
        return _dot(((a * _sigmoid(a)) * u).astype(BF16), wdb_ref[...])

    whole = (lo == 0) & (hi == EXPERT_TILE)

    @pl.when(whole)
    def _():
        ys_ref[...] = _pack_halves(ffn())

    @pl.when(jnp.logical_not(whole) & (hi > lo))
    def _():
        y = ffn()
        row = lax.broadcasted_iota(jnp.int32, y.shape, 0)
        y = jnp.where((row >= lo) & (row < hi), y, 0.0)

        @pl.when(lo == 0)
        def _():
            acc_ref[...] = y

        @pl.when((lo > 0) & (hi < EXPERT_TILE))
        def _():
            acc_ref[...] += y

        @pl.when((lo > 0) & (hi == EXPERT_TILE))
        def _():
            ys_ref[...] = _pack_halves(acc_ref[...] + y)


def _experts(xs, items, layer, exp_wg, exp_wu, exp_wd):
    n_slots, half = xs.shape
    d, hid = exp_wg.shape[-2:]
    tile, expert, lo, hi = items
    grid_spec = pltpu.PrefetchScalarGridSpec(
        num_scalar_prefetch=4,
        grid=(tile.shape[0],),
        in_specs=[pl.BlockSpec(memory_space=pl.ANY),
                  pl.BlockSpec((None, None, d, hid), lambda i, t, e, lo, hi: (layer, e[i], 0, 0)),
                  pl.BlockSpec((None, None, d, hid), lambda i, t, e, lo, hi: (layer, e[i], 0, 0)),
                  pl.BlockSpec((None, None, hid, d), lambda i, t, e, lo, hi: (layer, e[i], 0, 0))],
        out_specs=pl.BlockSpec((EXPERT_TILE, half), lambda i, t, e, lo, hi: (t[i], 0)),
        scratch_shapes=[pltpu.VMEM((EXPERT_TILE, d), F32), pltpu.VMEM((d, hid), BF16),
                        pltpu.VMEM((d, hid), BF16), pltpu.VMEM((hid, d), BF16),
                        pltpu.VMEM((XS_RING, EXPERT_TILE, half), jnp.uint32),
                        pltpu.SemaphoreType.DMA((XS_RING,))])
    return pl.pallas_call(
        _experts_kernel,
        grid_spec=grid_spec,
        out_shape=jax.ShapeDtypeStruct((n_slots, half), jnp.uint32),
        compiler_params=_cparams(("arbitrary",)),
        name="experts",
    )(tile, expert, lo, hi, xs, exp_wg, exp_wu, exp_wd)


def _moe_out_kernel(yg_ref, w_ref, v_ref, sg_ref, su_ref, sd_ref, h1_ref, mod_ref, gpost_ref, *rest,
                    tile0, tiles_per_batch, n_batch, d):
    o_ref = rest[-1]
    i = tile0 + pl.program_id(0)
    x = v_ref[...]
    a = _dot(x, sg_ref[...])
    f = _dot(((a * _sigmoid(a)) * _dot(x, su_ref[...])).astype(BF16), sd_ref[...])
    n = d // 2
    f_lo, f_hi = f[:, :n], f[:, n:]
    w = w_ref[...]
    for k in range(TOP_K):
        y_lo, y_hi = _unpack_halves(yg_ref[k])
        wk = w[:, k:k + 1]
        f_lo = f_lo + wk * y_lo
        f_hi = f_hi + wk * y_hi
    f = jnp.concatenate([f_lo, f_hi], axis=1)
    bi = jnp.minimum(i // tiles_per_batch, n_batch)
    g2 = mod_ref[pl.ds(bi, 1), 5 * d:6 * d]
    o_ref[...] = h1_ref[...] + g2 * _rms(f, gpost_ref[...])


def _moe_out(yg, w, v, sg, su, sd, h1, mods, gpost, tile0, prev_out, *, n_batch, seq):
    n_rows, d = v.shape
    n_tiles = w.shape[0] // TM
    part = lambda i: (i, 0)
    rows = lambda i: (tile0 + i, 0)
    const = lambda i: (0, 0)
    in_specs = [pl.BlockSpec((TOP_K, TM, d // 2), lambda i: (0, i, 0)), pl.BlockSpec((TM, TOP_K), part),
                pl.BlockSpec((TM, d), rows),
                pl.BlockSpec(sg.shape, const), pl.BlockSpec(su.shape, const), pl.BlockSpec(sd.shape, const),
                pl.BlockSpec((TM, d), rows), pl.BlockSpec(mods.shape, const), pl.BlockSpec((1, d), const)]
    args = [yg, w, v, sg, su, sd, h1, mods, gpost]
    aliases = {}
    if prev_out is not None:
        in_specs.append(pl.BlockSpec(memory_space=pl.ANY))
        aliases = {len(args): 0}
        args.append(prev_out)
    return pl.pallas_call(
        functools.partial(_moe_out_kernel, tile0=tile0, tiles_per_batch=seq // TM, n_batch=n_batch, d=d),
        grid=(n_tiles,),
        in_specs=in_specs,
        out_specs=pl.BlockSpec((TM, d), rows),
        out_shape=jax.ShapeDtypeStruct((n_rows, d), F32),
        input_output_aliases=aliases,
        compiler_params=_cparams(("arbitrary",)),
        name="moe_out",
    )(*args)


MOE_PARTS = 1


def _moe(v, vp, layer, rw, rb, exp_wg, exp_wu, exp_wd, sg, su, sd, h1, mods, gpost, *, n_batch, seq):
    n_tiles = v.shape[0] // TM
    per_part = n_tiles // MOE_PARTS
    assert per_part * MOE_PARTS == n_tiles and (per_part * TM * TOP_K) % EXPERT_TILE == 0
    n_slots = per_part * TM * TOP_K
    routed = []
    for p in range(MOE_PARTS):
        eidx, rank, w_t, cnt = _router(v, rw, rb, p * per_part, per_part)
        routed.append((_slots(eidx, rank, cnt), w_t, cnt))
    xs = [_sc_dispatch(vp, slot_t, n_slots, p * per_part * TM) for p, (slot_t, _, _) in enumerate(routed)]
    ys = [_experts(x, _work_items(cnt, n_slots), layer, exp_wg, exp_wu, exp_wd)
          for x, (_, _, cnt) in zip(xs, routed)]
    out = None
    for p, (y, (slot_t, w_t, _)) in enumerate(zip(ys, routed)):
        yg = _sc_collect(y, slot_t)
        out = _moe_out(yg, w_t.T, v, sg, su, sd, h1, mods, gpost, p * per_part, out, n_batch=n_batch, seq=seq)
    return out


def _rope_tables(seq):
    rows = seq // GRID_W
    row_id = np.repeat(np.arange(rows, dtype=np.float64), GRID_W)
    col_id = np.tile(np.arange(GRID_W, dtype=np.float64), rows)

    def tables(rot_dim):
        axis_dim = rot_dim // 2
        inv_freq = ROPE_BASE ** (-np.arange(0, axis_dim, 2, dtype=np.float64) / axis_dim)
        ang_r = row_id[:, None] * inv_freq[None, :]
        ang_c = col_id[:, None] * inv_freq[None, :]
        cos = np.concatenate([np.cos(ang_r), np.cos(ang_r), np.cos(ang_c), np.cos(ang_c)], axis=1)
        sin = np.concatenate([-np.sin(ang_r), np.sin(ang_r), -np.sin(ang_c), np.sin(ang_c)], axis=1)
        return cos, sin

    cos64, sin64 = tables(GQA_DIM)
    cos32, sin32 = tables(MLA_ROPE)
    ones = np.ones((seq, MLA_NOPE))
    pad = LANES - MLA_NOPE - MLA_ROPE
    cospe = np.concatenate([ones, cos32, np.ones((seq, pad))], axis=1)
    sinpe = np.concatenate([0 * ones, sin32, np.zeros((seq, pad))], axis=1)
    tab = np.concatenate([cos64, cos64, sin64, sin64, cospe, sinpe], axis=1)
    ident = np.concatenate([np.ones((TM, LANES)), np.zeros((TM, LANES)),
                            np.ones((TM, LANES)), np.zeros((TM, LANES))], axis=1)
    return jnp.asarray(np.concatenate([tab, ident], axis=0), F32)


def _pack_w_in(w):
    d = w.shape[0]
    sizes = (MLA_Q_LORA, MLA_KV_LORA, MLA_ROPE, GQA_HEADS * GQA_DIM, GQA_KV_HEADS * GQA_DIM,
             GQA_KV_HEADS * GQA_DIM, RET_HEADS * RET_QK, RET_HEADS * RET_QK, RET_HEADS * RET_V,
             RET_HEADS * RET_V, 3 * d)
    offs, parts = 0, []
    for s in sizes:
        parts.append(w[:, offs:offs + s])
        offs += s
    cq, ckv, kpe, gq, gk, gv, rq, rk, rv, rg, gates = parts

    def twice(m):
        heads = [m[:, i * GQA_DIM:(i + 1) * GQA_DIM] for i in range(GQA_KV_HEADS)]
        return jnp.concatenate([hh for hd in heads for hh in (hd, hd)], axis=1)

    kpe_slab = jnp.concatenate([jnp.zeros((d, MLA_NOPE), F32), kpe,
                                jnp.zeros((d, LANES - MLA_NOPE - MLA_ROPE), F32)], axis=1)
    packed = jnp.concatenate([cq, ckv, kpe_slab, gq * (GQA_DIM ** -0.5 * LOG2_E), twice(gk), twice(gv),
                              rq, rk * RET_QK ** -0.5, rv, rg, gates], axis=1)
    assert packed.shape[1] == W_COLS
    return packed.astype(BF16)


def _pack_mla_up(w_uq, w_ukv):
    r = w_uq.shape[0]
    dq = MLA_NOPE + MLA_ROPE
    wq = jnp.pad(w_uq.reshape(r, MLA_HEADS, dq), ((0, 0), (0, 0), (0, LANES - dq))).reshape(r, MLA_HEADS * LANES)
    kv = w_ukv.reshape(r, MLA_HEADS, MLA_NOPE + MLA_V)
    wk = jnp.pad(kv[:, :, :MLA_NOPE], ((0, 0), (0, 0), (0, LANES - MLA_NOPE))).reshape(r, MLA_HEADS * LANES)
    wv = kv[:, :, MLA_NOPE:]
    zeros = jnp.zeros_like(wv)
    even = jnp.concatenate([wv, zeros], axis=2)
    odd = jnp.concatenate([zeros, wv], axis=2)
    wv = jnp.where((jnp.arange(MLA_HEADS) % 2 == 0)[None, :, None], even, odd).reshape(r, MLA_HEADS * LANES)
    return wq.astype(BF16), wk.astype(BF16), wv.astype(BF16)


def kernel(x, c, ctx, c_ctx, ada_w, ada_b, norm_mix_pre, norm_mix_post, norm_ffn_pre, norm_ffn_post, w_in, mla_q_norm, mla_w_uq, mla_kv_norm, mla_w_ukv, gqa_sink, ret_decay_fwd, ret_decay_bwd, w_br_mla, w_br_gqa, w_br_ret, w_out, router_w, router_bias, exp_w_gate, exp_w_up, exp_w_down, shared_w_gate, shared_w_up, shared_w_down):
    n_batch, seq, d = x.shape
    n_ctx = ctx.shape[1]
    depth = ada_w.shape[0]
    n_lat_rows = n_batch * seq
    assert seq % TM == 0 and (n_batch * n_ctx) % TM == 0 and seq % ATT_TQ == 0 and seq % n_ctx == 0
    assert n_batch < MOD_ROWS and seq % GRID_W == 0

    cond = jnp.zeros((MOD_ROWS, d), F32).at[:n_batch].set(c).at[n_batch].set(c_ctx)
    mods_all = _adaln(cond, ada_w, ada_b)
    rope = _rope_tables(seq)
    h = jnp.concatenate([x.reshape(n_lat_rows, d), ctx.reshape(n_batch * n_ctx, d)], axis=0)
    row = lambda p: p.reshape(1, -1)
    dims = dict(n_batch=n_batch, seq=seq)

    for l in range(depth):
        last = l == depth - 1
        mods = mods_all[l]
        wq, wk, wv = _pack_mla_up(mla_w_uq[l], mla_w_ukv[l])
        mq, mk, mv, gqa, ret, gates = _inproj(h, mods, row(norm_mix_pre[l]), _pack_w_in(w_in[l]), rope,
                                              row(mla_q_norm[l]), row(mla_kv_norm[l]), wq, wk, wv, **dims)
        a = _mla_attention(mq, mk, mv, ctx=n_ctx, with_ctx_queries=not last, **dims)
        sink_tab = jnp.broadcast_to(gqa_sink[l].astype(F32)[:, None] * LOG2_E, (GQA_HEADS, LANES))
        w = _window_attention(gqa, sink_tab, ctx=n_ctx, with_ctx_queries=not last, **dims)
        lg = jnp.concatenate([jax.nn.log_sigmoid(ret_decay_fwd[l].astype(F32)),
                              jax.nn.log_sigmoid(ret_decay_bwd[l].astype(F32))])
        o_f, o_b = _retention(ret, jnp.broadcast_to(lg[:, None], (2 * RET_HEADS, LANES)), ctx=n_ctx, **dims)
        n_rows = n_lat_rows if last else h.shape[0]
        h1, v, vp = _merge(a, w, o_f, o_b, ret, gates, h, mods, row(norm_mix_post[l]), row(norm_ffn_pre[l]),
                           w_br_mla[l].astype(BF16), w_br_gqa[l].astype(BF16), w_br_ret[l].astype(BF16),
                           w_out[l].astype(BF16), n_rows=n_rows, **dims)
        h = _moe(v, vp, l, router_w[l].T.astype(BF16), router_bias[l].astype(F32).reshape(-1, 1),
                 exp_w_gate, exp_w_up, exp_w_down, shared_w_gate[l].astype(BF16),
                 shared_w_up[l].astype(BF16), shared_w_down[l].astype(BF16), h1, mods,
                 row(norm_ffn_post[l]), **dims)
    return h[:n_lat_rows].reshape(n_batch, seq, d)
```

```python
import functools

import numpy as np
import jax
import jax.numpy as jnp
from jax import lax
from jax.experimental import pallas as pl
from jax.experimental.pallas import tpu as pltpu
from jax.experimental.pallas import tpu_sc as plsc

F32 = jnp.float32
BF16 = jnp.bfloat16

GRID_W = 64
ROPE_BASE = 10000.0
NORM_EPS = 1e-6
NEG_INF = -1e30
LOG2_E = 1.4426950408889634
N_MOD = 6
MLA_HEADS, MLA_NOPE, MLA_ROPE, MLA_V = 8, 64, 32, 64
MLA_Q_LORA, MLA_KV_LORA = 256, 256
GQA_HEADS, GQA_KV_HEADS, GQA_DIM, WINDOW = 8, 2, 64, 128
RET_HEADS, RET_QK, RET_V, RET_CHUNK = 4, 64, 128, 128
N_EXPERTS, N_EXPERT_GROUPS, TOPK_GROUPS, TOP_K = 64, 8, 4, 8
EXPERTS_PER_GROUP = N_EXPERTS // N_EXPERT_GROUPS
ROUTED_SCALE = 2.5

LANES = 128
TM = 512
ATT_TQ = 512
WIN_TQ = 256
RET_TILE = 256
MOD_ROWS = 8
V7X_VMEM_LIMIT = 56 * 1024 * 1024

C_CQ, C_CKV, C_KPE = 0, 256, 512
C_G = 640
C_R = 1664
C_GATE = 3200
W_COLS = 6272


def _cparams(sem):
    return pltpu.CompilerParams(dimension_semantics=sem, vmem_limit_bytes=V7X_VMEM_LIMIT)


def _rms(x, g):
    return x * lax.rsqrt(jnp.mean(x * x, axis=-1, keepdims=True) + NORM_EPS) * g


def _sigmoid(x):
    return 0.5 * jnp.tanh(0.5 * x) + 0.5


def _dot(a, b):
    return jnp.dot(a, b, preferred_element_type=F32)


def _dot_nt(a, b):
    return lax.dot_general(a, b, (((1,), (1,)), ((), ())), preferred_element_type=F32)


def _dot_tn(a, b):
    return lax.dot_general(a, b, (((0,), (0,)), ((), ())), preferred_element_type=F32)


def _rope(x, cos, sin, half):
    n = x.shape[-1]
    reps = n // LANES
    if reps > 1:
        cos = jnp.concatenate([cos] * reps, axis=1)
        sin = jnp.concatenate([sin] * reps, axis=1)
    lane = lax.broadcasted_iota(jnp.int32, x.shape, 1)
    up = pltpu.roll(x, half, 1)
    dn = pltpu.roll(x, n - half, 1)
    partner = jnp.where((lane & (2 * half - 1)) < half, dn, up)
    return x * cos + partner * sin


def _lane_lo(shape):
    return (lax.broadcasted_iota(jnp.int32, shape, 1) & (LANES - 1)) < (LANES // 2)


def _pack_halves(x):
    n = x.shape[1] // 2
    bits = lambda t: lax.bitcast_convert_type(t.astype(BF16).astype(F32), jnp.uint32)
    return (bits(x[:, :n]) >> 16) | bits(x[:, n:])


def _unpack_halves(p):
    lo = lax.bitcast_convert_type(p << 16, F32)
    hi = lax.bitcast_convert_type(p & jnp.uint32(0xFFFF0000), F32)
    return lo, hi


def _ada_kernel(c_ref, w_ref, b_ref, o_ref):
    c = c_ref[...]
    s = c * _sigmoid(c)
    o_ref[...] = _dot(s.astype(BF16), w_ref[...].astype(BF16)) + b_ref[...]


def _adaln(cond, ada_w, ada_b):
    n_layers, d, n = ada_w.shape
    tn = 1024
    return pl.pallas_call(
        _ada_kernel,
        grid=(n_layers, n // tn),
        in_specs=[pl.BlockSpec((MOD_ROWS, d), lambda l, j: (0, 0)),
                  pl.BlockSpec((None, d, tn), lambda l, j: (l, 0, j)),
                  pl.BlockSpec((None, 1, tn), lambda l, j: (l, 0, j))],
        out_specs=pl.BlockSpec((None, MOD_ROWS, tn), lambda l, j: (l, 0, j)),
        out_shape=jax.ShapeDtypeStruct((n_layers, MOD_ROWS, n), F32),
        compiler_params=_cparams(("arbitrary", "arbitrary")),
        name="adaln",
    )(cond, ada_w, ada_b.reshape(n_layers, 1, n))


def _inproj_tile(h, mod_ref, gpre_ref, w_ref, rope_ref, qn_ref, kvn_ref, wuq_ref, wuk_ref, wuv_ref,
                 mq_ref, mk_ref, mv_ref, gqa_ref, ret_ref, gate_ref, *, tiles_per_batch, n_batch, d):
    i = pl.program_id(0)
    bi = jnp.minimum(i // tiles_per_batch, n_batch)
    sh = mod_ref[pl.ds(bi, 1), 0:d]
    sc = mod_ref[pl.ds(bi, 1), d:2 * d]
    u = (_rms(h, gpre_ref[...]) * (1.0 + sc) + sh).astype(BF16)

    cos64 = rope_ref[:, 0:LANES]
    sin64 = rope_ref[:, LANES:2 * LANES]
    cospe = rope_ref[:, 2 * LANES:3 * LANES]
    sinpe = rope_ref[:, 3 * LANES:4 * LANES]

    c = _dot(u, w_ref[:, C_CQ:C_G])
    kpe = _rope(c[:, C_KPE:C_G], cospe, sinpe, MLA_ROPE // 4)
    qn = _rms(c[:, C_CQ:C_CKV], qn_ref[...]).astype(BF16)
    q = _rope(_dot(qn, wuq_ref[...]), cospe, sinpe, MLA_ROPE // 4)
    mq_ref[...] = (q * ((MLA_NOPE + MLA_ROPE) ** -0.5 * LOG2_E)).astype(mq_ref.dtype)
    kvn = _rms(c[:, C_CKV:C_KPE], kvn_ref[...]).astype(BF16)
    k = _dot(kvn, wuk_ref[...]) + jnp.concatenate([kpe] * MLA_HEADS, axis=1)
    mk_ref[...] = k.astype(mk_ref.dtype)
    v = _dot(kvn, wuv_ref[...])
    lane = lax.broadcasted_iota(jnp.int32, v.shape, 1)
    value_lane = ((lane & (LANES - 1)) < MLA_V) == (((lane >> (LANES.bit_length() - 1)) & 1) == 0)
    mv_ref[...] = jnp.where(value_lane, v, 1.0).astype(mv_ref.dtype)

    g = _dot(u, w_ref[:, C_G:C_R])
    n_qk = GQA_HEADS * GQA_DIM + 2 * GQA_KV_HEADS * GQA_DIM
    gqa_ref[:, 0:n_qk] = _rope(g[:, 0:n_qk], cos64, sin64, GQA_DIM // 4).astype(gqa_ref.dtype)
    gqa_ref[:, n_qk:] = g[:, n_qk:].astype(gqa_ref.dtype)

    r = _dot(u, w_ref[:, C_R:C_GATE])
    n_qk = 2 * RET_HEADS * RET_QK
    ret_ref[:, 0:n_qk] = _rope(r[:, 0:n_qk], cos64, sin64, RET_QK // 4)
    ret_ref[:, n_qk:] = r[:, n_qk:]

    gate_ref[...] = _dot(u, w_ref[:, C_GATE:W_COLS]).astype(gate_ref.dtype)


def _inproj_kernel(h_ref, *refs, **kw):
    _inproj_tile(h_ref[...], *refs, **kw)


def _inproj_specs(t, d, mods, gpre, w_all, rope, qn, kvn, wuq, wuk, wuv, *, tm, n_batch, seq):
    tiles_per_batch = seq // tm
    n_lat_tiles = n_batch * tiles_per_batch
    const = lambda i: (0, 0)
    rows = lambda i: (i, 0)
    rope_idx = lambda i: (jnp.where(i < n_lat_tiles, i % tiles_per_batch, tiles_per_batch), 0)
    once = dict(pipeline_mode=pl.Buffered(1))
    hq = MLA_HEADS * LANES
    outs = [jax.ShapeDtypeStruct((t, hq), BF16), jax.ShapeDtypeStruct((t, hq), BF16),
            jax.ShapeDtypeStruct((t, hq), BF16),
            jax.ShapeDtypeStruct((t, C_R - C_G), BF16),
            jax.ShapeDtypeStruct((t, C_GATE - C_R), F32),
            jax.ShapeDtypeStruct((t, W_COLS - C_GATE), BF16)]
    in_specs = [pl.BlockSpec(mods.shape, const),
                pl.BlockSpec((1, d), const),
                pl.BlockSpec(w_all.shape, const, **once),
                pl.BlockSpec((tm, 4 * LANES), rope_idx),
                pl.BlockSpec(qn.shape, const), pl.BlockSpec(kvn.shape, const),
                pl.BlockSpec(wuq.shape, const, **once), pl.BlockSpec(wuk.shape, const, **once),
                pl.BlockSpec(wuv.shape, const, **once)]
    return in_specs, [pl.BlockSpec((tm, o.shape[1]), rows) for o in outs], outs


def _inproj(h, inproj_args, *, n_batch, seq):
    t, d = h.shape
    in_specs, out_specs, out_shape = _inproj_specs(t, d, *inproj_args, tm=TM, n_batch=n_batch, seq=seq)
    return pl.pallas_call(
        functools.partial(_inproj_kernel, tiles_per_batch=seq // TM, n_batch=n_batch, d=d),
        grid=(t // TM,),
        in_specs=[pl.BlockSpec((TM, d), lambda i: (i, 0))] + in_specs,
        out_specs=out_specs,
        out_shape=out_shape,
        compiler_params=_cparams(("arbitrary",)),
        name="inproj",
    )(h, *inproj_args)


MLA_HEADS_PER_STEP = 4


def _mla_kernel(q_ref, kl_ref, kc_ref, vl_ref, vc_ref, o_ref, s_ref, p_ref, *, with_lat):
    n_ctx = kc_ref.shape[0]

    def body(with_lat):
        n_keys = n_ctx + (kl_ref.shape[0] if with_lat else 0)

        def scores(h):
            sl = slice(h * LANES, (h + 1) * LANES)
            s_ref[h % 2, :, 0:n_ctx] = _dot_nt(q_ref[:, sl], kc_ref[:, sl])
            if with_lat:
                s_ref[h % 2, :, n_ctx:n_keys] = _dot_nt(q_ref[:, sl], kl_ref[:, sl])

        def probs(h):
            s = s_ref[h % 2, :, 0:n_keys]
            p_ref[h % 2, :, 0:n_keys] = jnp.exp2(s - jnp.max(s, axis=-1, keepdims=True)).astype(BF16)

        def weighted(h):
            sl = slice(h * LANES, (h + 1) * LANES)
            o = _dot(p_ref[h % 2, :, 0:n_ctx], vc_ref[:, sl])
            if with_lat:
                o = o + _dot(p_ref[h % 2, :, n_ctx:n_keys], vl_ref[:, sl])
            return o / pltpu.roll(o, LANES // 2, 1)

        outs = [None] * MLA_HEADS_PER_STEP
        scores(0)
        for h in range(MLA_HEADS_PER_STEP):
            if h + 1 < MLA_HEADS_PER_STEP:
                scores(h + 1)
            probs(h)
            outs[h] = weighted(h)
        for pr in range(MLA_HEADS_PER_STEP // 2):
            even, odd = outs[2 * pr], outs[2 * pr + 1]
            o_ref[:, pr * LANES:(pr + 1) * LANES] = jnp.where(_lane_lo(even.shape), even, odd).astype(o_ref.dtype)

    body(with_lat)


def _mla_call(mq, mk, mv, *, n_batch, seq, ctx, latent_queries):
    hps = MLA_HEADS_PER_STEP
    ctx_blk0 = n_batch * seq // ctx
    tq = ATT_TQ if latent_queries else ctx
    nq = seq // tq if latent_queries else 1
    q_blk0 = 0 if latent_queries else n_batch * seq // tq
    q_idx = lambda b, g, i: (q_blk0 + b * nq + i, g)
    ctx_idx = lambda b, g, i: (ctx_blk0 + b, g)
    lat_idx = (lambda b, g, i: (b, g)) if latent_queries else ctx_idx
    n_lat = seq if latent_queries else ctx
    n_keys = ctx + (seq if latent_queries else 0)
    in_specs = [pl.BlockSpec((tq, hps * LANES), q_idx),
                pl.BlockSpec((n_lat, hps * LANES), lat_idx), pl.BlockSpec((ctx, hps * LANES), ctx_idx),
                pl.BlockSpec((n_lat, hps * LANES), lat_idx), pl.BlockSpec((ctx, hps * LANES), ctx_idx)]
    return pl.pallas_call(
        functools.partial(_mla_kernel, with_lat=latent_queries),
        grid=(n_batch, MLA_HEADS // hps, nq),
        in_specs=in_specs,
        out_specs=pl.BlockSpec((tq, hps * MLA_V), lambda b, g, i: (b * nq + i, g)),
        out_shape=jax.ShapeDtypeStruct((n_batch * nq * tq, MLA_HEADS * MLA_V), BF16),
        scratch_shapes=[pltpu.VMEM((2, tq, n_keys), F32), pltpu.VMEM((2, tq, n_keys), BF16)],
        compiler_params=_cparams(("arbitrary", "arbitrary", "arbitrary")),
        name="mla_attn" if latent_queries else "mla_attn_ctx",
    )(mq, mk, mk, mv, mv)


def _mla_attention(mq, mk, mv, *, n_batch, seq, ctx, with_ctx_queries):
    dims = dict(n_batch=n_batch, seq=seq, ctx=ctx)
    lat = _mla_call(mq, mk, mv, latent_queries=True, **dims)
    return lat, (_mla_call(mq, mk, mv, latent_queries=False, **dims) if with_ctx_queries else lat)


def _win_kernel(q_ref, kp_ref, kcur_ref, kn_ref, vp_ref, vcur_ref, vn_ref, kc_ref, vc_ref, sink_ref, o_ref,
                s_ref, p_ref, *, seq, with_lat):
    i = pl.program_id(1)
    tq = q_ref.shape[0]
    group = GQA_HEADS // GQA_KV_HEADS

    def body(with_lat):
        n_ctx = kc_ref.shape[0]
        n_keys = n_ctx + (tq + 2 * WINDOW if with_lat else 0)
        if with_lat:
            q_pos = i * tq + lax.broadcasted_iota(jnp.int32, (tq, n_keys), 0)
            k_pos = i * tq - WINDOW - n_ctx + lax.broadcasted_iota(jnp.int32, (tq, n_keys), 1)
            in_band = (jnp.abs(q_pos - k_pos) <= WINDOW) & (k_pos >= 0) & (k_pos < seq)
            valid = in_band | (lax.broadcasted_iota(jnp.int32, (tq, n_keys), 1) < n_ctx)
        lo = _lane_lo((tq, LANES))
        lo_k = _lane_lo((n_keys, LANES))
        keys, values = [], []
        for kv in range(GQA_KV_HEADS):
            sl = slice(kv * LANES, (kv + 1) * LANES)
            if with_lat:
                k_all = jnp.concatenate([kc_ref[:, sl], kp_ref[:, sl], kcur_ref[:, sl], kn_ref[:, sl]], axis=0)
                v_all = jnp.concatenate([vc_ref[:, sl], vp_ref[:, sl], vcur_ref[:, sl], vn_ref[:, sl]], axis=0)
            else:
                k_all, v_all = kc_ref[:, sl], vc_ref[:, sl]
            keys.append(k_all)
            one = jnp.ones_like(v_all)
            values.append((jnp.where(lo_k, v_all, one), jnp.where(lo_k, one, v_all)))

        def scores(hd):
            kv, pair = hd // group, hd // 2
            qp = q_ref[:, pair * LANES:(pair + 1) * LANES]
            qm = jnp.where(lo if hd % 2 == 0 else jnp.logical_not(lo), qp, jnp.zeros_like(qp))
            s = _dot_nt(qm, keys[kv])
            s_ref[hd % 2, :, 0:n_keys] = jnp.where(valid, s, NEG_INF) if with_lat else s

        def probs(hd):
            s = s_ref[hd % 2, :, 0:n_keys]
            m = jnp.maximum(jnp.max(s, axis=-1, keepdims=True), sink_ref[hd:hd + 1, 0:1])
            p_ref[hd % 2, :, 0:n_keys] = jnp.exp2(s - m).astype(BF16)
            return jnp.exp2(sink_ref[hd:hd + 1, 0:1] - m)

        def weighted(hd, sink_term):
            o = _dot(p_ref[hd % 2, :, 0:n_keys], values[hd // group][hd % 2])
            return o / (pltpu.roll(o, LANES // 2, 1) + sink_term)

        outs = [None] * GQA_HEADS
        scores(0)
        for hd in range(GQA_HEADS):
            if hd + 1 < GQA_HEADS:
                scores(hd + 1)
            outs[hd] = weighted(hd, probs(hd))
        for pair in range(GQA_HEADS // 2):
            o_ref[:, pair * LANES:(pair + 1) * LANES] = jnp.where(
                lo, outs[2 * pair], outs[2 * pair + 1]).astype(o_ref.dtype)

    body(with_lat)


def _win_call(gqa, sink_tab, *, n_batch, seq, ctx, latent_queries):
    tq = WIN_TQ if latent_queries else ctx
    nq = seq // tq if latent_queries else 1
    q_blk0 = 0 if latent_queries else n_batch * seq // tq
    per_tile = tq // WINDOW
    n_win_blocks = seq // WINDOW
    ctx_blk0 = n_batch * seq // ctx
    nqk = GQA_HEADS * GQA_DIM
    kw = 2 * GQA_KV_HEADS * GQA_DIM
    k_col, v_col = nqk // kw, nqk // kw + 1
    q_idx = lambda b, i: (q_blk0 + b * nq + i, 0)
    cidx = lambda col: (lambda b, i: (ctx_blk0 + b, col))
    if latent_queries:
        cur = lambda col: (lambda b, i: (b * nq + i, col))
        prev = lambda col: (lambda b, i: (b * n_win_blocks + jnp.maximum(per_tile * i - 1, 0), col))
        nxt = lambda col: (lambda b, i: (b * n_win_blocks + jnp.minimum(per_tile * (i + 1), n_win_blocks - 1), col))
        band = [((WINDOW, kw), prev), ((tq, kw), cur), ((WINDOW, kw), nxt)]
    else:
        band = [((ctx, kw), cidx)] * 3
    n_keys = ctx + (tq + 2 * WINDOW if latent_queries else 0)
    in_specs = ([pl.BlockSpec((tq, nqk), q_idx)]
                + [pl.BlockSpec(shape, idx(k_col)) for shape, idx in band]
                + [pl.BlockSpec(shape, idx(v_col)) for shape, idx in band]
                + [pl.BlockSpec((ctx, kw), cidx(k_col)), pl.BlockSpec((ctx, kw), cidx(v_col)),
                   pl.BlockSpec(sink_tab.shape, lambda b, i: (0, 0))])
    return pl.pallas_call(
        functools.partial(_win_kernel, seq=seq, with_lat=latent_queries),
        grid=(n_batch, nq),
        in_specs=in_specs,
        out_specs=pl.BlockSpec((tq, nqk), lambda b, i: (b * nq + i, 0)),
        out_shape=jax.ShapeDtypeStruct((n_batch * nq * tq, nqk), BF16),
        scratch_shapes=[pltpu.VMEM((2, tq, n_keys), F32), pltpu.VMEM((2, tq, n_keys), BF16)],
        compiler_params=_cparams(("arbitrary", "arbitrary")),
        name="win_attn" if latent_queries else "win_attn_ctx",
    )(*([gqa] * 9 + [sink_tab]))


def _window_attention(gqa, sink_tab, *, n_batch, seq, ctx, with_ctx_queries):
    dims = dict(n_batch=n_batch, seq=seq, ctx=ctx)
    lat = _win_call(gqa, sink_tab, latent_queries=True, **dims)
    return lat, (_win_call(gqa, sink_tab, latent_queries=False, **dims) if with_ctx_queries else lat)


def _ret_kernel(f_ref, b_ref, lg_ref, of_ref, ob_ref, sf_ref, sb_ref, qdec_ref, kdec_ref, cdec_ref, inner_ref):
    @pl.when(pl.program_id(1) == 0)
    def _():
        sf_ref[...] = jnp.zeros_like(sf_ref)
        sb_ref[...] = jnp.zeros_like(sb_ref)

    L = f_ref.shape[0]
    lo = _lane_lo((L, LANES))
    srow_lo = lax.broadcasted_iota(jnp.int32, (LANES, LANES), 0) < RET_QK
    nq = RET_HEADS * RET_QK
    n_pairs = RET_HEADS // 2

    @pl.when(pl.program_id(1) == 0)
    def _():
        ii = lax.broadcasted_iota(jnp.int32, (L, L), 0)
        jj = lax.broadcasted_iota(jnp.int32, (L, L), 1)
        row = lax.broadcasted_iota(jnp.int32, (L, LANES), 0).astype(F32)
        for direction, forward in enumerate((True, False)):
            dist = ii - jj if forward else jj - ii
            distf = jnp.maximum(dist, 0).astype(F32)
            for pr in range(n_pairs):
                r0 = direction * RET_HEADS + 2 * pr
                lg = [lg_ref[r0 + e:r0 + e + 1, :] for e in range(2)]
                lg_lane = jnp.where(lo, lg[0], lg[1])
                qdec_ref[direction * n_pairs + pr] = jnp.exp(lg_lane * ((row + 1.0) if forward else (L - row)))
                kdec_ref[direction * n_pairs + pr] = jnp.exp(lg_lane * ((L - 1.0 - row) if forward else row))
                cdec_ref[direction * n_pairs + pr] = jnp.where(srow_lo, jnp.exp(lg[0] * float(L)),
                                                               jnp.exp(lg[1] * float(L)))
                for e in range(2):
                    inner_ref[r0 + e] = jnp.where(dist >= 0, jnp.exp(lg[e][:, 0:1] * distf), 0.0)

    def scan_chunk(x_ref, o_ref, s_ref, direction):
        for pr in range(n_pairs):
            q = x_ref[:, pr * LANES:(pr + 1) * LANES]
            k = x_ref[:, nq + pr * LANES:nq + (pr + 1) * LANES]
            qd = q * qdec_ref[direction * n_pairs + pr]
            kdb = (k * kdec_ref[direction * n_pairs + pr]).astype(BF16)
            kb = k.astype(BF16)
            state = s_ref[pr]
            state_b = state.astype(BF16)
            upd = []
            for e in range(2):
                hd = 2 * pr + e
                keep = lo if e == 0 else jnp.logical_not(lo)
                v = x_ref[:, 2 * nq + hd * RET_V:2 * nq + (hd + 1) * RET_V].astype(BF16)
                attn = _dot_nt(jnp.where(keep, q, 0.0).astype(BF16), kb) * inner_ref[direction * RET_HEADS + hd]
                o = _dot(attn.astype(BF16), v) + _dot(jnp.where(keep, qd, 0.0).astype(BF16), state_b)
                o_ref[:, hd * RET_V:(hd + 1) * RET_V] = o
                upd.append(_dot_tn(kdb, v))
            s_ref[pr] = state * cdec_ref[direction * n_pairs + pr] + jnp.where(srow_lo, upd[0], upd[1])

    scan_chunk(f_ref, of_ref, sf_ref, 0)
    scan_chunk(b_ref, ob_ref, sb_ref, 1)


def _retention(ret, lg_tab, *, n_batch, seq, ctx):
    t = ret.shape[0]
    L = RET_TILE
    assert seq % L == 0 and ctx % L == 0
    n_lat, n_ctx = seq // L, ctx // L
    ctx0 = n_batch * n_lat
    width = 2 * RET_HEADS * RET_QK + RET_HEADS * RET_V
    fwd = lambda b, s: (jnp.where(s < n_ctx, ctx0 + b * n_ctx + s, b * n_lat + s - n_ctx), 0)
    bwd = lambda b, s: (jnp.where(s < n_ctx, ctx0 + b * n_ctx + n_ctx - 1 - s, b * n_lat + n_lat - 1 - (s - n_ctx)), 0)
    out = jax.ShapeDtypeStruct((t, RET_HEADS * RET_V), F32)
    return pl.pallas_call(
        _ret_kernel,
        grid=(n_batch, n_lat + n_ctx),
        in_specs=[pl.BlockSpec((L, width), fwd), pl.BlockSpec((L, width), bwd),
                  pl.BlockSpec(lg_tab.shape, lambda b, s: (0, 0))],
        out_specs=[pl.BlockSpec((L, RET_HEADS * RET_V), fwd), pl.BlockSpec((L, RET_HEADS * RET_V), bwd)],
        out_shape=[out, out],
        scratch_shapes=[pltpu.VMEM((RET_HEADS // 2, LANES, RET_V), F32),
                        pltpu.VMEM((RET_HEADS // 2, LANES, RET_V), F32),
                        pltpu.VMEM((RET_HEADS, L, LANES), F32), pltpu.VMEM((RET_HEADS, L, LANES), F32),
                        pltpu.VMEM((RET_HEADS, LANES, RET_V), F32), pltpu.VMEM((2 * RET_HEADS, L, L), F32)],
        compiler_params=_cparams(("arbitrary", "arbitrary")),
        name="retention",
    )(ret, ret, lg_tab)


def _merge_kernel(al_ref, ac_ref, wl_ref, wc_ref, of_ref, ob_ref, rg_ref, gt_ref, h_ref, mod_ref, gpost_ref,
                  gffn_ref, wa_ref, ww_ref, wr_ref, wo_ref, h1_ref, v_ref, vp_ref,
                  *, tiles_per_batch, n_batch, d):
    i = pl.program_id(0)
    bi = jnp.minimum(i // tiles_per_batch, n_batch)
    latent = i < tiles_per_batch * n_batch
    a_tile = jnp.where(latent, al_ref[...], ac_ref[...])
    w_tile = jnp.where(latent, wl_ref[...], wc_ref[...])
    o = of_ref[...] + ob_ref[...]
    normed = []
    for hd in range(RET_HEADS):
        oh = o[:, hd * RET_V:(hd + 1) * RET_V]
        dev = oh - jnp.mean(oh, axis=-1, keepdims=True)
        normed.append(dev * lax.rsqrt(jnp.mean(dev * dev, axis=-1, keepdims=True) + NORM_EPS))
    g = rg_ref[...]
    r = (g * _sigmoid(g)) * jnp.concatenate(normed, axis=1)
    y = (_sigmoid(gt_ref[:, 0:d].astype(F32)) * _dot(a_tile, wa_ref[...])
         + _sigmoid(gt_ref[:, d:2 * d].astype(F32)) * _dot(w_tile, ww_ref[...])
         + _sigmoid(gt_ref[:, 2 * d:3 * d].astype(F32)) * _dot(r.astype(BF16), wr_ref[...]))
    z = _dot(y.astype(BF16), wo_ref[...])
    g1 = mod_ref[pl.ds(bi, 1), 2 * d:3 * d]
    sh2 = mod_ref[pl.ds(bi, 1), 3 * d:4 * d]
    sc2 = mod_ref[pl.ds(bi, 1), 4 * d:5 * d]
    h1 = h_ref[...] + g1 * _rms(z, gpost_ref[...])
    h1_ref[...] = h1
    v = _rms(h1, gffn_ref[...]) * (1.0 + sc2) + sh2
    v_ref[...] = v.astype(v_ref.dtype)
    vp_ref[...] = _pack_halves(v)


def _merge(a, w, o_f, o_b, ret, gates, h, mods, gpost, gffn, wa, ww, wr, wo, *, n_rows, n_batch, seq):
    d = h.shape[1]
    n_lat_tiles = n_batch * seq // TM
    rows = lambda i: (i, 0)
    lat_rows = lambda i: (jnp.minimum(i, n_lat_tiles - 1), 0)
    ctx_rows = lambda i: (jnp.maximum(i - n_lat_tiles, 0), 0)
    const = lambda i: (0, 0)
    rv = RET_HEADS * RET_V
    rg_col = (2 * RET_HEADS * RET_QK + rv) // rv
    outs = [jax.ShapeDtypeStruct((n_rows, d), F32), jax.ShapeDtypeStruct((n_rows, d), BF16),
            jax.ShapeDtypeStruct((n_rows, d // 2), jnp.uint32)]
    return pl.pallas_call(
        functools.partial(_merge_kernel, tiles_per_batch=seq // TM, n_batch=n_batch, d=d),
        grid=(n_rows // TM,),
        in_specs=[pl.BlockSpec((TM, a[0].shape[1]), lat_rows), pl.BlockSpec((TM, a[1].shape[1]), ctx_rows),
                  pl.BlockSpec((TM, w[0].shape[1]), lat_rows), pl.BlockSpec((TM, w[1].shape[1]), ctx_rows),
                  pl.BlockSpec((TM, rv), rows), pl.BlockSpec((TM, rv), rows),
                  pl.BlockSpec((TM, rv), lambda i: (i, rg_col)),
                  pl.BlockSpec((TM, 3 * d), rows), pl.BlockSpec((TM, d), rows),
                  pl.BlockSpec(mods.shape, const), pl.BlockSpec((1, d), const), pl.BlockSpec((1, d), const),
                  pl.BlockSpec(wa.shape, const), pl.BlockSpec(ww.shape, const),
                  pl.BlockSpec(wr.shape, const), pl.BlockSpec(wo.shape, const)],
        out_specs=[pl.BlockSpec((TM, o.shape[1]), rows) for o in outs],
        out_shape=outs,
        compiler_params=_cparams(("arbitrary",)),
        name="merge",
    )(a[0], a[1], w[0], w[1], o_f, o_b, ret, gates, h, mods, gpost, gffn, wa, ww, wr, wo)


def _router_kernel(v_ref, rw_ref, rb_ref, eidx_ref, rank_ref, w_ref, cnt_ref, carry_ref):
    @pl.when(pl.program_id(0) == 0)
    def _():
        carry_ref[...] = jnp.zeros_like(carry_ref)

    tm = v_ref.shape[0]
    scores = _sigmoid(_dot_nt(rw_ref[...], v_ref[...]))
    sel = scores + rb_ref[...]
    neg = -jnp.inf
    n_grp, per = N_EXPERT_GROUPS, EXPERTS_PER_GROUP

    sel3 = sel.reshape(n_grp, per, tm)
    member_id = lax.broadcasted_iota(jnp.int32, sel3.shape, 1)
    m1 = jnp.max(sel3, axis=1, keepdims=True)
    i1 = jnp.min(jnp.where(sel3 == m1, member_id, per), axis=1, keepdims=True)
    m2 = jnp.max(jnp.where(member_id == i1, neg, sel3), axis=1, keepdims=True)
    gscore = (m1 + m2).reshape(n_grp, tm)
    gid = lax.broadcasted_iota(jnp.int32, gscore.shape, 0)
    ahead = jnp.zeros(gscore.shape, jnp.int32)
    for gj in range(n_grp):
        other = gscore[gj:gj + 1, :]
        ahead = ahead + jnp.where((other > gscore) | ((other == gscore) & (gid > gj)), 1, 0)
    group_ok = (ahead < TOPK_GROUPS).reshape(n_grp, 1, tm)
    sel = jnp.where(group_ok, sel3, NEG_INF).reshape(N_EXPERTS, tm)

    eid = lax.broadcasted_iota(jnp.int32, sel.shape, 0)
    chosen = jnp.zeros(sel.shape, jnp.bool_)
    picks = []
    for _ in range(TOP_K):
        m = jnp.max(sel, axis=0, keepdims=True)
        idx = jnp.min(jnp.where(sel == m, eid, N_EXPERTS), axis=0, keepdims=True)
        hit = eid == idx
        chosen = chosen | hit
        sel = jnp.where(hit, neg, sel)
        picks.append(idx)
    w = jnp.where(chosen, scores, 0.0)
    gate = ROUTED_SCALE * w / jnp.sum(w, axis=0, keepdims=True)

    member = jnp.where(chosen, 1.0, 0.0)
    earlier = lax.broadcasted_iota(jnp.int32, (tm, tm), 0) < lax.broadcasted_iota(jnp.int32, (tm, tm), 1)
    pos = _dot(member.astype(BF16), jnp.where(earlier, 1.0, 0.0).astype(BF16)) + carry_ref[...]
    for k, idx in enumerate(picks):
        hit = eid == idx
        eidx_ref[k:k + 1, :] = idx
        rank_ref[k:k + 1, :] = jnp.sum(jnp.where(hit, pos, 0.0), axis=0, keepdims=True)
        w_ref[k:k + 1, :] = jnp.sum(jnp.where(hit, gate, 0.0), axis=0, keepdims=True)
    carry_ref[...] += jnp.sum(member, axis=1, keepdims=True)
    cnt_ref[...] = carry_ref[...]


def _router(v, rw_t, rb, tile0, n_tiles):
    d = v.shape[1]
    n_rows = n_tiles * TM
    cols = lambda i: (0, i)
    const = lambda i: (0, 0)
    outs = [jax.ShapeDtypeStruct((TOP_K, n_rows), jnp.int32), jax.ShapeDtypeStruct((TOP_K, n_rows), F32),
            jax.ShapeDtypeStruct((TOP_K, n_rows), F32), jax.ShapeDtypeStruct((N_EXPERTS, 1), F32)]
    return pl.pallas_call(
        _router_kernel,
        grid=(n_tiles,),
        in_specs=[pl.BlockSpec((TM, d), lambda i: (tile0 + i, 0)), pl.BlockSpec(rw_t.shape, const),
                  pl.BlockSpec(rb.shape, const)],
        out_specs=[pl.BlockSpec((TOP_K, TM), cols), pl.BlockSpec((TOP_K, TM), cols),
                   pl.BlockSpec((TOP_K, TM), cols), pl.BlockSpec((N_EXPERTS, 1), const)],
        out_shape=outs,
        scratch_shapes=[pltpu.VMEM((N_EXPERTS, 1), F32)],
        compiler_params=_cparams(("arbitrary",)),
        name="router",
    )(v, rw_t, rb)


def _slots_kernel(eidx_ref, rank_ref, cnt_ref, slot_ref):
    tm = eidx_ref.shape[1]
    eid = lax.broadcasted_iota(jnp.int32, (N_EXPERTS, tm), 0)
    for k in range(TOP_K):
        before = jnp.sum(jnp.where(eid < eidx_ref[k:k + 1, :], cnt_ref[...], 0.0), axis=0, keepdims=True)
        slot_ref[k:k + 1, :] = (before + rank_ref[k:k + 1, :]).astype(jnp.int32)


def _slots(eidx, rank, cnt):
    n_rows = eidx.shape[1]
    tm = next(c for c in (2048, 1024, 512, 256) if n_rows % c == 0)
    cols = lambda i: (0, i)
    return pl.pallas_call(
        _slots_kernel,
        grid=(n_rows // tm,),
        in_specs=[pl.BlockSpec((TOP_K, tm), cols), pl.BlockSpec((TOP_K, tm), cols),
                  pl.BlockSpec(cnt.shape, lambda i: (0, 0))],
        out_specs=pl.BlockSpec((TOP_K, tm), cols),
        out_shape=jax.ShapeDtypeStruct((TOP_K, n_rows), jnp.int32),
        compiler_params=_cparams(("arbitrary",)),
        name="slots",
    )(eidx, rank, cnt)


SC_WINDOW = 128


def _sc_mesh():
    return plsc.VectorSubcoreMesh(core_axis_name="core", subcore_axis_name="subcore")


def _sc_dispatch(rows, slot_t, n_out, row0):
    width = rows.shape[1]
    n_chunks = slot_t.shape[1] // SC_WINDOW
    info = plsc.get_sparse_core_info()
    n_workers = info.num_cores * info.num_subcores

    @functools.partial(
        pl.kernel, mesh=_sc_mesh(),
        out_type=jax.ShapeDtypeStruct((n_out, width), rows.dtype),
        scratch_types=[pltpu.VMEM((TOP_K, SC_WINDOW), jnp.int32), pltpu.VMEM((SC_WINDOW, width), rows.dtype)],
        name="moe_dispatch")
    def run(rows_hbm, idx_hbm, out_hbm, idx_v, rows_v):
        wid = lax.axis_index("subcore") * info.num_cores + lax.axis_index("core")

        @pl.loop(wid, n_chunks, step=n_workers)
        def _(c):
            r0 = pl.multiple_of(c * SC_WINDOW, SC_WINDOW)
            pltpu.sync_copy(idx_hbm.at[:, pl.ds(r0, SC_WINDOW)], idx_v)
            pltpu.sync_copy(rows_hbm.at[pl.ds(row0 + r0, SC_WINDOW)], rows_v)
            for k in range(TOP_K):
                pltpu.sync_copy(rows_v, out_hbm.at[idx_v.at[k]])

    return run(rows, slot_t)


def _sc_collect(rows, slot_t):
    n_picks, n_rows = slot_t.shape
    width = rows.shape[1]
    n_chunks = n_rows // SC_WINDOW
    info = plsc.get_sparse_core_info()
    n_workers = info.num_cores * info.num_subcores

    @functools.partial(
        pl.kernel, mesh=_sc_mesh(),
        out_type=jax.ShapeDtypeStruct((n_picks, n_rows, width), rows.dtype),
        scratch_types=[pltpu.VMEM((TOP_K, SC_WINDOW), jnp.int32), pltpu.VMEM((SC_WINDOW, width), rows.dtype)],
        name="moe_collect")
    def run(rows_hbm, idx_hbm, out_hbm, idx_v, rows_v):
        wid = lax.axis_index("subcore") * info.num_cores + lax.axis_index("core")

        @pl.loop(wid, n_chunks, step=n_workers)
        def _(c):
            r0 = pl.multiple_of(c * SC_WINDOW, SC_WINDOW)
            pltpu.sync_copy(idx_hbm.at[:, pl.ds(r0, SC_WINDOW)], idx_v)
            for k in range(TOP_K):
                pltpu.sync_copy(rows_hbm.at[idx_v.at[k]], rows_v)
                pltpu.sync_copy(rows_v, out_hbm.at[k, pl.ds(r0, SC_WINDOW)])

    return run(rows, slot_t)


EXPERT_TILE = 512


def _work_items(cnt, n_slots):
    counts = cnt[:, 0].astype(jnp.int32)
    ends = jnp.cumsum(counts)
    n_tiles = n_slots // EXPERT_TILE
    bounds = jnp.sort(jnp.concatenate([jnp.arange(n_tiles, dtype=jnp.int32) * EXPERT_TILE, ends - counts]))
    nxt = jnp.concatenate([bounds[1:], jnp.array([n_slots], jnp.int32)])
    tile = jnp.minimum(bounds // EXPERT_TILE, n_tiles - 1)
    expert = jnp.sum((ends[None, :] <= bounds[:, None]).astype(jnp.int32), axis=1)
    expert = jnp.minimum(expert, N_EXPERTS - 1)
    return tile, expert, bounds - tile * EXPERT_TILE, nxt - tile * EXPERT_TILE


XS_RING = 3


def _experts_kernel(tile_ref, exp_ref, lo_ref, hi_ref, xs_hbm, wg_ref, wu_ref, wd_ref, ys_ref,
                    acc_ref, wgb_ref, wub_ref, wdb_ref, xbuf_ref, xsem):
    i = pl.program_id(0)
    n_items = pl.num_programs(0)
    lo, hi = lo_ref[i], hi_ref[i]

    def tile_copy(item):
        slot = lax.rem(item, XS_RING)
        row0 = pl.multiple_of(tile_ref[item] * EXPERT_TILE, EXPERT_TILE)
        return pltpu.make_async_copy(xs_hbm.at[pl.ds(row0, EXPERT_TILE)], xbuf_ref.at[slot], xsem.at[slot])

    @pl.when(i == 0)
    def _():
        for ahead in range(XS_RING - 1):
            pl.when(ahead < n_items)(lambda: tile_copy(ahead).start())

    @pl.when(i + XS_RING - 1 < n_items)
    def _():
        tile_copy(i + XS_RING - 1).start()

    tile_copy(i).wait()
    xs_ref = xbuf_ref.at[lax.rem(i, XS_RING)]

    @pl.when((i == 0) | (exp_ref[i] != exp_ref[jnp.maximum(i - 1, 0)]))
    def _():
        wgb_ref[...] = wg_ref[...].astype(BF16)
        wub_ref[...] = wu_ref[...].astype(BF16)
        wdb_ref[...] = wd_ref[...].astype(BF16)

    def ffn():
        x_lo, x_hi = _unpack_halves(xs_ref[...])
        x_lo, x_hi = x_lo.astype(BF16), x_hi.astype(BF16)
        n = x_lo.shape[1]
        a = _dot(x_lo, wgb_ref[0:n, :]) + _dot(x_hi, wgb_ref[n:, :])
        u = _dot(x_lo, wub_ref[0:n, :]) + _dot(x_hi, wub_ref[n:, :])
        return _dot(((a * _sigmoid(a)) * u).astype(BF16), wdb_ref[...])

    whole = (lo == 0) & (hi == EXPERT_TILE)

    @pl.when(whole)
    def _():
        ys_ref[...] = _pack_halves(ffn())

    @pl.when(jnp.logical_not(whole) & (hi > lo))
    def _():
        y = ffn()
        row = lax.broadcasted_iota(jnp.int32, y.shape, 0)
        y = jnp.where((row >= lo) & (row < hi), y, 0.0)

        @pl.when(lo == 0)
        def _():
            acc_ref[...] = y

        @pl.when((lo > 0) & (hi < EXPERT_TILE))
        def _():
            acc_ref[...] += y

        @pl.when((lo > 0) & (hi == EXPERT_TILE))
        def _():
            ys_ref[...] = _pack_halves(acc_ref[...] + y)


def _experts(xs, items, layer, exp_wg, exp_wu, exp_wd):
    n_slots, half = xs.shape
    d, hid = exp_wg.shape[-2:]
    tile, expert, lo, hi = items
    grid_spec = pltpu.PrefetchScalarGridSpec(
        num_scalar_prefetch=4,
        grid=(tile.shape[0],),
        in_specs=[pl.BlockSpec(memory_space=pl.ANY),
                  pl.BlockSpec((None, None, d, hid), lambda i, t, e, lo, hi: (layer, e[i], 0, 0)),
                  pl.BlockSpec((None, None, d, hid), lambda i, t, e, lo, hi: (layer, e[i], 0, 0)),
                  pl.BlockSpec((None, None, hid, d), lambda i, t, e, lo, hi: (layer, e[i], 0, 0))],
        out_specs=pl.BlockSpec((EXPERT_TILE, half), lambda i, t, e, lo, hi: (t[i], 0)),
        scratch_shapes=[pltpu.VMEM((EXPERT_TILE, d), F32), pltpu.VMEM((d, hid), BF16),
                        pltpu.VMEM((d, hid), BF16), pltpu.VMEM((hid, d), BF16),
                        pltpu.VMEM((XS_RING, EXPERT_TILE, half), jnp.uint32),
                        pltpu.SemaphoreType.DMA((XS_RING,))])
    return pl.pallas_call(
        _experts_kernel,
        grid_spec=grid_spec,
        out_shape=jax.ShapeDtypeStruct((n_slots, half), jnp.uint32),
        compiler_params=_cparams(("arbitrary",)),
        name="experts",
    )(tile, expert, lo, hi, xs, exp_wg, exp_wu, exp_wd)


def _moe_out_tile(yg_ref, w_ref, v_ref, sg_ref, su_ref, sd_ref, h1_ref, mod_ref, gpost_ref,
                  *, tiles_per_batch, n_batch, d):
    i = pl.program_id(0)
    x = v_ref[...]
    a = _dot(x, sg_ref[...])
    f = _dot(((a * _sigmoid(a)) * _dot(x, su_ref[...])).astype(BF16), sd_ref[...])
    n = d // 2
    f_lo, f_hi = f[:, :n], f[:, n:]
    w = w_ref[...]
    for k in range(TOP_K):
        y_lo, y_hi = _unpack_halves(yg_ref[k])
        wk = w[:, k:k + 1]
        f_lo = f_lo + wk * y_lo
        f_hi = f_hi + wk * y_hi
    f = jnp.concatenate([f_lo, f_hi], axis=1)
    bi = jnp.minimum(i // tiles_per_batch, n_batch)
    g2 = mod_ref[pl.ds(bi, 1), 5 * d:6 * d]
    return h1_ref[...] + g2 * _rms(f, gpost_ref[...])


N_MOE_OUT_IN = 9
N_INPROJ_IN = 9


def _moe_out_kernel(*refs, **kw):
    refs[N_MOE_OUT_IN][...] = _moe_out_tile(*refs[:N_MOE_OUT_IN], **kw)


def _moe_out_inproj_kernel(*refs, n_batch, d, tiles_per_batch):
    dims = dict(tiles_per_batch=tiles_per_batch, n_batch=n_batch, d=d)
    h = _moe_out_tile(*refs[:N_MOE_OUT_IN], **dims)
    n_in = N_MOE_OUT_IN + N_INPROJ_IN
    refs[n_in][...] = h
    _inproj_tile(h, *refs[N_MOE_OUT_IN:n_in], *refs[n_in + 1:], **dims)


def _moe_out_specs(yg, w, v, sg, su, sd, h1, mods, gpost, tm):
    d = v.shape[1]
    rows = lambda i: (i, 0)
    const = lambda i: (0, 0)
    once = dict(pipeline_mode=pl.Buffered(1))
    return [pl.BlockSpec((TOP_K, tm, d // 2), lambda i: (0, i, 0)), pl.BlockSpec((tm, TOP_K), rows),
            pl.BlockSpec((tm, d), rows),
            pl.BlockSpec(sg.shape, const, **once), pl.BlockSpec(su.shape, const, **once),
            pl.BlockSpec(sd.shape, const, **once),
            pl.BlockSpec((tm, d), rows), pl.BlockSpec(mods.shape, const), pl.BlockSpec((1, d), const)]


def _moe_out(moe_args, *, n_batch, seq):
    v = moe_args[2]
    n_rows, d = v.shape
    return pl.pallas_call(
        functools.partial(_moe_out_kernel, tiles_per_batch=seq // TM, n_batch=n_batch, d=d),
        grid=(n_rows // TM,),
        in_specs=_moe_out_specs(*moe_args, TM),
        out_specs=pl.BlockSpec((TM, d), lambda i: (i, 0)),
        out_shape=jax.ShapeDtypeStruct((n_rows, d), F32),
        compiler_params=_cparams(("arbitrary",)),
        name="moe_out",
    )(*moe_args)


FUSED_TM = 256


def _moe_out_inproj(moe_args, inproj_args, *, n_batch, seq):
    v = moe_args[2]
    n_rows, d = v.shape
    tm = FUSED_TM
    in_specs, out_specs, out_shape = _inproj_specs(n_rows, d, *inproj_args, tm=tm, n_batch=n_batch, seq=seq)
    rows = lambda i: (i, 0)
    return pl.pallas_call(
        functools.partial(_moe_out_inproj_kernel, tiles_per_batch=seq // tm, n_batch=n_batch, d=d),
        grid=(n_rows // tm,),
        in_specs=_moe_out_specs(*moe_args, tm) + in_specs,
        out_specs=[pl.BlockSpec((tm, d), rows)] + out_specs,
        out_shape=[jax.ShapeDtypeStruct((n_rows, d), F32)] + out_shape,
        compiler_params=_cparams(("arbitrary",)),
        name="moe_out_inproj",
    )(*moe_args, *inproj_args)


def _moe_routed(v, vp, layer, rw, rb, exp_wg, exp_wu, exp_wd):
    n_rows = v.shape[0]
    n_slots = n_rows * TOP_K
    assert n_slots % EXPERT_TILE == 0
    eidx, rank, w_t, cnt = _router(v, rw, rb, 0, n_rows // TM)
    slot_t = _slots(eidx, rank, cnt)
    xs = _sc_dispatch(vp, slot_t, n_slots, 0)
    ys = _experts(xs, _work_items(cnt, n_slots), layer, exp_wg, exp_wu, exp_wd)
    return _sc_collect(ys, slot_t), w_t.T


def _rope_tables(seq):
    rows = seq // GRID_W
    row_id = np.repeat(np.arange(rows, dtype=np.float64), GRID_W)
    col_id = np.tile(np.arange(GRID_W, dtype=np.float64), rows)

    def tables(rot_dim):
        axis_dim = rot_dim // 2
        inv_freq = ROPE_BASE ** (-np.arange(0, axis_dim, 2, dtype=np.float64) / axis_dim)
        ang_r = row_id[:, None] * inv_freq[None, :]
        ang_c = col_id[:, None] * inv_freq[None, :]
        cos = np.concatenate([np.cos(ang_r), np.cos(ang_r), np.cos(ang_c), np.cos(ang_c)], axis=1)
        sin = np.concatenate([-np.sin(ang_r), np.sin(ang_r), -np.sin(ang_c), np.sin(ang_c)], axis=1)
        return cos, sin

    cos64, sin64 = tables(GQA_DIM)
    cos32, sin32 = tables(MLA_ROPE)
    ones = np.ones((seq, MLA_NOPE))
    pad = LANES - MLA_NOPE - MLA_ROPE
    cospe = np.concatenate([ones, cos32, np.ones((seq, pad))], axis=1)
    sinpe = np.concatenate([0 * ones, sin32, np.zeros((seq, pad))], axis=1)
    tab = np.concatenate([cos64, cos64, sin64, sin64, cospe, sinpe], axis=1)
    ident = np.concatenate([np.ones((TM, LANES)), np.zeros((TM, LANES)),
                            np.ones((TM, LANES)), np.zeros((TM, LANES))], axis=1)
    return jnp.asarray(np.concatenate([tab, ident], axis=0), F32)


def _pack_w_in(w):
    d = w.shape[0]
    sizes = (MLA_Q_LORA, MLA_KV_LORA, MLA_ROPE, GQA_HEADS * GQA_DIM, GQA_KV_HEADS * GQA_DIM,
             GQA_KV_HEADS * GQA_DIM, RET_HEADS * RET_QK, RET_HEADS * RET_QK, RET_HEADS * RET_V,
             RET_HEADS * RET_V, 3 * d)
    offs, parts = 0, []
    for s in sizes:
        parts.append(w[:, offs:offs + s])
        offs += s
    cq, ckv, kpe, gq, gk, gv, rq, rk, rv, rg, gates = parts

    def twice(m):
        heads = [m[:, i * GQA_DIM:(i + 1) * GQA_DIM] for i in range(GQA_KV_HEADS)]
        return jnp.concatenate([hh for hd in heads for hh in (hd, hd)], axis=1)

    kpe_slab = jnp.concatenate([jnp.zeros((d, MLA_NOPE), F32), kpe,
                                jnp.zeros((d, LANES - MLA_NOPE - MLA_ROPE), F32)], axis=1)
    packed = jnp.concatenate([cq, ckv, kpe_slab, gq * (GQA_DIM ** -0.5 * LOG2_E), twice(gk), twice(gv),
                              rq, rk * RET_QK ** -0.5, rv, rg, gates], axis=1)
    assert packed.shape[1] == W_COLS
    return packed.astype(BF16)


def _pack_mla_up(w_uq, w_ukv):
    r = w_uq.shape[0]
    dq = MLA_NOPE + MLA_ROPE
    wq = jnp.pad(w_uq.reshape(r, MLA_HEADS, dq), ((0, 0), (0, 0), (0, LANES - dq))).reshape(r, MLA_HEADS * LANES)
    kv = w_ukv.reshape(r, MLA_HEADS, MLA_NOPE + MLA_V)
    wk = jnp.pad(kv[:, :, :MLA_NOPE], ((0, 0), (0, 0), (0, LANES - MLA_NOPE))).reshape(r, MLA_HEADS * LANES)
    wv = kv[:, :, MLA_NOPE:]
    zeros = jnp.zeros_like(wv)
    even = jnp.concatenate([wv, zeros], axis=2)
    odd = jnp.concatenate([zeros, wv], axis=2)
    wv = jnp.where((jnp.arange(MLA_HEADS) % 2 == 0)[None, :, None], even, odd).reshape(r, MLA_HEADS * LANES)
    return wq.astype(BF16), wk.astype(BF16), wv.astype(BF16)


def kernel(x, c, ctx, c_ctx, ada_w, ada_b, norm_mix_pre, norm_mix_post, norm_ffn_pre, norm_ffn_post, w_in, mla_q_norm, mla_w_uq, mla_kv_norm, mla_w_ukv, gqa_sink, ret_decay_fwd, ret_decay_bwd, w_br_mla, w_br_gqa, w_br_ret, w_out, router_w, router_bias, exp_w_gate, exp_w_up, exp_w_down, shared_w_gate, shared_w_up, shared_w_down):
    n_batch, seq, d = x.shape
    n_ctx = ctx.shape[1]
    depth = ada_w.shape[0]
    n_lat_rows = n_batch * seq
    assert seq % TM == 0 and (n_batch * n_ctx) % TM == 0 and seq % ATT_TQ == 0 and seq % n_ctx == 0
    assert n_batch < MOD_ROWS and seq % GRID_W == 0

    cond = jnp.zeros((MOD_ROWS, d), F32).at[:n_batch].set(c).at[n_batch].set(c_ctx)
    mods_all = _adaln(cond, ada_w, ada_b)
    rope = _rope_tables(seq)
    h = jnp.concatenate([x.reshape(n_lat_rows, d), ctx.reshape(n_batch * n_ctx, d)], axis=0)
    row = lambda p: p.reshape(1, -1)
    dims = dict(n_batch=n_batch, seq=seq)

    def inproj_args(l):
        return (mods_all[l], row(norm_mix_pre[l]), _pack_w_in(w_in[l]), rope, row(mla_q_norm[l]),
                row(mla_kv_norm[l]), *_pack_mla_up(mla_w_uq[l], mla_w_ukv[l]))

    projected = _inproj(h, inproj_args(0), **dims)
    for l in range(depth):
        last = l == depth - 1
        mods = mods_all[l]
        mq, mk, mv, gqa, ret, gates = projected
        a = _mla_attention(mq, mk, mv, ctx=n_ctx, with_ctx_queries=not last, **dims)
        sink_tab = jnp.broadcast_to(gqa_sink[l].astype(F32)[:, None] * LOG2_E, (GQA_HEADS, LANES))
        w = _window_attention(gqa, sink_tab, ctx=n_ctx, with_ctx_queries=not last, **dims)
        lg = jnp.concatenate([jax.nn.log_sigmoid(ret_decay_fwd[l].astype(F32)),
                              jax.nn.log_sigmoid(ret_decay_bwd[l].astype(F32))])
        o_f, o_b = _retention(ret, jnp.broadcast_to(lg[:, None], (2 * RET_HEADS, LANES)), ctx=n_ctx, **dims)
        n_rows = n_lat_rows if last else h.shape[0]
        h1, v, vp = _merge(a, w, o_f, o_b, ret, gates, h, mods, row(norm_mix_post[l]), row(norm_ffn_pre[l]),
                           w_br_mla[l].astype(BF16), w_br_gqa[l].astype(BF16), w_br_ret[l].astype(BF16),
                           w_out[l].astype(BF16), n_rows=n_rows, **dims)
        yg, gate_w = _moe_routed(v, vp, l, router_w[l].T.astype(BF16), router_bias[l].astype(F32).reshape(-1, 1),
                                 exp_w_gate, exp_w_up, exp_w_down)
        moe_args = (yg, gate_w, v, shared_w_gate[l].astype(BF16), shared_w_up[l].astype(BF16),
                    shared_w_down[l].astype(BF16), h1, mods, row(norm_ffn_post[l]))
        if last:
            h = _moe_out(moe_args, **dims)
        else:
            h, *projected = _moe_out_inproj(moe_args, inproj_args(l + 1), **dims)
    return h[:n_lat_rows].reshape(n_batch, seq, d)
```

```python
import functools

import numpy as np
import jax
import jax.numpy as jnp
from jax import lax
from jax.experimental import pallas as pl
from jax.experimental.pallas import tpu as pltpu
from jax.experimental.pallas import tpu_sc as plsc

F32 = jnp.float32
BF16 = jnp.bfloat16

GRID_W = 64
ROPE_BASE = 10000.0
NORM_EPS = 1e-6
NEG_INF = -1e30
LOG2_E = 1.4426950408889634
N_MOD = 6
MLA_HEADS, MLA_NOPE, MLA_ROPE, MLA_V = 8, 64, 32, 64
MLA_Q_LORA, MLA_KV_LORA = 256, 256
GQA_HEADS, GQA_KV_HEADS, GQA_DIM, WINDOW = 8, 2, 64, 128
RET_HEADS, RET_QK, RET_V, RET_CHUNK = 4, 64, 128, 128
N_EXPERTS, N_EXPERT_GROUPS, TOPK_GROUPS, TOP_K = 64, 8, 4, 8
EXPERTS_PER_GROUP = N_EXPERTS // N_EXPERT_GROUPS
ROUTED_SCALE = 2.5

LANES = 128
TM = 512
ATT_TQ = 512
WIN_TQ = 256
RET_TILE = 256
MOD_ROWS = 8
V7X_VMEM_LIMIT = 56 * 1024 * 1024

C_CQ, C_CKV, C_KPE = 0, 256, 512
C_G = 640
C_R = 1664
C_GATE = 3200
W_COLS = 6272


def _cparams(sem):
    return pltpu.CompilerParams(dimension_semantics=sem, vmem_limit_bytes=V7X_VMEM_LIMIT)


def _rms(x, g):
    return x * lax.rsqrt(jnp.mean(x * x, axis=-1, keepdims=True) + NORM_EPS) * g


def _sigmoid(x):
    return 0.5 * jnp.tanh(0.5 * x) + 0.5


def _dot(a, b):
    return jnp.dot(a, b, preferred_element_type=F32)


def _dot_nt(a, b):
    return lax.dot_general(a, b, (((1,), (1,)), ((), ())), preferred_element_type=F32)


def _dot_tn(a, b):
    return lax.dot_general(a, b, (((0,), (0,)), ((), ())), preferred_element_type=F32)


def _rope(x, cos, sin, half):
    n = x.shape[-1]
    reps = n // LANES
    if reps > 1:
        cos = jnp.concatenate([cos] * reps, axis=1)
        sin = jnp.concatenate([sin] * reps, axis=1)
    lane = lax.broadcasted_iota(jnp.int32, x.shape, 1)
    up = pltpu.roll(x, half, 1)
    dn = pltpu.roll(x, n - half, 1)
    partner = jnp.where((lane & (2 * half - 1)) < half, dn, up)
    return x * cos + partner * sin


def _lane_lo(shape):
    return (lax.broadcasted_iota(jnp.int32, shape, 1) & (LANES - 1)) < (LANES // 2)


def _pack_halves(x):
    n = x.shape[1] // 2
    bits = lambda t: lax.bitcast_convert_type(t.astype(BF16).astype(F32), jnp.uint32)
    return (bits(x[:, :n]) >> 16) | bits(x[:, n:])


def _unpack_halves(p):
    lo = lax.bitcast_convert_type(p << 16, F32)
    hi = lax.bitcast_convert_type(p & jnp.uint32(0xFFFF0000), F32)
    return lo, hi


def _ada_kernel(c_ref, w_ref, b_ref, o_ref):
    c = c_ref[...]
    s = c * _sigmoid(c)
    o_ref[...] = _dot(s.astype(BF16), w_ref[...].astype(BF16)) + b_ref[...]


def _adaln(cond, ada_w, ada_b):
    n_layers, d, n = ada_w.shape
    tn = 1024
    return pl.pallas_call(
        _ada_kernel,
        grid=(n_layers, n // tn),
        in_specs=[pl.BlockSpec((MOD_ROWS, d), lambda l, j: (0, 0)),
                  pl.BlockSpec((None, d, tn), lambda l, j: (l, 0, j)),
                  pl.BlockSpec((None, 1, tn), lambda l, j: (l, 0, j))],
        out_specs=pl.BlockSpec((None, MOD_ROWS, tn), lambda l, j: (l, 0, j)),
        out_shape=jax.ShapeDtypeStruct((n_layers, MOD_ROWS, n), F32),
        compiler_params=_cparams(("arbitrary", "arbitrary")),
        name="adaln",
    )(cond, ada_w, ada_b.reshape(n_layers, 1, n))


def _inproj_tile(h, mod_ref, gpre_ref, w_ref, rope_ref, qn_ref, kvn_ref, wuq_ref, wuk_ref, wuv_ref,
                 mq_ref, mk_ref, mv_ref, gqa_ref, ret_ref, gate_ref, *, tiles_per_batch, n_batch, d):
    i = pl.program_id(0)
    bi = jnp.minimum(i // tiles_per_batch, n_batch)
    sh = mod_ref[pl.ds(bi, 1), 0:d]
    sc = mod_ref[pl.ds(bi, 1), d:2 * d]
    u = (_rms(h, gpre_ref[...]) * (1.0 + sc) + sh).astype(BF16)

    cos64 = rope_ref[:, 0:LANES]
    sin64 = rope_ref[:, LANES:2 * LANES]
    cospe = rope_ref[:, 2 * LANES:3 * LANES]
    sinpe = rope_ref[:, 3 * LANES:4 * LANES]

    c = _dot(u, w_ref[:, C_CQ:C_G])
    kpe = _rope(c[:, C_KPE:C_G], cospe, sinpe, MLA_ROPE // 4)
    qn = _rms(c[:, C_CQ:C_CKV], qn_ref[...]).astype(BF16)
    q = _rope(_dot(qn, wuq_ref[...]), cospe, sinpe, MLA_ROPE // 4)
    mq_ref[...] = (q * ((MLA_NOPE + MLA_ROPE) ** -0.5 * LOG2_E)).astype(mq_ref.dtype)
    kvn = _rms(c[:, C_CKV:C_KPE], kvn_ref[...]).astype(BF16)
    k = _dot(kvn, wuk_ref[...]) + jnp.concatenate([kpe] * MLA_HEADS, axis=1)
    mk_ref[...] = k.astype(mk_ref.dtype)
    v = _dot(kvn, wuv_ref[...])
    lane = lax.broadcasted_iota(jnp.int32, v.shape, 1)
    value_lane = ((lane & (LANES - 1)) < MLA_V) == (((lane >> (LANES.bit_length() - 1)) & 1) == 0)
    mv_ref[...] = jnp.where(value_lane, v, 1.0).astype(mv_ref.dtype)

    g = _dot(u, w_ref[:, C_G:C_R])
    n_qk = GQA_HEADS * GQA_DIM + 2 * GQA_KV_HEADS * GQA_DIM
    gqa_ref[:, 0:n_qk] = _rope(g[:, 0:n_qk], cos64, sin64, GQA_DIM // 4).astype(gqa_ref.dtype)
    gqa_ref[:, n_qk:] = g[:, n_qk:].astype(gqa_ref.dtype)

    r = _dot(u, w_ref[:, C_R:C_GATE])
    n_qk = 2 * RET_HEADS * RET_QK
    ret_ref[:, 0:n_qk] = _rope(r[:, 0:n_qk], cos64, sin64, RET_QK // 4)
    ret_ref[:, n_qk:] = r[:, n_qk:]

    gate_ref[...] = _dot(u, w_ref[:, C_GATE:W_COLS]).astype(gate_ref.dtype)


def _inproj_kernel(h_ref, *refs, **kw):
    _inproj_tile(h_ref[...], *refs, **kw)


def _inproj_specs(t, d, mods, gpre, w_all, rope, qn, kvn, wuq, wuk, wuv, *, tm, n_batch, seq):
    tiles_per_batch = seq // tm
    n_lat_tiles = n_batch * tiles_per_batch
    const = lambda i: (0, 0)
    rows = lambda i: (i, 0)
    rope_idx = lambda i: (jnp.where(i < n_lat_tiles, i % tiles_per_batch, tiles_per_batch), 0)
    once = dict(pipeline_mode=pl.Buffered(1))
    hq = MLA_HEADS * LANES
    outs = [jax.ShapeDtypeStruct((t, hq), BF16), jax.ShapeDtypeStruct((t, hq), BF16),
            jax.ShapeDtypeStruct((t, hq), BF16),
            jax.ShapeDtypeStruct((t, C_R - C_G), BF16),
            jax.ShapeDtypeStruct((t, C_GATE - C_R), F32),
            jax.ShapeDtypeStruct((t, W_COLS - C_GATE), BF16)]
    in_specs = [pl.BlockSpec(mods.shape, const),
                pl.BlockSpec((1, d), const),
                pl.BlockSpec(w_all.shape, const, **once),
                pl.BlockSpec((tm, 4 * LANES), rope_idx),
                pl.BlockSpec(qn.shape, const), pl.BlockSpec(kvn.shape, const),
                pl.BlockSpec(wuq.shape, const, **once), pl.BlockSpec(wuk.shape, const, **once),
                pl.BlockSpec(wuv.shape, const, **once)]
    return in_specs, [pl.BlockSpec((tm, o.shape[1]), rows) for o in outs], outs


def _inproj(h, inproj_args, *, n_batch, seq):
    t, d = h.shape
    in_specs, out_specs, out_shape = _inproj_specs(t, d, *inproj_args, tm=TM, n_batch=n_batch, seq=seq)
    return pl.pallas_call(
        functools.partial(_inproj_kernel, tiles_per_batch=seq // TM, n_batch=n_batch, d=d),
        grid=(t // TM,),
        in_specs=[pl.BlockSpec((TM, d), lambda i: (i, 0))] + in_specs,
        out_specs=out_specs,
        out_shape=out_shape,
        compiler_params=_cparams(("arbitrary",)),
        name="inproj",
    )(h, *inproj_args)


MLA_HEADS_PER_STEP = 4


def _mla_kernel(q_ref, kl_ref, kc_ref, vl_ref, vc_ref, o_ref, s_ref, p_ref, *, with_lat):
    n_ctx = kc_ref.shape[0]

    def body(with_lat):
        n_keys = n_ctx + (kl_ref.shape[0] if with_lat else 0)

        def scores(h):
            sl = slice(h * LANES, (h + 1) * LANES)
            s_ref[h % 2, :, 0:n_ctx] = _dot_nt(q_ref[:, sl], kc_ref[:, sl])
            if with_lat:
                s_ref[h % 2, :, n_ctx:n_keys] = _dot_nt(q_ref[:, sl], kl_ref[:, sl])

        def probs(h):
            s = s_ref[h % 2, :, 0:n_keys]
            p_ref[h % 2, :, 0:n_keys] = jnp.exp2(s - jnp.max(s, axis=-1, keepdims=True)).astype(BF16)

        def weighted(h):
            sl = slice(h * LANES, (h + 1) * LANES)
            o = _dot(p_ref[h % 2, :, 0:n_ctx], vc_ref[:, sl])
            if with_lat:
                o = o + _dot(p_ref[h % 2, :, n_ctx:n_keys], vl_ref[:, sl])
            return o / pltpu.roll(o, LANES // 2, 1)

        outs = [None] * MLA_HEADS_PER_STEP
        scores(0)
        for h in range(MLA_HEADS_PER_STEP):
            if h + 1 < MLA_HEADS_PER_STEP:
                scores(h + 1)
            probs(h)
            outs[h] = weighted(h)
        for pr in range(MLA_HEADS_PER_STEP // 2):
            even, odd = outs[2 * pr], outs[2 * pr + 1]
            o_ref[:, pr * LANES:(pr + 1) * LANES] = jnp.where(_lane_lo(even.shape), even, odd).astype(o_ref.dtype)

    body(with_lat)


def _mla_call(mq, mk, mv, *, n_batch, seq, ctx, latent_queries):
    hps = MLA_HEADS_PER_STEP
    ctx_blk0 = n_batch * seq // ctx
    tq = ATT_TQ if latent_queries else ctx
    nq = seq // tq if latent_queries else 1
    q_blk0 = 0 if latent_queries else n_batch * seq // tq
    q_idx = lambda b, g, i: (q_blk0 + b * nq + i, g)
    ctx_idx = lambda b, g, i: (ctx_blk0 + b, g)
    lat_idx = (lambda b, g, i: (b, g)) if latent_queries else ctx_idx
    n_lat = seq if latent_queries else ctx
    n_keys = ctx + (seq if latent_queries else 0)
    in_specs = [pl.BlockSpec((tq, hps * LANES), q_idx),
                pl.BlockSpec((n_lat, hps * LANES), lat_idx), pl.BlockSpec((ctx, hps * LANES), ctx_idx),
                pl.BlockSpec((n_lat, hps * LANES), lat_idx), pl.BlockSpec((ctx, hps * LANES), ctx_idx)]
    return pl.pallas_call(
        functools.partial(_mla_kernel, with_lat=latent_queries),
        grid=(n_batch, MLA_HEADS // hps, nq),
        in_specs=in_specs,
        out_specs=pl.BlockSpec((tq, hps * MLA_V), lambda b, g, i: (b * nq + i, g)),
        out_shape=jax.ShapeDtypeStruct((n_batch * nq * tq, MLA_HEADS * MLA_V), BF16),
        scratch_shapes=[pltpu.VMEM((2, tq, n_keys), F32), pltpu.VMEM((2, tq, n_keys), BF16)],
        compiler_params=_cparams(("arbitrary", "arbitrary", "arbitrary")),
        name="mla_attn" if latent_queries else "mla_attn_ctx",
    )(mq, mk, mk, mv, mv)


def _mla_attention(mq, mk, mv, *, n_batch, seq, ctx, with_ctx_queries):
    dims = dict(n_batch=n_batch, seq=seq, ctx=ctx)
    lat = _mla_call(mq, mk, mv, latent_queries=True, **dims)
    return lat, (_mla_call(mq, mk, mv, latent_queries=False, **dims) if with_ctx_queries else lat)


def _win_kernel(q_ref, kp_ref, kcur_ref, kn_ref, vp_ref, vcur_ref, vn_ref, kc_ref, vc_ref, sink_ref, o_ref,
                s_ref, p_ref, *, seq, with_lat):
    i = pl.program_id(1)
    tq = q_ref.shape[0]
    group = GQA_HEADS // GQA_KV_HEADS

    def body(with_lat):
        n_ctx = kc_ref.shape[0]
        n_keys = n_ctx + (tq + 2 * WINDOW if with_lat else 0)
        if with_lat:
            q_pos = i * tq + lax.broadcasted_iota(jnp.int32, (tq, n_keys), 0)
            k_pos = i * tq - WINDOW - n_ctx + lax.broadcasted_iota(jnp.int32, (tq, n_keys), 1)
            in_band = (jnp.abs(q_pos - k_pos) <= WINDOW) & (k_pos >= 0) & (k_pos < seq)
            valid = in_band | (lax.broadcasted_iota(jnp.int32, (tq, n_keys), 1) < n_ctx)
        lo = _lane_lo((tq, LANES))
        lo_k = _lane_lo((n_keys, LANES))
        keys, values = [], []
        for kv in range(GQA_KV_HEADS):
            sl = slice(kv * LANES, (kv + 1) * LANES)
            if with_lat:
                k_all = jnp.concatenate([kc_ref[:, sl], kp_ref[:, sl], kcur_ref[:, sl], kn_ref[:, sl]], axis=0)
                v_all = jnp.concatenate([vc_ref[:, sl], vp_ref[:, sl], vcur_ref[:, sl], vn_ref[:, sl]], axis=0)
            else:
                k_all, v_all = kc_ref[:, sl], vc_ref[:, sl]
            keys.append(k_all)
            one = jnp.ones_like(v_all)
            values.append((jnp.where(lo_k, v_all, one), jnp.where(lo_k, one, v_all)))

        def scores(hd):
            kv, pair = hd // group, hd // 2
            qp = q_ref[:, pair * LANES:(pair + 1) * LANES]
            qm = jnp.where(lo if hd % 2 == 0 else jnp.logical_not(lo), qp, jnp.zeros_like(qp))
            s = _dot_nt(qm, keys[kv])
            s_ref[hd % 2, :, 0:n_keys] = jnp.where(valid, s, NEG_INF) if with_lat else s

        def probs(hd):
            s = s_ref[hd % 2, :, 0:n_keys]
            m = jnp.maximum(jnp.max(s, axis=-1, keepdims=True), sink_ref[hd:hd + 1, 0:1])
            p_ref[hd % 2, :, 0:n_keys] = jnp.exp2(s - m).astype(BF16)
            return jnp.exp2(sink_ref[hd:hd + 1, 0:1] - m)

        def weighted(hd, sink_term):
            o = _dot(p_ref[hd % 2, :, 0:n_keys], values[hd // group][hd % 2])
            return o / (pltpu.roll(o, LANES // 2, 1) + sink_term)

        outs = [None] * GQA_HEADS
        scores(0)
        for hd in range(GQA_HEADS):
            if hd + 1 < GQA_HEADS:
                scores(hd + 1)
            outs[hd] = weighted(hd, probs(hd))
        for pair in range(GQA_HEADS // 2):
            o_ref[:, pair * LANES:(pair + 1) * LANES] = jnp.where(
                lo, outs[2 * pair], outs[2 * pair + 1]).astype(o_ref.dtype)

    body(with_lat)


def _win_call(gqa, sink_tab, *, n_batch, seq, ctx, latent_queries):
    tq = WIN_TQ if latent_queries else ctx
    nq = seq // tq if latent_queries else 1
    q_blk0 = 0 if latent_queries else n_batch * seq // tq
    per_tile = tq // WINDOW
    n_win_blocks = seq // WINDOW
    ctx_blk0 = n_batch * seq // ctx
    nqk = GQA_HEADS * GQA_DIM
    kw = 2 * GQA_KV_HEADS * GQA_DIM
    k_col, v_col = nqk // kw, nqk // kw + 1
    q_idx = lambda b, i: (q_blk0 + b * nq + i, 0)
    cidx = lambda col: (lambda b, i: (ctx_blk0 + b, col))
    if latent_queries:
        cur = lambda col: (lambda b, i: (b * nq + i, col))
        prev = lambda col: (lambda b, i: (b * n_win_blocks + jnp.maximum(per_tile * i - 1, 0), col))
        nxt = lambda col: (lambda b, i: (b * n_win_blocks + jnp.minimum(per_tile * (i + 1), n_win_blocks - 1), col))
        band = [((WINDOW, kw), prev), ((tq, kw), cur), ((WINDOW, kw), nxt)]
    else:
        band = [((ctx, kw), cidx)] * 3
    n_keys = ctx + (tq + 2 * WINDOW if latent_queries else 0)
    in_specs = ([pl.BlockSpec((tq, nqk), q_idx)]
                + [pl.BlockSpec(shape, idx(k_col)) for shape, idx in band]
                + [pl.BlockSpec(shape, idx(v_col)) for shape, idx in band]
                + [pl.BlockSpec((ctx, kw), cidx(k_col)), pl.BlockSpec((ctx, kw), cidx(v_col)),
                   pl.BlockSpec(sink_tab.shape, lambda b, i: (0, 0))])
    return pl.pallas_call(
        functools.partial(_win_kernel, seq=seq, with_lat=latent_queries),
        grid=(n_batch, nq),
        in_specs=in_specs,
        out_specs=pl.BlockSpec((tq, nqk), lambda b, i: (b * nq + i, 0)),
        out_shape=jax.ShapeDtypeStruct((n_batch * nq * tq, nqk), BF16),
        scratch_shapes=[pltpu.VMEM((2, tq, n_keys), F32), pltpu.VMEM((2, tq, n_keys), BF16)],
        compiler_params=_cparams(("arbitrary", "arbitrary")),
        name="win_attn" if latent_queries else "win_attn_ctx",
    )(*([gqa] * 9 + [sink_tab]))


def _window_attention(gqa, sink_tab, *, n_batch, seq, ctx, with_ctx_queries):
    dims = dict(n_batch=n_batch, seq=seq, ctx=ctx)
    lat = _win_call(gqa, sink_tab, latent_queries=True, **dims)
    return lat, (_win_call(gqa, sink_tab, latent_queries=False, **dims) if with_ctx_queries else lat)


def _ret_kernel(f_ref, b_ref, lg_ref, of_ref, ob_ref, sf_ref, sb_ref, qdec_ref, kdec_ref, cdec_ref, inner_ref):
    @pl.when(pl.program_id(1) == 0)
    def _():
        sf_ref[...] = jnp.zeros_like(sf_ref)
        sb_ref[...] = jnp.zeros_like(sb_ref)

    L = f_ref.shape[0]
    lo = _lane_lo((L, LANES))
    srow_lo = lax.broadcasted_iota(jnp.int32, (LANES, LANES), 0) < RET_QK
    nq = RET_HEADS * RET_QK
    n_pairs = RET_HEADS // 2

    @pl.when(pl.program_id(1) == 0)
    def _():
        ii = lax.broadcasted_iota(jnp.int32, (L, L), 0)
        jj = lax.broadcasted_iota(jnp.int32, (L, L), 1)
        row = lax.broadcasted_iota(jnp.int32, (L, LANES), 0).astype(F32)
        for direction, forward in enumerate((True, False)):
            dist = ii - jj if forward else jj - ii
            distf = jnp.maximum(dist, 0).astype(F32)
            for pr in range(n_pairs):
                r0 = direction * RET_HEADS + 2 * pr
                lg = [lg_ref[r0 + e:r0 + e + 1, :] for e in range(2)]
                lg_lane = jnp.where(lo, lg[0], lg[1])
                qdec_ref[direction * n_pairs + pr] = jnp.exp(lg_lane * ((row + 1.0) if forward else (L - row)))
                kdec_ref[direction * n_pairs + pr] = jnp.exp(lg_lane * ((L - 1.0 - row) if forward else row))
                cdec_ref[direction * n_pairs + pr] = jnp.where(srow_lo, jnp.exp(lg[0] * float(L)),
                                                               jnp.exp(lg[1] * float(L)))
                for e in range(2):
                    inner_ref[r0 + e] = jnp.where(dist >= 0, jnp.exp(lg[e][:, 0:1] * distf), 0.0)

    def scan_chunk(x_ref, o_ref, s_ref, direction):
        for pr in range(n_pairs):
            q = x_ref[:, pr * LANES:(pr + 1) * LANES]
            k = x_ref[:, nq + pr * LANES:nq + (pr + 1) * LANES]
            qd = q * qdec_ref[direction * n_pairs + pr]
            kdb = (k * kdec_ref[direction * n_pairs + pr]).astype(BF16)
            kb = k.astype(BF16)
            state = s_ref[pr]
            state_b = state.astype(BF16)
            upd = []
            for e in range(2):
                hd = 2 * pr + e
                keep = lo if e == 0 else jnp.logical_not(lo)
                v = x_ref[:, 2 * nq + hd * RET_V:2 * nq + (hd + 1) * RET_V].astype(BF16)
                attn = _dot_nt(jnp.where(keep, q, 0.0).astype(BF16), kb) * inner_ref[direction * RET_HEADS + hd]
                o = _dot(attn.astype(BF16), v) + _dot(jnp.where(keep, qd, 0.0).astype(BF16), state_b)
                o_ref[:, hd * RET_V:(hd + 1) * RET_V] = o
                upd.append(_dot_tn(kdb, v))
            s_ref[pr] = state * cdec_ref[direction * n_pairs + pr] + jnp.where(srow_lo, upd[0], upd[1])

    scan_chunk(f_ref, of_ref, sf_ref, 0)
    scan_chunk(b_ref, ob_ref, sb_ref, 1)


def _retention(ret, lg_tab, *, n_batch, seq, ctx):
    t = ret.shape[0]
    L = RET_TILE
    assert seq % L == 0 and ctx % L == 0
    n_lat, n_ctx = seq // L, ctx // L
    ctx0 = n_batch * n_lat
    width = 2 * RET_HEADS * RET_QK + RET_HEADS * RET_V
    fwd = lambda b, s: (jnp.where(s < n_ctx, ctx0 + b * n_ctx + s, b * n_lat + s - n_ctx), 0)
    bwd = lambda b, s: (jnp.where(s < n_ctx, ctx0 + b * n_ctx + n_ctx - 1 - s, b * n_lat + n_lat - 1 - (s - n_ctx)), 0)
    out = jax.ShapeDtypeStruct((t, RET_HEADS * RET_V), F32)
    return pl.pallas_call(
        _ret_kernel,
        grid=(n_batch, n_lat + n_ctx),
        in_specs=[pl.BlockSpec((L, width), fwd), pl.BlockSpec((L, width), bwd),
                  pl.BlockSpec(lg_tab.shape, lambda b, s: (0, 0))],
        out_specs=[pl.BlockSpec((L, RET_HEADS * RET_V), fwd), pl.BlockSpec((L, RET_HEADS * RET_V), bwd)],
        out_shape=[out, out],
        scratch_shapes=[pltpu.VMEM((RET_HEADS // 2, LANES, RET_V), F32),
                        pltpu.VMEM((RET_HEADS // 2, LANES, RET_V), F32),
                        pltpu.VMEM((RET_HEADS, L, LANES), F32), pltpu.VMEM((RET_HEADS, L, LANES), F32),
                        pltpu.VMEM((RET_HEADS, LANES, RET_V), F32), pltpu.VMEM((2 * RET_HEADS, L, L), F32)],
        compiler_params=_cparams(("arbitrary", "arbitrary")),
        name="retention",
    )(ret, ret, lg_tab)


def _merge_kernel(al_ref, ac_ref, wl_ref, wc_ref, of_ref, ob_ref, rg_ref, gt_ref, h_ref, mod_ref, gpost_ref,
                  gffn_ref, wa_ref, ww_ref, wr_ref, wo_ref, h1_ref, v_ref, vp_ref,
                  *, tiles_per_batch, n_batch, d):
    i = pl.program_id(0)
    bi = jnp.minimum(i // tiles_per_batch, n_batch)
    latent = i < tiles_per_batch * n_batch
    a_tile = jnp.where(latent, al_ref[...], ac_ref[...])
    w_tile = jnp.where(latent, wl_ref[...], wc_ref[...])
    o = of_ref[...] + ob_ref[...]
    normed = []
    for hd in range(RET_HEADS):
        oh = o[:, hd * RET_V:(hd + 1) * RET_V]
        dev = oh - jnp.mean(oh, axis=-1, keepdims=True)
        normed.append(dev * lax.rsqrt(jnp.mean(dev * dev, axis=-1, keepdims=True) + NORM_EPS))
    g = rg_ref[...]
    r = (g * _sigmoid(g)) * jnp.concatenate(normed, axis=1)
    y = (_sigmoid(gt_ref[:, 0:d].astype(F32)) * _dot(a_tile, wa_ref[...])
         + _sigmoid(gt_ref[:, d:2 * d].astype(F32)) * _dot(w_tile, ww_ref[...])
         + _sigmoid(gt_ref[:, 2 * d:3 * d].astype(F32)) * _dot(r.astype(BF16), wr_ref[...]))
    z = _dot(y.astype(BF16), wo_ref[...])
    g1 = mod_ref[pl.ds(bi, 1), 2 * d:3 * d]
    sh2 = mod_ref[pl.ds(bi, 1), 3 * d:4 * d]
    sc2 = mod_ref[pl.ds(bi, 1), 4 * d:5 * d]
    h1 = h_ref[...] + g1 * _rms(z, gpost_ref[...])
    h1_ref[...] = h1
    v = _rms(h1, gffn_ref[...]) * (1.0 + sc2) + sh2
    v_ref[...] = v.astype(v_ref.dtype)
    vp_ref[...] = _pack_halves(v)


def _merge(a, w, o_f, o_b, ret, gates, h, mods, gpost, gffn, wa, ww, wr, wo, *, n_rows, n_batch, seq):
    d = h.shape[1]
    n_lat_tiles = n_batch * seq // TM
    rows = lambda i: (i, 0)
    lat_rows = lambda i: (jnp.minimum(i, n_lat_tiles - 1), 0)
    ctx_rows = lambda i: (jnp.maximum(i - n_lat_tiles, 0), 0)
    const = lambda i: (0, 0)
    rv = RET_HEADS * RET_V
    rg_col = (2 * RET_HEADS * RET_QK + rv) // rv
    outs = [jax.ShapeDtypeStruct((n_rows, d), F32), jax.ShapeDtypeStruct((n_rows, d), BF16),
            jax.ShapeDtypeStruct((n_rows, d // 2), jnp.uint32)]
    return pl.pallas_call(
        functools.partial(_merge_kernel, tiles_per_batch=seq // TM, n_batch=n_batch, d=d),
        grid=(n_rows // TM,),
        in_specs=[pl.BlockSpec((TM, a[0].shape[1]), lat_rows), pl.BlockSpec((TM, a[1].shape[1]), ctx_rows),
                  pl.BlockSpec((TM, w[0].shape[1]), lat_rows), pl.BlockSpec((TM, w[1].shape[1]), ctx_rows),
                  pl.BlockSpec((TM, rv), rows), pl.BlockSpec((TM, rv), rows),
                  pl.BlockSpec((TM, rv), lambda i: (i, rg_col)),
                  pl.BlockSpec((TM, 3 * d), rows), pl.BlockSpec((TM, d), rows),
                  pl.BlockSpec(mods.shape, const), pl.BlockSpec((1, d), const), pl.BlockSpec((1, d), const),
                  pl.BlockSpec(wa.shape, const), pl.BlockSpec(ww.shape, const),
                  pl.BlockSpec(wr.shape, const), pl.BlockSpec(wo.shape, const)],
        out_specs=[pl.BlockSpec((TM, o.shape[1]), rows) for o in outs],
        out_shape=outs,
        compiler_params=_cparams(("arbitrary",)),
        name="merge",
    )(a[0], a[1], w[0], w[1], o_f, o_b, ret, gates, h, mods, gpost, gffn, wa, ww, wr, wo)


def _router_kernel(v_ref, rw_ref, rb_ref, eidx_ref, rank_ref, w_ref, cnt_ref, carry_ref):
    @pl.when(pl.program_id(0) == 0)
    def _():
        carry_ref[...] = jnp.zeros_like(carry_ref)

    tm = v_ref.shape[0]
    scores = _sigmoid(_dot_nt(rw_ref[...], v_ref[...]))
    sel = scores + rb_ref[...]
    neg = -jnp.inf
    n_grp, per = N_EXPERT_GROUPS, EXPERTS_PER_GROUP

    sel3 = sel.reshape(n_grp, per, tm)
    member_id = lax.broadcasted_iota(jnp.int32, sel3.shape, 1)
    m1 = jnp.max(sel3, axis=1, keepdims=True)
    i1 = jnp.min(jnp.where(sel3 == m1, member_id, per), axis=1, keepdims=True)
    m2 = jnp.max(jnp.where(member_id == i1, neg, sel3), axis=1, keepdims=True)
    gscore = (m1 + m2).reshape(n_grp, tm)
    gid = lax.broadcasted_iota(jnp.int32, gscore.shape, 0)
    ahead = jnp.zeros(gscore.shape, jnp.int32)
    for gj in range(n_grp):
        other = gscore[gj:gj + 1, :]
        ahead = ahead + jnp.where((other > gscore) | ((other == gscore) & (gid > gj)), 1, 0)
    group_ok = (ahead < TOPK_GROUPS).reshape(n_grp, 1, tm)
    sel = jnp.where(group_ok, sel3, NEG_INF).reshape(N_EXPERTS, tm)

    eid = lax.broadcasted_iota(jnp.int32, sel.shape, 0)
    chosen = jnp.zeros(sel.shape, jnp.bool_)
    picks = []
    for _ in range(TOP_K):
        m = jnp.max(sel, axis=0, keepdims=True)
        idx = jnp.min(jnp.where(sel == m, eid, N_EXPERTS), axis=0, keepdims=True)
        hit = eid == idx
        chosen = chosen | hit
        sel = jnp.where(hit, neg, sel)
        picks.append(idx)
    w = jnp.where(chosen, scores, 0.0)
    gate = ROUTED_SCALE * w / jnp.sum(w, axis=0, keepdims=True)

    member = jnp.where(chosen, 1.0, 0.0)
    earlier = lax.broadcasted_iota(jnp.int32, (tm, tm), 0) < lax.broadcasted_iota(jnp.int32, (tm, tm), 1)
    pos = _dot(member.astype(BF16), jnp.where(earlier, 1.0, 0.0).astype(BF16)) + carry_ref[...]
    for k, idx in enumerate(picks):
        hit = eid == idx
        eidx_ref[k:k + 1, :] = idx
        rank_ref[k:k + 1, :] = jnp.sum(jnp.where(hit, pos, 0.0), axis=0, keepdims=True)
        w_ref[k:k + 1, :] = jnp.sum(jnp.where(hit, gate, 0.0), axis=0, keepdims=True)
    carry_ref[...] += jnp.sum(member, axis=1, keepdims=True)
    cnt_ref[...] = carry_ref[...]


def _router(v, rw_t, rb, tile0, n_tiles):
    d = v.shape[1]
    n_rows = n_tiles * TM
    cols = lambda i: (0, i)
    const = lambda i: (0, 0)
    outs = [jax.ShapeDtypeStruct((TOP_K, n_rows), jnp.int32), jax.ShapeDtypeStruct((TOP_K, n_rows), F32),
            jax.ShapeDtypeStruct((TOP_K, n_rows), F32), jax.ShapeDtypeStruct((N_EXPERTS, 1), F32)]
    return pl.pallas_call(
        _router_kernel,
        grid=(n_tiles,),
        in_specs=[pl.BlockSpec((TM, d), lambda i: (tile0 + i, 0)), pl.BlockSpec(rw_t.shape, const),
                  pl.BlockSpec(rb.shape, const)],
        out_specs=[pl.BlockSpec((TOP_K, TM), cols), pl.BlockSpec((TOP_K, TM), cols),
                   pl.BlockSpec((TOP_K, TM), cols), pl.BlockSpec((N_EXPERTS, 1), const)],
        out_shape=outs,
        scratch_shapes=[pltpu.VMEM((N_EXPERTS, 1), F32)],
        compiler_params=_cparams(("arbitrary",)),
        name="router",
    )(v, rw_t, rb)


def _slots_kernel(eidx_ref, rank_ref, cnt_ref, slot_ref):
    tm = eidx_ref.shape[1]
    eid = lax.broadcasted_iota(jnp.int32, (N_EXPERTS, tm), 0)
    padded = jnp.ceil(cnt_ref[...] * (1.0 / EXPERT_TILE)) * EXPERT_TILE
    for k in range(TOP_K):
        before = jnp.sum(jnp.where(eid < eidx_ref[k:k + 1, :], padded, 0.0), axis=0, keepdims=True)
        slot_ref[k:k + 1, :] = (before + rank_ref[k:k + 1, :]).astype(jnp.int32)


def _slots(eidx, rank, cnt):
    n_rows = eidx.shape[1]
    tm = next(c for c in (2048, 1024, 512, 256) if n_rows % c == 0)
    cols = lambda i: (0, i)
    return pl.pallas_call(
        _slots_kernel,
        grid=(n_rows // tm,),
        in_specs=[pl.BlockSpec((TOP_K, tm), cols), pl.BlockSpec((TOP_K, tm), cols),
                  pl.BlockSpec(cnt.shape, lambda i: (0, 0))],
        out_specs=pl.BlockSpec((TOP_K, tm), cols),
        out_shape=jax.ShapeDtypeStruct((TOP_K, n_rows), jnp.int32),
        compiler_params=_cparams(("arbitrary",)),
        name="slots",
    )(eidx, rank, cnt)


SC_WINDOW = 128


def _sc_mesh():
    return plsc.VectorSubcoreMesh(core_axis_name="core", subcore_axis_name="subcore")


def _sc_dispatch(rows, slot_t, pad_slots, n_out):
    width = rows.shape[1]
    n_chunks = slot_t.shape[1] // SC_WINDOW
    info = plsc.get_sparse_core_info()
    n_workers = info.num_cores * info.num_subcores
    n_pad_blocks = pad_slots.shape[0]
    assert pad_slots.shape[1:] == (TOP_K, SC_WINDOW)
    zero_rows = jnp.zeros((SC_WINDOW, width), rows.dtype)

    @functools.partial(
        pl.kernel, mesh=_sc_mesh(),
        out_type=jax.ShapeDtypeStruct((n_out, width), rows.dtype),
        scratch_types=[pltpu.VMEM((TOP_K, SC_WINDOW), jnp.int32), pltpu.VMEM((SC_WINDOW, width), rows.dtype)],
        name="moe_dispatch")
    def run(rows_hbm, idx_hbm, pad_hbm, zero_hbm, out_hbm, idx_v, rows_v):
        wid = lax.axis_index("subcore") * info.num_cores + lax.axis_index("core")

        @pl.loop(wid, n_chunks, step=n_workers)
        def _(c):
            r0 = pl.multiple_of(c * SC_WINDOW, SC_WINDOW)
            pltpu.sync_copy(idx_hbm.at[:, pl.ds(r0, SC_WINDOW)], idx_v)
            pltpu.sync_copy(rows_hbm.at[pl.ds(r0, SC_WINDOW)], rows_v)
            for k in range(TOP_K):
                pltpu.sync_copy(rows_v, out_hbm.at[idx_v.at[k]])

        pltpu.sync_copy(zero_hbm, rows_v)

        @pl.loop(wid, n_pad_blocks, step=n_workers)
        def _(blk):
            pltpu.sync_copy(pad_hbm.at[blk], idx_v)
            for k in range(TOP_K):
                pltpu.sync_copy(rows_v, out_hbm.at[idx_v.at[k]])

    return run(rows, slot_t, pad_slots, zero_rows)


def _sc_collect(rows, slot_t):
    n_picks, n_rows = slot_t.shape
    width = rows.shape[1]
    n_chunks = n_rows // SC_WINDOW
    info = plsc.get_sparse_core_info()
    n_workers = info.num_cores * info.num_subcores

    @functools.partial(
        pl.kernel, mesh=_sc_mesh(),
        out_type=jax.ShapeDtypeStruct((n_picks, n_rows, width), rows.dtype),
        scratch_types=[pltpu.VMEM((TOP_K, SC_WINDOW), jnp.int32), pltpu.VMEM((SC_WINDOW, width), rows.dtype)],
        name="moe_collect")
    def run(rows_hbm, idx_hbm, out_hbm, idx_v, rows_v):
        wid = lax.axis_index("subcore") * info.num_cores + lax.axis_index("core")

        @pl.loop(wid, n_chunks, step=n_workers)
        def _(c):
            r0 = pl.multiple_of(c * SC_WINDOW, SC_WINDOW)
            pltpu.sync_copy(idx_hbm.at[:, pl.ds(r0, SC_WINDOW)], idx_v)
            for k in range(TOP_K):
                pltpu.sync_copy(rows_hbm.at[idx_v.at[k]], rows_v)
                pltpu.sync_copy(rows_v, out_hbm.at[k, pl.ds(r0, SC_WINDOW)])

    return run(rows, slot_t)


EXPERT_TILE = 512


def _expert_layout(cnt, n_tiles, n_filler_blocks):
    counts = cnt[:, 0].astype(jnp.int32)
    padded = (counts + EXPERT_TILE - 1) // EXPERT_TILE * EXPERT_TILE
    ends = jnp.cumsum(padded)
    n_used = ends[-1] // EXPERT_TILE
    tile = jnp.minimum(jnp.arange(n_tiles, dtype=jnp.int32), n_used - 1)
    expert = jnp.sum((ends[None, :] <= (tile * EXPERT_TILE)[:, None]).astype(jnp.int32), axis=1)
    r = jnp.arange(EXPERT_TILE, dtype=jnp.int32)[None, :]
    filler = jnp.where(r < (padded - counts)[:, None], (ends - padded + counts)[:, None] + r, n_tiles * EXPERT_TILE)
    return tile, expert, n_used.reshape(1), filler.reshape(n_filler_blocks, TOP_K, SC_WINDOW)


XS_RING = 3


def _experts_kernel(tile_ref, exp_ref, used_ref, xs_hbm, wg_ref, wu_ref, wd_ref, ys_ref,
                    wgb_ref, wub_ref, wdb_ref, xbuf_ref, xsem):
    i = pl.program_id(0)
    n_items = pl.num_programs(0)

    def tile_copy(item):
        slot = lax.rem(item, XS_RING)
        row0 = pl.multiple_of(tile_ref[item] * EXPERT_TILE, EXPERT_TILE)
        return pltpu.make_async_copy(xs_hbm.at[pl.ds(row0, EXPERT_TILE)], xbuf_ref.at[slot], xsem.at[slot])

    @pl.when(i == 0)
    def _():
        for ahead in range(XS_RING - 1):
            pl.when(ahead < n_items)(lambda: tile_copy(ahead).start())

    @pl.when(i + XS_RING - 1 < n_items)
    def _():
        tile_copy(i + XS_RING - 1).start()

    tile_copy(i).wait()
    xs_ref = xbuf_ref.at[lax.rem(i, XS_RING)]

    @pl.when((i == 0) | (exp_ref[i] != exp_ref[jnp.maximum(i - 1, 0)]))
    def _():
        wgb_ref[...] = wg_ref[...].astype(BF16)
        wub_ref[...] = wu_ref[...].astype(BF16)
        wdb_ref[...] = wd_ref[...].astype(BF16)

    def ffn():
        x_lo, x_hi = _unpack_halves(xs_ref[...])
        x_lo, x_hi = x_lo.astype(BF16), x_hi.astype(BF16)
        n = x_lo.shape[1]
        a = _dot(x_lo, wgb_ref[0:n, :]) + _dot(x_hi, wgb_ref[n:, :])
        u = _dot(x_lo, wub_ref[0:n, :]) + _dot(x_hi, wub_ref[n:, :])
        return _dot(((a * _sigmoid(a)) * u).astype(BF16), wdb_ref[...])

    @pl.when(i < used_ref[0])
    def _():
        ys_ref[...] = _pack_halves(ffn())


def _experts(xs, layout, layer, exp_wg, exp_wu, exp_wd):
    half = xs.shape[1]
    d, hid = exp_wg.shape[-2:]
    tile, expert, n_used = layout
    n_tiles = tile.shape[0]
    weights = lambda i, t, e, u: (layer, e[i], 0, 0)
    grid_spec = pltpu.PrefetchScalarGridSpec(
        num_scalar_prefetch=3,
        grid=(n_tiles,),
        in_specs=[pl.BlockSpec(memory_space=pl.ANY),
                  pl.BlockSpec((None, None, d, hid), weights), pl.BlockSpec((None, None, d, hid), weights),
                  pl.BlockSpec((None, None, hid, d), weights)],
        out_specs=pl.BlockSpec((EXPERT_TILE, half), lambda i, t, e, u: (t[i], 0)),
        scratch_shapes=[pltpu.VMEM((d, hid), BF16), pltpu.VMEM((d, hid), BF16), pltpu.VMEM((hid, d), BF16),
                        pltpu.VMEM((XS_RING, EXPERT_TILE, half), jnp.uint32),
                        pltpu.SemaphoreType.DMA((XS_RING,))])
    return pl.pallas_call(
        _experts_kernel,
        grid_spec=grid_spec,
        out_shape=jax.ShapeDtypeStruct((n_tiles * EXPERT_TILE, half), jnp.uint32),
        compiler_params=_cparams(("arbitrary",)),
        name="experts",
    )(tile, expert, n_used, xs, exp_wg, exp_wu, exp_wd)


def _moe_out_tile(yg_ref, w_ref, v_ref, sg_ref, su_ref, sd_ref, h1_ref, mod_ref, gpost_ref,
                  *, tiles_per_batch, n_batch, d):
    i = pl.program_id(0)
    x = v_ref[...]
    a = _dot(x, sg_ref[...])
    f = _dot(((a * _sigmoid(a)) * _dot(x, su_ref[...])).astype(BF16), sd_ref[...])
    n = d // 2
    f_lo, f_hi = f[:, :n], f[:, n:]
    w = w_ref[...]
    for k in range(TOP_K):
        y_lo, y_hi = _unpack_halves(yg_ref[k])
        wk = w[:, k:k + 1]
        f_lo = f_lo + wk * y_lo
        f_hi = f_hi + wk * y_hi
    f = jnp.concatenate([f_lo, f_hi], axis=1)
    bi = jnp.minimum(i // tiles_per_batch, n_batch)
    g2 = mod_ref[pl.ds(bi, 1), 5 * d:6 * d]
    return h1_ref[...] + g2 * _rms(f, gpost_ref[...])


N_MOE_OUT_IN = 9
N_INPROJ_IN = 9


def _moe_out_kernel(*refs, **kw):
    refs[N_MOE_OUT_IN][...] = _moe_out_tile(*refs[:N_MOE_OUT_IN], **kw)


def _moe_out_inproj_kernel(*refs, n_batch, d, tiles_per_batch):
    dims = dict(tiles_per_batch=tiles_per_batch, n_batch=n_batch, d=d)
    h = _moe_out_tile(*refs[:N_MOE_OUT_IN], **dims)
    n_in = N_MOE_OUT_IN + N_INPROJ_IN
    refs[n_in][...] = h
    _inproj_tile(h, *refs[N_MOE_OUT_IN:n_in], *refs[n_in + 1:], **dims)


def _moe_out_specs(yg, w, v, sg, su, sd, h1, mods, gpost, tm):
    d = v.shape[1]
    rows = lambda i: (i, 0)
    const = lambda i: (0, 0)
    once = dict(pipeline_mode=pl.Buffered(1))
    return [pl.BlockSpec((TOP_K, tm, d // 2), lambda i: (0, i, 0)), pl.BlockSpec((tm, TOP_K), rows),
            pl.BlockSpec((tm, d), rows),
            pl.BlockSpec(sg.shape, const, **once), pl.BlockSpec(su.shape, const, **once),
            pl.BlockSpec(sd.shape, const, **once),
            pl.BlockSpec((tm, d), rows), pl.BlockSpec(mods.shape, const), pl.BlockSpec((1, d), const)]


def _moe_out(moe_args, *, n_batch, seq):
    v = moe_args[2]
    n_rows, d = v.shape
    return pl.pallas_call(
        functools.partial(_moe_out_kernel, tiles_per_batch=seq // TM, n_batch=n_batch, d=d),
        grid=(n_rows // TM,),
        in_specs=_moe_out_specs(*moe_args, TM),
        out_specs=pl.BlockSpec((TM, d), lambda i: (i, 0)),
        out_shape=jax.ShapeDtypeStruct((n_rows, d), F32),
        compiler_params=_cparams(("arbitrary",)),
        name="moe_out",
    )(*moe_args)


FUSED_TM = 256


def _moe_out_inproj(moe_args, inproj_args, *, n_batch, seq):
    v = moe_args[2]
    n_rows, d = v.shape
    tm = FUSED_TM
    in_specs, out_specs, out_shape = _inproj_specs(n_rows, d, *inproj_args, tm=tm, n_batch=n_batch, seq=seq)
    rows = lambda i: (i, 0)
    return pl.pallas_call(
        functools.partial(_moe_out_inproj_kernel, tiles_per_batch=seq // tm, n_batch=n_batch, d=d),
        grid=(n_rows // tm,),
        in_specs=_moe_out_specs(*moe_args, tm) + in_specs,
        out_specs=[pl.BlockSpec((tm, d), rows)] + out_specs,
        out_shape=[jax.ShapeDtypeStruct((n_rows, d), F32)] + out_shape,
        compiler_params=_cparams(("arbitrary",)),
        name="moe_out_inproj",
    )(*moe_args, *inproj_args)


def _moe_routed(v, vp, layer, rw, rb, exp_wg, exp_wu, exp_wd):
    n_rows = v.shape[0]
    filler_rows = N_EXPERTS * EXPERT_TILE
    assert (n_rows * TOP_K) % EXPERT_TILE == 0 and filler_rows % (TOP_K * SC_WINDOW) == 0
    n_tiles = (n_rows * TOP_K + filler_rows) // EXPERT_TILE
    eidx, rank, w_t, cnt = _router(v, rw, rb, 0, n_rows // TM)
    slot_t = _slots(eidx, rank, cnt)
    tile, expert, n_used, filler = _expert_layout(cnt, n_tiles, filler_rows // (TOP_K * SC_WINDOW))
    xs = _sc_dispatch(vp, slot_t, filler, (n_tiles + 1) * EXPERT_TILE)
    ys = _experts(xs, (tile, expert, n_used), layer, exp_wg, exp_wu, exp_wd)
    return _sc_collect(ys, slot_t), w_t.T


def _rope_tables(seq):
    rows = seq // GRID_W
    row_id = np.repeat(np.arange(rows, dtype=np.float64), GRID_W)
    col_id = np.tile(np.arange(GRID_W, dtype=np.float64), rows)

    def tables(rot_dim):
        axis_dim = rot_dim // 2
        inv_freq = ROPE_BASE ** (-np.arange(0, axis_dim, 2, dtype=np.float64) / axis_dim)
        ang_r = row_id[:, None] * inv_freq[None, :]
        ang_c = col_id[:, None] * inv_freq[None, :]
        cos = np.concatenate([np.cos(ang_r), np.cos(ang_r), np.cos(ang_c), np.cos(ang_c)], axis=1)
        sin = np.concatenate([-np.sin(ang_r), np.sin(ang_r), -np.sin(ang_c), np.sin(ang_c)], axis=1)
        return cos, sin

    cos64, sin64 = tables(GQA_DIM)
    cos32, sin32 = tables(MLA_ROPE)
    ones = np.ones((seq, MLA_NOPE))
    pad = LANES - MLA_NOPE - MLA_ROPE
    cospe = np.concatenate([ones, cos32, np.ones((seq, pad))], axis=1)
    sinpe = np.concatenate([0 * ones, sin32, np.zeros((seq, pad))], axis=1)
    tab = np.concatenate([cos64, cos64, sin64, sin64, cospe, sinpe], axis=1)
    ident = np.concatenate([np.ones((TM, LANES)), np.zeros((TM, LANES)),
                            np.ones((TM, LANES)), np.zeros((TM, LANES))], axis=1)
    return jnp.asarray(np.concatenate([tab, ident], axis=0), F32)


def _pack_w_in(w):
    d = w.shape[0]
    sizes = (MLA_Q_LORA, MLA_KV_LORA, MLA_ROPE, GQA_HEADS * GQA_DIM, GQA_KV_HEADS * GQA_DIM,
             GQA_KV_HEADS * GQA_DIM, RET_HEADS * RET_QK, RET_HEADS * RET_QK, RET_HEADS * RET_V,
             RET_HEADS * RET_V, 3 * d)
    offs, parts = 0, []
    for s in sizes:
        parts.append(w[:, offs:offs + s])
        offs += s
    cq, ckv, kpe, gq, gk, gv, rq, rk, rv, rg, gates = parts

    def twice(m):
        heads = [m[:, i * GQA_DIM:(i + 1) * GQA_DIM] for i in range(GQA_KV_HEADS)]
        return jnp.concatenate([hh for hd in heads for hh in (hd, hd)], axis=1)

    kpe_slab = jnp.concatenate([jnp.zeros((d, MLA_NOPE), F32), kpe,
                                jnp.zeros((d, LANES - MLA_NOPE - MLA_ROPE), F32)], axis=1)
    packed = jnp.concatenate([cq, ckv, kpe_slab, gq * (GQA_DIM ** -0.5 * LOG2_E), twice(gk), twice(gv),
                              rq, rk * RET_QK ** -0.5, rv, rg, gates], axis=1)
    assert packed.shape[1] == W_COLS
    return packed.astype(BF16)


def _pack_mla_up(w_uq, w_ukv):
    r = w_uq.shape[0]
    dq = MLA_NOPE + MLA_ROPE
    wq = jnp.pad(w_uq.reshape(r, MLA_HEADS, dq), ((0, 0), (0, 0), (0, LANES - dq))).reshape(r, MLA_HEADS * LANES)
    kv = w_ukv.reshape(r, MLA_HEADS, MLA_NOPE + MLA_V)
    wk = jnp.pad(kv[:, :, :MLA_NOPE], ((0, 0), (0, 0), (0, LANES - MLA_NOPE))).reshape(r, MLA_HEADS * LANES)
    wv = kv[:, :, MLA_NOPE:]
    zeros = jnp.zeros_like(wv)
    even = jnp.concatenate([wv, zeros], axis=2)
    odd = jnp.concatenate([zeros, wv], axis=2)
    wv = jnp.where((jnp.arange(MLA_HEADS) % 2 == 0)[None, :, None], even, odd).reshape(r, MLA_HEADS * LANES)
    return wq.astype(BF16), wk.astype(BF16), wv.astype(BF16)


def kernel(x, c, ctx, c_ctx, ada_w, ada_b, norm_mix_pre, norm_mix_post, norm_ffn_pre, norm_ffn_post, w_in, mla_q_norm, mla_w_uq, mla_kv_norm, mla_w_ukv, gqa_sink, ret_decay_fwd, ret_decay_bwd, w_br_mla, w_br_gqa, w_br_ret, w_out, router_w, router_bias, exp_w_gate, exp_w_up, exp_w_down, shared_w_gate, shared_w_up, shared_w_down):
    n_batch, seq, d = x.shape
    n_ctx = ctx.shape[1]
    depth = ada_w.shape[0]
    n_lat_rows = n_batch * seq
    assert seq % TM == 0 and (n_batch * n_ctx) % TM == 0 and seq % ATT_TQ == 0 and seq % n_ctx == 0
    assert n_batch < MOD_ROWS and seq % GRID_W == 0

    cond = jnp.zeros((MOD_ROWS, d), F32).at[:n_batch].set(c).at[n_batch].set(c_ctx)
    mods_all = _adaln(cond, ada_w, ada_b)
    rope = _rope_tables(seq)
    h = jnp.concatenate([x.reshape(n_lat_rows, d), ctx.reshape(n_batch * n_ctx, d)], axis=0)
    row = lambda p: p.reshape(1, -1)
    dims = dict(n_batch=n_batch, seq=seq)

    def inproj_args(l):
        return (mods_all[l], row(norm_mix_pre[l]), _pack_w_in(w_in[l]), rope, row(mla_q_norm[l]),
                row(mla_kv_norm[l]), *_pack_mla_up(mla_w_uq[l], mla_w_ukv[l]))

    projected = _inproj(h, inproj_args(0), **dims)
    for l in range(depth):
        last = l == depth - 1
        mods = mods_all[l]
        mq, mk, mv, gqa, ret, gates = projected
        a = _mla_attention(mq, mk, mv, ctx=n_ctx, with_ctx_queries=not last, **dims)
        sink_tab = jnp.broadcast_to(gqa_sink[l].astype(F32)[:, None] * LOG2_E, (GQA_HEADS, LANES))
        w = _window_attention(gqa, sink_tab, ctx=n_ctx, with_ctx_queries=not last, **dims)
        lg = jnp.concatenate([jax.nn.log_sigmoid(ret_decay_fwd[l].astype(F32)),
                              jax.nn.log_sigmoid(ret_decay_bwd[l].astype(F32))])
        o_f, o_b = _retention(ret, jnp.broadcast_to(lg[:, None], (2 * RET_HEADS, LANES)), ctx=n_ctx, **dims)
        n_rows = n_lat_rows if last else h.shape[0]
        h1, v, vp = _merge(a, w, o_f, o_b, ret, gates, h, mods, row(norm_mix_post[l]), row(norm_ffn_pre[l]),
                           w_br_mla[l].astype(BF16), w_br_gqa[l].astype(BF16), w_br_ret[l].astype(BF16),
                           w_out[l].astype(BF16), n_rows=n_rows, **dims)
        yg, gate_w = _moe_routed(v, vp, l, router_w[l].T.astype(BF16), router_bias[l].astype(F32).reshape(-1, 1),
                                 exp_w_gate, exp_w_up, exp_w_down)
        moe_args = (yg, gate_w, v, shared_w_gate[l].astype(BF16), shared_w_up[l].astype(BF16),
                    shared_w_down[l].astype(BF16), h1, mods, row(norm_ffn_post[l]))
        if last:
            h = _moe_out(moe_args, **dims)
        else:
            h, *projected = _moe_out_inproj(moe_args, inproj_args(l + 1), **dims)
    return h[:n_lat_rows].reshape(n_batch, seq, d)
```

```python
import functools

import numpy as np
import jax
import jax.numpy as jnp
from jax import lax
from jax.experimental import pallas as pl
from jax.experimental.pallas import tpu as pltpu
from jax.experimental.pallas import tpu_sc as plsc

F32 = jnp.float32
BF16 = jnp.bfloat16

GRID_W = 64
ROPE_BASE = 10000.0
NORM_EPS = 1e-6
NEG_INF = -1e30
LOG2_E = 1.4426950408889634
N_MOD = 6
MLA_HEADS, MLA_NOPE, MLA_ROPE, MLA_V = 8, 64, 32, 64
MLA_Q_LORA, MLA_KV_LORA = 256, 256
GQA_HEADS, GQA_KV_HEADS, GQA_DIM, WINDOW = 8, 2, 64, 128
RET_HEADS, RET_QK, RET_V, RET_CHUNK = 4, 64, 128, 128
N_EXPERTS, N_EXPERT_GROUPS, TOPK_GROUPS, TOP_K = 64, 8, 4, 8
EXPERTS_PER_GROUP = N_EXPERTS // N_EXPERT_GROUPS
ROUTED_SCALE = 2.5

LANES = 128
TM = 512
ATT_TQ = 512
WIN_TQ = 256
RET_TILE = 256
MOD_ROWS = 8
V7X_VMEM_LIMIT = 56 * 1024 * 1024

C_CQ, C_CKV, C_KPE = 0, 256, 512
C_G = 640
C_R = 1664
C_GATE = 3200
W_COLS = 6272


def _cparams(sem):
    return pltpu.CompilerParams(dimension_semantics=sem, vmem_limit_bytes=V7X_VMEM_LIMIT)


def _rms(x, g):
    return x * lax.rsqrt(jnp.mean(x * x, axis=-1, keepdims=True) + NORM_EPS) * g


def _sigmoid(x):
    return 0.5 * jnp.tanh(0.5 * x) + 0.5


def _dot(a, b):
    return jnp.dot(a, b, preferred_element_type=F32)


def _dot_nt(a, b):
    return lax.dot_general(a, b, (((1,), (1,)), ((), ())), preferred_element_type=F32)


def _dot_tn(a, b):
    return lax.dot_general(a, b, (((0,), (0,)), ((), ())), preferred_element_type=F32)


def _rope(x, cos, sin, half):
    n = x.shape[-1]
    reps = n // LANES
    if reps > 1:
        cos = jnp.concatenate([cos] * reps, axis=1)
        sin = jnp.concatenate([sin] * reps, axis=1)
    lane = lax.broadcasted_iota(jnp.int32, x.shape, 1)
    up = pltpu.roll(x, half, 1)
    dn = pltpu.roll(x, n - half, 1)
    partner = jnp.where((lane & (2 * half - 1)) < half, dn, up)
    return x * cos + partner * sin


def _lane_lo(shape):
    return (lax.broadcasted_iota(jnp.int32, shape, 1) & (LANES - 1)) < (LANES // 2)


def _pack_halves(x):
    n = x.shape[1] // 2
    bits = lambda t: lax.bitcast_convert_type(t.astype(BF16).astype(F32), jnp.uint32)
    return (bits(x[:, :n]) >> 16) | bits(x[:, n:])


def _unpack_halves(p):
    lo = lax.bitcast_convert_type(p << 16, F32)
    hi = lax.bitcast_convert_type(p & jnp.uint32(0xFFFF0000), F32)
    return lo, hi


def _ada_kernel(c_ref, w_ref, b_ref, o_ref):
    c = c_ref[...]
    s = c * _sigmoid(c)
    o_ref[...] = _dot(s.astype(BF16), w_ref[...].astype(BF16)) + b_ref[...]


def _adaln(cond, ada_w, ada_b):
    n_layers, d, n = ada_w.shape
    tn = 1024
    return pl.pallas_call(
        _ada_kernel,
        grid=(n_layers, n // tn),
        in_specs=[pl.BlockSpec((MOD_ROWS, d), lambda l, j: (0, 0)),
                  pl.BlockSpec((None, d, tn), lambda l, j: (l, 0, j)),
                  pl.BlockSpec((None, 1, tn), lambda l, j: (l, 0, j))],
        out_specs=pl.BlockSpec((None, MOD_ROWS, tn), lambda l, j: (l, 0, j)),
        out_shape=jax.ShapeDtypeStruct((n_layers, MOD_ROWS, n), F32),
        compiler_params=_cparams(("arbitrary", "arbitrary")),
        name="adaln",
    )(cond, ada_w, ada_b.reshape(n_layers, 1, n))


def _inproj_tile(h, mod_ref, gpre_ref, w_ref, rope_ref, qn_ref, kvn_ref, wuq_ref, wuk_ref, wuv_ref,
                 mq_ref, mk_ref, mv_ref, gqa_ref, ret_ref, gate_ref, *, tiles_per_batch, n_batch, d):
    i = pl.program_id(0)
    bi = jnp.minimum(i // tiles_per_batch, n_batch)
    sh = mod_ref[pl.ds(bi, 1), 0:d]
    sc = mod_ref[pl.ds(bi, 1), d:2 * d]
    u = (_rms(h, gpre_ref[...]) * (1.0 + sc) + sh).astype(BF16)

    cos64 = rope_ref[:, 0:LANES]
    sin64 = rope_ref[:, LANES:2 * LANES]
    cospe = rope_ref[:, 2 * LANES:3 * LANES]
    sinpe = rope_ref[:, 3 * LANES:4 * LANES]

    c = _dot(u, w_ref[:, C_CQ:C_G])
    kpe = _rope(c[:, C_KPE:C_G], cospe, sinpe, MLA_ROPE // 4)
    qn = _rms(c[:, C_CQ:C_CKV], qn_ref[...]).astype(BF16)
    q = _rope(_dot(qn, wuq_ref[...]), cospe, sinpe, MLA_ROPE // 4)
    mq_ref[...] = (q * ((MLA_NOPE + MLA_ROPE) ** -0.5 * LOG2_E)).astype(mq_ref.dtype)
    kvn = _rms(c[:, C_CKV:C_KPE], kvn_ref[...]).astype(BF16)
    k = _dot(kvn, wuk_ref[...]) + jnp.concatenate([kpe] * MLA_HEADS, axis=1)
    mk_ref[...] = k.astype(mk_ref.dtype)
    v = _dot(kvn, wuv_ref[...])
    lane = lax.broadcasted_iota(jnp.int32, v.shape, 1)
    value_lane = ((lane & (LANES - 1)) < MLA_V) == (((lane >> (LANES.bit_length() - 1)) & 1) == 0)
    mv_ref[...] = jnp.where(value_lane, v, 1.0).astype(mv_ref.dtype)

    g = _dot(u, w_ref[:, C_G:C_R])
    n_qk = GQA_HEADS * GQA_DIM + 2 * GQA_KV_HEADS * GQA_DIM
    gqa_ref[:, 0:n_qk] = _rope(g[:, 0:n_qk], cos64, sin64, GQA_DIM // 4).astype(gqa_ref.dtype)
    gqa_ref[:, n_qk:] = g[:, n_qk:].astype(gqa_ref.dtype)

    r = _dot(u, w_ref[:, C_R:C_GATE])
    n_qk = 2 * RET_HEADS * RET_QK
    ret_ref[:, 0:n_qk] = _rope(r[:, 0:n_qk], cos64, sin64, RET_QK // 4).astype(ret_ref.dtype)
    ret_ref[:, n_qk:] = r[:, n_qk:].astype(ret_ref.dtype)

    gate_ref[...] = _dot(u, w_ref[:, C_GATE:W_COLS]).astype(gate_ref.dtype)


def _inproj_kernel(hl_ref, hc_ref, *refs, tiles_per_batch, n_batch, d):
    latent = pl.program_id(0) < tiles_per_batch * n_batch
    _inproj_tile(jnp.where(latent, hl_ref[...], hc_ref[...]), *refs,
                 tiles_per_batch=tiles_per_batch, n_batch=n_batch, d=d)


def _inproj_specs(t, d, mods, gpre, w_all, rope, qn, kvn, wuq, wuk, wuv, *, tm, n_batch, seq):
    tiles_per_batch = seq // tm
    n_lat_tiles = n_batch * tiles_per_batch
    const = lambda i: (0, 0)
    rows = lambda i: (i, 0)
    rope_idx = lambda i: (jnp.where(i < n_lat_tiles, i % tiles_per_batch, tiles_per_batch), 0)
    once = dict(pipeline_mode=pl.Buffered(1))
    hq = MLA_HEADS * LANES
    outs = [jax.ShapeDtypeStruct((t, hq), BF16), jax.ShapeDtypeStruct((t, hq), BF16),
            jax.ShapeDtypeStruct((t, hq), BF16),
            jax.ShapeDtypeStruct((t, C_R - C_G), BF16),
            jax.ShapeDtypeStruct((t, C_GATE - C_R), F32),
            jax.ShapeDtypeStruct((t, W_COLS - C_GATE), BF16)]
    in_specs = [pl.BlockSpec(mods.shape, const),
                pl.BlockSpec((1, d), const),
                pl.BlockSpec(w_all.shape, const, **once),
                pl.BlockSpec((tm, 4 * LANES), rope_idx),
                pl.BlockSpec(qn.shape, const), pl.BlockSpec(kvn.shape, const),
                pl.BlockSpec(wuq.shape, const, **once), pl.BlockSpec(wuk.shape, const, **once),
                pl.BlockSpec(wuv.shape, const, **once)]
    return in_specs, [pl.BlockSpec((tm, o.shape[1]), rows) for o in outs], outs


def _inproj(h_lat, h_ctx, inproj_args, *, n_batch, seq):
    d = h_lat.shape[1]
    t = h_lat.shape[0] + h_ctx.shape[0]
    n_lat_tiles = h_lat.shape[0] // TM
    in_specs, out_specs, out_shape = _inproj_specs(t, d, *inproj_args, tm=TM, n_batch=n_batch, seq=seq)
    return pl.pallas_call(
        functools.partial(_inproj_kernel, tiles_per_batch=seq // TM, n_batch=n_batch, d=d),
        grid=(t // TM,),
        in_specs=[pl.BlockSpec((TM, d), lambda i: (jnp.minimum(i, n_lat_tiles - 1), 0)),
                  pl.BlockSpec((TM, d), lambda i: (jnp.maximum(i - n_lat_tiles, 0), 0))] + in_specs,
        out_specs=out_specs,
        out_shape=out_shape,
        compiler_params=_cparams(("arbitrary",)),
        name="inproj",
    )(h_lat, h_ctx, *inproj_args)


MLA_HEADS_PER_STEP = 4


def _mla_kernel(q_ref, kl_ref, kc_ref, vl_ref, vc_ref, o_ref, s_ref, p_ref, *, with_lat):
    n_ctx = kc_ref.shape[0]

    def body(with_lat):
        n_keys = n_ctx + (kl_ref.shape[0] if with_lat else 0)

        def scores(h):
            sl = slice(h * LANES, (h + 1) * LANES)
            s_ref[h % 2, :, 0:n_ctx] = _dot_nt(q_ref[:, sl], kc_ref[:, sl])
            if with_lat:
                s_ref[h % 2, :, n_ctx:n_keys] = _dot_nt(q_ref[:, sl], kl_ref[:, sl])

        def probs(h):
            s = s_ref[h % 2, :, 0:n_keys]
            p_ref[h % 2, :, 0:n_keys] = jnp.exp2(s - jnp.max(s, axis=-1, keepdims=True)).astype(BF16)

        def weighted(h):
            sl = slice(h * LANES, (h + 1) * LANES)
            o = _dot(p_ref[h % 2, :, 0:n_ctx], vc_ref[:, sl])
            if with_lat:
                o = o + _dot(p_ref[h % 2, :, n_ctx:n_keys], vl_ref[:, sl])
            return o / pltpu.roll(o, LANES // 2, 1)

        outs = [None] * MLA_HEADS_PER_STEP
        scores(0)
        for h in range(MLA_HEADS_PER_STEP):
            if h + 1 < MLA_HEADS_PER_STEP:
                scores(h + 1)
            probs(h)
            outs[h] = weighted(h)
        for pr in range(MLA_HEADS_PER_STEP // 2):
            even, odd = outs[2 * pr], outs[2 * pr + 1]
            o_ref[:, pr * LANES:(pr + 1) * LANES] = jnp.where(_lane_lo(even.shape), even, odd).astype(o_ref.dtype)

    body(with_lat)


def _mla_call(mq, mk, mv, *, n_batch, seq, ctx, latent_queries):
    hps = MLA_HEADS_PER_STEP
    ctx_blk0 = n_batch * seq // ctx
    tq = ATT_TQ if latent_queries else ctx
    nq = seq // tq if latent_queries else 1
    q_blk0 = 0 if latent_queries else n_batch * seq // tq
    q_idx = lambda b, g, i: (q_blk0 + b * nq + i, g)
    ctx_idx = lambda b, g, i: (ctx_blk0 + b, g)
    lat_idx = (lambda b, g, i: (b, g)) if latent_queries else ctx_idx
    n_lat = seq if latent_queries else ctx
    n_keys = ctx + (seq if latent_queries else 0)
    in_specs = [pl.BlockSpec((tq, hps * LANES), q_idx),
                pl.BlockSpec((n_lat, hps * LANES), lat_idx), pl.BlockSpec((ctx, hps * LANES), ctx_idx),
                pl.BlockSpec((n_lat, hps * LANES), lat_idx), pl.BlockSpec((ctx, hps * LANES), ctx_idx)]
    return pl.pallas_call(
        functools.partial(_mla_kernel, with_lat=latent_queries),
        grid=(n_batch, MLA_HEADS // hps, nq),
        in_specs=in_specs,
        out_specs=pl.BlockSpec((tq, hps * MLA_V), lambda b, g, i: (b * nq + i, g)),
        out_shape=jax.ShapeDtypeStruct((n_batch * nq * tq, MLA_HEADS * MLA_V), BF16),
        scratch_shapes=[pltpu.VMEM((2, tq, n_keys), F32), pltpu.VMEM((2, tq, n_keys), BF16)],
        compiler_params=_cparams(("arbitrary", "arbitrary", "arbitrary")),
        name="mla_attn" if latent_queries else "mla_attn_ctx",
    )(mq, mk, mk, mv, mv)


def _mla_attention(mq, mk, mv, *, n_batch, seq, ctx, with_ctx_queries):
    dims = dict(n_batch=n_batch, seq=seq, ctx=ctx)
    lat = _mla_call(mq, mk, mv, latent_queries=True, **dims)
    return lat, (_mla_call(mq, mk, mv, latent_queries=False, **dims) if with_ctx_queries else lat)


def _win_kernel(q_ref, kp_ref, kcur_ref, kn_ref, vp_ref, vcur_ref, vn_ref, kc_ref, vc_ref, sink_ref, o_ref,
                s_ref, p_ref, *, seq, with_lat):
    i = pl.program_id(1)
    tq = q_ref.shape[0]
    group = GQA_HEADS // GQA_KV_HEADS

    def body(with_lat):
        n_ctx = kc_ref.shape[0]
        n_keys = n_ctx + (tq + 2 * WINDOW if with_lat else 0)
        if with_lat:
            q_pos = i * tq + lax.broadcasted_iota(jnp.int32, (tq, n_keys), 0)
            k_pos = i * tq - WINDOW - n_ctx + lax.broadcasted_iota(jnp.int32, (tq, n_keys), 1)
            in_band = (jnp.abs(q_pos - k_pos) <= WINDOW) & (k_pos >= 0) & (k_pos < seq)
            valid = in_band | (lax.broadcasted_iota(jnp.int32, (tq, n_keys), 1) < n_ctx)
        lo = _lane_lo((tq, LANES))
        lo_k = _lane_lo((n_keys, LANES))
        keys, values = [], []
        for kv in range(GQA_KV_HEADS):
            sl = slice(kv * LANES, (kv + 1) * LANES)
            if with_lat:
                k_all = jnp.concatenate([kc_ref[:, sl], kp_ref[:, sl], kcur_ref[:, sl], kn_ref[:, sl]], axis=0)
                v_all = jnp.concatenate([vc_ref[:, sl], vp_ref[:, sl], vcur_ref[:, sl], vn_ref[:, sl]], axis=0)
            else:
                k_all, v_all = kc_ref[:, sl], vc_ref[:, sl]
            keys.append(k_all)
            one = jnp.ones_like(v_all)
            values.append((jnp.where(lo_k, v_all, one), jnp.where(lo_k, one, v_all)))

        def scores(hd):
            kv, pair = hd // group, hd // 2
            qp = q_ref[:, pair * LANES:(pair + 1) * LANES]
            qm = jnp.where(lo if hd % 2 == 0 else jnp.logical_not(lo), qp, jnp.zeros_like(qp))
            s = _dot_nt(qm, keys[kv])
            s_ref[hd % 2, :, 0:n_keys] = jnp.where(valid, s, NEG_INF) if with_lat else s

        def probs(hd):
            s = s_ref[hd % 2, :, 0:n_keys]
            m = jnp.maximum(jnp.max(s, axis=-1, keepdims=True), sink_ref[hd:hd + 1, 0:1])
            p_ref[hd % 2, :, 0:n_keys] = jnp.exp2(s - m).astype(BF16)
            return jnp.exp2(sink_ref[hd:hd + 1, 0:1] - m)

        def weighted(hd, sink_term):
            o = _dot(p_ref[hd % 2, :, 0:n_keys], values[hd // group][hd % 2])
            return o / (pltpu.roll(o, LANES // 2, 1) + sink_term)

        outs = [None] * GQA_HEADS
        scores(0)
        for hd in range(GQA_HEADS):
            if hd + 1 < GQA_HEADS:
                scores(hd + 1)
            outs[hd] = weighted(hd, probs(hd))
        for pair in range(GQA_HEADS // 2):
            o_ref[:, pair * LANES:(pair + 1) * LANES] = jnp.where(
                lo, outs[2 * pair], outs[2 * pair + 1]).astype(o_ref.dtype)

    body(with_lat)


def _win_call(gqa, sink_tab, *, n_batch, seq, ctx, latent_queries):
    tq = WIN_TQ if latent_queries else ctx
    nq = seq // tq if latent_queries else 1
    q_blk0 = 0 if latent_queries else n_batch * seq // tq
    per_tile = tq // WINDOW
    n_win_blocks = seq // WINDOW
    ctx_blk0 = n_batch * seq // ctx
    nqk = GQA_HEADS * GQA_DIM
    kw = 2 * GQA_KV_HEADS * GQA_DIM
    k_col, v_col = nqk // kw, nqk // kw + 1
    q_idx = lambda b, i: (q_blk0 + b * nq + i, 0)
    cidx = lambda col: (lambda b, i: (ctx_blk0 + b, col))
    if latent_queries:
        cur = lambda col: (lambda b, i: (b * nq + i, col))
        prev = lambda col: (lambda b, i: (b * n_win_blocks + jnp.maximum(per_tile * i - 1, 0), col))
        nxt = lambda col: (lambda b, i: (b * n_win_blocks + jnp.minimum(per_tile * (i + 1), n_win_blocks - 1), col))
        band = [((WINDOW, kw), prev), ((tq, kw), cur), ((WINDOW, kw), nxt)]
    else:
        band = [((ctx, kw), cidx)] * 3
    n_keys = ctx + (tq + 2 * WINDOW if latent_queries else 0)
    in_specs = ([pl.BlockSpec((tq, nqk), q_idx)]
                + [pl.BlockSpec(shape, idx(k_col)) for shape, idx in band]
                + [pl.BlockSpec(shape, idx(v_col)) for shape, idx in band]
                + [pl.BlockSpec((ctx, kw), cidx(k_col)), pl.BlockSpec((ctx, kw), cidx(v_col)),
                   pl.BlockSpec(sink_tab.shape, lambda b, i: (0, 0))])
    return pl.pallas_call(
        functools.partial(_win_kernel, seq=seq, with_lat=latent_queries),
        grid=(n_batch, nq),
        in_specs=in_specs,
        out_specs=pl.BlockSpec((tq, nqk), lambda b, i: (b * nq + i, 0)),
        out_shape=jax.ShapeDtypeStruct((n_batch * nq * tq, nqk), BF16),
        scratch_shapes=[pltpu.VMEM((2, tq, n_keys), F32), pltpu.VMEM((2, tq, n_keys), BF16)],
        compiler_params=_cparams(("arbitrary", "arbitrary")),
        name="win_attn" if latent_queries else "win_attn_ctx",
    )(*([gqa] * 9 + [sink_tab]))


def _window_attention(gqa, sink_tab, *, n_batch, seq, ctx, with_ctx_queries):
    dims = dict(n_batch=n_batch, seq=seq, ctx=ctx)
    lat = _win_call(gqa, sink_tab, latent_queries=True, **dims)
    return lat, (_win_call(gqa, sink_tab, latent_queries=False, **dims) if with_ctx_queries else lat)


def _ret_kernel(f_ref, b_ref, lg_ref, of_ref, ob_ref, sf_ref, sb_ref, qdec_ref, kdec_ref, cdec_ref, inner_ref):
    @pl.when(pl.program_id(1) == 0)
    def _():
        sf_ref[...] = jnp.zeros_like(sf_ref)
        sb_ref[...] = jnp.zeros_like(sb_ref)

    L = f_ref.shape[0]
    lo = _lane_lo((L, LANES))
    srow_lo = lax.broadcasted_iota(jnp.int32, (LANES, LANES), 0) < RET_QK
    nq = RET_HEADS * RET_QK
    n_pairs = RET_HEADS // 2

    @pl.when(pl.program_id(1) == 0)
    def _():
        ii = lax.broadcasted_iota(jnp.int32, (L, L), 0)
        jj = lax.broadcasted_iota(jnp.int32, (L, L), 1)
        row = lax.broadcasted_iota(jnp.int32, (L, LANES), 0).astype(F32)
        for direction, forward in enumerate((True, False)):
            dist = ii - jj if forward else jj - ii
            distf = jnp.maximum(dist, 0).astype(F32)
            for pr in range(n_pairs):
                r0 = direction * RET_HEADS + 2 * pr
                lg = [lg_ref[r0 + e:r0 + e + 1, :] for e in range(2)]
                lg_lane = jnp.where(lo, lg[0], lg[1])
                qdec_ref[direction * n_pairs + pr] = jnp.exp(lg_lane * ((row + 1.0) if forward else (L - row)))
                kdec_ref[direction * n_pairs + pr] = jnp.exp(lg_lane * ((L - 1.0 - row) if forward else row))
                cdec_ref[direction * n_pairs + pr] = jnp.where(srow_lo, jnp.exp(lg[0] * float(L)),
                                                               jnp.exp(lg[1] * float(L)))
                for e in range(2):
                    inner_ref[r0 + e] = jnp.where(dist >= 0, jnp.exp(lg[e][:, 0:1] * distf), 0.0)

    def scan_chunk(x_ref, o_ref, s_ref, direction):
        for pr in range(n_pairs):
            q = x_ref[:, pr * LANES:(pr + 1) * LANES].astype(F32)
            k = x_ref[:, nq + pr * LANES:nq + (pr + 1) * LANES].astype(F32)
            qd = q * qdec_ref[direction * n_pairs + pr]
            kdb = (k * kdec_ref[direction * n_pairs + pr]).astype(BF16)
            kb = k.astype(BF16)
            state = s_ref[pr]
            state_b = state.astype(BF16)
            upd = []
            for e in range(2):
                hd = 2 * pr + e
                keep = lo if e == 0 else jnp.logical_not(lo)
                v = x_ref[:, 2 * nq + hd * RET_V:2 * nq + (hd + 1) * RET_V].astype(BF16)
                attn = _dot_nt(jnp.where(keep, q, 0.0).astype(BF16), kb) * inner_ref[direction * RET_HEADS + hd]
                o = _dot(attn.astype(BF16), v) + _dot(jnp.where(keep, qd, 0.0).astype(BF16), state_b)
                o_ref[:, hd * RET_V:(hd + 1) * RET_V] = o
                upd.append(_dot_tn(kdb, v))
            s_ref[pr] = state * cdec_ref[direction * n_pairs + pr] + jnp.where(srow_lo, upd[0], upd[1])

    scan_chunk(f_ref, of_ref, sf_ref, 0)
    scan_chunk(b_ref, ob_ref, sb_ref, 1)


def _retention(ret, lg_tab, *, n_batch, seq, ctx):
    t = ret.shape[0]
    L = RET_TILE
    assert seq % L == 0 and ctx % L == 0
    n_lat, n_ctx = seq // L, ctx // L
    ctx0 = n_batch * n_lat
    width = 2 * RET_HEADS * RET_QK + RET_HEADS * RET_V
    fwd = lambda b, s: (jnp.where(s < n_ctx, ctx0 + b * n_ctx + s, b * n_lat + s - n_ctx), 0)
    bwd = lambda b, s: (jnp.where(s < n_ctx, ctx0 + b * n_ctx + n_ctx - 1 - s, b * n_lat + n_lat - 1 - (s - n_ctx)), 0)
    out = jax.ShapeDtypeStruct((t, RET_HEADS * RET_V), F32)
    return pl.pallas_call(
        _ret_kernel,
        grid=(n_batch, n_lat + n_ctx),
        in_specs=[pl.BlockSpec((L, width), fwd), pl.BlockSpec((L, width), bwd),
                  pl.BlockSpec(lg_tab.shape, lambda b, s: (0, 0))],
        out_specs=[pl.BlockSpec((L, RET_HEADS * RET_V), fwd), pl.BlockSpec((L, RET_HEADS * RET_V), bwd)],
        out_shape=[out, out],
        scratch_shapes=[pltpu.VMEM((RET_HEADS // 2, LANES, RET_V), F32),
                        pltpu.VMEM((RET_HEADS // 2, LANES, RET_V), F32),
                        pltpu.VMEM((RET_HEADS, L, LANES), F32), pltpu.VMEM((RET_HEADS, L, LANES), F32),
                        pltpu.VMEM((RET_HEADS, LANES, RET_V), F32), pltpu.VMEM((2 * RET_HEADS, L, L), F32)],
        compiler_params=_cparams(("arbitrary", "arbitrary")),
        name="retention",
    )(ret, ret, lg_tab)


def _merge_kernel(al_ref, ac_ref, wl_ref, wc_ref, of_ref, ob_ref, rg_ref, gt_ref, hl_ref, hc_ref, mod_ref,
                  gpost_ref, gffn_ref, wa_ref, ww_ref, wr_ref, wo_ref, h1_ref, v_ref, vp_ref,
                  *, tiles_per_batch, n_batch, d):
    i = pl.program_id(0)
    bi = jnp.minimum(i // tiles_per_batch, n_batch)
    latent = i < tiles_per_batch * n_batch
    a_tile = jnp.where(latent, al_ref[...], ac_ref[...])
    w_tile = jnp.where(latent, wl_ref[...], wc_ref[...])
    h_tile = jnp.where(latent, hl_ref[...], hc_ref[...])
    o = of_ref[...] + ob_ref[...]
    normed = []
    for hd in range(RET_HEADS):
        oh = o[:, hd * RET_V:(hd + 1) * RET_V]
        dev = oh - jnp.mean(oh, axis=-1, keepdims=True)
        normed.append(dev * lax.rsqrt(jnp.mean(dev * dev, axis=-1, keepdims=True) + NORM_EPS))
    g = rg_ref[...].astype(F32)
    r = (g * _sigmoid(g)) * jnp.concatenate(normed, axis=1)
    y = (_sigmoid(gt_ref[:, 0:d].astype(F32)) * _dot(a_tile, wa_ref[...])
         + _sigmoid(gt_ref[:, d:2 * d].astype(F32)) * _dot(w_tile, ww_ref[...])
         + _sigmoid(gt_ref[:, 2 * d:3 * d].astype(F32)) * _dot(r.astype(BF16), wr_ref[...]))
    z = _dot(y.astype(BF16), wo_ref[...])
    g1 = mod_ref[pl.ds(bi, 1), 2 * d:3 * d]
    sh2 = mod_ref[pl.ds(bi, 1), 3 * d:4 * d]
    sc2 = mod_ref[pl.ds(bi, 1), 4 * d:5 * d]
    h1 = h_tile + g1 * _rms(z, gpost_ref[...])
    h1_ref[...] = h1
    v = _rms(h1, gffn_ref[...]) * (1.0 + sc2) + sh2
    v_ref[...] = v.astype(v_ref.dtype)
    vp_ref[...] = _pack_halves(v)


def _merge(a, w, o_f, o_b, ret, gates, h, mods, gpost, gffn, wa, ww, wr, wo, *, n_rows, n_batch, seq):
    d = h[0].shape[1]
    n_lat_tiles = n_batch * seq // TM
    rows = lambda i: (i, 0)
    lat_rows = lambda i: (jnp.minimum(i, n_lat_tiles - 1), 0)
    ctx_rows = lambda i: (jnp.maximum(i - n_lat_tiles, 0), 0)
    const = lambda i: (0, 0)
    rv = RET_HEADS * RET_V
    rg_col = (2 * RET_HEADS * RET_QK + rv) // rv
    outs = [jax.ShapeDtypeStruct((n_rows, d), F32), jax.ShapeDtypeStruct((n_rows, d), BF16),
            jax.ShapeDtypeStruct((n_rows, d // 2), jnp.uint32)]
    return pl.pallas_call(
        functools.partial(_merge_kernel, tiles_per_batch=seq // TM, n_batch=n_batch, d=d),
        grid=(n_rows // TM,),
        in_specs=[pl.BlockSpec((TM, a[0].shape[1]), lat_rows), pl.BlockSpec((TM, a[1].shape[1]), ctx_rows),
                  pl.BlockSpec((TM, w[0].shape[1]), lat_rows), pl.BlockSpec((TM, w[1].shape[1]), ctx_rows),
                  pl.BlockSpec((TM, rv), rows), pl.BlockSpec((TM, rv), rows),
                  pl.BlockSpec((TM, rv), lambda i: (i, rg_col)),
                  pl.BlockSpec((TM, 3 * d), rows), pl.BlockSpec((TM, d), lat_rows),
                  pl.BlockSpec((TM, d), lambda i: (h[2] + jnp.maximum(i - n_lat_tiles, 0), 0)),
                  pl.BlockSpec(mods.shape, const), pl.BlockSpec((1, d), const), pl.BlockSpec((1, d), const),
                  pl.BlockSpec(wa.shape, const), pl.BlockSpec(ww.shape, const),
                  pl.BlockSpec(wr.shape, const), pl.BlockSpec(wo.shape, const)],
        out_specs=[pl.BlockSpec((TM, o.shape[1]), rows) for o in outs],
        out_shape=outs,
        compiler_params=_cparams(("arbitrary",)),
        name="merge",
    )(a[0], a[1], w[0], w[1], o_f, o_b, ret, gates, h[0], h[1], mods, gpost, gffn, wa, ww, wr, wo)


def _router_kernel(v_ref, rw_ref, rb_ref, eidx_ref, rank_ref, w_ref, cnt_ref, carry_ref):
    @pl.when(pl.program_id(0) == 0)
    def _():
        carry_ref[...] = jnp.zeros_like(carry_ref)

    tm = v_ref.shape[0]
    scores = _sigmoid(_dot_nt(rw_ref[...], v_ref[...]))
    sel = scores + rb_ref[...]
    neg = -jnp.inf
    n_grp, per = N_EXPERT_GROUPS, EXPERTS_PER_GROUP

    sel3 = sel.reshape(n_grp, per, tm)
    member_id = lax.broadcasted_iota(jnp.int32, sel3.shape, 1)
    m1 = jnp.max(sel3, axis=1, keepdims=True)
    i1 = jnp.min(jnp.where(sel3 == m1, member_id, per), axis=1, keepdims=True)
    m2 = jnp.max(jnp.where(member_id == i1, neg, sel3), axis=1, keepdims=True)
    gscore = (m1 + m2).reshape(n_grp, tm)
    gid = lax.broadcasted_iota(jnp.int32, gscore.shape, 0)
    ahead = jnp.zeros(gscore.shape, jnp.int32)
    for gj in range(n_grp):
        other = gscore[gj:gj + 1, :]
        ahead = ahead + jnp.where((other > gscore) | ((other == gscore) & (gid > gj)), 1, 0)
    group_ok = (ahead < TOPK_GROUPS).reshape(n_grp, 1, tm)
    sel = jnp.where(group_ok, sel3, NEG_INF).reshape(N_EXPERTS, tm)

    eid = lax.broadcasted_iota(jnp.int32, sel.shape, 0)
    chosen = jnp.zeros(sel.shape, jnp.bool_)
    picks = []
    for _ in range(TOP_K):
        m = jnp.max(sel, axis=0, keepdims=True)
        idx = jnp.min(jnp.where(sel == m, eid, N_EXPERTS), axis=0, keepdims=True)
        hit = eid == idx
        chosen = chosen | hit
        sel = jnp.where(hit, neg, sel)
        picks.append(idx)
    w = jnp.where(chosen, scores, 0.0)
    gate = ROUTED_SCALE * w / jnp.sum(w, axis=0, keepdims=True)

    member = jnp.where(chosen, 1.0, 0.0)
    earlier = lax.broadcasted_iota(jnp.int32, (tm, tm), 0) < lax.broadcasted_iota(jnp.int32, (tm, tm), 1)
    pos = _dot(member.astype(BF16), jnp.where(earlier, 1.0, 0.0).astype(BF16)) + carry_ref[...]
    for k, idx in enumerate(picks):
        hit = eid == idx
        eidx_ref[k:k + 1, :] = idx
        rank_ref[k:k + 1, :] = jnp.sum(jnp.where(hit, pos, 0.0), axis=0, keepdims=True)
        w_ref[k:k + 1, :] = jnp.sum(jnp.where(hit, gate, 0.0), axis=0, keepdims=True)
    carry_ref[...] += jnp.sum(member, axis=1, keepdims=True)
    cnt_ref[...] = carry_ref[...]


def _router(v, rw_t, rb):
    n_rows, d = v.shape
    cols = lambda i: (0, i)
    const = lambda i: (0, 0)
    outs = [jax.ShapeDtypeStruct((TOP_K, n_rows), jnp.int32), jax.ShapeDtypeStruct((TOP_K, n_rows), F32),
            jax.ShapeDtypeStruct((TOP_K, n_rows), F32), jax.ShapeDtypeStruct((N_EXPERTS, 1), F32)]
    return pl.pallas_call(
        _router_kernel,
        grid=(n_rows // TM,),
        in_specs=[pl.BlockSpec((TM, d), lambda i: (i, 0)), pl.BlockSpec(rw_t.shape, const),
                  pl.BlockSpec(rb.shape, const)],
        out_specs=[pl.BlockSpec((TOP_K, TM), cols), pl.BlockSpec((TOP_K, TM), cols),
                   pl.BlockSpec((TOP_K, TM), cols), pl.BlockSpec((N_EXPERTS, 1), const)],
        out_shape=outs,
        scratch_shapes=[pltpu.VMEM((N_EXPERTS, 1), F32)],
        compiler_params=_cparams(("arbitrary",)),
        name="router",
    )(v, rw_t, rb)


def _slots_kernel(eidx_ref, rank_ref, cnt_ref, slot_ref):
    tm = eidx_ref.shape[1]
    eid = lax.broadcasted_iota(jnp.int32, (N_EXPERTS, tm), 0)
    for k in range(TOP_K):
        before = jnp.sum(jnp.where(eid < eidx_ref[k:k + 1, :], cnt_ref[...], 0.0), axis=0, keepdims=True)
        slot_ref[k:k + 1, :] = (before + rank_ref[k:k + 1, :]).astype(jnp.int32)


def _slots(eidx, rank, cnt):
    n_rows = eidx.shape[1]
    tm = next(c for c in (2048, 1024, 512, 256) if n_rows % c == 0)
    cols = lambda i: (0, i)
    return pl.pallas_call(
        _slots_kernel,
        grid=(n_rows // tm,),
        in_specs=[pl.BlockSpec((TOP_K, tm), cols), pl.BlockSpec((TOP_K, tm), cols),
                  pl.BlockSpec(cnt.shape, lambda i: (0, 0))],
        out_specs=pl.BlockSpec((TOP_K, tm), cols),
        out_shape=jax.ShapeDtypeStruct((TOP_K, n_rows), jnp.int32),
        compiler_params=_cparams(("arbitrary",)),
        name="slots",
    )(eidx, rank, cnt)


SC_WINDOW = 128


def _sc_mesh():
    return plsc.VectorSubcoreMesh(core_axis_name="core", subcore_axis_name="subcore")


def _sc_dispatch(rows, slot_t, n_out):
    width = rows.shape[1]
    n_chunks = slot_t.shape[1] // SC_WINDOW
    info = plsc.get_sparse_core_info()
    n_workers = info.num_cores * info.num_subcores

    @functools.partial(
        pl.kernel, mesh=_sc_mesh(),
        out_type=jax.ShapeDtypeStruct((n_out, width), rows.dtype),
        scratch_types=[pltpu.VMEM((TOP_K, SC_WINDOW), jnp.int32), pltpu.VMEM((SC_WINDOW, width), rows.dtype)],
        name="moe_dispatch")
    def run(rows_hbm, idx_hbm, out_hbm, idx_v, rows_v):
        wid = lax.axis_index("subcore") * info.num_cores + lax.axis_index("core")

        @pl.loop(wid, n_chunks, step=n_workers)
        def _(c):
            r0 = pl.multiple_of(c * SC_WINDOW, SC_WINDOW)
            pltpu.sync_copy(idx_hbm.at[:, pl.ds(r0, SC_WINDOW)], idx_v)
            pltpu.sync_copy(rows_hbm.at[pl.ds(r0, SC_WINDOW)], rows_v)
            for k in range(TOP_K):
                pltpu.sync_copy(rows_v, out_hbm.at[idx_v.at[k]])

    return run(rows, slot_t)


def _sc_collect(rows, slot_t):
    n_picks, n_rows = slot_t.shape
    width = rows.shape[1]
    n_chunks = n_rows // SC_WINDOW
    info = plsc.get_sparse_core_info()
    n_workers = info.num_cores * info.num_subcores

    @functools.partial(
        pl.kernel, mesh=_sc_mesh(),
        out_type=jax.ShapeDtypeStruct((n_picks, n_rows, width), rows.dtype),
        scratch_types=[pltpu.VMEM((TOP_K, SC_WINDOW), jnp.int32), pltpu.VMEM((SC_WINDOW, width), rows.dtype)],
        name="moe_collect")
    def run(rows_hbm, idx_hbm, out_hbm, idx_v, rows_v):
        wid = lax.axis_index("subcore") * info.num_cores + lax.axis_index("core")

        @pl.loop(wid, n_chunks, step=n_workers)
        def _(c):
            r0 = pl.multiple_of(c * SC_WINDOW, SC_WINDOW)
            pltpu.sync_copy(idx_hbm.at[:, pl.ds(r0, SC_WINDOW)], idx_v)
            for k in range(TOP_K):
                pltpu.sync_copy(rows_hbm.at[idx_v.at[k]], rows_v)
                pltpu.sync_copy(rows_v, out_hbm.at[k, pl.ds(r0, SC_WINDOW)])

    return run(rows, slot_t)


EXPERT_TILE = 512


def _work_items(cnt, n_slots):
    counts = cnt[:, 0].astype(jnp.int32)
    ends = jnp.cumsum(counts)
    n_tiles = n_slots // EXPERT_TILE
    bounds = jnp.sort(jnp.concatenate([jnp.arange(n_tiles, dtype=jnp.int32) * EXPERT_TILE, ends - counts]))
    nxt = jnp.concatenate([bounds[1:], jnp.array([n_slots], jnp.int32)])
    tile = jnp.minimum(bounds // EXPERT_TILE, n_tiles - 1)
    expert = jnp.sum((ends[None, :] <= bounds[:, None]).astype(jnp.int32), axis=1)
    expert = jnp.minimum(expert, N_EXPERTS - 1)
    return tile, expert, bounds - tile * EXPERT_TILE, nxt - tile * EXPERT_TILE


XS_RING = 3


def _experts_kernel(tile_ref, exp_ref, lo_ref, hi_ref, xs_hbm, wg_ref, wu_ref, wd_ref, ys_ref,
                    acc_ref, wgb_ref, wub_ref, wdb_ref, xbuf_ref, xsem):
    i = pl.program_id(0)
    n_items = pl.num_programs(0)
    lo, hi = lo_ref[i], hi_ref[i]

    def tile_copy(item):
        slot = lax.rem(item, XS_RING)
        row0 = pl.multiple_of(tile_ref[item] * EXPERT_TILE, EXPERT_TILE)
        return pltpu.make_async_copy(xs_hbm.at[pl.ds(row0, EXPERT_TILE)], xbuf_ref.at[slot], xsem.at[slot])

    @pl.when(i == 0)
    def _():
        for ahead in range(XS_RING - 1):
            pl.when(ahead < n_items)(lambda: tile_copy(ahead).start())

    @pl.when(i + XS_RING - 1 < n_items)
    def _():
        tile_copy(i + XS_RING - 1).start()

    tile_copy(i).wait()
    xs_ref = xbuf_ref.at[lax.rem(i, XS_RING)]

    @pl.when((i == 0) | (exp_ref[i] != exp_ref[jnp.maximum(i - 1, 0)]))
    def _():
        wgb_ref[...] = wg_ref[...].astype(BF16)
        wub_ref[...] = wu_ref[...].astype(BF16)
        wdb_ref[...] = wd_ref[...].astype(BF16)

    def ffn():
        x_lo, x_hi = _unpack_halves(xs_ref[...])
        x_lo, x_hi = x_lo.astype(BF16), x_hi.astype(BF16)
        n = x_lo.shape[1]
        a = _dot(x_lo, wgb_ref[0:n, :]) + _dot(x_hi, wgb_ref[n:, :])
        u = _dot(x_lo, wub_ref[0:n, :]) + _dot(x_hi, wub_ref[n:, :])
        return _dot(((a * _sigmoid(a)) * u).astype(BF16), wdb_ref[...])

    whole = (lo == 0) & (hi == EXPERT_TILE)

    @pl.when(whole)
    def _():
        ys_ref[...] = _pack_halves(ffn())

    @pl.when(jnp.logical_not(whole) & (hi > lo))
    def _():
        y = ffn()
        row = lax.broadcasted_iota(jnp.int32, y.shape, 0)
        y = jnp.where((row >= lo) & (row < hi), y, 0.0)

        @pl.when(lo == 0)
        def _():
            acc_ref[...] = y

        @pl.when((lo > 0) & (hi < EXPERT_TILE))
        def _():
            acc_ref[...] += y

        @pl.when((lo > 0) & (hi == EXPERT_TILE))
        def _():
            ys_ref[...] = _pack_halves(acc_ref[...] + y)


def _experts(xs, items, layer, exp_wg, exp_wu, exp_wd):
    n_slots, half = xs.shape
    d, hid = exp_wg.shape[-2:]
    tile, expert, lo, hi = items
    grid_spec = pltpu.PrefetchScalarGridSpec(
        num_scalar_prefetch=4,
        grid=(tile.shape[0],),
        in_specs=[pl.BlockSpec(memory_space=pl.ANY),
                  pl.BlockSpec((None, None, d, hid), lambda i, t, e, lo, hi: (layer, e[i], 0, 0)),
                  pl.BlockSpec((None, None, d, hid), lambda i, t, e, lo, hi: (layer, e[i], 0, 0)),
                  pl.BlockSpec((None, None, hid, d), lambda i, t, e, lo, hi: (layer, e[i], 0, 0))],
        out_specs=pl.BlockSpec((EXPERT_TILE, half), lambda i, t, e, lo, hi: (t[i], 0)),
        scratch_shapes=[pltpu.VMEM((EXPERT_TILE, d), F32), pltpu.VMEM((d, hid), BF16),
                        pltpu.VMEM((d, hid), BF16), pltpu.VMEM((hid, d), BF16),
                        pltpu.VMEM((XS_RING, EXPERT_TILE, half), jnp.uint32),
                        pltpu.SemaphoreType.DMA((XS_RING,))])
    return pl.pallas_call(
        _experts_kernel,
        grid_spec=grid_spec,
        out_shape=jax.ShapeDtypeStruct((n_slots, half), jnp.uint32),
        compiler_params=_cparams(("arbitrary",)),
        name="experts",
    )(tile, expert, lo, hi, xs, exp_wg, exp_wu, exp_wd)


def _moe_out_tile(yg_ref, w_ref, v_ref, sg_ref, su_ref, sd_ref, h1_ref, mod_ref, gpost_ref,
                  *, tiles_per_batch, n_batch, d):
    i = pl.program_id(0)
    x = v_ref[...]
    a = _dot(x, sg_ref[...])
    f = _dot(((a * _sigmoid(a)) * _dot(x, su_ref[...])).astype(BF16), sd_ref[...])
    n = d // 2
    f_lo, f_hi = f[:, :n], f[:, n:]
    w = w_ref[...]
    for k in range(TOP_K):
        y_lo, y_hi = _unpack_halves(yg_ref[k])
        wk = w[:, k:k + 1]
        f_lo = f_lo + wk * y_lo
        f_hi = f_hi + wk * y_hi
    f = jnp.concatenate([f_lo, f_hi], axis=1)
    bi = jnp.minimum(i // tiles_per_batch, n_batch)
    g2 = mod_ref[pl.ds(bi, 1), 5 * d:6 * d]
    return h1_ref[...] + g2 * _rms(f, gpost_ref[...])


N_MOE_OUT_IN = 9
N_INPROJ_IN = 9


def _moe_out_kernel(*refs, **kw):
    refs[N_MOE_OUT_IN][...] = _moe_out_tile(*refs[:N_MOE_OUT_IN], **kw)


def _moe_out_inproj_kernel(*refs, n_batch, d, tiles_per_batch):
    dims = dict(tiles_per_batch=tiles_per_batch, n_batch=n_batch, d=d)
    h = _moe_out_tile(*refs[:N_MOE_OUT_IN], **dims)
    n_in = N_MOE_OUT_IN + N_INPROJ_IN
    refs[n_in][...] = h
    _inproj_tile(h, *refs[N_MOE_OUT_IN:n_in], *refs[n_in + 1:], **dims)


def _moe_out_specs(yg, w, v, sg, su, sd, h1, mods, gpost, tm):
    d = v.shape[1]
    rows = lambda i: (i, 0)
    const = lambda i: (0, 0)
    once = dict(pipeline_mode=pl.Buffered(1))
    return [pl.BlockSpec((TOP_K, tm, d // 2), lambda i: (0, i, 0)), pl.BlockSpec((tm, TOP_K), rows),
            pl.BlockSpec((tm, d), rows),
            pl.BlockSpec(sg.shape, const, **once), pl.BlockSpec(su.shape, const, **once),
            pl.BlockSpec(sd.shape, const, **once),
            pl.BlockSpec((tm, d), rows), pl.BlockSpec(mods.shape, const), pl.BlockSpec((1, d), const)]


def _moe_out(moe_args, *, n_batch, seq):
    v = moe_args[2]
    n_rows, d = v.shape
    return pl.pallas_call(
        functools.partial(_moe_out_kernel, tiles_per_batch=seq // TM, n_batch=n_batch, d=d),
        grid=(n_rows // TM,),
        in_specs=_moe_out_specs(*moe_args, TM),
        out_specs=pl.BlockSpec((TM, d), lambda i: (i, 0)),
        out_shape=jax.ShapeDtypeStruct((n_rows, d), F32),
        compiler_params=_cparams(("arbitrary",)),
        name="moe_out",
    )(*moe_args)


FUSED_TM = 256


def _moe_out_inproj(moe_args, inproj_args, *, n_batch, seq):
    v = moe_args[2]
    n_rows, d = v.shape
    tm = FUSED_TM
    in_specs, out_specs, out_shape = _inproj_specs(n_rows, d, *inproj_args, tm=tm, n_batch=n_batch, seq=seq)
    rows = lambda i: (i, 0)
    return pl.pallas_call(
        functools.partial(_moe_out_inproj_kernel, tiles_per_batch=seq // tm, n_batch=n_batch, d=d),
        grid=(n_rows // tm,),
        in_specs=_moe_out_specs(*moe_args, tm) + in_specs,
        out_specs=[pl.BlockSpec((tm, d), rows)] + out_specs,
        out_shape=[jax.ShapeDtypeStruct((n_rows, d), F32)] + out_shape,
        compiler_params=_cparams(("arbitrary",)),
        name="moe_out_inproj",
    )(*moe_args, *inproj_args)


def _moe_routed(v, vp, layer, rw, rb, exp_wg, exp_wu, exp_wd):
    n_rows = v.shape[0]
    n_slots = n_rows * TOP_K
    assert n_slots % EXPERT_TILE == 0
    eidx, rank, w_t, cnt = _router(v, rw, rb)
    slot_t = _slots(eidx, rank, cnt)
    xs = _sc_dispatch(vp, slot_t, n_slots)
    ys = _experts(xs, _work_items(cnt, n_slots), layer, exp_wg, exp_wu, exp_wd)
    return _sc_collect(ys, slot_t), w_t.T


def _rope_tables(seq):
    rows = seq // GRID_W
    row_id = np.repeat(np.arange(rows, dtype=np.float64), GRID_W)
    col_id = np.tile(np.arange(GRID_W, dtype=np.float64), rows)

    def tables(rot_dim):
        axis_dim = rot_dim // 2
        inv_freq = ROPE_BASE ** (-np.arange(0, axis_dim, 2, dtype=np.float64) / axis_dim)
        ang_r = row_id[:, None] * inv_freq[None, :]
        ang_c = col_id[:, None] * inv_freq[None, :]
        cos = np.concatenate([np.cos(ang_r), np.cos(ang_r), np.cos(ang_c), np.cos(ang_c)], axis=1)
        sin = np.concatenate([-np.sin(ang_r), np.sin(ang_r), -np.sin(ang_c), np.sin(ang_c)], axis=1)
        return cos, sin

    cos64, sin64 = tables(GQA_DIM)
    cos32, sin32 = tables(MLA_ROPE)
    ones = np.ones((seq, MLA_NOPE))
    pad = LANES - MLA_NOPE - MLA_ROPE
    cospe = np.concatenate([ones, cos32, np.ones((seq, pad))], axis=1)
    sinpe = np.concatenate([0 * ones, sin32, np.zeros((seq, pad))], axis=1)
    tab = np.concatenate([cos64, cos64, sin64, sin64, cospe, sinpe], axis=1)
    ident = np.concatenate([np.ones((TM, LANES)), np.zeros((TM, LANES)),
                            np.ones((TM, LANES)), np.zeros((TM, LANES))], axis=1)
    return jnp.asarray(np.concatenate([tab, ident], axis=0), F32)


def _pack_w_in(w):
    d = w.shape[0]
    sizes = (MLA_Q_LORA, MLA_KV_LORA, MLA_ROPE, GQA_HEADS * GQA_DIM, GQA_KV_HEADS * GQA_DIM,
             GQA_KV_HEADS * GQA_DIM, RET_HEADS * RET_QK, RET_HEADS * RET_QK, RET_HEADS * RET_V,
             RET_HEADS * RET_V, 3 * d)
    offs, parts = 0, []
    for s in sizes:
        parts.append(w[:, offs:offs + s])
        offs += s
    cq, ckv, kpe, gq, gk, gv, rq, rk, rv, rg, gates = parts

    def twice(m):
        heads = [m[:, i * GQA_DIM:(i + 1) * GQA_DIM] for i in range(GQA_KV_HEADS)]
        return jnp.concatenate([hh for hd in heads for hh in (hd, hd)], axis=1)

    kpe_slab = jnp.concatenate([jnp.zeros((d, MLA_NOPE), F32), kpe,
                                jnp.zeros((d, LANES - MLA_NOPE - MLA_ROPE), F32)], axis=1)
    packed = jnp.concatenate([cq, ckv, kpe_slab, gq * (GQA_DIM ** -0.5 * LOG2_E), twice(gk), twice(gv),
                              rq, rk * RET_QK ** -0.5, rv, rg, gates], axis=1)
    assert packed.shape[1] == W_COLS
    return packed.astype(BF16)


def _pack_mla_up(w_uq, w_ukv):
    r = w_uq.shape[0]
    dq = MLA_NOPE + MLA_ROPE
    wq = jnp.pad(w_uq.reshape(r, MLA_HEADS, dq), ((0, 0), (0, 0), (0, LANES - dq))).reshape(r, MLA_HEADS * LANES)
    kv = w_ukv.reshape(r, MLA_HEADS, MLA_NOPE + MLA_V)
    wk = jnp.pad(kv[:, :, :MLA_NOPE], ((0, 0), (0, 0), (0, LANES - MLA_NOPE))).reshape(r, MLA_HEADS * LANES)
    wv = kv[:, :, MLA_NOPE:]
    zeros = jnp.zeros_like(wv)
    even = jnp.concatenate([wv, zeros], axis=2)
    odd = jnp.concatenate([zeros, wv], axis=2)
    wv = jnp.where((jnp.arange(MLA_HEADS) % 2 == 0)[None, :, None], even, odd).reshape(r, MLA_HEADS * LANES)
    return wq.astype(BF16), wk.astype(BF16), wv.astype(BF16)


def kernel(x, c, ctx, c_ctx, ada_w, ada_b, norm_mix_pre, norm_mix_post, norm_ffn_pre, norm_ffn_post, w_in, mla_q_norm, mla_w_uq, mla_kv_norm, mla_w_ukv, gqa_sink, ret_decay_fwd, ret_decay_bwd, w_br_mla, w_br_gqa, w_br_ret, w_out, router_w, router_bias, exp_w_gate, exp_w_up, exp_w_down, shared_w_gate, shared_w_up, shared_w_down):
    n_batch, seq, d = x.shape
    n_ctx = ctx.shape[1]
    depth = ada_w.shape[0]
    n_lat_rows = n_batch * seq
    assert seq % TM == 0 and (n_batch * n_ctx) % TM == 0 and seq % ATT_TQ == 0 and seq % n_ctx == 0
    assert n_batch < MOD_ROWS and seq % GRID_W == 0

    cond = jnp.zeros((MOD_ROWS, d), F32).at[:n_batch].set(c).at[n_batch].set(c_ctx)
    mods_all = _adaln(cond, ada_w, ada_b)
    rope = _rope_tables(seq)
    h = (x.reshape(n_lat_rows, d), ctx.reshape(n_batch * n_ctx, d), 0)
    n_all_rows = n_lat_rows + n_batch * n_ctx
    row = lambda p: p.reshape(1, -1)
    dims = dict(n_batch=n_batch, seq=seq)

    def inproj_args(l):
        return (mods_all[l], row(norm_mix_pre[l]), _pack_w_in(w_in[l]), rope, row(mla_q_norm[l]),
                row(mla_kv_norm[l]), *_pack_mla_up(mla_w_uq[l], mla_w_ukv[l]))

    projected = _inproj(h[0], h[1], inproj_args(0), **dims)
    for l in range(depth):
        last = l == depth - 1
        mods = mods_all[l]
        mq, mk, mv, gqa, ret, gates = projected
        a = _mla_attention(mq, mk, mv, ctx=n_ctx, with_ctx_queries=not last, **dims)
        sink_tab = jnp.broadcast_to(gqa_sink[l].astype(F32)[:, None] * LOG2_E, (GQA_HEADS, LANES))
        w = _window_attention(gqa, sink_tab, ctx=n_ctx, with_ctx_queries=not last, **dims)
        lg = jnp.concatenate([jax.nn.log_sigmoid(ret_decay_fwd[l].astype(F32)),
                              jax.nn.log_sigmoid(ret_decay_bwd[l].astype(F32))])
        o_f, o_b = _retention(ret, jnp.broadcast_to(lg[:, None], (2 * RET_HEADS, LANES)), ctx=n_ctx, **dims)
        n_rows = n_lat_rows if last else n_all_rows
        h1, v, vp = _merge(a, w, o_f, o_b, ret, gates, h, mods, row(norm_mix_post[l]), row(norm_ffn_pre[l]),
                           w_br_mla[l].astype(BF16), w_br_gqa[l].astype(BF16), w_br_ret[l].astype(BF16),
                           w_out[l].astype(BF16), n_rows=n_rows, **dims)
        yg, gate_w = _moe_routed(v, vp, l, router_w[l].T.astype(BF16), router_bias[l].astype(F32).reshape(-1, 1),
                                 exp_w_gate, exp_w_up, exp_w_down)
        moe_args = (yg, gate_w, v, shared_w_gate[l].astype(BF16), shared_w_up[l].astype(BF16),
                    shared_w_down[l].astype(BF16), h1, mods, row(norm_ffn_post[l]))
        if last:
            out = _moe_out(moe_args, **dims)
        else:
            stream, *projected = _moe_out_inproj(moe_args, inproj_args(l + 1), **dims)
            h = (stream, stream, n_lat_rows // TM)
    return out[:n_lat_rows].reshape(n_batch, seq, d)
```

```python
import functools

import numpy as np
import jax
import jax.numpy as jnp
from jax import lax
from jax.experimental import pallas as pl
from jax.experimental.pallas import tpu as pltpu
from jax.experimental.pallas import tpu_sc as plsc

F32 = jnp.float32
BF16 = jnp.bfloat16

GRID_W = 64
ROPE_BASE = 10000.0
NORM_EPS = 1e-6
NEG_INF = -1e30
LOG2_E = 1.4426950408889634
N_MOD = 6
MLA_HEADS, MLA_NOPE, MLA_ROPE, MLA_V = 8, 64, 32, 64
MLA_Q_LORA, MLA_KV_LORA = 256, 256
GQA_HEADS, GQA_KV_HEADS, GQA_DIM, WINDOW = 8, 2, 64, 128
RET_HEADS, RET_QK, RET_V, RET_CHUNK = 4, 64, 128, 128
N_EXPERTS, N_EXPERT_GROUPS, TOPK_GROUPS, TOP_K = 64, 8, 4, 8
EXPERTS_PER_GROUP = N_EXPERTS // N_EXPERT_GROUPS
ROUTED_SCALE = 2.5

LANES = 128
TM = 512
ATT_TQ = 512
WIN_TQ = 256
RET_TILE = 256
MOD_ROWS = 8
V7X_VMEM_LIMIT = 56 * 1024 * 1024

C_CQ, C_CKV, C_KPE = 0, 256, 512
C_G = 640
C_R = 1664
C_GATE = 3200
W_COLS = 6272


def _cparams(sem):
    return pltpu.CompilerParams(dimension_semantics=sem, vmem_limit_bytes=V7X_VMEM_LIMIT)


def _rms(x, g):
    return x * lax.rsqrt(jnp.mean(x * x, axis=-1, keepdims=True) + NORM_EPS) * g


def _sigmoid(x):
    return 0.5 * jnp.tanh(0.5 * x) + 0.5


def _dot(a, b):
    return jnp.dot(a, b, preferred_element_type=F32)


def _dot_nt(a, b):
    return lax.dot_general(a, b, (((1,), (1,)), ((), ())), preferred_element_type=F32)


def _dot_tn(a, b):
    return lax.dot_general(a, b, (((0,), (0,)), ((), ())), preferred_element_type=F32)


def _rope(x, cos, sin, half):
    n = x.shape[-1]
    reps = n // LANES
    if reps > 1:
        cos = jnp.concatenate([cos] * reps, axis=1)
        sin = jnp.concatenate([sin] * reps, axis=1)
    lane = lax.broadcasted_iota(jnp.int32, x.shape, 1)
    up = pltpu.roll(x, half, 1)
    dn = pltpu.roll(x, n - half, 1)
    partner = jnp.where((lane & (2 * half - 1)) < half, dn, up)
    return x * cos + partner * sin


def _lane_lo(shape):
    return (lax.broadcasted_iota(jnp.int32, shape, 1) & (LANES - 1)) < (LANES // 2)


def _pack_halves(x):
    n = x.shape[1] // 2
    bits = lambda t: lax.bitcast_convert_type(t.astype(BF16).astype(F32), jnp.uint32)
    return (bits(x[:, :n]) >> 16) | bits(x[:, n:])


def _unpack_halves(p):
    lo = lax.bitcast_convert_type(p << 16, F32)
    hi = lax.bitcast_convert_type(p & jnp.uint32(0xFFFF0000), F32)
    return lo, hi


def _ada_kernel(c_ref, w_ref, b_ref, o_ref):
    c = c_ref[...]
    s = c * _sigmoid(c)
    o_ref[...] = _dot(s.astype(BF16), w_ref[...].astype(BF16)) + b_ref[...]


def _adaln(cond, ada_w, ada_b):
    n_layers, d, n = ada_w.shape
    tn = 1024
    return pl.pallas_call(
        _ada_kernel,
        grid=(n_layers, n // tn),
        in_specs=[pl.BlockSpec((MOD_ROWS, d), lambda l, j: (0, 0)),
                  pl.BlockSpec((None, d, tn), lambda l, j: (l, 0, j)),
                  pl.BlockSpec((None, 1, tn), lambda l, j: (l, 0, j))],
        out_specs=pl.BlockSpec((None, MOD_ROWS, tn), lambda l, j: (l, 0, j)),
        out_shape=jax.ShapeDtypeStruct((n_layers, MOD_ROWS, n), F32),
        compiler_params=_cparams(("arbitrary", "arbitrary")),
        name="adaln",
    )(cond, ada_w, ada_b.reshape(n_layers, 1, n))


def _inproj_tile(h, mod_ref, gpre_ref, w_ref, rope_ref, qn_ref, kvn_ref, wuq_ref, wuk_ref, wuv_ref,
                 mq_ref, mk_ref, mv_ref, gqa_ref, ret_ref, gate_ref, *, tiles_per_batch, n_batch, d):
    i = pl.program_id(0)
    bi = jnp.minimum(i // tiles_per_batch, n_batch)
    sh = mod_ref[pl.ds(bi, 1), 0:d]
    sc = mod_ref[pl.ds(bi, 1), d:2 * d]
    u = (_rms(h, gpre_ref[...]) * (1.0 + sc) + sh).astype(BF16)

    cos64 = rope_ref[:, 0:LANES]
    sin64 = rope_ref[:, LANES:2 * LANES]
    cospe = rope_ref[:, 2 * LANES:3 * LANES]
    sinpe = rope_ref[:, 3 * LANES:4 * LANES]

    c = _dot(u, w_ref[:, C_CQ:C_G])
    kpe = _rope(c[:, C_KPE:C_G], cospe, sinpe, MLA_ROPE // 4)
    qn = _rms(c[:, C_CQ:C_CKV], qn_ref[...]).astype(BF16)
    q = _rope(_dot(qn, wuq_ref[...]), cospe, sinpe, MLA_ROPE // 4)
    mq_ref[...] = (q * ((MLA_NOPE + MLA_ROPE) ** -0.5 * LOG2_E)).astype(mq_ref.dtype)
    kvn = _rms(c[:, C_CKV:C_KPE], kvn_ref[...]).astype(BF16)
    k = _dot(kvn, wuk_ref[...]) + jnp.concatenate([kpe] * MLA_HEADS, axis=1)
    mk_ref[...] = k.astype(mk_ref.dtype)
    v = _dot(kvn, wuv_ref[...])
    lane = lax.broadcasted_iota(jnp.int32, v.shape, 1)
    value_lane = ((lane & (LANES - 1)) < MLA_V) == (((lane >> (LANES.bit_length() - 1)) & 1) == 0)
    mv_ref[...] = jnp.where(value_lane, v, 1.0).astype(mv_ref.dtype)

    g = _dot(u, w_ref[:, C_G:C_R])
    n_qk = GQA_HEADS * GQA_DIM + 2 * GQA_KV_HEADS * GQA_DIM
    gqa_ref[:, 0:n_qk] = _rope(g[:, 0:n_qk], cos64, sin64, GQA_DIM // 4).astype(gqa_ref.dtype)
    gqa_ref[:, n_qk:] = g[:, n_qk:].astype(gqa_ref.dtype)

    r = _dot(u, w_ref[:, C_R:C_GATE])
    n_qk = 2 * RET_HEADS * RET_QK
    ret_ref[:, 0:n_qk] = _rope(r[:, 0:n_qk], cos64, sin64, RET_QK // 4).astype(ret_ref.dtype)
    ret_ref[:, n_qk:] = r[:, n_qk:].astype(ret_ref.dtype)

    gate_ref[...] = _dot(u, w_ref[:, C_GATE:W_COLS]).astype(gate_ref.dtype)


def _inproj_kernel(hl_ref, hc_ref, *refs, tiles_per_batch, n_batch, d):
    latent = pl.program_id(0) < tiles_per_batch * n_batch
    _inproj_tile(jnp.where(latent, hl_ref[...], hc_ref[...]), *refs,
                 tiles_per_batch=tiles_per_batch, n_batch=n_batch, d=d)


def _inproj_specs(t, d, mods, gpre, w_all, rope, qn, kvn, wuq, wuk, wuv, *, tm, n_batch, seq):
    tiles_per_batch = seq // tm
    n_lat_tiles = n_batch * tiles_per_batch
    const = lambda i: (0, 0)
    rows = lambda i: (i, 0)
    rope_idx = lambda i: (jnp.where(i < n_lat_tiles, i % tiles_per_batch, tiles_per_batch), 0)
    once = dict(pipeline_mode=pl.Buffered(1))
    hq = MLA_HEADS * LANES
    outs = [jax.ShapeDtypeStruct((t, hq), BF16), jax.ShapeDtypeStruct((t, hq), BF16),
            jax.ShapeDtypeStruct((t, hq), BF16),
            jax.ShapeDtypeStruct((t, C_R - C_G), BF16),
            jax.ShapeDtypeStruct((t, C_GATE - C_R), F32),
            jax.ShapeDtypeStruct((t, W_COLS - C_GATE), BF16)]
    in_specs = [pl.BlockSpec(mods.shape, const),
                pl.BlockSpec((1, d), const),
                pl.BlockSpec(w_all.shape, const, **once),
                pl.BlockSpec((tm, 4 * LANES), rope_idx),
                pl.BlockSpec(qn.shape, const), pl.BlockSpec(kvn.shape, const),
                pl.BlockSpec(wuq.shape, const, **once), pl.BlockSpec(wuk.shape, const, **once),
                pl.BlockSpec(wuv.shape, const, **once)]
    return in_specs, [pl.BlockSpec((tm, o.shape[1]), rows) for o in outs], outs


def _inproj(h_lat, h_ctx, inproj_args, *, n_batch, seq):
    d = h_lat.shape[1]
    t = h_lat.shape[0] + h_ctx.shape[0]
    n_lat_tiles = h_lat.shape[0] // TM
    in_specs, out_specs, out_shape = _inproj_specs(t, d, *inproj_args, tm=TM, n_batch=n_batch, seq=seq)
    return pl.pallas_call(
        functools.partial(_inproj_kernel, tiles_per_batch=seq // TM, n_batch=n_batch, d=d),
        grid=(t // TM,),
        in_specs=[pl.BlockSpec((TM, d), lambda i: (jnp.minimum(i, n_lat_tiles - 1), 0)),
                  pl.BlockSpec((TM, d), lambda i: (jnp.maximum(i - n_lat_tiles, 0), 0))] + in_specs,
        out_specs=out_specs,
        out_shape=out_shape,
        compiler_params=_cparams(("arbitrary",)),
        name="inproj",
    )(h_lat, h_ctx, *inproj_args)


MLA_HEADS_PER_STEP = 4


def _mla_kernel(q_ref, kl_ref, kc_ref, vl_ref, vc_ref, o_ref, s_ref, p_ref, *, with_lat):
    n_ctx = kc_ref.shape[0]

    def body(with_lat):
        n_keys = n_ctx + (kl_ref.shape[0] if with_lat else 0)

        def scores(h):
            sl = slice(h * LANES, (h + 1) * LANES)
            s_ref[h % 2, :, 0:n_ctx] = _dot_nt(q_ref[:, sl], kc_ref[:, sl])
            if with_lat:
                s_ref[h % 2, :, n_ctx:n_keys] = _dot_nt(q_ref[:, sl], kl_ref[:, sl])

        def probs(h):
            s = s_ref[h % 2, :, 0:n_keys]
            p_ref[h % 2, :, 0:n_keys] = jnp.exp2(s - jnp.max(s, axis=-1, keepdims=True)).astype(BF16)

        def weighted(h):
            sl = slice(h * LANES, (h + 1) * LANES)
            o = _dot(p_ref[h % 2, :, 0:n_ctx], vc_ref[:, sl])
            if with_lat:
                o = o + _dot(p_ref[h % 2, :, n_ctx:n_keys], vl_ref[:, sl])
            return o / pltpu.roll(o, LANES // 2, 1)

        outs = [None] * MLA_HEADS_PER_STEP
        scores(0)
        for h in range(MLA_HEADS_PER_STEP):
            if h + 1 < MLA_HEADS_PER_STEP:
                scores(h + 1)
            probs(h)
            outs[h] = weighted(h)
        for pr in range(MLA_HEADS_PER_STEP // 2):
            even, odd = outs[2 * pr], outs[2 * pr + 1]
            o_ref[:, pr * LANES:(pr + 1) * LANES] = jnp.where(_lane_lo(even.shape), even, odd).astype(o_ref.dtype)

    body(with_lat)


def _mla_call(mq, mk, mv, *, n_batch, seq, ctx, latent_queries):
    hps = MLA_HEADS_PER_STEP
    ctx_blk0 = n_batch * seq // ctx
    tq = ATT_TQ if latent_queries else ctx
    nq = seq // tq if latent_queries else 1
    q_blk0 = 0 if latent_queries else n_batch * seq // tq
    q_idx = lambda b, g, i: (q_blk0 + b * nq + i, g)
    ctx_idx = lambda b, g, i: (ctx_blk0 + b, g)
    lat_idx = (lambda b, g, i: (b, g)) if latent_queries else ctx_idx
    n_lat = seq if latent_queries else ctx
    n_keys = ctx + (seq if latent_queries else 0)
    in_specs = [pl.BlockSpec((tq, hps * LANES), q_idx),
                pl.BlockSpec((n_lat, hps * LANES), lat_idx), pl.BlockSpec((ctx, hps * LANES), ctx_idx),
                pl.BlockSpec((n_lat, hps * LANES), lat_idx), pl.BlockSpec((ctx, hps * LANES), ctx_idx)]
    return pl.pallas_call(
        functools.partial(_mla_kernel, with_lat=latent_queries),
        grid=(n_batch, MLA_HEADS // hps, nq),
        in_specs=in_specs,
        out_specs=pl.BlockSpec((tq, hps * MLA_V), lambda b, g, i: (b * nq + i, g)),
        out_shape=jax.ShapeDtypeStruct((n_batch * nq * tq, MLA_HEADS * MLA_V), BF16),
        scratch_shapes=[pltpu.VMEM((2, tq, n_keys), F32), pltpu.VMEM((2, tq, n_keys), BF16)],
        compiler_params=_cparams(("arbitrary", "arbitrary", "arbitrary")),
        name="mla_attn" if latent_queries else "mla_attn_ctx",
    )(mq, mk, mk, mv, mv)


def _mla_attention(mq, mk, mv, *, n_batch, seq, ctx, with_ctx_queries):
    dims = dict(n_batch=n_batch, seq=seq, ctx=ctx)
    lat = _mla_call(mq, mk, mv, latent_queries=True, **dims)
    return lat, (_mla_call(mq, mk, mv, latent_queries=False, **dims) if with_ctx_queries else lat)


def _win_kernel(q_ref, kp_ref, kcur_ref, kn_ref, vp_ref, vcur_ref, vn_ref, kc_ref, vc_ref, sink_ref, o_ref,
                s_ref, p_ref, *, seq, with_lat):
    i = pl.program_id(1)
    tq = q_ref.shape[0]
    group = GQA_HEADS // GQA_KV_HEADS

    def body(with_lat):
        n_ctx = kc_ref.shape[0]
        n_keys = n_ctx + (tq + 2 * WINDOW if with_lat else 0)
        if with_lat:
            q_pos = i * tq + lax.broadcasted_iota(jnp.int32, (tq, n_keys), 0)
            k_pos = i * tq - WINDOW - n_ctx + lax.broadcasted_iota(jnp.int32, (tq, n_keys), 1)
            in_band = (jnp.abs(q_pos - k_pos) <= WINDOW) & (k_pos >= 0) & (k_pos < seq)
            valid = in_band | (lax.broadcasted_iota(jnp.int32, (tq, n_keys), 1) < n_ctx)
        lo = _lane_lo((tq, LANES))
        lo_k = _lane_lo((n_keys, LANES))
        keys, values = [], []
        for kv in range(GQA_KV_HEADS):
            sl = slice(kv * LANES, (kv + 1) * LANES)
            if with_lat:
                k_all = jnp.concatenate([kc_ref[:, sl], kp_ref[:, sl], kcur_ref[:, sl], kn_ref[:, sl]], axis=0)
                v_all = jnp.concatenate([vc_ref[:, sl], vp_ref[:, sl], vcur_ref[:, sl], vn_ref[:, sl]], axis=0)
            else:
                k_all, v_all = kc_ref[:, sl], vc_ref[:, sl]
            keys.append(k_all)
            one = jnp.ones_like(v_all)
            values.append((jnp.where(lo_k, v_all, one), jnp.where(lo_k, one, v_all)))

        def scores(hd):
            kv, pair = hd // group, hd // 2
            qp = q_ref[:, pair * LANES:(pair + 1) * LANES]
            qm = jnp.where(lo if hd % 2 == 0 else jnp.logical_not(lo), qp, jnp.zeros_like(qp))
            s = _dot_nt(qm, keys[kv])
            s_ref[hd % 2, :, 0:n_keys] = jnp.where(valid, s, NEG_INF) if with_lat else s

        def probs(hd):
            s = s_ref[hd % 2, :, 0:n_keys]
            m = jnp.maximum(jnp.max(s, axis=-1, keepdims=True), sink_ref[hd:hd + 1, 0:1])
            p_ref[hd % 2, :, 0:n_keys] = jnp.exp2(s - m).astype(BF16)
            return jnp.exp2(sink_ref[hd:hd + 1, 0:1] - m)

        def weighted(hd, sink_term):
            o = _dot(p_ref[hd % 2, :, 0:n_keys], values[hd // group][hd % 2])
            return o / (pltpu.roll(o, LANES // 2, 1) + sink_term)

        outs = [None] * GQA_HEADS
        scores(0)
        for hd in range(GQA_HEADS):
            if hd + 1 < GQA_HEADS:
                scores(hd + 1)
            outs[hd] = weighted(hd, probs(hd))
        for pair in range(GQA_HEADS // 2):
            o_ref[:, pair * LANES:(pair + 1) * LANES] = jnp.where(
                lo, outs[2 * pair], outs[2 * pair + 1]).astype(o_ref.dtype)

    body(with_lat)


def _win_call(gqa, sink_tab, *, n_batch, seq, ctx, latent_queries):
    tq = WIN_TQ if latent_queries else ctx
    nq = seq // tq if latent_queries else 1
    q_blk0 = 0 if latent_queries else n_batch * seq // tq
    per_tile = tq // WINDOW
    n_win_blocks = seq // WINDOW
    ctx_blk0 = n_batch * seq // ctx
    nqk = GQA_HEADS * GQA_DIM
    kw = 2 * GQA_KV_HEADS * GQA_DIM
    k_col, v_col = nqk // kw, nqk // kw + 1
    q_idx = lambda b, i: (q_blk0 + b * nq + i, 0)
    cidx = lambda col: (lambda b, i: (ctx_blk0 + b, col))
    if latent_queries:
        cur = lambda col: (lambda b, i: (b * nq + i, col))
        prev = lambda col: (lambda b, i: (b * n_win_blocks + jnp.maximum(per_tile * i - 1, 0), col))
        nxt = lambda col: (lambda b, i: (b * n_win_blocks + jnp.minimum(per_tile * (i + 1), n_win_blocks - 1), col))
        band = [((WINDOW, kw), prev), ((tq, kw), cur), ((WINDOW, kw), nxt)]
    else:
        band = [((ctx, kw), cidx)] * 3
    n_keys = ctx + (tq + 2 * WINDOW if latent_queries else 0)
    in_specs = ([pl.BlockSpec((tq, nqk), q_idx)]
                + [pl.BlockSpec(shape, idx(k_col)) for shape, idx in band]
                + [pl.BlockSpec(shape, idx(v_col)) for shape, idx in band]
                + [pl.BlockSpec((ctx, kw), cidx(k_col)), pl.BlockSpec((ctx, kw), cidx(v_col)),
                   pl.BlockSpec(sink_tab.shape, lambda b, i: (0, 0))])
    return pl.pallas_call(
        functools.partial(_win_kernel, seq=seq, with_lat=latent_queries),
        grid=(n_batch, nq),
        in_specs=in_specs,
        out_specs=pl.BlockSpec((tq, nqk), lambda b, i: (b * nq + i, 0)),
        out_shape=jax.ShapeDtypeStruct((n_batch * nq * tq, nqk), BF16),
        scratch_shapes=[pltpu.VMEM((2, tq, n_keys), F32), pltpu.VMEM((2, tq, n_keys), BF16)],
        compiler_params=_cparams(("arbitrary", "arbitrary")),
        name="win_attn" if latent_queries else "win_attn_ctx",
    )(*([gqa] * 9 + [sink_tab]))


def _window_attention(gqa, sink_tab, *, n_batch, seq, ctx, with_ctx_queries):
    dims = dict(n_batch=n_batch, seq=seq, ctx=ctx)
    lat = _win_call(gqa, sink_tab, latent_queries=True, **dims)
    return lat, (_win_call(gqa, sink_tab, latent_queries=False, **dims) if with_ctx_queries else lat)


def _ret_kernel(f_ref, b_ref, lg_ref, of_ref, ob_ref, sf_ref, sb_ref, qdec_ref, kdec_ref, cdec_ref, inner_ref):
    @pl.when(pl.program_id(1) == 0)
    def _():
        sf_ref[...] = jnp.zeros_like(sf_ref)
        sb_ref[...] = jnp.zeros_like(sb_ref)

    L = f_ref.shape[0]
    lo = _lane_lo((L, LANES))
    srow_lo = lax.broadcasted_iota(jnp.int32, (LANES, LANES), 0) < RET_QK
    nq = RET_HEADS * RET_QK
    n_pairs = RET_HEADS // 2

    @pl.when(pl.program_id(1) == 0)
    def _():
        ii = lax.broadcasted_iota(jnp.int32, (L, L), 0)
        jj = lax.broadcasted_iota(jnp.int32, (L, L), 1)
        row = lax.broadcasted_iota(jnp.int32, (L, LANES), 0).astype(F32)
        for direction, forward in enumerate((True, False)):
            dist = ii - jj if forward else jj - ii
            distf = jnp.maximum(dist, 0).astype(F32)
            for pr in range(n_pairs):
                r0 = direction * RET_HEADS + 2 * pr
                lg = [lg_ref[r0 + e:r0 + e + 1, :] for e in range(2)]
                lg_lane = jnp.where(lo, lg[0], lg[1])
                qdec_ref[direction * n_pairs + pr] = jnp.exp(lg_lane * ((row + 1.0) if forward else (L - row)))
                kdec_ref[direction * n_pairs + pr] = jnp.exp(lg_lane * ((L - 1.0 - row) if forward else row))
                cdec_ref[direction * n_pairs + pr] = jnp.where(srow_lo, jnp.exp(lg[0] * float(L)),
                                                               jnp.exp(lg[1] * float(L)))
                for e in range(2):
                    inner_ref[r0 + e] = jnp.where(dist >= 0, jnp.exp(lg[e][:, 0:1] * distf), 0.0)

    def scan_chunk(x_ref, o_ref, s_ref, direction):
        for pr in range(n_pairs):
            q = x_ref[:, pr * LANES:(pr + 1) * LANES].astype(F32)
            k = x_ref[:, nq + pr * LANES:nq + (pr + 1) * LANES].astype(F32)
            qd = q * qdec_ref[direction * n_pairs + pr]
            kdb = (k * kdec_ref[direction * n_pairs + pr]).astype(BF16)
            kb = k.astype(BF16)
            state = s_ref[pr]
            state_b = state.astype(BF16)
            upd = []
            for e in range(2):
                hd = 2 * pr + e
                keep = lo if e == 0 else jnp.logical_not(lo)
                v = x_ref[:, 2 * nq + hd * RET_V:2 * nq + (hd + 1) * RET_V].astype(BF16)
                attn = _dot_nt(jnp.where(keep, q, 0.0).astype(BF16), kb) * inner_ref[direction * RET_HEADS + hd]
                o = _dot(attn.astype(BF16), v) + _dot(jnp.where(keep, qd, 0.0).astype(BF16), state_b)
                o_ref[:, hd * RET_V:(hd + 1) * RET_V] = o
                upd.append(_dot_tn(kdb, v))
            s_ref[pr] = state * cdec_ref[direction * n_pairs + pr] + jnp.where(srow_lo, upd[0], upd[1])

    scan_chunk(f_ref, of_ref, sf_ref, 0)
    scan_chunk(b_ref, ob_ref, sb_ref, 1)


def _retention(ret, lg_tab, *, n_batch, seq, ctx):
    t = ret.shape[0]
    L = RET_TILE
    assert seq % L == 0 and ctx % L == 0
    n_lat, n_ctx = seq // L, ctx // L
    ctx0 = n_batch * n_lat
    width = 2 * RET_HEADS * RET_QK + RET_HEADS * RET_V
    fwd = lambda b, s: (jnp.where(s < n_ctx, ctx0 + b * n_ctx + s, b * n_lat + s - n_ctx), 0)
    bwd = lambda b, s: (jnp.where(s < n_ctx, ctx0 + b * n_ctx + n_ctx - 1 - s, b * n_lat + n_lat - 1 - (s - n_ctx)), 0)
    out = jax.ShapeDtypeStruct((t, RET_HEADS * RET_V), F32)
    return pl.pallas_call(
        _ret_kernel,
        grid=(n_batch, n_lat + n_ctx),
        in_specs=[pl.BlockSpec((L, width), fwd), pl.BlockSpec((L, width), bwd),
                  pl.BlockSpec(lg_tab.shape, lambda b, s: (0, 0))],
        out_specs=[pl.BlockSpec((L, RET_HEADS * RET_V), fwd), pl.BlockSpec((L, RET_HEADS * RET_V), bwd)],
        out_shape=[out, out],
        scratch_shapes=[pltpu.VMEM((RET_HEADS // 2, LANES, RET_V), F32),
                        pltpu.VMEM((RET_HEADS // 2, LANES, RET_V), F32),
                        pltpu.VMEM((RET_HEADS, L, LANES), F32), pltpu.VMEM((RET_HEADS, L, LANES), F32),
                        pltpu.VMEM((RET_HEADS, LANES, RET_V), F32), pltpu.VMEM((2 * RET_HEADS, L, L), F32)],
        compiler_params=_cparams(("arbitrary", "arbitrary")),
        name="retention",
    )(ret, ret, lg_tab)


def _merge_kernel(al_ref, ac_ref, wl_ref, wc_ref, of_ref, ob_ref, rg_ref, gt_ref, hl_ref, hc_ref, mod_ref,
                  gpost_ref, gffn_ref, wa_ref, ww_ref, wr_ref, wo_ref, h1_ref, v_ref, vp_ref,
                  *, tiles_per_batch, n_batch, d):
    i = pl.program_id(0)
    bi = jnp.minimum(i // tiles_per_batch, n_batch)
    latent = i < tiles_per_batch * n_batch
    a_tile = jnp.where(latent, al_ref[...], ac_ref[...])
    w_tile = jnp.where(latent, wl_ref[...], wc_ref[...])
    h_tile = jnp.where(latent, hl_ref[...], hc_ref[...])
    o = of_ref[...] + ob_ref[...]
    normed = []
    for hd in range(RET_HEADS):
        oh = o[:, hd * RET_V:(hd + 1) * RET_V]
        dev = oh - jnp.mean(oh, axis=-1, keepdims=True)
        normed.append(dev * lax.rsqrt(jnp.mean(dev * dev, axis=-1, keepdims=True) + NORM_EPS))
    g = rg_ref[...].astype(F32)
    r = (g * _sigmoid(g)) * jnp.concatenate(normed, axis=1)
    y = (_sigmoid(gt_ref[:, 0:d].astype(F32)) * _dot(a_tile, wa_ref[...])
         + _sigmoid(gt_ref[:, d:2 * d].astype(F32)) * _dot(w_tile, ww_ref[...])
         + _sigmoid(gt_ref[:, 2 * d:3 * d].astype(F32)) * _dot(r.astype(BF16), wr_ref[...]))
    z = _dot(y.astype(BF16), wo_ref[...])
    g1 = mod_ref[pl.ds(bi, 1), 2 * d:3 * d]
    sh2 = mod_ref[pl.ds(bi, 1), 3 * d:4 * d]
    sc2 = mod_ref[pl.ds(bi, 1), 4 * d:5 * d]
    h1 = h_tile + g1 * _rms(z, gpost_ref[...])
    h1_ref[...] = h1
    v = _rms(h1, gffn_ref[...]) * (1.0 + sc2) + sh2
    v_ref[...] = v.astype(v_ref.dtype)
    vp_ref[...] = _pack_halves(v)


def _merge(a, w, o_f, o_b, ret, gates, h, mods, gpost, gffn, wa, ww, wr, wo, *, n_rows, n_batch, seq):
    d = h[0].shape[1]
    n_lat_tiles = n_batch * seq // TM
    rows = lambda i: (i, 0)
    lat_rows = lambda i: (jnp.minimum(i, n_lat_tiles - 1), 0)
    ctx_rows = lambda i: (jnp.maximum(i - n_lat_tiles, 0), 0)
    const = lambda i: (0, 0)
    rv = RET_HEADS * RET_V
    rg_col = (2 * RET_HEADS * RET_QK + rv) // rv
    outs = [jax.ShapeDtypeStruct((n_rows, d), F32), jax.ShapeDtypeStruct((n_rows, d), BF16),
            jax.ShapeDtypeStruct((n_rows, d // 2), jnp.uint32)]
    return pl.pallas_call(
        functools.partial(_merge_kernel, tiles_per_batch=seq // TM, n_batch=n_batch, d=d),
        grid=(n_rows // TM,),
        in_specs=[pl.BlockSpec((TM, a[0].shape[1]), lat_rows), pl.BlockSpec((TM, a[1].shape[1]), ctx_rows),
                  pl.BlockSpec((TM, w[0].shape[1]), lat_rows), pl.BlockSpec((TM, w[1].shape[1]), ctx_rows),
                  pl.BlockSpec((TM, rv), rows), pl.BlockSpec((TM, rv), rows),
                  pl.BlockSpec((TM, rv), lambda i: (i, rg_col)),
                  pl.BlockSpec((TM, 3 * d), rows), pl.BlockSpec((TM, d), lat_rows),
                  pl.BlockSpec((TM, d), lambda i: (h[2] + jnp.maximum(i - n_lat_tiles, 0), 0)),
                  pl.BlockSpec(mods.shape, const), pl.BlockSpec((1, d), const), pl.BlockSpec((1, d), const),
                  pl.BlockSpec(wa.shape, const), pl.BlockSpec(ww.shape, const),
                  pl.BlockSpec(wr.shape, const), pl.BlockSpec(wo.shape, const)],
        out_specs=[pl.BlockSpec((TM, o.shape[1]), rows) for o in outs],
        out_shape=outs,
        compiler_params=_cparams(("arbitrary",)),
        name="merge",
    )(a[0], a[1], w[0], w[1], o_f, o_b, ret, gates, h[0], h[1], mods, gpost, gffn, wa, ww, wr, wo)


def _router_kernel(v_ref, rw_ref, rb_ref, eidx_ref, rank_ref, w_ref, cnt_ref, carry_ref):
    @pl.when(pl.program_id(0) == 0)
    def _():
        carry_ref[...] = jnp.zeros_like(carry_ref)

    tm = v_ref.shape[0]
    scores = _sigmoid(_dot_nt(rw_ref[...], v_ref[...]))
    sel = scores + rb_ref[...]
    neg = -jnp.inf
    n_grp, per = N_EXPERT_GROUPS, EXPERTS_PER_GROUP

    sel3 = sel.reshape(n_grp, per, tm)
    member_id = lax.broadcasted_iota(jnp.int32, sel3.shape, 1)
    m1 = jnp.max(sel3, axis=1, keepdims=True)
    i1 = jnp.min(jnp.where(sel3 == m1, member_id, per), axis=1, keepdims=True)
    m2 = jnp.max(jnp.where(member_id == i1, neg, sel3), axis=1, keepdims=True)
    gscore = (m1 + m2).reshape(n_grp, tm)
    gid = lax.broadcasted_iota(jnp.int32, gscore.shape, 0)
    ahead = jnp.zeros(gscore.shape, jnp.int32)
    for gj in range(n_grp):
        other = gscore[gj:gj + 1, :]
        ahead = ahead + jnp.where((other > gscore) | ((other == gscore) & (gid > gj)), 1, 0)
    group_ok = (ahead < TOPK_GROUPS).reshape(n_grp, 1, tm)
    sel = jnp.where(group_ok, sel3, NEG_INF).reshape(N_EXPERTS, tm)

    eid = lax.broadcasted_iota(jnp.int32, sel.shape, 0)
    chosen = jnp.zeros(sel.shape, jnp.bool_)
    picks = []
    for _ in range(TOP_K):
        m = jnp.max(sel, axis=0, keepdims=True)
        idx = jnp.min(jnp.where(sel == m, eid, N_EXPERTS), axis=0, keepdims=True)
        hit = eid == idx
        chosen = chosen | hit
        sel = jnp.where(hit, neg, sel)
        picks.append(idx)
    w = jnp.where(chosen, scores, 0.0)
    gate = ROUTED_SCALE * w / jnp.sum(w, axis=0, keepdims=True)

    member = jnp.where(chosen, 1.0, 0.0)
    earlier = lax.broadcasted_iota(jnp.int32, (tm, tm), 0) < lax.broadcasted_iota(jnp.int32, (tm, tm), 1)
    pos = _dot(member.astype(BF16), jnp.where(earlier, 1.0, 0.0).astype(BF16)) + carry_ref[...]
    for k, idx in enumerate(picks):
        hit = eid == idx
        eidx_ref[k:k + 1, :] = idx
        rank_ref[k:k + 1, :] = jnp.sum(jnp.where(hit, pos, 0.0), axis=0, keepdims=True)
        w_ref[k:k + 1, :] = jnp.sum(jnp.where(hit, gate, 0.0), axis=0, keepdims=True)
    carry_ref[...] += jnp.sum(member, axis=1, keepdims=True)
    cnt_ref[...] = carry_ref[...]


def _router(v, rw_t, rb):
    n_rows, d = v.shape
    cols = lambda i: (0, i)
    const = lambda i: (0, 0)
    outs = [jax.ShapeDtypeStruct((TOP_K, n_rows), jnp.int32), jax.ShapeDtypeStruct((TOP_K, n_rows), F32),
            jax.ShapeDtypeStruct((TOP_K, n_rows), F32), jax.ShapeDtypeStruct((N_EXPERTS, 1), F32)]
    return pl.pallas_call(
        _router_kernel,
        grid=(n_rows // TM,),
        in_specs=[pl.BlockSpec((TM, d), lambda i: (i, 0)), pl.BlockSpec(rw_t.shape, const),
                  pl.BlockSpec(rb.shape, const)],
        out_specs=[pl.BlockSpec((TOP_K, TM), cols), pl.BlockSpec((TOP_K, TM), cols),
                   pl.BlockSpec((TOP_K, TM), cols), pl.BlockSpec((N_EXPERTS, 1), const)],
        out_shape=outs,
        scratch_shapes=[pltpu.VMEM((N_EXPERTS, 1), F32)],
        compiler_params=_cparams(("arbitrary",)),
        name="router",
    )(v, rw_t, rb)


def _slots_kernel(eidx_ref, rank_ref, cnt_ref, slot_ref):
    tm = eidx_ref.shape[1]
    eid = lax.broadcasted_iota(jnp.int32, (N_EXPERTS, tm), 0)
    for k in range(TOP_K):
        before = jnp.sum(jnp.where(eid < eidx_ref[k:k + 1, :], cnt_ref[...], 0.0), axis=0, keepdims=True)
        slot_ref[k:k + 1, :] = (before + rank_ref[k:k + 1, :]).astype(jnp.int32)


def _slots(eidx, rank, cnt):
    n_rows = eidx.shape[1]
    tm = next(c for c in (2048, 1024, 512, 256) if n_rows % c == 0)
    cols = lambda i: (0, i)
    return pl.pallas_call(
        _slots_kernel,
        grid=(n_rows // tm,),
        in_specs=[pl.BlockSpec((TOP_K, tm), cols), pl.BlockSpec((TOP_K, tm), cols),
                  pl.BlockSpec(cnt.shape, lambda i: (0, 0))],
        out_specs=pl.BlockSpec((TOP_K, tm), cols),
        out_shape=jax.ShapeDtypeStruct((TOP_K, n_rows), jnp.int32),
        compiler_params=_cparams(("arbitrary",)),
        name="slots",
    )(eidx, rank, cnt)


SC_WINDOW = 128


def _sc_mesh():
    return plsc.VectorSubcoreMesh(core_axis_name="core", subcore_axis_name="subcore")


def _sc_dispatch(rows, slot_t, n_out):
    width = rows.shape[1]
    n_chunks = slot_t.shape[1] // SC_WINDOW
    info = plsc.get_sparse_core_info()
    n_workers = info.num_cores * info.num_subcores

    @functools.partial(
        pl.kernel, mesh=_sc_mesh(),
        out_type=jax.ShapeDtypeStruct((n_out, width), rows.dtype),
        scratch_types=[pltpu.VMEM((TOP_K, SC_WINDOW), jnp.int32), pltpu.VMEM((SC_WINDOW, width), rows.dtype)],
        name="moe_dispatch")
    def run(rows_hbm, idx_hbm, out_hbm, idx_v, rows_v):
        wid = lax.axis_index("subcore") * info.num_cores + lax.axis_index("core")

        @pl.loop(wid, n_chunks, step=n_workers)
        def _(c):
            r0 = pl.multiple_of(c * SC_WINDOW, SC_WINDOW)
            pltpu.sync_copy(idx_hbm.at[:, pl.ds(r0, SC_WINDOW)], idx_v)
            pltpu.sync_copy(rows_hbm.at[pl.ds(r0, SC_WINDOW)], rows_v)
            for k in range(TOP_K):
                pltpu.sync_copy(rows_v, out_hbm.at[idx_v.at[k]])

    return run(rows, slot_t)


def _sc_collect(rows, slot_t):
    n_picks, n_rows = slot_t.shape
    width = rows.shape[1]
    n_chunks = n_rows // SC_WINDOW
    info = plsc.get_sparse_core_info()
    n_workers = info.num_cores * info.num_subcores

    @functools.partial(
        pl.kernel, mesh=_sc_mesh(),
        out_type=jax.ShapeDtypeStruct((n_picks, n_rows, width), rows.dtype),
        scratch_types=[pltpu.VMEM((TOP_K, SC_WINDOW), jnp.int32), pltpu.VMEM((SC_WINDOW, width), rows.dtype)],
        name="moe_collect")
    def run(rows_hbm, idx_hbm, out_hbm, idx_v, rows_v):
        wid = lax.axis_index("subcore") * info.num_cores + lax.axis_index("core")

        @pl.loop(wid, n_chunks, step=n_workers)
        def _(c):
            r0 = pl.multiple_of(c * SC_WINDOW, SC_WINDOW)
            pltpu.sync_copy(idx_hbm.at[:, pl.ds(r0, SC_WINDOW)], idx_v)
            for k in range(TOP_K):
                pltpu.sync_copy(rows_hbm.at[idx_v.at[k]], rows_v)
                pltpu.sync_copy(rows_v, out_hbm.at[k, pl.ds(r0, SC_WINDOW)])

    return run(rows, slot_t)


EXPERT_TILE = 512


def _work_items(cnt, n_slots):
    counts = cnt[:, 0].astype(jnp.int32)
    ends = jnp.cumsum(counts)
    n_tiles = n_slots // EXPERT_TILE
    bounds = jnp.sort(jnp.concatenate([jnp.arange(n_tiles, dtype=jnp.int32) * EXPERT_TILE, ends - counts]))
    nxt = jnp.concatenate([bounds[1:], jnp.array([n_slots], jnp.int32)])
    tile = jnp.minimum(bounds // EXPERT_TILE, n_tiles - 1)
    expert = jnp.sum((ends[None, :] <= bounds[:, None]).astype(jnp.int32), axis=1)
    expert = jnp.minimum(expert, N_EXPERTS - 1)
    return tile, expert, bounds - tile * EXPERT_TILE, nxt - tile * EXPERT_TILE


XS_RING = 3


def _experts_kernel(tile_ref, exp_ref, lo_ref, hi_ref, xs_hbm, wg_ref, wu_ref, wd_ref, ys_ref,
                    acc_ref, wgb_ref, wub_ref, wdb_ref, xbuf_ref, xsem):
    i = pl.program_id(0)
    n_items = pl.num_programs(0)
    lo, hi = lo_ref[i], hi_ref[i]

    def tile_copy(item):
        slot = lax.rem(item, XS_RING)
        row0 = pl.multiple_of(tile_ref[item] * EXPERT_TILE, EXPERT_TILE)
        return pltpu.make_async_copy(xs_hbm.at[pl.ds(row0, EXPERT_TILE)], xbuf_ref.at[slot], xsem.at[slot])

    @pl.when(i == 0)
    def _():
        for ahead in range(XS_RING - 1):
            pl.when(ahead < n_items)(lambda: tile_copy(ahead).start())

    @pl.when(i + XS_RING - 1 < n_items)
    def _():
        tile_copy(i + XS_RING - 1).start()

    tile_copy(i).wait()
    xs_ref = xbuf_ref.at[lax.rem(i, XS_RING)]

    @pl.when((i == 0) | (exp_ref[i] != exp_ref[jnp.maximum(i - 1, 0)]))
    def _():
        wgb_ref[...] = wg_ref[...].astype(BF16)
        wub_ref[...] = wu_ref[...].astype(BF16)
        wdb_ref[...] = wd_ref[...].astype(BF16)

    def ffn():
        x_lo, x_hi = _unpack_halves(xs_ref[...])
        x_lo, x_hi = x_lo.astype(BF16), x_hi.astype(BF16)
        n = x_lo.shape[1]
        a = _dot(x_lo, wgb_ref[0:n, :]) + _dot(x_hi, wgb_ref[n:, :])
        u = _dot(x_lo, wub_ref[0:n, :]) + _dot(x_hi, wub_ref[n:, :])
        return _dot(((a * _sigmoid(a)) * u).astype(BF16), wdb_ref[...])

    whole = (lo == 0) & (hi == EXPERT_TILE)

    @pl.when(whole)
    def _():
        ys_ref[...] = _pack_halves(ffn())

    @pl.when(jnp.logical_not(whole) & (hi > lo))
    def _():
        y = ffn()
        row = lax.broadcasted_iota(jnp.int32, y.shape, 0)
        y = jnp.where((row >= lo) & (row < hi), y, 0.0)

        @pl.when(lo == 0)
        def _():
            acc_ref[...] = y

        @pl.when((lo > 0) & (hi < EXPERT_TILE))
        def _():
            acc_ref[...] += y

        @pl.when((lo > 0) & (hi == EXPERT_TILE))
        def _():
            ys_ref[...] = _pack_halves(acc_ref[...] + y)


def _experts(xs, items, layer, exp_wg, exp_wu, exp_wd):
    n_slots, half = xs.shape
    d, hid = exp_wg.shape[-2:]
    tile, expert, lo, hi = items
    grid_spec = pltpu.PrefetchScalarGridSpec(
        num_scalar_prefetch=4,
        grid=(tile.shape[0],),
        in_specs=[pl.BlockSpec(memory_space=pl.ANY),
                  pl.BlockSpec((None, None, d, hid), lambda i, t, e, lo, hi: (layer, e[i], 0, 0)),
                  pl.BlockSpec((None, None, d, hid), lambda i, t, e, lo, hi: (layer, e[i], 0, 0)),
                  pl.BlockSpec((None, None, hid, d), lambda i, t, e, lo, hi: (layer, e[i], 0, 0))],
        out_specs=pl.BlockSpec((EXPERT_TILE, half), lambda i, t, e, lo, hi: (t[i], 0)),
        scratch_shapes=[pltpu.VMEM((EXPERT_TILE, d), F32), pltpu.VMEM((d, hid), BF16),
                        pltpu.VMEM((d, hid), BF16), pltpu.VMEM((hid, d), BF16),
                        pltpu.VMEM((XS_RING, EXPERT_TILE, half), jnp.uint32),
                        pltpu.SemaphoreType.DMA((XS_RING,))])
    return pl.pallas_call(
        _experts_kernel,
        grid_spec=grid_spec,
        out_shape=jax.ShapeDtypeStruct((n_slots, half), jnp.uint32),
        compiler_params=_cparams(("arbitrary",)),
        name="experts",
    )(tile, expert, lo, hi, xs, exp_wg, exp_wu, exp_wd)


def _moe_out_tile(yg_ref, w_ref, v_ref, sg_ref, su_ref, sd_ref, h1_ref, mod_ref, gpost_ref,
                  *, tiles_per_batch, n_batch, d):
    i = pl.program_id(0)
    x = v_ref[...]
    a = _dot(x, sg_ref[...])
    f = _dot(((a * _sigmoid(a)) * _dot(x, su_ref[...])).astype(BF16), sd_ref[...])
    n = d // 2
    f_lo, f_hi = f[:, :n], f[:, n:]
    w = w_ref[...]
    for k in range(TOP_K):
        y_lo, y_hi = _unpack_halves(yg_ref[k])
        wk = w[:, k:k + 1]
        f_lo = f_lo + wk * y_lo
        f_hi = f_hi + wk * y_hi
    f = jnp.concatenate([f_lo, f_hi], axis=1)
    bi = jnp.minimum(i // tiles_per_batch, n_batch)
    g2 = mod_ref[pl.ds(bi, 1), 5 * d:6 * d]
    return h1_ref[...] + g2 * _rms(f, gpost_ref[...])


N_MOE_OUT_IN = 9
N_INPROJ_IN = 9


def _moe_out_kernel(*refs, **kw):
    refs[N_MOE_OUT_IN][...] = _moe_out_tile(*refs[:N_MOE_OUT_IN], **kw)


def _moe_out_inproj_kernel(*refs, n_batch, d, tiles_per_batch):
    dims = dict(tiles_per_batch=tiles_per_batch, n_batch=n_batch, d=d)
    h = _moe_out_tile(*refs[:N_MOE_OUT_IN], **dims)
    n_in = N_MOE_OUT_IN + N_INPROJ_IN
    refs[n_in][...] = h
    _inproj_tile(h, *refs[N_MOE_OUT_IN:n_in], *refs[n_in + 1:], **dims)


def _moe_out_specs(yg, w, v, sg, su, sd, h1, mods, gpost, tm):
    d = v.shape[1]
    rows = lambda i: (i, 0)
    const = lambda i: (0, 0)
    once = dict(pipeline_mode=pl.Buffered(1))
    return [pl.BlockSpec((TOP_K, tm, d // 2), lambda i: (0, i, 0)), pl.BlockSpec((tm, TOP_K), rows),
            pl.BlockSpec((tm, d), rows),
            pl.BlockSpec(sg.shape, const, **once), pl.BlockSpec(su.shape, const, **once),
            pl.BlockSpec(sd.shape, const, **once),
            pl.BlockSpec((tm, d), rows), pl.BlockSpec(mods.shape, const), pl.BlockSpec((1, d), const)]


def _moe_out(moe_args, *, n_batch, seq):
    v = moe_args[2]
    n_rows, d = v.shape
    return pl.pallas_call(
        functools.partial(_moe_out_kernel, tiles_per_batch=seq // TM, n_batch=n_batch, d=d),
        grid=(n_rows // TM,),
        in_specs=_moe_out_specs(*moe_args, TM),
        out_specs=pl.BlockSpec((TM, d), lambda i: (i, 0)),
        out_shape=jax.ShapeDtypeStruct((n_rows, d), F32),
        compiler_params=_cparams(("arbitrary",)),
        name="moe_out",
    )(*moe_args)


FUSED_TM = 256


def _moe_out_inproj(moe_args, inproj_args, *, n_batch, seq):
    v = moe_args[2]
    n_rows, d = v.shape
    tm = FUSED_TM
    in_specs, out_specs, out_shape = _inproj_specs(n_rows, d, *inproj_args, tm=tm, n_batch=n_batch, seq=seq)
    rows = lambda i: (i, 0)
    return pl.pallas_call(
        functools.partial(_moe_out_inproj_kernel, tiles_per_batch=seq // tm, n_batch=n_batch, d=d),
        grid=(n_rows // tm,),
        in_specs=_moe_out_specs(*moe_args, tm) + in_specs,
        out_specs=[pl.BlockSpec((tm, d), rows)] + out_specs,
        out_shape=[jax.ShapeDtypeStruct((n_rows, d), F32)] + out_shape,
        compiler_params=_cparams(("arbitrary",)),
        name="moe_out_inproj",
    )(*moe_args, *inproj_args)


def _moe_routed(v, vp, layer, rw, rb, exp_wg, exp_wu, exp_wd):
    n_rows = v.shape[0]
    n_slots = n_rows * TOP_K
    assert n_slots % EXPERT_TILE == 0
    eidx, rank, w_t, cnt = _router(v, rw, rb)
    slot_t = _slots(eidx, rank, cnt)
    xs = _sc_dispatch(vp, slot_t, n_slots)
    ys = _experts(xs, _work_items(cnt, n_slots), layer, exp_wg, exp_wu, exp_wd)
    return _sc_collect(ys, slot_t), w_t.T


def _rope_tables(seq):
    rows = seq // GRID_W
    row_id = np.repeat(np.arange(rows, dtype=np.float64), GRID_W)
    col_id = np.tile(np.arange(GRID_W, dtype=np.float64), rows)

    def tables(rot_dim):
        axis_dim = rot_dim // 2
        inv_freq = ROPE_BASE ** (-np.arange(0, axis_dim, 2, dtype=np.float64) / axis_dim)
        ang_r = row_id[:, None] * inv_freq[None, :]
        ang_c = col_id[:, None] * inv_freq[None, :]
        cos = np.concatenate([np.cos(ang_r), np.cos(ang_r), np.cos(ang_c), np.cos(ang_c)], axis=1)
        sin = np.concatenate([-np.sin(ang_r), np.sin(ang_r), -np.sin(ang_c), np.sin(ang_c)], axis=1)
        return cos, sin

    cos64, sin64 = tables(GQA_DIM)
    cos32, sin32 = tables(MLA_ROPE)
    ones = np.ones((seq, MLA_NOPE))
    pad = LANES - MLA_NOPE - MLA_ROPE
    cospe = np.concatenate([ones, cos32, np.ones((seq, pad))], axis=1)
    sinpe = np.concatenate([0 * ones, sin32, np.zeros((seq, pad))], axis=1)
    tab = np.concatenate([cos64, cos64, sin64, sin64, cospe, sinpe], axis=1)
    ident = np.concatenate([np.ones((TM, LANES)), np.zeros((TM, LANES)),
                            np.ones((TM, LANES)), np.zeros((TM, LANES))], axis=1)
    return jnp.asarray(np.concatenate([tab, ident], axis=0), F32)


def _w_in_moves(d):
    sizes = (MLA_Q_LORA, MLA_KV_LORA, MLA_ROPE, GQA_HEADS * GQA_DIM, GQA_KV_HEADS * GQA_DIM,
             GQA_KV_HEADS * GQA_DIM, RET_HEADS * RET_QK, RET_HEADS * RET_QK, RET_HEADS * RET_V,
             RET_HEADS * RET_V, 3 * d)
    src = [sum(sizes[:i]) for i in range(len(sizes))]
    cq, ckv, kpe, gq, gk, gv, rq, rk, rv, rg, gates = src
    moves = [(cq, C_CQ, MLA_Q_LORA, 1.0), (ckv, C_CKV, MLA_KV_LORA, 1.0), (kpe, C_KPE + MLA_NOPE, MLA_ROPE, 1.0),
             (gq, C_G, GQA_HEADS * GQA_DIM, GQA_DIM ** -0.5 * LOG2_E)]
    dst = C_G + GQA_HEADS * GQA_DIM
    for base in (gk, gv):
        for hd in range(GQA_KV_HEADS):
            for _ in range(2):
                moves.append((base + hd * GQA_DIM, dst, GQA_DIM, 1.0))
                dst += GQA_DIM
    assert dst == C_R
    for s, width, scale in ((rq, RET_HEADS * RET_QK, 1.0), (rk, RET_HEADS * RET_QK, RET_QK ** -0.5),
                            (rv, RET_HEADS * RET_V, 1.0), (rg, RET_HEADS * RET_V, 1.0), (gates, 3 * d, 1.0)):
        moves.append((s, dst, width, scale))
        dst += width
    assert dst == W_COLS
    return moves


def _pack_kernel(w_ref, o_ref, *, moves):
    o_ref[:, C_KPE:C_G] = jnp.zeros((o_ref.shape[0], C_G - C_KPE), o_ref.dtype)
    for s, t, width, scale in moves:
        piece = w_ref[:, s:s + width]
        o_ref[:, t:t + width] = (piece if scale == 1.0 else piece * scale).astype(o_ref.dtype)


def _pack_w_in(w_in):
    n_layers, d, n_cols = w_in.shape
    rows = 256
    return pl.pallas_call(
        functools.partial(_pack_kernel, moves=_w_in_moves(d)),
        grid=(n_layers, d // rows),
        in_specs=[pl.BlockSpec((None, rows, n_cols), lambda l, i: (l, i, 0))],
        out_specs=pl.BlockSpec((None, rows, W_COLS), lambda l, i: (l, i, 0)),
        out_shape=jax.ShapeDtypeStruct((n_layers, d, W_COLS), BF16),
        compiler_params=_cparams(("arbitrary", "arbitrary")),
        name="pack_w_in",
    )(w_in)


def _pack_mla_up(w_uq, w_ukv):
    r = w_uq.shape[0]
    dq = MLA_NOPE + MLA_ROPE
    wq = jnp.pad(w_uq.reshape(r, MLA_HEADS, dq), ((0, 0), (0, 0), (0, LANES - dq))).reshape(r, MLA_HEADS * LANES)
    kv = w_ukv.reshape(r, MLA_HEADS, MLA_NOPE + MLA_V)
    wk = jnp.pad(kv[:, :, :MLA_NOPE], ((0, 0), (0, 0), (0, LANES - MLA_NOPE))).reshape(r, MLA_HEADS * LANES)
    wv = kv[:, :, MLA_NOPE:]
    zeros = jnp.zeros_like(wv)
    even = jnp.concatenate([wv, zeros], axis=2)
    odd = jnp.concatenate([zeros, wv], axis=2)
    wv = jnp.where((jnp.arange(MLA_HEADS) % 2 == 0)[None, :, None], even, odd).reshape(r, MLA_HEADS * LANES)
    return wq.astype(BF16), wk.astype(BF16), wv.astype(BF16)


def kernel(x, c, ctx, c_ctx, ada_w, ada_b, norm_mix_pre, norm_mix_post, norm_ffn_pre, norm_ffn_post, w_in, mla_q_norm, mla_w_uq, mla_kv_norm, mla_w_ukv, gqa_sink, ret_decay_fwd, ret_decay_bwd, w_br_mla, w_br_gqa, w_br_ret, w_out, router_w, router_bias, exp_w_gate, exp_w_up, exp_w_down, shared_w_gate, shared_w_up, shared_w_down):
    n_batch, seq, d = x.shape
    n_ctx = ctx.shape[1]
    depth = ada_w.shape[0]
    n_lat_rows = n_batch * seq
    assert seq % TM == 0 and (n_batch * n_ctx) % TM == 0 and seq % ATT_TQ == 0 and seq % n_ctx == 0
    assert n_batch < MOD_ROWS and seq % GRID_W == 0

    cond = jnp.zeros((MOD_ROWS, d), F32).at[:n_batch].set(c).at[n_batch].set(c_ctx)
    mods_all = _adaln(cond, ada_w, ada_b)
    rope = _rope_tables(seq)
    h = (x.reshape(n_lat_rows, d), ctx.reshape(n_batch * n_ctx, d), 0)
    n_all_rows = n_lat_rows + n_batch * n_ctx
    row = lambda p: p.reshape(1, -1)
    dims = dict(n_batch=n_batch, seq=seq)

    w_in_packed = _pack_w_in(w_in)

    def inproj_args(l):
        return (mods_all[l], row(norm_mix_pre[l]), w_in_packed[l], rope, row(mla_q_norm[l]),
                row(mla_kv_norm[l]), *_pack_mla_up(mla_w_uq[l], mla_w_ukv[l]))

    projected = _inproj(h[0], h[1], inproj_args(0), **dims)
    for l in range(depth):
        last = l == depth - 1
        mods = mods_all[l]
        mq, mk, mv, gqa, ret, gates = projected
        a = _mla_attention(mq, mk, mv, ctx=n_ctx, with_ctx_queries=not last, **dims)
        sink_tab = jnp.broadcast_to(gqa_sink[l].astype(F32)[:, None] * LOG2_E, (GQA_HEADS, LANES))
        w = _window_attention(gqa, sink_tab, ctx=n_ctx, with_ctx_queries=not last, **dims)
        lg = jnp.concatenate([jax.nn.log_sigmoid(ret_decay_fwd[l].astype(F32)),
                              jax.nn.log_sigmoid(ret_decay_bwd[l].astype(F32))])
        o_f, o_b = _retention(ret, jnp.broadcast_to(lg[:, None], (2 * RET_HEADS, LANES)), ctx=n_ctx, **dims)
        n_rows = n_lat_rows if last else n_all_rows
        h1, v, vp = _merge(a, w, o_f, o_b, ret, gates, h, mods, row(norm_mix_post[l]), row(norm_ffn_pre[l]),
                           w_br_mla[l].astype(BF16), w_br_gqa[l].astype(BF16), w_br_ret[l].astype(BF16),
                           w_out[l].astype(BF16), n_rows=n_rows, **dims)
        yg, gate_w = _moe_routed(v, vp, l, router_w[l].T.astype(BF16), router_bias[l].astype(F32).reshape(-1, 1),
                                 exp_w_gate, exp_w_up, exp_w_down)
        moe_args = (yg, gate_w, v, shared_w_gate[l].astype(BF16), shared_w_up[l].astype(BF16),
                    shared_w_down[l].astype(BF16), h1, mods, row(norm_ffn_post[l]))
        if last:
            out = _moe_out(moe_args, **dims)
        else:
            stream, *projected = _moe_out_inproj(moe_args, inproj_args(l + 1), **dims)
            h = (stream, stream, n_lat_rows // TM)
    return out[:n_lat_rows].reshape(n_batch, seq, d)
```

```python
import functools

import numpy as np
import jax
import jax.numpy as jnp
from jax import lax
from jax.experimental import pallas as pl
from jax.experimental.pallas import tpu as pltpu
from jax.experimental.pallas import tpu_sc as plsc

F32 = jnp.float32
BF16 = jnp.bfloat16

GRID_W = 64
ROPE_BASE = 10000.0
NORM_EPS = 1e-6
NEG_INF = -1e30
LOG2_E = 1.4426950408889634
N_MOD = 6
MLA_HEADS, MLA_NOPE, MLA_ROPE, MLA_V = 8, 64, 32, 64
MLA_Q_LORA, MLA_KV_LORA = 256, 256
GQA_HEADS, GQA_KV_HEADS, GQA_DIM, WINDOW = 8, 2, 64, 128
RET_HEADS, RET_QK, RET_V, RET_CHUNK = 4, 64, 128, 128
N_EXPERTS, N_EXPERT_GROUPS, TOPK_GROUPS, TOP_K = 64, 8, 4, 8
EXPERTS_PER_GROUP = N_EXPERTS // N_EXPERT_GROUPS
ROUTED_SCALE = 2.5

LANES = 128
TM = 512
ATT_TQ = 512
WIN_TQ = 256
RET_TILE = 256
MOD_ROWS = 8
V7X_VMEM_LIMIT = 56 * 1024 * 1024

D_MODEL = 1024
C_CQ = 0
C_CKV = C_CQ + MLA_Q_LORA
C_KPE = C_CKV + MLA_KV_LORA
C_G = C_KPE + LANES
C_R = C_G + GQA_HEADS * GQA_DIM + 4 * GQA_KV_HEADS * GQA_DIM
C_GATE = C_R + 2 * RET_HEADS * RET_QK + 2 * RET_HEADS * RET_V
W_COLS = C_GATE + 3 * D_MODEL


def _cparams(sem):
    return pltpu.CompilerParams(dimension_semantics=sem, vmem_limit_bytes=V7X_VMEM_LIMIT)


def _rms(x, g):
    return x * lax.rsqrt(jnp.mean(x * x, axis=-1, keepdims=True) + NORM_EPS) * g


def _sigmoid(x):
    return 0.5 * jnp.tanh(0.5 * x) + 0.5


def _dot(a, b):
    return jnp.dot(a, b, preferred_element_type=F32)


def _dot_nt(a, b):
    return lax.dot_general(a, b, (((1,), (1,)), ((), ())), preferred_element_type=F32)


def _dot_tn(a, b):
    return lax.dot_general(a, b, (((0,), (0,)), ((), ())), preferred_element_type=F32)


def _rope(x, cos, sin, half):
    n = x.shape[-1]
    reps = n // LANES
    if reps > 1:
        cos = jnp.concatenate([cos] * reps, axis=1)
        sin = jnp.concatenate([sin] * reps, axis=1)
    lane = lax.broadcasted_iota(jnp.int32, x.shape, 1)
    up = pltpu.roll(x, half, 1)
    dn = pltpu.roll(x, n - half, 1)
    partner = jnp.where((lane & (2 * half - 1)) < half, dn, up)
    return x * cos + partner * sin


def _lane_lo(shape):
    return (lax.broadcasted_iota(jnp.int32, shape, 1) & (LANES - 1)) < (LANES // 2)


def _pack_halves(x):
    n = x.shape[1] // 2
    bits = lambda t: lax.bitcast_convert_type(t.astype(BF16).astype(F32), jnp.uint32)
    return (bits(x[:, :n]) >> 16) | bits(x[:, n:])


def _unpack_halves(p):
    lo = lax.bitcast_convert_type(p << 16, F32)
    hi = lax.bitcast_convert_type(p & jnp.uint32(0xFFFF0000), F32)
    return lo, hi


def _ada_kernel(c_ref, w_ref, b_ref, o_ref):
    c = c_ref[...]
    s = c * _sigmoid(c)
    o_ref[...] = _dot(s.astype(BF16), w_ref[...].astype(BF16)) + b_ref[...]


def _adaln(cond, ada_w, ada_b):
    n_layers, d, n = ada_w.shape
    tn = 1024
    return pl.pallas_call(
        _ada_kernel,
        grid=(n_layers, n // tn),
        in_specs=[pl.BlockSpec((MOD_ROWS, d), lambda l, j: (0, 0)),
                  pl.BlockSpec((None, d, tn), lambda l, j: (l, 0, j)),
                  pl.BlockSpec((None, 1, tn), lambda l, j: (l, 0, j))],
        out_specs=pl.BlockSpec((None, MOD_ROWS, tn), lambda l, j: (l, 0, j)),
        out_shape=jax.ShapeDtypeStruct((n_layers, MOD_ROWS, n), F32),
        compiler_params=_cparams(("arbitrary", "arbitrary")),
        name="adaln",
    )(cond, ada_w, ada_b.reshape(n_layers, 1, n))


def _inproj_tile(h, mod_ref, gpre_ref, w_ref, rope_ref, qn_ref, kvn_ref, wuq_ref, wuk_ref, wuv_ref,
                 mq_ref, mk_ref, mv_ref, gqa_ref, ret_ref, gate_ref, *, tiles_per_batch, n_batch, d):
    i = pl.program_id(0)
    bi = jnp.minimum(i // tiles_per_batch, n_batch)
    sh = mod_ref[pl.ds(bi, 1), 0:d]
    sc = mod_ref[pl.ds(bi, 1), d:2 * d]
    u = (_rms(h, gpre_ref[...]) * (1.0 + sc) + sh).astype(BF16)

    cos64 = rope_ref[:, 0:LANES]
    sin64 = rope_ref[:, LANES:2 * LANES]
    cospe = rope_ref[:, 2 * LANES:3 * LANES]
    sinpe = rope_ref[:, 3 * LANES:4 * LANES]

    c = _dot(u, w_ref[:, C_CQ:C_G])
    kpe = _rope(c[:, C_KPE:C_G], cospe, sinpe, MLA_ROPE // 4)
    qn = _rms(c[:, C_CQ:C_CKV], qn_ref[...]).astype(BF16)
    q = _rope(_dot(qn, wuq_ref[...]), cospe, sinpe, MLA_ROPE // 4)
    mq_ref[...] = (q * ((MLA_NOPE + MLA_ROPE) ** -0.5 * LOG2_E)).astype(mq_ref.dtype)
    kvn = _rms(c[:, C_CKV:C_KPE], kvn_ref[...]).astype(BF16)
    k = _dot(kvn, wuk_ref[...]) + jnp.concatenate([kpe] * MLA_HEADS, axis=1)
    mk_ref[...] = k.astype(mk_ref.dtype)
    v = _dot(kvn, wuv_ref[...])
    lane = lax.broadcasted_iota(jnp.int32, v.shape, 1)
    value_lane = ((lane & (LANES - 1)) < MLA_V) == (((lane >> (LANES.bit_length() - 1)) & 1) == 0)
    mv_ref[...] = jnp.where(value_lane, v, 1.0).astype(mv_ref.dtype)

    g = _dot(u, w_ref[:, C_G:C_R])
    n_qk = GQA_HEADS * GQA_DIM + 2 * GQA_KV_HEADS * GQA_DIM
    gqa_ref[:, 0:n_qk] = _rope(g[:, 0:n_qk], cos64, sin64, GQA_DIM // 4).astype(gqa_ref.dtype)
    gqa_ref[:, n_qk:] = g[:, n_qk:].astype(gqa_ref.dtype)

    r = _dot(u, w_ref[:, C_R:C_GATE])
    n_qk = 2 * RET_HEADS * RET_QK
    ret_ref[:, 0:n_qk] = _rope(r[:, 0:n_qk], cos64, sin64, RET_QK // 4).astype(ret_ref.dtype)
    ret_ref[:, n_qk:] = r[:, n_qk:].astype(ret_ref.dtype)

    gate_ref[...] = _dot(u, w_ref[:, C_GATE:W_COLS]).astype(gate_ref.dtype)


def _inproj_kernel(hl_ref, hc_ref, *refs, tiles_per_batch, n_batch, d):
    latent = pl.program_id(0) < tiles_per_batch * n_batch
    _inproj_tile(jnp.where(latent, hl_ref[...], hc_ref[...]), *refs,
                 tiles_per_batch=tiles_per_batch, n_batch=n_batch, d=d)


def _inproj_specs(t, d, mods, gpre, w_all, rope, qn, kvn, wuq, wuk, wuv, *, layer, tm, n_batch, seq):
    tiles_per_batch = seq // tm
    n_lat_tiles = n_batch * tiles_per_batch
    const = lambda i: (0, 0)
    rows = lambda i: (i, 0)
    rope_idx = lambda i: (jnp.where(i < n_lat_tiles, i % tiles_per_batch, tiles_per_batch), 0)
    once = dict(pipeline_mode=pl.Buffered(1))
    hq = MLA_HEADS * LANES
    outs = [jax.ShapeDtypeStruct((t, hq), BF16), jax.ShapeDtypeStruct((t, hq), BF16),
            jax.ShapeDtypeStruct((t, hq), BF16),
            jax.ShapeDtypeStruct((t, C_R - C_G), BF16),
            jax.ShapeDtypeStruct((t, C_GATE - C_R), F32),
            jax.ShapeDtypeStruct((t, W_COLS - C_GATE), BF16)]
    in_specs = [pl.BlockSpec(mods.shape, const),
                pl.BlockSpec((1, d), const),
                pl.BlockSpec((None,) + w_all.shape[1:], lambda i: (layer, 0, 0), **once),
                pl.BlockSpec((tm, 4 * LANES), rope_idx),
                pl.BlockSpec(qn.shape, const), pl.BlockSpec(kvn.shape, const),
                pl.BlockSpec(wuq.shape, const, **once), pl.BlockSpec(wuk.shape, const, **once),
                pl.BlockSpec(wuv.shape, const, **once)]
    return in_specs, [pl.BlockSpec((tm, o.shape[1]), rows) for o in outs], outs


def _inproj(h_lat, h_ctx, inproj_args, *, layer, n_batch, seq):
    d = h_lat.shape[1]
    t = h_lat.shape[0] + h_ctx.shape[0]
    n_lat_tiles = h_lat.shape[0] // TM
    in_specs, out_specs, out_shape = _inproj_specs(t, d, *inproj_args, layer=layer, tm=TM, n_batch=n_batch, seq=seq)
    return pl.pallas_call(
        functools.partial(_inproj_kernel, tiles_per_batch=seq // TM, n_batch=n_batch, d=d),
        grid=(t // TM,),
        in_specs=[pl.BlockSpec((TM, d), lambda i: (jnp.minimum(i, n_lat_tiles - 1), 0)),
                  pl.BlockSpec((TM, d), lambda i: (jnp.maximum(i - n_lat_tiles, 0), 0))] + in_specs,
        out_specs=out_specs,
        out_shape=out_shape,
        compiler_params=_cparams(("arbitrary",)),
        name="inproj",
    )(h_lat, h_ctx, *inproj_args)


MLA_HEADS_PER_STEP = 4


def _mla_kernel(q_ref, kl_ref, kc_ref, vl_ref, vc_ref, o_ref, s_ref, p_ref, *, with_lat):
    n_ctx = kc_ref.shape[0]

    def body(with_lat):
        n_keys = n_ctx + (kl_ref.shape[0] if with_lat else 0)

        def scores(h):
            sl = slice(h * LANES, (h + 1) * LANES)
            s_ref[h % 2, :, 0:n_ctx] = _dot_nt(q_ref[:, sl], kc_ref[:, sl])
            if with_lat:
                s_ref[h % 2, :, n_ctx:n_keys] = _dot_nt(q_ref[:, sl], kl_ref[:, sl])

        def probs(h):
            s = s_ref[h % 2, :, 0:n_keys]
            p_ref[h % 2, :, 0:n_keys] = jnp.exp2(s - jnp.max(s, axis=-1, keepdims=True)).astype(BF16)

        def weighted(h):
            sl = slice(h * LANES, (h + 1) * LANES)
            o = _dot(p_ref[h % 2, :, 0:n_ctx], vc_ref[:, sl])
            if with_lat:
                o = o + _dot(p_ref[h % 2, :, n_ctx:n_keys], vl_ref[:, sl])
            return o / pltpu.roll(o, LANES // 2, 1)

        outs = [None] * MLA_HEADS_PER_STEP
        scores(0)
        for h in range(MLA_HEADS_PER_STEP):
            if h + 1 < MLA_HEADS_PER_STEP:
                scores(h + 1)
            probs(h)
            outs[h] = weighted(h)
        for pr in range(MLA_HEADS_PER_STEP // 2):
            even, odd = outs[2 * pr], outs[2 * pr + 1]
            o_ref[:, pr * LANES:(pr + 1) * LANES] = jnp.where(_lane_lo(even.shape), even, odd).astype(o_ref.dtype)

    body(with_lat)


def _mla_call(mq, mk, mv, *, n_batch, seq, ctx, latent_queries):
    hps = MLA_HEADS_PER_STEP
    ctx_blk0 = n_batch * seq // ctx
    tq = ATT_TQ if latent_queries else ctx
    nq = seq // tq if latent_queries else 1
    q_blk0 = 0 if latent_queries else n_batch * seq // tq
    q_idx = lambda b, g, i: (q_blk0 + b * nq + i, g)
    ctx_idx = lambda b, g, i: (ctx_blk0 + b, g)
    lat_idx = (lambda b, g, i: (b, g)) if latent_queries else ctx_idx
    n_lat = seq if latent_queries else ctx
    n_keys = ctx + (seq if latent_queries else 0)
    in_specs = [pl.BlockSpec((tq, hps * LANES), q_idx),
                pl.BlockSpec((n_lat, hps * LANES), lat_idx), pl.BlockSpec((ctx, hps * LANES), ctx_idx),
                pl.BlockSpec((n_lat, hps * LANES), lat_idx), pl.BlockSpec((ctx, hps * LANES), ctx_idx)]
    return pl.pallas_call(
        functools.partial(_mla_kernel, with_lat=latent_queries),
        grid=(n_batch, MLA_HEADS // hps, nq),
        in_specs=in_specs,
        out_specs=pl.BlockSpec((tq, hps * MLA_V), lambda b, g, i: (b * nq + i, g)),
        out_shape=jax.ShapeDtypeStruct((n_batch * nq * tq, MLA_HEADS * MLA_V), BF16),
        scratch_shapes=[pltpu.VMEM((2, tq, n_keys), F32), pltpu.VMEM((2, tq, n_keys), BF16)],
        compiler_params=_cparams(("arbitrary", "arbitrary", "arbitrary")),
        name="mla_attn" if latent_queries else "mla_attn_ctx",
    )(mq, mk, mk, mv, mv)


def _mla_attention(mq, mk, mv, *, n_batch, seq, ctx, with_ctx_queries):
    dims = dict(n_batch=n_batch, seq=seq, ctx=ctx)
    lat = _mla_call(mq, mk, mv, latent_queries=True, **dims)
    return lat, (_mla_call(mq, mk, mv, latent_queries=False, **dims) if with_ctx_queries else lat)


def _win_kernel(q_ref, kp_ref, kcur_ref, kn_ref, vp_ref, vcur_ref, vn_ref, kc_ref, vc_ref, sink_ref, o_ref,
                s_ref, p_ref, *, seq, with_lat):
    i = pl.program_id(1)
    tq = q_ref.shape[0]
    group = GQA_HEADS // GQA_KV_HEADS

    def body(with_lat):
        n_ctx = kc_ref.shape[0]
        n_keys = n_ctx + (tq + 2 * WINDOW if with_lat else 0)
        if with_lat:
            q_pos = i * tq + lax.broadcasted_iota(jnp.int32, (tq, n_keys), 0)
            k_pos = i * tq - WINDOW - n_ctx + lax.broadcasted_iota(jnp.int32, (tq, n_keys), 1)
            in_band = (jnp.abs(q_pos - k_pos) <= WINDOW) & (k_pos >= 0) & (k_pos < seq)
            valid = in_band | (lax.broadcasted_iota(jnp.int32, (tq, n_keys), 1) < n_ctx)
        lo = _lane_lo((tq, LANES))
        lo_k = _lane_lo((n_keys, LANES))
        keys, values = [], []
        for kv in range(GQA_KV_HEADS):
            sl = slice(kv * LANES, (kv + 1) * LANES)
            if with_lat:
                k_all = jnp.concatenate([kc_ref[:, sl], kp_ref[:, sl], kcur_ref[:, sl], kn_ref[:, sl]], axis=0)
                v_all = jnp.concatenate([vc_ref[:, sl], vp_ref[:, sl], vcur_ref[:, sl], vn_ref[:, sl]], axis=0)
            else:
                k_all, v_all = kc_ref[:, sl], vc_ref[:, sl]
            keys.append(k_all)
            one = jnp.ones_like(v_all)
            values.append((jnp.where(lo_k, v_all, one), jnp.where(lo_k, one, v_all)))

        def scores(hd):
            kv, pair = hd // group, hd // 2
            qp = q_ref[:, pair * LANES:(pair + 1) * LANES]
            qm = jnp.where(lo if hd % 2 == 0 else jnp.logical_not(lo), qp, jnp.zeros_like(qp))
            s = _dot_nt(qm, keys[kv])
            s_ref[hd % 2, :, 0:n_keys] = jnp.where(valid, s, NEG_INF) if with_lat else s

        def probs(hd):
            s = s_ref[hd % 2, :, 0:n_keys]
            m = jnp.maximum(jnp.max(s, axis=-1, keepdims=True), sink_ref[hd:hd + 1, 0:1])
            p_ref[hd % 2, :, 0:n_keys] = jnp.exp2(s - m).astype(BF16)
            return jnp.exp2(sink_ref[hd:hd + 1, 0:1] - m)

        def weighted(hd, sink_term):
            o = _dot(p_ref[hd % 2, :, 0:n_keys], values[hd // group][hd % 2])
            return o / (pltpu.roll(o, LANES // 2, 1) + sink_term)

        outs = [None] * GQA_HEADS
        scores(0)
        for hd in range(GQA_HEADS):
            if hd + 1 < GQA_HEADS:
                scores(hd + 1)
            outs[hd] = weighted(hd, probs(hd))
        for pair in range(GQA_HEADS // 2):
            o_ref[:, pair * LANES:(pair + 1) * LANES] = jnp.where(
                lo, outs[2 * pair], outs[2 * pair + 1]).astype(o_ref.dtype)

    body(with_lat)


def _win_call(gqa, sink_tab, *, n_batch, seq, ctx, latent_queries):
    tq = WIN_TQ if latent_queries else ctx
    nq = seq // tq if latent_queries else 1
    q_blk0 = 0 if latent_queries else n_batch * seq // tq
    per_tile = tq // WINDOW
    n_win_blocks = seq // WINDOW
    ctx_blk0 = n_batch * seq // ctx
    nqk = GQA_HEADS * GQA_DIM
    kw = 2 * GQA_KV_HEADS * GQA_DIM
    k_col, v_col = nqk // kw, nqk // kw + 1
    q_idx = lambda b, i: (q_blk0 + b * nq + i, 0)
    cidx = lambda col: (lambda b, i: (ctx_blk0 + b, col))
    if latent_queries:
        cur = lambda col: (lambda b, i: (b * nq + i, col))
        prev = lambda col: (lambda b, i: (b * n_win_blocks + jnp.maximum(per_tile * i - 1, 0), col))
        nxt = lambda col: (lambda b, i: (b * n_win_blocks + jnp.minimum(per_tile * (i + 1), n_win_blocks - 1), col))
        band = [((WINDOW, kw), prev), ((tq, kw), cur), ((WINDOW, kw), nxt)]
    else:
        band = [((ctx, kw), cidx)] * 3
    n_keys = ctx + (tq + 2 * WINDOW if latent_queries else 0)
    in_specs = ([pl.BlockSpec((tq, nqk), q_idx)]
                + [pl.BlockSpec(shape, idx(k_col)) for shape, idx in band]
                + [pl.BlockSpec(shape, idx(v_col)) for shape, idx in band]
                + [pl.BlockSpec((ctx, kw), cidx(k_col)), pl.BlockSpec((ctx, kw), cidx(v_col)),
                   pl.BlockSpec(sink_tab.shape, lambda b, i: (0, 0))])
    return pl.pallas_call(
        functools.partial(_win_kernel, seq=seq, with_lat=latent_queries),
        grid=(n_batch, nq),
        in_specs=in_specs,
        out_specs=pl.BlockSpec((tq, nqk), lambda b, i: (b * nq + i, 0)),
        out_shape=jax.ShapeDtypeStruct((n_batch * nq * tq, nqk), BF16),
        scratch_shapes=[pltpu.VMEM((2, tq, n_keys), F32), pltpu.VMEM((2, tq, n_keys), BF16)],
        compiler_params=_cparams(("arbitrary", "arbitrary")),
        name="win_attn" if latent_queries else "win_attn_ctx",
    )(*([gqa] * 9 + [sink_tab]))


def _window_attention(gqa, sink_tab, *, n_batch, seq, ctx, with_ctx_queries):
    dims = dict(n_batch=n_batch, seq=seq, ctx=ctx)
    lat = _win_call(gqa, sink_tab, latent_queries=True, **dims)
    return lat, (_win_call(gqa, sink_tab, latent_queries=False, **dims) if with_ctx_queries else lat)


def _ret_kernel(f_ref, b_ref, lg_ref, of_ref, ob_ref, sf_ref, sb_ref, qdec_ref, kdec_ref, cdec_ref, inner_ref):
    @pl.when(pl.program_id(1) == 0)
    def _():
        sf_ref[...] = jnp.zeros_like(sf_ref)
        sb_ref[...] = jnp.zeros_like(sb_ref)

    L = f_ref.shape[0]
    lo = _lane_lo((L, LANES))
    srow_lo = lax.broadcasted_iota(jnp.int32, (LANES, LANES), 0) < RET_QK
    nq = RET_HEADS * RET_QK
    n_pairs = RET_HEADS // 2

    @pl.when(pl.program_id(1) == 0)
    def _():
        ii = lax.broadcasted_iota(jnp.int32, (L, L), 0)
        jj = lax.broadcasted_iota(jnp.int32, (L, L), 1)
        row = lax.broadcasted_iota(jnp.int32, (L, LANES), 0).astype(F32)
        for direction, forward in enumerate((True, False)):
            dist = ii - jj if forward else jj - ii
            distf = jnp.maximum(dist, 0).astype(F32)
            for pr in range(n_pairs):
                r0 = direction * RET_HEADS + 2 * pr
                lg = [lg_ref[r0 + e:r0 + e + 1, :] for e in range(2)]
                lg_lane = jnp.where(lo, lg[0], lg[1])
                qdec_ref[direction * n_pairs + pr] = jnp.exp(lg_lane * ((row + 1.0) if forward else (L - row)))
                kdec_ref[direction * n_pairs + pr] = jnp.exp(lg_lane * ((L - 1.0 - row) if forward else row))
                cdec_ref[direction * n_pairs + pr] = jnp.where(srow_lo, jnp.exp(lg[0] * float(L)),
                                                               jnp.exp(lg[1] * float(L)))
                for e in range(2):
                    inner_ref[r0 + e] = jnp.where(dist >= 0, jnp.exp(lg[e][:, 0:1] * distf), 0.0)

    def scan_chunk(x_ref, o_ref, s_ref, direction):
        for pr in range(n_pairs):
            q = x_ref[:, pr * LANES:(pr + 1) * LANES].astype(F32)
            k = x_ref[:, nq + pr * LANES:nq + (pr + 1) * LANES].astype(F32)
            qd = q * qdec_ref[direction * n_pairs + pr]
            kdb = (k * kdec_ref[direction * n_pairs + pr]).astype(BF16)
            kb = k.astype(BF16)
            state = s_ref[pr]
            state_b = state.astype(BF16)
            upd = []
            for e in range(2):
                hd = 2 * pr + e
                keep = lo if e == 0 else jnp.logical_not(lo)
                v = x_ref[:, 2 * nq + hd * RET_V:2 * nq + (hd + 1) * RET_V].astype(BF16)
                attn = _dot_nt(jnp.where(keep, q, 0.0).astype(BF16), kb) * inner_ref[direction * RET_HEADS + hd]
                o = _dot(attn.astype(BF16), v) + _dot(jnp.where(keep, qd, 0.0).astype(BF16), state_b)
                o_ref[:, hd * RET_V:(hd + 1) * RET_V] = o
                upd.append(_dot_tn(kdb, v))
            s_ref[pr] = state * cdec_ref[direction * n_pairs + pr] + jnp.where(srow_lo, upd[0], upd[1])

    scan_chunk(f_ref, of_ref, sf_ref, 0)
    scan_chunk(b_ref, ob_ref, sb_ref, 1)


def _retention(ret, lg_tab, *, n_batch, seq, ctx):
    t = ret.shape[0]
    L = RET_TILE
    assert seq % L == 0 and ctx % L == 0
    n_lat, n_ctx = seq // L, ctx // L
    ctx0 = n_batch * n_lat
    width = 2 * RET_HEADS * RET_QK + RET_HEADS * RET_V
    fwd = lambda b, s: (jnp.where(s < n_ctx, ctx0 + b * n_ctx + s, b * n_lat + s - n_ctx), 0)
    bwd = lambda b, s: (jnp.where(s < n_ctx, ctx0 + b * n_ctx + n_ctx - 1 - s, b * n_lat + n_lat - 1 - (s - n_ctx)), 0)
    out = jax.ShapeDtypeStruct((t, RET_HEADS * RET_V), F32)
    return pl.pallas_call(
        _ret_kernel,
        grid=(n_batch, n_lat + n_ctx),
        in_specs=[pl.BlockSpec((L, width), fwd), pl.BlockSpec((L, width), bwd),
                  pl.BlockSpec(lg_tab.shape, lambda b, s: (0, 0))],
        out_specs=[pl.BlockSpec((L, RET_HEADS * RET_V), fwd), pl.BlockSpec((L, RET_HEADS * RET_V), bwd)],
        out_shape=[out, out],
        scratch_shapes=[pltpu.VMEM((RET_HEADS // 2, LANES, RET_V), F32),
                        pltpu.VMEM((RET_HEADS // 2, LANES, RET_V), F32),
                        pltpu.VMEM((RET_HEADS, L, LANES), F32), pltpu.VMEM((RET_HEADS, L, LANES), F32),
                        pltpu.VMEM((RET_HEADS, LANES, RET_V), F32), pltpu.VMEM((2 * RET_HEADS, L, L), F32)],
        compiler_params=_cparams(("arbitrary", "arbitrary")),
        name="retention",
    )(ret, ret, lg_tab)


def _merge_kernel(al_ref, ac_ref, wl_ref, wc_ref, of_ref, ob_ref, rg_ref, gt_ref, hl_ref, hc_ref, mod_ref,
                  gpost_ref, gffn_ref, wa_ref, ww_ref, wr_ref, wo_ref, h1_ref, v_ref, vp_ref,
                  *, tiles_per_batch, n_batch, d):
    i = pl.program_id(0)
    bi = jnp.minimum(i // tiles_per_batch, n_batch)
    latent = i < tiles_per_batch * n_batch
    a_tile = jnp.where(latent, al_ref[...], ac_ref[...])
    w_tile = jnp.where(latent, wl_ref[...], wc_ref[...])
    h_tile = jnp.where(latent, hl_ref[...], hc_ref[...])
    o = of_ref[...] + ob_ref[...]
    normed = []
    for hd in range(RET_HEADS):
        oh = o[:, hd * RET_V:(hd + 1) * RET_V]
        dev = oh - jnp.mean(oh, axis=-1, keepdims=True)
        normed.append(dev * lax.rsqrt(jnp.mean(dev * dev, axis=-1, keepdims=True) + NORM_EPS))
    g = rg_ref[...].astype(F32)
    r = (g * _sigmoid(g)) * jnp.concatenate(normed, axis=1)
    y = (_sigmoid(gt_ref[:, 0:d].astype(F32)) * _dot(a_tile, wa_ref[...])
         + _sigmoid(gt_ref[:, d:2 * d].astype(F32)) * _dot(w_tile, ww_ref[...])
         + _sigmoid(gt_ref[:, 2 * d:3 * d].astype(F32)) * _dot(r.astype(BF16), wr_ref[...]))
    z = _dot(y.astype(BF16), wo_ref[...])
    g1 = mod_ref[pl.ds(bi, 1), 2 * d:3 * d]
    sh2 = mod_ref[pl.ds(bi, 1), 3 * d:4 * d]
    sc2 = mod_ref[pl.ds(bi, 1), 4 * d:5 * d]
    h1 = h_tile + g1 * _rms(z, gpost_ref[...])
    h1_ref[...] = h1
    v = _rms(h1, gffn_ref[...]) * (1.0 + sc2) + sh2
    v_ref[...] = v.astype(v_ref.dtype)
    vp_ref[...] = _pack_halves(v)


def _merge(a, w, o_f, o_b, ret, gates, h, mods, gpost, gffn, wa, ww, wr, wo, *, n_rows, n_batch, seq):
    d = h[0].shape[1]
    n_lat_tiles = n_batch * seq // TM
    rows = lambda i: (i, 0)
    lat_rows = lambda i: (jnp.minimum(i, n_lat_tiles - 1), 0)
    ctx_rows = lambda i: (jnp.maximum(i - n_lat_tiles, 0), 0)
    const = lambda i: (0, 0)
    rv = RET_HEADS * RET_V
    rg_col = (2 * RET_HEADS * RET_QK + rv) // rv
    outs = [jax.ShapeDtypeStruct((n_rows, d), F32), jax.ShapeDtypeStruct((n_rows, d), BF16),
            jax.ShapeDtypeStruct((n_rows, d // 2), jnp.uint32)]
    return pl.pallas_call(
        functools.partial(_merge_kernel, tiles_per_batch=seq // TM, n_batch=n_batch, d=d),
        grid=(n_rows // TM,),
        in_specs=[pl.BlockSpec((TM, a[0].shape[1]), lat_rows), pl.BlockSpec((TM, a[1].shape[1]), ctx_rows),
                  pl.BlockSpec((TM, w[0].shape[1]), lat_rows), pl.BlockSpec((TM, w[1].shape[1]), ctx_rows),
                  pl.BlockSpec((TM, rv), rows), pl.BlockSpec((TM, rv), rows),
                  pl.BlockSpec((TM, rv), lambda i: (i, rg_col)),
                  pl.BlockSpec((TM, 3 * d), rows), pl.BlockSpec((TM, d), lat_rows),
                  pl.BlockSpec((TM, d), lambda i: (h[2] + jnp.maximum(i - n_lat_tiles, 0), 0)),
                  pl.BlockSpec(mods.shape, const), pl.BlockSpec((1, d), const), pl.BlockSpec((1, d), const),
                  pl.BlockSpec(wa.shape, const), pl.BlockSpec(ww.shape, const),
                  pl.BlockSpec(wr.shape, const), pl.BlockSpec(wo.shape, const)],
        out_specs=[pl.BlockSpec((TM, o.shape[1]), rows) for o in outs],
        out_shape=outs,
        compiler_params=_cparams(("arbitrary",)),
        name="merge",
    )(a[0], a[1], w[0], w[1], o_f, o_b, ret, gates, h[0], h[1], mods, gpost, gffn, wa, ww, wr, wo)


def _router_kernel(v_ref, rw_ref, rb_ref, eidx_ref, rank_ref, w_ref, cnt_ref, carry_ref):
    @pl.when(pl.program_id(0) == 0)
    def _():
        carry_ref[...] = jnp.zeros_like(carry_ref)

    tm = v_ref.shape[0]
    scores = _sigmoid(_dot_nt(rw_ref[...], v_ref[...]))
    sel = scores + rb_ref[...]
    neg = -jnp.inf
    n_grp, per = N_EXPERT_GROUPS, EXPERTS_PER_GROUP

    sel3 = sel.reshape(n_grp, per, tm)
    member_id = lax.broadcasted_iota(jnp.int32, sel3.shape, 1)
    m1 = jnp.max(sel3, axis=1, keepdims=True)
    i1 = jnp.min(jnp.where(sel3 == m1, member_id, per), axis=1, keepdims=True)
    m2 = jnp.max(jnp.where(member_id == i1, neg, sel3), axis=1, keepdims=True)
    gscore = (m1 + m2).reshape(n_grp, tm)
    gid = lax.broadcasted_iota(jnp.int32, gscore.shape, 0)
    ahead = jnp.zeros(gscore.shape, jnp.int32)
    for gj in range(n_grp):
        other = gscore[gj:gj + 1, :]
        ahead = ahead + jnp.where((other > gscore) | ((other == gscore) & (gid > gj)), 1, 0)
    group_ok = (ahead < TOPK_GROUPS).reshape(n_grp, 1, tm)
    sel = jnp.where(group_ok, sel3, NEG_INF).reshape(N_EXPERTS, tm)

    eid = lax.broadcasted_iota(jnp.int32, sel.shape, 0)
    chosen = jnp.zeros(sel.shape, jnp.bool_)
    picks = []
    for _ in range(TOP_K):
        m = jnp.max(sel, axis=0, keepdims=True)
        idx = jnp.min(jnp.where(sel == m, eid, N_EXPERTS), axis=0, keepdims=True)
        hit = eid == idx
        chosen = chosen | hit
        sel = jnp.where(hit, neg, sel)
        picks.append(idx)
    w = jnp.where(chosen, scores, 0.0)
    gate = ROUTED_SCALE * w / jnp.sum(w, axis=0, keepdims=True)

    member = jnp.where(chosen, 1.0, 0.0)
    earlier = lax.broadcasted_iota(jnp.int32, (tm, tm), 0) < lax.broadcasted_iota(jnp.int32, (tm, tm), 1)
    pos = _dot(member.astype(BF16), jnp.where(earlier, 1.0, 0.0).astype(BF16)) + carry_ref[...]
    for k, idx in enumerate(picks):
        hit = eid == idx
        eidx_ref[k:k + 1, :] = idx
        rank_ref[k:k + 1, :] = jnp.sum(jnp.where(hit, pos, 0.0), axis=0, keepdims=True)
        w_ref[k:k + 1, :] = jnp.sum(jnp.where(hit, gate, 0.0), axis=0, keepdims=True)
    carry_ref[...] += jnp.sum(member, axis=1, keepdims=True)
    cnt_ref[...] = carry_ref[...]


def _router(v, rw_t, rb):
    n_rows, d = v.shape
    cols = lambda i: (0, i)
    const = lambda i: (0, 0)
    outs = [jax.ShapeDtypeStruct((TOP_K, n_rows), jnp.int32), jax.ShapeDtypeStruct((TOP_K, n_rows), F32),
            jax.ShapeDtypeStruct((TOP_K, n_rows), F32), jax.ShapeDtypeStruct((N_EXPERTS, 1), F32)]
    return pl.pallas_call(
        _router_kernel,
        grid=(n_rows // TM,),
        in_specs=[pl.BlockSpec((TM, d), lambda i: (i, 0)), pl.BlockSpec(rw_t.shape, const),
                  pl.BlockSpec(rb.shape, const)],
        out_specs=[pl.BlockSpec((TOP_K, TM), cols), pl.BlockSpec((TOP_K, TM), cols),
                   pl.BlockSpec((TOP_K, TM), cols), pl.BlockSpec((N_EXPERTS, 1), const)],
        out_shape=outs,
        scratch_shapes=[pltpu.VMEM((N_EXPERTS, 1), F32)],
        compiler_params=_cparams(("arbitrary",)),
        name="router",
    )(v, rw_t, rb)


def _slots_kernel(eidx_ref, rank_ref, cnt_ref, slot_ref):
    tm = eidx_ref.shape[1]
    eid = lax.broadcasted_iota(jnp.int32, (N_EXPERTS, tm), 0)
    for k in range(TOP_K):
        before = jnp.sum(jnp.where(eid < eidx_ref[k:k + 1, :], cnt_ref[...], 0.0), axis=0, keepdims=True)
        slot_ref[k:k + 1, :] = (before + rank_ref[k:k + 1, :]).astype(jnp.int32)


def _slots(eidx, rank, cnt):
    n_rows = eidx.shape[1]
    tm = next(c for c in (2048, 1024, 512, 256) if n_rows % c == 0)
    cols = lambda i: (0, i)
    return pl.pallas_call(
        _slots_kernel,
        grid=(n_rows // tm,),
        in_specs=[pl.BlockSpec((TOP_K, tm), cols), pl.BlockSpec((TOP_K, tm), cols),
                  pl.BlockSpec(cnt.shape, lambda i: (0, 0))],
        out_specs=pl.BlockSpec((TOP_K, tm), cols),
        out_shape=jax.ShapeDtypeStruct((TOP_K, n_rows), jnp.int32),
        compiler_params=_cparams(("arbitrary",)),
        name="slots",
    )(eidx, rank, cnt)


SC_WINDOW = 128


def _sc_mesh():
    return plsc.VectorSubcoreMesh(core_axis_name="core", subcore_axis_name="subcore")


def _sc_dispatch(rows, slot_t, n_out):
    width = rows.shape[1]
    n_chunks = slot_t.shape[1] // SC_WINDOW
    info = plsc.get_sparse_core_info()
    n_workers = info.num_cores * info.num_subcores

    @functools.partial(
        pl.kernel, mesh=_sc_mesh(),
        out_type=jax.ShapeDtypeStruct((n_out, width), rows.dtype),
        scratch_types=[pltpu.VMEM((TOP_K, SC_WINDOW), jnp.int32), pltpu.VMEM((SC_WINDOW, width), rows.dtype)],
        name="moe_dispatch")
    def run(rows_hbm, idx_hbm, out_hbm, idx_v, rows_v):
        wid = lax.axis_index("subcore") * info.num_cores + lax.axis_index("core")

        @pl.loop(wid, n_chunks, step=n_workers)
        def _(c):
            r0 = pl.multiple_of(c * SC_WINDOW, SC_WINDOW)
            pltpu.sync_copy(idx_hbm.at[:, pl.ds(r0, SC_WINDOW)], idx_v)
            pltpu.sync_copy(rows_hbm.at[pl.ds(r0, SC_WINDOW)], rows_v)
            for k in range(TOP_K):
                pltpu.sync_copy(rows_v, out_hbm.at[idx_v.at[k]])

    return run(rows, slot_t)


def _sc_collect(rows, slot_t):
    n_picks, n_rows = slot_t.shape
    width = rows.shape[1]
    n_chunks = n_rows // SC_WINDOW
    info = plsc.get_sparse_core_info()
    n_workers = info.num_cores * info.num_subcores

    @functools.partial(
        pl.kernel, mesh=_sc_mesh(),
        out_type=jax.ShapeDtypeStruct((n_picks, n_rows, width), rows.dtype),
        scratch_types=[pltpu.VMEM((TOP_K, SC_WINDOW), jnp.int32), pltpu.VMEM((SC_WINDOW, width), rows.dtype)],
        name="moe_collect")
    def run(rows_hbm, idx_hbm, out_hbm, idx_v, rows_v):
        wid = lax.axis_index("subcore") * info.num_cores + lax.axis_index("core")

        @pl.loop(wid, n_chunks, step=n_workers)
        def _(c):
            r0 = pl.multiple_of(c * SC_WINDOW, SC_WINDOW)
            pltpu.sync_copy(idx_hbm.at[:, pl.ds(r0, SC_WINDOW)], idx_v)
            for k in range(TOP_K):
                pltpu.sync_copy(rows_hbm.at[idx_v.at[k]], rows_v)
                pltpu.sync_copy(rows_v, out_hbm.at[k, pl.ds(r0, SC_WINDOW)])

    return run(rows, slot_t)


EXPERT_TILE = 512


def _work_items(cnt, n_slots):
    counts = cnt[:, 0].astype(jnp.int32)
    ends = jnp.cumsum(counts)
    n_tiles = n_slots // EXPERT_TILE
    bounds = jnp.sort(jnp.concatenate([jnp.arange(n_tiles, dtype=jnp.int32) * EXPERT_TILE, ends - counts]))
    nxt = jnp.concatenate([bounds[1:], jnp.array([n_slots], jnp.int32)])
    tile = jnp.minimum(bounds // EXPERT_TILE, n_tiles - 1)
    expert = jnp.sum((ends[None, :] <= bounds[:, None]).astype(jnp.int32), axis=1)
    expert = jnp.minimum(expert, N_EXPERTS - 1)
    return tile, expert, bounds - tile * EXPERT_TILE, nxt - tile * EXPERT_TILE


XS_RING = 3


def _experts_kernel(tile_ref, exp_ref, lo_ref, hi_ref, xs_hbm, wg_ref, wu_ref, wd_ref, ys_ref,
                    acc_ref, wgb_ref, wub_ref, wdb_ref, xbuf_ref, xsem):
    i = pl.program_id(0)
    n_items = pl.num_programs(0)
    lo, hi = lo_ref[i], hi_ref[i]

    def tile_copy(item):
        slot = lax.rem(item, XS_RING)
        row0 = pl.multiple_of(tile_ref[item] * EXPERT_TILE, EXPERT_TILE)
        return pltpu.make_async_copy(xs_hbm.at[pl.ds(row0, EXPERT_TILE)], xbuf_ref.at[slot], xsem.at[slot])

    @pl.when(i == 0)
    def _():
        for ahead in range(XS_RING - 1):
            pl.when(ahead < n_items)(lambda: tile_copy(ahead).start())

    @pl.when(i + XS_RING - 1 < n_items)
    def _():
        tile_copy(i + XS_RING - 1).start()

    tile_copy(i).wait()
    xs_ref = xbuf_ref.at[lax.rem(i, XS_RING)]

    @pl.when((i == 0) | (exp_ref[i] != exp_ref[jnp.maximum(i - 1, 0)]))
    def _():
        wgb_ref[...] = wg_ref[...].astype(BF16)
        wub_ref[...] = wu_ref[...].astype(BF16)
        wdb_ref[...] = wd_ref[...].astype(BF16)

    def ffn():
        x_lo, x_hi = _unpack_halves(xs_ref[...])
        x_lo, x_hi = x_lo.astype(BF16), x_hi.astype(BF16)
        n = x_lo.shape[1]
        a = _dot(x_lo, wgb_ref[0:n, :]) + _dot(x_hi, wgb_ref[n:, :])
        u = _dot(x_lo, wub_ref[0:n, :]) + _dot(x_hi, wub_ref[n:, :])
        return _dot(((a * _sigmoid(a)) * u).astype(BF16), wdb_ref[...])

    whole = (lo == 0) & (hi == EXPERT_TILE)

    @pl.when(whole)
    def _():
        ys_ref[...] = _pack_halves(ffn())

    @pl.when(jnp.logical_not(whole) & (hi > lo))
    def _():
        y = ffn()
        row = lax.broadcasted_iota(jnp.int32, y.shape, 0)
        y = jnp.where((row >= lo) & (row < hi), y, 0.0)

        @pl.when(lo == 0)
        def _():
            acc_ref[...] = y

        @pl.when((lo > 0) & (hi < EXPERT_TILE))
        def _():
            acc_ref[...] += y

        @pl.when((lo > 0) & (hi == EXPERT_TILE))
        def _():
            ys_ref[...] = _pack_halves(acc_ref[...] + y)


def _experts(xs, items, layer, exp_wg, exp_wu, exp_wd):
    n_slots, half = xs.shape
    d, hid = exp_wg.shape[-2:]
    tile, expert, lo, hi = items
    grid_spec = pltpu.PrefetchScalarGridSpec(
        num_scalar_prefetch=4,
        grid=(tile.shape[0],),
        in_specs=[pl.BlockSpec(memory_space=pl.ANY),
                  pl.BlockSpec((None, None, d, hid), lambda i, t, e, lo, hi: (layer, e[i], 0, 0)),
                  pl.BlockSpec((None, None, d, hid), lambda i, t, e, lo, hi: (layer, e[i], 0, 0)),
                  pl.BlockSpec((None, None, hid, d), lambda i, t, e, lo, hi: (layer, e[i], 0, 0))],
        out_specs=pl.BlockSpec((EXPERT_TILE, half), lambda i, t, e, lo, hi: (t[i], 0)),
        scratch_shapes=[pltpu.VMEM((EXPERT_TILE, d), F32), pltpu.VMEM((d, hid), BF16),
                        pltpu.VMEM((d, hid), BF16), pltpu.VMEM((hid, d), BF16),
                        pltpu.VMEM((XS_RING, EXPERT_TILE, half), jnp.uint32),
                        pltpu.SemaphoreType.DMA((XS_RING,))])
    return pl.pallas_call(
        _experts_kernel,
        grid_spec=grid_spec,
        out_shape=jax.ShapeDtypeStruct((n_slots, half), jnp.uint32),
        compiler_params=_cparams(("arbitrary",)),
        name="experts",
    )(tile, expert, lo, hi, xs, exp_wg, exp_wu, exp_wd)


def _moe_out_tile(yg_ref, w_ref, v_ref, sg_ref, su_ref, sd_ref, h1_ref, mod_ref, gpost_ref,
                  *, tiles_per_batch, n_batch, d):
    i = pl.program_id(0)
    x = v_ref[...]
    a = _dot(x, sg_ref[...])
    f = _dot(((a * _sigmoid(a)) * _dot(x, su_ref[...])).astype(BF16), sd_ref[...])
    n = d // 2
    f_lo, f_hi = f[:, :n], f[:, n:]
    w = w_ref[...]
    for k in range(TOP_K):
        y_lo, y_hi = _unpack_halves(yg_ref[k])
        wk = w[:, k:k + 1]
        f_lo = f_lo + wk * y_lo
        f_hi = f_hi + wk * y_hi
    f = jnp.concatenate([f_lo, f_hi], axis=1)
    bi = jnp.minimum(i // tiles_per_batch, n_batch)
    g2 = mod_ref[pl.ds(bi, 1), 5 * d:6 * d]
    return h1_ref[...] + g2 * _rms(f, gpost_ref[...])


N_MOE_OUT_IN = 9
N_INPROJ_IN = 9


def _moe_out_kernel(*refs, **kw):
    refs[N_MOE_OUT_IN][...] = _moe_out_tile(*refs[:N_MOE_OUT_IN], **kw)


def _moe_out_inproj_kernel(*refs, n_batch, d, tiles_per_batch):
    dims = dict(tiles_per_batch=tiles_per_batch, n_batch=n_batch, d=d)
    h = _moe_out_tile(*refs[:N_MOE_OUT_IN], **dims)
    n_in = N_MOE_OUT_IN + N_INPROJ_IN
    refs[n_in][...] = h
    _inproj_tile(h, *refs[N_MOE_OUT_IN:n_in], *refs[n_in + 1:], **dims)


def _moe_out_specs(yg, w, v, sg, su, sd, h1, mods, gpost, tm):
    d = v.shape[1]
    rows = lambda i: (i, 0)
    const = lambda i: (0, 0)
    once = dict(pipeline_mode=pl.Buffered(1))
    return [pl.BlockSpec((TOP_K, tm, d // 2), lambda i: (0, i, 0)), pl.BlockSpec((tm, TOP_K), rows),
            pl.BlockSpec((tm, d), rows),
            pl.BlockSpec(sg.shape, const, **once), pl.BlockSpec(su.shape, const, **once),
            pl.BlockSpec(sd.shape, const, **once),
            pl.BlockSpec((tm, d), rows), pl.BlockSpec(mods.shape, const), pl.BlockSpec((1, d), const)]


def _moe_out(moe_args, *, n_batch, seq):
    v = moe_args[2]
    n_rows, d = v.shape
    return pl.pallas_call(
        functools.partial(_moe_out_kernel, tiles_per_batch=seq // TM, n_batch=n_batch, d=d),
        grid=(n_rows // TM,),
        in_specs=_moe_out_specs(*moe_args, TM),
        out_specs=pl.BlockSpec((TM, d), lambda i: (i, 0)),
        out_shape=jax.ShapeDtypeStruct((n_rows, d), F32),
        compiler_params=_cparams(("arbitrary",)),
        name="moe_out",
    )(*moe_args)


FUSED_TM = 256


def _moe_out_inproj(moe_args, inproj_args, *, layer, n_batch, seq):
    v = moe_args[2]
    n_rows, d = v.shape
    tm = FUSED_TM
    in_specs, out_specs, out_shape = _inproj_specs(n_rows, d, *inproj_args, layer=layer, tm=tm, n_batch=n_batch,
                                                   seq=seq)
    rows = lambda i: (i, 0)
    return pl.pallas_call(
        functools.partial(_moe_out_inproj_kernel, tiles_per_batch=seq // tm, n_batch=n_batch, d=d),
        grid=(n_rows // tm,),
        in_specs=_moe_out_specs(*moe_args, tm) + in_specs,
        out_specs=[pl.BlockSpec((tm, d), rows)] + out_specs,
        out_shape=[jax.ShapeDtypeStruct((n_rows, d), F32)] + out_shape,
        compiler_params=_cparams(("arbitrary",)),
        name="moe_out_inproj",
    )(*moe_args, *inproj_args)


def _moe_routed(v, vp, layer, rw, rb, exp_wg, exp_wu, exp_wd):
    n_rows = v.shape[0]
    n_slots = n_rows * TOP_K
    assert n_slots % EXPERT_TILE == 0
    eidx, rank, w_t, cnt = _router(v, rw, rb)
    slot_t = _slots(eidx, rank, cnt)
    xs = _sc_dispatch(vp, slot_t, n_slots)
    ys = _experts(xs, _work_items(cnt, n_slots), layer, exp_wg, exp_wu, exp_wd)
    return _sc_collect(ys, slot_t), w_t.T


def _rope_tables(seq):
    rows = seq // GRID_W
    row_id = np.repeat(np.arange(rows, dtype=np.float64), GRID_W)
    col_id = np.tile(np.arange(GRID_W, dtype=np.float64), rows)

    def tables(rot_dim):
        axis_dim = rot_dim // 2
        inv_freq = ROPE_BASE ** (-np.arange(0, axis_dim, 2, dtype=np.float64) / axis_dim)
        ang_r = row_id[:, None] * inv_freq[None, :]
        ang_c = col_id[:, None] * inv_freq[None, :]
        cos = np.concatenate([np.cos(ang_r), np.cos(ang_r), np.cos(ang_c), np.cos(ang_c)], axis=1)
        sin = np.concatenate([-np.sin(ang_r), np.sin(ang_r), -np.sin(ang_c), np.sin(ang_c)], axis=1)
        return cos, sin

    cos64, sin64 = tables(GQA_DIM)
    cos32, sin32 = tables(MLA_ROPE)
    ones = np.ones((seq, MLA_NOPE))
    pad = LANES - MLA_NOPE - MLA_ROPE
    cospe = np.concatenate([ones, cos32, np.ones((seq, pad))], axis=1)
    sinpe = np.concatenate([0 * ones, sin32, np.zeros((seq, pad))], axis=1)
    tab = np.concatenate([cos64, cos64, sin64, sin64, cospe, sinpe], axis=1)
    ident = np.concatenate([np.ones((TM, LANES)), np.zeros((TM, LANES)),
                            np.ones((TM, LANES)), np.zeros((TM, LANES))], axis=1)
    return jnp.asarray(np.concatenate([tab, ident], axis=0), F32)


def _w_in_moves(d):
    sizes = (MLA_Q_LORA, MLA_KV_LORA, MLA_ROPE, GQA_HEADS * GQA_DIM, GQA_KV_HEADS * GQA_DIM,
             GQA_KV_HEADS * GQA_DIM, RET_HEADS * RET_QK, RET_HEADS * RET_QK, RET_HEADS * RET_V,
             RET_HEADS * RET_V, 3 * d)
    src = [sum(sizes[:i]) for i in range(len(sizes))]
    cq, ckv, kpe, gq, gk, gv, rq, rk, rv, rg, gates = src
    moves = [(cq, C_CQ, MLA_Q_LORA, 1.0), (ckv, C_CKV, MLA_KV_LORA, 1.0), (kpe, C_KPE + MLA_NOPE, MLA_ROPE, 1.0),
             (gq, C_G, GQA_HEADS * GQA_DIM, GQA_DIM ** -0.5 * LOG2_E)]
    dst = C_G + GQA_HEADS * GQA_DIM
    for base in (gk, gv):
        for hd in range(GQA_KV_HEADS):
            for _ in range(2):
                moves.append((base + hd * GQA_DIM, dst, GQA_DIM, 1.0))
                dst += GQA_DIM
    assert dst == C_R
    for s, width, scale in ((rq, RET_HEADS * RET_QK, 1.0), (rk, RET_HEADS * RET_QK, RET_QK ** -0.5),
                            (rv, RET_HEADS * RET_V, 1.0), (rg, RET_HEADS * RET_V, 1.0), (gates, 3 * d, 1.0)):
        moves.append((s, dst, width, scale))
        dst += width
    assert dst == W_COLS
    return moves


def _pack_kernel(w_ref, o_ref, *, moves):
    o_ref[:, C_KPE:C_G] = jnp.zeros((o_ref.shape[0], C_G - C_KPE), o_ref.dtype)
    for s, t, width, scale in moves:
        piece = w_ref[:, s:s + width]
        o_ref[:, t:t + width] = (piece if scale == 1.0 else piece * scale).astype(o_ref.dtype)


def _pack_w_in(w_in):
    n_layers, d, n_cols = w_in.shape
    rows = 256
    return pl.pallas_call(
        functools.partial(_pack_kernel, moves=_w_in_moves(d)),
        grid=(n_layers, d // rows),
        in_specs=[pl.BlockSpec((None, rows, n_cols), lambda l, i: (l, i, 0))],
        out_specs=pl.BlockSpec((None, rows, W_COLS), lambda l, i: (l, i, 0)),
        out_shape=jax.ShapeDtypeStruct((n_layers, d, W_COLS), BF16),
        compiler_params=_cparams(("arbitrary", "arbitrary")),
        name="pack_w_in",
    )(w_in)


def _pack_mla_up(w_uq, w_ukv):
    r = w_uq.shape[0]
    dq = MLA_NOPE + MLA_ROPE
    wq = jnp.pad(w_uq.reshape(r, MLA_HEADS, dq), ((0, 0), (0, 0), (0, LANES - dq))).reshape(r, MLA_HEADS * LANES)
    kv = w_ukv.reshape(r, MLA_HEADS, MLA_NOPE + MLA_V)
    wk = jnp.pad(kv[:, :, :MLA_NOPE], ((0, 0), (0, 0), (0, LANES - MLA_NOPE))).reshape(r, MLA_HEADS * LANES)
    wv = kv[:, :, MLA_NOPE:]
    zeros = jnp.zeros_like(wv)
    even = jnp.concatenate([wv, zeros], axis=2)
    odd = jnp.concatenate([zeros, wv], axis=2)
    wv = jnp.where((jnp.arange(MLA_HEADS) % 2 == 0)[None, :, None], even, odd).reshape(r, MLA_HEADS * LANES)
    return wq.astype(BF16), wk.astype(BF16), wv.astype(BF16)


def kernel(x, c, ctx, c_ctx, ada_w, ada_b, norm_mix_pre, norm_mix_post, norm_ffn_pre, norm_ffn_post, w_in, mla_q_norm, mla_w_uq, mla_kv_norm, mla_w_ukv, gqa_sink, ret_decay_fwd, ret_decay_bwd, w_br_mla, w_br_gqa, w_br_ret, w_out, router_w, router_bias, exp_w_gate, exp_w_up, exp_w_down, shared_w_gate, shared_w_up, shared_w_down):
    n_batch, seq, d = x.shape
    n_ctx = ctx.shape[1]
    depth = ada_w.shape[0]
    n_lat_rows = n_batch * seq
    assert seq % TM == 0 and (n_batch * n_ctx) % TM == 0 and seq % ATT_TQ == 0 and seq % n_ctx == 0
    assert n_batch < MOD_ROWS and seq % GRID_W == 0 and d == D_MODEL

    cond = jnp.zeros((MOD_ROWS, d), F32).at[:n_batch].set(c).at[n_batch].set(c_ctx)
    mods_all = _adaln(cond, ada_w, ada_b)
    rope = _rope_tables(seq)
    h = (x.reshape(n_lat_rows, d), ctx.reshape(n_batch * n_ctx, d), 0)
    n_all_rows = n_lat_rows + n_batch * n_ctx
    row = lambda p: p.reshape(1, -1)
    dims = dict(n_batch=n_batch, seq=seq)

    w_in_packed = _pack_w_in(w_in)

    def inproj_args(l):
        return (mods_all[l], row(norm_mix_pre[l]), w_in_packed, rope, row(mla_q_norm[l]),
                row(mla_kv_norm[l]), *_pack_mla_up(mla_w_uq[l], mla_w_ukv[l]))

    projected = _inproj(h[0], h[1], inproj_args(0), layer=0, **dims)
    for l in range(depth):
        last = l == depth - 1
        mods = mods_all[l]
        mq, mk, mv, gqa, ret, gates = projected
        a = _mla_attention(mq, mk, mv, ctx=n_ctx, with_ctx_queries=not last, **dims)
        sink_tab = jnp.broadcast_to(gqa_sink[l].astype(F32)[:, None] * LOG2_E, (GQA_HEADS, LANES))
        w = _window_attention(gqa, sink_tab, ctx=n_ctx, with_ctx_queries=not last, **dims)
        lg = jnp.concatenate([jax.nn.log_sigmoid(ret_decay_fwd[l].astype(F32)),
                              jax.nn.log_sigmoid(ret_decay_bwd[l].astype(F32))])
        o_f, o_b = _retention(ret, jnp.broadcast_to(lg[:, None], (2 * RET_HEADS, LANES)), ctx=n_ctx, **dims)
        n_rows = n_lat_rows if last else n_all_rows
        h1, v, vp = _merge(a, w, o_f, o_b, ret, gates, h, mods, row(norm_mix_post[l]), row(norm_ffn_pre[l]),
                           w_br_mla[l].astype(BF16), w_br_gqa[l].astype(BF16), w_br_ret[l].astype(BF16),
                           w_out[l].astype(BF16), n_rows=n_rows, **dims)
        yg, gate_w = _moe_routed(v, vp, l, router_w[l].T.astype(BF16), router_bias[l].astype(F32).reshape(-1, 1),
                                 exp_w_gate, exp_w_up, exp_w_down)
        moe_args = (yg, gate_w, v, shared_w_gate[l].astype(BF16), shared_w_up[l].astype(BF16),
                    shared_w_down[l].astype(BF16), h1, mods, row(norm_ffn_post[l]))
        if last:
            out = _moe_out(moe_args, **dims)
        else:
            stream, *projected = _moe_out_inproj(moe_args, inproj_args(l + 1), layer=l + 1, **dims)
            h = (stream, stream, n_lat_rows // TM)
    return out[:n_lat_rows].reshape(n_batch, seq, d)
```

```python
import functools

import numpy as np
import jax
import jax.numpy as jnp
from jax import lax
from jax.experimental import pallas as pl
from jax.experimental.pallas import tpu as pltpu
from jax.experimental.pallas import tpu_sc as plsc

F32 = jnp.float32
BF16 = jnp.bfloat16

GRID_W = 64
ROPE_BASE = 10000.0
NORM_EPS = 1e-6
NEG_INF = -1e30
LOG2_E = 1.4426950408889634
N_MOD = 6
MLA_HEADS, MLA_NOPE, MLA_ROPE, MLA_V = 8, 64, 32, 64
MLA_Q_LORA, MLA_KV_LORA = 256, 256
GQA_HEADS, GQA_KV_HEADS, GQA_DIM, WINDOW = 8, 2, 64, 128
RET_HEADS, RET_QK, RET_V, RET_CHUNK = 4, 64, 128, 128
N_EXPERTS, N_EXPERT_GROUPS, TOPK_GROUPS, TOP_K = 64, 8, 4, 8
EXPERTS_PER_GROUP = N_EXPERTS // N_EXPERT_GROUPS
ROUTED_SCALE = 2.5

LANES = 128
TM = 512
ATT_TQ = 512
WIN_TQ = 256
RET_TILE = 256
MOD_ROWS = 8
V7X_VMEM_LIMIT = 56 * 1024 * 1024

D_MODEL = 1024
C_CQ = 0
C_CKV = C_CQ + MLA_Q_LORA
C_KPE = C_CKV + MLA_KV_LORA
C_G = C_KPE + LANES
C_R = C_G + GQA_HEADS * GQA_DIM + 4 * GQA_KV_HEADS * GQA_DIM
C_GATE = C_R + 2 * RET_HEADS * RET_QK + 2 * RET_HEADS * RET_V
W_COLS = C_GATE + 3 * D_MODEL


def _cparams(sem):
    return pltpu.CompilerParams(dimension_semantics=sem, vmem_limit_bytes=V7X_VMEM_LIMIT)


def _rms(x, g):
    return x * lax.rsqrt(jnp.mean(x * x, axis=-1, keepdims=True) + NORM_EPS) * g


def _sigmoid(x):
    return 0.5 * jnp.tanh(0.5 * x) + 0.5


def _dot(a, b):
    return jnp.dot(a, b, preferred_element_type=F32)


def _dot_nt(a, b):
    return lax.dot_general(a, b, (((1,), (1,)), ((), ())), preferred_element_type=F32)


def _dot_tn(a, b):
    return lax.dot_general(a, b, (((0,), (0,)), ((), ())), preferred_element_type=F32)


def _rope(x, cos, sin, half):
    n = x.shape[-1]
    reps = n // LANES
    if reps > 1:
        cos = jnp.concatenate([cos] * reps, axis=1)
        sin = jnp.concatenate([sin] * reps, axis=1)
    lane = lax.broadcasted_iota(jnp.int32, x.shape, 1)
    up = pltpu.roll(x, half, 1)
    dn = pltpu.roll(x, n - half, 1)
    partner = jnp.where((lane & (2 * half - 1)) < half, dn, up)
    return x * cos + partner * sin


def _lane_lo(shape):
    return (lax.broadcasted_iota(jnp.int32, shape, 1) & (LANES - 1)) < (LANES // 2)


def _pack_halves(x):
    n = x.shape[1] // 2
    bits = lambda t: lax.bitcast_convert_type(t.astype(BF16).astype(F32), jnp.uint32)
    return (bits(x[:, :n]) >> 16) | bits(x[:, n:])


def _unpack_halves(p):
    lo = lax.bitcast_convert_type(p << 16, F32)
    hi = lax.bitcast_convert_type(p & jnp.uint32(0xFFFF0000), F32)
    return lo, hi


def _ada_kernel(c_ref, w_ref, b_ref, o_ref):
    c = c_ref[...]
    s = c * _sigmoid(c)
    o_ref[...] = _dot(s.astype(BF16), w_ref[...].astype(BF16)) + b_ref[...]


def _adaln(cond, ada_w, ada_b):
    n_layers, d, n = ada_w.shape
    tn = 1024
    return pl.pallas_call(
        _ada_kernel,
        grid=(n_layers, n // tn),
        in_specs=[pl.BlockSpec((MOD_ROWS, d), lambda l, j: (0, 0)),
                  pl.BlockSpec((None, d, tn), lambda l, j: (l, 0, j)),
                  pl.BlockSpec((None, 1, tn), lambda l, j: (l, 0, j))],
        out_specs=pl.BlockSpec((None, MOD_ROWS, tn), lambda l, j: (l, 0, j)),
        out_shape=jax.ShapeDtypeStruct((n_layers, MOD_ROWS, n), F32),
        compiler_params=_cparams(("arbitrary", "arbitrary")),
        name="adaln",
    )(cond, ada_w, ada_b.reshape(n_layers, 1, n))


def _inproj_tile(h, mod_ref, gpre_ref, w_ref, rope_ref, qn_ref, kvn_ref, wuq_ref, wuk_ref, wuv_ref,
                 mq_ref, mk_ref, mv_ref, gqa_ref, ret_ref, gate_ref, *, tiles_per_batch, n_batch, d):
    i = pl.program_id(0)
    bi = jnp.minimum(i // tiles_per_batch, n_batch)
    sh = mod_ref[pl.ds(bi, 1), 0:d]
    sc = mod_ref[pl.ds(bi, 1), d:2 * d]
    u = (_rms(h, gpre_ref[...]) * (1.0 + sc) + sh).astype(BF16)

    cos64 = rope_ref[:, 0:LANES]
    sin64 = rope_ref[:, LANES:2 * LANES]
    cospe = rope_ref[:, 2 * LANES:3 * LANES]
    sinpe = rope_ref[:, 3 * LANES:4 * LANES]

    c = _dot(u, w_ref[:, C_CQ:C_G])
    kpe = _rope(c[:, C_KPE:C_G], cospe, sinpe, MLA_ROPE // 4)
    qn = _rms(c[:, C_CQ:C_CKV], qn_ref[...]).astype(BF16)
    q = _rope(_dot(qn, wuq_ref[...]), cospe, sinpe, MLA_ROPE // 4)
    mq_ref[...] = (q * ((MLA_NOPE + MLA_ROPE) ** -0.5 * LOG2_E)).astype(mq_ref.dtype)
    kvn = _rms(c[:, C_CKV:C_KPE], kvn_ref[...]).astype(BF16)
    k = _dot(kvn, wuk_ref[...]) + jnp.concatenate([kpe] * MLA_HEADS, axis=1)
    mk_ref[...] = k.astype(mk_ref.dtype)
    v = _dot(kvn, wuv_ref[...])
    lane = lax.broadcasted_iota(jnp.int32, v.shape, 1)
    value_lane = ((lane & (LANES - 1)) < MLA_V) == (((lane >> (LANES.bit_length() - 1)) & 1) == 0)
    mv_ref[...] = jnp.where(value_lane, v, 1.0).astype(mv_ref.dtype)

    g = _dot(u, w_ref[:, C_G:C_R])
    n_qk = GQA_HEADS * GQA_DIM + 2 * GQA_KV_HEADS * GQA_DIM
    gqa_ref[:, 0:n_qk] = _rope(g[:, 0:n_qk], cos64, sin64, GQA_DIM // 4).astype(gqa_ref.dtype)
    gqa_ref[:, n_qk:] = g[:, n_qk:].astype(gqa_ref.dtype)

    r = _dot(u, w_ref[:, C_R:C_GATE])
    n_qk = 2 * RET_HEADS * RET_QK
    ret_ref[:, 0:n_qk] = _rope(r[:, 0:n_qk], cos64, sin64, RET_QK // 4).astype(ret_ref.dtype)
    ret_ref[:, n_qk:] = r[:, n_qk:].astype(ret_ref.dtype)

    gate_ref[...] = _dot(u, w_ref[:, C_GATE:W_COLS]).astype(gate_ref.dtype)


def _inproj_kernel(hl_ref, hc_ref, *refs, tiles_per_batch, n_batch, d):
    latent = pl.program_id(0) < tiles_per_batch * n_batch
    _inproj_tile(jnp.where(latent, hl_ref[...], hc_ref[...]), *refs,
                 tiles_per_batch=tiles_per_batch, n_batch=n_batch, d=d)


def _inproj_specs(t, d, mods, gpre, w_all, rope, qn, kvn, wuq, wuk, wuv, *, layer, tm, n_batch, seq):
    tiles_per_batch = seq // tm
    n_lat_tiles = n_batch * tiles_per_batch
    const = lambda i: (0, 0)
    rows = lambda i: (i, 0)
    rope_idx = lambda i: (jnp.where(i < n_lat_tiles, i % tiles_per_batch, tiles_per_batch), 0)
    once = dict(pipeline_mode=pl.Buffered(1)) if tm > FUSED_TM else {}
    hq = MLA_HEADS * LANES
    outs = [jax.ShapeDtypeStruct((t, hq), BF16), jax.ShapeDtypeStruct((t, hq), BF16),
            jax.ShapeDtypeStruct((t, hq), BF16),
            jax.ShapeDtypeStruct((t, C_R - C_G), BF16),
            jax.ShapeDtypeStruct((t, C_GATE - C_R), F32),
            jax.ShapeDtypeStruct((t, W_COLS - C_GATE), BF16)]
    in_specs = [pl.BlockSpec(mods.shape, const),
                pl.BlockSpec((1, d), const),
                pl.BlockSpec((None,) + w_all.shape[1:], lambda i: (layer, 0, 0), **once),
                pl.BlockSpec((tm, 4 * LANES), rope_idx),
                pl.BlockSpec(qn.shape, const), pl.BlockSpec(kvn.shape, const),
                pl.BlockSpec(wuq.shape, const, **once), pl.BlockSpec(wuk.shape, const, **once),
                pl.BlockSpec(wuv.shape, const, **once)]
    return in_specs, [pl.BlockSpec((tm, o.shape[1]), rows) for o in outs], outs


def _inproj(h_lat, h_ctx, inproj_args, *, layer, n_batch, seq):
    d = h_lat.shape[1]
    t = h_lat.shape[0] + h_ctx.shape[0]
    n_lat_tiles = h_lat.shape[0] // TM
    in_specs, out_specs, out_shape = _inproj_specs(t, d, *inproj_args, layer=layer, tm=TM, n_batch=n_batch, seq=seq)
    return pl.pallas_call(
        functools.partial(_inproj_kernel, tiles_per_batch=seq // TM, n_batch=n_batch, d=d),
        grid=(t // TM,),
        in_specs=[pl.BlockSpec((TM, d), lambda i: (jnp.minimum(i, n_lat_tiles - 1), 0)),
                  pl.BlockSpec((TM, d), lambda i: (jnp.maximum(i - n_lat_tiles, 0), 0))] + in_specs,
        out_specs=out_specs,
        out_shape=out_shape,
        compiler_params=_cparams(("arbitrary",)),
        name="inproj",
    )(h_lat, h_ctx, *inproj_args)


MLA_HEADS_PER_STEP = 4


def _mla_kernel(q_ref, kl_ref, kc_ref, vl_ref, vc_ref, o_ref, s_ref, p_ref, *, with_lat):
    n_ctx = kc_ref.shape[0]

    def body(with_lat):
        n_keys = n_ctx + (kl_ref.shape[0] if with_lat else 0)

        def scores(h):
            sl = slice(h * LANES, (h + 1) * LANES)
            s_ref[h % 2, :, 0:n_ctx] = _dot_nt(q_ref[:, sl], kc_ref[:, sl])
            if with_lat:
                s_ref[h % 2, :, n_ctx:n_keys] = _dot_nt(q_ref[:, sl], kl_ref[:, sl])

        def probs(h):
            s = s_ref[h % 2, :, 0:n_keys]
            p_ref[h % 2, :, 0:n_keys] = jnp.exp2(s - jnp.max(s, axis=-1, keepdims=True)).astype(BF16)

        def weighted(h):
            sl = slice(h * LANES, (h + 1) * LANES)
            o = _dot(p_ref[h % 2, :, 0:n_ctx], vc_ref[:, sl])
            if with_lat:
                o = o + _dot(p_ref[h % 2, :, n_ctx:n_keys], vl_ref[:, sl])
            return o / pltpu.roll(o, LANES // 2, 1)

        outs = [None] * MLA_HEADS_PER_STEP
        scores(0)
        for h in range(MLA_HEADS_PER_STEP):
            if h + 1 < MLA_HEADS_PER_STEP:
                scores(h + 1)
            probs(h)
            outs[h] = weighted(h)
        for pr in range(MLA_HEADS_PER_STEP // 2):
            even, odd = outs[2 * pr], outs[2 * pr + 1]
            o_ref[:, pr * LANES:(pr + 1) * LANES] = jnp.where(_lane_lo(even.shape), even, odd).astype(o_ref.dtype)

    body(with_lat)


def _mla_call(mq, mk, mv, *, n_batch, seq, ctx, latent_queries):
    hps = MLA_HEADS_PER_STEP
    ctx_blk0 = n_batch * seq // ctx
    tq = ATT_TQ if latent_queries else ctx
    nq = seq // tq if latent_queries else 1
    q_blk0 = 0 if latent_queries else n_batch * seq // tq
    q_idx = lambda b, g, i: (q_blk0 + b * nq + i, g)
    ctx_idx = lambda b, g, i: (ctx_blk0 + b, g)
    lat_idx = (lambda b, g, i: (b, g)) if latent_queries else ctx_idx
    n_lat = seq if latent_queries else ctx
    n_keys = ctx + (seq if latent_queries else 0)
    in_specs = [pl.BlockSpec((tq, hps * LANES), q_idx),
                pl.BlockSpec((n_lat, hps * LANES), lat_idx), pl.BlockSpec((ctx, hps * LANES), ctx_idx),
                pl.BlockSpec((n_lat, hps * LANES), lat_idx), pl.BlockSpec((ctx, hps * LANES), ctx_idx)]
    return pl.pallas_call(
        functools.partial(_mla_kernel, with_lat=latent_queries),
        grid=(n_batch, MLA_HEADS // hps, nq),
        in_specs=in_specs,
        out_specs=pl.BlockSpec((tq, hps * MLA_V), lambda b, g, i: (b * nq + i, g)),
        out_shape=jax.ShapeDtypeStruct((n_batch * nq * tq, MLA_HEADS * MLA_V), BF16),
        scratch_shapes=[pltpu.VMEM((2, tq, n_keys), F32), pltpu.VMEM((2, tq, n_keys), BF16)],
        compiler_params=_cparams(("arbitrary", "arbitrary", "arbitrary")),
        name="mla_attn" if latent_queries else "mla_attn_ctx",
    )(mq, mk, mk, mv, mv)


def _mla_attention(mq, mk, mv, *, n_batch, seq, ctx, with_ctx_queries):
    dims = dict(n_batch=n_batch, seq=seq, ctx=ctx)
    lat = _mla_call(mq, mk, mv, latent_queries=True, **dims)
    return lat, (_mla_call(mq, mk, mv, latent_queries=False, **dims) if with_ctx_queries else lat)


def _win_kernel(q_ref, kp_ref, kcur_ref, kn_ref, vp_ref, vcur_ref, vn_ref, kc_ref, vc_ref, sink_ref, o_ref,
                s_ref, p_ref, *, seq, with_lat):
    i = pl.program_id(1)
    tq = q_ref.shape[0]
    group = GQA_HEADS // GQA_KV_HEADS

    def body(with_lat):
        n_ctx = kc_ref.shape[0]
        n_keys = n_ctx + (tq + 2 * WINDOW if with_lat else 0)
        if with_lat:
            q_pos = i * tq + lax.broadcasted_iota(jnp.int32, (tq, n_keys), 0)
            k_pos = i * tq - WINDOW - n_ctx + lax.broadcasted_iota(jnp.int32, (tq, n_keys), 1)
            in_band = (jnp.abs(q_pos - k_pos) <= WINDOW) & (k_pos >= 0) & (k_pos < seq)
            valid = in_band | (lax.broadcasted_iota(jnp.int32, (tq, n_keys), 1) < n_ctx)
        lo = _lane_lo((tq, LANES))
        lo_k = _lane_lo((n_keys, LANES))
        keys, values = [], []
        for kv in range(GQA_KV_HEADS):
            sl = slice(kv * LANES, (kv + 1) * LANES)
            if with_lat:
                k_all = jnp.concatenate([kc_ref[:, sl], kp_ref[:, sl], kcur_ref[:, sl], kn_ref[:, sl]], axis=0)
                v_all = jnp.concatenate([vc_ref[:, sl], vp_ref[:, sl], vcur_ref[:, sl], vn_ref[:, sl]], axis=0)
            else:
                k_all, v_all = kc_ref[:, sl], vc_ref[:, sl]
            keys.append(k_all)
            one = jnp.ones_like(v_all)
            values.append((jnp.where(lo_k, v_all, one), jnp.where(lo_k, one, v_all)))

        def scores(hd):
            kv, pair = hd // group, hd // 2
            qp = q_ref[:, pair * LANES:(pair + 1) * LANES]
            qm = jnp.where(lo if hd % 2 == 0 else jnp.logical_not(lo), qp, jnp.zeros_like(qp))
            s = _dot_nt(qm, keys[kv])
            s_ref[hd % 2, :, 0:n_keys] = jnp.where(valid, s, NEG_INF) if with_lat else s

        def probs(hd):
            s = s_ref[hd % 2, :, 0:n_keys]
            m = jnp.maximum(jnp.max(s, axis=-1, keepdims=True), sink_ref[hd:hd + 1, 0:1])
            p_ref[hd % 2, :, 0:n_keys] = jnp.exp2(s - m).astype(BF16)
            return jnp.exp2(sink_ref[hd:hd + 1, 0:1] - m)

        def weighted(hd, sink_term):
            o = _dot(p_ref[hd % 2, :, 0:n_keys], values[hd // group][hd % 2])
            return o / (pltpu.roll(o, LANES // 2, 1) + sink_term)

        outs = [None] * GQA_HEADS
        scores(0)
        for hd in range(GQA_HEADS):
            if hd + 1 < GQA_HEADS:
                scores(hd + 1)
            outs[hd] = weighted(hd, probs(hd))
        for pair in range(GQA_HEADS // 2):
            o_ref[:, pair * LANES:(pair + 1) * LANES] = jnp.where(
                lo, outs[2 * pair], outs[2 * pair + 1]).astype(o_ref.dtype)

    body(with_lat)


def _win_call(gqa, sink_tab, *, n_batch, seq, ctx, latent_queries):
    tq = WIN_TQ if latent_queries else ctx
    nq = seq // tq if latent_queries else 1
    q_blk0 = 0 if latent_queries else n_batch * seq // tq
    per_tile = tq // WINDOW
    n_win_blocks = seq // WINDOW
    ctx_blk0 = n_batch * seq // ctx
    nqk = GQA_HEADS * GQA_DIM
    kw = 2 * GQA_KV_HEADS * GQA_DIM
    k_col, v_col = nqk // kw, nqk // kw + 1
    q_idx = lambda b, i: (q_blk0 + b * nq + i, 0)
    cidx = lambda col: (lambda b, i: (ctx_blk0 + b, col))
    if latent_queries:
        cur = lambda col: (lambda b, i: (b * nq + i, col))
        prev = lambda col: (lambda b, i: (b * n_win_blocks + jnp.maximum(per_tile * i - 1, 0), col))
        nxt = lambda col: (lambda b, i: (b * n_win_blocks + jnp.minimum(per_tile * (i + 1), n_win_blocks - 1), col))
        band = [((WINDOW, kw), prev), ((tq, kw), cur), ((WINDOW, kw), nxt)]
    else:
        band = [((ctx, kw), cidx)] * 3
    n_keys = ctx + (tq + 2 * WINDOW if latent_queries else 0)
    in_specs = ([pl.BlockSpec((tq, nqk), q_idx)]
                + [pl.BlockSpec(shape, idx(k_col)) for shape, idx in band]
                + [pl.BlockSpec(shape, idx(v_col)) for shape, idx in band]
                + [pl.BlockSpec((ctx, kw), cidx(k_col)), pl.BlockSpec((ctx, kw), cidx(v_col)),
                   pl.BlockSpec(sink_tab.shape, lambda b, i: (0, 0))])
    return pl.pallas_call(
        functools.partial(_win_kernel, seq=seq, with_lat=latent_queries),
        grid=(n_batch, nq),
        in_specs=in_specs,
        out_specs=pl.BlockSpec((tq, nqk), lambda b, i: (b * nq + i, 0)),
        out_shape=jax.ShapeDtypeStruct((n_batch * nq * tq, nqk), BF16),
        scratch_shapes=[pltpu.VMEM((2, tq, n_keys), F32), pltpu.VMEM((2, tq, n_keys), BF16)],
        compiler_params=_cparams(("arbitrary", "arbitrary")),
        name="win_attn" if latent_queries else "win_attn_ctx",
    )(*([gqa] * 9 + [sink_tab]))


def _window_attention(gqa, sink_tab, *, n_batch, seq, ctx, with_ctx_queries):
    dims = dict(n_batch=n_batch, seq=seq, ctx=ctx)
    lat = _win_call(gqa, sink_tab, latent_queries=True, **dims)
    return lat, (_win_call(gqa, sink_tab, latent_queries=False, **dims) if with_ctx_queries else lat)


def _ret_kernel(f_ref, b_ref, lg_ref, of_ref, ob_ref, sf_ref, sb_ref, qdec_ref, kdec_ref, cdec_ref, inner_ref):
    @pl.when(pl.program_id(1) == 0)
    def _():
        sf_ref[...] = jnp.zeros_like(sf_ref)
        sb_ref[...] = jnp.zeros_like(sb_ref)

    L = f_ref.shape[0]
    lo = _lane_lo((L, LANES))
    srow_lo = lax.broadcasted_iota(jnp.int32, (LANES, LANES), 0) < RET_QK
    nq = RET_HEADS * RET_QK
    n_pairs = RET_HEADS // 2

    @pl.when(pl.program_id(1) == 0)
    def _():
        ii = lax.broadcasted_iota(jnp.int32, (L, L), 0)
        jj = lax.broadcasted_iota(jnp.int32, (L, L), 1)
        row = lax.broadcasted_iota(jnp.int32, (L, LANES), 0).astype(F32)
        for direction, forward in enumerate((True, False)):
            dist = ii - jj if forward else jj - ii
            distf = jnp.maximum(dist, 0).astype(F32)
            for pr in range(n_pairs):
                r0 = direction * RET_HEADS + 2 * pr
                lg = [lg_ref[r0 + e:r0 + e + 1, :] for e in range(2)]
                lg_lane = jnp.where(lo, lg[0], lg[1])
                qdec_ref[direction * n_pairs + pr] = jnp.exp(lg_lane * ((row + 1.0) if forward else (L - row)))
                kdec_ref[direction * n_pairs + pr] = jnp.exp(lg_lane * ((L - 1.0 - row) if forward else row))
                cdec_ref[direction * n_pairs + pr] = jnp.where(srow_lo, jnp.exp(lg[0] * float(L)),
                                                               jnp.exp(lg[1] * float(L)))
                for e in range(2):
                    inner_ref[r0 + e] = jnp.where(dist >= 0, jnp.exp(lg[e][:, 0:1] * distf), 0.0)

    def scan_chunk(x_ref, o_ref, s_ref, direction):
        for pr in range(n_pairs):
            q = x_ref[:, pr * LANES:(pr + 1) * LANES].astype(F32)
            k = x_ref[:, nq + pr * LANES:nq + (pr + 1) * LANES].astype(F32)
            qd = q * qdec_ref[direction * n_pairs + pr]
            kdb = (k * kdec_ref[direction * n_pairs + pr]).astype(BF16)
            kb = k.astype(BF16)
            state = s_ref[pr]
            state_b = state.astype(BF16)
            upd = []
            for e in range(2):
                hd = 2 * pr + e
                keep = lo if e == 0 else jnp.logical_not(lo)
                v = x_ref[:, 2 * nq + hd * RET_V:2 * nq + (hd + 1) * RET_V].astype(BF16)
                attn = _dot_nt(jnp.where(keep, q, 0.0).astype(BF16), kb) * inner_ref[direction * RET_HEADS + hd]
                o = _dot(attn.astype(BF16), v) + _dot(jnp.where(keep, qd, 0.0).astype(BF16), state_b)
                o_ref[:, hd * RET_V:(hd + 1) * RET_V] = o
                upd.append(_dot_tn(kdb, v))
            s_ref[pr] = state * cdec_ref[direction * n_pairs + pr] + jnp.where(srow_lo, upd[0], upd[1])

    scan_chunk(f_ref, of_ref, sf_ref, 0)
    scan_chunk(b_ref, ob_ref, sb_ref, 1)


def _retention(ret, lg_tab, *, n_batch, seq, ctx):
    t = ret.shape[0]
    L = RET_TILE
    assert seq % L == 0 and ctx % L == 0
    n_lat, n_ctx = seq // L, ctx // L
    ctx0 = n_batch * n_lat
    width = 2 * RET_HEADS * RET_QK + RET_HEADS * RET_V
    fwd = lambda b, s: (jnp.where(s < n_ctx, ctx0 + b * n_ctx + s, b * n_lat + s - n_ctx), 0)
    bwd = lambda b, s: (jnp.where(s < n_ctx, ctx0 + b * n_ctx + n_ctx - 1 - s, b * n_lat + n_lat - 1 - (s - n_ctx)), 0)
    out = jax.ShapeDtypeStruct((t, RET_HEADS * RET_V), F32)
    return pl.pallas_call(
        _ret_kernel,
        grid=(n_batch, n_lat + n_ctx),
        in_specs=[pl.BlockSpec((L, width), fwd), pl.BlockSpec((L, width), bwd),
                  pl.BlockSpec(lg_tab.shape, lambda b, s: (0, 0))],
        out_specs=[pl.BlockSpec((L, RET_HEADS * RET_V), fwd), pl.BlockSpec((L, RET_HEADS * RET_V), bwd)],
        out_shape=[out, out],
        scratch_shapes=[pltpu.VMEM((RET_HEADS // 2, LANES, RET_V), F32),
                        pltpu.VMEM((RET_HEADS // 2, LANES, RET_V), F32),
                        pltpu.VMEM((RET_HEADS, L, LANES), F32), pltpu.VMEM((RET_HEADS, L, LANES), F32),
                        pltpu.VMEM((RET_HEADS, LANES, RET_V), F32), pltpu.VMEM((2 * RET_HEADS, L, L), F32)],
        compiler_params=_cparams(("arbitrary", "arbitrary")),
        name="retention",
    )(ret, ret, lg_tab)


def _merge_kernel(al_ref, ac_ref, wl_ref, wc_ref, of_ref, ob_ref, rg_ref, gt_ref, hl_ref, hc_ref, mod_ref,
                  gpost_ref, gffn_ref, wa_ref, ww_ref, wr_ref, wo_ref, h1_ref, v_ref, vp_ref,
                  *, tiles_per_batch, n_batch, d):
    i = pl.program_id(0)
    bi = jnp.minimum(i // tiles_per_batch, n_batch)
    latent = i < tiles_per_batch * n_batch
    a_tile = jnp.where(latent, al_ref[...], ac_ref[...])
    w_tile = jnp.where(latent, wl_ref[...], wc_ref[...])
    h_tile = jnp.where(latent, hl_ref[...], hc_ref[...])
    o = of_ref[...] + ob_ref[...]
    normed = []
    for hd in range(RET_HEADS):
        oh = o[:, hd * RET_V:(hd + 1) * RET_V]
        dev = oh - jnp.mean(oh, axis=-1, keepdims=True)
        normed.append(dev * lax.rsqrt(jnp.mean(dev * dev, axis=-1, keepdims=True) + NORM_EPS))
    g = rg_ref[...].astype(F32)
    r = (g * _sigmoid(g)) * jnp.concatenate(normed, axis=1)
    y = (_sigmoid(gt_ref[:, 0:d].astype(F32)) * _dot(a_tile, wa_ref[...])
         + _sigmoid(gt_ref[:, d:2 * d].astype(F32)) * _dot(w_tile, ww_ref[...])
         + _sigmoid(gt_ref[:, 2 * d:3 * d].astype(F32)) * _dot(r.astype(BF16), wr_ref[...]))
    z = _dot(y.astype(BF16), wo_ref[...])
    g1 = mod_ref[pl.ds(bi, 1), 2 * d:3 * d]
    sh2 = mod_ref[pl.ds(bi, 1), 3 * d:4 * d]
    sc2 = mod_ref[pl.ds(bi, 1), 4 * d:5 * d]
    h1 = h_tile + g1 * _rms(z, gpost_ref[...])
    h1_ref[...] = h1
    v = _rms(h1, gffn_ref[...]) * (1.0 + sc2) + sh2
    v_ref[...] = v.astype(v_ref.dtype)
    vp_ref[...] = _pack_halves(v)


def _merge(a, w, o_f, o_b, ret, gates, h, mods, gpost, gffn, wa, ww, wr, wo, *, n_rows, n_batch, seq):
    d = h[0].shape[1]
    n_lat_tiles = n_batch * seq // TM
    rows = lambda i: (i, 0)
    lat_rows = lambda i: (jnp.minimum(i, n_lat_tiles - 1), 0)
    ctx_rows = lambda i: (jnp.maximum(i - n_lat_tiles, 0), 0)
    const = lambda i: (0, 0)
    rv = RET_HEADS * RET_V
    rg_col = (2 * RET_HEADS * RET_QK + rv) // rv
    outs = [jax.ShapeDtypeStruct((n_rows, d), F32), jax.ShapeDtypeStruct((n_rows, d), BF16),
            jax.ShapeDtypeStruct((n_rows, d // 2), jnp.uint32)]
    return pl.pallas_call(
        functools.partial(_merge_kernel, tiles_per_batch=seq // TM, n_batch=n_batch, d=d),
        grid=(n_rows // TM,),
        in_specs=[pl.BlockSpec((TM, a[0].shape[1]), lat_rows), pl.BlockSpec((TM, a[1].shape[1]), ctx_rows),
                  pl.BlockSpec((TM, w[0].shape[1]), lat_rows), pl.BlockSpec((TM, w[1].shape[1]), ctx_rows),
                  pl.BlockSpec((TM, rv), rows), pl.BlockSpec((TM, rv), rows),
                  pl.BlockSpec((TM, rv), lambda i: (i, rg_col)),
                  pl.BlockSpec((TM, 3 * d), rows), pl.BlockSpec((TM, d), lat_rows),
                  pl.BlockSpec((TM, d), lambda i: (h[2] + jnp.maximum(i - n_lat_tiles, 0), 0)),
                  pl.BlockSpec(mods.shape, const), pl.BlockSpec((1, d), const), pl.BlockSpec((1, d), const),
                  pl.BlockSpec(wa.shape, const), pl.BlockSpec(ww.shape, const),
                  pl.BlockSpec(wr.shape, const), pl.BlockSpec(wo.shape, const)],
        out_specs=[pl.BlockSpec((TM, o.shape[1]), rows) for o in outs],
        out_shape=outs,
        compiler_params=_cparams(("arbitrary",)),
        name="merge",
    )(a[0], a[1], w[0], w[1], o_f, o_b, ret, gates, h[0], h[1], mods, gpost, gffn, wa, ww, wr, wo)


def _router_kernel(v_ref, rw_ref, rb_ref, eidx_ref, rank_ref, w_ref, cnt_ref, carry_ref):
    @pl.when(pl.program_id(0) == 0)
    def _():
        carry_ref[...] = jnp.zeros_like(carry_ref)

    tm = v_ref.shape[0]
    scores = _sigmoid(_dot_nt(rw_ref[...], v_ref[...]))
    sel = scores + rb_ref[...]
    neg = -jnp.inf
    n_grp, per = N_EXPERT_GROUPS, EXPERTS_PER_GROUP

    sel3 = sel.reshape(n_grp, per, tm)
    member_id = lax.broadcasted_iota(jnp.int32, sel3.shape, 1)
    m1 = jnp.max(sel3, axis=1, keepdims=True)
    i1 = jnp.min(jnp.where(sel3 == m1, member_id, per), axis=1, keepdims=True)
    m2 = jnp.max(jnp.where(member_id == i1, neg, sel3), axis=1, keepdims=True)
    gscore = (m1 + m2).reshape(n_grp, tm)
    gid = lax.broadcasted_iota(jnp.int32, gscore.shape, 0)
    ahead = jnp.zeros(gscore.shape, jnp.int32)
    for gj in range(n_grp):
        other = gscore[gj:gj + 1, :]
        ahead = ahead + jnp.where((other > gscore) | ((other == gscore) & (gid > gj)), 1, 0)
    group_ok = (ahead < TOPK_GROUPS).reshape(n_grp, 1, tm)
    sel = jnp.where(group_ok, sel3, NEG_INF).reshape(N_EXPERTS, tm)

    eid = lax.broadcasted_iota(jnp.int32, sel.shape, 0)
    chosen = jnp.zeros(sel.shape, jnp.bool_)
    picks = []
    for _ in range(TOP_K):
        m = jnp.max(sel, axis=0, keepdims=True)
        idx = jnp.min(jnp.where(sel == m, eid, N_EXPERTS), axis=0, keepdims=True)
        hit = eid == idx
        chosen = chosen | hit
        sel = jnp.where(hit, neg, sel)
        picks.append(idx)
    w = jnp.where(chosen, scores, 0.0)
    gate = ROUTED_SCALE * w / jnp.sum(w, axis=0, keepdims=True)

    member = jnp.where(chosen, 1.0, 0.0)
    earlier = lax.broadcasted_iota(jnp.int32, (tm, tm), 0) < lax.broadcasted_iota(jnp.int32, (tm, tm), 1)
    pos = _dot(member.astype(BF16), jnp.where(earlier, 1.0, 0.0).astype(BF16)) + carry_ref[...]
    for k, idx in enumerate(picks):
        hit = eid == idx
        eidx_ref[k:k + 1, :] = idx
        rank_ref[k:k + 1, :] = jnp.sum(jnp.where(hit, pos, 0.0), axis=0, keepdims=True)
        w_ref[k:k + 1, :] = jnp.sum(jnp.where(hit, gate, 0.0), axis=0, keepdims=True)
    carry_ref[...] += jnp.sum(member, axis=1, keepdims=True)
    cnt_ref[...] = carry_ref[...]


def _router(v, rw_t, rb):
    n_rows, d = v.shape
    cols = lambda i: (0, i)
    const = lambda i: (0, 0)
    outs = [jax.ShapeDtypeStruct((TOP_K, n_rows), jnp.int32), jax.ShapeDtypeStruct((TOP_K, n_rows), F32),
            jax.ShapeDtypeStruct((TOP_K, n_rows), F32), jax.ShapeDtypeStruct((N_EXPERTS, 1), F32)]
    return pl.pallas_call(
        _router_kernel,
        grid=(n_rows // TM,),
        in_specs=[pl.BlockSpec((TM, d), lambda i: (i, 0)), pl.BlockSpec(rw_t.shape, const),
                  pl.BlockSpec(rb.shape, const)],
        out_specs=[pl.BlockSpec((TOP_K, TM), cols), pl.BlockSpec((TOP_K, TM), cols),
                   pl.BlockSpec((TOP_K, TM), cols), pl.BlockSpec((N_EXPERTS, 1), const)],
        out_shape=outs,
        scratch_shapes=[pltpu.VMEM((N_EXPERTS, 1), F32)],
        compiler_params=_cparams(("arbitrary",)),
        name="router",
    )(v, rw_t, rb)


def _slots_kernel(eidx_ref, rank_ref, cnt_ref, slot_ref):
    tm = eidx_ref.shape[1]
    eid = lax.broadcasted_iota(jnp.int32, (N_EXPERTS, tm), 0)
    for k in range(TOP_K):
        before = jnp.sum(jnp.where(eid < eidx_ref[k:k + 1, :], cnt_ref[...], 0.0), axis=0, keepdims=True)
        slot_ref[k:k + 1, :] = (before + rank_ref[k:k + 1, :]).astype(jnp.int32)


def _slots(eidx, rank, cnt):
    n_rows = eidx.shape[1]
    tm = next(c for c in (2048, 1024, 512, 256) if n_rows % c == 0)
    cols = lambda i: (0, i)
    return pl.pallas_call(
        _slots_kernel,
        grid=(n_rows // tm,),
        in_specs=[pl.BlockSpec((TOP_K, tm), cols), pl.BlockSpec((TOP_K, tm), cols),
                  pl.BlockSpec(cnt.shape, lambda i: (0, 0))],
        out_specs=pl.BlockSpec((TOP_K, tm), cols),
        out_shape=jax.ShapeDtypeStruct((TOP_K, n_rows), jnp.int32),
        compiler_params=_cparams(("arbitrary",)),
        name="slots",
    )(eidx, rank, cnt)


SC_WINDOW = 128


def _sc_mesh():
    return plsc.VectorSubcoreMesh(core_axis_name="core", subcore_axis_name="subcore")


def _sc_dispatch(rows, slot_t, n_out):
    width = rows.shape[1]
    n_chunks = slot_t.shape[1] // SC_WINDOW
    info = plsc.get_sparse_core_info()
    n_workers = info.num_cores * info.num_subcores

    @functools.partial(
        pl.kernel, mesh=_sc_mesh(),
        out_type=jax.ShapeDtypeStruct((n_out, width), rows.dtype),
        scratch_types=[pltpu.VMEM((TOP_K, SC_WINDOW), jnp.int32), pltpu.VMEM((SC_WINDOW, width), rows.dtype)],
        name="moe_dispatch")
    def run(rows_hbm, idx_hbm, out_hbm, idx_v, rows_v):
        wid = lax.axis_index("subcore") * info.num_cores + lax.axis_index("core")

        @pl.loop(wid, n_chunks, step=n_workers)
        def _(c):
            r0 = pl.multiple_of(c * SC_WINDOW, SC_WINDOW)
            pltpu.sync_copy(idx_hbm.at[:, pl.ds(r0, SC_WINDOW)], idx_v)
            pltpu.sync_copy(rows_hbm.at[pl.ds(r0, SC_WINDOW)], rows_v)
            for k in range(TOP_K):
                pltpu.sync_copy(rows_v, out_hbm.at[idx_v.at[k]])

    return run(rows, slot_t)


def _sc_collect(rows, slot_t):
    n_picks, n_rows = slot_t.shape
    width = rows.shape[1]
    n_chunks = n_rows // SC_WINDOW
    info = plsc.get_sparse_core_info()
    n_workers = info.num_cores * info.num_subcores

    @functools.partial(
        pl.kernel, mesh=_sc_mesh(),
        out_type=jax.ShapeDtypeStruct((n_picks, n_rows, width), rows.dtype),
        scratch_types=[pltpu.VMEM((TOP_K, SC_WINDOW), jnp.int32), pltpu.VMEM((SC_WINDOW, width), rows.dtype)],
        name="moe_collect")
    def run(rows_hbm, idx_hbm, out_hbm, idx_v, rows_v):
        wid = lax.axis_index("subcore") * info.num_cores + lax.axis_index("core")

        @pl.loop(wid, n_chunks, step=n_workers)
        def _(c):
            r0 = pl.multiple_of(c * SC_WINDOW, SC_WINDOW)
            pltpu.sync_copy(idx_hbm.at[:, pl.ds(r0, SC_WINDOW)], idx_v)
            for k in range(TOP_K):
                pltpu.sync_copy(rows_hbm.at[idx_v.at[k]], rows_v)
                pltpu.sync_copy(rows_v, out_hbm.at[k, pl.ds(r0, SC_WINDOW)])

    return run(rows, slot_t)


EXPERT_TILE = 512


def _work_items(cnt, n_slots):
    counts = cnt[:, 0].astype(jnp.int32)
    ends = jnp.cumsum(counts)
    n_tiles = n_slots // EXPERT_TILE
    bounds = jnp.sort(jnp.concatenate([jnp.arange(n_tiles, dtype=jnp.int32) * EXPERT_TILE, ends - counts]))
    nxt = jnp.concatenate([bounds[1:], jnp.array([n_slots], jnp.int32)])
    tile = jnp.minimum(bounds // EXPERT_TILE, n_tiles - 1)
    expert = jnp.sum((ends[None, :] <= bounds[:, None]).astype(jnp.int32), axis=1)
    expert = jnp.minimum(expert, N_EXPERTS - 1)
    return tile, expert, bounds - tile * EXPERT_TILE, nxt - tile * EXPERT_TILE


XS_RING = 3


def _experts_kernel(tile_ref, exp_ref, lo_ref, hi_ref, xs_hbm, wg_ref, wu_ref, wd_ref, ys_ref,
                    acc_ref, wgb_ref, wub_ref, wdb_ref, xbuf_ref, xsem):
    i = pl.program_id(0)
    n_items = pl.num_programs(0)
    lo, hi = lo_ref[i], hi_ref[i]

    def tile_copy(item):
        slot = lax.rem(item, XS_RING)
        row0 = pl.multiple_of(tile_ref[item] * EXPERT_TILE, EXPERT_TILE)
        return pltpu.make_async_copy(xs_hbm.at[pl.ds(row0, EXPERT_TILE)], xbuf_ref.at[slot], xsem.at[slot])

    @pl.when(i == 0)
    def _():
        for ahead in range(XS_RING - 1):
            pl.when(ahead < n_items)(lambda: tile_copy(ahead).start())

    @pl.when(i + XS_RING - 1 < n_items)
    def _():
        tile_copy(i + XS_RING - 1).start()

    tile_copy(i).wait()
    xs_ref = xbuf_ref.at[lax.rem(i, XS_RING)]

    @pl.when((i == 0) | (exp_ref[i] != exp_ref[jnp.maximum(i - 1, 0)]))
    def _():
        wgb_ref[...] = wg_ref[...].astype(BF16)
        wub_ref[...] = wu_ref[...].astype(BF16)
        wdb_ref[...] = wd_ref[...].astype(BF16)

    def ffn():
        x_lo, x_hi = _unpack_halves(xs_ref[...])
        x_lo, x_hi = x_lo.astype(BF16), x_hi.astype(BF16)
        n = x_lo.shape[1]
        a = _dot(x_lo, wgb_ref[0:n, :]) + _dot(x_hi, wgb_ref[n:, :])
        u = _dot(x_lo, wub_ref[0:n, :]) + _dot(x_hi, wub_ref[n:, :])
        return _dot(((a * _sigmoid(a)) * u).astype(BF16), wdb_ref[...])

    whole = (lo == 0) & (hi == EXPERT_TILE)

    @pl.when(whole)
    def _():
        ys_ref[...] = _pack_halves(ffn())

    @pl.when(jnp.logical_not(whole) & (hi > lo))
    def _():
        y = ffn()
        row = lax.broadcasted_iota(jnp.int32, y.shape, 0)
        y = jnp.where((row >= lo) & (row < hi), y, 0.0)

        @pl.when(lo == 0)
        def _():
            acc_ref[...] = y

        @pl.when((lo > 0) & (hi < EXPERT_TILE))
        def _():
            acc_ref[...] += y

        @pl.when((lo > 0) & (hi == EXPERT_TILE))
        def _():
            ys_ref[...] = _pack_halves(acc_ref[...] + y)


def _experts(xs, items, layer, exp_wg, exp_wu, exp_wd):
    n_slots, half = xs.shape
    d, hid = exp_wg.shape[-2:]
    tile, expert, lo, hi = items
    grid_spec = pltpu.PrefetchScalarGridSpec(
        num_scalar_prefetch=4,
        grid=(tile.shape[0],),
        in_specs=[pl.BlockSpec(memory_space=pl.ANY),
                  pl.BlockSpec((None, None, d, hid), lambda i, t, e, lo, hi: (layer, e[i], 0, 0)),
                  pl.BlockSpec((None, None, d, hid), lambda i, t, e, lo, hi: (layer, e[i], 0, 0)),
                  pl.BlockSpec((None, None, hid, d), lambda i, t, e, lo, hi: (layer, e[i], 0, 0))],
        out_specs=pl.BlockSpec((EXPERT_TILE, half), lambda i, t, e, lo, hi: (t[i], 0)),
        scratch_shapes=[pltpu.VMEM((EXPERT_TILE, d), F32), pltpu.VMEM((d, hid), BF16),
                        pltpu.VMEM((d, hid), BF16), pltpu.VMEM((hid, d), BF16),
                        pltpu.VMEM((XS_RING, EXPERT_TILE, half), jnp.uint32),
                        pltpu.SemaphoreType.DMA((XS_RING,))])
    return pl.pallas_call(
        _experts_kernel,
        grid_spec=grid_spec,
        out_shape=jax.ShapeDtypeStruct((n_slots, half), jnp.uint32),
        compiler_params=_cparams(("arbitrary",)),
        name="experts",
    )(tile, expert, lo, hi, xs, exp_wg, exp_wu, exp_wd)


def _moe_out_tile(yg_ref, w_ref, v_ref, sg_ref, su_ref, sd_ref, h1_ref, mod_ref, gpost_ref,
                  *, tiles_per_batch, n_batch, d):
    i = pl.program_id(0)
    x = v_ref[...]
    a = _dot(x, sg_ref[...])
    f = _dot(((a * _sigmoid(a)) * _dot(x, su_ref[...])).astype(BF16), sd_ref[...])
    n = d // 2
    f_lo, f_hi = f[:, :n], f[:, n:]
    w = w_ref[...]
    for k in range(TOP_K):
        y_lo, y_hi = _unpack_halves(yg_ref[k])
        wk = w[:, k:k + 1]
        f_lo = f_lo + wk * y_lo
        f_hi = f_hi + wk * y_hi
    f = jnp.concatenate([f_lo, f_hi], axis=1)
    bi = jnp.minimum(i // tiles_per_batch, n_batch)
    g2 = mod_ref[pl.ds(bi, 1), 5 * d:6 * d]
    return h1_ref[...] + g2 * _rms(f, gpost_ref[...])


N_MOE_OUT_IN = 9
N_INPROJ_IN = 9


def _moe_out_kernel(*refs, **kw):
    refs[N_MOE_OUT_IN][...] = _moe_out_tile(*refs[:N_MOE_OUT_IN], **kw)


def _moe_out_inproj_kernel(*refs, n_batch, d, tiles_per_batch):
    dims = dict(tiles_per_batch=tiles_per_batch, n_batch=n_batch, d=d)
    h = _moe_out_tile(*refs[:N_MOE_OUT_IN], **dims)
    n_in = N_MOE_OUT_IN + N_INPROJ_IN
    refs[n_in][...] = h
    _inproj_tile(h, *refs[N_MOE_OUT_IN:n_in], *refs[n_in + 1:], **dims)


def _moe_out_specs(yg, w, v, sg, su, sd, h1, mods, gpost, tm):
    d = v.shape[1]
    rows = lambda i: (i, 0)
    const = lambda i: (0, 0)
    return [pl.BlockSpec((TOP_K, tm, d // 2), lambda i: (0, i, 0)), pl.BlockSpec((tm, TOP_K), rows),
            pl.BlockSpec((tm, d), rows),
            pl.BlockSpec(sg.shape, const), pl.BlockSpec(su.shape, const), pl.BlockSpec(sd.shape, const),
            pl.BlockSpec((tm, d), rows), pl.BlockSpec(mods.shape, const), pl.BlockSpec((1, d), const)]


def _moe_out(moe_args, *, n_batch, seq):
    v = moe_args[2]
    n_rows, d = v.shape
    return pl.pallas_call(
        functools.partial(_moe_out_kernel, tiles_per_batch=seq // TM, n_batch=n_batch, d=d),
        grid=(n_rows // TM,),
        in_specs=_moe_out_specs(*moe_args, TM),
        out_specs=pl.BlockSpec((TM, d), lambda i: (i, 0)),
        out_shape=jax.ShapeDtypeStruct((n_rows, d), F32),
        compiler_params=_cparams(("arbitrary",)),
        name="moe_out",
    )(*moe_args)


FUSED_TM = 256


def _moe_out_inproj(moe_args, inproj_args, *, layer, n_batch, seq):
    v = moe_args[2]
    n_rows, d = v.shape
    tm = FUSED_TM
    in_specs, out_specs, out_shape = _inproj_specs(n_rows, d, *inproj_args, layer=layer, tm=tm, n_batch=n_batch,
                                                   seq=seq)
    rows = lambda i: (i, 0)
    return pl.pallas_call(
        functools.partial(_moe_out_inproj_kernel, tiles_per_batch=seq // tm, n_batch=n_batch, d=d),
        grid=(n_rows // tm,),
        in_specs=_moe_out_specs(*moe_args, tm) + in_specs,
        out_specs=[pl.BlockSpec((tm, d), rows)] + out_specs,
        out_shape=[jax.ShapeDtypeStruct((n_rows, d), F32)] + out_shape,
        compiler_params=_cparams(("arbitrary",)),
        name="moe_out_inproj",
    )(*moe_args, *inproj_args)


def _moe_routed(v, vp, layer, rw, rb, exp_wg, exp_wu, exp_wd):
    n_rows = v.shape[0]
    n_slots = n_rows * TOP_K
    assert n_slots % EXPERT_TILE == 0
    eidx, rank, w_t, cnt = _router(v, rw, rb)
    slot_t = _slots(eidx, rank, cnt)
    xs = _sc_dispatch(vp, slot_t, n_slots)
    ys = _experts(xs, _work_items(cnt, n_slots), layer, exp_wg, exp_wu, exp_wd)
    return _sc_collect(ys, slot_t), w_t.T


def _rope_tables(seq):
    rows = seq // GRID_W
    row_id = np.repeat(np.arange(rows, dtype=np.float64), GRID_W)
    col_id = np.tile(np.arange(GRID_W, dtype=np.float64), rows)

    def tables(rot_dim):
        axis_dim = rot_dim // 2
        inv_freq = ROPE_BASE ** (-np.arange(0, axis_dim, 2, dtype=np.float64) / axis_dim)
        ang_r = row_id[:, None] * inv_freq[None, :]
        ang_c = col_id[:, None] * inv_freq[None, :]
        cos = np.concatenate([np.cos(ang_r), np.cos(ang_r), np.cos(ang_c), np.cos(ang_c)], axis=1)
        sin = np.concatenate([-np.sin(ang_r), np.sin(ang_r), -np.sin(ang_c), np.sin(ang_c)], axis=1)
        return cos, sin

    cos64, sin64 = tables(GQA_DIM)
    cos32, sin32 = tables(MLA_ROPE)
    ones = np.ones((seq, MLA_NOPE))
    pad = LANES - MLA_NOPE - MLA_ROPE
    cospe = np.concatenate([ones, cos32, np.ones((seq, pad))], axis=1)
    sinpe = np.concatenate([0 * ones, sin32, np.zeros((seq, pad))], axis=1)
    tab = np.concatenate([cos64, cos64, sin64, sin64, cospe, sinpe], axis=1)
    ident = np.concatenate([np.ones((TM, LANES)), np.zeros((TM, LANES)),
                            np.ones((TM, LANES)), np.zeros((TM, LANES))], axis=1)
    return jnp.asarray(np.concatenate([tab, ident], axis=0), F32)


def _w_in_moves(d):
    sizes = (MLA_Q_LORA, MLA_KV_LORA, MLA_ROPE, GQA_HEADS * GQA_DIM, GQA_KV_HEADS * GQA_DIM,
             GQA_KV_HEADS * GQA_DIM, RET_HEADS * RET_QK, RET_HEADS * RET_QK, RET_HEADS * RET_V,
             RET_HEADS * RET_V, 3 * d)
    src = [sum(sizes[:i]) for i in range(len(sizes))]
    cq, ckv, kpe, gq, gk, gv, rq, rk, rv, rg, gates = src
    moves = [(cq, C_CQ, MLA_Q_LORA, 1.0), (ckv, C_CKV, MLA_KV_LORA, 1.0), (kpe, C_KPE + MLA_NOPE, MLA_ROPE, 1.0),
             (gq, C_G, GQA_HEADS * GQA_DIM, GQA_DIM ** -0.5 * LOG2_E)]
    dst = C_G + GQA_HEADS * GQA_DIM
    for base in (gk, gv):
        for hd in range(GQA_KV_HEADS):
            for _ in range(2):
                moves.append((base + hd * GQA_DIM, dst, GQA_DIM, 1.0))
                dst += GQA_DIM
    assert dst == C_R
    for s, width, scale in ((rq, RET_HEADS * RET_QK, 1.0), (rk, RET_HEADS * RET_QK, RET_QK ** -0.5),
                            (rv, RET_HEADS * RET_V, 1.0), (rg, RET_HEADS * RET_V, 1.0), (gates, 3 * d, 1.0)):
        moves.append((s, dst, width, scale))
        dst += width
    assert dst == W_COLS
    return moves


def _pack_kernel(w_ref, o_ref, *, moves):
    o_ref[:, C_KPE:C_G] = jnp.zeros((o_ref.shape[0], C_G - C_KPE), o_ref.dtype)
    for s, t, width, scale in moves:
        piece = w_ref[:, s:s + width]
        o_ref[:, t:t + width] = (piece if scale == 1.0 else piece * scale).astype(o_ref.dtype)


def _pack_w_in(w_in):
    n_layers, d, n_cols = w_in.shape
    rows = 256
    return pl.pallas_call(
        functools.partial(_pack_kernel, moves=_w_in_moves(d)),
        grid=(n_layers, d // rows),
        in_specs=[pl.BlockSpec((None, rows, n_cols), lambda l, i: (l, i, 0))],
        out_specs=pl.BlockSpec((None, rows, W_COLS), lambda l, i: (l, i, 0)),
        out_shape=jax.ShapeDtypeStruct((n_layers, d, W_COLS), BF16),
        compiler_params=_cparams(("arbitrary", "arbitrary")),
        name="pack_w_in",
    )(w_in)


def _pack_mla_up(w_uq, w_ukv):
    r = w_uq.shape[0]
    dq = MLA_NOPE + MLA_ROPE
    wq = jnp.pad(w_uq.reshape(r, MLA_HEADS, dq), ((0, 0), (0, 0), (0, LANES - dq))).reshape(r, MLA_HEADS * LANES)
    kv = w_ukv.reshape(r, MLA_HEADS, MLA_NOPE + MLA_V)
    wk = jnp.pad(kv[:, :, :MLA_NOPE], ((0, 0), (0, 0), (0, LANES - MLA_NOPE))).reshape(r, MLA_HEADS * LANES)
    wv = kv[:, :, MLA_NOPE:]
    zeros = jnp.zeros_like(wv)
    even = jnp.concatenate([wv, zeros], axis=2)
    odd = jnp.concatenate([zeros, wv], axis=2)
    wv = jnp.where((jnp.arange(MLA_HEADS) % 2 == 0)[None, :, None], even, odd).reshape(r, MLA_HEADS * LANES)
    return wq.astype(BF16), wk.astype(BF16), wv.astype(BF16)


def kernel(x, c, ctx, c_ctx, ada_w, ada_b, norm_mix_pre, norm_mix_post, norm_ffn_pre, norm_ffn_post, w_in, mla_q_norm, mla_w_uq, mla_kv_norm, mla_w_ukv, gqa_sink, ret_decay_fwd, ret_decay_bwd, w_br_mla, w_br_gqa, w_br_ret, w_out, router_w, router_bias, exp_w_gate, exp_w_up, exp_w_down, shared_w_gate, shared_w_up, shared_w_down):
    n_batch, seq, d = x.shape
    n_ctx = ctx.shape[1]
    depth = ada_w.shape[0]
    n_lat_rows = n_batch * seq
    assert seq % TM == 0 and (n_batch * n_ctx) % TM == 0 and seq % ATT_TQ == 0 and seq % n_ctx == 0
    assert n_batch < MOD_ROWS and seq % GRID_W == 0 and d == D_MODEL

    cond = jnp.zeros((MOD_ROWS, d), F32).at[:n_batch].set(c).at[n_batch].set(c_ctx)
    mods_all = _adaln(cond, ada_w, ada_b)
    rope = _rope_tables(seq)
    h = (x.reshape(n_lat_rows, d), ctx.reshape(n_batch * n_ctx, d), 0)
    n_all_rows = n_lat_rows + n_batch * n_ctx
    row = lambda p: p.reshape(1, -1)
    dims = dict(n_batch=n_batch, seq=seq)

    w_in_packed = _pack_w_in(w_in)

    def inproj_args(l):
        return (mods_all[l], row(norm_mix_pre[l]), w_in_packed, rope, row(mla_q_norm[l]),
                row(mla_kv_norm[l]), *_pack_mla_up(mla_w_uq[l], mla_w_ukv[l]))

    projected = _inproj(h[0], h[1], inproj_args(0), layer=0, **dims)
    for l in range(depth):
        last = l == depth - 1
        mods = mods_all[l]
        mq, mk, mv, gqa, ret, gates = projected
        a = _mla_attention(mq, mk, mv, ctx=n_ctx, with_ctx_queries=not last, **dims)
        sink_tab = jnp.broadcast_to(gqa_sink[l].astype(F32)[:, None] * LOG2_E, (GQA_HEADS, LANES))
        w = _window_attention(gqa, sink_tab, ctx=n_ctx, with_ctx_queries=not last, **dims)
        lg = jnp.concatenate([jax.nn.log_sigmoid(ret_decay_fwd[l].astype(F32)),
                              jax.nn.log_sigmoid(ret_decay_bwd[l].astype(F32))])
        o_f, o_b = _retention(ret, jnp.broadcast_to(lg[:, None], (2 * RET_HEADS, LANES)), ctx=n_ctx, **dims)
        n_rows = n_lat_rows if last else n_all_rows
        h1, v, vp = _merge(a, w, o_f, o_b, ret, gates, h, mods, row(norm_mix_post[l]), row(norm_ffn_pre[l]),
                           w_br_mla[l].astype(BF16), w_br_gqa[l].astype(BF16), w_br_ret[l].astype(BF16),
                           w_out[l].astype(BF16), n_rows=n_rows, **dims)
        yg, gate_w = _moe_routed(v, vp, l, router_w[l].T.astype(BF16), router_bias[l].astype(F32).reshape(-1, 1),
                                 exp_w_gate, exp_w_up, exp_w_down)
        moe_args = (yg, gate_w, v, shared_w_gate[l].astype(BF16), shared_w_up[l].astype(BF16),
                    shared_w_down[l].astype(BF16), h1, mods, row(norm_ffn_post[l]))
        if last:
            out = _moe_out(moe_args, **dims)
        else:
            stream, *projected = _moe_out_inproj(moe_args, inproj_args(l + 1), layer=l + 1, **dims)
            h = (stream, stream, n_lat_rows // TM)
    return out[:n_lat_rows].reshape(n_batch, seq, d)
```

```python
import functools

import numpy as np
import jax
import jax.numpy as jnp
from jax import lax
from jax.experimental import pallas as pl
from jax.experimental.pallas import tpu as pltpu
from jax.experimental.pallas import tpu_sc as plsc

F32 = jnp.float32
BF16 = jnp.bfloat16

GRID_W = 64
ROPE_BASE = 10000.0
NORM_EPS = 1e-6
NEG_INF = -1e30
LOG2_E = 1.4426950408889634
N_MOD = 6
MLA_HEADS, MLA_NOPE, MLA_ROPE, MLA_V = 8, 64, 32, 64
MLA_Q_LORA, MLA_KV_LORA = 256, 256
GQA_HEADS, GQA_KV_HEADS, GQA_DIM, WINDOW = 8, 2, 64, 128
RET_HEADS, RET_QK, RET_V, RET_CHUNK = 4, 64, 128, 128
N_EXPERTS, N_EXPERT_GROUPS, TOPK_GROUPS, TOP_K = 64, 8, 4, 8
EXPERTS_PER_GROUP = N_EXPERTS // N_EXPERT_GROUPS
ROUTED_SCALE = 2.5

LANES = 128
TM = 512
ATT_TQ = 512
WIN_TQ = 256
RET_TILE = 256
MOD_ROWS = 8
V7X_VMEM_LIMIT = 56 * 1024 * 1024

D_MODEL = 1024
C_CQ = 0
C_CKV = C_CQ + MLA_Q_LORA
C_KPE = C_CKV + MLA_KV_LORA
C_G = C_KPE + LANES
C_R = C_G + GQA_HEADS * GQA_DIM + 4 * GQA_KV_HEADS * GQA_DIM
C_GATE = C_R + 2 * RET_HEADS * RET_QK + 2 * RET_HEADS * RET_V
W_COLS = C_GATE + 3 * D_MODEL


def _cparams(sem):
    return pltpu.CompilerParams(dimension_semantics=sem, vmem_limit_bytes=V7X_VMEM_LIMIT)


def _rms(x, g):
    return x * lax.rsqrt(jnp.mean(x * x, axis=-1, keepdims=True) + NORM_EPS) * g


def _sigmoid(x):
    return 0.5 * jnp.tanh(0.5 * x) + 0.5


def _dot(a, b):
    return jnp.dot(a, b, preferred_element_type=F32)


def _dot_nt(a, b):
    return lax.dot_general(a, b, (((1,), (1,)), ((), ())), preferred_element_type=F32)


def _dot_tn(a, b):
    return lax.dot_general(a, b, (((0,), (0,)), ((), ())), preferred_element_type=F32)


def _rope(x, cos, sin, half):
    n = x.shape[-1]
    reps = n // LANES
    if reps > 1:
        cos = jnp.concatenate([cos] * reps, axis=1)
        sin = jnp.concatenate([sin] * reps, axis=1)
    lane = lax.broadcasted_iota(jnp.int32, x.shape, 1)
    up = pltpu.roll(x, half, 1)
    dn = pltpu.roll(x, n - half, 1)
    partner = jnp.where((lane & (2 * half - 1)) < half, dn, up)
    return x * cos + partner * sin


def _lane_lo(shape):
    return (lax.broadcasted_iota(jnp.int32, shape, 1) & (LANES - 1)) < (LANES // 2)


def _pack_halves(x):
    n = x.shape[1] // 2
    bits = lambda t: lax.bitcast_convert_type(t.astype(BF16).astype(F32), jnp.uint32)
    return (bits(x[:, :n]) >> 16) | bits(x[:, n:])


def _unpack_halves(p):
    lo = lax.bitcast_convert_type(p << 16, F32)
    hi = lax.bitcast_convert_type(p & jnp.uint32(0xFFFF0000), F32)
    return lo, hi


def _ada_kernel(c_ref, w_ref, b_ref, o_ref):
    c = c_ref[...]
    s = c * _sigmoid(c)
    o_ref[...] = _dot(s.astype(BF16), w_ref[...].astype(BF16)) + b_ref[...]


def _adaln(cond, ada_w, ada_b):
    n_layers, d, n = ada_w.shape
    tn = 1024
    return pl.pallas_call(
        _ada_kernel,
        grid=(n_layers, n // tn),
        in_specs=[pl.BlockSpec((MOD_ROWS, d), lambda l, j: (0, 0)),
                  pl.BlockSpec((None, d, tn), lambda l, j: (l, 0, j)),
                  pl.BlockSpec((None, 1, tn), lambda l, j: (l, 0, j))],
        out_specs=pl.BlockSpec((None, MOD_ROWS, tn), lambda l, j: (l, 0, j)),
        out_shape=jax.ShapeDtypeStruct((n_layers, MOD_ROWS, n), F32),
        compiler_params=_cparams(("arbitrary", "arbitrary")),
        name="adaln",
    )(cond, ada_w, ada_b.reshape(n_layers, 1, n))


def _inproj_tile(h, mod_ref, gpre_ref, w_ref, rope_ref, qn_ref, kvn_ref, wuq_ref, wuk_ref, wuv_ref,
                 mq_ref, mk_ref, mv_ref, gqa_ref, ret_ref, gate_ref, *, tiles_per_batch, n_batch, d):
    i = pl.program_id(0)
    bi = jnp.minimum(i // tiles_per_batch, n_batch)
    sh = mod_ref[pl.ds(bi, 1), 0:d]
    sc = mod_ref[pl.ds(bi, 1), d:2 * d]
    u = (_rms(h, gpre_ref[...]) * (1.0 + sc) + sh).astype(BF16)

    cos64 = rope_ref[:, 0:LANES]
    sin64 = rope_ref[:, LANES:2 * LANES]
    cospe = rope_ref[:, 2 * LANES:3 * LANES]
    sinpe = rope_ref[:, 3 * LANES:4 * LANES]

    c = _dot(u, w_ref[:, C_CQ:C_G])
    kpe = _rope(c[:, C_KPE:C_G], cospe, sinpe, MLA_ROPE // 4)
    qn = _rms(c[:, C_CQ:C_CKV], qn_ref[...]).astype(BF16)
    q = _rope(_dot(qn, wuq_ref[...]), cospe, sinpe, MLA_ROPE // 4)
    mq_ref[...] = (q * ((MLA_NOPE + MLA_ROPE) ** -0.5 * LOG2_E)).astype(mq_ref.dtype)
    kvn = _rms(c[:, C_CKV:C_KPE], kvn_ref[...]).astype(BF16)
    k = _dot(kvn, wuk_ref[...]) + jnp.concatenate([kpe] * MLA_HEADS, axis=1)
    mk_ref[...] = k.astype(mk_ref.dtype)
    v = _dot(kvn, wuv_ref[...])
    lane = lax.broadcasted_iota(jnp.int32, v.shape, 1)
    value_lane = ((lane & (LANES - 1)) < MLA_V) == (((lane >> (LANES.bit_length() - 1)) & 1) == 0)
    mv_ref[...] = jnp.where(value_lane, v, 1.0).astype(mv_ref.dtype)

    g = _dot(u, w_ref[:, C_G:C_R])
    n_qk = GQA_HEADS * GQA_DIM + 2 * GQA_KV_HEADS * GQA_DIM
    gqa_ref[:, 0:n_qk] = _rope(g[:, 0:n_qk], cos64, sin64, GQA_DIM // 4).astype(gqa_ref.dtype)
    gqa_ref[:, n_qk:] = g[:, n_qk:].astype(gqa_ref.dtype)

    r = _dot(u, w_ref[:, C_R:C_GATE])
    n_qk = 2 * RET_HEADS * RET_QK
    ret_ref[:, 0:n_qk] = _rope(r[:, 0:n_qk], cos64, sin64, RET_QK // 4).astype(ret_ref.dtype)
    ret_ref[:, n_qk:] = r[:, n_qk:].astype(ret_ref.dtype)

    gate_ref[...] = _dot(u, w_ref[:, C_GATE:W_COLS]).astype(gate_ref.dtype)


def _inproj_kernel(hl_ref, hc_ref, *refs, tiles_per_batch, n_batch, d):
    latent = pl.program_id(0) < tiles_per_batch * n_batch
    _inproj_tile(jnp.where(latent, hl_ref[...], hc_ref[...]), *refs,
                 tiles_per_batch=tiles_per_batch, n_batch=n_batch, d=d)


def _inproj_specs(t, d, mods, gpre, w_all, rope, qn, kvn, wuq, wuk, wuv, *, layer, tm, n_batch, seq):
    tiles_per_batch = seq // tm
    n_lat_tiles = n_batch * tiles_per_batch
    const = lambda i: (0, 0)
    rows = lambda i: (i, 0)
    rope_idx = lambda i: (jnp.where(i < n_lat_tiles, i % tiles_per_batch, tiles_per_batch), 0)
    once = dict(pipeline_mode=pl.Buffered(1))
    hq = MLA_HEADS * LANES
    outs = [jax.ShapeDtypeStruct((t, hq), BF16), jax.ShapeDtypeStruct((t, hq), BF16),
            jax.ShapeDtypeStruct((t, hq), BF16),
            jax.ShapeDtypeStruct((t, C_R - C_G), BF16),
            jax.ShapeDtypeStruct((t, C_GATE - C_R), F32),
            jax.ShapeDtypeStruct((t, W_COLS - C_GATE), BF16)]
    in_specs = [pl.BlockSpec(mods.shape, const),
                pl.BlockSpec((1, d), const),
                pl.BlockSpec((None,) + w_all.shape[1:], lambda i: (layer, 0, 0), **once),
                pl.BlockSpec((tm, 4 * LANES), rope_idx),
                pl.BlockSpec(qn.shape, const), pl.BlockSpec(kvn.shape, const),
                pl.BlockSpec(wuq.shape, const, **once), pl.BlockSpec(wuk.shape, const, **once),
                pl.BlockSpec(wuv.shape, const, **once)]
    return in_specs, [pl.BlockSpec((tm, o.shape[1]), rows) for o in outs], outs


def _inproj(h_lat, h_ctx, inproj_args, *, layer, n_batch, seq):
    d = h_lat.shape[1]
    t = h_lat.shape[0] + h_ctx.shape[0]
    n_lat_tiles = h_lat.shape[0] // TM
    in_specs, out_specs, out_shape = _inproj_specs(t, d, *inproj_args, layer=layer, tm=TM, n_batch=n_batch, seq=seq)
    return pl.pallas_call(
        functools.partial(_inproj_kernel, tiles_per_batch=seq // TM, n_batch=n_batch, d=d),
        grid=(t // TM,),
        in_specs=[pl.BlockSpec((TM, d), lambda i: (jnp.minimum(i, n_lat_tiles - 1), 0)),
                  pl.BlockSpec((TM, d), lambda i: (jnp.maximum(i - n_lat_tiles, 0), 0))] + in_specs,
        out_specs=out_specs,
        out_shape=out_shape,
        compiler_params=_cparams(("arbitrary",)),
        name="inproj",
    )(h_lat, h_ctx, *inproj_args)


MLA_HEADS_PER_STEP = 4


def _mla_kernel(q_hbm, kl_ref, kc_ref, vl_ref, vc_ref, o_hbm, s_ref, p_ref, qbuf_ref, obuf_ref, qsem, osem,
                *, with_lat, n_q_tiles, q_tile0):
    n_ctx = kc_ref.shape[0]
    tq = qbuf_ref.shape[1]
    q_cols = pl.multiple_of(pl.program_id(1) * qbuf_ref.shape[2], LANES)
    o_cols = pl.multiple_of(pl.program_id(1) * obuf_ref.shape[2], LANES)
    tile0 = pl.program_id(0) * n_q_tiles

    def q_copy(i):
        rows = pl.ds(pl.multiple_of((q_tile0 + tile0 + i) * tq, tq), tq)
        return pltpu.make_async_copy(q_hbm.at[rows, pl.ds(q_cols, qbuf_ref.shape[2])], qbuf_ref.at[i % 2],
                                     qsem.at[i % 2])

    def o_copy(i):
        rows = pl.ds(pl.multiple_of((tile0 + i) * tq, tq), tq)
        return pltpu.make_async_copy(obuf_ref.at[i % 2], o_hbm.at[rows, pl.ds(o_cols, obuf_ref.shape[2])],
                                     osem.at[i % 2])

    def body(q_ref, o_ref):
        n_keys = n_ctx + (kl_ref.shape[0] if with_lat else 0)

        def scores(h):
            sl = slice(h * LANES, (h + 1) * LANES)
            s_ref[h % 2, :, 0:n_ctx] = _dot_nt(q_ref[:, sl], kc_ref[:, sl])
            if with_lat:
                s_ref[h % 2, :, n_ctx:n_keys] = _dot_nt(q_ref[:, sl], kl_ref[:, sl])

        def probs(h):
            s = s_ref[h % 2, :, 0:n_keys]
            p_ref[h % 2, :, 0:n_keys] = jnp.exp2(s - jnp.max(s, axis=-1, keepdims=True)).astype(BF16)

        def weighted(h):
            sl = slice(h * LANES, (h + 1) * LANES)
            o = _dot(p_ref[h % 2, :, 0:n_ctx], vc_ref[:, sl])
            if with_lat:
                o = o + _dot(p_ref[h % 2, :, n_ctx:n_keys], vl_ref[:, sl])
            return o / pltpu.roll(o, LANES // 2, 1)

        outs = [None] * MLA_HEADS_PER_STEP
        scores(0)
        for h in range(MLA_HEADS_PER_STEP):
            if h + 1 < MLA_HEADS_PER_STEP:
                scores(h + 1)
            probs(h)
            outs[h] = weighted(h)
        for pr in range(MLA_HEADS_PER_STEP // 2):
            even, odd = outs[2 * pr], outs[2 * pr + 1]
            o_ref[:, pr * LANES:(pr + 1) * LANES] = jnp.where(_lane_lo(even.shape), even, odd).astype(o_ref.dtype)

    q_copy(0).start()

    def tile(i, carry):
        @pl.when(i + 1 < n_q_tiles)
        def _():
            q_copy(i + 1).start()

        q_copy(i).wait()

        @pl.when(i >= 2)
        def _():
            o_copy(i - 2).wait()

        body(qbuf_ref.at[i % 2], obuf_ref.at[i % 2])
        o_copy(i).start()
        return carry

    lax.fori_loop(0, n_q_tiles, tile, 0)
    for back in range(min(2, n_q_tiles), 0, -1):
        o_copy(n_q_tiles - back).wait()


def _mla_call(mq, mk, mv, *, n_batch, seq, ctx, latent_queries):
    hps = MLA_HEADS_PER_STEP
    ctx_blk0 = n_batch * seq // ctx
    tq = ATT_TQ if latent_queries else ctx
    nq = seq // tq if latent_queries else 1
    q_tile0 = 0 if latent_queries else n_batch * seq // tq
    ctx_idx = lambda b, g: (ctx_blk0 + b, g)
    lat_idx = (lambda b, g: (b, g)) if latent_queries else ctx_idx
    n_lat = seq if latent_queries else ctx
    n_keys = ctx + (seq if latent_queries else 0)
    in_specs = [pl.BlockSpec(memory_space=pl.ANY),
                pl.BlockSpec((n_lat, hps * LANES), lat_idx), pl.BlockSpec((ctx, hps * LANES), ctx_idx),
                pl.BlockSpec((n_lat, hps * LANES), lat_idx), pl.BlockSpec((ctx, hps * LANES), ctx_idx)]
    return pl.pallas_call(
        functools.partial(_mla_kernel, with_lat=latent_queries, n_q_tiles=nq, q_tile0=q_tile0),
        grid=(n_batch, MLA_HEADS // hps),
        in_specs=in_specs,
        out_specs=pl.BlockSpec(memory_space=pl.ANY),
        out_shape=jax.ShapeDtypeStruct((n_batch * nq * tq, MLA_HEADS * MLA_V), BF16),
        scratch_shapes=[pltpu.VMEM((2, tq, n_keys), F32), pltpu.VMEM((2, tq, n_keys), BF16),
                        pltpu.VMEM((2, tq, hps * LANES), BF16), pltpu.VMEM((2, tq, hps * MLA_V), BF16),
                        pltpu.SemaphoreType.DMA((2,)), pltpu.SemaphoreType.DMA((2,))],
        compiler_params=_cparams(("arbitrary", "arbitrary")),
        name="mla_attn" if latent_queries else "mla_attn_ctx",
    )(mq, mk, mk, mv, mv)


def _mla_attention(mq, mk, mv, *, n_batch, seq, ctx, with_ctx_queries):
    dims = dict(n_batch=n_batch, seq=seq, ctx=ctx)
    lat = _mla_call(mq, mk, mv, latent_queries=True, **dims)
    return lat, (_mla_call(mq, mk, mv, latent_queries=False, **dims) if with_ctx_queries else lat)


def _win_kernel(q_ref, kp_ref, kcur_ref, kn_ref, vp_ref, vcur_ref, vn_ref, kc_ref, vc_ref, sink_ref, o_ref,
                s_ref, p_ref, *, seq, with_lat):
    i = pl.program_id(1)
    tq = q_ref.shape[0]
    group = GQA_HEADS // GQA_KV_HEADS

    def body(with_lat):
        n_ctx = kc_ref.shape[0]
        n_keys = n_ctx + (tq + 2 * WINDOW if with_lat else 0)
        if with_lat:
            q_pos = i * tq + lax.broadcasted_iota(jnp.int32, (tq, n_keys), 0)
            k_pos = i * tq - WINDOW - n_ctx + lax.broadcasted_iota(jnp.int32, (tq, n_keys), 1)
            in_band = (jnp.abs(q_pos - k_pos) <= WINDOW) & (k_pos >= 0) & (k_pos < seq)
            valid = in_band | (lax.broadcasted_iota(jnp.int32, (tq, n_keys), 1) < n_ctx)
        lo = _lane_lo((tq, LANES))
        lo_k = _lane_lo((n_keys, LANES))
        keys, values = [], []
        for kv in range(GQA_KV_HEADS):
            sl = slice(kv * LANES, (kv + 1) * LANES)
            if with_lat:
                k_all = jnp.concatenate([kc_ref[:, sl], kp_ref[:, sl], kcur_ref[:, sl], kn_ref[:, sl]], axis=0)
                v_all = jnp.concatenate([vc_ref[:, sl], vp_ref[:, sl], vcur_ref[:, sl], vn_ref[:, sl]], axis=0)
            else:
                k_all, v_all = kc_ref[:, sl], vc_ref[:, sl]
            keys.append(k_all)
            one = jnp.ones_like(v_all)
            values.append((jnp.where(lo_k, v_all, one), jnp.where(lo_k, one, v_all)))

        def scores(hd):
            kv, pair = hd // group, hd // 2
            qp = q_ref[:, pair * LANES:(pair + 1) * LANES]
            qm = jnp.where(lo if hd % 2 == 0 else jnp.logical_not(lo), qp, jnp.zeros_like(qp))
            s = _dot_nt(qm, keys[kv])
            s_ref[hd % 2, :, 0:n_keys] = jnp.where(valid, s, NEG_INF) if with_lat else s

        def probs(hd):
            s = s_ref[hd % 2, :, 0:n_keys]
            m = jnp.maximum(jnp.max(s, axis=-1, keepdims=True), sink_ref[hd:hd + 1, 0:1])
            p_ref[hd % 2, :, 0:n_keys] = jnp.exp2(s - m).astype(BF16)
            return jnp.exp2(sink_ref[hd:hd + 1, 0:1] - m)

        def weighted(hd, sink_term):
            o = _dot(p_ref[hd % 2, :, 0:n_keys], values[hd // group][hd % 2])
            return o / (pltpu.roll(o, LANES // 2, 1) + sink_term)

        outs = [None] * GQA_HEADS
        scores(0)
        for hd in range(GQA_HEADS):
            if hd + 1 < GQA_HEADS:
                scores(hd + 1)
            outs[hd] = weighted(hd, probs(hd))
        for pair in range(GQA_HEADS // 2):
            o_ref[:, pair * LANES:(pair + 1) * LANES] = jnp.where(
                lo, outs[2 * pair], outs[2 * pair + 1]).astype(o_ref.dtype)

    body(with_lat)


def _win_call(gqa, sink_tab, *, n_batch, seq, ctx, latent_queries):
    tq = WIN_TQ if latent_queries else ctx
    nq = seq // tq if latent_queries else 1
    q_blk0 = 0 if latent_queries else n_batch * seq // tq
    per_tile = tq // WINDOW
    n_win_blocks = seq // WINDOW
    ctx_blk0 = n_batch * seq // ctx
    nqk = GQA_HEADS * GQA_DIM
    kw = 2 * GQA_KV_HEADS * GQA_DIM
    k_col, v_col = nqk // kw, nqk // kw + 1
    q_idx = lambda b, i: (q_blk0 + b * nq + i, 0)
    cidx = lambda col: (lambda b, i: (ctx_blk0 + b, col))
    if latent_queries:
        cur = lambda col: (lambda b, i: (b * nq + i, col))
        prev = lambda col: (lambda b, i: (b * n_win_blocks + jnp.maximum(per_tile * i - 1, 0), col))
        nxt = lambda col: (lambda b, i: (b * n_win_blocks + jnp.minimum(per_tile * (i + 1), n_win_blocks - 1), col))
        band = [((WINDOW, kw), prev), ((tq, kw), cur), ((WINDOW, kw), nxt)]
    else:
        band = [((ctx, kw), cidx)] * 3
    n_keys = ctx + (tq + 2 * WINDOW if latent_queries else 0)
    in_specs = ([pl.BlockSpec((tq, nqk), q_idx)]
                + [pl.BlockSpec(shape, idx(k_col)) for shape, idx in band]
                + [pl.BlockSpec(shape, idx(v_col)) for shape, idx in band]
                + [pl.BlockSpec((ctx, kw), cidx(k_col)), pl.BlockSpec((ctx, kw), cidx(v_col)),
                   pl.BlockSpec(sink_tab.shape, lambda b, i: (0, 0))])
    return pl.pallas_call(
        functools.partial(_win_kernel, seq=seq, with_lat=latent_queries),
        grid=(n_batch, nq),
        in_specs=in_specs,
        out_specs=pl.BlockSpec((tq, nqk), lambda b, i: (b * nq + i, 0)),
        out_shape=jax.ShapeDtypeStruct((n_batch * nq * tq, nqk), BF16),
        scratch_shapes=[pltpu.VMEM((2, tq, n_keys), F32), pltpu.VMEM((2, tq, n_keys), BF16)],
        compiler_params=_cparams(("arbitrary", "arbitrary")),
        name="win_attn" if latent_queries else "win_attn_ctx",
    )(*([gqa] * 9 + [sink_tab]))


def _window_attention(gqa, sink_tab, *, n_batch, seq, ctx, with_ctx_queries):
    dims = dict(n_batch=n_batch, seq=seq, ctx=ctx)
    lat = _win_call(gqa, sink_tab, latent_queries=True, **dims)
    return lat, (_win_call(gqa, sink_tab, latent_queries=False, **dims) if with_ctx_queries else lat)


def _ret_kernel(f_ref, b_ref, lg_ref, of_ref, ob_ref, sf_ref, sb_ref, qdec_ref, kdec_ref, cdec_ref, inner_ref):
    @pl.when(pl.program_id(1) == 0)
    def _():
        sf_ref[...] = jnp.zeros_like(sf_ref)
        sb_ref[...] = jnp.zeros_like(sb_ref)

    L = f_ref.shape[0]
    lo = _lane_lo((L, LANES))
    srow_lo = lax.broadcasted_iota(jnp.int32, (LANES, LANES), 0) < RET_QK
    nq = RET_HEADS * RET_QK
    n_pairs = RET_HEADS // 2

    @pl.when(pl.program_id(1) == 0)
    def _():
        ii = lax.broadcasted_iota(jnp.int32, (L, L), 0)
        jj = lax.broadcasted_iota(jnp.int32, (L, L), 1)
        row = lax.broadcasted_iota(jnp.int32, (L, LANES), 0).astype(F32)
        for direction, forward in enumerate((True, False)):
            dist = ii - jj if forward else jj - ii
            distf = jnp.maximum(dist, 0).astype(F32)
            for pr in range(n_pairs):
                r0 = direction * RET_HEADS + 2 * pr
                lg = [lg_ref[r0 + e:r0 + e + 1, :] for e in range(2)]
                lg_lane = jnp.where(lo, lg[0], lg[1])
                qdec_ref[direction * n_pairs + pr] = jnp.exp(lg_lane * ((row + 1.0) if forward else (L - row)))
                kdec_ref[direction * n_pairs + pr] = jnp.exp(lg_lane * ((L - 1.0 - row) if forward else row))
                cdec_ref[direction * n_pairs + pr] = jnp.where(srow_lo, jnp.exp(lg[0] * float(L)),
                                                               jnp.exp(lg[1] * float(L)))
                for e in range(2):
                    inner_ref[r0 + e] = jnp.where(dist >= 0, jnp.exp(lg[e][:, 0:1] * distf), 0.0)

    def scan_chunk(x_ref, o_ref, s_ref, direction):
        for pr in range(n_pairs):
            q = x_ref[:, pr * LANES:(pr + 1) * LANES].astype(F32)
            k = x_ref[:, nq + pr * LANES:nq + (pr + 1) * LANES].astype(F32)
            qd = q * qdec_ref[direction * n_pairs + pr]
            kdb = (k * kdec_ref[direction * n_pairs + pr]).astype(BF16)
            kb = k.astype(BF16)
            state = s_ref[pr]
            state_b = state.astype(BF16)
            upd = []
            for e in range(2):
                hd = 2 * pr + e
                keep = lo if e == 0 else jnp.logical_not(lo)
                v = x_ref[:, 2 * nq + hd * RET_V:2 * nq + (hd + 1) * RET_V].astype(BF16)
                attn = _dot_nt(jnp.where(keep, q, 0.0).astype(BF16), kb) * inner_ref[direction * RET_HEADS + hd]
                o = _dot(attn.astype(BF16), v) + _dot(jnp.where(keep, qd, 0.0).astype(BF16), state_b)
                o_ref[:, hd * RET_V:(hd + 1) * RET_V] = o
                upd.append(_dot_tn(kdb, v))
            s_ref[pr] = state * cdec_ref[direction * n_pairs + pr] + jnp.where(srow_lo, upd[0], upd[1])

    scan_chunk(f_ref, of_ref, sf_ref, 0)
    scan_chunk(b_ref, ob_ref, sb_ref, 1)


def _retention(ret, lg_tab, *, n_batch, seq, ctx):
    t = ret.shape[0]
    L = RET_TILE
    assert seq % L == 0 and ctx % L == 0
    n_lat, n_ctx = seq // L, ctx // L
    ctx0 = n_batch * n_lat
    width = 2 * RET_HEADS * RET_QK + RET_HEADS * RET_V
    fwd = lambda b, s: (jnp.where(s < n_ctx, ctx0 + b * n_ctx + s, b * n_lat + s - n_ctx), 0)
    bwd = lambda b, s: (jnp.where(s < n_ctx, ctx0 + b * n_ctx + n_ctx - 1 - s, b * n_lat + n_lat - 1 - (s - n_ctx)), 0)
    out = jax.ShapeDtypeStruct((t, RET_HEADS * RET_V), F32)
    return pl.pallas_call(
        _ret_kernel,
        grid=(n_batch, n_lat + n_ctx),
        in_specs=[pl.BlockSpec((L, width), fwd), pl.BlockSpec((L, width), bwd),
                  pl.BlockSpec(lg_tab.shape, lambda b, s: (0, 0))],
        out_specs=[pl.BlockSpec((L, RET_HEADS * RET_V), fwd), pl.BlockSpec((L, RET_HEADS * RET_V), bwd)],
        out_shape=[out, out],
        scratch_shapes=[pltpu.VMEM((RET_HEADS // 2, LANES, RET_V), F32),
                        pltpu.VMEM((RET_HEADS // 2, LANES, RET_V), F32),
                        pltpu.VMEM((RET_HEADS, L, LANES), F32), pltpu.VMEM((RET_HEADS, L, LANES), F32),
                        pltpu.VMEM((RET_HEADS, LANES, RET_V), F32), pltpu.VMEM((2 * RET_HEADS, L, L), F32)],
        compiler_params=_cparams(("arbitrary", "arbitrary")),
        name="retention",
    )(ret, ret, lg_tab)


def _merge_kernel(al_ref, ac_ref, wl_ref, wc_ref, of_ref, ob_ref, rg_ref, gt_ref, hl_ref, hc_ref, mod_ref,
                  gpost_ref, gffn_ref, wa_ref, ww_ref, wr_ref, wo_ref, h1_ref, v_ref, vp_ref,
                  *, tiles_per_batch, n_batch, d):
    i = pl.program_id(0)
    bi = jnp.minimum(i // tiles_per_batch, n_batch)
    latent = i < tiles_per_batch * n_batch
    a_tile = jnp.where(latent, al_ref[...], ac_ref[...])
    w_tile = jnp.where(latent, wl_ref[...], wc_ref[...])
    h_tile = jnp.where(latent, hl_ref[...], hc_ref[...])
    o = of_ref[...] + ob_ref[...]
    normed = []
    for hd in range(RET_HEADS):
        oh = o[:, hd * RET_V:(hd + 1) * RET_V]
        dev = oh - jnp.mean(oh, axis=-1, keepdims=True)
        normed.append(dev * lax.rsqrt(jnp.mean(dev * dev, axis=-1, keepdims=True) + NORM_EPS))
    g = rg_ref[...].astype(F32)
    r = (g * _sigmoid(g)) * jnp.concatenate(normed, axis=1)
    y = (_sigmoid(gt_ref[:, 0:d].astype(F32)) * _dot(a_tile, wa_ref[...])
         + _sigmoid(gt_ref[:, d:2 * d].astype(F32)) * _dot(w_tile, ww_ref[...])
         + _sigmoid(gt_ref[:, 2 * d:3 * d].astype(F32)) * _dot(r.astype(BF16), wr_ref[...]))
    z = _dot(y.astype(BF16), wo_ref[...])
    g1 = mod_ref[pl.ds(bi, 1), 2 * d:3 * d]
    sh2 = mod_ref[pl.ds(bi, 1), 3 * d:4 * d]
    sc2 = mod_ref[pl.ds(bi, 1), 4 * d:5 * d]
    h1 = h_tile + g1 * _rms(z, gpost_ref[...])
    h1_ref[...] = h1
    v = _rms(h1, gffn_ref[...]) * (1.0 + sc2) + sh2
    v_ref[...] = v.astype(v_ref.dtype)
    vp_ref[...] = _pack_halves(v)


def _merge(a, w, o_f, o_b, ret, gates, h, mods, gpost, gffn, wa, ww, wr, wo, *, n_rows, n_batch, seq):
    d = h[0].shape[1]
    n_lat_tiles = n_batch * seq // TM
    rows = lambda i: (i, 0)
    lat_rows = lambda i: (jnp.minimum(i, n_lat_tiles - 1), 0)
    ctx_rows = lambda i: (jnp.maximum(i - n_lat_tiles, 0), 0)
    const = lambda i: (0, 0)
    rv = RET_HEADS * RET_V
    rg_col = (2 * RET_HEADS * RET_QK + rv) // rv
    outs = [jax.ShapeDtypeStruct((n_rows, d), F32), jax.ShapeDtypeStruct((n_rows, d), BF16),
            jax.ShapeDtypeStruct((n_rows, d // 2), jnp.uint32)]
    return pl.pallas_call(
        functools.partial(_merge_kernel, tiles_per_batch=seq // TM, n_batch=n_batch, d=d),
        grid=(n_rows // TM,),
        in_specs=[pl.BlockSpec((TM, a[0].shape[1]), lat_rows), pl.BlockSpec((TM, a[1].shape[1]), ctx_rows),
                  pl.BlockSpec((TM, w[0].shape[1]), lat_rows), pl.BlockSpec((TM, w[1].shape[1]), ctx_rows),
                  pl.BlockSpec((TM, rv), rows), pl.BlockSpec((TM, rv), rows),
                  pl.BlockSpec((TM, rv), lambda i: (i, rg_col)),
                  pl.BlockSpec((TM, 3 * d), rows), pl.BlockSpec((TM, d), lat_rows),
                  pl.BlockSpec((TM, d), lambda i: (h[2] + jnp.maximum(i - n_lat_tiles, 0), 0)),
                  pl.BlockSpec(mods.shape, const), pl.BlockSpec((1, d), const), pl.BlockSpec((1, d), const),
                  pl.BlockSpec(wa.shape, const), pl.BlockSpec(ww.shape, const),
                  pl.BlockSpec(wr.shape, const), pl.BlockSpec(wo.shape, const)],
        out_specs=[pl.BlockSpec((TM, o.shape[1]), rows) for o in outs],
        out_shape=outs,
        compiler_params=_cparams(("arbitrary",)),
        name="merge",
    )(a[0], a[1], w[0], w[1], o_f, o_b, ret, gates, h[0], h[1], mods, gpost, gffn, wa, ww, wr, wo)


def _router_kernel(v_ref, rw_ref, rb_ref, eidx_ref, rank_ref, w_ref, cnt_ref, carry_ref):
    @pl.when(pl.program_id(0) == 0)
    def _():
        carry_ref[...] = jnp.zeros_like(carry_ref)

    tm = v_ref.shape[0]
    scores = _sigmoid(_dot_nt(rw_ref[...], v_ref[...]))
    sel = scores + rb_ref[...]
    neg = -jnp.inf
    n_grp, per = N_EXPERT_GROUPS, EXPERTS_PER_GROUP

    sel3 = sel.reshape(n_grp, per, tm)
    member_id = lax.broadcasted_iota(jnp.int32, sel3.shape, 1)
    m1 = jnp.max(sel3, axis=1, keepdims=True)
    i1 = jnp.min(jnp.where(sel3 == m1, member_id, per), axis=1, keepdims=True)
    m2 = jnp.max(jnp.where(member_id == i1, neg, sel3), axis=1, keepdims=True)
    gscore = (m1 + m2).reshape(n_grp, tm)
    gid = lax.broadcasted_iota(jnp.int32, gscore.shape, 0)
    ahead = jnp.zeros(gscore.shape, jnp.int32)
    for gj in range(n_grp):
        other = gscore[gj:gj + 1, :]
        ahead = ahead + jnp.where((other > gscore) | ((other == gscore) & (gid > gj)), 1, 0)
    group_ok = (ahead < TOPK_GROUPS).reshape(n_grp, 1, tm)
    sel = jnp.where(group_ok, sel3, NEG_INF).reshape(N_EXPERTS, tm)

    eid = lax.broadcasted_iota(jnp.int32, sel.shape, 0)
    chosen = jnp.zeros(sel.shape, jnp.bool_)
    picks = []
    for _ in range(TOP_K):
        m = jnp.max(sel, axis=0, keepdims=True)
        idx = jnp.min(jnp.where(sel == m, eid, N_EXPERTS), axis=0, keepdims=True)
        hit = eid == idx
        chosen = chosen | hit
        sel = jnp.where(hit, neg, sel)
        picks.append(idx)
    w = jnp.where(chosen, scores, 0.0)
    gate = ROUTED_SCALE * w / jnp.sum(w, axis=0, keepdims=True)

    member = jnp.where(chosen, 1.0, 0.0)
    earlier = lax.broadcasted_iota(jnp.int32, (tm, tm), 0) < lax.broadcasted_iota(jnp.int32, (tm, tm), 1)
    pos = _dot(member.astype(BF16), jnp.where(earlier, 1.0, 0.0).astype(BF16)) + carry_ref[...]
    for k, idx in enumerate(picks):
        hit = eid == idx
        eidx_ref[k:k + 1, :] = idx
        rank_ref[k:k + 1, :] = jnp.sum(jnp.where(hit, pos, 0.0), axis=0, keepdims=True)
        w_ref[k:k + 1, :] = jnp.sum(jnp.where(hit, gate, 0.0), axis=0, keepdims=True)
    carry_ref[...] += jnp.sum(member, axis=1, keepdims=True)
    cnt_ref[...] = carry_ref[...]


def _router(v, rw_t, rb):
    n_rows, d = v.shape
    cols = lambda i: (0, i)
    const = lambda i: (0, 0)
    outs = [jax.ShapeDtypeStruct((TOP_K, n_rows), jnp.int32), jax.ShapeDtypeStruct((TOP_K, n_rows), F32),
            jax.ShapeDtypeStruct((TOP_K, n_rows), F32), jax.ShapeDtypeStruct((N_EXPERTS, 1), F32)]
    return pl.pallas_call(
        _router_kernel,
        grid=(n_rows // TM,),
        in_specs=[pl.BlockSpec((TM, d), lambda i: (i, 0)), pl.BlockSpec(rw_t.shape, const),
                  pl.BlockSpec(rb.shape, const)],
        out_specs=[pl.BlockSpec((TOP_K, TM), cols), pl.BlockSpec((TOP_K, TM), cols),
                   pl.BlockSpec((TOP_K, TM), cols), pl.BlockSpec((N_EXPERTS, 1), const)],
        out_shape=outs,
        scratch_shapes=[pltpu.VMEM((N_EXPERTS, 1), F32)],
        compiler_params=_cparams(("arbitrary",)),
        name="router",
    )(v, rw_t, rb)


def _slots_kernel(eidx_ref, rank_ref, cnt_ref, slot_ref):
    tm = eidx_ref.shape[1]
    eid = lax.broadcasted_iota(jnp.int32, (N_EXPERTS, tm), 0)
    for k in range(TOP_K):
        before = jnp.sum(jnp.where(eid < eidx_ref[k:k + 1, :], cnt_ref[...], 0.0), axis=0, keepdims=True)
        slot_ref[k:k + 1, :] = (before + rank_ref[k:k + 1, :]).astype(jnp.int32)


def _slots(eidx, rank, cnt):
    n_rows = eidx.shape[1]
    tm = next(c for c in (2048, 1024, 512, 256) if n_rows % c == 0)
    cols = lambda i: (0, i)
    return pl.pallas_call(
        _slots_kernel,
        grid=(n_rows // tm,),
        in_specs=[pl.BlockSpec((TOP_K, tm), cols), pl.BlockSpec((TOP_K, tm), cols),
                  pl.BlockSpec(cnt.shape, lambda i: (0, 0))],
        out_specs=pl.BlockSpec((TOP_K, tm), cols),
        out_shape=jax.ShapeDtypeStruct((TOP_K, n_rows), jnp.int32),
        compiler_params=_cparams(("arbitrary",)),
        name="slots",
    )(eidx, rank, cnt)


SC_WINDOW = 128


def _sc_mesh():
    return plsc.VectorSubcoreMesh(core_axis_name="core", subcore_axis_name="subcore")


def _sc_dispatch(rows, slot_t, n_out):
    width = rows.shape[1]
    n_chunks = slot_t.shape[1] // SC_WINDOW
    info = plsc.get_sparse_core_info()
    n_workers = info.num_cores * info.num_subcores

    @functools.partial(
        pl.kernel, mesh=_sc_mesh(),
        out_type=jax.ShapeDtypeStruct((n_out, width), rows.dtype),
        scratch_types=[pltpu.VMEM((TOP_K, SC_WINDOW), jnp.int32), pltpu.VMEM((SC_WINDOW, width), rows.dtype)],
        name="moe_dispatch")
    def run(rows_hbm, idx_hbm, out_hbm, idx_v, rows_v):
        wid = lax.axis_index("subcore") * info.num_cores + lax.axis_index("core")

        @pl.loop(wid, n_chunks, step=n_workers)
        def _(c):
            r0 = pl.multiple_of(c * SC_WINDOW, SC_WINDOW)
            pltpu.sync_copy(idx_hbm.at[:, pl.ds(r0, SC_WINDOW)], idx_v)
            pltpu.sync_copy(rows_hbm.at[pl.ds(r0, SC_WINDOW)], rows_v)
            for k in range(TOP_K):
                pltpu.sync_copy(rows_v, out_hbm.at[idx_v.at[k]])

    return run(rows, slot_t)


def _sc_collect(rows, slot_t):
    n_picks, n_rows = slot_t.shape
    width = rows.shape[1]
    n_chunks = n_rows // SC_WINDOW
    info = plsc.get_sparse_core_info()
    n_workers = info.num_cores * info.num_subcores

    @functools.partial(
        pl.kernel, mesh=_sc_mesh(),
        out_type=jax.ShapeDtypeStruct((n_picks, n_rows, width), rows.dtype),
        scratch_types=[pltpu.VMEM((TOP_K, SC_WINDOW), jnp.int32), pltpu.VMEM((SC_WINDOW, width), rows.dtype)],
        name="moe_collect")
    def run(rows_hbm, idx_hbm, out_hbm, idx_v, rows_v):
        wid = lax.axis_index("subcore") * info.num_cores + lax.axis_index("core")

        @pl.loop(wid, n_chunks, step=n_workers)
        def _(c):
            r0 = pl.multiple_of(c * SC_WINDOW, SC_WINDOW)
            pltpu.sync_copy(idx_hbm.at[:, pl.ds(r0, SC_WINDOW)], idx_v)
            for k in range(TOP_K):
                pltpu.sync_copy(rows_hbm.at[idx_v.at[k]], rows_v)
                pltpu.sync_copy(rows_v, out_hbm.at[k, pl.ds(r0, SC_WINDOW)])

    return run(rows, slot_t)


EXPERT_TILE = 512


def _work_items(cnt, n_slots):
    counts = cnt[:, 0].astype(jnp.int32)
    ends = jnp.cumsum(counts)
    n_tiles = n_slots // EXPERT_TILE
    bounds = jnp.sort(jnp.concatenate([jnp.arange(n_tiles, dtype=jnp.int32) * EXPERT_TILE, ends - counts]))
    nxt = jnp.concatenate([bounds[1:], jnp.array([n_slots], jnp.int32)])
    tile = jnp.minimum(bounds // EXPERT_TILE, n_tiles - 1)
    expert = jnp.sum((ends[None, :] <= bounds[:, None]).astype(jnp.int32), axis=1)
    expert = jnp.minimum(expert, N_EXPERTS - 1)
    return tile, expert, bounds - tile * EXPERT_TILE, nxt - tile * EXPERT_TILE


XS_RING = 3


def _experts_kernel(tile_ref, exp_ref, lo_ref, hi_ref, xs_hbm, wg_ref, wu_ref, wd_ref, ys_ref,
                    acc_ref, wgb_ref, wub_ref, wdb_ref, xbuf_ref, xsem):
    i = pl.program_id(0)
    n_items = pl.num_programs(0)
    lo, hi = lo_ref[i], hi_ref[i]

    def tile_copy(item):
        slot = lax.rem(item, XS_RING)
        row0 = pl.multiple_of(tile_ref[item] * EXPERT_TILE, EXPERT_TILE)
        return pltpu.make_async_copy(xs_hbm.at[pl.ds(row0, EXPERT_TILE)], xbuf_ref.at[slot], xsem.at[slot])

    @pl.when(i == 0)
    def _():
        for ahead in range(XS_RING - 1):
            pl.when(ahead < n_items)(lambda: tile_copy(ahead).start())

    @pl.when(i + XS_RING - 1 < n_items)
    def _():
        tile_copy(i + XS_RING - 1).start()

    tile_copy(i).wait()
    xs_ref = xbuf_ref.at[lax.rem(i, XS_RING)]

    @pl.when((i == 0) | (exp_ref[i] != exp_ref[jnp.maximum(i - 1, 0)]))
    def _():
        wgb_ref[...] = wg_ref[...].astype(BF16)
        wub_ref[...] = wu_ref[...].astype(BF16)
        wdb_ref[...] = wd_ref[...].astype(BF16)

    def ffn():
        x_lo, x_hi = _unpack_halves(xs_ref[...])
        x_lo, x_hi = x_lo.astype(BF16), x_hi.astype(BF16)
        n = x_lo.shape[1]
        a = _dot(x_lo, wgb_ref[0:n, :]) + _dot(x_hi, wgb_ref[n:, :])
        u = _dot(x_lo, wub_ref[0:n, :]) + _dot(x_hi, wub_ref[n:, :])
        return _dot(((a * _sigmoid(a)) * u).astype(BF16), wdb_ref[...])

    whole = (lo == 0) & (hi == EXPERT_TILE)

    @pl.when(whole)
    def _():
        ys_ref[...] = _pack_halves(ffn())

    @pl.when(jnp.logical_not(whole) & (hi > lo))
    def _():
        y = ffn()
        row = lax.broadcasted_iota(jnp.int32, y.shape, 0)
        y = jnp.where((row >= lo) & (row < hi), y, 0.0)

        @pl.when(lo == 0)
        def _():
            acc_ref[...] = y

        @pl.when((lo > 0) & (hi < EXPERT_TILE))
        def _():
            acc_ref[...] += y

        @pl.when((lo > 0) & (hi == EXPERT_TILE))
        def _():
            ys_ref[...] = _pack_halves(acc_ref[...] + y)


def _experts(xs, items, layer, exp_wg, exp_wu, exp_wd):
    n_slots, half = xs.shape
    d, hid = exp_wg.shape[-2:]
    tile, expert, lo, hi = items
    grid_spec = pltpu.PrefetchScalarGridSpec(
        num_scalar_prefetch=4,
        grid=(tile.shape[0],),
        in_specs=[pl.BlockSpec(memory_space=pl.ANY),
                  pl.BlockSpec((None, None, d, hid), lambda i, t, e, lo, hi: (layer, e[i], 0, 0)),
                  pl.BlockSpec((None, None, d, hid), lambda i, t, e, lo, hi: (layer, e[i], 0, 0)),
                  pl.BlockSpec((None, None, hid, d), lambda i, t, e, lo, hi: (layer, e[i], 0, 0))],
        out_specs=pl.BlockSpec((EXPERT_TILE, half), lambda i, t, e, lo, hi: (t[i], 0)),
        scratch_shapes=[pltpu.VMEM((EXPERT_TILE, d), F32), pltpu.VMEM((d, hid), BF16),
                        pltpu.VMEM((d, hid), BF16), pltpu.VMEM((hid, d), BF16),
                        pltpu.VMEM((XS_RING, EXPERT_TILE, half), jnp.uint32),
                        pltpu.SemaphoreType.DMA((XS_RING,))])
    return pl.pallas_call(
        _experts_kernel,
        grid_spec=grid_spec,
        out_shape=jax.ShapeDtypeStruct((n_slots, half), jnp.uint32),
        compiler_params=_cparams(("arbitrary",)),
        name="experts",
    )(tile, expert, lo, hi, xs, exp_wg, exp_wu, exp_wd)


def _moe_out_tile(yg_ref, w_ref, v_ref, sg_ref, su_ref, sd_ref, h1_ref, mod_ref, gpost_ref,
                  *, tiles_per_batch, n_batch, d):
    i = pl.program_id(0)
    x = v_ref[...]
    a = _dot(x, sg_ref[...])
    f = _dot(((a * _sigmoid(a)) * _dot(x, su_ref[...])).astype(BF16), sd_ref[...])
    n = d // 2
    f_lo, f_hi = f[:, :n], f[:, n:]
    w = w_ref[...]
    for k in range(TOP_K):
        y_lo, y_hi = _unpack_halves(yg_ref[k])
        wk = w[:, k:k + 1]
        f_lo = f_lo + wk * y_lo
        f_hi = f_hi + wk * y_hi
    f = jnp.concatenate([f_lo, f_hi], axis=1)
    bi = jnp.minimum(i // tiles_per_batch, n_batch)
    g2 = mod_ref[pl.ds(bi, 1), 5 * d:6 * d]
    return h1_ref[...] + g2 * _rms(f, gpost_ref[...])


N_MOE_OUT_IN = 9
N_INPROJ_IN = 9


def _moe_out_kernel(*refs, **kw):
    refs[N_MOE_OUT_IN][...] = _moe_out_tile(*refs[:N_MOE_OUT_IN], **kw)


def _moe_out_inproj_kernel(*refs, n_batch, d, tiles_per_batch):
    dims = dict(tiles_per_batch=tiles_per_batch, n_batch=n_batch, d=d)
    h = _moe_out_tile(*refs[:N_MOE_OUT_IN], **dims)
    n_in = N_MOE_OUT_IN + N_INPROJ_IN
    refs[n_in][...] = h
    _inproj_tile(h, *refs[N_MOE_OUT_IN:n_in], *refs[n_in + 1:], **dims)


def _moe_out_specs(yg, w, v, sg, su, sd, h1, mods, gpost, tm):
    d = v.shape[1]
    rows = lambda i: (i, 0)
    const = lambda i: (0, 0)
    once = dict(pipeline_mode=pl.Buffered(1))
    return [pl.BlockSpec((TOP_K, tm, d // 2), lambda i: (0, i, 0)), pl.BlockSpec((tm, TOP_K), rows),
            pl.BlockSpec((tm, d), rows),
            pl.BlockSpec(sg.shape, const, **once), pl.BlockSpec(su.shape, const, **once),
            pl.BlockSpec(sd.shape, const, **once),
            pl.BlockSpec((tm, d), rows), pl.BlockSpec(mods.shape, const), pl.BlockSpec((1, d), const)]


def _moe_out(moe_args, *, n_batch, seq):
    v = moe_args[2]
    n_rows, d = v.shape
    return pl.pallas_call(
        functools.partial(_moe_out_kernel, tiles_per_batch=seq // TM, n_batch=n_batch, d=d),
        grid=(n_rows // TM,),
        in_specs=_moe_out_specs(*moe_args, TM),
        out_specs=pl.BlockSpec((TM, d), lambda i: (i, 0)),
        out_shape=jax.ShapeDtypeStruct((n_rows, d), F32),
        compiler_params=_cparams(("arbitrary",)),
        name="moe_out",
    )(*moe_args)


FUSED_TM = 256


def _moe_out_inproj(moe_args, inproj_args, *, layer, n_batch, seq):
    v = moe_args[2]
    n_rows, d = v.shape
    tm = FUSED_TM
    in_specs, out_specs, out_shape = _inproj_specs(n_rows, d, *inproj_args, layer=layer, tm=tm, n_batch=n_batch,
                                                   seq=seq)
    rows = lambda i: (i, 0)
    return pl.pallas_call(
        functools.partial(_moe_out_inproj_kernel, tiles_per_batch=seq // tm, n_batch=n_batch, d=d),
        grid=(n_rows // tm,),
        in_specs=_moe_out_specs(*moe_args, tm) + in_specs,
        out_specs=[pl.BlockSpec((tm, d), rows)] + out_specs,
        out_shape=[jax.ShapeDtypeStruct((n_rows, d), F32)] + out_shape,
        compiler_params=_cparams(("arbitrary",)),
        name="moe_out_inproj",
    )(*moe_args, *inproj_args)


def _moe_routed(v, vp, layer, rw, rb, exp_wg, exp_wu, exp_wd):
    n_rows = v.shape[0]
    n_slots = n_rows * TOP_K
    assert n_slots % EXPERT_TILE == 0
    eidx, rank, w_t, cnt = _router(v, rw, rb)
    slot_t = _slots(eidx, rank, cnt)
    xs = _sc_dispatch(vp, slot_t, n_slots)
    ys = _experts(xs, _work_items(cnt, n_slots), layer, exp_wg, exp_wu, exp_wd)
    return _sc_collect(ys, slot_t), w_t.T


def _rope_tables(seq):
    rows = seq // GRID_W
    row_id = np.repeat(np.arange(rows, dtype=np.float64), GRID_W)
    col_id = np.tile(np.arange(GRID_W, dtype=np.float64), rows)

    def tables(rot_dim):
        axis_dim = rot_dim // 2
        inv_freq = ROPE_BASE ** (-np.arange(0, axis_dim, 2, dtype=np.float64) / axis_dim)
        ang_r = row_id[:, None] * inv_freq[None, :]
        ang_c = col_id[:, None] * inv_freq[None, :]
        cos = np.concatenate([np.cos(ang_r), np.cos(ang_r), np.cos(ang_c), np.cos(ang_c)], axis=1)
        sin = np.concatenate([-np.sin(ang_r), np.sin(ang_r), -np.sin(ang_c), np.sin(ang_c)], axis=1)
        return cos, sin

    cos64, sin64 = tables(GQA_DIM)
    cos32, sin32 = tables(MLA_ROPE)
    ones = np.ones((seq, MLA_NOPE))
    pad = LANES - MLA_NOPE - MLA_ROPE
    cospe = np.concatenate([ones, cos32, np.ones((seq, pad))], axis=1)
    sinpe = np.concatenate([0 * ones, sin32, np.zeros((seq, pad))], axis=1)
    tab = np.concatenate([cos64, cos64, sin64, sin64, cospe, sinpe], axis=1)
    ident = np.concatenate([np.ones((TM, LANES)), np.zeros((TM, LANES)),
                            np.ones((TM, LANES)), np.zeros((TM, LANES))], axis=1)
    return jnp.asarray(np.concatenate([tab, ident], axis=0), F32)


def _w_in_moves(d):
    sizes = (MLA_Q_LORA, MLA_KV_LORA, MLA_ROPE, GQA_HEADS * GQA_DIM, GQA_KV_HEADS * GQA_DIM,
             GQA_KV_HEADS * GQA_DIM, RET_HEADS * RET_QK, RET_HEADS * RET_QK, RET_HEADS * RET_V,
             RET_HEADS * RET_V, 3 * d)
    src = [sum(sizes[:i]) for i in range(len(sizes))]
    cq, ckv, kpe, gq, gk, gv, rq, rk, rv, rg, gates = src
    moves = [(cq, C_CQ, MLA_Q_LORA, 1.0), (ckv, C_CKV, MLA_KV_LORA, 1.0), (kpe, C_KPE + MLA_NOPE, MLA_ROPE, 1.0),
             (gq, C_G, GQA_HEADS * GQA_DIM, GQA_DIM ** -0.5 * LOG2_E)]
    dst = C_G + GQA_HEADS * GQA_DIM
    for base in (gk, gv):
        for hd in range(GQA_KV_HEADS):
            for _ in range(2):
                moves.append((base + hd * GQA_DIM, dst, GQA_DIM, 1.0))
                dst += GQA_DIM
    assert dst == C_R
    for s, width, scale in ((rq, RET_HEADS * RET_QK, 1.0), (rk, RET_HEADS * RET_QK, RET_QK ** -0.5),
                            (rv, RET_HEADS * RET_V, 1.0), (rg, RET_HEADS * RET_V, 1.0), (gates, 3 * d, 1.0)):
        moves.append((s, dst, width, scale))
        dst += width
    assert dst == W_COLS
    return moves


def _pack_kernel(w_ref, o_ref, *, moves):
    o_ref[:, C_KPE:C_G] = jnp.zeros((o_ref.shape[0], C_G - C_KPE), o_ref.dtype)
    for s, t, width, scale in moves:
        piece = w_ref[:, s:s + width]
        o_ref[:, t:t + width] = (piece if scale == 1.0 else piece * scale).astype(o_ref.dtype)


def _pack_w_in(w_in):
    n_layers, d, n_cols = w_in.shape
    rows = 256
    return pl.pallas_call(
        functools.partial(_pack_kernel, moves=_w_in_moves(d)),
        grid=(n_layers, d // rows),
        in_specs=[pl.BlockSpec((None, rows, n_cols), lambda l, i: (l, i, 0))],
        out_specs=pl.BlockSpec((None, rows, W_COLS), lambda l, i: (l, i, 0)),
        out_shape=jax.ShapeDtypeStruct((n_layers, d, W_COLS), BF16),
        compiler_params=_cparams(("arbitrary", "arbitrary")),
        name="pack_w_in",
    )(w_in)


def _pack_mla_up(w_uq, w_ukv):
    r = w_uq.shape[0]
    dq = MLA_NOPE + MLA_ROPE
    wq = jnp.pad(w_uq.reshape(r, MLA_HEADS, dq), ((0, 0), (0, 0), (0, LANES - dq))).reshape(r, MLA_HEADS * LANES)
    kv = w_ukv.reshape(r, MLA_HEADS, MLA_NOPE + MLA_V)
    wk = jnp.pad(kv[:, :, :MLA_NOPE], ((0, 0), (0, 0), (0, LANES - MLA_NOPE))).reshape(r, MLA_HEADS * LANES)
    wv = kv[:, :, MLA_NOPE:]
    zeros = jnp.zeros_like(wv)
    even = jnp.concatenate([wv, zeros], axis=2)
    odd = jnp.concatenate([zeros, wv], axis=2)
    wv = jnp.where((jnp.arange(MLA_HEADS) % 2 == 0)[None, :, None], even, odd).reshape(r, MLA_HEADS * LANES)
    return wq.astype(BF16), wk.astype(BF16), wv.astype(BF16)


def kernel(x, c, ctx, c_ctx, ada_w, ada_b, norm_mix_pre, norm_mix_post, norm_ffn_pre, norm_ffn_post, w_in, mla_q_norm, mla_w_uq, mla_kv_norm, mla_w_ukv, gqa_sink, ret_decay_fwd, ret_decay_bwd, w_br_mla, w_br_gqa, w_br_ret, w_out, router_w, router_bias, exp_w_gate, exp_w_up, exp_w_down, shared_w_gate, shared_w_up, shared_w_down):
    n_batch, seq, d = x.shape
    n_ctx = ctx.shape[1]
    depth = ada_w.shape[0]
    n_lat_rows = n_batch * seq
    assert seq % TM == 0 and (n_batch * n_ctx) % TM == 0 and seq % ATT_TQ == 0 and seq % n_ctx == 0
    assert n_batch < MOD_ROWS and seq % GRID_W == 0 and d == D_MODEL

    cond = jnp.zeros((MOD_ROWS, d), F32).at[:n_batch].set(c).at[n_batch].set(c_ctx)
    mods_all = _adaln(cond, ada_w, ada_b)
    rope = _rope_tables(seq)
    h = (x.reshape(n_lat_rows, d), ctx.reshape(n_batch * n_ctx, d), 0)
    n_all_rows = n_lat_rows + n_batch * n_ctx
    row = lambda p: p.reshape(1, -1)
    dims = dict(n_batch=n_batch, seq=seq)

    w_in_packed = _pack_w_in(w_in)

    def inproj_args(l):
        return (mods_all[l], row(norm_mix_pre[l]), w_in_packed, rope, row(mla_q_norm[l]),
                row(mla_kv_norm[l]), *_pack_mla_up(mla_w_uq[l], mla_w_ukv[l]))

    projected = _inproj(h[0], h[1], inproj_args(0), layer=0, **dims)
    for l in range(depth):
        last = l == depth - 1
        mods = mods_all[l]
        mq, mk, mv, gqa, ret, gates = projected
        a = _mla_attention(mq, mk, mv, ctx=n_ctx, with_ctx_queries=not last, **dims)
        sink_tab = jnp.broadcast_to(gqa_sink[l].astype(F32)[:, None] * LOG2_E, (GQA_HEADS, LANES))
        w = _window_attention(gqa, sink_tab, ctx=n_ctx, with_ctx_queries=not last, **dims)
        lg = jnp.concatenate([jax.nn.log_sigmoid(ret_decay_fwd[l].astype(F32)),
                              jax.nn.log_sigmoid(ret_decay_bwd[l].astype(F32))])
        o_f, o_b = _retention(ret, jnp.broadcast_to(lg[:, None], (2 * RET_HEADS, LANES)), ctx=n_ctx, **dims)
        n_rows = n_lat_rows if last else n_all_rows
        h1, v, vp = _merge(a, w, o_f, o_b, ret, gates, h, mods, row(norm_mix_post[l]), row(norm_ffn_pre[l]),
                           w_br_mla[l].astype(BF16), w_br_gqa[l].astype(BF16), w_br_ret[l].astype(BF16),
                           w_out[l].astype(BF16), n_rows=n_rows, **dims)
        yg, gate_w = _moe_routed(v, vp, l, router_w[l].T.astype(BF16), router_bias[l].astype(F32).reshape(-1, 1),
                                 exp_w_gate, exp_w_up, exp_w_down)
        moe_args = (yg, gate_w, v, shared_w_gate[l].astype(BF16), shared_w_up[l].astype(BF16),
                    shared_w_down[l].astype(BF16), h1, mods, row(norm_ffn_post[l]))
        if last:
            out = _moe_out(moe_args, **dims)
        else:
            stream, *projected = _moe_out_inproj(moe_args, inproj_args(l + 1), layer=l + 1, **dims)
            h = (stream, stream, n_lat_rows // TM)
    return out[:n_lat_rows].reshape(n_batch, seq, d)
```

```python
import functools

import numpy as np
import jax
import jax.numpy as jnp
from jax import lax
from jax.experimental import pallas as pl
from jax.experimental.pallas import tpu as pltpu
from jax.experimental.pallas import tpu_sc as plsc

F32 = jnp.float32
BF16 = jnp.bfloat16

GRID_W = 64
ROPE_BASE = 10000.0
NORM_EPS = 1e-6
NEG_INF = -1e30
LOG2_E = 1.4426950408889634
N_MOD = 6
MLA_HEADS, MLA_NOPE, MLA_ROPE, MLA_V = 8, 64, 32, 64
MLA_Q_LORA, MLA_KV_LORA = 256, 256
GQA_HEADS, GQA_KV_HEADS, GQA_DIM, WINDOW = 8, 2, 64, 128
RET_HEADS, RET_QK, RET_V, RET_CHUNK = 4, 64, 128, 128
N_EXPERTS, N_EXPERT_GROUPS, TOPK_GROUPS, TOP_K = 64, 8, 4, 8
EXPERTS_PER_GROUP = N_EXPERTS // N_EXPERT_GROUPS
ROUTED_SCALE = 2.5

LANES = 128
TM = 512
ATT_TQ = 512
WIN_TQ = 256
RET_TILE = 256
MOD_ROWS = 8
V7X_VMEM_LIMIT = 56 * 1024 * 1024

D_MODEL = 1024
C_CQ = 0
C_CKV = C_CQ + MLA_Q_LORA
C_KPE = C_CKV + MLA_KV_LORA
C_G = C_KPE + LANES
C_R = C_G + GQA_HEADS * GQA_DIM + 4 * GQA_KV_HEADS * GQA_DIM
C_GATE = C_R + 2 * RET_HEADS * RET_QK + 2 * RET_HEADS * RET_V
W_COLS = C_GATE + 3 * D_MODEL


def _cparams(sem):
    return pltpu.CompilerParams(dimension_semantics=sem, vmem_limit_bytes=V7X_VMEM_LIMIT)


def _rms(x, g):
    return x * lax.rsqrt(jnp.mean(x * x, axis=-1, keepdims=True) + NORM_EPS) * g


def _sigmoid(x):
    return 0.5 * jnp.tanh(0.5 * x) + 0.5


def _dot(a, b):
    return jnp.dot(a, b, preferred_element_type=F32)


def _dot_nt(a, b):
    return lax.dot_general(a, b, (((1,), (1,)), ((), ())), preferred_element_type=F32)


def _dot_tn(a, b):
    return lax.dot_general(a, b, (((0,), (0,)), ((), ())), preferred_element_type=F32)


def _rope(x, cos, sin, half):
    n = x.shape[-1]
    reps = n // LANES
    if reps > 1:
        cos = jnp.concatenate([cos] * reps, axis=1)
        sin = jnp.concatenate([sin] * reps, axis=1)
    lane = lax.broadcasted_iota(jnp.int32, x.shape, 1)
    up = pltpu.roll(x, half, 1)
    dn = pltpu.roll(x, n - half, 1)
    partner = jnp.where((lane & (2 * half - 1)) < half, dn, up)
    return x * cos + partner * sin


def _lane_lo(shape):
    return (lax.broadcasted_iota(jnp.int32, shape, 1) & (LANES - 1)) < (LANES // 2)


def _pack_halves(x):
    n = x.shape[1] // 2
    bits = lambda t: lax.bitcast_convert_type(t.astype(BF16).astype(F32), jnp.uint32)
    return (bits(x[:, :n]) >> 16) | bits(x[:, n:])


def _unpack_halves(p):
    lo = lax.bitcast_convert_type(p << 16, F32)
    hi = lax.bitcast_convert_type(p & jnp.uint32(0xFFFF0000), F32)
    return lo, hi


def _ada_kernel(c_ref, w_ref, b_ref, o_ref):
    c = c_ref[...]
    s = c * _sigmoid(c)
    o_ref[...] = _dot(s.astype(BF16), w_ref[...].astype(BF16)) + b_ref[...]


def _adaln(cond, ada_w, ada_b):
    n_layers, d, n = ada_w.shape
    tn = 1024
    return pl.pallas_call(
        _ada_kernel,
        grid=(n_layers, n // tn),
        in_specs=[pl.BlockSpec((MOD_ROWS, d), lambda l, j: (0, 0)),
                  pl.BlockSpec((None, d, tn), lambda l, j: (l, 0, j)),
                  pl.BlockSpec((None, 1, tn), lambda l, j: (l, 0, j))],
        out_specs=pl.BlockSpec((None, MOD_ROWS, tn), lambda l, j: (l, 0, j)),
        out_shape=jax.ShapeDtypeStruct((n_layers, MOD_ROWS, n), F32),
        compiler_params=_cparams(("arbitrary", "arbitrary")),
        name="adaln",
    )(cond, ada_w, ada_b.reshape(n_layers, 1, n))


def _inproj_tile(h, mod_ref, gpre_ref, w_ref, rope_ref, qn_ref, kvn_ref, wuq_ref, wuk_ref, wuv_ref,
                 mq_ref, mk_ref, mv_ref, gqa_ref, ret_ref, gate_ref, *, tiles_per_batch, n_batch, d):
    i = pl.program_id(0)
    bi = jnp.minimum(i // tiles_per_batch, n_batch)
    sh = mod_ref[pl.ds(bi, 1), 0:d]
    sc = mod_ref[pl.ds(bi, 1), d:2 * d]
    u = (_rms(h, gpre_ref[...]) * (1.0 + sc) + sh).astype(BF16)

    cos64 = rope_ref[:, 0:LANES]
    sin64 = rope_ref[:, LANES:2 * LANES]
    cospe = rope_ref[:, 2 * LANES:3 * LANES]
    sinpe = rope_ref[:, 3 * LANES:4 * LANES]

    c = _dot(u, w_ref[:, C_CQ:C_G])
    kpe = _rope(c[:, C_KPE:C_G], cospe, sinpe, MLA_ROPE // 4)
    qn = _rms(c[:, C_CQ:C_CKV], qn_ref[...]).astype(BF16)
    q = _rope(_dot(qn, wuq_ref[...]), cospe, sinpe, MLA_ROPE // 4)
    mq_ref[...] = (q * ((MLA_NOPE + MLA_ROPE) ** -0.5 * LOG2_E)).astype(mq_ref.dtype)
    kvn = _rms(c[:, C_CKV:C_KPE], kvn_ref[...]).astype(BF16)
    k = _dot(kvn, wuk_ref[...]) + jnp.concatenate([kpe] * MLA_HEADS, axis=1)
    mk_ref[...] = k.astype(mk_ref.dtype)
    v = _dot(kvn, wuv_ref[...])
    lane = lax.broadcasted_iota(jnp.int32, v.shape, 1)
    value_lane = ((lane & (LANES - 1)) < MLA_V) == (((lane >> (LANES.bit_length() - 1)) & 1) == 0)
    mv_ref[...] = jnp.where(value_lane, v, 1.0).astype(mv_ref.dtype)

    g = _dot(u, w_ref[:, C_G:C_R])
    n_qk = GQA_HEADS * GQA_DIM + 2 * GQA_KV_HEADS * GQA_DIM
    gqa_ref[:, 0:n_qk] = _rope(g[:, 0:n_qk], cos64, sin64, GQA_DIM // 4).astype(gqa_ref.dtype)
    gqa_ref[:, n_qk:] = g[:, n_qk:].astype(gqa_ref.dtype)

    r = _dot(u, w_ref[:, C_R:C_GATE])
    n_qk = 2 * RET_HEADS * RET_QK
    ret_ref[:, 0:n_qk] = _rope(r[:, 0:n_qk], cos64, sin64, RET_QK // 4).astype(ret_ref.dtype)
    ret_ref[:, n_qk:] = r[:, n_qk:].astype(ret_ref.dtype)

    gate_ref[...] = _dot(u, w_ref[:, C_GATE:W_COLS]).astype(gate_ref.dtype)


def _inproj_kernel(hl_ref, hc_ref, *refs, tiles_per_batch, n_batch, d):
    latent = pl.program_id(0) < tiles_per_batch * n_batch
    _inproj_tile(jnp.where(latent, hl_ref[...], hc_ref[...]), *refs,
                 tiles_per_batch=tiles_per_batch, n_batch=n_batch, d=d)


def _inproj_specs(t, d, mods, gpre, w_all, rope, qn, kvn, wuq, wuk, wuv, *, layer, tm, n_batch, seq):
    tiles_per_batch = seq // tm
    n_lat_tiles = n_batch * tiles_per_batch
    const = lambda i: (0, 0)
    rows = lambda i: (i, 0)
    rope_idx = lambda i: (jnp.where(i < n_lat_tiles, i % tiles_per_batch, tiles_per_batch), 0)
    once = dict(pipeline_mode=pl.Buffered(1))
    hq = MLA_HEADS * LANES
    outs = [jax.ShapeDtypeStruct((t, hq), BF16), jax.ShapeDtypeStruct((t, hq), BF16),
            jax.ShapeDtypeStruct((t, hq), BF16),
            jax.ShapeDtypeStruct((t, C_R - C_G), BF16),
            jax.ShapeDtypeStruct((t, C_GATE - C_R), F32),
            jax.ShapeDtypeStruct((t, W_COLS - C_GATE), BF16)]
    in_specs = [pl.BlockSpec(mods.shape, const),
                pl.BlockSpec((1, d), const),
                pl.BlockSpec((None,) + w_all.shape[1:], lambda i: (layer, 0, 0), **once),
                pl.BlockSpec((tm, 4 * LANES), rope_idx),
                pl.BlockSpec(qn.shape, const), pl.BlockSpec(kvn.shape, const),
                pl.BlockSpec(wuq.shape, const, **once), pl.BlockSpec(wuk.shape, const, **once),
                pl.BlockSpec(wuv.shape, const, **once)]
    return in_specs, [pl.BlockSpec((tm, o.shape[1]), rows) for o in outs], outs


def _inproj(h_lat, h_ctx, inproj_args, *, layer, n_batch, seq):
    d = h_lat.shape[1]
    t = h_lat.shape[0] + h_ctx.shape[0]
    n_lat_tiles = h_lat.shape[0] // TM
    in_specs, out_specs, out_shape = _inproj_specs(t, d, *inproj_args, layer=layer, tm=TM, n_batch=n_batch, seq=seq)
    return pl.pallas_call(
        functools.partial(_inproj_kernel, tiles_per_batch=seq // TM, n_batch=n_batch, d=d),
        grid=(t // TM,),
        in_specs=[pl.BlockSpec((TM, d), lambda i: (jnp.minimum(i, n_lat_tiles - 1), 0)),
                  pl.BlockSpec((TM, d), lambda i: (jnp.maximum(i - n_lat_tiles, 0), 0))] + in_specs,
        out_specs=out_specs,
        out_shape=out_shape,
        compiler_params=_cparams(("arbitrary",)),
        name="inproj",
    )(h_lat, h_ctx, *inproj_args)


MLA_HEADS_PER_STEP = 4


def _mla_kernel(q_ref, kl_ref, kc_ref, vl_ref, vc_ref, o_ref, s_ref, p_ref, *, with_lat):
    n_ctx = kc_ref.shape[0]

    def body(with_lat):
        n_keys = n_ctx + (kl_ref.shape[0] if with_lat else 0)

        def scores(h):
            sl = slice(h * LANES, (h + 1) * LANES)
            s_ref[h % 2, :, 0:n_ctx] = _dot_nt(q_ref[:, sl], kc_ref[:, sl])
            if with_lat:
                s_ref[h % 2, :, n_ctx:n_keys] = _dot_nt(q_ref[:, sl], kl_ref[:, sl])

        def probs(h):
            s = s_ref[h % 2, :, 0:n_keys]
            p_ref[h % 2, :, 0:n_keys] = jnp.exp2(s - jnp.max(s, axis=-1, keepdims=True)).astype(BF16)

        def weighted(h):
            sl = slice(h * LANES, (h + 1) * LANES)
            o = _dot(p_ref[h % 2, :, 0:n_ctx], vc_ref[:, sl])
            if with_lat:
                o = o + _dot(p_ref[h % 2, :, n_ctx:n_keys], vl_ref[:, sl])
            return o / pltpu.roll(o, LANES // 2, 1)

        outs = [None] * MLA_HEADS_PER_STEP
        scores(0)
        for h in range(MLA_HEADS_PER_STEP):
            if h + 1 < MLA_HEADS_PER_STEP:
                scores(h + 1)
            probs(h)
            outs[h] = weighted(h)
        for pr in range(MLA_HEADS_PER_STEP // 2):
            even, odd = outs[2 * pr], outs[2 * pr + 1]
            o_ref[:, pr * LANES:(pr + 1) * LANES] = jnp.where(_lane_lo(even.shape), even, odd).astype(o_ref.dtype)

    body(with_lat)


def _mla_call(mq, mk, mv, *, n_batch, seq, ctx, latent_queries):
    hps = MLA_HEADS_PER_STEP
    ctx_blk0 = n_batch * seq // ctx
    tq = ATT_TQ if latent_queries else ctx
    nq = seq // tq if latent_queries else 1
    q_blk0 = 0 if latent_queries else n_batch * seq // tq
    q_idx = lambda b, g, i: (q_blk0 + b * nq + i, g)
    ctx_idx = lambda b, g, i: (ctx_blk0 + b, g)
    lat_idx = (lambda b, g, i: (b, g)) if latent_queries else ctx_idx
    n_lat = seq if latent_queries else ctx
    n_keys = ctx + (seq if latent_queries else 0)
    in_specs = [pl.BlockSpec((tq, hps * LANES), q_idx),
                pl.BlockSpec((n_lat, hps * LANES), lat_idx), pl.BlockSpec((ctx, hps * LANES), ctx_idx),
                pl.BlockSpec((n_lat, hps * LANES), lat_idx), pl.BlockSpec((ctx, hps * LANES), ctx_idx)]
    return pl.pallas_call(
        functools.partial(_mla_kernel, with_lat=latent_queries),
        grid=(n_batch, MLA_HEADS // hps, nq),
        in_specs=in_specs,
        out_specs=pl.BlockSpec((tq, hps * MLA_V), lambda b, g, i: (b * nq + i, g)),
        out_shape=jax.ShapeDtypeStruct((n_batch * nq * tq, MLA_HEADS * MLA_V), BF16),
        scratch_shapes=[pltpu.VMEM((2, tq, n_keys), F32), pltpu.VMEM((2, tq, n_keys), BF16)],
        compiler_params=_cparams(("arbitrary", "arbitrary", "arbitrary")),
        name="mla_attn" if latent_queries else "mla_attn_ctx",
    )(mq, mk, mk, mv, mv)


def _mla_attention(mq, mk, mv, *, n_batch, seq, ctx, with_ctx_queries):
    dims = dict(n_batch=n_batch, seq=seq, ctx=ctx)
    lat = _mla_call(mq, mk, mv, latent_queries=True, **dims)
    return lat, (_mla_call(mq, mk, mv, latent_queries=False, **dims) if with_ctx_queries else lat)


def _win_kernel(q_ref, kp_ref, kcur_ref, kn_ref, vp_ref, vcur_ref, vn_ref, kc_ref, vc_ref, sink_ref, o_ref,
                s_ref, p_ref, *, seq, with_lat):
    i = pl.program_id(1)
    tq = q_ref.shape[0]
    group = GQA_HEADS // GQA_KV_HEADS

    def body(with_lat):
        n_ctx = kc_ref.shape[0]
        n_keys = n_ctx + (tq + 2 * WINDOW if with_lat else 0)
        if with_lat:
            q_pos = i * tq + lax.broadcasted_iota(jnp.int32, (tq, n_keys), 0)
            k_pos = i * tq - WINDOW - n_ctx + lax.broadcasted_iota(jnp.int32, (tq, n_keys), 1)
            in_band = (jnp.abs(q_pos - k_pos) <= WINDOW) & (k_pos >= 0) & (k_pos < seq)
            valid = in_band | (lax.broadcasted_iota(jnp.int32, (tq, n_keys), 1) < n_ctx)
        lo = _lane_lo((tq, LANES))
        lo_k = _lane_lo((n_keys, LANES))
        keys, values = [], []
        for kv in range(GQA_KV_HEADS):
            sl = slice(kv * LANES, (kv + 1) * LANES)
            if with_lat:
                k_all = jnp.concatenate([kc_ref[:, sl], kp_ref[:, sl], kcur_ref[:, sl], kn_ref[:, sl]], axis=0)
                v_all = jnp.concatenate([vc_ref[:, sl], vp_ref[:, sl], vcur_ref[:, sl], vn_ref[:, sl]], axis=0)
            else:
                k_all, v_all = kc_ref[:, sl], vc_ref[:, sl]
            keys.append(k_all)
            one = jnp.ones_like(v_all)
            values.append((jnp.where(lo_k, v_all, one), jnp.where(lo_k, one, v_all)))

        def scores(hd):
            kv, pair = hd // group, hd // 2
            qp = q_ref[:, pair * LANES:(pair + 1) * LANES]
            qm = jnp.where(lo if hd % 2 == 0 else jnp.logical_not(lo), qp, jnp.zeros_like(qp))
            s = _dot_nt(qm, keys[kv])
            s_ref[hd % 2, :, 0:n_keys] = jnp.where(valid, s, NEG_INF) if with_lat else s

        def probs(hd):
            s = s_ref[hd % 2, :, 0:n_keys]
            m = jnp.maximum(jnp.max(s, axis=-1, keepdims=True), sink_ref[hd:hd + 1, 0:1])
            p_ref[hd % 2, :, 0:n_keys] = jnp.exp2(s - m).astype(BF16)
            return jnp.exp2(sink_ref[hd:hd + 1, 0:1] - m)

        def weighted(hd, sink_term):
            o = _dot(p_ref[hd % 2, :, 0:n_keys], values[hd // group][hd % 2])
            return o / (pltpu.roll(o, LANES // 2, 1) + sink_term)

        outs = [None] * GQA_HEADS
        scores(0)
        for hd in range(GQA_HEADS):
            if hd + 1 < GQA_HEADS:
                scores(hd + 1)
            outs[hd] = weighted(hd, probs(hd))
        for pair in range(GQA_HEADS // 2):
            o_ref[:, pair * LANES:(pair + 1) * LANES] = jnp.where(
                lo, outs[2 * pair], outs[2 * pair + 1]).astype(o_ref.dtype)

    body(with_lat)


def _win_call(gqa, sink_tab, *, n_batch, seq, ctx, latent_queries):
    tq = WIN_TQ if latent_queries else ctx
    nq = seq // tq if latent_queries else 1
    q_blk0 = 0 if latent_queries else n_batch * seq // tq
    per_tile = tq // WINDOW
    n_win_blocks = seq // WINDOW
    ctx_blk0 = n_batch * seq // ctx
    nqk = GQA_HEADS * GQA_DIM
    kw = 2 * GQA_KV_HEADS * GQA_DIM
    k_col, v_col = nqk // kw, nqk // kw + 1
    q_idx = lambda b, i: (q_blk0 + b * nq + i, 0)
    cidx = lambda col: (lambda b, i: (ctx_blk0 + b, col))
    if latent_queries:
        cur = lambda col: (lambda b, i: (b * nq + i, col))
        prev = lambda col: (lambda b, i: (b * n_win_blocks + jnp.maximum(per_tile * i - 1, 0), col))
        nxt = lambda col: (lambda b, i: (b * n_win_blocks + jnp.minimum(per_tile * (i + 1), n_win_blocks - 1), col))
        band = [((WINDOW, kw), prev), ((tq, kw), cur), ((WINDOW, kw), nxt)]
    else:
        band = [((ctx, kw), cidx)] * 3
    n_keys = ctx + (tq + 2 * WINDOW if latent_queries else 0)
    in_specs = ([pl.BlockSpec((tq, nqk), q_idx)]
                + [pl.BlockSpec(shape, idx(k_col)) for shape, idx in band]
                + [pl.BlockSpec(shape, idx(v_col)) for shape, idx in band]
                + [pl.BlockSpec((ctx, kw), cidx(k_col)), pl.BlockSpec((ctx, kw), cidx(v_col)),
                   pl.BlockSpec(sink_tab.shape, lambda b, i: (0, 0))])
    return pl.pallas_call(
        functools.partial(_win_kernel, seq=seq, with_lat=latent_queries),
        grid=(n_batch, nq),
        in_specs=in_specs,
        out_specs=pl.BlockSpec((tq, nqk), lambda b, i: (b * nq + i, 0)),
        out_shape=jax.ShapeDtypeStruct((n_batch * nq * tq, nqk), BF16),
        scratch_shapes=[pltpu.VMEM((2, tq, n_keys), F32), pltpu.VMEM((2, tq, n_keys), BF16)],
        compiler_params=_cparams(("arbitrary", "arbitrary")),
        name="win_attn" if latent_queries else "win_attn_ctx",
    )(*([gqa] * 9 + [sink_tab]))


def _window_attention(gqa, sink_tab, *, n_batch, seq, ctx, with_ctx_queries):
    dims = dict(n_batch=n_batch, seq=seq, ctx=ctx)
    lat = _win_call(gqa, sink_tab, latent_queries=True, **dims)
    return lat, (_win_call(gqa, sink_tab, latent_queries=False, **dims) if with_ctx_queries else lat)


def _ret_kernel(f_ref, b_ref, lg_ref, of_ref, ob_ref, sf_ref, sb_ref, qdec_ref, kdec_ref, cdec_ref, inner_ref):
    @pl.when(pl.program_id(1) == 0)
    def _():
        sf_ref[...] = jnp.zeros_like(sf_ref)
        sb_ref[...] = jnp.zeros_like(sb_ref)

    L = f_ref.shape[0]
    lo = _lane_lo((L, LANES))
    srow_lo = lax.broadcasted_iota(jnp.int32, (LANES, LANES), 0) < RET_QK
    nq = RET_HEADS * RET_QK
    n_pairs = RET_HEADS // 2

    @pl.when(pl.program_id(1) == 0)
    def _():
        ii = lax.broadcasted_iota(jnp.int32, (L, L), 0)
        jj = lax.broadcasted_iota(jnp.int32, (L, L), 1)
        row = lax.broadcasted_iota(jnp.int32, (L, LANES), 0).astype(F32)
        for direction, forward in enumerate((True, False)):
            dist = ii - jj if forward else jj - ii
            distf = jnp.maximum(dist, 0).astype(F32)
            for pr in range(n_pairs):
                r0 = direction * RET_HEADS + 2 * pr
                lg = [lg_ref[r0 + e:r0 + e + 1, :] for e in range(2)]
                lg_lane = jnp.where(lo, lg[0], lg[1])
                qdec_ref[direction * n_pairs + pr] = jnp.exp(lg_lane * ((row + 1.0) if forward else (L - row)))
                kdec_ref[direction * n_pairs + pr] = jnp.exp(lg_lane * ((L - 1.0 - row) if forward else row))
                cdec_ref[direction * n_pairs + pr] = jnp.where(srow_lo, jnp.exp(lg[0] * float(L)),
                                                               jnp.exp(lg[1] * float(L)))
                for e in range(2):
                    inner_ref[r0 + e] = jnp.where(dist >= 0, jnp.exp(lg[e][:, 0:1] * distf), 0.0)

    def scan_chunk(x_ref, o_ref, s_ref, direction):
        for pr in range(n_pairs):
            q = x_ref[:, pr * LANES:(pr + 1) * LANES].astype(F32)
            k = x_ref[:, nq + pr * LANES:nq + (pr + 1) * LANES].astype(F32)
            qd = q * qdec_ref[direction * n_pairs + pr]
            kdb = (k * kdec_ref[direction * n_pairs + pr]).astype(BF16)
            kb = k.astype(BF16)
            state = s_ref[pr]
            state_b = state.astype(BF16)
            upd = []
            for e in range(2):
                hd = 2 * pr + e
                keep = lo if e == 0 else jnp.logical_not(lo)
                v = x_ref[:, 2 * nq + hd * RET_V:2 * nq + (hd + 1) * RET_V].astype(BF16)
                attn = _dot_nt(jnp.where(keep, q, 0.0).astype(BF16), kb) * inner_ref[direction * RET_HEADS + hd]
                o = _dot(attn.astype(BF16), v) + _dot(jnp.where(keep, qd, 0.0).astype(BF16), state_b)
                o_ref[:, hd * RET_V:(hd + 1) * RET_V] = o
                upd.append(_dot_tn(kdb, v))
            s_ref[pr] = state * cdec_ref[direction * n_pairs + pr] + jnp.where(srow_lo, upd[0], upd[1])

    scan_chunk(f_ref, of_ref, sf_ref, 0)
    scan_chunk(b_ref, ob_ref, sb_ref, 1)


def _retention(ret, lg_tab, *, n_batch, seq, ctx):
    t = ret.shape[0]
    L = RET_TILE
    assert seq % L == 0 and ctx % L == 0
    n_lat, n_ctx = seq // L, ctx // L
    ctx0 = n_batch * n_lat
    width = 2 * RET_HEADS * RET_QK + RET_HEADS * RET_V
    fwd = lambda b, s: (jnp.where(s < n_ctx, ctx0 + b * n_ctx + s, b * n_lat + s - n_ctx), 0)
    bwd = lambda b, s: (jnp.where(s < n_ctx, ctx0 + b * n_ctx + n_ctx - 1 - s, b * n_lat + n_lat - 1 - (s - n_ctx)), 0)
    out = jax.ShapeDtypeStruct((t, RET_HEADS * RET_V), F32)
    return pl.pallas_call(
        _ret_kernel,
        grid=(n_batch, n_lat + n_ctx),
        in_specs=[pl.BlockSpec((L, width), fwd), pl.BlockSpec((L, width), bwd),
                  pl.BlockSpec(lg_tab.shape, lambda b, s: (0, 0))],
        out_specs=[pl.BlockSpec((L, RET_HEADS * RET_V), fwd), pl.BlockSpec((L, RET_HEADS * RET_V), bwd)],
        out_shape=[out, out],
        scratch_shapes=[pltpu.VMEM((RET_HEADS // 2, LANES, RET_V), F32),
                        pltpu.VMEM((RET_HEADS // 2, LANES, RET_V), F32),
                        pltpu.VMEM((RET_HEADS, L, LANES), F32), pltpu.VMEM((RET_HEADS, L, LANES), F32),
                        pltpu.VMEM((RET_HEADS, LANES, RET_V), F32), pltpu.VMEM((2 * RET_HEADS, L, L), F32)],
        compiler_params=_cparams(("arbitrary", "arbitrary")),
        name="retention",
    )(ret, ret, lg_tab)


def _merge_kernel(al_ref, ac_ref, wl_ref, wc_ref, of_ref, ob_ref, rg_ref, gt_ref, hl_ref, hc_ref, mod_ref,
                  gpost_ref, gffn_ref, wa_ref, ww_ref, wr_ref, wo_ref, h1_ref, v_ref, vp_ref,
                  *, tiles_per_batch, n_batch, d):
    i = pl.program_id(0)
    bi = jnp.minimum(i // tiles_per_batch, n_batch)
    latent = i < tiles_per_batch * n_batch
    a_tile = jnp.where(latent, al_ref[...], ac_ref[...])
    w_tile = jnp.where(latent, wl_ref[...], wc_ref[...])
    h_tile = jnp.where(latent, hl_ref[...], hc_ref[...])
    o = of_ref[...] + ob_ref[...]
    normed = []
    for hd in range(RET_HEADS):
        oh = o[:, hd * RET_V:(hd + 1) * RET_V]
        dev = oh - jnp.mean(oh, axis=-1, keepdims=True)
        normed.append(dev * lax.rsqrt(jnp.mean(dev * dev, axis=-1, keepdims=True) + NORM_EPS))
    g = rg_ref[...].astype(F32)
    r = (g * _sigmoid(g)) * jnp.concatenate(normed, axis=1)
    y = (_sigmoid(gt_ref[:, 0:d].astype(F32)) * _dot(a_tile, wa_ref[...])
         + _sigmoid(gt_ref[:, d:2 * d].astype(F32)) * _dot(w_tile, ww_ref[...])
         + _sigmoid(gt_ref[:, 2 * d:3 * d].astype(F32)) * _dot(r.astype(BF16), wr_ref[...]))
    z = _dot(y.astype(BF16), wo_ref[...])
    g1 = mod_ref[pl.ds(bi, 1), 2 * d:3 * d]
    sh2 = mod_ref[pl.ds(bi, 1), 3 * d:4 * d]
    sc2 = mod_ref[pl.ds(bi, 1), 4 * d:5 * d]
    h1 = h_tile + g1 * _rms(z, gpost_ref[...])
    h1_ref[...] = h1
    v = _rms(h1, gffn_ref[...]) * (1.0 + sc2) + sh2
    v_ref[...] = v.astype(v_ref.dtype)
    vp_ref[...] = _pack_halves(v)


def _merge(a, w, o_f, o_b, ret, gates, h, mods, gpost, gffn, wa, ww, wr, wo, *, n_rows, n_batch, seq):
    d = h[0].shape[1]
    n_lat_tiles = n_batch * seq // TM
    rows = lambda i: (i, 0)
    lat_rows = lambda i: (jnp.minimum(i, n_lat_tiles - 1), 0)
    ctx_rows = lambda i: (jnp.maximum(i - n_lat_tiles, 0), 0)
    const = lambda i: (0, 0)
    rv = RET_HEADS * RET_V
    rg_col = (2 * RET_HEADS * RET_QK + rv) // rv
    outs = [jax.ShapeDtypeStruct((n_rows, d), F32), jax.ShapeDtypeStruct((n_rows, d), BF16),
            jax.ShapeDtypeStruct((n_rows, d // 2), jnp.uint32)]
    return pl.pallas_call(
        functools.partial(_merge_kernel, tiles_per_batch=seq // TM, n_batch=n_batch, d=d),
        grid=(n_rows // TM,),
        in_specs=[pl.BlockSpec((TM, a[0].shape[1]), lat_rows), pl.BlockSpec((TM, a[1].shape[1]), ctx_rows),
                  pl.BlockSpec((TM, w[0].shape[1]), lat_rows), pl.BlockSpec((TM, w[1].shape[1]), ctx_rows),
                  pl.BlockSpec((TM, rv), rows), pl.BlockSpec((TM, rv), rows),
                  pl.BlockSpec((TM, rv), lambda i: (i, rg_col)),
                  pl.BlockSpec((TM, 3 * d), rows), pl.BlockSpec((TM, d), lat_rows),
                  pl.BlockSpec((TM, d), lambda i: (h[2] + jnp.maximum(i - n_lat_tiles, 0), 0)),
                  pl.BlockSpec(mods.shape, const), pl.BlockSpec((1, d), const), pl.BlockSpec((1, d), const),
                  pl.BlockSpec(wa.shape, const), pl.BlockSpec(ww.shape, const),
                  pl.BlockSpec(wr.shape, const), pl.BlockSpec(wo.shape, const)],
        out_specs=[pl.BlockSpec((TM, o.shape[1]), rows) for o in outs],
        out_shape=outs,
        compiler_params=_cparams(("arbitrary",)),
        name="merge",
    )(a[0], a[1], w[0], w[1], o_f, o_b, ret, gates, h[0], h[1], mods, gpost, gffn, wa, ww, wr, wo)


def _router_kernel(v_ref, rw_ref, rb_ref, eidx_ref, rank_ref, w_ref, cnt_ref, carry_ref):
    @pl.when(pl.program_id(0) == 0)
    def _():
        carry_ref[...] = jnp.zeros_like(carry_ref)

    tm = v_ref.shape[0]
    scores = _sigmoid(_dot_nt(rw_ref[...], v_ref[...]))
    sel = scores + rb_ref[...]
    neg = -jnp.inf
    n_grp, per = N_EXPERT_GROUPS, EXPERTS_PER_GROUP

    sel3 = sel.reshape(n_grp, per, tm)
    member_id = lax.broadcasted_iota(jnp.int32, sel3.shape, 1)
    m1 = jnp.max(sel3, axis=1, keepdims=True)
    i1 = jnp.min(jnp.where(sel3 == m1, member_id, per), axis=1, keepdims=True)
    m2 = jnp.max(jnp.where(member_id == i1, neg, sel3), axis=1, keepdims=True)
    gscore = (m1 + m2).reshape(n_grp, tm)
    gid = lax.broadcasted_iota(jnp.int32, gscore.shape, 0)
    ahead = jnp.zeros(gscore.shape, jnp.int32)
    for gj in range(n_grp):
        other = gscore[gj:gj + 1, :]
        ahead = ahead + jnp.where((other > gscore) | ((other == gscore) & (gid > gj)), 1, 0)
    group_ok = (ahead < TOPK_GROUPS).reshape(n_grp, 1, tm)
    sel = jnp.where(group_ok, sel3, NEG_INF).reshape(N_EXPERTS, tm)

    eid = lax.broadcasted_iota(jnp.int32, sel.shape, 0)
    chosen = jnp.zeros(sel.shape, jnp.bool_)
    picks = []
    for _ in range(TOP_K):
        m = jnp.max(sel, axis=0, keepdims=True)
        idx = jnp.min(jnp.where(sel == m, eid, N_EXPERTS), axis=0, keepdims=True)
        hit = eid == idx
        chosen = chosen | hit
        sel = jnp.where(hit, neg, sel)
        picks.append(idx)
    w = jnp.where(chosen, scores, 0.0)
    gate = ROUTED_SCALE * w / jnp.sum(w, axis=0, keepdims=True)

    member = jnp.where(chosen, 1.0, 0.0)
    earlier = lax.broadcasted_iota(jnp.int32, (tm, tm), 0) < lax.broadcasted_iota(jnp.int32, (tm, tm), 1)
    pos = _dot(member.astype(BF16), jnp.where(earlier, 1.0, 0.0).astype(BF16)) + carry_ref[...]
    for k, idx in enumerate(picks):
        hit = eid == idx
        eidx_ref[k:k + 1, :] = idx
        rank_ref[k:k + 1, :] = jnp.sum(jnp.where(hit, pos, 0.0), axis=0, keepdims=True)
        w_ref[k:k + 1, :] = jnp.sum(jnp.where(hit, gate, 0.0), axis=0, keepdims=True)
    carry_ref[...] += jnp.sum(member, axis=1, keepdims=True)
    cnt_ref[...] = carry_ref[...]


def _router(v, rw_t, rb):
    n_rows, d = v.shape
    cols = lambda i: (0, i)
    const = lambda i: (0, 0)
    outs = [jax.ShapeDtypeStruct((TOP_K, n_rows), jnp.int32), jax.ShapeDtypeStruct((TOP_K, n_rows), F32),
            jax.ShapeDtypeStruct((TOP_K, n_rows), F32), jax.ShapeDtypeStruct((N_EXPERTS, 1), F32)]
    return pl.pallas_call(
        _router_kernel,
        grid=(n_rows // TM,),
        in_specs=[pl.BlockSpec((TM, d), lambda i: (i, 0)), pl.BlockSpec(rw_t.shape, const),
                  pl.BlockSpec(rb.shape, const)],
        out_specs=[pl.BlockSpec((TOP_K, TM), cols), pl.BlockSpec((TOP_K, TM), cols),
                   pl.BlockSpec((TOP_K, TM), cols), pl.BlockSpec((N_EXPERTS, 1), const)],
        out_shape=outs,
        scratch_shapes=[pltpu.VMEM((N_EXPERTS, 1), F32)],
        compiler_params=_cparams(("arbitrary",)),
        name="router",
    )(v, rw_t, rb)


def _slots_kernel(eidx_ref, rank_ref, cnt_ref, slot_ref):
    tm = eidx_ref.shape[1]
    eid = lax.broadcasted_iota(jnp.int32, (N_EXPERTS, tm), 0)
    for k in range(TOP_K):
        before = jnp.sum(jnp.where(eid < eidx_ref[k:k + 1, :], cnt_ref[...], 0.0), axis=0, keepdims=True)
        slot_ref[k:k + 1, :] = (before + rank_ref[k:k + 1, :]).astype(jnp.int32)


def _slots(eidx, rank, cnt):
    n_rows = eidx.shape[1]
    tm = next(c for c in (2048, 1024, 512, 256) if n_rows % c == 0)
    cols = lambda i: (0, i)
    return pl.pallas_call(
        _slots_kernel,
        grid=(n_rows // tm,),
        in_specs=[pl.BlockSpec((TOP_K, tm), cols), pl.BlockSpec((TOP_K, tm), cols),
                  pl.BlockSpec(cnt.shape, lambda i: (0, 0))],
        out_specs=pl.BlockSpec((TOP_K, tm), cols),
        out_shape=jax.ShapeDtypeStruct((TOP_K, n_rows), jnp.int32),
        compiler_params=_cparams(("arbitrary",)),
        name="slots",
    )(eidx, rank, cnt)


SC_WINDOW = 128


def _sc_mesh():
    return plsc.VectorSubcoreMesh(core_axis_name="core", subcore_axis_name="subcore")


def _sc_dispatch(rows, slot_t, n_out):
    width = rows.shape[1]
    n_chunks = slot_t.shape[1] // SC_WINDOW
    info = plsc.get_sparse_core_info()
    n_workers = info.num_cores * info.num_subcores

    @functools.partial(
        pl.kernel, mesh=_sc_mesh(),
        out_type=jax.ShapeDtypeStruct((n_out, width), rows.dtype),
        scratch_types=[pltpu.VMEM((TOP_K, SC_WINDOW), jnp.int32), pltpu.VMEM((SC_WINDOW, width), rows.dtype)],
        name="moe_dispatch")
    def run(rows_hbm, idx_hbm, out_hbm, idx_v, rows_v):
        wid = lax.axis_index("subcore") * info.num_cores + lax.axis_index("core")

        @pl.loop(wid, n_chunks, step=n_workers)
        def _(c):
            r0 = pl.multiple_of(c * SC_WINDOW, SC_WINDOW)
            pltpu.sync_copy(idx_hbm.at[:, pl.ds(r0, SC_WINDOW)], idx_v)
            pltpu.sync_copy(rows_hbm.at[pl.ds(r0, SC_WINDOW)], rows_v)
            for k in range(TOP_K):
                pltpu.sync_copy(rows_v, out_hbm.at[idx_v.at[k]])

    return run(rows, slot_t)


def _sc_collect(rows, slot_t):
    n_picks, n_rows = slot_t.shape
    width = rows.shape[1]
    n_chunks = n_rows // SC_WINDOW
    info = plsc.get_sparse_core_info()
    n_workers = info.num_cores * info.num_subcores

    @functools.partial(
        pl.kernel, mesh=_sc_mesh(),
        out_type=jax.ShapeDtypeStruct((n_picks, n_rows, width), rows.dtype),
        scratch_types=[pltpu.VMEM((TOP_K, SC_WINDOW), jnp.int32), pltpu.VMEM((SC_WINDOW, width), rows.dtype)],
        name="moe_collect")
    def run(rows_hbm, idx_hbm, out_hbm, idx_v, rows_v):
        wid = lax.axis_index("subcore") * info.num_cores + lax.axis_index("core")

        @pl.loop(wid, n_chunks, step=n_workers)
        def _(c):
            r0 = pl.multiple_of(c * SC_WINDOW, SC_WINDOW)
            pltpu.sync_copy(idx_hbm.at[:, pl.ds(r0, SC_WINDOW)], idx_v)
            for k in range(TOP_K):
                pltpu.sync_copy(rows_hbm.at[idx_v.at[k]], rows_v)
                pltpu.sync_copy(rows_v, out_hbm.at[k, pl.ds(r0, SC_WINDOW)])

    return run(rows, slot_t)


EXPERT_TILE = 512


def _work_items(cnt, n_slots):
    counts = cnt[:, 0].astype(jnp.int32)
    ends = jnp.cumsum(counts)
    n_tiles = n_slots // EXPERT_TILE
    bounds = jnp.sort(jnp.concatenate([jnp.arange(n_tiles, dtype=jnp.int32) * EXPERT_TILE, ends - counts]))
    nxt = jnp.concatenate([bounds[1:], jnp.array([n_slots], jnp.int32)])
    tile = jnp.minimum(bounds // EXPERT_TILE, n_tiles - 1)
    expert = jnp.sum((ends[None, :] <= bounds[:, None]).astype(jnp.int32), axis=1)
    expert = jnp.minimum(expert, N_EXPERTS - 1)
    return tile, expert, bounds - tile * EXPERT_TILE, nxt - tile * EXPERT_TILE


XS_RING = 3


def _experts_kernel(tile_ref, exp_ref, lo_ref, hi_ref, first_ref, next_ref, wslot_ref, xs_hbm, wg_hbm, wu_hbm, wd_hbm,
                    ys_ref, acc_ref, wgb_ref, wub_ref, wdb_ref, xbuf_ref, xsem, wg_st, wu_st, wd_st, wsem, *, layer):
    i = pl.program_id(0)
    n_items = pl.num_programs(0)
    lo, hi = lo_ref[i], hi_ref[i]

    def weight_copies(expert, slot):
        return [pltpu.make_async_copy(w_hbm.at[layer, expert], stage.at[slot], wsem.at[slot, k])
                for k, (w_hbm, stage) in enumerate(((wg_hbm, wg_st), (wu_hbm, wu_st), (wd_hbm, wd_st)))]

    @pl.when(i == 0)
    def _():
        for c in weight_copies(exp_ref[0], 0):
            c.start()

    def tile_copy(item):
        slot = lax.rem(item, XS_RING)
        row0 = pl.multiple_of(tile_ref[item] * EXPERT_TILE, EXPERT_TILE)
        return pltpu.make_async_copy(xs_hbm.at[pl.ds(row0, EXPERT_TILE)], xbuf_ref.at[slot], xsem.at[slot])

    @pl.when(i == 0)
    def _():
        for ahead in range(XS_RING - 1):
            pl.when(ahead < n_items)(lambda: tile_copy(ahead).start())

    @pl.when(i + XS_RING - 1 < n_items)
    def _():
        tile_copy(i + XS_RING - 1).start()

    tile_copy(i).wait()
    xs_ref = xbuf_ref.at[lax.rem(i, XS_RING)]

    @pl.when(first_ref[i] == 1)
    def _():
        slot = wslot_ref[i]
        for c in weight_copies(exp_ref[i], slot):
            c.wait()

        @pl.when(next_ref[i] >= 0)
        def _():
            for c in weight_copies(next_ref[i], 1 - slot):
                c.start()

        wgb_ref[...] = wg_st[slot].astype(BF16)
        wub_ref[...] = wu_st[slot].astype(BF16)
        wdb_ref[...] = wd_st[slot].astype(BF16)

    def ffn():
        x_lo, x_hi = _unpack_halves(xs_ref[...])
        x_lo, x_hi = x_lo.astype(BF16), x_hi.astype(BF16)
        n = x_lo.shape[1]
        a = _dot(x_lo, wgb_ref[0:n, :]) + _dot(x_hi, wgb_ref[n:, :])
        u = _dot(x_lo, wub_ref[0:n, :]) + _dot(x_hi, wub_ref[n:, :])
        return _dot(((a * _sigmoid(a)) * u).astype(BF16), wdb_ref[...])

    whole = (lo == 0) & (hi == EXPERT_TILE)

    @pl.when(whole)
    def _():
        ys_ref[...] = _pack_halves(ffn())

    @pl.when(jnp.logical_not(whole) & (hi > lo))
    def _():
        y = ffn()
        row = lax.broadcasted_iota(jnp.int32, y.shape, 0)
        y = jnp.where((row >= lo) & (row < hi), y, 0.0)

        @pl.when(lo == 0)
        def _():
            acc_ref[...] = y

        @pl.when((lo > 0) & (hi < EXPERT_TILE))
        def _():
            acc_ref[...] += y

        @pl.when((lo > 0) & (hi == EXPERT_TILE))
        def _():
            ys_ref[...] = _pack_halves(acc_ref[...] + y)


def _experts(xs, items, layer, exp_wg, exp_wu, exp_wd):
    n_slots, half = xs.shape
    d, hid = exp_wg.shape[-2:]
    tile, expert, lo, hi = items
    n = tile.shape[0]
    pos = jnp.arange(n)
    first = jnp.concatenate([jnp.ones((1,), jnp.int32), (expert[1:] != expert[:-1]).astype(jnp.int32)])
    later_other = (expert[None, :] != expert[:, None]) & (pos[None, :] > pos[:, None])
    nxt = jnp.where(later_other.any(axis=1), expert[jnp.argmax(later_other, axis=1)], -1).astype(jnp.int32)
    wslot = ((jnp.cumsum(first) - 1) % 2).astype(jnp.int32)
    any_space = pl.BlockSpec(memory_space=pl.ANY)
    grid_spec = pltpu.PrefetchScalarGridSpec(
        num_scalar_prefetch=7,
        grid=(n,),
        in_specs=[any_space, any_space, any_space, any_space],
        out_specs=pl.BlockSpec((EXPERT_TILE, half), lambda i, t, *_: (t[i], 0)),
        scratch_shapes=[pltpu.VMEM((EXPERT_TILE, d), F32), pltpu.VMEM((d, hid), BF16),
                        pltpu.VMEM((d, hid), BF16), pltpu.VMEM((hid, d), BF16),
                        pltpu.VMEM((XS_RING, EXPERT_TILE, half), jnp.uint32),
                        pltpu.SemaphoreType.DMA((XS_RING,)),
                        pltpu.VMEM((2, d, hid), F32), pltpu.VMEM((2, d, hid), F32), pltpu.VMEM((2, hid, d), F32),
                        pltpu.SemaphoreType.DMA((2, 3))])
    return pl.pallas_call(
        functools.partial(_experts_kernel, layer=layer),
        grid_spec=grid_spec,
        out_shape=jax.ShapeDtypeStruct((n_slots, half), jnp.uint32),
        compiler_params=_cparams(("arbitrary",)),
        name="experts",
    )(tile, expert, lo, hi, first, nxt, wslot, xs, exp_wg, exp_wu, exp_wd)


def _moe_out_tile(yg_ref, w_ref, v_ref, sg_ref, su_ref, sd_ref, h1_ref, mod_ref, gpost_ref,
                  *, tiles_per_batch, n_batch, d):
    i = pl.program_id(0)
    x = v_ref[...]
    a = _dot(x, sg_ref[...])
    f = _dot(((a * _sigmoid(a)) * _dot(x, su_ref[...])).astype(BF16), sd_ref[...])
    n = d // 2
    f_lo, f_hi = f[:, :n], f[:, n:]
    w = w_ref[...]
    for k in range(TOP_K):
        y_lo, y_hi = _unpack_halves(yg_ref[k])
        wk = w[:, k:k + 1]
        f_lo = f_lo + wk * y_lo
        f_hi = f_hi + wk * y_hi
    f = jnp.concatenate([f_lo, f_hi], axis=1)
    bi = jnp.minimum(i // tiles_per_batch, n_batch)
    g2 = mod_ref[pl.ds(bi, 1), 5 * d:6 * d]
    return h1_ref[...] + g2 * _rms(f, gpost_ref[...])


N_MOE_OUT_IN = 9
N_INPROJ_IN = 9


def _moe_out_kernel(*refs, **kw):
    refs[N_MOE_OUT_IN][...] = _moe_out_tile(*refs[:N_MOE_OUT_IN], **kw)


def _moe_out_inproj_kernel(*refs, n_batch, d, tiles_per_batch):
    dims = dict(tiles_per_batch=tiles_per_batch, n_batch=n_batch, d=d)
    h = _moe_out_tile(*refs[:N_MOE_OUT_IN], **dims)
    n_in = N_MOE_OUT_IN + N_INPROJ_IN
    refs[n_in][...] = h
    _inproj_tile(h, *refs[N_MOE_OUT_IN:n_in], *refs[n_in + 1:], **dims)


def _moe_out_specs(yg, w, v, sg, su, sd, h1, mods, gpost, tm):
    d = v.shape[1]
    rows = lambda i: (i, 0)
    const = lambda i: (0, 0)
    once = dict(pipeline_mode=pl.Buffered(1))
    return [pl.BlockSpec((TOP_K, tm, d // 2), lambda i: (0, i, 0)), pl.BlockSpec((tm, TOP_K), rows),
            pl.BlockSpec((tm, d), rows),
            pl.BlockSpec(sg.shape, const, **once), pl.BlockSpec(su.shape, const, **once),
            pl.BlockSpec(sd.shape, const, **once),
            pl.BlockSpec((tm, d), rows), pl.BlockSpec(mods.shape, const), pl.BlockSpec((1, d), const)]


def _moe_out(moe_args, *, n_batch, seq):
    v = moe_args[2]
    n_rows, d = v.shape
    return pl.pallas_call(
        functools.partial(_moe_out_kernel, tiles_per_batch=seq // TM, n_batch=n_batch, d=d),
        grid=(n_rows // TM,),
        in_specs=_moe_out_specs(*moe_args, TM),
        out_specs=pl.BlockSpec((TM, d), lambda i: (i, 0)),
        out_shape=jax.ShapeDtypeStruct((n_rows, d), F32),
        compiler_params=_cparams(("arbitrary",)),
        name="moe_out",
    )(*moe_args)


FUSED_TM = 256


def _moe_out_inproj(moe_args, inproj_args, *, layer, n_batch, seq):
    v = moe_args[2]
    n_rows, d = v.shape
    tm = FUSED_TM
    in_specs, out_specs, out_shape = _inproj_specs(n_rows, d, *inproj_args, layer=layer, tm=tm, n_batch=n_batch,
                                                   seq=seq)
    rows = lambda i: (i, 0)
    return pl.pallas_call(
        functools.partial(_moe_out_inproj_kernel, tiles_per_batch=seq // tm, n_batch=n_batch, d=d),
        grid=(n_rows // tm,),
        in_specs=_moe_out_specs(*moe_args, tm) + in_specs,
        out_specs=[pl.BlockSpec((tm, d), rows)] + out_specs,
        out_shape=[jax.ShapeDtypeStruct((n_rows, d), F32)] + out_shape,
        compiler_params=_cparams(("arbitrary",)),
        name="moe_out_inproj",
    )(*moe_args, *inproj_args)


def _moe_routed(v, vp, layer, rw, rb, exp_wg, exp_wu, exp_wd):
    n_rows = v.shape[0]
    n_slots = n_rows * TOP_K
    assert n_slots % EXPERT_TILE == 0
    eidx, rank, w_t, cnt = _router(v, rw, rb)
    slot_t = _slots(eidx, rank, cnt)
    xs = _sc_dispatch(vp, slot_t, n_slots)
    ys = _experts(xs, _work_items(cnt, n_slots), layer, exp_wg, exp_wu, exp_wd)
    return _sc_collect(ys, slot_t), w_t.T


def _rope_tables(seq):
    rows = seq // GRID_W
    row_id = np.repeat(np.arange(rows, dtype=np.float64), GRID_W)
    col_id = np.tile(np.arange(GRID_W, dtype=np.float64), rows)

    def tables(rot_dim):
        axis_dim = rot_dim // 2
        inv_freq = ROPE_BASE ** (-np.arange(0, axis_dim, 2, dtype=np.float64) / axis_dim)
        ang_r = row_id[:, None] * inv_freq[None, :]
        ang_c = col_id[:, None] * inv_freq[None, :]
        cos = np.concatenate([np.cos(ang_r), np.cos(ang_r), np.cos(ang_c), np.cos(ang_c)], axis=1)
        sin = np.concatenate([-np.sin(ang_r), np.sin(ang_r), -np.sin(ang_c), np.sin(ang_c)], axis=1)
        return cos, sin

    cos64, sin64 = tables(GQA_DIM)
    cos32, sin32 = tables(MLA_ROPE)
    ones = np.ones((seq, MLA_NOPE))
    pad = LANES - MLA_NOPE - MLA_ROPE
    cospe = np.concatenate([ones, cos32, np.ones((seq, pad))], axis=1)
    sinpe = np.concatenate([0 * ones, sin32, np.zeros((seq, pad))], axis=1)
    tab = np.concatenate([cos64, cos64, sin64, sin64, cospe, sinpe], axis=1)
    ident = np.concatenate([np.ones((TM, LANES)), np.zeros((TM, LANES)),
                            np.ones((TM, LANES)), np.zeros((TM, LANES))], axis=1)
    return jnp.asarray(np.concatenate([tab, ident], axis=0), F32)


def _w_in_moves(d):
    sizes = (MLA_Q_LORA, MLA_KV_LORA, MLA_ROPE, GQA_HEADS * GQA_DIM, GQA_KV_HEADS * GQA_DIM,
             GQA_KV_HEADS * GQA_DIM, RET_HEADS * RET_QK, RET_HEADS * RET_QK, RET_HEADS * RET_V,
             RET_HEADS * RET_V, 3 * d)
    src = [sum(sizes[:i]) for i in range(len(sizes))]
    cq, ckv, kpe, gq, gk, gv, rq, rk, rv, rg, gates = src
    moves = [(cq, C_CQ, MLA_Q_LORA, 1.0), (ckv, C_CKV, MLA_KV_LORA, 1.0), (kpe, C_KPE + MLA_NOPE, MLA_ROPE, 1.0),
             (gq, C_G, GQA_HEADS * GQA_DIM, GQA_DIM ** -0.5 * LOG2_E)]
    dst = C_G + GQA_HEADS * GQA_DIM
    for base in (gk, gv):
        for hd in range(GQA_KV_HEADS):
            for _ in range(2):
                moves.append((base + hd * GQA_DIM, dst, GQA_DIM, 1.0))
                dst += GQA_DIM
    assert dst == C_R
    for s, width, scale in ((rq, RET_HEADS * RET_QK, 1.0), (rk, RET_HEADS * RET_QK, RET_QK ** -0.5),
                            (rv, RET_HEADS * RET_V, 1.0), (rg, RET_HEADS * RET_V, 1.0), (gates, 3 * d, 1.0)):
        moves.append((s, dst, width, scale))
        dst += width
    assert dst == W_COLS
    return moves


def _pack_kernel(w_ref, o_ref, *, moves):
    o_ref[:, C_KPE:C_G] = jnp.zeros((o_ref.shape[0], C_G - C_KPE), o_ref.dtype)
    for s, t, width, scale in moves:
        piece = w_ref[:, s:s + width]
        o_ref[:, t:t + width] = (piece if scale == 1.0 else piece * scale).astype(o_ref.dtype)


def _pack_w_in(w_in):
    n_layers, d, n_cols = w_in.shape
    rows = 256
    return pl.pallas_call(
        functools.partial(_pack_kernel, moves=_w_in_moves(d)),
        grid=(n_layers, d // rows),
        in_specs=[pl.BlockSpec((None, rows, n_cols), lambda l, i: (l, i, 0))],
        out_specs=pl.BlockSpec((None, rows, W_COLS), lambda l, i: (l, i, 0)),
        out_shape=jax.ShapeDtypeStruct((n_layers, d, W_COLS), BF16),
        compiler_params=_cparams(("arbitrary", "arbitrary")),
        name="pack_w_in",
    )(w_in)


def _pack_mla_up(w_uq, w_ukv):
    r = w_uq.shape[0]
    dq = MLA_NOPE + MLA_ROPE
    wq = jnp.pad(w_uq.reshape(r, MLA_HEADS, dq), ((0, 0), (0, 0), (0, LANES - dq))).reshape(r, MLA_HEADS * LANES)
    kv = w_ukv.reshape(r, MLA_HEADS, MLA_NOPE + MLA_V)
    wk = jnp.pad(kv[:, :, :MLA_NOPE], ((0, 0), (0, 0), (0, LANES - MLA_NOPE))).reshape(r, MLA_HEADS * LANES)
    wv = kv[:, :, MLA_NOPE:]
    zeros = jnp.zeros_like(wv)
    even = jnp.concatenate([wv, zeros], axis=2)
    odd = jnp.concatenate([zeros, wv], axis=2)
    wv = jnp.where((jnp.arange(MLA_HEADS) % 2 == 0)[None, :, None], even, odd).reshape(r, MLA_HEADS * LANES)
    return wq.astype(BF16), wk.astype(BF16), wv.astype(BF16)


def kernel(x, c, ctx, c_ctx, ada_w, ada_b, norm_mix_pre, norm_mix_post, norm_ffn_pre, norm_ffn_post, w_in, mla_q_norm, mla_w_uq, mla_kv_norm, mla_w_ukv, gqa_sink, ret_decay_fwd, ret_decay_bwd, w_br_mla, w_br_gqa, w_br_ret, w_out, router_w, router_bias, exp_w_gate, exp_w_up, exp_w_down, shared_w_gate, shared_w_up, shared_w_down):
    n_batch, seq, d = x.shape
    n_ctx = ctx.shape[1]
    depth = ada_w.shape[0]
    n_lat_rows = n_batch * seq
    assert seq % TM == 0 and (n_batch * n_ctx) % TM == 0 and seq % ATT_TQ == 0 and seq % n_ctx == 0
    assert n_batch < MOD_ROWS and seq % GRID_W == 0 and d == D_MODEL

    cond = jnp.zeros((MOD_ROWS, d), F32).at[:n_batch].set(c).at[n_batch].set(c_ctx)
    mods_all = _adaln(cond, ada_w, ada_b)
    rope = _rope_tables(seq)
    h = (x.reshape(n_lat_rows, d), ctx.reshape(n_batch * n_ctx, d), 0)
    n_all_rows = n_lat_rows + n_batch * n_ctx
    row = lambda p: p.reshape(1, -1)
    dims = dict(n_batch=n_batch, seq=seq)

    w_in_packed = _pack_w_in(w_in)

    def inproj_args(l):
        return (mods_all[l], row(norm_mix_pre[l]), w_in_packed, rope, row(mla_q_norm[l]),
                row(mla_kv_norm[l]), *_pack_mla_up(mla_w_uq[l], mla_w_ukv[l]))

    projected = _inproj(h[0], h[1], inproj_args(0), layer=0, **dims)
    for l in range(depth):
        last = l == depth - 1
        mods = mods_all[l]
        mq, mk, mv, gqa, ret, gates = projected
        a = _mla_attention(mq, mk, mv, ctx=n_ctx, with_ctx_queries=not last, **dims)
        sink_tab = jnp.broadcast_to(gqa_sink[l].astype(F32)[:, None] * LOG2_E, (GQA_HEADS, LANES))
        w = _window_attention(gqa, sink_tab, ctx=n_ctx, with_ctx_queries=not last, **dims)
        lg = jnp.concatenate([jax.nn.log_sigmoid(ret_decay_fwd[l].astype(F32)),
                              jax.nn.log_sigmoid(ret_decay_bwd[l].astype(F32))])
        o_f, o_b = _retention(ret, jnp.broadcast_to(lg[:, None], (2 * RET_HEADS, LANES)), ctx=n_ctx, **dims)
        n_rows = n_lat_rows if last else n_all_rows
        h1, v, vp = _merge(a, w, o_f, o_b, ret, gates, h, mods, row(norm_mix_post[l]), row(norm_ffn_pre[l]),
                           w_br_mla[l].astype(BF16), w_br_gqa[l].astype(BF16), w_br_ret[l].astype(BF16),
                           w_out[l].astype(BF16), n_rows=n_rows, **dims)
        yg, gate_w = _moe_routed(v, vp, l, router_w[l].T.astype(BF16), router_bias[l].astype(F32).reshape(-1, 1),
                                 exp_w_gate, exp_w_up, exp_w_down)
        moe_args = (yg, gate_w, v, shared_w_gate[l].astype(BF16), shared_w_up[l].astype(BF16),
                    shared_w_down[l].astype(BF16), h1, mods, row(norm_ffn_post[l]))
        if last:
            out = _moe_out(moe_args, **dims)
        else:
            stream, *projected = _moe_out_inproj(moe_args, inproj_args(l + 1), layer=l + 1, **dims)
            h = (stream, stream, n_lat_rows // TM)
    return out[:n_lat_rows].reshape(n_batch, seq, d)
```

```python
import functools

import numpy as np
import jax
import jax.numpy as jnp
from jax import lax
from jax.experimental import pallas as pl
from jax.experimental.pallas import tpu as pltpu
from jax.experimental.pallas import tpu_sc as plsc

F32 = jnp.float32
BF16 = jnp.bfloat16

GRID_W = 64
ROPE_BASE = 10000.0
NORM_EPS = 1e-6
NEG_INF = -1e30
LOG2_E = 1.4426950408889634
N_MOD = 6
MLA_HEADS, MLA_NOPE, MLA_ROPE, MLA_V = 8, 64, 32, 64
MLA_Q_LORA, MLA_KV_LORA = 256, 256
GQA_HEADS, GQA_KV_HEADS, GQA_DIM, WINDOW = 8, 2, 64, 128
RET_HEADS, RET_QK, RET_V, RET_CHUNK = 4, 64, 128, 128
N_EXPERTS, N_EXPERT_GROUPS, TOPK_GROUPS, TOP_K = 64, 8, 4, 8
EXPERTS_PER_GROUP = N_EXPERTS // N_EXPERT_GROUPS
ROUTED_SCALE = 2.5

LANES = 128
TM = 512
ATT_TQ = 512
WIN_TQ = 256
RET_TILE = 256
MOD_ROWS = 8
V7X_VMEM_LIMIT = 56 * 1024 * 1024

D_MODEL = 1024
C_CQ = 0
C_CKV = C_CQ + MLA_Q_LORA
C_KPE = C_CKV + MLA_KV_LORA
C_G = C_KPE + LANES
C_R = C_G + GQA_HEADS * GQA_DIM + 4 * GQA_KV_HEADS * GQA_DIM
C_GATE = C_R + 2 * RET_HEADS * RET_QK + 2 * RET_HEADS * RET_V
W_COLS = C_GATE + 3 * D_MODEL


def _cparams(sem):
    return pltpu.CompilerParams(dimension_semantics=sem, vmem_limit_bytes=V7X_VMEM_LIMIT)


def _rms(x, g):
    return x * lax.rsqrt(jnp.mean(x * x, axis=-1, keepdims=True) + NORM_EPS) * g


def _sigmoid(x):
    return 0.5 * jnp.tanh(0.5 * x) + 0.5


def _dot(a, b):
    return jnp.dot(a, b, preferred_element_type=F32)


def _dot_nt(a, b):
    return lax.dot_general(a, b, (((1,), (1,)), ((), ())), preferred_element_type=F32)


def _dot_tn(a, b):
    return lax.dot_general(a, b, (((0,), (0,)), ((), ())), preferred_element_type=F32)


def _rope(x, cos, sin, half):
    n = x.shape[-1]
    reps = n // LANES
    if reps > 1:
        cos = jnp.concatenate([cos] * reps, axis=1)
        sin = jnp.concatenate([sin] * reps, axis=1)
    lane = lax.broadcasted_iota(jnp.int32, x.shape, 1)
    up = pltpu.roll(x, half, 1)
    dn = pltpu.roll(x, n - half, 1)
    partner = jnp.where((lane & (2 * half - 1)) < half, dn, up)
    return x * cos + partner * sin


def _lane_lo(shape):
    return (lax.broadcasted_iota(jnp.int32, shape, 1) & (LANES - 1)) < (LANES // 2)


def _pack_halves(x):
    n = x.shape[1] // 2
    bits = lambda t: lax.bitcast_convert_type(t.astype(BF16).astype(F32), jnp.uint32)
    return (bits(x[:, :n]) >> 16) | bits(x[:, n:])


def _unpack_halves(p):
    lo = lax.bitcast_convert_type(p << 16, F32)
    hi = lax.bitcast_convert_type(p & jnp.uint32(0xFFFF0000), F32)
    return lo, hi


def _ada_kernel(c_ref, w_ref, b_ref, o_ref):
    c = c_ref[...]
    s = c * _sigmoid(c)
    o_ref[...] = _dot(s.astype(BF16), w_ref[...].astype(BF16)) + b_ref[...]


def _adaln(cond, ada_w, ada_b):
    n_layers, d, n = ada_w.shape
    tn = 1024
    return pl.pallas_call(
        _ada_kernel,
        grid=(n_layers, n // tn),
        in_specs=[pl.BlockSpec((MOD_ROWS, d), lambda l, j: (0, 0)),
                  pl.BlockSpec((None, d, tn), lambda l, j: (l, 0, j)),
                  pl.BlockSpec((None, 1, tn), lambda l, j: (l, 0, j))],
        out_specs=pl.BlockSpec((None, MOD_ROWS, tn), lambda l, j: (l, 0, j)),
        out_shape=jax.ShapeDtypeStruct((n_layers, MOD_ROWS, n), F32),
        compiler_params=_cparams(("arbitrary", "arbitrary")),
        name="adaln",
    )(cond, ada_w, ada_b.reshape(n_layers, 1, n))


def _inproj_tile(h, mod_ref, gpre_ref, w_ref, rope_ref, qn_ref, kvn_ref, wuq_ref, wuk_ref, wuv_ref,
                 mq_ref, mk_ref, mv_ref, gqa_ref, ret_ref, gate_ref, *, tiles_per_batch, n_batch, d):
    i = pl.program_id(0)
    bi = jnp.minimum(i // tiles_per_batch, n_batch)
    sh = mod_ref[pl.ds(bi, 1), 0:d]
    sc = mod_ref[pl.ds(bi, 1), d:2 * d]
    u = (_rms(h, gpre_ref[...]) * (1.0 + sc) + sh).astype(BF16)

    cos64 = rope_ref[:, 0:LANES]
    sin64 = rope_ref[:, LANES:2 * LANES]
    cospe = rope_ref[:, 2 * LANES:3 * LANES]
    sinpe = rope_ref[:, 3 * LANES:4 * LANES]

    c = _dot(u, w_ref[:, C_CQ:C_G])
    kpe = _rope(c[:, C_KPE:C_G], cospe, sinpe, MLA_ROPE // 4)
    qn = _rms(c[:, C_CQ:C_CKV], qn_ref[...]).astype(BF16)
    q = _rope(_dot(qn, wuq_ref[...]), cospe, sinpe, MLA_ROPE // 4)
    mq_ref[...] = (q * ((MLA_NOPE + MLA_ROPE) ** -0.5 * LOG2_E)).astype(mq_ref.dtype)
    kvn = _rms(c[:, C_CKV:C_KPE], kvn_ref[...]).astype(BF16)
    k = _dot(kvn, wuk_ref[...]) + jnp.concatenate([kpe] * MLA_HEADS, axis=1)
    mk_ref[...] = k.astype(mk_ref.dtype)
    v = _dot(kvn, wuv_ref[...])
    lane = lax.broadcasted_iota(jnp.int32, v.shape, 1)
    value_lane = ((lane & (LANES - 1)) < MLA_V) == (((lane >> (LANES.bit_length() - 1)) & 1) == 0)
    mv_ref[...] = jnp.where(value_lane, v, 1.0).astype(mv_ref.dtype)

    g = _dot(u, w_ref[:, C_G:C_R])
    n_qk = GQA_HEADS * GQA_DIM + 2 * GQA_KV_HEADS * GQA_DIM
    gqa_ref[:, 0:n_qk] = _rope(g[:, 0:n_qk], cos64, sin64, GQA_DIM // 4).astype(gqa_ref.dtype)
    gqa_ref[:, n_qk:] = g[:, n_qk:].astype(gqa_ref.dtype)

    r = _dot(u, w_ref[:, C_R:C_GATE])
    n_qk = 2 * RET_HEADS * RET_QK
    ret_ref[:, 0:n_qk] = _rope(r[:, 0:n_qk], cos64, sin64, RET_QK // 4).astype(ret_ref.dtype)
    ret_ref[:, n_qk:] = r[:, n_qk:].astype(ret_ref.dtype)

    gate_ref[...] = _dot(u, w_ref[:, C_GATE:W_COLS]).astype(gate_ref.dtype)


def _inproj_kernel(hl_ref, hc_ref, *refs, tiles_per_batch, n_batch, d):
    latent = pl.program_id(0) < tiles_per_batch * n_batch
    _inproj_tile(jnp.where(latent, hl_ref[...], hc_ref[...]), *refs,
                 tiles_per_batch=tiles_per_batch, n_batch=n_batch, d=d)


def _inproj_specs(t, d, mods, gpre, w_all, rope, qn, kvn, wuq, wuk, wuv, *, layer, tm, n_batch, seq):
    tiles_per_batch = seq // tm
    n_lat_tiles = n_batch * tiles_per_batch
    const = lambda i: (0, 0)
    rows = lambda i: (i, 0)
    rope_idx = lambda i: (jnp.where(i < n_lat_tiles, i % tiles_per_batch, tiles_per_batch), 0)
    once = dict(pipeline_mode=pl.Buffered(1))
    hq = MLA_HEADS * LANES
    outs = [jax.ShapeDtypeStruct((t, hq), BF16), jax.ShapeDtypeStruct((t, hq), BF16),
            jax.ShapeDtypeStruct((t, hq), BF16),
            jax.ShapeDtypeStruct((t, C_R - C_G), BF16),
            jax.ShapeDtypeStruct((t, C_GATE - C_R), F32),
            jax.ShapeDtypeStruct((t, W_COLS - C_GATE), BF16)]
    in_specs = [pl.BlockSpec(mods.shape, const),
                pl.BlockSpec((1, d), const),
                pl.BlockSpec((None,) + w_all.shape[1:], lambda i: (layer, 0, 0), **once),
                pl.BlockSpec((tm, 4 * LANES), rope_idx),
                pl.BlockSpec(qn.shape, const), pl.BlockSpec(kvn.shape, const),
                pl.BlockSpec(wuq.shape, const, **once), pl.BlockSpec(wuk.shape, const, **once),
                pl.BlockSpec(wuv.shape, const, **once)]
    return in_specs, [pl.BlockSpec((tm, o.shape[1]), rows) for o in outs], outs


def _inproj(h_lat, h_ctx, inproj_args, *, layer, n_batch, seq):
    d = h_lat.shape[1]
    t = h_lat.shape[0] + h_ctx.shape[0]
    n_lat_tiles = h_lat.shape[0] // TM
    in_specs, out_specs, out_shape = _inproj_specs(t, d, *inproj_args, layer=layer, tm=TM, n_batch=n_batch, seq=seq)
    return pl.pallas_call(
        functools.partial(_inproj_kernel, tiles_per_batch=seq // TM, n_batch=n_batch, d=d),
        grid=(t // TM,),
        in_specs=[pl.BlockSpec((TM, d), lambda i: (jnp.minimum(i, n_lat_tiles - 1), 0)),
                  pl.BlockSpec((TM, d), lambda i: (jnp.maximum(i - n_lat_tiles, 0), 0))] + in_specs,
        out_specs=out_specs,
        out_shape=out_shape,
        compiler_params=_cparams(("arbitrary",)),
        name="inproj",
    )(h_lat, h_ctx, *inproj_args)


MLA_HEADS_PER_STEP = 4


def _mla_kernel(q_ref, kl_ref, kc_ref, vl_ref, vc_ref, o_ref, s_ref, p_ref, *, with_lat):
    n_ctx = kc_ref.shape[0]

    def body(with_lat):
        n_keys = n_ctx + (kl_ref.shape[0] if with_lat else 0)

        def scores(h):
            sl = slice(h * LANES, (h + 1) * LANES)
            s_ref[h % 2, :, 0:n_ctx] = _dot_nt(q_ref[:, sl], kc_ref[:, sl])
            if with_lat:
                s_ref[h % 2, :, n_ctx:n_keys] = _dot_nt(q_ref[:, sl], kl_ref[:, sl])

        def probs(h):
            s = s_ref[h % 2, :, 0:n_keys]
            p_ref[h % 2, :, 0:n_keys] = jnp.exp2(s - jnp.max(s, axis=-1, keepdims=True)).astype(BF16)

        def weighted(h):
            sl = slice(h * LANES, (h + 1) * LANES)
            o = _dot(p_ref[h % 2, :, 0:n_ctx], vc_ref[:, sl])
            if with_lat:
                o = o + _dot(p_ref[h % 2, :, n_ctx:n_keys], vl_ref[:, sl])
            return o / pltpu.roll(o, LANES // 2, 1)

        outs = [None] * MLA_HEADS_PER_STEP
        scores(0)
        for h in range(MLA_HEADS_PER_STEP):
            if h + 1 < MLA_HEADS_PER_STEP:
                scores(h + 1)
            probs(h)
            outs[h] = weighted(h)
        for pr in range(MLA_HEADS_PER_STEP // 2):
            even, odd = outs[2 * pr], outs[2 * pr + 1]
            o_ref[:, pr * LANES:(pr + 1) * LANES] = jnp.where(_lane_lo(even.shape), even, odd).astype(o_ref.dtype)

    body(with_lat)


def _mla_call(mq, mk, mv, *, n_batch, seq, ctx, latent_queries):
    hps = MLA_HEADS_PER_STEP
    ctx_blk0 = n_batch * seq // ctx
    tq = ATT_TQ if latent_queries else ctx
    nq = seq // tq if latent_queries else 1
    q_blk0 = 0 if latent_queries else n_batch * seq // tq
    q_idx = lambda b, g, i: (q_blk0 + b * nq + i, g)
    ctx_idx = lambda b, g, i: (ctx_blk0 + b, g)
    lat_idx = (lambda b, g, i: (b, g)) if latent_queries else ctx_idx
    n_lat = seq if latent_queries else ctx
    n_keys = ctx + (seq if latent_queries else 0)
    in_specs = [pl.BlockSpec((tq, hps * LANES), q_idx),
                pl.BlockSpec((n_lat, hps * LANES), lat_idx), pl.BlockSpec((ctx, hps * LANES), ctx_idx),
                pl.BlockSpec((n_lat, hps * LANES), lat_idx), pl.BlockSpec((ctx, hps * LANES), ctx_idx)]
    return pl.pallas_call(
        functools.partial(_mla_kernel, with_lat=latent_queries),
        grid=(n_batch, MLA_HEADS // hps, nq),
        in_specs=in_specs,
        out_specs=pl.BlockSpec((tq, hps * MLA_V), lambda b, g, i: (b * nq + i, g)),
        out_shape=jax.ShapeDtypeStruct((n_batch * nq * tq, MLA_HEADS * MLA_V), BF16),
        scratch_shapes=[pltpu.VMEM((2, tq, n_keys), F32), pltpu.VMEM((2, tq, n_keys), BF16)],
        compiler_params=_cparams(("arbitrary", "arbitrary", "arbitrary")),
        name="mla_attn" if latent_queries else "mla_attn_ctx",
    )(mq, mk, mk, mv, mv)


def _mla_attention(mq, mk, mv, *, n_batch, seq, ctx, with_ctx_queries):
    dims = dict(n_batch=n_batch, seq=seq, ctx=ctx)
    lat = _mla_call(mq, mk, mv, latent_queries=True, **dims)
    return lat, (_mla_call(mq, mk, mv, latent_queries=False, **dims) if with_ctx_queries else lat)


def _win_kernel(q_ref, kp_ref, kcur_ref, kn_ref, vp_ref, vcur_ref, vn_ref, kc_ref, vc_ref, sink_ref, o_ref,
                s_ref, p_ref, *, seq, with_lat, tile=None):
    i = pl.program_id(1) if tile is None else tile
    tq = q_ref.shape[0]
    group = GQA_HEADS // GQA_KV_HEADS

    def body(with_lat):
        n_ctx = kc_ref.shape[0]
        n_keys = n_ctx + (tq + 2 * WINDOW if with_lat else 0)
        if with_lat:
            q_pos = i * tq + lax.broadcasted_iota(jnp.int32, (tq, n_keys), 0)
            k_pos = i * tq - WINDOW - n_ctx + lax.broadcasted_iota(jnp.int32, (tq, n_keys), 1)
            in_band = (jnp.abs(q_pos - k_pos) <= WINDOW) & (k_pos >= 0) & (k_pos < seq)
            valid = in_band | (lax.broadcasted_iota(jnp.int32, (tq, n_keys), 1) < n_ctx)
        lo = _lane_lo((tq, LANES))
        lo_k = _lane_lo((n_keys, LANES))
        keys, values = [], []
        for kv in range(GQA_KV_HEADS):
            sl = slice(kv * LANES, (kv + 1) * LANES)
            if with_lat:
                k_all = jnp.concatenate([kc_ref[:, sl], kp_ref[:, sl], kcur_ref[:, sl], kn_ref[:, sl]], axis=0)
                v_all = jnp.concatenate([vc_ref[:, sl], vp_ref[:, sl], vcur_ref[:, sl], vn_ref[:, sl]], axis=0)
            else:
                k_all, v_all = kc_ref[:, sl], vc_ref[:, sl]
            keys.append(k_all)
            one = jnp.ones_like(v_all)
            values.append((jnp.where(lo_k, v_all, one), jnp.where(lo_k, one, v_all)))

        def scores(hd):
            kv, pair = hd // group, hd // 2
            qp = q_ref[:, pair * LANES:(pair + 1) * LANES]
            qm = jnp.where(lo if hd % 2 == 0 else jnp.logical_not(lo), qp, jnp.zeros_like(qp))
            s = _dot_nt(qm, keys[kv])
            s_ref[hd % 2, :, 0:n_keys] = jnp.where(valid, s, NEG_INF) if with_lat else s

        def probs(hd):
            s = s_ref[hd % 2, :, 0:n_keys]
            m = jnp.maximum(jnp.max(s, axis=-1, keepdims=True), sink_ref[hd:hd + 1, 0:1])
            p_ref[hd % 2, :, 0:n_keys] = jnp.exp2(s - m).astype(BF16)
            return jnp.exp2(sink_ref[hd:hd + 1, 0:1] - m)

        def weighted(hd, sink_term):
            o = _dot(p_ref[hd % 2, :, 0:n_keys], values[hd // group][hd % 2])
            return o / (pltpu.roll(o, LANES // 2, 1) + sink_term)

        outs = [None] * GQA_HEADS
        scores(0)
        for hd in range(GQA_HEADS):
            if hd + 1 < GQA_HEADS:
                scores(hd + 1)
            outs[hd] = weighted(hd, probs(hd))
        for pair in range(GQA_HEADS // 2):
            o_ref[:, pair * LANES:(pair + 1) * LANES] = jnp.where(
                lo, outs[2 * pair], outs[2 * pair + 1]).astype(o_ref.dtype)

    body(with_lat)


def _win_call(gqa, sink_tab, *, n_batch, seq, ctx, latent_queries):
    tq = WIN_TQ if latent_queries else ctx
    nq = seq // tq if latent_queries else 1
    q_blk0 = 0 if latent_queries else n_batch * seq // tq
    per_tile = tq // WINDOW
    n_win_blocks = seq // WINDOW
    ctx_blk0 = n_batch * seq // ctx
    nqk = GQA_HEADS * GQA_DIM
    kw = 2 * GQA_KV_HEADS * GQA_DIM
    k_col, v_col = nqk // kw, nqk // kw + 1
    q_idx = lambda b, i: (q_blk0 + b * nq + i, 0)
    cidx = lambda col: (lambda b, i: (ctx_blk0 + b, col))
    if latent_queries:
        cur = lambda col: (lambda b, i: (b * nq + i, col))
        prev = lambda col: (lambda b, i: (b * n_win_blocks + jnp.maximum(per_tile * i - 1, 0), col))
        nxt = lambda col: (lambda b, i: (b * n_win_blocks + jnp.minimum(per_tile * (i + 1), n_win_blocks - 1), col))
        band = [((WINDOW, kw), prev), ((tq, kw), cur), ((WINDOW, kw), nxt)]
    else:
        band = [((ctx, kw), cidx)] * 3
    n_keys = ctx + (tq + 2 * WINDOW if latent_queries else 0)
    in_specs = ([pl.BlockSpec((tq, nqk), q_idx)]
                + [pl.BlockSpec(shape, idx(k_col)) for shape, idx in band]
                + [pl.BlockSpec(shape, idx(v_col)) for shape, idx in band]
                + [pl.BlockSpec((ctx, kw), cidx(k_col)), pl.BlockSpec((ctx, kw), cidx(v_col)),
                   pl.BlockSpec(sink_tab.shape, lambda b, i: (0, 0))])
    return pl.pallas_call(
        functools.partial(_win_kernel, seq=seq, with_lat=latent_queries),
        grid=(n_batch, nq),
        in_specs=in_specs,
        out_specs=pl.BlockSpec((tq, nqk), lambda b, i: (b * nq + i, 0)),
        out_shape=jax.ShapeDtypeStruct((n_batch * nq * tq, nqk), BF16),
        scratch_shapes=[pltpu.VMEM((2, tq, n_keys), F32), pltpu.VMEM((2, tq, n_keys), BF16)],
        compiler_params=_cparams(("arbitrary", "arbitrary")),
        name="win_attn" if latent_queries else "win_attn_ctx",
    )(*([gqa] * 9 + [sink_tab]))


def _window_attention(gqa, sink_tab, *, n_batch, seq, ctx, with_ctx_queries):
    dims = dict(n_batch=n_batch, seq=seq, ctx=ctx)
    lat = _win_call(gqa, sink_tab, latent_queries=True, **dims)
    return lat, (_win_call(gqa, sink_tab, latent_queries=False, **dims) if with_ctx_queries else lat)


def _ret_kernel(f_ref, b_ref, lg_ref, of_ref, ob_ref, sf_ref, sb_ref, qdec_ref, kdec_ref, cdec_ref, inner_ref):
    @pl.when(pl.program_id(1) == 0)
    def _():
        sf_ref[...] = jnp.zeros_like(sf_ref)
        sb_ref[...] = jnp.zeros_like(sb_ref)

    L = f_ref.shape[0]
    lo = _lane_lo((L, LANES))
    srow_lo = lax.broadcasted_iota(jnp.int32, (LANES, LANES), 0) < RET_QK
    nq = RET_HEADS * RET_QK
    n_pairs = RET_HEADS // 2

    @pl.when(pl.program_id(1) == 0)
    def _():
        ii = lax.broadcasted_iota(jnp.int32, (L, L), 0)
        jj = lax.broadcasted_iota(jnp.int32, (L, L), 1)
        row = lax.broadcasted_iota(jnp.int32, (L, LANES), 0).astype(F32)
        for direction, forward in enumerate((True, False)):
            dist = ii - jj if forward else jj - ii
            distf = jnp.maximum(dist, 0).astype(F32)
            for pr in range(n_pairs):
                r0 = direction * RET_HEADS + 2 * pr
                lg = [lg_ref[r0 + e:r0 + e + 1, :] for e in range(2)]
                lg_lane = jnp.where(lo, lg[0], lg[1])
                qdec_ref[direction * n_pairs + pr] = jnp.exp(lg_lane * ((row + 1.0) if forward else (L - row)))
                kdec_ref[direction * n_pairs + pr] = jnp.exp(lg_lane * ((L - 1.0 - row) if forward else row))
                cdec_ref[direction * n_pairs + pr] = jnp.where(srow_lo, jnp.exp(lg[0] * float(L)),
                                                               jnp.exp(lg[1] * float(L)))
                for e in range(2):
                    inner_ref[r0 + e] = jnp.where(dist >= 0, jnp.exp(lg[e][:, 0:1] * distf), 0.0)

    def scan_chunk(x_ref, o_ref, s_ref, direction):
        for pr in range(n_pairs):
            q = x_ref[:, pr * LANES:(pr + 1) * LANES].astype(F32)
            k = x_ref[:, nq + pr * LANES:nq + (pr + 1) * LANES].astype(F32)
            qd = q * qdec_ref[direction * n_pairs + pr]
            kdb = (k * kdec_ref[direction * n_pairs + pr]).astype(BF16)
            kb = k.astype(BF16)
            state = s_ref[pr]
            state_b = state.astype(BF16)
            upd = []
            for e in range(2):
                hd = 2 * pr + e
                keep = lo if e == 0 else jnp.logical_not(lo)
                v = x_ref[:, 2 * nq + hd * RET_V:2 * nq + (hd + 1) * RET_V].astype(BF16)
                attn = _dot_nt(jnp.where(keep, q, 0.0).astype(BF16), kb) * inner_ref[direction * RET_HEADS + hd]
                o = _dot(attn.astype(BF16), v) + _dot(jnp.where(keep, qd, 0.0).astype(BF16), state_b)
                o_ref[:, hd * RET_V:(hd + 1) * RET_V] = o
                upd.append(_dot_tn(kdb, v))
            s_ref[pr] = state * cdec_ref[direction * n_pairs + pr] + jnp.where(srow_lo, upd[0], upd[1])

    scan_chunk(f_ref, of_ref, sf_ref, 0)
    scan_chunk(b_ref, ob_ref, sb_ref, 1)


def _retention(ret, lg_tab, *, n_batch, seq, ctx):
    t = ret.shape[0]
    L = RET_TILE
    assert seq % L == 0 and ctx % L == 0
    n_lat, n_ctx = seq // L, ctx // L
    ctx0 = n_batch * n_lat
    width = 2 * RET_HEADS * RET_QK + RET_HEADS * RET_V
    fwd = lambda b, s: (jnp.where(s < n_ctx, ctx0 + b * n_ctx + s, b * n_lat + s - n_ctx), 0)
    bwd = lambda b, s: (jnp.where(s < n_ctx, ctx0 + b * n_ctx + n_ctx - 1 - s, b * n_lat + n_lat - 1 - (s - n_ctx)), 0)
    out = jax.ShapeDtypeStruct((t, RET_HEADS * RET_V), F32)
    return pl.pallas_call(
        _ret_kernel,
        grid=(n_batch, n_lat + n_ctx),
        in_specs=[pl.BlockSpec((L, width), fwd), pl.BlockSpec((L, width), bwd),
                  pl.BlockSpec(lg_tab.shape, lambda b, s: (0, 0))],
        out_specs=[pl.BlockSpec((L, RET_HEADS * RET_V), fwd), pl.BlockSpec((L, RET_HEADS * RET_V), bwd)],
        out_shape=[out, out],
        scratch_shapes=[pltpu.VMEM((RET_HEADS // 2, LANES, RET_V), F32),
                        pltpu.VMEM((RET_HEADS // 2, LANES, RET_V), F32),
                        pltpu.VMEM((RET_HEADS, L, LANES), F32), pltpu.VMEM((RET_HEADS, L, LANES), F32),
                        pltpu.VMEM((RET_HEADS, LANES, RET_V), F32), pltpu.VMEM((2 * RET_HEADS, L, L), F32)],
        compiler_params=_cparams(("arbitrary", "arbitrary")),
        name="retention",
    )(ret, ret, lg_tab)


def _ret_win_kernel(f_ref, b_ref, lg_ref, q_ref, kp_ref, kcur_ref, kn_ref, vp_ref, vcur_ref, vn_ref, kc_ref, vc_ref,
                    sink_ref, of_ref, ob_ref, w_ref, sf_ref, sb_ref, qdec_ref, kdec_ref, cdec_ref, inner_ref,
                    s_ref, p_ref, *, seq, n_win_tiles):
    _ret_kernel(f_ref, b_ref, lg_ref, of_ref, ob_ref, sf_ref, sb_ref, qdec_ref, kdec_ref, cdec_ref, inner_ref)
    _win_kernel(q_ref, kp_ref, kcur_ref, kn_ref, vp_ref, vcur_ref, vn_ref, kc_ref, vc_ref, sink_ref, w_ref,
                s_ref, p_ref, seq=seq, with_lat=True, tile=jnp.minimum(pl.program_id(1), n_win_tiles - 1))


def _retention_and_window(ret, lg_tab, gqa, sink_tab, *, n_batch, seq, ctx):
    t = ret.shape[0]
    L = RET_TILE
    assert L == WIN_TQ and ctx == L and seq % L == 0
    n_lat = seq // L
    ctx0 = n_batch * n_lat
    width = 2 * RET_HEADS * RET_QK + RET_HEADS * RET_V
    fwd = lambda b, s: (jnp.where(s < 1, ctx0 + b, b * n_lat + s - 1), 0)
    bwd = lambda b, s: (jnp.where(s < 1, ctx0 + b, b * n_lat + n_lat - s), 0)
    per_tile = L // WINDOW
    n_win_blocks = seq // WINDOW
    nqk = GQA_HEADS * GQA_DIM
    kw = 2 * GQA_KV_HEADS * GQA_DIM
    k_col, v_col = nqk // kw, nqk // kw + 1
    tile = lambda s: jnp.minimum(s, n_lat - 1)
    q_idx = lambda b, s: (b * n_lat + tile(s), 0)
    cur = lambda col: (lambda b, s: (b * n_lat + tile(s), col))
    prev = lambda col: (lambda b, s: (b * n_win_blocks + jnp.maximum(per_tile * tile(s) - 1, 0), col))
    nxt = lambda col: (lambda b, s: (b * n_win_blocks + jnp.minimum(per_tile * (tile(s) + 1), n_win_blocks - 1), col))
    cidx = lambda col: (lambda b, s: (ctx0 + b, col))
    band = [((WINDOW, kw), prev), ((L, kw), cur), ((WINDOW, kw), nxt)]
    n_keys = ctx + L + 2 * WINDOW
    o_shape = jax.ShapeDtypeStruct((t, RET_HEADS * RET_V), F32)
    return pl.pallas_call(
        functools.partial(_ret_win_kernel, seq=seq, n_win_tiles=n_lat),
        grid=(n_batch, n_lat + 1),
        in_specs=([pl.BlockSpec((L, width), fwd), pl.BlockSpec((L, width), bwd),
                   pl.BlockSpec(lg_tab.shape, lambda b, s: (0, 0)), pl.BlockSpec((L, nqk), q_idx)]
                  + [pl.BlockSpec(shape, idx(k_col)) for shape, idx in band]
                  + [pl.BlockSpec(shape, idx(v_col)) for shape, idx in band]
                  + [pl.BlockSpec((ctx, kw), cidx(k_col)), pl.BlockSpec((ctx, kw), cidx(v_col)),
                     pl.BlockSpec(sink_tab.shape, lambda b, s: (0, 0))]),
        out_specs=[pl.BlockSpec((L, RET_HEADS * RET_V), fwd), pl.BlockSpec((L, RET_HEADS * RET_V), bwd),
                   pl.BlockSpec((L, nqk), q_idx)],
        out_shape=[o_shape, o_shape, jax.ShapeDtypeStruct((n_batch * seq, nqk), BF16)],
        scratch_shapes=[pltpu.VMEM((RET_HEADS // 2, LANES, RET_V), F32),
                        pltpu.VMEM((RET_HEADS // 2, LANES, RET_V), F32),
                        pltpu.VMEM((RET_HEADS, L, LANES), F32), pltpu.VMEM((RET_HEADS, L, LANES), F32),
                        pltpu.VMEM((RET_HEADS, LANES, RET_V), F32), pltpu.VMEM((2 * RET_HEADS, L, L), F32),
                        pltpu.VMEM((2, L, n_keys), F32), pltpu.VMEM((2, L, n_keys), BF16)],
        compiler_params=_cparams(("arbitrary", "arbitrary")),
        name="retention_window",
    )(ret, ret, lg_tab, *([gqa] * 9), sink_tab)


def _merge_kernel(al_ref, ac_ref, wl_ref, wc_ref, of_ref, ob_ref, rg_ref, gt_ref, hl_ref, hc_ref, mod_ref,
                  gpost_ref, gffn_ref, wa_ref, ww_ref, wr_ref, wo_ref, h1_ref, v_ref, vp_ref,
                  *, tiles_per_batch, n_batch, d):
    i = pl.program_id(0)
    bi = jnp.minimum(i // tiles_per_batch, n_batch)
    latent = i < tiles_per_batch * n_batch
    a_tile = jnp.where(latent, al_ref[...], ac_ref[...])
    w_tile = jnp.where(latent, wl_ref[...], wc_ref[...])
    h_tile = jnp.where(latent, hl_ref[...], hc_ref[...])
    o = of_ref[...] + ob_ref[...]
    normed = []
    for hd in range(RET_HEADS):
        oh = o[:, hd * RET_V:(hd + 1) * RET_V]
        dev = oh - jnp.mean(oh, axis=-1, keepdims=True)
        normed.append(dev * lax.rsqrt(jnp.mean(dev * dev, axis=-1, keepdims=True) + NORM_EPS))
    g = rg_ref[...].astype(F32)
    r = (g * _sigmoid(g)) * jnp.concatenate(normed, axis=1)
    y = (_sigmoid(gt_ref[:, 0:d].astype(F32)) * _dot(a_tile, wa_ref[...])
         + _sigmoid(gt_ref[:, d:2 * d].astype(F32)) * _dot(w_tile, ww_ref[...])
         + _sigmoid(gt_ref[:, 2 * d:3 * d].astype(F32)) * _dot(r.astype(BF16), wr_ref[...]))
    z = _dot(y.astype(BF16), wo_ref[...])
    g1 = mod_ref[pl.ds(bi, 1), 2 * d:3 * d]
    sh2 = mod_ref[pl.ds(bi, 1), 3 * d:4 * d]
    sc2 = mod_ref[pl.ds(bi, 1), 4 * d:5 * d]
    h1 = h_tile + g1 * _rms(z, gpost_ref[...])
    h1_ref[...] = h1
    v = _rms(h1, gffn_ref[...]) * (1.0 + sc2) + sh2
    v_ref[...] = v.astype(v_ref.dtype)
    vp_ref[...] = _pack_halves(v)


def _merge(a, w, o_f, o_b, ret, gates, h, mods, gpost, gffn, wa, ww, wr, wo, *, n_rows, n_batch, seq):
    d = h[0].shape[1]
    n_lat_tiles = n_batch * seq // TM
    rows = lambda i: (i, 0)
    lat_rows = lambda i: (jnp.minimum(i, n_lat_tiles - 1), 0)
    ctx_rows = lambda i: (jnp.maximum(i - n_lat_tiles, 0), 0)
    const = lambda i: (0, 0)
    rv = RET_HEADS * RET_V
    rg_col = (2 * RET_HEADS * RET_QK + rv) // rv
    outs = [jax.ShapeDtypeStruct((n_rows, d), F32), jax.ShapeDtypeStruct((n_rows, d), BF16),
            jax.ShapeDtypeStruct((n_rows, d // 2), jnp.uint32)]
    return pl.pallas_call(
        functools.partial(_merge_kernel, tiles_per_batch=seq // TM, n_batch=n_batch, d=d),
        grid=(n_rows // TM,),
        in_specs=[pl.BlockSpec((TM, a[0].shape[1]), lat_rows), pl.BlockSpec((TM, a[1].shape[1]), ctx_rows),
                  pl.BlockSpec((TM, w[0].shape[1]), lat_rows), pl.BlockSpec((TM, w[1].shape[1]), ctx_rows),
                  pl.BlockSpec((TM, rv), rows), pl.BlockSpec((TM, rv), rows),
                  pl.BlockSpec((TM, rv), lambda i: (i, rg_col)),
                  pl.BlockSpec((TM, 3 * d), rows), pl.BlockSpec((TM, d), lat_rows),
                  pl.BlockSpec((TM, d), lambda i: (h[2] + jnp.maximum(i - n_lat_tiles, 0), 0)),
                  pl.BlockSpec(mods.shape, const), pl.BlockSpec((1, d), const), pl.BlockSpec((1, d), const),
                  pl.BlockSpec(wa.shape, const), pl.BlockSpec(ww.shape, const),
                  pl.BlockSpec(wr.shape, const), pl.BlockSpec(wo.shape, const)],
        out_specs=[pl.BlockSpec((TM, o.shape[1]), rows) for o in outs],
        out_shape=outs,
        compiler_params=_cparams(("arbitrary",)),
        name="merge",
    )(a[0], a[1], w[0], w[1], o_f, o_b, ret, gates, h[0], h[1], mods, gpost, gffn, wa, ww, wr, wo)


def _router_kernel(v_ref, rw_ref, rb_ref, eidx_ref, rank_ref, w_ref, cnt_ref, carry_ref):
    @pl.when(pl.program_id(0) == 0)
    def _():
        carry_ref[...] = jnp.zeros_like(carry_ref)

    tm = v_ref.shape[0]
    scores = _sigmoid(_dot_nt(rw_ref[...], v_ref[...]))
    sel = scores + rb_ref[...]
    neg = -jnp.inf
    n_grp, per = N_EXPERT_GROUPS, EXPERTS_PER_GROUP

    sel3 = sel.reshape(n_grp, per, tm)
    member_id = lax.broadcasted_iota(jnp.int32, sel3.shape, 1)
    m1 = jnp.max(sel3, axis=1, keepdims=True)
    i1 = jnp.min(jnp.where(sel3 == m1, member_id, per), axis=1, keepdims=True)
    m2 = jnp.max(jnp.where(member_id == i1, neg, sel3), axis=1, keepdims=True)
    gscore = (m1 + m2).reshape(n_grp, tm)
    gid = lax.broadcasted_iota(jnp.int32, gscore.shape, 0)
    ahead = jnp.zeros(gscore.shape, jnp.int32)
    for gj in range(n_grp):
        other = gscore[gj:gj + 1, :]
        ahead = ahead + jnp.where((other > gscore) | ((other == gscore) & (gid > gj)), 1, 0)
    group_ok = (ahead < TOPK_GROUPS).reshape(n_grp, 1, tm)
    sel = jnp.where(group_ok, sel3, NEG_INF).reshape(N_EXPERTS, tm)

    eid = lax.broadcasted_iota(jnp.int32, sel.shape, 0)
    chosen = jnp.zeros(sel.shape, jnp.bool_)
    picks = []
    for _ in range(TOP_K):
        m = jnp.max(sel, axis=0, keepdims=True)
        idx = jnp.min(jnp.where(sel == m, eid, N_EXPERTS), axis=0, keepdims=True)
        hit = eid == idx
        chosen = chosen | hit
        sel = jnp.where(hit, neg, sel)
        picks.append(idx)
    w = jnp.where(chosen, scores, 0.0)
    gate = ROUTED_SCALE * w / jnp.sum(w, axis=0, keepdims=True)

    member = jnp.where(chosen, 1.0, 0.0)
    earlier = lax.broadcasted_iota(jnp.int32, (tm, tm), 0) < lax.broadcasted_iota(jnp.int32, (tm, tm), 1)
    pos = _dot(member.astype(BF16), jnp.where(earlier, 1.0, 0.0).astype(BF16)) + carry_ref[...]
    for k, idx in enumerate(picks):
        hit = eid == idx
        eidx_ref[k:k + 1, :] = idx
        rank_ref[k:k + 1, :] = jnp.sum(jnp.where(hit, pos, 0.0), axis=0, keepdims=True)
        w_ref[k:k + 1, :] = jnp.sum(jnp.where(hit, gate, 0.0), axis=0, keepdims=True)
    carry_ref[...] += jnp.sum(member, axis=1, keepdims=True)
    cnt_ref[...] = carry_ref[...]


def _router(v, rw_t, rb):
    n_rows, d = v.shape
    cols = lambda i: (0, i)
    const = lambda i: (0, 0)
    outs = [jax.ShapeDtypeStruct((TOP_K, n_rows), jnp.int32), jax.ShapeDtypeStruct((TOP_K, n_rows), F32),
            jax.ShapeDtypeStruct((TOP_K, n_rows), F32), jax.ShapeDtypeStruct((N_EXPERTS, 1), F32)]
    return pl.pallas_call(
        _router_kernel,
        grid=(n_rows // TM,),
        in_specs=[pl.BlockSpec((TM, d), lambda i: (i, 0)), pl.BlockSpec(rw_t.shape, const),
                  pl.BlockSpec(rb.shape, const)],
        out_specs=[pl.BlockSpec((TOP_K, TM), cols), pl.BlockSpec((TOP_K, TM), cols),
                   pl.BlockSpec((TOP_K, TM), cols), pl.BlockSpec((N_EXPERTS, 1), const)],
        out_shape=outs,
        scratch_shapes=[pltpu.VMEM((N_EXPERTS, 1), F32)],
        compiler_params=_cparams(("arbitrary",)),
        name="router",
    )(v, rw_t, rb)


def _slots_kernel(eidx_ref, rank_ref, cnt_ref, slot_ref):
    tm = eidx_ref.shape[1]
    eid = lax.broadcasted_iota(jnp.int32, (N_EXPERTS, tm), 0)
    for k in range(TOP_K):
        before = jnp.sum(jnp.where(eid < eidx_ref[k:k + 1, :], cnt_ref[...], 0.0), axis=0, keepdims=True)
        slot_ref[k:k + 1, :] = (before + rank_ref[k:k + 1, :]).astype(jnp.int32)


def _slots(eidx, rank, cnt):
    n_rows = eidx.shape[1]
    tm = next(c for c in (2048, 1024, 512, 256) if n_rows % c == 0)
    cols = lambda i: (0, i)
    return pl.pallas_call(
        _slots_kernel,
        grid=(n_rows // tm,),
        in_specs=[pl.BlockSpec((TOP_K, tm), cols), pl.BlockSpec((TOP_K, tm), cols),
                  pl.BlockSpec(cnt.shape, lambda i: (0, 0))],
        out_specs=pl.BlockSpec((TOP_K, tm), cols),
        out_shape=jax.ShapeDtypeStruct((TOP_K, n_rows), jnp.int32),
        compiler_params=_cparams(("arbitrary",)),
        name="slots",
    )(eidx, rank, cnt)


SC_WINDOW = 128


def _sc_mesh():
    return plsc.VectorSubcoreMesh(core_axis_name="core", subcore_axis_name="subcore")


def _sc_dispatch(rows, slot_t, n_out):
    width = rows.shape[1]
    n_chunks = slot_t.shape[1] // SC_WINDOW
    info = plsc.get_sparse_core_info()
    n_workers = info.num_cores * info.num_subcores

    @functools.partial(
        pl.kernel, mesh=_sc_mesh(),
        out_type=jax.ShapeDtypeStruct((n_out, width), rows.dtype),
        scratch_types=[pltpu.VMEM((TOP_K, SC_WINDOW), jnp.int32), pltpu.VMEM((SC_WINDOW, width), rows.dtype)],
        name="moe_dispatch")
    def run(rows_hbm, idx_hbm, out_hbm, idx_v, rows_v):
        wid = lax.axis_index("subcore") * info.num_cores + lax.axis_index("core")

        @pl.loop(wid, n_chunks, step=n_workers)
        def _(c):
            r0 = pl.multiple_of(c * SC_WINDOW, SC_WINDOW)
            pltpu.sync_copy(idx_hbm.at[:, pl.ds(r0, SC_WINDOW)], idx_v)
            pltpu.sync_copy(rows_hbm.at[pl.ds(r0, SC_WINDOW)], rows_v)
            for k in range(TOP_K):
                pltpu.sync_copy(rows_v, out_hbm.at[idx_v.at[k]])

    return run(rows, slot_t)


def _sc_collect(rows, slot_t):
    n_picks, n_rows = slot_t.shape
    width = rows.shape[1]
    n_chunks = n_rows // SC_WINDOW
    info = plsc.get_sparse_core_info()
    n_workers = info.num_cores * info.num_subcores

    @functools.partial(
        pl.kernel, mesh=_sc_mesh(),
        out_type=jax.ShapeDtypeStruct((n_picks, n_rows, width), rows.dtype),
        scratch_types=[pltpu.VMEM((TOP_K, SC_WINDOW), jnp.int32), pltpu.VMEM((SC_WINDOW, width), rows.dtype)],
        name="moe_collect")
    def run(rows_hbm, idx_hbm, out_hbm, idx_v, rows_v):
        wid = lax.axis_index("subcore") * info.num_cores + lax.axis_index("core")

        @pl.loop(wid, n_chunks, step=n_workers)
        def _(c):
            r0 = pl.multiple_of(c * SC_WINDOW, SC_WINDOW)
            pltpu.sync_copy(idx_hbm.at[:, pl.ds(r0, SC_WINDOW)], idx_v)
            for k in range(TOP_K):
                pltpu.sync_copy(rows_hbm.at[idx_v.at[k]], rows_v)
                pltpu.sync_copy(rows_v, out_hbm.at[k, pl.ds(r0, SC_WINDOW)])

    return run(rows, slot_t)


EXPERT_TILE = 512


def _work_items(cnt, n_slots):
    counts = cnt[:, 0].astype(jnp.int32)
    ends = jnp.cumsum(counts)
    n_tiles = n_slots // EXPERT_TILE
    bounds = jnp.sort(jnp.concatenate([jnp.arange(n_tiles, dtype=jnp.int32) * EXPERT_TILE, ends - counts]))
    nxt = jnp.concatenate([bounds[1:], jnp.array([n_slots], jnp.int32)])
    tile = jnp.minimum(bounds // EXPERT_TILE, n_tiles - 1)
    expert = jnp.sum((ends[None, :] <= bounds[:, None]).astype(jnp.int32), axis=1)
    expert = jnp.minimum(expert, N_EXPERTS - 1)
    return tile, expert, bounds - tile * EXPERT_TILE, nxt - tile * EXPERT_TILE


XS_RING = 3


def _experts_kernel(tile_ref, exp_ref, lo_ref, hi_ref, first_ref, next_ref, wslot_ref, xs_hbm, wg_hbm, wu_hbm, wd_hbm,
                    ys_ref, acc_ref, wgb_ref, wub_ref, wdb_ref, xbuf_ref, xsem, wg_st, wu_st, wd_st, wsem, *, layer):
    i = pl.program_id(0)
    n_items = pl.num_programs(0)
    lo, hi = lo_ref[i], hi_ref[i]

    def weight_copies(expert, slot):
        return [pltpu.make_async_copy(w_hbm.at[layer, expert], stage.at[slot], wsem.at[slot, k])
                for k, (w_hbm, stage) in enumerate(((wg_hbm, wg_st), (wu_hbm, wu_st), (wd_hbm, wd_st)))]

    @pl.when(i == 0)
    def _():
        for c in weight_copies(exp_ref[0], 0):
            c.start()

    def tile_copy(item):
        slot = lax.rem(item, XS_RING)
        row0 = pl.multiple_of(tile_ref[item] * EXPERT_TILE, EXPERT_TILE)
        return pltpu.make_async_copy(xs_hbm.at[pl.ds(row0, EXPERT_TILE)], xbuf_ref.at[slot], xsem.at[slot])

    @pl.when(i == 0)
    def _():
        for ahead in range(XS_RING - 1):
            pl.when(ahead < n_items)(lambda: tile_copy(ahead).start())

    @pl.when(i + XS_RING - 1 < n_items)
    def _():
        tile_copy(i + XS_RING - 1).start()

    tile_copy(i).wait()
    xs_ref = xbuf_ref.at[lax.rem(i, XS_RING)]

    @pl.when(first_ref[i] == 1)
    def _():
        slot = wslot_ref[i]
        for c in weight_copies(exp_ref[i], slot):
            c.wait()

        @pl.when(next_ref[i] >= 0)
        def _():
            for c in weight_copies(next_ref[i], 1 - slot):
                c.start()

        wgb_ref[...] = wg_st[slot].astype(BF16)
        wub_ref[...] = wu_st[slot].astype(BF16)
        wdb_ref[...] = wd_st[slot].astype(BF16)

    def ffn():
        x_lo, x_hi = _unpack_halves(xs_ref[...])
        x_lo, x_hi = x_lo.astype(BF16), x_hi.astype(BF16)
        n = x_lo.shape[1]
        a = _dot(x_lo, wgb_ref[0:n, :]) + _dot(x_hi, wgb_ref[n:, :])
        u = _dot(x_lo, wub_ref[0:n, :]) + _dot(x_hi, wub_ref[n:, :])
        return _dot(((a * _sigmoid(a)) * u).astype(BF16), wdb_ref[...])

    whole = (lo == 0) & (hi == EXPERT_TILE)

    @pl.when(whole)
    def _():
        ys_ref[...] = _pack_halves(ffn())

    @pl.when(jnp.logical_not(whole) & (hi > lo))
    def _():
        y = ffn()
        row = lax.broadcasted_iota(jnp.int32, y.shape, 0)
        y = jnp.where((row >= lo) & (row < hi), y, 0.0)

        @pl.when(lo == 0)
        def _():
            acc_ref[...] = y

        @pl.when((lo > 0) & (hi < EXPERT_TILE))
        def _():
            acc_ref[...] += y

        @pl.when((lo > 0) & (hi == EXPERT_TILE))
        def _():
            ys_ref[...] = _pack_halves(acc_ref[...] + y)


def _experts(xs, items, layer, exp_wg, exp_wu, exp_wd):
    n_slots, half = xs.shape
    d, hid = exp_wg.shape[-2:]
    tile, expert, lo, hi = items
    n = tile.shape[0]
    pos = jnp.arange(n)
    first = jnp.concatenate([jnp.ones((1,), jnp.int32), (expert[1:] != expert[:-1]).astype(jnp.int32)])
    later_other = (expert[None, :] != expert[:, None]) & (pos[None, :] > pos[:, None])
    nxt = jnp.where(later_other.any(axis=1), expert[jnp.argmax(later_other, axis=1)], -1).astype(jnp.int32)
    wslot = ((jnp.cumsum(first) - 1) % 2).astype(jnp.int32)
    any_space = pl.BlockSpec(memory_space=pl.ANY)
    grid_spec = pltpu.PrefetchScalarGridSpec(
        num_scalar_prefetch=7,
        grid=(n,),
        in_specs=[any_space, any_space, any_space, any_space],
        out_specs=pl.BlockSpec((EXPERT_TILE, half), lambda i, t, *_: (t[i], 0)),
        scratch_shapes=[pltpu.VMEM((EXPERT_TILE, d), F32), pltpu.VMEM((d, hid), BF16),
                        pltpu.VMEM((d, hid), BF16), pltpu.VMEM((hid, d), BF16),
                        pltpu.VMEM((XS_RING, EXPERT_TILE, half), jnp.uint32),
                        pltpu.SemaphoreType.DMA((XS_RING,)),
                        pltpu.VMEM((2, d, hid), F32), pltpu.VMEM((2, d, hid), F32), pltpu.VMEM((2, hid, d), F32),
                        pltpu.SemaphoreType.DMA((2, 3))])
    return pl.pallas_call(
        functools.partial(_experts_kernel, layer=layer),
        grid_spec=grid_spec,
        out_shape=jax.ShapeDtypeStruct((n_slots, half), jnp.uint32),
        compiler_params=_cparams(("arbitrary",)),
        name="experts",
    )(tile, expert, lo, hi, first, nxt, wslot, xs, exp_wg, exp_wu, exp_wd)


def _moe_out_tile(yg_ref, w_ref, v_ref, sg_ref, su_ref, sd_ref, h1_ref, mod_ref, gpost_ref,
                  *, tiles_per_batch, n_batch, d):
    i = pl.program_id(0)
    x = v_ref[...]
    a = _dot(x, sg_ref[...])
    f = _dot(((a * _sigmoid(a)) * _dot(x, su_ref[...])).astype(BF16), sd_ref[...])
    n = d // 2
    f_lo, f_hi = f[:, :n], f[:, n:]
    w = w_ref[...]
    for k in range(TOP_K):
        y_lo, y_hi = _unpack_halves(yg_ref[k])
        wk = w[:, k:k + 1]
        f_lo = f_lo + wk * y_lo
        f_hi = f_hi + wk * y_hi
    f = jnp.concatenate([f_lo, f_hi], axis=1)
    bi = jnp.minimum(i // tiles_per_batch, n_batch)
    g2 = mod_ref[pl.ds(bi, 1), 5 * d:6 * d]
    return h1_ref[...] + g2 * _rms(f, gpost_ref[...])


N_MOE_OUT_IN = 9
N_INPROJ_IN = 9


def _moe_out_kernel(*refs, **kw):
    refs[N_MOE_OUT_IN][...] = _moe_out_tile(*refs[:N_MOE_OUT_IN], **kw)


def _moe_out_inproj_kernel(*refs, n_batch, d, tiles_per_batch):
    dims = dict(tiles_per_batch=tiles_per_batch, n_batch=n_batch, d=d)
    h = _moe_out_tile(*refs[:N_MOE_OUT_IN], **dims)
    n_in = N_MOE_OUT_IN + N_INPROJ_IN
    refs[n_in][...] = h
    _inproj_tile(h, *refs[N_MOE_OUT_IN:n_in], *refs[n_in + 1:], **dims)


def _moe_out_specs(yg, w, v, sg, su, sd, h1, mods, gpost, tm):
    d = v.shape[1]
    rows = lambda i: (i, 0)
    const = lambda i: (0, 0)
    once = dict(pipeline_mode=pl.Buffered(1))
    return [pl.BlockSpec((TOP_K, tm, d // 2), lambda i: (0, i, 0)), pl.BlockSpec((tm, TOP_K), rows),
            pl.BlockSpec((tm, d), rows),
            pl.BlockSpec(sg.shape, const, **once), pl.BlockSpec(su.shape, const, **once),
            pl.BlockSpec(sd.shape, const, **once),
            pl.BlockSpec((tm, d), rows), pl.BlockSpec(mods.shape, const), pl.BlockSpec((1, d), const)]


def _moe_out(moe_args, *, n_batch, seq):
    v = moe_args[2]
    n_rows, d = v.shape
    return pl.pallas_call(
        functools.partial(_moe_out_kernel, tiles_per_batch=seq // TM, n_batch=n_batch, d=d),
        grid=(n_rows // TM,),
        in_specs=_moe_out_specs(*moe_args, TM),
        out_specs=pl.BlockSpec((TM, d), lambda i: (i, 0)),
        out_shape=jax.ShapeDtypeStruct((n_rows, d), F32),
        compiler_params=_cparams(("arbitrary",)),
        name="moe_out",
    )(*moe_args)


FUSED_TM = 256


def _moe_out_inproj(moe_args, inproj_args, *, layer, n_batch, seq):
    v = moe_args[2]
    n_rows, d = v.shape
    tm = FUSED_TM
    in_specs, out_specs, out_shape = _inproj_specs(n_rows, d, *inproj_args, layer=layer, tm=tm, n_batch=n_batch,
                                                   seq=seq)
    rows = lambda i: (i, 0)
    return pl.pallas_call(
        functools.partial(_moe_out_inproj_kernel, tiles_per_batch=seq // tm, n_batch=n_batch, d=d),
        grid=(n_rows // tm,),
        in_specs=_moe_out_specs(*moe_args, tm) + in_specs,
        out_specs=[pl.BlockSpec((tm, d), rows)] + out_specs,
        out_shape=[jax.ShapeDtypeStruct((n_rows, d), F32)] + out_shape,
        compiler_params=_cparams(("arbitrary",)),
        name="moe_out_inproj",
    )(*moe_args, *inproj_args)


def _moe_routed(v, vp, layer, rw, rb, exp_wg, exp_wu, exp_wd):
    n_rows = v.shape[0]
    n_slots = n_rows * TOP_K
    assert n_slots % EXPERT_TILE == 0
    eidx, rank, w_t, cnt = _router(v, rw, rb)
    slot_t = _slots(eidx, rank, cnt)
    xs = _sc_dispatch(vp, slot_t, n_slots)
    ys = _experts(xs, _work_items(cnt, n_slots), layer, exp_wg, exp_wu, exp_wd)
    return _sc_collect(ys, slot_t), w_t.T


def _rope_tables(seq):
    rows = seq // GRID_W
    row_id = np.repeat(np.arange(rows, dtype=np.float64), GRID_W)
    col_id = np.tile(np.arange(GRID_W, dtype=np.float64), rows)

    def tables(rot_dim):
        axis_dim = rot_dim // 2
        inv_freq = ROPE_BASE ** (-np.arange(0, axis_dim, 2, dtype=np.float64) / axis_dim)
        ang_r = row_id[:, None] * inv_freq[None, :]
        ang_c = col_id[:, None] * inv_freq[None, :]
        cos = np.concatenate([np.cos(ang_r), np.cos(ang_r), np.cos(ang_c), np.cos(ang_c)], axis=1)
        sin = np.concatenate([-np.sin(ang_r), np.sin(ang_r), -np.sin(ang_c), np.sin(ang_c)], axis=1)
        return cos, sin

    cos64, sin64 = tables(GQA_DIM)
    cos32, sin32 = tables(MLA_ROPE)
    ones = np.ones((seq, MLA_NOPE))
    pad = LANES - MLA_NOPE - MLA_ROPE
    cospe = np.concatenate([ones, cos32, np.ones((seq, pad))], axis=1)
    sinpe = np.concatenate([0 * ones, sin32, np.zeros((seq, pad))], axis=1)
    tab = np.concatenate([cos64, cos64, sin64, sin64, cospe, sinpe], axis=1)
    ident = np.concatenate([np.ones((TM, LANES)), np.zeros((TM, LANES)),
                            np.ones((TM, LANES)), np.zeros((TM, LANES))], axis=1)
    return jnp.asarray(np.concatenate([tab, ident], axis=0), F32)


def _w_in_moves(d):
    sizes = (MLA_Q_LORA, MLA_KV_LORA, MLA_ROPE, GQA_HEADS * GQA_DIM, GQA_KV_HEADS * GQA_DIM,
             GQA_KV_HEADS * GQA_DIM, RET_HEADS * RET_QK, RET_HEADS * RET_QK, RET_HEADS * RET_V,
             RET_HEADS * RET_V, 3 * d)
    src = [sum(sizes[:i]) for i in range(len(sizes))]
    cq, ckv, kpe, gq, gk, gv, rq, rk, rv, rg, gates = src
    moves = [(cq, C_CQ, MLA_Q_LORA, 1.0), (ckv, C_CKV, MLA_KV_LORA, 1.0), (kpe, C_KPE + MLA_NOPE, MLA_ROPE, 1.0),
             (gq, C_G, GQA_HEADS * GQA_DIM, GQA_DIM ** -0.5 * LOG2_E)]
    dst = C_G + GQA_HEADS * GQA_DIM
    for base in (gk, gv):
        for hd in range(GQA_KV_HEADS):
            for _ in range(2):
                moves.append((base + hd * GQA_DIM, dst, GQA_DIM, 1.0))
                dst += GQA_DIM
    assert dst == C_R
    for s, width, scale in ((rq, RET_HEADS * RET_QK, 1.0), (rk, RET_HEADS * RET_QK, RET_QK ** -0.5),
                            (rv, RET_HEADS * RET_V, 1.0), (rg, RET_HEADS * RET_V, 1.0), (gates, 3 * d, 1.0)):
        moves.append((s, dst, width, scale))
        dst += width
    assert dst == W_COLS
    return moves


def _pack_kernel(w_ref, o_ref, *, moves):
    o_ref[:, C_KPE:C_G] = jnp.zeros((o_ref.shape[0], C_G - C_KPE), o_ref.dtype)
    for s, t, width, scale in moves:
        piece = w_ref[:, s:s + width]
        o_ref[:, t:t + width] = (piece if scale == 1.0 else piece * scale).astype(o_ref.dtype)


def _pack_w_in(w_in):
    n_layers, d, n_cols = w_in.shape
    rows = 256
    return pl.pallas_call(
        functools.partial(_pack_kernel, moves=_w_in_moves(d)),
        grid=(n_layers, d // rows),
        in_specs=[pl.BlockSpec((None, rows, n_cols), lambda l, i: (l, i, 0))],
        out_specs=pl.BlockSpec((None, rows, W_COLS), lambda l, i: (l, i, 0)),
        out_shape=jax.ShapeDtypeStruct((n_layers, d, W_COLS), BF16),
        compiler_params=_cparams(("arbitrary", "arbitrary")),
        name="pack_w_in",
    )(w_in)


def _pack_mla_up(w_uq, w_ukv):
    r = w_uq.shape[0]
    dq = MLA_NOPE + MLA_ROPE
    wq = jnp.pad(w_uq.reshape(r, MLA_HEADS, dq), ((0, 0), (0, 0), (0, LANES - dq))).reshape(r, MLA_HEADS * LANES)
    kv = w_ukv.reshape(r, MLA_HEADS, MLA_NOPE + MLA_V)
    wk = jnp.pad(kv[:, :, :MLA_NOPE], ((0, 0), (0, 0), (0, LANES - MLA_NOPE))).reshape(r, MLA_HEADS * LANES)
    wv = kv[:, :, MLA_NOPE:]
    zeros = jnp.zeros_like(wv)
    even = jnp.concatenate([wv, zeros], axis=2)
    odd = jnp.concatenate([zeros, wv], axis=2)
    wv = jnp.where((jnp.arange(MLA_HEADS) % 2 == 0)[None, :, None], even, odd).reshape(r, MLA_HEADS * LANES)
    return wq.astype(BF16), wk.astype(BF16), wv.astype(BF16)


def kernel(x, c, ctx, c_ctx, ada_w, ada_b, norm_mix_pre, norm_mix_post, norm_ffn_pre, norm_ffn_post, w_in, mla_q_norm, mla_w_uq, mla_kv_norm, mla_w_ukv, gqa_sink, ret_decay_fwd, ret_decay_bwd, w_br_mla, w_br_gqa, w_br_ret, w_out, router_w, router_bias, exp_w_gate, exp_w_up, exp_w_down, shared_w_gate, shared_w_up, shared_w_down):
    n_batch, seq, d = x.shape
    n_ctx = ctx.shape[1]
    depth = ada_w.shape[0]
    n_lat_rows = n_batch * seq
    assert seq % TM == 0 and (n_batch * n_ctx) % TM == 0 and seq % ATT_TQ == 0 and seq % n_ctx == 0
    assert n_batch < MOD_ROWS and seq % GRID_W == 0 and d == D_MODEL

    cond = jnp.zeros((MOD_ROWS, d), F32).at[:n_batch].set(c).at[n_batch].set(c_ctx)
    mods_all = _adaln(cond, ada_w, ada_b)
    rope = _rope_tables(seq)
    h = (x.reshape(n_lat_rows, d), ctx.reshape(n_batch * n_ctx, d), 0)
    n_all_rows = n_lat_rows + n_batch * n_ctx
    row = lambda p: p.reshape(1, -1)
    dims = dict(n_batch=n_batch, seq=seq)

    w_in_packed = _pack_w_in(w_in)

    def inproj_args(l):
        return (mods_all[l], row(norm_mix_pre[l]), w_in_packed, rope, row(mla_q_norm[l]),
                row(mla_kv_norm[l]), *_pack_mla_up(mla_w_uq[l], mla_w_ukv[l]))

    projected = _inproj(h[0], h[1], inproj_args(0), layer=0, **dims)
    for l in range(depth):
        last = l == depth - 1
        mods = mods_all[l]
        mq, mk, mv, gqa, ret, gates = projected
        a = _mla_attention(mq, mk, mv, ctx=n_ctx, with_ctx_queries=not last, **dims)
        sink_tab = jnp.broadcast_to(gqa_sink[l].astype(F32)[:, None] * LOG2_E, (GQA_HEADS, LANES))
        lg = jnp.concatenate([jax.nn.log_sigmoid(ret_decay_fwd[l].astype(F32)),
                              jax.nn.log_sigmoid(ret_decay_bwd[l].astype(F32))])
        o_f, o_b, w_lat = _retention_and_window(ret, jnp.broadcast_to(lg[:, None], (2 * RET_HEADS, LANES)),
                                                gqa, sink_tab, ctx=n_ctx, **dims)
        w = (w_lat, w_lat if last else _win_call(gqa, sink_tab, ctx=n_ctx, latent_queries=False, **dims))
        n_rows = n_lat_rows if last else n_all_rows
        h1, v, vp = _merge(a, w, o_f, o_b, ret, gates, h, mods, row(norm_mix_post[l]), row(norm_ffn_pre[l]),
                           w_br_mla[l].astype(BF16), w_br_gqa[l].astype(BF16), w_br_ret[l].astype(BF16),
                           w_out[l].astype(BF16), n_rows=n_rows, **dims)
        yg, gate_w = _moe_routed(v, vp, l, router_w[l].T.astype(BF16), router_bias[l].astype(F32).reshape(-1, 1),
                                 exp_w_gate, exp_w_up, exp_w_down)
        moe_args = (yg, gate_w, v, shared_w_gate[l].astype(BF16), shared_w_up[l].astype(BF16),
                    shared_w_down[l].astype(BF16), h1, mods, row(norm_ffn_post[l]))
        if last:
            out = _moe_out(moe_args, **dims)
        else:
            stream, *projected = _moe_out_inproj(moe_args, inproj_args(l + 1), layer=l + 1, **dims)
            h = (stream, stream, n_lat_rows // TM)
    return out[:n_lat_rows].reshape(n_batch, seq, d)
```

```python
import functools

import numpy as np
import jax
import jax.numpy as jnp
from jax import lax
from jax.experimental import pallas as pl
from jax.experimental.pallas import tpu as pltpu
from jax.experimental.pallas import tpu_sc as plsc

F32 = jnp.float32
BF16 = jnp.bfloat16

GRID_W = 64
ROPE_BASE = 10000.0
NORM_EPS = 1e-6
NEG_INF = -1e30
LOG2_E = 1.4426950408889634
N_MOD = 6
MLA_HEADS, MLA_NOPE, MLA_ROPE, MLA_V = 8, 64, 32, 64
MLA_Q_LORA, MLA_KV_LORA = 256, 256
GQA_HEADS, GQA_KV_HEADS, GQA_DIM, WINDOW = 8, 2, 64, 128
RET_HEADS, RET_QK, RET_V, RET_CHUNK = 4, 64, 128, 128
N_EXPERTS, N_EXPERT_GROUPS, TOPK_GROUPS, TOP_K = 64, 8, 4, 8
EXPERTS_PER_GROUP = N_EXPERTS // N_EXPERT_GROUPS
ROUTED_SCALE = 2.5

LANES = 128
TM = 512
ATT_TQ = 512
WIN_TQ = 256
RET_TILE = 256
MOD_ROWS = 8
V7X_VMEM_LIMIT = 56 * 1024 * 1024

D_MODEL = 1024
C_CQ = 0
C_CKV = C_CQ + MLA_Q_LORA
C_KPE = C_CKV + MLA_KV_LORA
C_G = C_KPE + LANES
C_R = C_G + GQA_HEADS * GQA_DIM + 4 * GQA_KV_HEADS * GQA_DIM
C_GATE = C_R + 2 * RET_HEADS * RET_QK + 2 * RET_HEADS * RET_V
W_COLS = C_GATE + 3 * D_MODEL


def _cparams(sem):
    return pltpu.CompilerParams(dimension_semantics=sem, vmem_limit_bytes=V7X_VMEM_LIMIT)


def _rms(x, g):
    return x * lax.rsqrt(jnp.mean(x * x, axis=-1, keepdims=True) + NORM_EPS) * g


def _sigmoid(x):
    return 0.5 * jnp.tanh(0.5 * x) + 0.5


def _dot(a, b):
    return jnp.dot(a, b, preferred_element_type=F32)


def _dot_nt(a, b):
    return lax.dot_general(a, b, (((1,), (1,)), ((), ())), preferred_element_type=F32)


def _dot_tn(a, b):
    return lax.dot_general(a, b, (((0,), (0,)), ((), ())), preferred_element_type=F32)


def _rope(x, cos, sin, half):
    n = x.shape[-1]
    reps = n // LANES
    if reps > 1:
        cos = jnp.concatenate([cos] * reps, axis=1)
        sin = jnp.concatenate([sin] * reps, axis=1)
    lane = lax.broadcasted_iota(jnp.int32, x.shape, 1)
    up = pltpu.roll(x, half, 1)
    dn = pltpu.roll(x, n - half, 1)
    partner = jnp.where((lane & (2 * half - 1)) < half, dn, up)
    return x * cos + partner * sin


def _lane_lo(shape):
    return (lax.broadcasted_iota(jnp.int32, shape, 1) & (LANES - 1)) < (LANES // 2)


def _pack_halves(x):
    n = x.shape[1] // 2
    bits = lambda t: lax.bitcast_convert_type(t.astype(BF16).astype(F32), jnp.uint32)
    return (bits(x[:, :n]) >> 16) | bits(x[:, n:])


def _unpack_halves(p):
    lo = lax.bitcast_convert_type(p << 16, F32)
    hi = lax.bitcast_convert_type(p & jnp.uint32(0xFFFF0000), F32)
    return lo, hi


def _ada_kernel(c_ref, w_ref, b_ref, o_ref):
    c = c_ref[...]
    s = c * _sigmoid(c)
    o_ref[...] = _dot(s.astype(BF16), w_ref[...].astype(BF16)) + b_ref[...]


def _adaln(cond, ada_w, ada_b):
    n_layers, d, n = ada_w.shape
    tn = 1024
    return pl.pallas_call(
        _ada_kernel,
        grid=(n_layers, n // tn),
        in_specs=[pl.BlockSpec((MOD_ROWS, d), lambda l, j: (0, 0)),
                  pl.BlockSpec((None, d, tn), lambda l, j: (l, 0, j)),
                  pl.BlockSpec((None, 1, tn), lambda l, j: (l, 0, j))],
        out_specs=pl.BlockSpec((None, MOD_ROWS, tn), lambda l, j: (l, 0, j)),
        out_shape=jax.ShapeDtypeStruct((n_layers, MOD_ROWS, n), F32),
        compiler_params=_cparams(("arbitrary", "arbitrary")),
        name="adaln",
    )(cond, ada_w, ada_b.reshape(n_layers, 1, n))


def _inproj_tile(h, mod_ref, gpre_ref, w_ref, rope_ref, qn_ref, kvn_ref, wuq_ref, wuk_ref, wuv_ref,
                 mq_ref, mk_ref, mv_ref, gqa_ref, ret_ref, gate_ref, *, tiles_per_batch, n_batch, d):
    i = pl.program_id(0)
    bi = jnp.minimum(i // tiles_per_batch, n_batch)
    sh = mod_ref[pl.ds(bi, 1), 0:d]
    sc = mod_ref[pl.ds(bi, 1), d:2 * d]
    u = (_rms(h, gpre_ref[...]) * (1.0 + sc) + sh).astype(BF16)

    cos64 = rope_ref[:, 0:LANES]
    sin64 = rope_ref[:, LANES:2 * LANES]
    cospe = rope_ref[:, 2 * LANES:3 * LANES]
    sinpe = rope_ref[:, 3 * LANES:4 * LANES]

    c = _dot(u, w_ref[:, C_CQ:C_G])
    kpe = _rope(c[:, C_KPE:C_G], cospe, sinpe, MLA_ROPE // 4)
    qn = _rms(c[:, C_CQ:C_CKV], qn_ref[...]).astype(BF16)
    q = _rope(_dot(qn, wuq_ref[...]), cospe, sinpe, MLA_ROPE // 4)
    mq_ref[...] = (q * ((MLA_NOPE + MLA_ROPE) ** -0.5 * LOG2_E)).astype(mq_ref.dtype)
    kvn = _rms(c[:, C_CKV:C_KPE], kvn_ref[...]).astype(BF16)
    k = _dot(kvn, wuk_ref[...]) + jnp.concatenate([kpe] * MLA_HEADS, axis=1)
    mk_ref[...] = k.astype(mk_ref.dtype)
    v = _dot(kvn, wuv_ref[...])
    lane = lax.broadcasted_iota(jnp.int32, v.shape, 1)
    value_lane = ((lane & (LANES - 1)) < MLA_V) == (((lane >> (LANES.bit_length() - 1)) & 1) == 0)
    mv_ref[...] = jnp.where(value_lane, v, 1.0).astype(mv_ref.dtype)

    g = _dot(u, w_ref[:, C_G:C_R])
    n_qk = GQA_HEADS * GQA_DIM + 2 * GQA_KV_HEADS * GQA_DIM
    gqa_ref[:, 0:n_qk] = _rope(g[:, 0:n_qk], cos64, sin64, GQA_DIM // 4).astype(gqa_ref.dtype)
    gqa_ref[:, n_qk:] = g[:, n_qk:].astype(gqa_ref.dtype)

    r = _dot(u, w_ref[:, C_R:C_GATE])
    n_qk = 2 * RET_HEADS * RET_QK
    ret_ref[:, 0:n_qk] = _rope(r[:, 0:n_qk], cos64, sin64, RET_QK // 4).astype(ret_ref.dtype)
    ret_ref[:, n_qk:] = r[:, n_qk:].astype(ret_ref.dtype)

    gate_ref[...] = _dot(u, w_ref[:, C_GATE:W_COLS]).astype(gate_ref.dtype)


def _inproj_kernel(hl_ref, hc_ref, *refs, tiles_per_batch, n_batch, d):
    latent = pl.program_id(0) < tiles_per_batch * n_batch
    _inproj_tile(jnp.where(latent, hl_ref[...], hc_ref[...]), *refs,
                 tiles_per_batch=tiles_per_batch, n_batch=n_batch, d=d)


def _inproj_specs(t, d, mods, gpre, w_all, rope, qn, kvn, wuq, wuk, wuv, *, layer, tm, n_batch, seq):
    tiles_per_batch = seq // tm
    n_lat_tiles = n_batch * tiles_per_batch
    const = lambda i: (0, 0)
    rows = lambda i: (i, 0)
    rope_idx = lambda i: (jnp.where(i < n_lat_tiles, i % tiles_per_batch, tiles_per_batch), 0)
    once = dict(pipeline_mode=pl.Buffered(1))
    hq = MLA_HEADS * LANES
    outs = [jax.ShapeDtypeStruct((t, hq), BF16), jax.ShapeDtypeStruct((t, hq), BF16),
            jax.ShapeDtypeStruct((t, hq), BF16),
            jax.ShapeDtypeStruct((t, C_R - C_G), BF16),
            jax.ShapeDtypeStruct((t, C_GATE - C_R), F32),
            jax.ShapeDtypeStruct((t, W_COLS - C_GATE), BF16)]
    in_specs = [pl.BlockSpec(mods.shape, const),
                pl.BlockSpec((1, d), const),
                pl.BlockSpec((None,) + w_all.shape[1:], lambda i: (layer, 0, 0), **once),
                pl.BlockSpec((tm, 4 * LANES), rope_idx),
                pl.BlockSpec(qn.shape, const), pl.BlockSpec(kvn.shape, const),
                pl.BlockSpec(wuq.shape, const, **once), pl.BlockSpec(wuk.shape, const, **once),
                pl.BlockSpec(wuv.shape, const, **once)]
    return in_specs, [pl.BlockSpec((tm, o.shape[1]), rows) for o in outs], outs


def _inproj(h_lat, h_ctx, inproj_args, *, layer, n_batch, seq):
    d = h_lat.shape[1]
    t = h_lat.shape[0] + h_ctx.shape[0]
    n_lat_tiles = h_lat.shape[0] // TM
    in_specs, out_specs, out_shape = _inproj_specs(t, d, *inproj_args, layer=layer, tm=TM, n_batch=n_batch, seq=seq)
    return pl.pallas_call(
        functools.partial(_inproj_kernel, tiles_per_batch=seq // TM, n_batch=n_batch, d=d),
        grid=(t // TM,),
        in_specs=[pl.BlockSpec((TM, d), lambda i: (jnp.minimum(i, n_lat_tiles - 1), 0)),
                  pl.BlockSpec((TM, d), lambda i: (jnp.maximum(i - n_lat_tiles, 0), 0))] + in_specs,
        out_specs=out_specs,
        out_shape=out_shape,
        compiler_params=_cparams(("arbitrary",)),
        name="inproj",
    )(h_lat, h_ctx, *inproj_args)


MLA_HEADS_PER_STEP = 4


def _mla_kernel(q_ref, kl_ref, kc_ref, vl_ref, vc_ref, o_ref, s_ref, p_ref, *, with_lat):
    n_ctx = kc_ref.shape[0]

    def body(with_lat):
        n_keys = n_ctx + (kl_ref.shape[0] if with_lat else 0)

        def scores(h):
            sl = slice(h * LANES, (h + 1) * LANES)
            s_ref[h % 2, :, 0:n_ctx] = _dot_nt(q_ref[:, sl], kc_ref[:, sl])
            if with_lat:
                s_ref[h % 2, :, n_ctx:n_keys] = _dot_nt(q_ref[:, sl], kl_ref[:, sl])

        def probs(h):
            s = s_ref[h % 2, :, 0:n_keys]
            p_ref[h % 2, :, 0:n_keys] = jnp.exp2(s - jnp.max(s, axis=-1, keepdims=True)).astype(BF16)

        def weighted(h):
            sl = slice(h * LANES, (h + 1) * LANES)
            o = _dot(p_ref[h % 2, :, 0:n_ctx], vc_ref[:, sl])
            if with_lat:
                o = o + _dot(p_ref[h % 2, :, n_ctx:n_keys], vl_ref[:, sl])
            return o / pltpu.roll(o, LANES // 2, 1)

        outs = [None] * MLA_HEADS_PER_STEP
        scores(0)
        for h in range(MLA_HEADS_PER_STEP):
            if h + 1 < MLA_HEADS_PER_STEP:
                scores(h + 1)
            probs(h)
            outs[h] = weighted(h)
        for pr in range(MLA_HEADS_PER_STEP // 2):
            even, odd = outs[2 * pr], outs[2 * pr + 1]
            o_ref[:, pr * LANES:(pr + 1) * LANES] = jnp.where(_lane_lo(even.shape), even, odd).astype(o_ref.dtype)

    body(with_lat)


def _mla_call(mq, mk, mv, *, n_batch, seq, ctx, latent_queries):
    hps = MLA_HEADS_PER_STEP
    ctx_blk0 = n_batch * seq // ctx
    tq = ATT_TQ if latent_queries else ctx
    nq = seq // tq if latent_queries else 1
    q_blk0 = 0 if latent_queries else n_batch * seq // tq
    q_idx = lambda b, g, i: (q_blk0 + b * nq + i, g)
    ctx_idx = lambda b, g, i: (ctx_blk0 + b, g)
    lat_idx = (lambda b, g, i: (b, g)) if latent_queries else ctx_idx
    n_lat = seq if latent_queries else ctx
    n_keys = ctx + (seq if latent_queries else 0)
    in_specs = [pl.BlockSpec((tq, hps * LANES), q_idx),
                pl.BlockSpec((n_lat, hps * LANES), lat_idx), pl.BlockSpec((ctx, hps * LANES), ctx_idx),
                pl.BlockSpec((n_lat, hps * LANES), lat_idx), pl.BlockSpec((ctx, hps * LANES), ctx_idx)]
    return pl.pallas_call(
        functools.partial(_mla_kernel, with_lat=latent_queries),
        grid=(n_batch, MLA_HEADS // hps, nq),
        in_specs=in_specs,
        out_specs=pl.BlockSpec((tq, hps * MLA_V), lambda b, g, i: (b * nq + i, g)),
        out_shape=jax.ShapeDtypeStruct((n_batch * nq * tq, MLA_HEADS * MLA_V), BF16),
        scratch_shapes=[pltpu.VMEM((2, tq, n_keys), F32), pltpu.VMEM((2, tq, n_keys), BF16)],
        compiler_params=_cparams(("arbitrary", "arbitrary", "arbitrary")),
        name="mla_attn" if latent_queries else "mla_attn_ctx",
    )(mq, mk, mk, mv, mv)


def _mla_attention(mq, mk, mv, *, n_batch, seq, ctx, with_ctx_queries):
    dims = dict(n_batch=n_batch, seq=seq, ctx=ctx)
    lat = _mla_call(mq, mk, mv, latent_queries=True, **dims)
    return lat, (_mla_call(mq, mk, mv, latent_queries=False, **dims) if with_ctx_queries else lat)


def _win_kernel(q_ref, kp_ref, kcur_ref, kn_ref, vp_ref, vcur_ref, vn_ref, kc_ref, vc_ref, sink_ref, o_ref,
                s_ref, p_ref, *, seq, with_lat, tile=None):
    i = pl.program_id(1) if tile is None else tile
    tq = q_ref.shape[0]
    group = GQA_HEADS // GQA_KV_HEADS

    def body(with_lat):
        n_ctx = kc_ref.shape[0]
        n_keys = n_ctx + (tq + 2 * WINDOW if with_lat else 0)
        if with_lat:
            q_pos = i * tq + lax.broadcasted_iota(jnp.int32, (tq, n_keys), 0)
            k_pos = i * tq - WINDOW - n_ctx + lax.broadcasted_iota(jnp.int32, (tq, n_keys), 1)
            in_band = (jnp.abs(q_pos - k_pos) <= WINDOW) & (k_pos >= 0) & (k_pos < seq)
            valid = in_band | (lax.broadcasted_iota(jnp.int32, (tq, n_keys), 1) < n_ctx)
        lo = _lane_lo((tq, LANES))
        lo_k = _lane_lo((n_keys, LANES))
        keys, values = [], []
        for kv in range(GQA_KV_HEADS):
            sl = slice(kv * LANES, (kv + 1) * LANES)
            if with_lat:
                k_all = jnp.concatenate([kc_ref[:, sl], kp_ref[:, sl], kcur_ref[:, sl], kn_ref[:, sl]], axis=0)
                v_all = jnp.concatenate([vc_ref[:, sl], vp_ref[:, sl], vcur_ref[:, sl], vn_ref[:, sl]], axis=0)
            else:
                k_all, v_all = kc_ref[:, sl], vc_ref[:, sl]
            keys.append(k_all)
            one = jnp.ones_like(v_all)
            values.append((jnp.where(lo_k, v_all, one), jnp.where(lo_k, one, v_all)))

        def scores(hd):
            kv, pair = hd // group, hd // 2
            qp = q_ref[:, pair * LANES:(pair + 1) * LANES]
            qm = jnp.where(lo if hd % 2 == 0 else jnp.logical_not(lo), qp, jnp.zeros_like(qp))
            s = _dot_nt(qm, keys[kv])
            s_ref[hd % 2, :, 0:n_keys] = jnp.where(valid, s, NEG_INF) if with_lat else s

        def probs(hd):
            s = s_ref[hd % 2, :, 0:n_keys]
            m = jnp.maximum(jnp.max(s, axis=-1, keepdims=True), sink_ref[hd:hd + 1, 0:1])
            p_ref[hd % 2, :, 0:n_keys] = jnp.exp2(s - m).astype(BF16)
            return jnp.exp2(sink_ref[hd:hd + 1, 0:1] - m)

        def weighted(hd, sink_term):
            o = _dot(p_ref[hd % 2, :, 0:n_keys], values[hd // group][hd % 2])
            return o / (pltpu.roll(o, LANES // 2, 1) + sink_term)

        outs = [None] * GQA_HEADS
        scores(0)
        for hd in range(GQA_HEADS):
            if hd + 1 < GQA_HEADS:
                scores(hd + 1)
            outs[hd] = weighted(hd, probs(hd))
        for pair in range(GQA_HEADS // 2):
            o_ref[:, pair * LANES:(pair + 1) * LANES] = jnp.where(
                lo, outs[2 * pair], outs[2 * pair + 1]).astype(o_ref.dtype)

    body(with_lat)


def _win_call(gqa, sink_tab, *, n_batch, seq, ctx, latent_queries):
    tq = WIN_TQ if latent_queries else ctx
    nq = seq // tq if latent_queries else 1
    q_blk0 = 0 if latent_queries else n_batch * seq // tq
    per_tile = tq // WINDOW
    n_win_blocks = seq // WINDOW
    ctx_blk0 = n_batch * seq // ctx
    nqk = GQA_HEADS * GQA_DIM
    kw = 2 * GQA_KV_HEADS * GQA_DIM
    k_col, v_col = nqk // kw, nqk // kw + 1
    q_idx = lambda b, i: (q_blk0 + b * nq + i, 0)
    cidx = lambda col: (lambda b, i: (ctx_blk0 + b, col))
    if latent_queries:
        cur = lambda col: (lambda b, i: (b * nq + i, col))
        prev = lambda col: (lambda b, i: (b * n_win_blocks + jnp.maximum(per_tile * i - 1, 0), col))
        nxt = lambda col: (lambda b, i: (b * n_win_blocks + jnp.minimum(per_tile * (i + 1), n_win_blocks - 1), col))
        band = [((WINDOW, kw), prev), ((tq, kw), cur), ((WINDOW, kw), nxt)]
    else:
        band = [((ctx, kw), cidx)] * 3
    n_keys = ctx + (tq + 2 * WINDOW if latent_queries else 0)
    in_specs = ([pl.BlockSpec((tq, nqk), q_idx)]
                + [pl.BlockSpec(shape, idx(k_col)) for shape, idx in band]
                + [pl.BlockSpec(shape, idx(v_col)) for shape, idx in band]
                + [pl.BlockSpec((ctx, kw), cidx(k_col)), pl.BlockSpec((ctx, kw), cidx(v_col)),
                   pl.BlockSpec(sink_tab.shape, lambda b, i: (0, 0))])
    return pl.pallas_call(
        functools.partial(_win_kernel, seq=seq, with_lat=latent_queries),
        grid=(n_batch, nq),
        in_specs=in_specs,
        out_specs=pl.BlockSpec((tq, nqk), lambda b, i: (b * nq + i, 0)),
        out_shape=jax.ShapeDtypeStruct((n_batch * nq * tq, nqk), BF16),
        scratch_shapes=[pltpu.VMEM((2, tq, n_keys), F32), pltpu.VMEM((2, tq, n_keys), BF16)],
        compiler_params=_cparams(("arbitrary", "arbitrary")),
        name="win_attn" if latent_queries else "win_attn_ctx",
    )(*([gqa] * 9 + [sink_tab]))


def _window_attention(gqa, sink_tab, *, n_batch, seq, ctx, with_ctx_queries):
    dims = dict(n_batch=n_batch, seq=seq, ctx=ctx)
    lat = _win_call(gqa, sink_tab, latent_queries=True, **dims)
    return lat, (_win_call(gqa, sink_tab, latent_queries=False, **dims) if with_ctx_queries else lat)


def _ret_kernel(f_ref, b_ref, lg_ref, of_ref, ob_ref, sf_ref, sb_ref, qdec_ref, kdec_ref, cdec_ref, inner_ref):
    @pl.when(pl.program_id(1) == 0)
    def _():
        sf_ref[...] = jnp.zeros_like(sf_ref)
        sb_ref[...] = jnp.zeros_like(sb_ref)

    L = f_ref.shape[0]
    lo = _lane_lo((L, LANES))
    srow_lo = lax.broadcasted_iota(jnp.int32, (LANES, LANES), 0) < RET_QK
    nq = RET_HEADS * RET_QK
    n_pairs = RET_HEADS // 2

    @pl.when(pl.program_id(1) == 0)
    def _():
        ii = lax.broadcasted_iota(jnp.int32, (L, L), 0)
        jj = lax.broadcasted_iota(jnp.int32, (L, L), 1)
        row = lax.broadcasted_iota(jnp.int32, (L, LANES), 0).astype(F32)
        for direction, forward in enumerate((True, False)):
            dist = ii - jj if forward else jj - ii
            distf = jnp.maximum(dist, 0).astype(F32)
            for pr in range(n_pairs):
                r0 = direction * RET_HEADS + 2 * pr
                lg = [lg_ref[r0 + e:r0 + e + 1, :] for e in range(2)]
                lg_lane = jnp.where(lo, lg[0], lg[1])
                qdec_ref[direction * n_pairs + pr] = jnp.exp(lg_lane * ((row + 1.0) if forward else (L - row)))
                kdec_ref[direction * n_pairs + pr] = jnp.exp(lg_lane * ((L - 1.0 - row) if forward else row))
                cdec_ref[direction * n_pairs + pr] = jnp.where(srow_lo, jnp.exp(lg[0] * float(L)),
                                                               jnp.exp(lg[1] * float(L)))
                for e in range(2):
                    inner_ref[r0 + e] = jnp.where(dist >= 0, jnp.exp(lg[e][:, 0:1] * distf), 0.0)

    def scan_chunk(x_ref, o_ref, s_ref, direction):
        for pr in range(n_pairs):
            q = x_ref[:, pr * LANES:(pr + 1) * LANES].astype(F32)
            k = x_ref[:, nq + pr * LANES:nq + (pr + 1) * LANES].astype(F32)
            qd = q * qdec_ref[direction * n_pairs + pr]
            kdb = (k * kdec_ref[direction * n_pairs + pr]).astype(BF16)
            kb = k.astype(BF16)
            state = s_ref[pr]
            state_b = state.astype(BF16)
            upd = []
            for e in range(2):
                hd = 2 * pr + e
                keep = lo if e == 0 else jnp.logical_not(lo)
                v = x_ref[:, 2 * nq + hd * RET_V:2 * nq + (hd + 1) * RET_V].astype(BF16)
                attn = _dot_nt(jnp.where(keep, q, 0.0).astype(BF16), kb) * inner_ref[direction * RET_HEADS + hd]
                o = _dot(attn.astype(BF16), v) + _dot(jnp.where(keep, qd, 0.0).astype(BF16), state_b)
                o_ref[:, hd * RET_V:(hd + 1) * RET_V] = o
                upd.append(_dot_tn(kdb, v))
            s_ref[pr] = state * cdec_ref[direction * n_pairs + pr] + jnp.where(srow_lo, upd[0], upd[1])

    scan_chunk(f_ref, of_ref, sf_ref, 0)
    scan_chunk(b_ref, ob_ref, sb_ref, 1)


def _retention(ret, lg_tab, *, n_batch, seq, ctx):
    t = ret.shape[0]
    L = RET_TILE
    assert seq % L == 0 and ctx % L == 0
    n_lat, n_ctx = seq // L, ctx // L
    ctx0 = n_batch * n_lat
    width = 2 * RET_HEADS * RET_QK + RET_HEADS * RET_V
    fwd = lambda b, s: (jnp.where(s < n_ctx, ctx0 + b * n_ctx + s, b * n_lat + s - n_ctx), 0)
    bwd = lambda b, s: (jnp.where(s < n_ctx, ctx0 + b * n_ctx + n_ctx - 1 - s, b * n_lat + n_lat - 1 - (s - n_ctx)), 0)
    out = jax.ShapeDtypeStruct((t, RET_HEADS * RET_V), F32)
    return pl.pallas_call(
        _ret_kernel,
        grid=(n_batch, n_lat + n_ctx),
        in_specs=[pl.BlockSpec((L, width), fwd), pl.BlockSpec((L, width), bwd),
                  pl.BlockSpec(lg_tab.shape, lambda b, s: (0, 0))],
        out_specs=[pl.BlockSpec((L, RET_HEADS * RET_V), fwd), pl.BlockSpec((L, RET_HEADS * RET_V), bwd)],
        out_shape=[out, out],
        scratch_shapes=[pltpu.VMEM((RET_HEADS // 2, LANES, RET_V), F32),
                        pltpu.VMEM((RET_HEADS // 2, LANES, RET_V), F32),
                        pltpu.VMEM((RET_HEADS, L, LANES), F32), pltpu.VMEM((RET_HEADS, L, LANES), F32),
                        pltpu.VMEM((RET_HEADS, LANES, RET_V), F32), pltpu.VMEM((2 * RET_HEADS, L, L), F32)],
        compiler_params=_cparams(("arbitrary", "arbitrary")),
        name="retention",
    )(ret, ret, lg_tab)


def _ret_win_kernel(f_ref, b_ref, lg_ref, q_ref, kp_ref, kcur_ref, kn_ref, vp_ref, vcur_ref, vn_ref, kc_ref, vc_ref,
                    sink_ref, of_ref, ob_ref, w_ref, sf_ref, sb_ref, qdec_ref, kdec_ref, cdec_ref, inner_ref,
                    s_ref, p_ref, *, seq, n_win_tiles):
    _ret_kernel(f_ref, b_ref, lg_ref, of_ref, ob_ref, sf_ref, sb_ref, qdec_ref, kdec_ref, cdec_ref, inner_ref)
    _win_kernel(q_ref, kp_ref, kcur_ref, kn_ref, vp_ref, vcur_ref, vn_ref, kc_ref, vc_ref, sink_ref, w_ref,
                s_ref, p_ref, seq=seq, with_lat=True, tile=jnp.minimum(pl.program_id(1), n_win_tiles - 1))


def _retention_and_window(ret, lg_tab, gqa, sink_tab, *, n_batch, seq, ctx):
    t = ret.shape[0]
    L = RET_TILE
    assert L == WIN_TQ and ctx == L and seq % L == 0
    n_lat = seq // L
    ctx0 = n_batch * n_lat
    width = 2 * RET_HEADS * RET_QK + RET_HEADS * RET_V
    fwd = lambda b, s: (jnp.where(s < 1, ctx0 + b, b * n_lat + s - 1), 0)
    bwd = lambda b, s: (jnp.where(s < 1, ctx0 + b, b * n_lat + n_lat - s), 0)
    per_tile = L // WINDOW
    n_win_blocks = seq // WINDOW
    nqk = GQA_HEADS * GQA_DIM
    kw = 2 * GQA_KV_HEADS * GQA_DIM
    k_col, v_col = nqk // kw, nqk // kw + 1
    tile = lambda s: jnp.minimum(s, n_lat - 1)
    q_idx = lambda b, s: (b * n_lat + tile(s), 0)
    cur = lambda col: (lambda b, s: (b * n_lat + tile(s), col))
    prev = lambda col: (lambda b, s: (b * n_win_blocks + jnp.maximum(per_tile * tile(s) - 1, 0), col))
    nxt = lambda col: (lambda b, s: (b * n_win_blocks + jnp.minimum(per_tile * (tile(s) + 1), n_win_blocks - 1), col))
    cidx = lambda col: (lambda b, s: (ctx0 + b, col))
    band = [((WINDOW, kw), prev), ((L, kw), cur), ((WINDOW, kw), nxt)]
    n_keys = ctx + L + 2 * WINDOW
    o_shape = jax.ShapeDtypeStruct((t, RET_HEADS * RET_V), F32)
    return pl.pallas_call(
        functools.partial(_ret_win_kernel, seq=seq, n_win_tiles=n_lat),
        grid=(n_batch, n_lat + 1),
        in_specs=([pl.BlockSpec((L, width), fwd), pl.BlockSpec((L, width), bwd),
                   pl.BlockSpec(lg_tab.shape, lambda b, s: (0, 0)), pl.BlockSpec((L, nqk), q_idx)]
                  + [pl.BlockSpec(shape, idx(k_col)) for shape, idx in band]
                  + [pl.BlockSpec(shape, idx(v_col)) for shape, idx in band]
                  + [pl.BlockSpec((ctx, kw), cidx(k_col)), pl.BlockSpec((ctx, kw), cidx(v_col)),
                     pl.BlockSpec(sink_tab.shape, lambda b, s: (0, 0))]),
        out_specs=[pl.BlockSpec((L, RET_HEADS * RET_V), fwd), pl.BlockSpec((L, RET_HEADS * RET_V), bwd),
                   pl.BlockSpec((L, nqk), q_idx)],
        out_shape=[o_shape, o_shape, jax.ShapeDtypeStruct((n_batch * seq, nqk), BF16)],
        scratch_shapes=[pltpu.VMEM((RET_HEADS // 2, LANES, RET_V), F32),
                        pltpu.VMEM((RET_HEADS // 2, LANES, RET_V), F32),
                        pltpu.VMEM((RET_HEADS, L, LANES), F32), pltpu.VMEM((RET_HEADS, L, LANES), F32),
                        pltpu.VMEM((RET_HEADS, LANES, RET_V), F32), pltpu.VMEM((2 * RET_HEADS, L, L), F32),
                        pltpu.VMEM((2, L, n_keys), F32), pltpu.VMEM((2, L, n_keys), BF16)],
        compiler_params=_cparams(("arbitrary", "arbitrary")),
        name="retention_window",
    )(ret, ret, lg_tab, *([gqa] * 9), sink_tab)


def _merge_kernel(al_ref, ac_ref, wl_ref, wc_ref, of_ref, ob_ref, rg_ref, gt_ref, hl_ref, hc_ref, mod_ref,
                  gpost_ref, gffn_ref, wa_ref, ww_ref, wr_ref, wo_ref, h1_ref, v_ref, vp_ref,
                  *, tiles_per_batch, n_batch, d):
    i = pl.program_id(0)
    bi = jnp.minimum(i // tiles_per_batch, n_batch)
    latent = i < tiles_per_batch * n_batch
    a_tile = jnp.where(latent, al_ref[...], ac_ref[...])
    w_tile = jnp.where(latent, wl_ref[...], wc_ref[...])
    h_tile = jnp.where(latent, hl_ref[...], hc_ref[...])
    o = of_ref[...] + ob_ref[...]
    normed = []
    for hd in range(RET_HEADS):
        oh = o[:, hd * RET_V:(hd + 1) * RET_V]
        dev = oh - jnp.mean(oh, axis=-1, keepdims=True)
        normed.append(dev * lax.rsqrt(jnp.mean(dev * dev, axis=-1, keepdims=True) + NORM_EPS))
    g = rg_ref[...].astype(F32)
    r = (g * _sigmoid(g)) * jnp.concatenate(normed, axis=1)
    y = (_sigmoid(gt_ref[:, 0:d].astype(F32)) * _dot(a_tile, wa_ref[...])
         + _sigmoid(gt_ref[:, d:2 * d].astype(F32)) * _dot(w_tile, ww_ref[...])
         + _sigmoid(gt_ref[:, 2 * d:3 * d].astype(F32)) * _dot(r.astype(BF16), wr_ref[...]))
    z = _dot(y.astype(BF16), wo_ref[...])
    g1 = mod_ref[pl.ds(bi, 1), 2 * d:3 * d]
    sh2 = mod_ref[pl.ds(bi, 1), 3 * d:4 * d]
    sc2 = mod_ref[pl.ds(bi, 1), 4 * d:5 * d]
    h1 = h_tile + g1 * _rms(z, gpost_ref[...])
    h1_ref[...] = h1
    v = _rms(h1, gffn_ref[...]) * (1.0 + sc2) + sh2
    v_ref[...] = v.astype(v_ref.dtype)
    vp_ref[...] = _pack_halves(v)


def _merge(a, w, o_f, o_b, ret, gates, h, mods, gpost, gffn, wa, ww, wr, wo, *, n_rows, n_batch, seq):
    d = h[0].shape[1]
    n_lat_tiles = n_batch * seq // TM
    rows = lambda i: (i, 0)
    lat_rows = lambda i: (jnp.minimum(i, n_lat_tiles - 1), 0)
    ctx_rows = lambda i: (jnp.maximum(i - n_lat_tiles, 0), 0)
    const = lambda i: (0, 0)
    rv = RET_HEADS * RET_V
    rg_col = (2 * RET_HEADS * RET_QK + rv) // rv
    outs = [jax.ShapeDtypeStruct((n_rows, d), F32), jax.ShapeDtypeStruct((n_rows, d), BF16),
            jax.ShapeDtypeStruct((n_rows, d // 2), jnp.uint32)]
    return pl.pallas_call(
        functools.partial(_merge_kernel, tiles_per_batch=seq // TM, n_batch=n_batch, d=d),
        grid=(n_rows // TM,),
        in_specs=[pl.BlockSpec((TM, a[0].shape[1]), lat_rows), pl.BlockSpec((TM, a[1].shape[1]), ctx_rows),
                  pl.BlockSpec((TM, w[0].shape[1]), lat_rows), pl.BlockSpec((TM, w[1].shape[1]), ctx_rows),
                  pl.BlockSpec((TM, rv), rows), pl.BlockSpec((TM, rv), rows),
                  pl.BlockSpec((TM, rv), lambda i: (i, rg_col)),
                  pl.BlockSpec((TM, 3 * d), rows), pl.BlockSpec((TM, d), lat_rows),
                  pl.BlockSpec((TM, d), lambda i: (h[2] + jnp.maximum(i - n_lat_tiles, 0), 0)),
                  pl.BlockSpec(mods.shape, const), pl.BlockSpec((1, d), const), pl.BlockSpec((1, d), const),
                  pl.BlockSpec(wa.shape, const), pl.BlockSpec(ww.shape, const),
                  pl.BlockSpec(wr.shape, const), pl.BlockSpec(wo.shape, const)],
        out_specs=[pl.BlockSpec((TM, o.shape[1]), rows) for o in outs],
        out_shape=outs,
        compiler_params=_cparams(("arbitrary",)),
        name="merge",
    )(a[0], a[1], w[0], w[1], o_f, o_b, ret, gates, h[0], h[1], mods, gpost, gffn, wa, ww, wr, wo)


def _router_kernel(v_ref, rw_ref, rb_ref, eidx_ref, rank_ref, w_ref, cnt_ref, carry_ref):
    @pl.when(pl.program_id(0) == 0)
    def _():
        carry_ref[...] = jnp.zeros_like(carry_ref)

    tm = v_ref.shape[0]
    scores = _sigmoid(_dot_nt(rw_ref[...], v_ref[...]))
    sel = scores + rb_ref[...]
    neg = -jnp.inf
    n_grp, per = N_EXPERT_GROUPS, EXPERTS_PER_GROUP

    sel3 = sel.reshape(n_grp, per, tm)
    member_id = lax.broadcasted_iota(jnp.int32, sel3.shape, 1)
    m1 = jnp.max(sel3, axis=1, keepdims=True)
    i1 = jnp.min(jnp.where(sel3 == m1, member_id, per), axis=1, keepdims=True)
    m2 = jnp.max(jnp.where(member_id == i1, neg, sel3), axis=1, keepdims=True)
    gscore = (m1 + m2).reshape(n_grp, tm)
    gid = lax.broadcasted_iota(jnp.int32, gscore.shape, 0)
    ahead = jnp.zeros(gscore.shape, jnp.int32)
    for gj in range(n_grp):
        other = gscore[gj:gj + 1, :]
        ahead = ahead + jnp.where((other > gscore) | ((other == gscore) & (gid > gj)), 1, 0)
    group_ok = (ahead < TOPK_GROUPS).reshape(n_grp, 1, tm)
    sel = jnp.where(group_ok, sel3, NEG_INF).reshape(N_EXPERTS, tm)

    eid = lax.broadcasted_iota(jnp.int32, sel.shape, 0)
    chosen = jnp.zeros(sel.shape, jnp.bool_)
    picks = []
    for _ in range(TOP_K):
        m = jnp.max(sel, axis=0, keepdims=True)
        idx = jnp.min(jnp.where(sel == m, eid, N_EXPERTS), axis=0, keepdims=True)
        hit = eid == idx
        chosen = chosen | hit
        sel = jnp.where(hit, neg, sel)
        picks.append(idx)
    w = jnp.where(chosen, scores, 0.0)
    gate = ROUTED_SCALE * w / jnp.sum(w, axis=0, keepdims=True)

    member = jnp.where(chosen, 1.0, 0.0)
    earlier = lax.broadcasted_iota(jnp.int32, (tm, tm), 0) < lax.broadcasted_iota(jnp.int32, (tm, tm), 1)
    pos = _dot(member.astype(BF16), jnp.where(earlier, 1.0, 0.0).astype(BF16)) + carry_ref[...]
    for k, idx in enumerate(picks):
        hit = eid == idx
        eidx_ref[k:k + 1, :] = idx
        rank_ref[k:k + 1, :] = jnp.sum(jnp.where(hit, pos, 0.0), axis=0, keepdims=True)
        w_ref[k:k + 1, :] = jnp.sum(jnp.where(hit, gate, 0.0), axis=0, keepdims=True)
    carry_ref[...] += jnp.sum(member, axis=1, keepdims=True)
    cnt_ref[...] = carry_ref[...]


def _router(v, rw_t, rb):
    n_rows, d = v.shape
    cols = lambda i: (0, i)
    const = lambda i: (0, 0)
    outs = [jax.ShapeDtypeStruct((TOP_K, n_rows), jnp.int32), jax.ShapeDtypeStruct((TOP_K, n_rows), F32),
            jax.ShapeDtypeStruct((TOP_K, n_rows), F32), jax.ShapeDtypeStruct((N_EXPERTS, 1), F32)]
    return pl.pallas_call(
        _router_kernel,
        grid=(n_rows // TM,),
        in_specs=[pl.BlockSpec((TM, d), lambda i: (i, 0)), pl.BlockSpec(rw_t.shape, const),
                  pl.BlockSpec(rb.shape, const)],
        out_specs=[pl.BlockSpec((TOP_K, TM), cols), pl.BlockSpec((TOP_K, TM), cols),
                   pl.BlockSpec((TOP_K, TM), cols), pl.BlockSpec((N_EXPERTS, 1), const)],
        out_shape=outs,
        scratch_shapes=[pltpu.VMEM((N_EXPERTS, 1), F32)],
        compiler_params=_cparams(("arbitrary",)),
        name="router",
    )(v, rw_t, rb)


def _slots_kernel(eidx_ref, rank_ref, cnt_ref, slot_ref):
    tm = eidx_ref.shape[1]
    eid = lax.broadcasted_iota(jnp.int32, (N_EXPERTS, tm), 0)
    for k in range(TOP_K):
        before = jnp.sum(jnp.where(eid < eidx_ref[k:k + 1, :], cnt_ref[...], 0.0), axis=0, keepdims=True)
        slot_ref[k:k + 1, :] = (before + rank_ref[k:k + 1, :]).astype(jnp.int32)


def _slots(eidx, rank, cnt):
    n_rows = eidx.shape[1]
    tm = next(c for c in (2048, 1024, 512, 256) if n_rows % c == 0)
    cols = lambda i: (0, i)
    return pl.pallas_call(
        _slots_kernel,
        grid=(n_rows // tm,),
        in_specs=[pl.BlockSpec((TOP_K, tm), cols), pl.BlockSpec((TOP_K, tm), cols),
                  pl.BlockSpec(cnt.shape, lambda i: (0, 0))],
        out_specs=pl.BlockSpec((TOP_K, tm), cols),
        out_shape=jax.ShapeDtypeStruct((TOP_K, n_rows), jnp.int32),
        compiler_params=_cparams(("arbitrary",)),
        name="slots",
    )(eidx, rank, cnt)


SC_WINDOW = 128


def _sc_mesh():
    return plsc.VectorSubcoreMesh(core_axis_name="core", subcore_axis_name="subcore")


def _sc_dispatch(rows, slot_t, n_out):
    width = rows.shape[1]
    n_chunks = slot_t.shape[1] // SC_WINDOW
    info = plsc.get_sparse_core_info()
    n_workers = info.num_cores * info.num_subcores

    @functools.partial(
        pl.kernel, mesh=_sc_mesh(),
        out_type=jax.ShapeDtypeStruct((n_out, width), rows.dtype),
        scratch_types=[pltpu.VMEM((TOP_K, SC_WINDOW), jnp.int32), pltpu.VMEM((SC_WINDOW, width), rows.dtype)],
        name="moe_dispatch")
    def run(rows_hbm, idx_hbm, out_hbm, idx_v, rows_v):
        wid = lax.axis_index("subcore") * info.num_cores + lax.axis_index("core")

        @pl.loop(wid, n_chunks, step=n_workers)
        def _(c):
            r0 = pl.multiple_of(c * SC_WINDOW, SC_WINDOW)
            pltpu.sync_copy(idx_hbm.at[:, pl.ds(r0, SC_WINDOW)], idx_v)
            pltpu.sync_copy(rows_hbm.at[pl.ds(r0, SC_WINDOW)], rows_v)
            for k in range(TOP_K):
                pltpu.sync_copy(rows_v, out_hbm.at[idx_v.at[k]])

    return run(rows, slot_t)


def _sc_collect(rows, slot_t):
    n_picks, n_rows = slot_t.shape
    width = rows.shape[1]
    n_chunks = n_rows // SC_WINDOW
    info = plsc.get_sparse_core_info()
    n_workers = info.num_cores * info.num_subcores

    @functools.partial(
        pl.kernel, mesh=_sc_mesh(),
        out_type=jax.ShapeDtypeStruct((n_picks, n_rows, width), rows.dtype),
        scratch_types=[pltpu.VMEM((TOP_K, SC_WINDOW), jnp.int32), pltpu.VMEM((SC_WINDOW, width), rows.dtype)],
        name="moe_collect")
    def run(rows_hbm, idx_hbm, out_hbm, idx_v, rows_v):
        wid = lax.axis_index("subcore") * info.num_cores + lax.axis_index("core")

        @pl.loop(wid, n_chunks, step=n_workers)
        def _(c):
            r0 = pl.multiple_of(c * SC_WINDOW, SC_WINDOW)
            pltpu.sync_copy(idx_hbm.at[:, pl.ds(r0, SC_WINDOW)], idx_v)
            for k in range(TOP_K):
                pltpu.sync_copy(rows_hbm.at[idx_v.at[k]], rows_v)
                pltpu.sync_copy(rows_v, out_hbm.at[k, pl.ds(r0, SC_WINDOW)])

    return run(rows, slot_t)


EXPERT_TILE = 256


def _work_items(cnt, n_slots):
    counts = cnt[:, 0].astype(jnp.int32)
    ends = jnp.cumsum(counts)
    n_tiles = n_slots // EXPERT_TILE
    bounds = jnp.sort(jnp.concatenate([jnp.arange(n_tiles, dtype=jnp.int32) * EXPERT_TILE, ends - counts]))
    nxt = jnp.concatenate([bounds[1:], jnp.array([n_slots], jnp.int32)])
    tile = jnp.minimum(bounds // EXPERT_TILE, n_tiles - 1)
    expert = jnp.sum((ends[None, :] <= bounds[:, None]).astype(jnp.int32), axis=1)
    expert = jnp.minimum(expert, N_EXPERTS - 1)
    return tile, expert, bounds - tile * EXPERT_TILE, nxt - tile * EXPERT_TILE


XS_RING = 3


def _experts_kernel(tile_ref, exp_ref, lo_ref, hi_ref, first_ref, next_ref, wslot_ref, xs_hbm, wg_hbm, wu_hbm, wd_hbm,
                    ys_ref, acc_ref, wgb_ref, wub_ref, wdb_ref, xbuf_ref, xsem, wg_st, wu_st, wd_st, wsem, *, layer):
    i = pl.program_id(0)
    n_items = pl.num_programs(0)
    lo, hi = lo_ref[i], hi_ref[i]

    def weight_copies(expert, slot):
        return [pltpu.make_async_copy(w_hbm.at[layer, expert], stage.at[slot], wsem.at[slot, k])
                for k, (w_hbm, stage) in enumerate(((wg_hbm, wg_st), (wu_hbm, wu_st), (wd_hbm, wd_st)))]

    @pl.when(i == 0)
    def _():
        for c in weight_copies(exp_ref[0], 0):
            c.start()

    def tile_copy(item):
        slot = lax.rem(item, XS_RING)
        row0 = pl.multiple_of(tile_ref[item] * EXPERT_TILE, EXPERT_TILE)
        return pltpu.make_async_copy(xs_hbm.at[pl.ds(row0, EXPERT_TILE)], xbuf_ref.at[slot], xsem.at[slot])

    @pl.when(i == 0)
    def _():
        for ahead in range(XS_RING - 1):
            pl.when(ahead < n_items)(lambda: tile_copy(ahead).start())

    @pl.when(i + XS_RING - 1 < n_items)
    def _():
        tile_copy(i + XS_RING - 1).start()

    tile_copy(i).wait()
    xs_ref = xbuf_ref.at[lax.rem(i, XS_RING)]

    @pl.when(first_ref[i] == 1)
    def _():
        slot = wslot_ref[i]
        for c in weight_copies(exp_ref[i], slot):
            c.wait()

        @pl.when(next_ref[i] >= 0)
        def _():
            for c in weight_copies(next_ref[i], 1 - slot):
                c.start()

        wgb_ref[...] = wg_st[slot].astype(BF16)
        wub_ref[...] = wu_st[slot].astype(BF16)
        wdb_ref[...] = wd_st[slot].astype(BF16)

    def ffn():
        x_lo, x_hi = _unpack_halves(xs_ref[...])
        x_lo, x_hi = x_lo.astype(BF16), x_hi.astype(BF16)
        n = x_lo.shape[1]
        a = _dot(x_lo, wgb_ref[0:n, :]) + _dot(x_hi, wgb_ref[n:, :])
        u = _dot(x_lo, wub_ref[0:n, :]) + _dot(x_hi, wub_ref[n:, :])
        return _dot(((a * _sigmoid(a)) * u).astype(BF16), wdb_ref[...])

    whole = (lo == 0) & (hi == EXPERT_TILE)

    @pl.when(whole)
    def _():
        ys_ref[...] = _pack_halves(ffn())

    @pl.when(jnp.logical_not(whole) & (hi > lo))
    def _():
        y = ffn()
        row = lax.broadcasted_iota(jnp.int32, y.shape, 0)
        y = jnp.where((row >= lo) & (row < hi), y, 0.0)

        @pl.when(lo == 0)
        def _():
            acc_ref[...] = y

        @pl.when((lo > 0) & (hi < EXPERT_TILE))
        def _():
            acc_ref[...] += y

        @pl.when((lo > 0) & (hi == EXPERT_TILE))
        def _():
            ys_ref[...] = _pack_halves(acc_ref[...] + y)


def _experts(xs, items, layer, exp_wg, exp_wu, exp_wd):
    n_slots, half = xs.shape
    d, hid = exp_wg.shape[-2:]
    tile, expert, lo, hi = items
    n = tile.shape[0]
    pos = jnp.arange(n)
    first = jnp.concatenate([jnp.ones((1,), jnp.int32), (expert[1:] != expert[:-1]).astype(jnp.int32)])
    later_other = (expert[None, :] != expert[:, None]) & (pos[None, :] > pos[:, None])
    nxt = jnp.where(later_other.any(axis=1), expert[jnp.argmax(later_other, axis=1)], -1).astype(jnp.int32)
    wslot = ((jnp.cumsum(first) - 1) % 2).astype(jnp.int32)
    any_space = pl.BlockSpec(memory_space=pl.ANY)
    grid_spec = pltpu.PrefetchScalarGridSpec(
        num_scalar_prefetch=7,
        grid=(n,),
        in_specs=[any_space, any_space, any_space, any_space],
        out_specs=pl.BlockSpec((EXPERT_TILE, half), lambda i, t, *_: (t[i], 0)),
        scratch_shapes=[pltpu.VMEM((EXPERT_TILE, d), F32), pltpu.VMEM((d, hid), BF16),
                        pltpu.VMEM((d, hid), BF16), pltpu.VMEM((hid, d), BF16),
                        pltpu.VMEM((XS_RING, EXPERT_TILE, half), jnp.uint32),
                        pltpu.SemaphoreType.DMA((XS_RING,)),
                        pltpu.VMEM((2, d, hid), F32), pltpu.VMEM((2, d, hid), F32), pltpu.VMEM((2, hid, d), F32),
                        pltpu.SemaphoreType.DMA((2, 3))])
    return pl.pallas_call(
        functools.partial(_experts_kernel, layer=layer),
        grid_spec=grid_spec,
        out_shape=jax.ShapeDtypeStruct((n_slots, half), jnp.uint32),
        compiler_params=_cparams(("arbitrary",)),
        name="experts",
    )(tile, expert, lo, hi, first, nxt, wslot, xs, exp_wg, exp_wu, exp_wd)


def _moe_out_tile(yg_ref, w_ref, v_ref, sg_ref, su_ref, sd_ref, h1_ref, mod_ref, gpost_ref,
                  *, tiles_per_batch, n_batch, d):
    i = pl.program_id(0)
    x = v_ref[...]
    a = _dot(x, sg_ref[...])
    f = _dot(((a * _sigmoid(a)) * _dot(x, su_ref[...])).astype(BF16), sd_ref[...])
    n = d // 2
    f_lo, f_hi = f[:, :n], f[:, n:]
    w = w_ref[...]
    for k in range(TOP_K):
        y_lo, y_hi = _unpack_halves(yg_ref[k])
        wk = w[:, k:k + 1]
        f_lo = f_lo + wk * y_lo
        f_hi = f_hi + wk * y_hi
    f = jnp.concatenate([f_lo, f_hi], axis=1)
    bi = jnp.minimum(i // tiles_per_batch, n_batch)
    g2 = mod_ref[pl.ds(bi, 1), 5 * d:6 * d]
    return h1_ref[...] + g2 * _rms(f, gpost_ref[...])


N_MOE_OUT_IN = 9
N_INPROJ_IN = 9


def _moe_out_kernel(*refs, **kw):
    refs[N_MOE_OUT_IN][...] = _moe_out_tile(*refs[:N_MOE_OUT_IN], **kw)


def _moe_out_inproj_kernel(*refs, n_batch, d, tiles_per_batch):
    dims = dict(tiles_per_batch=tiles_per_batch, n_batch=n_batch, d=d)
    h = _moe_out_tile(*refs[:N_MOE_OUT_IN], **dims)
    n_in = N_MOE_OUT_IN + N_INPROJ_IN
    refs[n_in][...] = h
    _inproj_tile(h, *refs[N_MOE_OUT_IN:n_in], *refs[n_in + 1:], **dims)


def _moe_out_specs(yg, w, v, sg, su, sd, h1, mods, gpost, tm):
    d = v.shape[1]
    rows = lambda i: (i, 0)
    const = lambda i: (0, 0)
    once = dict(pipeline_mode=pl.Buffered(1))
    return [pl.BlockSpec((TOP_K, tm, d // 2), lambda i: (0, i, 0)), pl.BlockSpec((tm, TOP_K), rows),
            pl.BlockSpec((tm, d), rows),
            pl.BlockSpec(sg.shape, const, **once), pl.BlockSpec(su.shape, const, **once),
            pl.BlockSpec(sd.shape, const, **once),
            pl.BlockSpec((tm, d), rows), pl.BlockSpec(mods.shape, const), pl.BlockSpec((1, d), const)]


def _moe_out(moe_args, *, n_batch, seq):
    v = moe_args[2]
    n_rows, d = v.shape
    return pl.pallas_call(
        functools.partial(_moe_out_kernel, tiles_per_batch=seq // TM, n_batch=n_batch, d=d),
        grid=(n_rows // TM,),
        in_specs=_moe_out_specs(*moe_args, TM),
        out_specs=pl.BlockSpec((TM, d), lambda i: (i, 0)),
        out_shape=jax.ShapeDtypeStruct((n_rows, d), F32),
        compiler_params=_cparams(("arbitrary",)),
        name="moe_out",
    )(*moe_args)


FUSED_TM = 256


def _moe_out_inproj(moe_args, inproj_args, *, layer, n_batch, seq):
    v = moe_args[2]
    n_rows, d = v.shape
    tm = FUSED_TM
    in_specs, out_specs, out_shape = _inproj_specs(n_rows, d, *inproj_args, layer=layer, tm=tm, n_batch=n_batch,
                                                   seq=seq)
    rows = lambda i: (i, 0)
    return pl.pallas_call(
        functools.partial(_moe_out_inproj_kernel, tiles_per_batch=seq // tm, n_batch=n_batch, d=d),
        grid=(n_rows // tm,),
        in_specs=_moe_out_specs(*moe_args, tm) + in_specs,
        out_specs=[pl.BlockSpec((tm, d), rows)] + out_specs,
        out_shape=[jax.ShapeDtypeStruct((n_rows, d), F32)] + out_shape,
        compiler_params=_cparams(("arbitrary",)),
        name="moe_out_inproj",
    )(*moe_args, *inproj_args)


def _moe_routed(v, vp, layer, rw, rb, exp_wg, exp_wu, exp_wd):
    n_rows = v.shape[0]
    n_slots = n_rows * TOP_K
    assert n_slots % EXPERT_TILE == 0
    eidx, rank, w_t, cnt = _router(v, rw, rb)
    slot_t = _slots(eidx, rank, cnt)
    xs = _sc_dispatch(vp, slot_t, n_slots)
    ys = _experts(xs, _work_items(cnt, n_slots), layer, exp_wg, exp_wu, exp_wd)
    return _sc_collect(ys, slot_t), w_t.T


def _rope_tables(seq):
    rows = seq // GRID_W
    row_id = np.repeat(np.arange(rows, dtype=np.float64), GRID_W)
    col_id = np.tile(np.arange(GRID_W, dtype=np.float64), rows)

    def tables(rot_dim):
        axis_dim = rot_dim // 2
        inv_freq = ROPE_BASE ** (-np.arange(0, axis_dim, 2, dtype=np.float64) / axis_dim)
        ang_r = row_id[:, None] * inv_freq[None, :]
        ang_c = col_id[:, None] * inv_freq[None, :]
        cos = np.concatenate([np.cos(ang_r), np.cos(ang_r), np.cos(ang_c), np.cos(ang_c)], axis=1)
        sin = np.concatenate([-np.sin(ang_r), np.sin(ang_r), -np.sin(ang_c), np.sin(ang_c)], axis=1)
        return cos, sin

    cos64, sin64 = tables(GQA_DIM)
    cos32, sin32 = tables(MLA_ROPE)
    ones = np.ones((seq, MLA_NOPE))
    pad = LANES - MLA_NOPE - MLA_ROPE
    cospe = np.concatenate([ones, cos32, np.ones((seq, pad))], axis=1)
    sinpe = np.concatenate([0 * ones, sin32, np.zeros((seq, pad))], axis=1)
    tab = np.concatenate([cos64, cos64, sin64, sin64, cospe, sinpe], axis=1)
    ident = np.concatenate([np.ones((TM, LANES)), np.zeros((TM, LANES)),
                            np.ones((TM, LANES)), np.zeros((TM, LANES))], axis=1)
    return jnp.asarray(np.concatenate([tab, ident], axis=0), F32)


def _w_in_moves(d):
    sizes = (MLA_Q_LORA, MLA_KV_LORA, MLA_ROPE, GQA_HEADS * GQA_DIM, GQA_KV_HEADS * GQA_DIM,
             GQA_KV_HEADS * GQA_DIM, RET_HEADS * RET_QK, RET_HEADS * RET_QK, RET_HEADS * RET_V,
             RET_HEADS * RET_V, 3 * d)
    src = [sum(sizes[:i]) for i in range(len(sizes))]
    cq, ckv, kpe, gq, gk, gv, rq, rk, rv, rg, gates = src
    moves = [(cq, C_CQ, MLA_Q_LORA, 1.0), (ckv, C_CKV, MLA_KV_LORA, 1.0), (kpe, C_KPE + MLA_NOPE, MLA_ROPE, 1.0),
             (gq, C_G, GQA_HEADS * GQA_DIM, GQA_DIM ** -0.5 * LOG2_E)]
    dst = C_G + GQA_HEADS * GQA_DIM
    for base in (gk, gv):
        for hd in range(GQA_KV_HEADS):
            for _ in range(2):
                moves.append((base + hd * GQA_DIM, dst, GQA_DIM, 1.0))
                dst += GQA_DIM
    assert dst == C_R
    for s, width, scale in ((rq, RET_HEADS * RET_QK, 1.0), (rk, RET_HEADS * RET_QK, RET_QK ** -0.5),
                            (rv, RET_HEADS * RET_V, 1.0), (rg, RET_HEADS * RET_V, 1.0), (gates, 3 * d, 1.0)):
        moves.append((s, dst, width, scale))
        dst += width
    assert dst == W_COLS
    return moves


def _pack_kernel(w_ref, o_ref, *, moves):
    o_ref[:, C_KPE:C_G] = jnp.zeros((o_ref.shape[0], C_G - C_KPE), o_ref.dtype)
    for s, t, width, scale in moves:
        piece = w_ref[:, s:s + width]
        o_ref[:, t:t + width] = (piece if scale == 1.0 else piece * scale).astype(o_ref.dtype)


def _pack_w_in(w_in):
    n_layers, d, n_cols = w_in.shape
    rows = 256
    return pl.pallas_call(
        functools.partial(_pack_kernel, moves=_w_in_moves(d)),
        grid=(n_layers, d // rows),
        in_specs=[pl.BlockSpec((None, rows, n_cols), lambda l, i: (l, i, 0))],
        out_specs=pl.BlockSpec((None, rows, W_COLS), lambda l, i: (l, i, 0)),
        out_shape=jax.ShapeDtypeStruct((n_layers, d, W_COLS), BF16),
        compiler_params=_cparams(("arbitrary", "arbitrary")),
        name="pack_w_in",
    )(w_in)


def _pack_mla_up(w_uq, w_ukv):
    r = w_uq.shape[0]
    dq = MLA_NOPE + MLA_ROPE
    wq = jnp.pad(w_uq.reshape(r, MLA_HEADS, dq), ((0, 0), (0, 0), (0, LANES - dq))).reshape(r, MLA_HEADS * LANES)
    kv = w_ukv.reshape(r, MLA_HEADS, MLA_NOPE + MLA_V)
    wk = jnp.pad(kv[:, :, :MLA_NOPE], ((0, 0), (0, 0), (0, LANES - MLA_NOPE))).reshape(r, MLA_HEADS * LANES)
    wv = kv[:, :, MLA_NOPE:]
    zeros = jnp.zeros_like(wv)
    even = jnp.concatenate([wv, zeros], axis=2)
    odd = jnp.concatenate([zeros, wv], axis=2)
    wv = jnp.where((jnp.arange(MLA_HEADS) % 2 == 0)[None, :, None], even, odd).reshape(r, MLA_HEADS * LANES)
    return wq.astype(BF16), wk.astype(BF16), wv.astype(BF16)


def kernel(x, c, ctx, c_ctx, ada_w, ada_b, norm_mix_pre, norm_mix_post, norm_ffn_pre, norm_ffn_post, w_in, mla_q_norm, mla_w_uq, mla_kv_norm, mla_w_ukv, gqa_sink, ret_decay_fwd, ret_decay_bwd, w_br_mla, w_br_gqa, w_br_ret, w_out, router_w, router_bias, exp_w_gate, exp_w_up, exp_w_down, shared_w_gate, shared_w_up, shared_w_down):
    n_batch, seq, d = x.shape
    n_ctx = ctx.shape[1]
    depth = ada_w.shape[0]
    n_lat_rows = n_batch * seq
    assert seq % TM == 0 and (n_batch * n_ctx) % TM == 0 and seq % ATT_TQ == 0 and seq % n_ctx == 0
    assert n_batch < MOD_ROWS and seq % GRID_W == 0 and d == D_MODEL

    cond = jnp.zeros((MOD_ROWS, d), F32).at[:n_batch].set(c).at[n_batch].set(c_ctx)
    mods_all = _adaln(cond, ada_w, ada_b)
    rope = _rope_tables(seq)
    h = (x.reshape(n_lat_rows, d), ctx.reshape(n_batch * n_ctx, d), 0)
    n_all_rows = n_lat_rows + n_batch * n_ctx
    row = lambda p: p.reshape(1, -1)
    dims = dict(n_batch=n_batch, seq=seq)

    w_in_packed = _pack_w_in(w_in)

    def inproj_args(l):
        return (mods_all[l], row(norm_mix_pre[l]), w_in_packed, rope, row(mla_q_norm[l]),
                row(mla_kv_norm[l]), *_pack_mla_up(mla_w_uq[l], mla_w_ukv[l]))

    projected = _inproj(h[0], h[1], inproj_args(0), layer=0, **dims)
    for l in range(depth):
        last = l == depth - 1
        mods = mods_all[l]
        mq, mk, mv, gqa, ret, gates = projected
        a = _mla_attention(mq, mk, mv, ctx=n_ctx, with_ctx_queries=not last, **dims)
        sink_tab = jnp.broadcast_to(gqa_sink[l].astype(F32)[:, None] * LOG2_E, (GQA_HEADS, LANES))
        lg = jnp.concatenate([jax.nn.log_sigmoid(ret_decay_fwd[l].astype(F32)),
                              jax.nn.log_sigmoid(ret_decay_bwd[l].astype(F32))])
        o_f, o_b, w_lat = _retention_and_window(ret, jnp.broadcast_to(lg[:, None], (2 * RET_HEADS, LANES)),
                                                gqa, sink_tab, ctx=n_ctx, **dims)
        w = (w_lat, w_lat if last else _win_call(gqa, sink_tab, ctx=n_ctx, latent_queries=False, **dims))
        n_rows = n_lat_rows if last else n_all_rows
        h1, v, vp = _merge(a, w, o_f, o_b, ret, gates, h, mods, row(norm_mix_post[l]), row(norm_ffn_pre[l]),
                           w_br_mla[l].astype(BF16), w_br_gqa[l].astype(BF16), w_br_ret[l].astype(BF16),
                           w_out[l].astype(BF16), n_rows=n_rows, **dims)
        yg, gate_w = _moe_routed(v, vp, l, router_w[l].T.astype(BF16), router_bias[l].astype(F32).reshape(-1, 1),
                                 exp_w_gate, exp_w_up, exp_w_down)
        moe_args = (yg, gate_w, v, shared_w_gate[l].astype(BF16), shared_w_up[l].astype(BF16),
                    shared_w_down[l].astype(BF16), h1, mods, row(norm_ffn_post[l]))
        if last:
            out = _moe_out(moe_args, **dims)
        else:
            stream, *projected = _moe_out_inproj(moe_args, inproj_args(l + 1), layer=l + 1, **dims)
            h = (stream, stream, n_lat_rows // TM)
    return out[:n_lat_rows].reshape(n_batch, seq, d)
```
